```python
import math
import jax, jax.numpy as jnp
from jax import lax
import numpy as np

D_MODEL = 1024
BATCH = 8
SEQ = 16384
DEPTH = 4

N_MIXERS = 3
EPS = 1e-6
N_A = (DEPTH + 2) // 3
N_B = (DEPTH + 1) // 3
N_C = DEPTH // 3

HG_HEADS = 8
HG_DK = D_MODEL // HG_HEADS
HG_DV = D_MODEL // HG_HEADS
HG_CHUNK = 64
HG_IN = 4 * D_MODEL

SW_HEADS = 16
SW_KV_HEADS = 4
SW_GROUP = SW_HEADS // SW_KV_HEADS
SW_DH = 64
SW_WINDOW = 128
SW_BLOCK = 128
SW_QW = SW_HEADS * SW_DH
SW_KVW = SW_KV_HEADS * SW_DH
SW_IN = 2 * SW_QW + 2 * SW_KVW
ROPE_THETA = 10000.0
POS_OFFSET_MAX = 4096

GD_QK_HEADS = 8
GD_V_HEADS = 16
GD_DK = 128
GD_DV = 128
GD_CONV = 4
GD_CHUNK = 64
GD_QKW = GD_QK_HEADS * GD_DK
GD_VW = GD_V_HEADS * GD_DV
GD_QKV = 2 * GD_QKW + GD_VW
GD_IN = GD_QKV + GD_VW + 2 * GD_V_HEADS

kernel_name = 'hybrid_hgrn2_swa_sink_gdn_interleaved'


def rmsnorm(x, g):
    xf = x.astype(jnp.float32)
    y = xf * lax.rsqrt(jnp.mean(xf * xf, axis=-1, keepdims=True) + EPS)
    return (y * g.astype(jnp.float32)).astype(x.dtype)


def l2norm(x):
    xf = x.astype(jnp.float32)
    return xf * lax.rsqrt(jnp.sum(xf * xf, axis=-1, keepdims=True) + EPS)


def rope(x, ang):
    ang = ang.reshape(ang.shape[:2] + (1,) * (x.ndim - 3) + ang.shape[-1:])
    cos, sin = jnp.cos(ang), jnp.sin(ang)
    x1, x2 = jnp.split(x.astype(jnp.float32), 2, axis=-1)
    return jnp.concatenate([x1 * cos - x2 * sin, x2 * cos + x1 * sin], axis=-1).astype(x.dtype)


def to_chunks(a, chunk):
    B, T = a.shape[:2]
    a = a.reshape((B, T // chunk, chunk) + a.shape[2:])
    return jnp.moveaxis(jnp.moveaxis(a, 1, 0), 3, 2)


def from_chunks(a):
    a = jnp.moveaxis(jnp.moveaxis(a, 2, 3), 0, 1)
    return a.reshape((a.shape[0], a.shape[1] * a.shape[2]) + a.shape[3:])


def causal_conv(x, w):
    K, C = w.shape
    return lax.conv_general_dilated(x, w[:, None, :].astype(x.dtype), window_strides=(1,),
                                    padding=[(K - 1, 0)], dimension_numbers=('NWC', 'WIO', 'NWC'),
                                    feature_group_count=C)


def hgrn2_scan(q, k, v, log_f):
    B, T, H, dk = q.shape
    dv = v.shape[-1]
    causal = jnp.tril(jnp.ones((HG_CHUNK, HG_CHUNK), bool))

    def step(S, inp):
        q_, k_, v_, lf = inp
        b = jnp.cumsum(lf, axis=-2)
        diff = b[..., :, None, :] - b[..., None, :, :]
        dec = jnp.exp(jnp.where(causal[:, :, None], diff, -jnp.inf))
        A = jnp.einsum('bhtd,bhsd,bhtsd->bhts', q_, k_, dec)
        o = jnp.einsum('bhts,bhse->bhte', A, v_) + jnp.einsum('bhtd,bhde->bhte', q_ * jnp.exp(b), S)
        b_last = b[..., -1:, :]
        S = jnp.exp(b_last)[..., 0, :, None] * S + jnp.einsum('bhsd,bhse->bhde', k_ * jnp.exp(b_last - b), v_)
        return S, o

    xs = tuple(to_chunks(a.astype(jnp.float32), HG_CHUNK) for a in (q, k, v, log_f))
    S0 = jnp.zeros((B, H, dk, dv), jnp.float32)
    _, o = lax.scan(step, S0, xs)
    return from_chunks(o)


def hgrn2_mixer(h, w_in, w_out, onorm_g, lb):
    B, T, _ = h.shape
    q, f_pre, i_, z = jnp.split(h @ w_in, [D_MODEL, 2 * D_MODEL, 3 * D_MODEL], axis=-1)
    f_pre = f_pre.astype(jnp.float32)
    lb = lb.astype(jnp.float32)
    log_f = jnp.log(lb + (1.0 - lb) * jax.nn.sigmoid(f_pre))
    k = (1.0 - lb) * jax.nn.sigmoid(-f_pre)
    q = jax.nn.silu(q)
    shp = (B, T, HG_HEADS, HG_DK)
    o = hgrn2_scan(q.reshape(shp), k.reshape(shp), i_.reshape(B, T, HG_HEADS, HG_DV), log_f.reshape(shp))
    o = rmsnorm(o, onorm_g).reshape(B, T, HG_HEADS * HG_DV).astype(h.dtype)
    return (o * jax.nn.silu(z)) @ w_out


def swa_mixer(h, w_in, w_out, qn_g, kn_g, sinks, ang):
    B, T, _ = h.shape
    nb = T // SW_BLOCK
    q, k, v, z = jnp.split(h @ w_in, [SW_QW, SW_QW + SW_KVW, SW_QW + 2 * SW_KVW], axis=-1)
    q = rope(rmsnorm(q.reshape(B, T, SW_KV_HEADS, SW_GROUP, SW_DH), qn_g), ang)
    k = rope(rmsnorm(k.reshape(B, T, SW_KV_HEADS, SW_DH), kn_g), ang)
    v = v.reshape(B, T, SW_KV_HEADS, SW_DH)

    def band(a):
        cur = a.reshape(B, nb, SW_BLOCK, SW_KV_HEADS, SW_DH)
        prev = jnp.concatenate([jnp.zeros_like(cur[:, :1]), cur[:, :-1]], axis=1)
        return jnp.concatenate([prev, cur], axis=2)

    qb = q.reshape(B, nb, SW_BLOCK, SW_KV_HEADS, SW_GROUP, SW_DH)
    kb, vb = band(k), band(v)
    s = jnp.einsum('bnqhgd,bnkhd->bnhgqk', qb, kb).astype(jnp.float32) * (SW_DH ** -0.5)
    qi = jnp.arange(SW_BLOCK)[:, None]
    kj = jnp.arange(2 * SW_BLOCK)[None, :]
    rel = qi + SW_BLOCK - kj
    in_band = (rel >= 0) & (rel < SW_WINDOW)
    key_pos = jnp.arange(nb)[:, None] * SW_BLOCK + jnp.arange(2 * SW_BLOCK)[None, :] - SW_BLOCK
    mask = in_band[None] & (key_pos >= 0)[:, None, :]
    s = jnp.where(mask[None, :, None, None], s, -jnp.inf)
    sink = sinks.astype(jnp.float32).reshape(SW_KV_HEADS, SW_GROUP)[None, None, :, :, None, None]
    m = jnp.maximum(jnp.max(s, axis=-1, keepdims=True), sink)
    p = jnp.exp(s - m)
    p = p / (jnp.sum(p, axis=-1, keepdims=True) + jnp.exp(sink - m))
    o = jnp.einsum('bnhgqk,bnkhd->bnqhgd', p.astype(h.dtype), vb).reshape(B, T, SW_QW)
    return (o * jax.nn.silu(z)) @ w_out


def gated_delta_scan(q, k, v, beta, g):
    B, T, H, dk = q.shape
    dv = v.shape[-1]
    incl = jnp.tril(jnp.ones((GD_CHUNK, GD_CHUNK), bool))
    strict = jnp.tril(jnp.ones((GD_CHUNK, GD_CHUNK), jnp.float32), -1)
    eye = jnp.eye(GD_CHUNK, dtype=jnp.float32)

    def step(S, inp):
        q_, k_, v_, beta_, g_ = inp
        d = jnp.cumsum(g_, axis=-1)
        dec = jnp.exp(jnp.where(incl, d[..., :, None] - d[..., None, :], -jnp.inf))
        kb = k_ * beta_[..., None]
        A = jnp.einsum('bhid,bhjd->bhij', kb, k_) * dec * strict
        rhs = jnp.concatenate([v_ * beta_[..., None], kb * jnp.exp(d)[..., None]], axis=-1)
        X = lax.linalg.triangular_solve(A + eye, rhs, left_side=True, lower=True, unit_diagonal=True)
        u, w = X[..., :dv], X[..., dv:]
        v_new = u - jnp.einsum('bhid,bhde->bhie', w, S)
        qk = jnp.einsum('bhid,bhjd->bhij', q_, k_) * dec
        o = jnp.einsum('bhid,bhde->bhie', q_ * jnp.exp(d)[..., None], S) + jnp.einsum('bhij,bhje->bhie', qk, v_new)
        d_last = d[..., -1:]
        S = S * jnp.exp(d_last)[..., None] + jnp.einsum('bhid,bhie->bhde', k_ * jnp.exp(d_last - d)[..., None], v_new)
        return S, o

    xs = tuple(to_chunks(a.astype(jnp.float32), GD_CHUNK) for a in (q, k, v, beta, g))
    S0 = jnp.zeros((B, H, dk, dv), jnp.float32)
    _, o = lax.scan(step, S0, xs)
    return from_chunks(o)


def gdn_mixer(h, w_in, w_out, conv_w, a_log, dt_bias, onorm_g):
    B, T, _ = h.shape
    qkv, z, a, b = jnp.split(h @ w_in, [GD_QKV, GD_QKV + GD_VW, GD_QKV + GD_VW + GD_V_HEADS], axis=-1)
    qkv = jax.nn.silu(causal_conv(qkv, conv_w))
    q, k, v = jnp.split(qkv, [GD_QKW, 2 * GD_QKW], axis=-1)
    rep = GD_V_HEADS // GD_QK_HEADS
    q = jnp.repeat(l2norm(q.reshape(B, T, GD_QK_HEADS, GD_DK)) * (GD_DK ** -0.5), rep, axis=2)
    k = jnp.repeat(l2norm(k.reshape(B, T, GD_QK_HEADS, GD_DK)), rep, axis=2)
    v = v.reshape(B, T, GD_V_HEADS, GD_DV)
    beta = jax.nn.sigmoid(b.astype(jnp.float32))
    g = -jnp.exp(a_log.astype(jnp.float32)) * jax.nn.softplus(a.astype(jnp.float32) + dt_bias.astype(jnp.float32))
    o = gated_delta_scan(q, k, v, beta, g)
    o = rmsnorm(o, onorm_g).reshape(B, T, GD_VW).astype(h.dtype)
    return (o * jax.nn.silu(z)) @ w_out


def _fwd_setup_inputs(seed: int = 0) -> dict:
    key = jax.random.key(seed)
    ks = jax.random.split(key, 24)
    nrm = lambda k, shape, s: jax.random.normal(k, shape, jnp.float32) * s
    x = nrm(ks[0], (BATCH, SEQ, D_MODEL), 1.0)
    c = nrm(ks[1], (BATCH, D_MODEL), 1.0)
    positions = (jnp.arange(SEQ, dtype=jnp.int32)[None, :]
                 + jax.random.randint(ks[2], (BATCH, 1), 0, POS_OFFSET_MAX, dtype=jnp.int32))
    hgrn_lb = nrm(ks[3], (DEPTH, D_MODEL), 0.1)
    ada_w = nrm(ks[4], (DEPTH, D_MODEL, 3 * D_MODEL), 0.5 * D_MODEL ** -0.5)
    ada_b = nrm(ks[5], (DEPTH, 3 * D_MODEL), 0.01)
    norm_g = 1.0 + nrm(ks[6], (DEPTH, D_MODEL), 0.02)
    hg_in_w = nrm(ks[7], (N_A, D_MODEL, HG_IN), D_MODEL ** -0.5)
    hg_out_w = nrm(ks[8], (N_A, HG_HEADS * HG_DV, D_MODEL), (HG_HEADS * HG_DV) ** -0.5)
    hg_onorm = 1.0 + nrm(ks[9], (N_A, HG_DV), 0.02)
    sw_in_w = nrm(ks[10], (N_B, D_MODEL, SW_IN), D_MODEL ** -0.5)
    sw_out_w = nrm(ks[11], (N_B, SW_QW, D_MODEL), SW_QW ** -0.5)
    sw_qnorm = 1.0 + nrm(ks[12], (N_B, SW_DH), 0.02)
    sw_knorm = 1.0 + nrm(ks[13], (N_B, SW_DH), 0.02)
    sw_sinks = nrm(ks[14], (N_B, SW_HEADS), 0.5)
    gd_in_w = nrm(ks[15], (N_C, D_MODEL, GD_IN), D_MODEL ** -0.5)
    gd_out_w = nrm(ks[16], (N_C, GD_VW, D_MODEL), GD_VW ** -0.5)
    gd_conv_w = nrm(ks[17], (N_C, GD_CONV, GD_QKV), GD_CONV ** -0.5)
    gd_a_log = jnp.log(jax.random.uniform(ks[18], (N_C, GD_V_HEADS), jnp.float32, 1.0, 16.0))
    dt = jnp.exp(jax.random.uniform(ks[19], (N_C, GD_V_HEADS), jnp.float32, math.log(1e-3), math.log(1e-1)))
    gd_dt_bias = dt + jnp.log(-jnp.expm1(-dt))
    gd_onorm = 1.0 + nrm(ks[20], (N_C, GD_DV), 0.02)
    return {'x': x, 'c': c, 'positions': positions, 'hgrn_lb': hgrn_lb,
            'ada_w': ada_w, 'ada_b': ada_b, 'norm_g': norm_g,
            'hg_in_w': hg_in_w, 'hg_out_w': hg_out_w, 'hg_onorm': hg_onorm,
            'sw_in_w': sw_in_w, 'sw_out_w': sw_out_w, 'sw_qnorm': sw_qnorm, 'sw_knorm': sw_knorm,
            'sw_sinks': sw_sinks,
            'gd_in_w': gd_in_w, 'gd_out_w': gd_out_w, 'gd_conv_w': gd_conv_w,
            'gd_a_log': gd_a_log, 'gd_dt_bias': gd_dt_bias, 'gd_onorm': gd_onorm}


def _fwd_reference(x, c, positions, hgrn_lb, ada_w, ada_b, norm_g,
              hg_in_w, hg_out_w, hg_onorm,
              sw_in_w, sw_out_w, sw_qnorm, sw_knorm, sw_sinks,
              gd_in_w, gd_out_w, gd_conv_w, gd_a_log, gd_dt_bias, gd_onorm):
    lb_all = jnp.cumsum(jax.nn.softmax(hgrn_lb.astype(jnp.float32), axis=0), axis=0)
    lb_all = lb_all - lb_all[0:1]
    inv_freq = ROPE_THETA ** (-jnp.arange(0, SW_DH, 2, dtype=jnp.float32) / SW_DH)
    ang = positions.astype(jnp.float32)[..., None] * inv_freq
    for i in range(DEPTH):
        j = i // N_MIXERS
        mod = (c @ ada_w[i] + ada_b[i])[:, None, :]
        shift, scale, gate = jnp.split(mod, 3, axis=-1)
        h = rmsnorm(x, norm_g[i]) * (1.0 + scale) + shift
        kind = i % N_MIXERS
        if kind == 0:
            y = hgrn2_mixer(h, hg_in_w[j], hg_out_w[j], hg_onorm[j], lb_all[i])
        elif kind == 1:
            y = swa_mixer(h, sw_in_w[j], sw_out_w[j], sw_qnorm[j], sw_knorm[j], sw_sinks[j], ang)
        else:
            y = gdn_mixer(h, gd_in_w[j], gd_out_w[j], gd_conv_w[j], gd_a_log[j], gd_dt_bias[j], gd_onorm[j])
        x = x + gate * y
    return x


import jax as _jax
import jax.numpy as _jnp

TWIN_FORMAT = 'train_step'
FWD_PARAMS = ['x', 'c', 'positions', 'hgrn_lb', 'ada_w', 'ada_b', 'norm_g', 'hg_in_w', 'hg_out_w', 'hg_onorm', 'sw_in_w', 'sw_out_w', 'sw_qnorm', 'sw_knorm', 'sw_sinks', 'gd_in_w', 'gd_out_w', 'gd_conv_w', 'gd_a_log', 'gd_dt_bias', 'gd_onorm']
TWIN_WEIGHTS = ['hgrn_lb', 'ada_w', 'ada_b', 'norm_g', 'hg_in_w', 'hg_out_w', 'hg_onorm', 'sw_in_w', 'sw_out_w', 'sw_qnorm', 'sw_knorm', 'sw_sinks', 'gd_in_w', 'gd_out_w', 'gd_conv_w', 'gd_a_log', 'gd_dt_bias', 'gd_onorm']
TWIN_DIFF_INPUT = 'x'
TWIN_INPUTS = ['x', 'c', 'positions', 'hgrn_lb', 'ada_w', 'ada_b', 'norm_g', 'hg_in_w', 'hg_out_w', 'hg_onorm', 'sw_in_w', 'sw_out_w', 'sw_qnorm', 'sw_knorm', 'sw_sinks', 'gd_in_w', 'gd_out_w', 'gd_conv_w', 'gd_a_log', 'gd_dt_bias', 'gd_onorm', 'loss_target', 'm_hgrn_lb', 'm_ada_w', 'm_ada_b', 'm_norm_g', 'm_hg_in_w', 'm_hg_out_w', 'm_hg_onorm', 'm_sw_in_w', 'm_sw_out_w', 'm_sw_qnorm', 'm_sw_knorm', 'm_sw_sinks', 'm_gd_in_w', 'm_gd_out_w', 'm_gd_conv_w', 'm_gd_a_log', 'm_gd_dt_bias', 'm_gd_onorm', 'v_hgrn_lb', 'v_ada_w', 'v_ada_b', 'v_norm_g', 'v_hg_in_w', 'v_hg_out_w', 'v_hg_onorm', 'v_sw_in_w', 'v_sw_out_w', 'v_sw_qnorm', 'v_sw_knorm', 'v_sw_sinks', 'v_gd_in_w', 'v_gd_out_w', 'v_gd_conv_w', 'v_gd_a_log', 'v_gd_dt_bias', 'v_gd_onorm']
TWIN_OUTPUTS = ['loss', 'grad_x', 'grad_hgrn_lb', 'grad_ada_w', 'grad_ada_b', 'grad_norm_g', 'grad_hg_in_w', 'grad_hg_out_w', 'grad_hg_onorm', 'grad_sw_in_w', 'grad_sw_out_w', 'grad_sw_qnorm', 'grad_sw_knorm', 'grad_sw_sinks', 'grad_gd_in_w', 'grad_gd_out_w', 'grad_gd_conv_w', 'grad_gd_a_log', 'grad_gd_dt_bias', 'grad_gd_onorm', 'delta_hgrn_lb', 'delta_ada_w', 'delta_ada_b', 'delta_norm_g', 'delta_hg_in_w', 'delta_hg_out_w', 'delta_hg_onorm', 'delta_sw_in_w', 'delta_sw_out_w', 'delta_sw_qnorm', 'delta_sw_knorm', 'delta_sw_sinks', 'delta_gd_in_w', 'delta_gd_out_w', 'delta_gd_conv_w', 'delta_gd_a_log', 'delta_gd_dt_bias', 'delta_gd_onorm', 'new_m_hgrn_lb', 'new_m_ada_w', 'new_m_ada_b', 'new_m_norm_g', 'new_m_hg_in_w', 'new_m_hg_out_w', 'new_m_hg_onorm', 'new_m_sw_in_w', 'new_m_sw_out_w', 'new_m_sw_qnorm', 'new_m_sw_knorm', 'new_m_sw_sinks', 'new_m_gd_in_w', 'new_m_gd_out_w', 'new_m_gd_conv_w', 'new_m_gd_a_log', 'new_m_gd_dt_bias', 'new_m_gd_onorm', 'new_v_hgrn_lb', 'new_v_ada_w', 'new_v_ada_b', 'new_v_norm_g', 'new_v_hg_in_w', 'new_v_hg_out_w', 'new_v_hg_onorm', 'new_v_sw_in_w', 'new_v_sw_out_w', 'new_v_sw_qnorm', 'new_v_sw_knorm', 'new_v_sw_sinks', 'new_v_gd_in_w', 'new_v_gd_out_w', 'new_v_gd_conv_w', 'new_v_gd_a_log', 'new_v_gd_dt_bias', 'new_v_gd_onorm']
TWIN_LEAF_KINDS = {'loss': 'loss', 'grad_x': 'grad_x', 'grad_hgrn_lb': 'grad_w', 'grad_ada_w': 'grad_w', 'grad_ada_b': 'grad_w', 'grad_norm_g': 'grad_w', 'grad_hg_in_w': 'grad_w', 'grad_hg_out_w': 'grad_w', 'grad_hg_onorm': 'grad_w', 'grad_sw_in_w': 'grad_w', 'grad_sw_out_w': 'grad_w', 'grad_sw_qnorm': 'grad_w', 'grad_sw_knorm': 'grad_w', 'grad_sw_sinks': 'grad_w', 'grad_gd_in_w': 'grad_w', 'grad_gd_out_w': 'grad_w', 'grad_gd_conv_w': 'grad_w', 'grad_gd_a_log': 'grad_w', 'grad_gd_dt_bias': 'grad_w', 'grad_gd_onorm': 'grad_w', 'delta_hgrn_lb': 'delta_w', 'delta_ada_w': 'delta_w', 'delta_ada_b': 'delta_w', 'delta_norm_g': 'delta_w', 'delta_hg_in_w': 'delta_w', 'delta_hg_out_w': 'delta_w', 'delta_hg_onorm': 'delta_w', 'delta_sw_in_w': 'delta_w', 'delta_sw_out_w': 'delta_w', 'delta_sw_qnorm': 'delta_w', 'delta_sw_knorm': 'delta_w', 'delta_sw_sinks': 'delta_w', 'delta_gd_in_w': 'delta_w', 'delta_gd_out_w': 'delta_w', 'delta_gd_conv_w': 'delta_w', 'delta_gd_a_log': 'delta_w', 'delta_gd_dt_bias': 'delta_w', 'delta_gd_onorm': 'delta_w', 'new_m_hgrn_lb': 'new_m', 'new_m_ada_w': 'new_m', 'new_m_ada_b': 'new_m', 'new_m_norm_g': 'new_m', 'new_m_hg_in_w': 'new_m', 'new_m_hg_out_w': 'new_m', 'new_m_hg_onorm': 'new_m', 'new_m_sw_in_w': 'new_m', 'new_m_sw_out_w': 'new_m', 'new_m_sw_qnorm': 'new_m', 'new_m_sw_knorm': 'new_m', 'new_m_sw_sinks': 'new_m', 'new_m_gd_in_w': 'new_m', 'new_m_gd_out_w': 'new_m', 'new_m_gd_conv_w': 'new_m', 'new_m_gd_a_log': 'new_m', 'new_m_gd_dt_bias': 'new_m', 'new_m_gd_onorm': 'new_m', 'new_v_hgrn_lb': 'new_v', 'new_v_ada_w': 'new_v', 'new_v_ada_b': 'new_v', 'new_v_norm_g': 'new_v', 'new_v_hg_in_w': 'new_v', 'new_v_hg_out_w': 'new_v', 'new_v_hg_onorm': 'new_v', 'new_v_sw_in_w': 'new_v', 'new_v_sw_out_w': 'new_v', 'new_v_sw_qnorm': 'new_v', 'new_v_sw_knorm': 'new_v', 'new_v_sw_sinks': 'new_v', 'new_v_gd_in_w': 'new_v', 'new_v_gd_out_w': 'new_v', 'new_v_gd_conv_w': 'new_v', 'new_v_gd_a_log': 'new_v', 'new_v_gd_dt_bias': 'new_v', 'new_v_gd_onorm': 'new_v'}


def _forward(args):
    return _fwd_reference(*[args[k] for k in FWD_PARAMS])


def _output_shape():
    def fwd():
        inp = _fwd_setup_inputs(0)
        return _fwd_reference(*[inp[k] for k in FWD_PARAMS])
    out = _jax.eval_shape(fwd)
    return out.shape, out.dtype

N_MICROBATCH = 1
ADAM_LR = 0.001
ADAM_B1 = 0.9
ADAM_B2 = 0.999
ADAM_EPS = 1e-08
ADAM_WD = 0.01
ADAM_STEP = 10
PER_EXAMPLE_BATCH_AXIS = {'x': 0, 'c': 0, 'positions': 0, 'loss_target': 0}
SHARED_INPUTS = []
_WEIGHT_DTYPES = {'hgrn_lb': _jnp.float32, 'ada_w': _jnp.float32, 'ada_b': _jnp.float32, 'norm_g': _jnp.float32, 'hg_in_w': _jnp.float32, 'hg_out_w': _jnp.float32, 'hg_onorm': _jnp.float32, 'sw_in_w': _jnp.float32, 'sw_out_w': _jnp.float32, 'sw_qnorm': _jnp.float32, 'sw_knorm': _jnp.float32, 'sw_sinks': _jnp.float32, 'gd_in_w': _jnp.float32, 'gd_out_w': _jnp.float32, 'gd_conv_w': _jnp.float32, 'gd_a_log': _jnp.float32, 'gd_dt_bias': _jnp.float32, 'gd_onorm': _jnp.float32}
MOMENT_SCALE = {'hgrn_lb': 4.822948e-02, 'ada_w': 1.011527e+01, 'ada_b': 1.350435e+01, 'norm_g': 1.664124e+01, 'hg_in_w': 2.831567e+00, 'hg_out_w': 2.751137e+00, 'hg_onorm': 1.525288e+02, 'sw_in_w': 2.734823e+00, 'sw_out_w': 1.989640e+00, 'sw_qnorm': 3.254193e+00, 'sw_knorm': 3.237395e+00, 'sw_sinks': 9.260573e-01, 'gd_in_w': 2.098931e+00, 'gd_out_w': 2.703339e+00, 'gd_conv_w': 2.097440e+00, 'gd_a_log': 2.189058e+01, 'gd_dt_bias': 2.098240e+01, 'gd_onorm': 1.497732e+02}


def _to_microbatches(a, axis):
    t = _jnp.moveaxis(a, axis, 0)
    t = t.reshape((N_MICROBATCH, t.shape[0] // N_MICROBATCH) + t.shape[1:])
    return _jnp.moveaxis(t, 1, axis + 1)


def setup_inputs(seed: int = 0) -> dict:
    inp = _fwd_setup_inputs(seed)
    key = _jax.random.fold_in(_jax.random.key(seed), 7919)
    shape, _ = _output_shape()
    out = dict(inp)
    out["loss_target"] = _jax.random.normal(_jax.random.fold_in(key, 0), shape, _jnp.float32)
    for i, name in enumerate(TWIN_WEIGHTS):
        w = inp[name].astype(_jnp.float32)
        if MOMENT_SCALE is None:
            s = _jnp.sqrt(_jnp.mean(_jnp.square(w)) + 1e-30)
        else:
            s = MOMENT_SCALE[name]
        km, kv = _jax.random.split(_jax.random.fold_in(key, i + 1))
        out[name] = w
        out["m_" + name] = s * _jax.random.normal(km, w.shape, _jnp.float32)
        out["v_" + name] = (s * s) * _jax.random.uniform(kv, w.shape, _jnp.float32, 0.5, 1.5)
    if N_MICROBATCH > 1:
        for name, axis in PER_EXAMPLE_BATCH_AXIS.items():
            out[name] = _to_microbatches(out[name], axis)
    return {'x': out['x'], 'c': out['c'], 'positions': out['positions'], 'hgrn_lb': out['hgrn_lb'], 'ada_w': out['ada_w'], 'ada_b': out['ada_b'], 'norm_g': out['norm_g'], 'hg_in_w': out['hg_in_w'], 'hg_out_w': out['hg_out_w'], 'hg_onorm': out['hg_onorm'], 'sw_in_w': out['sw_in_w'], 'sw_out_w': out['sw_out_w'], 'sw_qnorm': out['sw_qnorm'], 'sw_knorm': out['sw_knorm'], 'sw_sinks': out['sw_sinks'], 'gd_in_w': out['gd_in_w'], 'gd_out_w': out['gd_out_w'], 'gd_conv_w': out['gd_conv_w'], 'gd_a_log': out['gd_a_log'], 'gd_dt_bias': out['gd_dt_bias'], 'gd_onorm': out['gd_onorm'], 'loss_target': out['loss_target'], 'm_hgrn_lb': out['m_hgrn_lb'], 'm_ada_w': out['m_ada_w'], 'm_ada_b': out['m_ada_b'], 'm_norm_g': out['m_norm_g'], 'm_hg_in_w': out['m_hg_in_w'], 'm_hg_out_w': out['m_hg_out_w'], 'm_hg_onorm': out['m_hg_onorm'], 'm_sw_in_w': out['m_sw_in_w'], 'm_sw_out_w': out['m_sw_out_w'], 'm_sw_qnorm': out['m_sw_qnorm'], 'm_sw_knorm': out['m_sw_knorm'], 'm_sw_sinks': out['m_sw_sinks'], 'm_gd_in_w': out['m_gd_in_w'], 'm_gd_out_w': out['m_gd_out_w'], 'm_gd_conv_w': out['m_gd_conv_w'], 'm_gd_a_log': out['m_gd_a_log'], 'm_gd_dt_bias': out['m_gd_dt_bias'], 'm_gd_onorm': out['m_gd_onorm'], 'v_hgrn_lb': out['v_hgrn_lb'], 'v_ada_w': out['v_ada_w'], 'v_ada_b': out['v_ada_b'], 'v_norm_g': out['v_norm_g'], 'v_hg_in_w': out['v_hg_in_w'], 'v_hg_out_w': out['v_hg_out_w'], 'v_hg_onorm': out['v_hg_onorm'], 'v_sw_in_w': out['v_sw_in_w'], 'v_sw_out_w': out['v_sw_out_w'], 'v_sw_qnorm': out['v_sw_qnorm'], 'v_sw_knorm': out['v_sw_knorm'], 'v_sw_sinks': out['v_sw_sinks'], 'v_gd_in_w': out['v_gd_in_w'], 'v_gd_out_w': out['v_gd_out_w'], 'v_gd_conv_w': out['v_gd_conv_w'], 'v_gd_a_log': out['v_gd_a_log'], 'v_gd_dt_bias': out['v_gd_dt_bias'], 'v_gd_onorm': out['v_gd_onorm']}


def _loss(weights, diff, rest, loss_target):
    with _jax.named_scope("forward"):
        args = {**rest, TWIN_DIFF_INPUT: diff, **{k: w.astype(_WEIGHT_DTYPES[k]) for k, w in weights.items()}}
        y = _forward(args)
    with _jax.named_scope("loss_head"):
        err = _jnp.square(y.astype(_jnp.float32) - loss_target)
        return 0.5 * _jnp.sum(_jnp.mean(err, axis=-1)) if err.ndim else 0.5 * err


def _adamw(w, g, m, v):
    m = ADAM_B1 * m + (1.0 - ADAM_B1) * g
    v = ADAM_B2 * v + (1.0 - ADAM_B2) * _jnp.square(g)
    m_hat = m / (1.0 - ADAM_B1 ** ADAM_STEP)
    v_hat = v / (1.0 - ADAM_B2 ** ADAM_STEP)
    delta = -ADAM_LR * (m_hat / (_jnp.sqrt(v_hat) + ADAM_EPS) + ADAM_WD * w)
    return delta, m, v


def reference(x, c, positions, hgrn_lb, ada_w, ada_b, norm_g, hg_in_w, hg_out_w, hg_onorm, sw_in_w, sw_out_w, sw_qnorm, sw_knorm, sw_sinks, gd_in_w, gd_out_w, gd_conv_w, gd_a_log, gd_dt_bias, gd_onorm, loss_target, m_hgrn_lb, m_ada_w, m_ada_b, m_norm_g, m_hg_in_w, m_hg_out_w, m_hg_onorm, m_sw_in_w, m_sw_out_w, m_sw_qnorm, m_sw_knorm, m_sw_sinks, m_gd_in_w, m_gd_out_w, m_gd_conv_w, m_gd_a_log, m_gd_dt_bias, m_gd_onorm, v_hgrn_lb, v_ada_w, v_ada_b, v_norm_g, v_hg_in_w, v_hg_out_w, v_hg_onorm, v_sw_in_w, v_sw_out_w, v_sw_qnorm, v_sw_knorm, v_sw_sinks, v_gd_in_w, v_gd_out_w, v_gd_conv_w, v_gd_a_log, v_gd_dt_bias, v_gd_onorm):
    given = dict(x=x, c=c, positions=positions, hgrn_lb=hgrn_lb, ada_w=ada_w, ada_b=ada_b, norm_g=norm_g, hg_in_w=hg_in_w, hg_out_w=hg_out_w, hg_onorm=hg_onorm, sw_in_w=sw_in_w, sw_out_w=sw_out_w, sw_qnorm=sw_qnorm, sw_knorm=sw_knorm, sw_sinks=sw_sinks, gd_in_w=gd_in_w, gd_out_w=gd_out_w, gd_conv_w=gd_conv_w, gd_a_log=gd_a_log, gd_dt_bias=gd_dt_bias, gd_onorm=gd_onorm, loss_target=loss_target, m_hgrn_lb=m_hgrn_lb, m_ada_w=m_ada_w, m_ada_b=m_ada_b, m_norm_g=m_norm_g, m_hg_in_w=m_hg_in_w, m_hg_out_w=m_hg_out_w, m_hg_onorm=m_hg_onorm, m_sw_in_w=m_sw_in_w, m_sw_out_w=m_sw_out_w, m_sw_qnorm=m_sw_qnorm, m_sw_knorm=m_sw_knorm, m_sw_sinks=m_sw_sinks, m_gd_in_w=m_gd_in_w, m_gd_out_w=m_gd_out_w, m_gd_conv_w=m_gd_conv_w, m_gd_a_log=m_gd_a_log, m_gd_dt_bias=m_gd_dt_bias, m_gd_onorm=m_gd_onorm, v_hgrn_lb=v_hgrn_lb, v_ada_w=v_ada_w, v_ada_b=v_ada_b, v_norm_g=v_norm_g, v_hg_in_w=v_hg_in_w, v_hg_out_w=v_hg_out_w, v_hg_onorm=v_hg_onorm, v_sw_in_w=v_sw_in_w, v_sw_out_w=v_sw_out_w, v_sw_qnorm=v_sw_qnorm, v_sw_knorm=v_sw_knorm, v_sw_sinks=v_sw_sinks, v_gd_in_w=v_gd_in_w, v_gd_out_w=v_gd_out_w, v_gd_conv_w=v_gd_conv_w, v_gd_a_log=v_gd_a_log, v_gd_dt_bias=v_gd_dt_bias, v_gd_onorm=v_gd_onorm)
    weights = {n: given[n] for n in TWIN_WEIGHTS}
    shared = {n: given[n] for n in SHARED_INPUTS}
    per_example = {n: given[n] for n in ['x', 'c', 'positions']}
    grad_fn = _jax.value_and_grad(_loss, argnums=(0, 1))

    def one_microbatch(ex, loss_target):
        ex = dict(ex)
        diff = ex.pop(TWIN_DIFF_INPUT)
        return grad_fn(weights, diff, {**shared, **ex}, loss_target)

    if N_MICROBATCH == 1:
        loss, (grad_w, grad_x) = one_microbatch(per_example, given["loss_target"])
    else:
        def body(carry, xs):
            loss_sum, grad_sum = carry
            l_k, (gw_k, gx_k) = one_microbatch(xs[0], xs[1])
            with _jax.named_scope("update"):
                return (loss_sum + l_k, _jax.tree.map(_jnp.add, grad_sum, gw_k)), gx_k

        init = (_jnp.zeros((), _jnp.float32), _jax.tree.map(_jnp.zeros_like, weights))
        (loss, grad_w), grad_x = _jax.lax.scan(body, init, (per_example, given["loss_target"]))
    with _jax.named_scope("update"):
        delta_w, new_m, new_v = {}, {}, {}
        for n in TWIN_WEIGHTS:
            delta_w[n], new_m[n], new_v[n] = _adamw(weights[n], grad_w[n], given["m_" + n], given["v_" + n])
    return (loss, grad_x, *[grad_w[n] for n in TWIN_WEIGHTS], *[delta_w[n] for n in TWIN_WEIGHTS],
            *[new_m[n] for n in TWIN_WEIGHTS], *[new_v[n] for n in TWIN_WEIGHTS])
```

```python
import functools

import jax
import jax.numpy as jnp
from jax import lax
from jax.experimental import pallas as pl
from jax.experimental.pallas import tpu as pltpu

F32 = jnp.float32
BF16 = jnp.bfloat16
D = 1024
EPS = 1e-6
CHUNK = 64
SUB = 16
HG_H = 8
HD = 128
VMEM_LIMIT = 56 * 1024 * 1024


def _cparams(sem=None):
    return pltpu.CompilerParams(dimension_semantics=sem, vmem_limit_bytes=VMEM_LIMIT)


def _dot(a, b, ca, cb, prec=None):
    return lax.dot_general(a, b, (((ca,), (cb,)), ((), ())), precision=prec, preferred_element_type=F32)


def _mm(a, b):
    return _dot(a.astype(BF16), b.astype(BF16), 1, 0)


def _mm_nt(a, b):
    return _dot(a.astype(BF16), b.astype(BF16), 1, 1)


def _mm_tn(a, b):
    return _dot(a.astype(BF16), b.astype(BF16), 0, 0)


def _mm_f32(a, b):
    return _dot(a, b, 1, 0, lax.Precision.HIGHEST)


def _silu(x):
    return x * jax.nn.sigmoid(x)


def _hg_chunk(q_raw, f_pre, v, z, st, lb, go):
    c = q_raw.shape[0]
    nsub = c // SUB
    lf = jnp.log(lb + (1.0 - lb) * jax.nn.sigmoid(f_pre))
    k = (1.0 - lb) * jax.nn.sigmoid(-f_pre)
    q = _silu(q_raw)
    ti = lax.broadcasted_iota(jnp.int32, (c, c), 0)
    si = lax.broadcasted_iota(jnp.int32, (c, c), 1)
    mats = [(si <= ti).astype(F32)] + [(si <= SUB * i + SUB // 2).astype(F32) for i in range(nsub)]
    cums = jnp.split(_mm_f32(jnp.concatenate(mats, axis=0), lf), nsub + 1, axis=0)
    b, bmid = cums[0], cums[1:]
    row = lax.broadcasted_iota(jnp.int32, (c, 1), 0)
    ref = sum(jnp.where((row >= SUB * i) & (row < SUB * (i + 1)), bmid[i], 0.0) for i in range(nsub))
    qt = q * jnp.exp(b - ref)
    kall = jnp.concatenate(
        [k * jnp.exp(jnp.where(row < SUB * (i + 1), bmid[i] - b, -jnp.inf)) for i in range(nsub)], axis=0)
    v4 = jnp.concatenate([v] * nsub, axis=0)
    b_last = jnp.sum(lf, axis=0, keepdims=True)
    qb = q * jnp.exp(b)
    kd = k * jnp.exp(b_last - b)
    e_last = jnp.exp(b_last)
    tq = lax.broadcasted_iota(jnp.int32, (c, nsub * c), 0)
    cq = lax.broadcasted_iota(jnp.int32, (c, nsub * c), 1)
    m_all = ((cq // c) == (tq // SUB)) & ((cq % c) <= tq)
    hs = lambda a: jnp.split(a, HG_H, axis=1)
    qt_h, kall_h, v4_h, qb_h, kd_h, v_h, z_h, el_h = map(hs, (qt, kall, v4, qb, kd, v, z, e_last))
    st_h = jnp.split(st, HG_H, axis=0)
    p_out, st_out = [], []
    for h in range(HG_H):
        pm = jnp.where(m_all, _mm_nt(qt_h[h], kall_h[h]), 0.0)
        o = _mm(pm, v4_h[h]) + _mm_nt(qb_h[h], st_h[h])
        st_out.append(el_h[h] * st_h[h] + _mm_tn(v_h[h], kd_h[h]))
        y = o * lax.rsqrt(jnp.mean(o * o, axis=1, keepdims=True) + EPS) * go
        p_out.append(y * _silu(z_h[h]))
    return jnp.concatenate(p_out, axis=1), jnp.concatenate(st_out, axis=0)


def _hg_fwd(u, lb, go):
    t = u.shape[0]
    n = t // CHUNK

    def body(u_ref, lb_ref, go_ref, p_ref, sts_ref, st_ref):
        @pl.when(pl.program_id(0) == 0)
        def _():
            st_ref[...] = jnp.zeros_like(st_ref)

        st = st_ref[...]
        sts_ref[0] = st
        p, st_next = _hg_chunk(u_ref[:, 0:D], u_ref[:, D:2 * D], u_ref[:, 2 * D:3 * D], u_ref[:, 3 * D:4 * D],
                               st, lb_ref[...], go_ref[...])
        p_ref[...] = p.astype(BF16)
        st_ref[...] = st_next

    return pl.pallas_call(
        body, name="hg_fwd", grid=(n,),
        in_specs=[pl.BlockSpec((CHUNK, 4 * D), lambda i: (i, 0)),
                  pl.BlockSpec((1, D), lambda i: (0, 0)),
                  pl.BlockSpec((1, HD), lambda i: (0, 0))],
        out_specs=[pl.BlockSpec((CHUNK, D), lambda i: (i, 0)),
                   pl.BlockSpec((1, HG_H * HD, HD), lambda i: (i, 0, 0))],
        out_shape=[jax.ShapeDtypeStruct((t, D), BF16), jax.ShapeDtypeStruct((n, HG_H * HD, HD), F32)],
        scratch_shapes=[pltpu.VMEM((HG_H * HD, HD), F32)],
        compiler_params=_cparams(("arbitrary",)),
    )(u, lb, go)


def _hg_bwd(u, sts, dp, lb, go):
    t = u.shape[0]
    n = t // CHUNK

    def body(u_ref, sts_ref, dp_ref, lb_ref, go_ref, du_ref, dlb_ref, dgo_ref, dst_ref):
        @pl.when(pl.program_id(0) == 0)
        def _():
            dst_ref[...] = jnp.zeros_like(dst_ref)
            dlb_ref[...] = jnp.zeros_like(dlb_ref)
            dgo_ref[...] = jnp.zeros_like(dgo_ref)

        _, vjp = jax.vjp(_hg_chunk, u_ref[:, 0:D], u_ref[:, D:2 * D], u_ref[:, 2 * D:3 * D], u_ref[:, 3 * D:4 * D],
                         sts_ref[0], lb_ref[...], go_ref[...])
        dq, df, dv, dz, dst, dlb, dgo = vjp((dp_ref[...].astype(F32), dst_ref[...]))
        du_ref[:, 0:D] = dq
        du_ref[:, D:2 * D] = df
        du_ref[:, 2 * D:3 * D] = dv
        du_ref[:, 3 * D:4 * D] = dz
        dst_ref[...] = dst
        dlb_ref[...] += dlb
        dgo_ref[...] += dgo

    rev = lambda i: (n - 1 - i, 0)
    return pl.pallas_call(
        body, name="hg_bwd", grid=(n,),
        in_specs=[pl.BlockSpec((CHUNK, 4 * D), rev),
                  pl.BlockSpec((1, HG_H * HD, HD), lambda i: (n - 1 - i, 0, 0)),
                  pl.BlockSpec((CHUNK, D), rev),
                  pl.BlockSpec((1, D), lambda i: (0, 0)),
                  pl.BlockSpec((1, HD), lambda i: (0, 0))],
        out_specs=[pl.BlockSpec((CHUNK, 4 * D), rev),
                   pl.BlockSpec((1, D), lambda i: (0, 0)),
                   pl.BlockSpec((1, HD), lambda i: (0, 0))],
        out_shape=[jax.ShapeDtypeStruct((t, 4 * D), F32), jax.ShapeDtypeStruct((1, D), F32),
                   jax.ShapeDtypeStruct((1, HD), F32)],
        scratch_shapes=[pltpu.VMEM((HG_H * HD, HD), F32)],
        compiler_params=_cparams(("arbitrary",)),
    )(u, sts, dp, lb, go)


GD_VH = 16
GD_QKH = 8
GD_QKV = 4096
GD_VW = 2048
GD_N = GD_QKV + GD_VW + HD
HALO = 8


def _mm_high(a, b):
    return _dot(a, b, 1, 0, lax.Precision.HIGH)


def _lane_pick(a, h):
    lane = lax.broadcasted_iota(jnp.int32, a.shape, 1)
    return jnp.sum(jnp.where(lane == h, a, 0.0), axis=1, keepdims=True)


def _l2n(x):
    return x * lax.rsqrt(jnp.sum(x * x, axis=1, keepdims=True) + EPS)


def _gd_chunk(xh, x, z, ab, st, cw, alog, dtb, go):
    c = x.shape[0]
    xa = jnp.concatenate([xh, x], axis=0)
    ti = lax.broadcasted_iota(jnp.int32, (3 * c, c + HALO), 0)
    si = lax.broadcasted_iota(jnp.int32, (3 * c, c + HALO), 1)
    shift = (si == (ti % c) + (ti // c) + HALO - 3).astype(F32)
    sh = jnp.split(_mm_high(shift, xa), 3, axis=0)
    qkv = _silu(cw[0:1] * sh[0] + cw[1:2] * sh[1] + cw[2:3] * sh[2] + cw[3:4] * x)
    q_all, k_all, v_all = jnp.split(qkv, [1024, 2048], axis=1)
    lane = lax.broadcasted_iota(jnp.int32, (c, HD), 1)
    a_part = jnp.where(lane < GD_VH, ab, 0.0)
    g_all = -jnp.exp(alog) * jax.nn.softplus(a_part + dtb)
    tri = (lax.broadcasted_iota(jnp.int32, (c, c), 1) <= lax.broadcasted_iota(jnp.int32, (c, c), 0))
    d_all = _mm_f32(tri.astype(F32), g_all)
    dl_all = jnp.sum(g_all, axis=0, keepdims=True)
    beta_all = jax.nn.sigmoid(ab)
    strict = (lax.broadcasted_iota(jnp.int32, (c, c), 1) < lax.broadcasted_iota(jnp.int32, (c, c), 0))
    eye = (lax.broadcasted_iota(jnp.int32, (c, c), 1) == lax.broadcasted_iota(jnp.int32, (c, c), 0)).astype(F32)
    ones = jnp.ones((c, HD), F32)
    r_i, c_i = lax.broadcasted_iota(jnp.int32, (c, c), 0), lax.broadcasted_iota(jnp.int32, (c, c), 1)
    blk = {nb: (r_i // nb) == (c_i // nb) for nb in (8, 16, 32, 64)}
    qs = jnp.split(q_all, GD_QKH, axis=1)
    ks = jnp.split(k_all, GD_QKH, axis=1)
    vs = jnp.split(v_all, GD_VH, axis=1)
    zs = jnp.split(z, GD_VH, axis=1)
    sts = jnp.split(st, GD_VH, axis=0)
    qn = [_l2n(a) * (HD ** -0.5) for a in qs]
    kn = [_l2n(a) for a in ks]
    p_out, st_out = [], []
    for h in range(GD_VH):
        q_, k_, v_, s_ = qn[h // 2], kn[h // 2], vs[h], sts[h]
        dcol = _lane_pick(d_all, h)
        bcol = _lane_pick(beta_all, GD_VH + h)
        dlast = _lane_pick(dl_all, h)
        drow = _dot(ones, jnp.where(lane == h, d_all, 0.0), 1, 1, lax.Precision.HIGHEST)
        dec = jnp.exp(jnp.where(tri, dcol - drow, -jnp.inf))
        kb = k_ * bcol
        a_mat = jnp.where(strict, _mm_nt(kb, k_) * dec, 0.0)
        d0 = jnp.where(blk[8], a_mat, 0.0)
        d2 = _mm_f32(d0, d0)
        tinv = eye - d0
        tinv = tinv + _mm_f32(tinv, d2)
        tinv = tinv + _mm_f32(tinv, _mm_f32(d2, d2))
        for nb in (16, 32, 64):
            low = jnp.where(blk[nb] & ~blk[nb // 2], a_mat, 0.0)
            tinv = tinv - _mm_f32(_mm_f32(tinv, low), tinv)
        xsol = _mm_f32(tinv, jnp.concatenate([v_ * bcol, kb * jnp.exp(dcol)], axis=1))
        u_, w_ = jnp.split(xsol, 2, axis=1)
        v_new = u_ - _mm(w_, s_)
        qk = _mm_nt(q_, k_) * dec
        o = _mm(q_ * jnp.exp(dcol), s_) + _mm(qk, v_new)
        st_out.append(s_ * jnp.exp(dlast) + _mm_tn(k_ * jnp.exp(dlast - dcol), v_new))
        y = o * lax.rsqrt(jnp.mean(o * o, axis=1, keepdims=True) + EPS) * go
        p_out.append(y * _silu(zs[h]))
    return jnp.concatenate(p_out, axis=1), jnp.concatenate(st_out, axis=0)


def _gd_specs(n, rev):
    ci = (lambda i: n - 1 - i) if rev else (lambda i: i)
    return [pl.BlockSpec((HALO, GD_QKV), lambda i: (jnp.maximum(ci(i) * (CHUNK // HALO) - 1, 0), 0)),
            pl.BlockSpec((CHUNK, GD_N), lambda i: (ci(i), 0))]


def _gd_load(uh_ref, u_ref, first):
    xh = jnp.where(first, 0.0, uh_ref[...])
    return xh, u_ref[:, 0:GD_QKV], u_ref[:, GD_QKV:GD_QKV + GD_VW], u_ref[:, GD_QKV + GD_VW:GD_N]


def _gd_fwd(u, cw, alog, dtb, go):
    t = u.shape[0]
    n = t // CHUNK
    small = lambda r, w: pl.BlockSpec((r, w), lambda i: (0, 0))

    def body(uh_ref, u_ref, cw_ref, alog_ref, dtb_ref, go_ref, p_ref, sts_ref, st_ref):
        i = pl.program_id(0)

        @pl.when(i == 0)
        def _():
            st_ref[...] = jnp.zeros_like(st_ref)

        st = st_ref[...]
        sts_ref[0] = st
        p, st_next = _gd_chunk(*_gd_load(uh_ref, u_ref, i == 0), st, cw_ref[...], alog_ref[...], dtb_ref[...],
                               go_ref[...])
        p_ref[...] = p.astype(BF16)
        st_ref[...] = st_next

    return pl.pallas_call(
        body, name="gd_fwd", grid=(n,),
        in_specs=_gd_specs(n, False) + [small(8, GD_QKV), small(1, HD), small(1, HD), small(1, HD)],
        out_specs=[pl.BlockSpec((CHUNK, GD_VW), lambda i: (i, 0)),
                   pl.BlockSpec((1, GD_VH * HD, HD), lambda i: (i, 0, 0))],
        out_shape=[jax.ShapeDtypeStruct((t, GD_VW), BF16), jax.ShapeDtypeStruct((n, GD_VH * HD, HD), F32)],
        scratch_shapes=[pltpu.VMEM((GD_VH * HD, HD), F32)],
        compiler_params=_cparams(("arbitrary",)),
    )(u, u, cw, alog, dtb, go)


def _gd_bwd(u, sts, dp, cw, alog, dtb, go):
    t = u.shape[0]
    n = t // CHUNK
    small = lambda r, w: pl.BlockSpec((r, w), lambda i: (0, 0))

    def body(uh_ref, u_ref, sts_ref, dp_ref, cw_ref, alog_ref, dtb_ref, go_ref,
             du_ref, dcw_ref, dalog_ref, ddtb_ref, dgo_ref, dst_ref, dhalo_ref):
        i = pl.program_id(0)

        @pl.when(i == 0)
        def _():
            for r in (dst_ref, dhalo_ref, dcw_ref, dalog_ref, ddtb_ref, dgo_ref):
                r[...] = jnp.zeros_like(r)

        _, vjp = jax.vjp(_gd_chunk, *_gd_load(uh_ref, u_ref, i == n - 1), sts_ref[0], cw_ref[...], alog_ref[...],
                         dtb_ref[...], go_ref[...])
        dxh, dx, dz, dab, dst, dcw, dalog, ddtb, dgo = vjp((dp_ref[...].astype(F32), dst_ref[...]))
        du_ref[:, 0:GD_QKV] = dx
        du_ref[CHUNK - HALO:CHUNK, 0:GD_QKV] += dhalo_ref[...]
        du_ref[:, GD_QKV:GD_QKV + GD_VW] = dz
        du_ref[:, GD_QKV + GD_VW:GD_N] = dab
        dhalo_ref[...] = dxh
        dst_ref[...] = dst
        dcw_ref[...] += dcw
        dalog_ref[...] += dalog
        ddtb_ref[...] += ddtb
        dgo_ref[...] += dgo

    return pl.pallas_call(
        body, name="gd_bwd", grid=(n,),
        in_specs=_gd_specs(n, True) + [pl.BlockSpec((1, GD_VH * HD, HD), lambda i: (n - 1 - i, 0, 0)),
                                       pl.BlockSpec((CHUNK, GD_VW), lambda i: (n - 1 - i, 0)),
                                       small(8, GD_QKV), small(1, HD), small(1, HD), small(1, HD)],
        out_specs=[pl.BlockSpec((CHUNK, GD_N), lambda i: (n - 1 - i, 0)),
                   small(8, GD_QKV), small(1, HD), small(1, HD), small(1, HD)],
        out_shape=[jax.ShapeDtypeStruct((t, GD_N), F32), jax.ShapeDtypeStruct((8, GD_QKV), F32)]
        + [jax.ShapeDtypeStruct((1, HD), F32)] * 3,
        scratch_shapes=[pltpu.VMEM((GD_VH * HD, HD), F32), pltpu.VMEM((HALO, GD_QKV), F32)],
        compiler_params=_cparams(("arbitrary",)),
    )(u, u, sts, dp, cw, alog, dtb, go)


SW_B = 128
SW_H = 16
SW_G = 4
SW_N = 2560
SW_KV0 = 1024


def _blockdiag(n, blk):
    r = lax.broadcasted_iota(jnp.int32, (n, n), 0) // blk
    c = lax.broadcasted_iota(jnp.int32, (n, n), 1) // blk
    return (r == c).astype(F32)


def _sw_normrope(x, g1, g2, cos, sin):
    w = x.shape[1] // 2
    x1, x2 = jnp.split(x, 2, axis=1)
    ms = _mm_high(x1 * x1 + x2 * x2, _blockdiag(w, 32)) * (1.0 / 64.0)
    rinv = lax.rsqrt(ms + EPS)
    n1, n2 = x1 * rinv * g1, x2 * rinv * g2
    return jnp.concatenate([n1 * cos - n2 * sin, n2 * cos + n1 * sin], axis=1)


def _sw_block(q, kvp, kvc, z, csp, csc, gq, gk, sinks, has_prev):
    b = q.shape[0]
    cos_c, sin_c = jnp.split(csc, 2, axis=1)
    cos_p, sin_p = jnp.split(csp, 2, axis=1)
    tile4 = lambda a: jnp.concatenate([a] * 4, axis=1)
    qh = _sw_normrope(q, gq[0:1], gq[1:2], tile4(cos_c), tile4(sin_c))
    kp, vp = jnp.split(kvp, 2, axis=1)
    kc, vc = jnp.split(kvc, 2, axis=1)
    kh = jnp.concatenate([_sw_normrope(kp, gk[0:1], gk[1:2], cos_p, sin_p),
                          _sw_normrope(kc, gk[0:1], gk[1:2], cos_c, sin_c)], axis=0)
    vv = jnp.concatenate([vp, vc], axis=0)
    q1, q2 = jnp.split(qh, 2, axis=1)
    q1g, q2g = jnp.split(q1, SW_G, axis=1), jnp.split(q2, SW_G, axis=1)
    qi = lax.broadcasted_iota(jnp.int32, (b, 2 * b), 0)
    kj = lax.broadcasted_iota(jnp.int32, (b, 2 * b), 1)
    rel = qi + b - kj
    mask = (rel >= 0) & (rel < SW_B) & (has_prev | (kj >= b))
    ri = lax.broadcasted_iota(jnp.int32, (256, 256), 0)
    ci = lax.broadcasted_iota(jnp.int32, (256, 256), 1)
    lane256 = lax.broadcasted_iota(jnp.int32, (1, 256), 1)
    o_out = []
    for g in range(SW_G):
        ek = ((ri // 128 == ci // 128) & ((ri % 128) // 32 == g) & (ri % 32 == ci % 32)).astype(F32)
        ev = ((ri // 64 == g) & (ri % 64 == ci % 64)).astype(F32)
        kx = _mm(kh, ek)
        vx = _mm(vv, ev)
        qg = jnp.concatenate([q1g[g], q2g[g]], axis=1)
        og = jnp.zeros((b, 256), F32)
        for j in range(4):
            hmask = (lane256 % 128) // 32 == j
            s = _mm_nt(jnp.where(hmask, qg, 0.0), kx) * (64 ** -0.5)
            s = jnp.where(mask, s, -jnp.inf)
            sink = _lane_pick(sinks, 4 * g + j)
            m = jnp.maximum(jnp.max(s, axis=1, keepdims=True), sink)
            p = jnp.exp(s - m)
            pn = p / (jnp.sum(p, axis=1, keepdims=True) + jnp.exp(sink - m))
            og = og + jnp.where(lane256 // 64 == j, _mm(pn, vx), 0.0)
        o_out.append(og)
    return jnp.concatenate(o_out, axis=1) * _silu(z)


def _sw_specs(n, rev):
    ci = (lambda i: n - 1 - i) if rev else (lambda i: i)
    prev = lambda i: jnp.maximum(ci(i) - 1, 0)
    return [pl.BlockSpec((SW_B, SW_N), lambda i: (ci(i), 0)),
            pl.BlockSpec((SW_B, 512), lambda i: (prev(i), SW_KV0 // 512)),
            pl.BlockSpec((SW_B, 256), lambda i: (ci(i), 0)),
            pl.BlockSpec((SW_B, 256), lambda i: (prev(i), 0)),
            pl.BlockSpec((2, 512), lambda i: (0, 0)), pl.BlockSpec((2, 128), lambda i: (0, 0)),
            pl.BlockSpec((1, 128), lambda i: (0, 0))]


def _sw_args(u_ref, kvp_ref, csc_ref, csp_ref, gq_ref, gk_ref, sk_ref, has_prev):
    return (u_ref[:, 0:D], kvp_ref[...], u_ref[:, SW_KV0:SW_KV0 + 512], u_ref[:, SW_KV0 + 512:SW_N],
            csp_ref[...], csc_ref[...], gq_ref[...], gk_ref[...], sk_ref[...], has_prev)


def _sw_fwd(u, cs, gq, gk, sinks):
    t = u.shape[0]
    n = t // SW_B

    def body(u_ref, kvp_ref, csc_ref, csp_ref, gq_ref, gk_ref, sk_ref, p_ref):
        has_prev = pl.program_id(0) > 0
        p_ref[...] = _sw_block(*_sw_args(u_ref, kvp_ref, csc_ref, csp_ref, gq_ref, gk_ref, sk_ref, has_prev)
                               ).astype(BF16)

    return pl.pallas_call(
        body, name="sw_fwd", grid=(n,), in_specs=_sw_specs(n, False),
        out_specs=pl.BlockSpec((SW_B, D), lambda i: (i, 0)),
        out_shape=jax.ShapeDtypeStruct((t, D), BF16),
        compiler_params=_cparams(("arbitrary",)),
    )(u, u, cs, cs, gq, gk, sinks)


def _sw_bwd(u, cs, dp, gq, gk, sinks):
    t = u.shape[0]
    n = t // SW_B

    def body(u_ref, kvp_ref, csc_ref, csp_ref, gq_ref, gk_ref, sk_ref, dp_ref,
             du_ref, dgq_ref, dgk_ref, dsk_ref, dkv_ref):
        i = pl.program_id(0)

        @pl.when(i == 0)
        def _():
            for r in (dkv_ref, dgq_ref, dgk_ref, dsk_ref):
                r[...] = jnp.zeros_like(r)

        has_prev = i < n - 1
        args = _sw_args(u_ref, kvp_ref, csc_ref, csp_ref, gq_ref, gk_ref, sk_ref, has_prev)
        fn = lambda q, kvp, kvc, z, gq_, gk_, sk_: _sw_block(q, kvp, kvc, z, args[4], args[5], gq_, gk_, sk_, has_prev)
        _, vjp = jax.vjp(fn, args[0], args[1], args[2], args[3], args[6], args[7], args[8])
        dq, dkvp, dkvc, dz, dgq, dgk, dsk = vjp(dp_ref[...].astype(F32))
        du_ref[:, 0:D] = dq
        du_ref[:, SW_KV0:SW_KV0 + 512] = dkvc + dkv_ref[...]
        du_ref[:, SW_KV0 + 512:SW_N] = dz
        dkv_ref[...] = dkvp
        dgq_ref[...] += dgq
        dgk_ref[...] += dgk
        dsk_ref[...] += dsk

    small = lambda r, w: pl.BlockSpec((r, w), lambda i: (0, 0))
    return pl.pallas_call(
        body, name="sw_bwd", grid=(n,),
        in_specs=_sw_specs(n, True) + [pl.BlockSpec((SW_B, D), lambda i: (n - 1 - i, 0))],
        out_specs=[pl.BlockSpec((SW_B, SW_N), lambda i: (n - 1 - i, 0)), small(2, 512), small(2, 128), small(1, 128)],
        out_shape=[jax.ShapeDtypeStruct((t, SW_N), F32), jax.ShapeDtypeStruct((2, 512), F32),
                   jax.ShapeDtypeStruct((2, 128), F32), jax.ShapeDtypeStruct((1, 128), F32)],
        scratch_shapes=[pltpu.VMEM((SW_B, 512), F32)],
        compiler_params=_cparams(("arbitrary",)),
    )(u, u, cs, cs, gq, gk, sinks, dp)


def _ln_mod(x, g, scale, shift):
    y = x * lax.rsqrt(jnp.mean(x * x, axis=1, keepdims=True) + EPS) * g
    return y * (1.0 + scale) + shift


def _row_tile(t):
    return min(t, 1024)


def _ln_mm(x, g, scale, shift, w, tn):
    t, n = x.shape[0], w.shape[1]
    tm = _row_tile(t)
    vec = pl.BlockSpec((1, D), lambda i, j: (0, 0))

    def body(x_ref, g_ref, sc_ref, sh_ref, w_ref, u_ref, h_ref):
        @pl.when(pl.program_id(1) == 0)
        def _():
            h_ref[...] = _ln_mod(x_ref[...], g_ref[...], sc_ref[...], sh_ref[...]).astype(BF16)

        u_ref[...] = _dot(h_ref[...], w_ref[...], 1, 0)

    return pl.pallas_call(
        body, name="ln_mm", grid=(t // tm, n // tn),
        in_specs=[pl.BlockSpec((tm, D), lambda i, j: (i, 0)), vec, vec, vec,
                  pl.BlockSpec((D, tn), lambda i, j: (0, j))],
        out_specs=[pl.BlockSpec((tm, tn), lambda i, j: (i, j)), pl.BlockSpec((tm, D), lambda i, j: (i, 0))],
        out_shape=[jax.ShapeDtypeStruct((t, n), F32), jax.ShapeDtypeStruct((t, D), BF16)],
        compiler_params=_cparams(("arbitrary", "arbitrary")),
    )(x, g, scale, shift, w)


def _mm_res(p, w, x, gate):
    t, k = p.shape
    tm = _row_tile(t)

    def body(p_ref, w_ref, x_ref, gate_ref, o_ref):
        o_ref[...] = x_ref[...] + gate_ref[...] * _dot(p_ref[...], w_ref[...], 1, 0)

    return pl.pallas_call(
        body, name="mm_res", grid=(t // tm,),
        in_specs=[pl.BlockSpec((tm, k), lambda i: (i, 0)), pl.BlockSpec((k, D), lambda i: (0, 0)),
                  pl.BlockSpec((tm, D), lambda i: (i, 0)), pl.BlockSpec((1, D), lambda i: (0, 0))],
        out_specs=pl.BlockSpec((tm, D), lambda i: (i, 0)),
        out_shape=jax.ShapeDtypeStruct((t, D), F32),
        compiler_params=_cparams(("arbitrary",)),
    )(p, w, x, gate)


def _loss_grad(x, target):
    t = x.shape[0]
    tm = _row_tile(t)

    def body(x_ref, t_ref, l_ref, dx_ref):
        @pl.when(pl.program_id(0) == 0)
        def _():
            l_ref[...] = jnp.zeros_like(l_ref)

        err = x_ref[...] - t_ref[...]
        dx_ref[...] = err * (1.0 / D)
        l_ref[...] += 0.5 * jnp.sum(jnp.mean(err * err, axis=1, keepdims=True), axis=0, keepdims=True)

    return pl.pallas_call(
        body, name="loss_grad", grid=(t // tm,),
        in_specs=[pl.BlockSpec((tm, D), lambda i: (i, 0))] * 2,
        out_specs=[pl.BlockSpec((8, 128), lambda i: (0, 0)), pl.BlockSpec((tm, D), lambda i: (i, 0))],
        out_shape=[jax.ShapeDtypeStruct((8, 128), F32), jax.ShapeDtypeStruct((t, D), F32)],
        compiler_params=_cparams(("arbitrary",)),
    )(x, target)


def _mm_scaled(a, s, w, tn):
    t, k = a.shape
    n = w.shape[1]
    tm = _row_tile(t)

    def body(a_ref, s_ref, w_ref, o_ref):
        o_ref[...] = _dot((a_ref[...] * s_ref[...]).astype(BF16), w_ref[...], 1, 0).astype(BF16)

    return pl.pallas_call(
        body, name="mm_scaled", grid=(t // tm, n // tn),
        in_specs=[pl.BlockSpec((tm, k), lambda i, j: (i, 0)), pl.BlockSpec((1, k), lambda i, j: (0, 0)),
                  pl.BlockSpec((k, tn), lambda i, j: (0, j))],
        out_specs=pl.BlockSpec((tm, tn), lambda i, j: (i, j)),
        out_shape=jax.ShapeDtypeStruct((t, n), BF16),
        compiler_params=_cparams(("arbitrary", "arbitrary")),
    )(a, s, w)


def _mm_tn_acc(a, b, tn):
    t, m = a.shape
    n = b.shape[1]
    tk = min(t, 512)
    nk = t // tk

    def body(a_ref, b_ref, o_ref):
        @pl.when(pl.program_id(1) == 0)
        def _():
            o_ref[...] = jnp.zeros_like(o_ref)

        o_ref[...] += _dot(a_ref[...], b_ref[...].astype(BF16), 0, 0)

    return pl.pallas_call(
        body, name="mm_tn_acc", grid=(n // tn, nk),
        in_specs=[pl.BlockSpec((tk, m), lambda j, k: (k, 0)), pl.BlockSpec((tk, tn), lambda j, k: (k, j))],
        out_specs=pl.BlockSpec((m, tn), lambda j, k: (0, j)),
        out_shape=jax.ShapeDtypeStruct((m, n), F32),
        compiler_params=_cparams(("arbitrary", "arbitrary")),
    )(a, b)


def _inproj_bwd(du, wt, x, dxp, g, scale, shift, tk):
    t, kdim = du.shape
    tm = min(t, 512)
    nk = kdim // tk
    vec = pl.BlockSpec((1, D), lambda i, k: (0, 0))

    def body(du_ref, wt_ref, x_ref, dxp_ref, g_ref, sc_ref, sh_ref, dx_ref, dv_ref, acc_ref):
        k = pl.program_id(1)

        @pl.when((pl.program_id(0) == 0) & (k == 0))
        def _():
            dv_ref[...] = jnp.zeros_like(dv_ref)

        @pl.when(k == 0)
        def _():
            acc_ref[...] = jnp.zeros_like(acc_ref)

        acc_ref[...] += _dot(du_ref[...].astype(BF16), wt_ref[...], 1, 0)

        @pl.when(k == nk - 1)
        def _():
            _, vjp = jax.vjp(_ln_mod, x_ref[...], g_ref[...], sc_ref[...], sh_ref[...])
            dx, dg, dsc, dsh = vjp(acc_ref[...])
            dx_ref[...] = dxp_ref[...] + dx
            dv_ref[0:1, :] += dg
            dv_ref[1:2, :] += dsc
            dv_ref[2:3, :] += dsh

    return pl.pallas_call(
        body, name="inproj_bwd", grid=(t // tm, nk),
        in_specs=[pl.BlockSpec((tm, tk), lambda i, k: (i, k)), pl.BlockSpec((tk, D), lambda i, k: (k, 0)),
                  pl.BlockSpec((tm, D), lambda i, k: (i, 0)), pl.BlockSpec((tm, D), lambda i, k: (i, 0)),
                  vec, vec, vec],
        out_specs=[pl.BlockSpec((tm, D), lambda i, k: (i, 0)), pl.BlockSpec((8, D), lambda i, k: (0, 0))],
        out_shape=[jax.ShapeDtypeStruct((t, D), F32), jax.ShapeDtypeStruct((8, D), F32)],
        scratch_shapes=[pltpu.VMEM((tm, D), F32)],
        compiler_params=_cparams(("arbitrary", "arbitrary")),
    )(du, wt, x, dxp, g, scale, shift)


def _outgrad(gmat, w, gate):
    k = gmat.shape[0]
    tr = 256

    def body(g_ref, w_ref, gate_ref, dw_ref, dg_ref):
        @pl.when(pl.program_id(0) == 0)
        def _():
            dg_ref[...] = jnp.zeros_like(dg_ref)

        gm = g_ref[...]
        dw_ref[...] = gm * gate_ref[...]
        dg_ref[0:1, :] += jnp.sum(gm * w_ref[...].astype(F32), axis=0, keepdims=True)

    return pl.pallas_call(
        body, name="outgrad", grid=(k // tr,),
        in_specs=[pl.BlockSpec((tr, D), lambda i: (i, 0)), pl.BlockSpec((tr, D), lambda i: (i, 0)),
                  pl.BlockSpec((1, D), lambda i: (0, 0))],
        out_specs=[pl.BlockSpec((tr, D), lambda i: (i, 0)), pl.BlockSpec((8, D), lambda i: (0, 0))],
        out_shape=[jax.ShapeDtypeStruct((k, D), F32), jax.ShapeDtypeStruct((8, D), F32)],
        compiler_params=_cparams(("arbitrary",)),
    )(gmat, w, gate)


def _rope_table(pos, freq):
    t = pos.shape[0]
    tm = _row_tile(t)

    def body(p_ref, f_ref, o_ref):
        ang = p_ref[...].astype(F32) * f_ref[...]
        o_ref[:, 0:128] = jnp.cos(ang)
        o_ref[:, 128:256] = jnp.sin(ang)

    return pl.pallas_call(
        body, name="rope_table", grid=(t // tm,),
        in_specs=[pl.BlockSpec((tm, 1), lambda i: (i, 0)), pl.BlockSpec((1, 128), lambda i: (0, 0))],
        out_specs=pl.BlockSpec((tm, 256), lambda i: (i, 0)),
        out_shape=jax.ShapeDtypeStruct((t, 256), F32),
        compiler_params=_cparams(("arbitrary",)),
    )(pos, freq)


def _ada_fwd(c_all, w, b):
    nl, _, s = w.shape

    def body(c_ref, w_ref, b_ref, o_ref):
        o_ref[0] = _mm_f32(c_ref[...], w_ref[0]) + b_ref[0]

    return pl.pallas_call(
        body, name="ada_fwd", grid=(nl,),
        in_specs=[pl.BlockSpec((8, D), lambda l: (0, 0)), pl.BlockSpec((1, D, s), lambda l: (l, 0, 0)),
                  pl.BlockSpec((1, 1, s), lambda l: (l, 0, 0))],
        out_specs=pl.BlockSpec((1, 8, s), lambda l: (l, 0, 0)),
        out_shape=jax.ShapeDtypeStruct((nl, 8, s), F32),
        compiler_params=_cparams(("arbitrary",)),
    )(c_all, w, b)


def _ada_bwd(c_all, dmod_cols, dmod_all):
    nl, _, s = dmod_cols.shape

    def body(c_ref, dc_ref, da_ref, gw_ref, gb_ref):
        gw_ref[0] = _dot(c_ref[...], dc_ref[0], 0, 0, lax.Precision.HIGHEST)
        gb_ref[0] = jnp.sum(da_ref[0], axis=0, keepdims=True)

    return pl.pallas_call(
        body, name="ada_bwd", grid=(nl,),
        in_specs=[pl.BlockSpec((8, D), lambda l: (0, 0)), pl.BlockSpec((1, 8, s), lambda l: (l, 0, 0)),
                  pl.BlockSpec((1, 8, 3 * D), lambda l: (l, 0, 0))],
        out_specs=[pl.BlockSpec((1, D, s), lambda l: (l, 0, 0)), pl.BlockSpec((1, 1, 3 * D), lambda l: (l, 0, 0))],
        out_shape=[jax.ShapeDtypeStruct((nl, D, s), F32), jax.ShapeDtypeStruct((nl, 1, 3 * D), F32)],
        compiler_params=_cparams(("arbitrary",)),
    )(c_all, dmod_cols, dmod_all)


def _lb_fn(h8):
    sm = jax.nn.softmax(h8, axis=0)
    r = lax.broadcasted_iota(jnp.int32, (8, 8), 0)
    c = lax.broadcasted_iota(jnp.int32, (8, 8), 1)
    return _mm_f32(((c >= 1) & (c <= r)).astype(F32), sm)


def _lb_fwd(h8):
    def body(h_ref, o_ref):
        o_ref[...] = _lb_fn(h_ref[...])

    return pl.pallas_call(body, name="lb_fwd", out_shape=jax.ShapeDtypeStruct((8, D), F32))(h8)


def _lb_bwd(h8, dlb8):
    def body(h_ref, d_ref, o_ref):
        _, vjp = jax.vjp(_lb_fn, h_ref[...])
        o_ref[...] = vjp(d_ref[...])[0]

    return pl.pallas_call(body, name="lb_bwd", out_shape=jax.ShapeDtypeStruct((8, D), F32))(h8, dlb8)


ADAM_LR, ADAM_B1, ADAM_B2, ADAM_EPS, ADAM_WD, ADAM_STEP = 0.001, 0.9, 0.999, 1e-08, 0.01, 10


def _adamw(w, gparts, m, v):
    r, c = w.shape
    tr = r if r * c * 4 <= (1 << 20) else max(8, ((1 << 20) // (c * 4)) // 8 * 8)
    while r % tr:
        tr -= 8
    ng = len(gparts)

    def body(*refs):
        w_ref, m_ref, v_ref = refs[0], refs[1 + ng], refs[2 + ng]
        g_ref, d_ref, nm_ref, nv_ref = refs[3 + ng:]
        g = refs[1][...]
        for gr in refs[2:1 + ng]:
            g = g + gr[...]
        mm = ADAM_B1 * m_ref[...] + (1.0 - ADAM_B1) * g
        vv = ADAM_B2 * v_ref[...] + (1.0 - ADAM_B2) * (g * g)
        m_hat = mm / (1.0 - ADAM_B1 ** ADAM_STEP)
        v_hat = vv / (1.0 - ADAM_B2 ** ADAM_STEP)
        g_ref[...] = g
        d_ref[...] = -ADAM_LR * (m_hat / (jnp.sqrt(v_hat) + ADAM_EPS) + ADAM_WD * w_ref[...])
        nm_ref[...] = mm
        nv_ref[...] = vv

    spec = pl.BlockSpec((tr, c), lambda i: (i, 0))
    return pl.pallas_call(
        body, name="adamw", grid=(r // tr,), in_specs=[spec] * (3 + ng), out_specs=[spec] * 4,
        out_shape=[jax.ShapeDtypeStruct((r, c), F32)] * 4,
        compiler_params=_cparams(("arbitrary",)),
    )(w, *gparts, m, v)


def _sum_rows(parts):
    r, c = parts[0].shape
    tr = 8
    for cand in range(min(r, 512), 7, -8):
        if r % cand == 0:
            tr = cand
            break

    def body(*refs):
        acc = refs[0][...]
        for p in refs[1:-1]:
            acc = acc + p[...]
        refs[-1][...] = acc

    spec = pl.BlockSpec((tr, c), lambda i: (i, 0))
    return pl.pallas_call(
        body, name="sum_rows", grid=(r // tr,), in_specs=[spec] * len(parts), out_specs=spec,
        out_shape=jax.ShapeDtypeStruct((r, c), F32),
        compiler_params=_cparams(("arbitrary",)),
    )(*parts)


MESH = pl.DeviceIdType.MESH
ANY = pl.BlockSpec(memory_space=pl.ANY)


def _place():
    return lax.axis_index("x"), lax.axis_index("y"), lax.axis_index("c")


def _allgather8(blk):
    m_per, n = blk.shape

    def body(x_ref, out_ref, send_sems, recv_sems, local_sem):
        x, y, c = _place()
        me, sibling = (x, y, c), (x, y, 1 - c)
        chips = [(1 - x, y), (x, 1 - y), (1 - x, 1 - y)]

        def rows(px, py, pc):
            return out_ref.at[pl.ds((4 * px + 2 * py + pc) * m_per, m_per), :]

        def copy(k, block, to, src=None):
            return pltpu.make_async_remote_copy(
                src_ref=rows(*block) if src is None else src, dst_ref=rows(*block),
                send_sem=send_sems.at[k], recv_sem=recv_sems.at[k], device_id=to, device_id_type=MESH)

        mine = pltpu.make_async_copy(x_ref, rows(*me), local_sem)
        mine.start()
        first = [copy(0, me, sibling, src=x_ref)]
        first += [copy(1 + j, me, (*chip, c), src=x_ref) for j, chip in enumerate(chips)]
        for cp in first:
            cp.start()
        passed = [copy(4 + j, (*chip, c), sibling) for j, chip in enumerate(chips)]
        for j, chip in enumerate(chips):
            copy(1 + j, (*chip, c), me).wait_recv()
            passed[j].start()
        copy(0, sibling, me).wait_recv()
        for j, chip in enumerate(chips):
            copy(4 + j, (*chip, 1 - c), me).wait_recv()
        for cp in first + passed:
            cp.wait_send()
        mine.wait()

    return pl.pallas_call(
        body, name="allgather8",
        out_shape=jax.ShapeDtypeStruct((8 * m_per, n), blk.dtype),
        in_specs=[pl.BlockSpec(memory_space=pltpu.VMEM)],
        out_specs=pl.BlockSpec(memory_space=pltpu.VMEM),
        scratch_shapes=[pltpu.SemaphoreType.DMA((7,)), pltpu.SemaphoreType.DMA((7,)), pltpu.SemaphoreType.DMA],
    )(blk)


def _chip_peers():
    x, y, c = _place()
    return [(1 - x, y, c), (x, 1 - y, c), (1 - x, 1 - y, c)]


def _chip_allgather(shard):
    def body(x_ref, out_ref, send_sems, recv_sems, local_sem):
        x, y, _ = _place()
        peers = _chip_peers()

        def copy(j, chip_index):
            return pltpu.make_async_remote_copy(
                src_ref=x_ref, dst_ref=out_ref.at[chip_index], send_sem=send_sems.at[j], recv_sem=recv_sems.at[j],
                device_id=peers[j], device_id_type=MESH)

        mine = pltpu.make_async_copy(x_ref, out_ref.at[2 * x + y], local_sem)
        mine.start()
        sends = [copy(j, 2 * x + y) for j in range(3)]
        for cp in sends:
            cp.start()
        for j in range(3):
            copy(j, 2 * peers[j][0] + peers[j][1]).wait_recv()
        for cp in sends:
            cp.wait_send()
        mine.wait()

    return pl.pallas_call(
        body, name="chip_allgather",
        out_shape=jax.ShapeDtypeStruct((4,) + shard.shape, shard.dtype),
        in_specs=[ANY], out_specs=ANY,
        scratch_shapes=[pltpu.SemaphoreType.DMA((3,)), pltpu.SemaphoreType.DMA((3,)), pltpu.SemaphoreType.DMA],
    )(shard)


def _chip_scatter(parts):
    def body(p_ref, out_ref, send_sems, recv_sems):
        peers = _chip_peers()
        sends = [pltpu.make_async_remote_copy(
            src_ref=p_ref.at[2 * peers[j][0] + peers[j][1]], dst_ref=out_ref.at[j], send_sem=send_sems.at[j],
            recv_sem=recv_sems.at[j], device_id=peers[j], device_id_type=MESH) for j in range(3)]
        for cp in sends:
            cp.start()
        for cp in sends:
            cp.wait_recv()
        for cp in sends:
            cp.wait_send()

    return pl.pallas_call(
        body, name="chip_scatter",
        out_shape=jax.ShapeDtypeStruct((3,) + parts.shape[1:], parts.dtype),
        in_specs=[ANY], out_specs=ANY,
        scratch_shapes=[pltpu.SemaphoreType.DMA((3,)), pltpu.SemaphoreType.DMA((3,))],
    )(parts)


def _sibling_swap(a):
    def body(a_ref, out_ref, send_sem, recv_sem):
        x, y, c = _place()
        cp = pltpu.make_async_remote_copy(src_ref=a_ref, dst_ref=out_ref, send_sem=send_sem, recv_sem=recv_sem,
                                          device_id=(x, y, 1 - c), device_id_type=MESH)
        cp.start()
        cp.wait_recv()
        cp.wait_send()

    return pl.pallas_call(
        body, name="sibling_swap", out_shape=jax.ShapeDtypeStruct(a.shape, a.dtype),
        in_specs=[ANY], out_specs=ANY,
        scratch_shapes=[pltpu.SemaphoreType.DMA, pltpu.SemaphoreType.DMA],
    )(a)


WEIGHTS = ['hgrn_lb', 'ada_w', 'ada_b', 'norm_g', 'hg_in_w', 'hg_out_w', 'hg_onorm', 'sw_in_w', 'sw_out_w', 'sw_qnorm',
           'sw_knorm', 'sw_sinks', 'gd_in_w', 'gd_out_w', 'gd_conv_w', 'gd_a_log', 'gd_dt_bias', 'gd_onorm']
BIG = ['hg_in_w', 'hg_out_w', 'sw_in_w', 'sw_out_w', 'gd_in_w', 'gd_out_w']
PACK_ALIGN = 16
ROPE_THETA = 10000.0
ADA_S = 3 * D // 4
SMALL_ROW = {'hg_onorm': (0, 256), 'sw_qnorm': (256, 64), 'sw_knorm': (320, 64), 'sw_sinks': (384, 16),
             'gd_a_log': (400, 16), 'gd_dt_bias': (416, 16), 'gd_onorm': (432, 128)}


def _pack_rows(arrs):
    flat = jnp.concatenate([a.reshape(-1, D) for a in arrs], axis=0)
    return jnp.pad(flat, ((0, -flat.shape[0] % PACK_ALIGN), (0, 0)))


def _unpack_rows(packed, shapes):
    out, off = [], 0
    for s in shapes:
        rows = 1
        for d in s:
            rows *= d
        rows //= D
        out.append(packed[..., off:off + rows, :].reshape(packed.shape[:-2] + tuple(s)))
        off += rows
    return out


def _pack_small(vals):
    row = jnp.concatenate([vals[k].reshape(-1) for k in SMALL_ROW])
    row = jnp.pad(row, (0, D - row.shape[0]))[None]
    return jnp.concatenate([vals['hgrn_lb'], vals['norm_g'], vals['gd_conv_w'].reshape(16, D), row,
                            jnp.zeros((7, D), F32)], axis=0)


def _sw_cols(w, inverse=False):
    def split(a, heads):
        shp = (a.shape[0], 2, heads, 32) if inverse else (a.shape[0], heads, 2, 32)
        return a.reshape(shp).transpose(0, 2, 1, 3).reshape(a.shape[0], heads * 64)
    return jnp.concatenate([split(w[:, 0:1024], 16), split(w[:, 1024:1280], 4), w[:, 1280:]], axis=1)


def kernel(x, c, positions, hgrn_lb, ada_w, ada_b, norm_g, hg_in_w, hg_out_w, hg_onorm, sw_in_w, sw_out_w, sw_qnorm, sw_knorm, sw_sinks, gd_in_w, gd_out_w, gd_conv_w, gd_a_log, gd_dt_bias, gd_onorm, loss_target, m_hgrn_lb, m_ada_w, m_ada_b, m_norm_g, m_hg_in_w, m_hg_out_w, m_hg_onorm, m_sw_in_w, m_sw_out_w, m_sw_qnorm, m_sw_knorm, m_sw_sinks, m_gd_in_w, m_gd_out_w, m_gd_conv_w, m_gd_a_log, m_gd_dt_bias, m_gd_onorm, v_hgrn_lb, v_ada_w, v_ada_b, v_norm_g, v_hg_in_w, v_hg_out_w, v_hg_onorm, v_sw_in_w, v_sw_out_w, v_sw_qnorm, v_sw_knorm, v_sw_sinks, v_gd_in_w, v_gd_out_w, v_gd_conv_w, v_gd_a_log, v_gd_dt_bias, v_gd_onorm):
    w_in = dict(hgrn_lb=hgrn_lb, ada_w=ada_w, ada_b=ada_b, norm_g=norm_g, hg_in_w=hg_in_w, hg_out_w=hg_out_w,
                hg_onorm=hg_onorm, sw_in_w=sw_in_w, sw_out_w=sw_out_w, sw_qnorm=sw_qnorm, sw_knorm=sw_knorm,
                sw_sinks=sw_sinks, gd_in_w=gd_in_w, gd_out_w=gd_out_w, gd_conv_w=gd_conv_w, gd_a_log=gd_a_log,
                gd_dt_bias=gd_dt_bias, gd_onorm=gd_onorm)
    m_in = dict(zip(WEIGHTS, (m_hgrn_lb, m_ada_w, m_ada_b, m_norm_g, m_hg_in_w, m_hg_out_w, m_hg_onorm, m_sw_in_w,
                              m_sw_out_w, m_sw_qnorm, m_sw_knorm, m_sw_sinks, m_gd_in_w, m_gd_out_w, m_gd_conv_w,
                              m_gd_a_log, m_gd_dt_bias, m_gd_onorm)))
    v_in = dict(zip(WEIGHTS, (v_hgrn_lb, v_ada_w, v_ada_b, v_norm_g, v_hg_in_w, v_hg_out_w, v_hg_onorm, v_sw_in_w,
                              v_sw_out_w, v_sw_qnorm, v_sw_knorm, v_sw_sinks, v_gd_in_w, v_gd_out_w, v_gd_conv_w,
                              v_gd_a_log, v_gd_dt_bias, v_gd_onorm)))
    ax, ay, ac = _place()
    chip = 2 * ax + ay
    bidx = 4 * ax + 2 * ay + ac
    t = x.shape[1]
    x0, target = x[0], loss_target[0]

    c_all = _allgather8(jnp.pad(c, ((0, 7), (0, 0)))).reshape(8, 8, D)[:, 0, :]
    ada_b_cols = lax.dynamic_slice(ada_b, (0, chip * ADA_S), (4, ADA_S)).reshape(4, 1, ADA_S)
    mod_sh = _ada_fwd(c_all, ada_w, ada_b_cols)
    mod_g = _allgather8(mod_sh.reshape(32, ADA_S)).reshape(4, 2, 4, 8, ADA_S)[:, 0]
    mod = lax.dynamic_index_in_dim(mod_g, bidx, axis=2, keepdims=False).transpose(1, 0, 2).reshape(4, 3 * D)
    shift = [mod[l:l + 1, 0:D] for l in range(4)]
    scale = [mod[l:l + 1, D:2 * D] for l in range(4)]
    gate = [mod[l:l + 1, 2 * D:3 * D] for l in range(4)]

    h8 = jnp.concatenate([hgrn_lb, jnp.full((4, D), -1e30, F32)], axis=0)
    lb_all = _lb_fwd(h8)
    freq = ROPE_THETA ** (-jnp.arange(0, 64, 2, dtype=F32) / 64)
    cs = _rope_table(positions.reshape(t, 1), jnp.tile(freq, 4)[None])

    big_shapes = [w_in[k].shape for k in BIG]
    gathered = _chip_allgather(_pack_rows([w_in[k] for k in BIG]).astype(BF16))
    hg_in_k, hg_out_k, sw_in_k, sw_out_k, gd_in_k, gd_out_k = _unpack_rows(gathered, big_shapes)
    hg_in_f = hg_in_k.transpose(1, 2, 0, 3).reshape(2, D, 4 * D)
    hg_out_f = hg_out_k.transpose(1, 0, 2, 3).reshape(2, D, D)
    sw_in_f = _sw_cols(sw_in_k[:, 0].transpose(1, 0, 2).reshape(D, SW_N))
    sw_out_f = sw_out_k.reshape(D, D)
    gd_in_f = jnp.pad(gd_in_k[:, 0].transpose(1, 0, 2).reshape(D, 6176), ((0, 0), (0, GD_N - 6176)))
    gd_out_f = gd_out_k.reshape(GD_VW, D)
    win = [hg_in_f[0], sw_in_f, gd_in_f, hg_in_f[1]]
    wout = [hg_out_f[0], sw_out_f, gd_out_f, hg_out_f[1]]
    tn_in = [1024, 1280, 896, 1024]

    gq = jnp.stack([jnp.tile(sw_qnorm[0, :32], 16), jnp.tile(sw_qnorm[0, 32:], 16)])
    gk = jnp.stack([jnp.tile(sw_knorm[0, :32], 4), jnp.tile(sw_knorm[0, 32:], 4)])
    pad128 = lambda a: jnp.pad(a, ((0, 0), (0, HD - a.shape[1])))
    sinks, alog, dtb = pad128(sw_sinks), pad128(gd_a_log), pad128(gd_dt_bias)
    cw8 = jnp.pad(_chip_allgather(gd_conv_w[0]).transpose(1, 0, 2).reshape(4, GD_QKV), ((0, 4), (0, 0)))
    lbs = {0: lb_all[0:1], 3: lb_all[3:4]}

    xs, us, hs, ps, stss = [x0], [], [], [], []
    for l in range(4):
        u, h = _ln_mm(xs[l], norm_g[l:l + 1], scale[l], shift[l], win[l], tn_in[l])
        if l % 3 == 0:
            p, sts = _hg_fwd(u, lbs[l], hg_onorm[l // 3:l // 3 + 1])
        elif l % 3 == 1:
            p, sts = _sw_fwd(u, cs, gq, gk, sinks), None
        else:
            p, sts = _gd_fwd(u, cw8, alog, dtb, gd_onorm)
        xs.append(_mm_res(p, wout[l], xs[l], gate[l]))
        us.append(u), hs.append(h), ps.append(p), stss.append(sts)
    lpart, dx = _loss_grad(xs[4], target)
    loss = lax.psum(lpart[0, 0], ("x", "y", "c"))

    g_small = {}
    d_in, d_out, dmod, dnorm_g, dlb8, dgo_hg = [None] * 4, [None] * 4, [None] * 4, [None] * 4, jnp.zeros((8, D), F32), {}
    for l in (3, 2, 1, 0):
        dp = _mm_scaled(dx, gate[l], wout[l].T, 1024)
        d_out[l], dgate = _outgrad(_mm_tn_acc(ps[l], dx, 512), wout[l], gate[l])
        if l % 3 == 0:
            du, dlb, dgo_hg[l // 3] = _hg_bwd(us[l], stss[l], dp, lbs[l], hg_onorm[l // 3:l // 3 + 1])
            dlb8 = lax.dynamic_update_slice(dlb8, dlb, (l, 0))
        elif l % 3 == 1:
            du, dgq, dgk, dsk = _sw_bwd(us[l], cs, dp, gq, gk, sinks)
            g_small['sw_qnorm'] = jnp.concatenate([dgq[0].reshape(16, 32).sum(0), dgq[1].reshape(16, 32).sum(0)])
            g_small['sw_knorm'] = jnp.concatenate([dgk[0].reshape(4, 32).sum(0), dgk[1].reshape(4, 32).sum(0)])
            g_small['sw_sinks'] = dsk[0, :16]
        else:
            du, dcw, dalog, ddtb, g_small['gd_onorm'] = _gd_bwd(us[l], stss[l], dp, cw8, alog, dtb, gd_onorm)
            g_small['gd_conv_w'], g_small['gd_a_log'], g_small['gd_dt_bias'] = dcw[:4], dalog[0, :16], ddtb[0, :16]
        d_in[l] = _mm_tn_acc(hs[l], du, 896 if l == 2 else 512)
        dx, dvec = _inproj_bwd(du, win[l].T, xs[l], dx, norm_g[l:l + 1], scale[l], shift[l], tn_in[l])
        dnorm_g[l] = dvec[0:1]
        dmod[l] = jnp.concatenate([dvec[2:3], dvec[1:2], dgate[0:1]], axis=1)
    grad_x = dx[None]

    g_small['hgrn_lb'] = _lb_bwd(h8, dlb8)[0:4]
    g_small['norm_g'] = jnp.concatenate(dnorm_g, axis=0)
    g_small['hg_onorm'] = jnp.concatenate([dgo_hg[0], dgo_hg[1]], axis=0)
    gs_all = _allgather8(_pack_small(g_small))
    gs = _sum_rows([gs_all[32 * d:32 * (d + 1)] for d in range(8)])

    def small_view(packed, k):
        if k == 'hgrn_lb':
            return packed[0:4]
        if k == 'norm_g':
            return packed[4:8]
        off, size = SMALL_ROW[k]
        return packed[24, off:off + size].reshape(w_in[k].shape)

    conv_sl = lambda full: lax.dynamic_slice(full.reshape(4, GD_QKV), (0, chip * D), (4, D))
    out = {}

    def put(k, res, shape):
        for name, r in zip(('grad_', 'delta_', 'new_m_', 'new_v_'), res):
            out[name + k] = r.reshape(shape)

    zero_conv = dict(gd_conv_w=jnp.zeros((4, GD_QKV), F32))
    small_names = ['hgrn_lb', 'norm_g'] + list(SMALL_ROW)
    res = _adamw(_pack_small({**{k: w_in[k] for k in small_names}, **zero_conv}), (gs,),
                 _pack_small({**{k: m_in[k] for k in small_names}, **zero_conv}),
                 _pack_small({**{k: v_in[k] for k in small_names}, **zero_conv}))
    for k in small_names:
        put(k, [small_view(r, k) for r in res], w_in[k].shape)
    put('gd_conv_w', _adamw(gd_conv_w[0], (conv_sl(gs[8:24]),), m_in['gd_conv_w'][0], v_in['gd_conv_w'][0]),
        gd_conv_w.shape)

    dm = _allgather8(jnp.pad(jnp.concatenate(dmod, axis=0), ((0, 4), (0, 0)))).reshape(8, 8, 3 * D)[:, :4]
    dm = dm.transpose(1, 0, 2)
    g_ada_w, g_ada_b = _ada_bwd(c_all, lax.dynamic_slice(dm, (0, 0, chip * ADA_S), (4, 8, ADA_S)), dm)
    put('ada_w', _adamw(ada_w.reshape(4 * D, ADA_S), (g_ada_w.reshape(4 * D, ADA_S),),
                        m_in['ada_w'].reshape(4 * D, ADA_S), v_in['ada_w'].reshape(4 * D, ADA_S)), ada_w.shape)
    put('ada_b', _adamw(ada_b, (g_ada_b.reshape(4, 3 * D),), m_in['ada_b'], v_in['ada_b']), ada_b.shape)

    by_chip = lambda g, cols: g.reshape(g.shape[0], 4, cols).transpose(1, 0, 2)
    d_sw_in = _sw_cols(d_in[1], inverse=True)
    parts = {
        'hg_in_w': jnp.stack([by_chip(d_in[0], D), by_chip(d_in[3], D)], axis=1),
        'hg_out_w': jnp.stack([d_out[0].reshape(4, D // 4, D), d_out[3].reshape(4, D // 4, D)], axis=1),
        'sw_in_w': by_chip(d_sw_in, SW_N // 4)[:, None],
        'sw_out_w': d_out[1].reshape(4, 1, D // 4, D),
        'gd_in_w': by_chip(d_in[2][:, :6176], 1544)[:, None],
        'gd_out_w': d_out[2].reshape(4, 1, GD_VW // 4, D),
    }
    packed = jnp.stack([_pack_rows([parts[k][j] for k in BIG]) for j in range(4)])
    recv = _chip_scatter(packed)
    own = lax.dynamic_index_in_dim(packed, chip, axis=0, keepdims=False)
    half = _sum_rows([own, recv[0], recv[1], recv[2]])
    other = _sibling_swap(half)
    res = _adamw(_pack_rows([w_in[k] for k in BIG]), (half, other), _pack_rows([m_in[k] for k in BIG]),
                 _pack_rows([v_in[k] for k in BIG]))
    for name, r in zip(('grad_', 'delta_', 'new_m_', 'new_v_'), res):
        for k, a in zip(BIG, _unpack_rows(r, big_shapes)):
            out[name + k] = a

    return (loss, grad_x, *[out[p + k] for p in ('grad_', 'delta_', 'new_m_', 'new_v_') for k in WEIGHTS])
```

```python
import functools

import jax
import jax.numpy as jnp
from jax import lax
from jax.experimental import pallas as pl
from jax.experimental.pallas import tpu as pltpu

F32 = jnp.float32
BF16 = jnp.bfloat16
D = 1024
EPS = 1e-6
CHUNK = 64
SUB = 16
HG_H = 8
HD = 128
VMEM_LIMIT = 56 * 1024 * 1024


def _cparams(sem=None):
    return pltpu.CompilerParams(dimension_semantics=sem, vmem_limit_bytes=VMEM_LIMIT)


def _dot(a, b, ca, cb, prec=None):
    return lax.dot_general(a, b, (((ca,), (cb,)), ((), ())), precision=prec, preferred_element_type=F32)


def _mm(a, b):
    return _dot(a.astype(BF16), b.astype(BF16), 1, 0)


def _mm_nt(a, b):
    return _dot(a.astype(BF16), b.astype(BF16), 1, 1)


def _mm_tn(a, b):
    return _dot(a.astype(BF16), b.astype(BF16), 0, 0)


def _mm_f32(a, b):
    return _dot(a, b, 1, 0, lax.Precision.HIGHEST)


def _silu(x):
    return x * jax.nn.sigmoid(x)


def _hg_chunk(q_raw, f_pre, v, z, st, lb, go):
    c = q_raw.shape[0]
    nsub = c // SUB
    lf = jnp.log(lb + (1.0 - lb) * jax.nn.sigmoid(f_pre))
    k = (1.0 - lb) * jax.nn.sigmoid(-f_pre)
    q = _silu(q_raw)
    ti = lax.broadcasted_iota(jnp.int32, (c, c), 0)
    si = lax.broadcasted_iota(jnp.int32, (c, c), 1)
    mats = [(si <= ti).astype(F32)] + [(si <= SUB * i + SUB // 2).astype(F32) for i in range(nsub)]
    cums = jnp.split(_mm_f32(jnp.concatenate(mats, axis=0), lf), nsub + 1, axis=0)
    b, bmid = cums[0], cums[1:]
    row = lax.broadcasted_iota(jnp.int32, (c, 1), 0)
    ref = sum(jnp.where((row >= SUB * i) & (row < SUB * (i + 1)), bmid[i], 0.0) for i in range(nsub))
    qt = q * jnp.exp(b - ref)
    kall = jnp.concatenate(
        [k * jnp.exp(jnp.where(row < SUB * (i + 1), bmid[i] - b, -jnp.inf)) for i in range(nsub)], axis=0)
    v4 = jnp.concatenate([v] * nsub, axis=0)
    b_last = jnp.sum(lf, axis=0, keepdims=True)
    qb = q * jnp.exp(b)
    kd = k * jnp.exp(b_last - b)
    e_last = jnp.exp(b_last)
    tq = lax.broadcasted_iota(jnp.int32, (c, nsub * c), 0)
    cq = lax.broadcasted_iota(jnp.int32, (c, nsub * c), 1)
    m_all = ((cq // c) == (tq // SUB)) & ((cq % c) <= tq)
    hs = lambda a: jnp.split(a, HG_H, axis=1)
    qt_h, kall_h, v4_h, qb_h, kd_h, v_h, z_h, el_h = map(hs, (qt, kall, v4, qb, kd, v, z, e_last))
    st_h = jnp.split(st, HG_H, axis=0)
    p_out, st_out = [], []
    for h in range(HG_H):
        pm = jnp.where(m_all, _mm_nt(qt_h[h], kall_h[h]), 0.0)
        o = _mm(pm, v4_h[h]) + _mm_nt(qb_h[h], st_h[h])
        st_out.append(el_h[h] * st_h[h] + _mm_tn(v_h[h], kd_h[h]))
        y = o * lax.rsqrt(jnp.mean(o * o, axis=1, keepdims=True) + EPS) * go
        p_out.append(y * _silu(z_h[h]))
    return jnp.concatenate(p_out, axis=1), jnp.concatenate(st_out, axis=0)


def _hg_fwd(u, lb, go):
    t = u.shape[0]
    n = t // CHUNK

    def body(u_ref, lb_ref, go_ref, p_ref, sts_ref, st_ref):
        @pl.when(pl.program_id(0) == 0)
        def _():
            st_ref[...] = jnp.zeros_like(st_ref)

        st = st_ref[...]
        sts_ref[0] = st
        p, st_next = _hg_chunk(u_ref[:, 0:D], u_ref[:, D:2 * D], u_ref[:, 2 * D:3 * D], u_ref[:, 3 * D:4 * D],
                               st, lb_ref[...], go_ref[...])
        p_ref[...] = p.astype(BF16)
        st_ref[...] = st_next

    return pl.pallas_call(
        body, name="hg_fwd", grid=(n,),
        in_specs=[pl.BlockSpec((CHUNK, 4 * D), lambda i: (i, 0)),
                  pl.BlockSpec((1, D), lambda i: (0, 0)),
                  pl.BlockSpec((1, HD), lambda i: (0, 0))],
        out_specs=[pl.BlockSpec((CHUNK, D), lambda i: (i, 0)),
                   pl.BlockSpec((1, HG_H * HD, HD), lambda i: (i, 0, 0))],
        out_shape=[jax.ShapeDtypeStruct((t, D), BF16), jax.ShapeDtypeStruct((n, HG_H * HD, HD), F32)],
        scratch_shapes=[pltpu.VMEM((HG_H * HD, HD), F32)],
        compiler_params=_cparams(("arbitrary",)),
    )(u, lb, go)


def _hg_bwd(u, sts, dp, lb, go):
    t = u.shape[0]
    n = t // CHUNK

    def body(u_ref, sts_ref, dp_ref, lb_ref, go_ref, du_ref, dlb_ref, dgo_ref, dst_ref):
        @pl.when(pl.program_id(0) == 0)
        def _():
            dst_ref[...] = jnp.zeros_like(dst_ref)
            dlb_ref[...] = jnp.zeros_like(dlb_ref)
            dgo_ref[...] = jnp.zeros_like(dgo_ref)

        _, vjp = jax.vjp(_hg_chunk, u_ref[:, 0:D], u_ref[:, D:2 * D], u_ref[:, 2 * D:3 * D], u_ref[:, 3 * D:4 * D],
                         sts_ref[0], lb_ref[...], go_ref[...])
        dq, df, dv, dz, dst, dlb, dgo = vjp((dp_ref[...].astype(F32), dst_ref[...]))
        du_ref[:, 0:D] = dq
        du_ref[:, D:2 * D] = df
        du_ref[:, 2 * D:3 * D] = dv
        du_ref[:, 3 * D:4 * D] = dz
        dst_ref[...] = dst
        dlb_ref[...] += dlb
        dgo_ref[...] += dgo

    rev = lambda i: (n - 1 - i, 0)
    return pl.pallas_call(
        body, name="hg_bwd", grid=(n,),
        in_specs=[pl.BlockSpec((CHUNK, 4 * D), rev),
                  pl.BlockSpec((1, HG_H * HD, HD), lambda i: (n - 1 - i, 0, 0)),
                  pl.BlockSpec((CHUNK, D), rev),
                  pl.BlockSpec((1, D), lambda i: (0, 0)),
                  pl.BlockSpec((1, HD), lambda i: (0, 0))],
        out_specs=[pl.BlockSpec((CHUNK, 4 * D), rev),
                   pl.BlockSpec((1, D), lambda i: (0, 0)),
                   pl.BlockSpec((1, HD), lambda i: (0, 0))],
        out_shape=[jax.ShapeDtypeStruct((t, 4 * D), F32), jax.ShapeDtypeStruct((1, D), F32),
                   jax.ShapeDtypeStruct((1, HD), F32)],
        scratch_shapes=[pltpu.VMEM((HG_H * HD, HD), F32)],
        compiler_params=_cparams(("arbitrary",)),
    )(u, sts, dp, lb, go)


GD_VH = 16
GD_QKH = 8
GD_QKV = 4096
GD_VW = 2048
GD_N = GD_QKV + GD_VW + HD
GD_GRP = 4
HALO = 8


def _mm_high(a, b):
    return _dot(a, b, 1, 0, lax.Precision.HIGH)


def _lane_pick(a, h):
    lane = lax.broadcasted_iota(jnp.int32, a.shape, 1)
    return jnp.sum(jnp.where(lane == h, a, 0.0), axis=1, keepdims=True)


def _l2n(x):
    return x * lax.rsqrt(jnp.sum(x * x, axis=1, keepdims=True) + EPS)


def _gd_chunk(xh, x, z, ab, st, cw, alog, dtb, go):
    c = x.shape[0]
    xa = jnp.concatenate([xh, x], axis=0)
    ti = lax.broadcasted_iota(jnp.int32, (3 * c, c + HALO), 0)
    si = lax.broadcasted_iota(jnp.int32, (3 * c, c + HALO), 1)
    shift = (si == (ti % c) + (ti // c) + HALO - 3).astype(F32)
    sh = jnp.split(_mm_high(shift, xa), 3, axis=0)
    qkv = _silu(cw[0:1] * sh[0] + cw[1:2] * sh[1] + cw[2:3] * sh[2] + cw[3:4] * x)
    q_all, k_all, v_all = jnp.split(qkv, [1024, 2048], axis=1)
    lane = lax.broadcasted_iota(jnp.int32, (c, HD), 1)
    a_part = jnp.where(lane < GD_VH, ab, 0.0)
    g_all = -jnp.exp(alog) * jax.nn.softplus(a_part + dtb)
    tri = (lax.broadcasted_iota(jnp.int32, (c, c), 1) <= lax.broadcasted_iota(jnp.int32, (c, c), 0))
    d_all = _mm_f32(tri.astype(F32), g_all)
    dl_all = jnp.sum(g_all, axis=0, keepdims=True)
    beta_all = jax.nn.sigmoid(ab)
    gc = GD_GRP * c
    r_i, c_i = lax.broadcasted_iota(jnp.int32, (gc, gc), 0), lax.broadcasted_iota(jnp.int32, (gc, gc), 1)
    blk = {nb: (r_i // nb) == (c_i // nb) for nb in (8, 16, 32, c)}
    tri_g, strict_g = blk[c] & (c_i <= r_i), blk[c] & (c_i < r_i)
    eye = (r_i == c_i).astype(F32)
    ones = jnp.ones((gc, HD), F32)
    lane_g = lax.broadcasted_iota(jnp.int32, (gc, HD), 1)
    qs = jnp.split(q_all, GD_QKH, axis=1)
    ks = jnp.split(k_all, GD_QKH, axis=1)
    vs = jnp.split(v_all, GD_VH, axis=1)
    zs = jnp.split(z, GD_VH, axis=1)
    sts = jnp.split(st, GD_VH, axis=0)
    qn = [_l2n(a) * (HD ** -0.5) for a in qs]
    kn = [_l2n(a) for a in ks]
    p_out, st_out = [], []
    for g in range(GD_VH // GD_GRP):
        heads = range(GD_GRP * g, GD_GRP * (g + 1))
        stack = lambda f: jnp.concatenate([f(h) for h in heads], axis=0)
        q_, k_, v_ = stack(lambda h: qn[h // 2]), stack(lambda h: kn[h // 2]), stack(lambda h: vs[h])
        dcol = stack(lambda h: _lane_pick(d_all, h))
        bcol = stack(lambda h: _lane_pick(beta_all, GD_VH + h))
        dlast = stack(lambda h: jnp.broadcast_to(_lane_pick(dl_all, h), (c, 1)))
        drow = _dot(ones, jnp.where(lane_g == 0, dcol, 0.0), 1, 1, lax.Precision.HIGHEST)
        dec = jnp.exp(jnp.where(tri_g, dcol - drow, -jnp.inf))
        kb = k_ * bcol
        a_mat = jnp.where(strict_g, _mm_nt(kb, k_) * dec, 0.0)
        d0 = jnp.where(blk[8], a_mat, 0.0)
        d2 = _mm_high(d0, d0)
        tinv = eye - d0
        tinv = tinv + _mm_high(tinv, d2)
        tinv = tinv + _mm_high(tinv, _mm_high(d2, d2))
        for nb in (16, 32, c):
            low = jnp.where(blk[nb] & ~blk[nb // 2], a_mat, 0.0)
            tinv = tinv - _mm_high(_mm_high(tinv, low), tinv)
        xsol = _mm_high(tinv, jnp.concatenate([v_ * bcol, kb * jnp.exp(dcol)], axis=1))
        u_, w_ = jnp.split(xsol, 2, axis=1)
        w_h = jnp.split(w_, GD_GRP, axis=0)
        v_new = u_ - jnp.concatenate([_mm(w_h[i], sts[h]) for i, h in enumerate(heads)], axis=0)
        qd_h = jnp.split(q_ * jnp.exp(dcol), GD_GRP, axis=0)
        o_g = _mm(_mm_nt(q_, k_) * dec, v_new) + jnp.concatenate(
            [_mm(qd_h[i], sts[h]) for i, h in enumerate(heads)], axis=0)
        kd_h = jnp.split(k_ * jnp.exp(dlast - dcol), GD_GRP, axis=0)
        vn_h = jnp.split(v_new, GD_GRP, axis=0)
        o_h = jnp.split(o_g, GD_GRP, axis=0)
        for i, h in enumerate(heads):
            st_out.append(sts[h] * jnp.exp(_lane_pick(dl_all, h)) + _mm_tn(kd_h[i], vn_h[i]))
            o = o_h[i]
            y = o * lax.rsqrt(jnp.mean(o * o, axis=1, keepdims=True) + EPS) * go
            p_out.append(y * _silu(zs[h]))
    return jnp.concatenate(p_out, axis=1), jnp.concatenate(st_out, axis=0)


def _gd_specs(n, rev):
    ci = (lambda i: n - 1 - i) if rev else (lambda i: i)
    return [pl.BlockSpec((HALO, GD_QKV), lambda i: (jnp.maximum(ci(i) * (CHUNK // HALO) - 1, 0), 0)),
            pl.BlockSpec((CHUNK, GD_N), lambda i: (ci(i), 0))]


def _gd_load(uh_ref, u_ref, first):
    xh = jnp.where(first, 0.0, uh_ref[...])
    return xh, u_ref[:, 0:GD_QKV], u_ref[:, GD_QKV:GD_QKV + GD_VW], u_ref[:, GD_QKV + GD_VW:GD_N]


def _gd_fwd(u, cw, alog, dtb, go):
    t = u.shape[0]
    n = t // CHUNK
    small = lambda r, w: pl.BlockSpec((r, w), lambda i: (0, 0))

    def body(uh_ref, u_ref, cw_ref, alog_ref, dtb_ref, go_ref, p_ref, sts_ref, st_ref):
        i = pl.program_id(0)

        @pl.when(i == 0)
        def _():
            st_ref[...] = jnp.zeros_like(st_ref)

        st = st_ref[...]
        sts_ref[0] = st
        p, st_next = _gd_chunk(*_gd_load(uh_ref, u_ref, i == 0), st, cw_ref[...], alog_ref[...], dtb_ref[...],
                               go_ref[...])
        p_ref[...] = p.astype(BF16)
        st_ref[...] = st_next

    return pl.pallas_call(
        body, name="gd_fwd", grid=(n,),
        in_specs=_gd_specs(n, False) + [small(8, GD_QKV), small(1, HD), small(1, HD), small(1, HD)],
        out_specs=[pl.BlockSpec((CHUNK, GD_VW), lambda i: (i, 0)),
                   pl.BlockSpec((1, GD_VH * HD, HD), lambda i: (i, 0, 0))],
        out_shape=[jax.ShapeDtypeStruct((t, GD_VW), BF16), jax.ShapeDtypeStruct((n, GD_VH * HD, HD), F32)],
        scratch_shapes=[pltpu.VMEM((GD_VH * HD, HD), F32)],
        compiler_params=_cparams(("arbitrary",)),
    )(u, u, cw, alog, dtb, go)


def _gd_bwd(u, sts, dp, cw, alog, dtb, go):
    t = u.shape[0]
    n = t // CHUNK
    small = lambda r, w: pl.BlockSpec((r, w), lambda i: (0, 0))

    def body(uh_ref, u_ref, sts_ref, dp_ref, cw_ref, alog_ref, dtb_ref, go_ref,
             du_ref, dcw_ref, dalog_ref, ddtb_ref, dgo_ref, dst_ref, dhalo_ref):
        i = pl.program_id(0)

        @pl.when(i == 0)
        def _():
            for r in (dst_ref, dhalo_ref, dcw_ref, dalog_ref, ddtb_ref, dgo_ref):
                r[...] = jnp.zeros_like(r)

        _, vjp = jax.vjp(_gd_chunk, *_gd_load(uh_ref, u_ref, i == n - 1), sts_ref[0], cw_ref[...], alog_ref[...],
                         dtb_ref[...], go_ref[...])
        dxh, dx, dz, dab, dst, dcw, dalog, ddtb, dgo = vjp((dp_ref[...].astype(F32), dst_ref[...]))
        du_ref[:, 0:GD_QKV] = dx
        du_ref[CHUNK - HALO:CHUNK, 0:GD_QKV] += dhalo_ref[...]
        du_ref[:, GD_QKV:GD_QKV + GD_VW] = dz
        du_ref[:, GD_QKV + GD_VW:GD_N] = dab
        dhalo_ref[...] = dxh
        dst_ref[...] = dst
        dcw_ref[...] += dcw
        dalog_ref[...] += dalog
        ddtb_ref[...] += ddtb
        dgo_ref[...] += dgo

    return pl.pallas_call(
        body, name="gd_bwd", grid=(n,),
        in_specs=_gd_specs(n, True) + [pl.BlockSpec((1, GD_VH * HD, HD), lambda i: (n - 1 - i, 0, 0)),
                                       pl.BlockSpec((CHUNK, GD_VW), lambda i: (n - 1 - i, 0)),
                                       small(8, GD_QKV), small(1, HD), small(1, HD), small(1, HD)],
        out_specs=[pl.BlockSpec((CHUNK, GD_N), lambda i: (n - 1 - i, 0)),
                   small(8, GD_QKV), small(1, HD), small(1, HD), small(1, HD)],
        out_shape=[jax.ShapeDtypeStruct((t, GD_N), F32), jax.ShapeDtypeStruct((8, GD_QKV), F32)]
        + [jax.ShapeDtypeStruct((1, HD), F32)] * 3,
        scratch_shapes=[pltpu.VMEM((GD_VH * HD, HD), F32), pltpu.VMEM((HALO, GD_QKV), F32)],
        compiler_params=_cparams(("arbitrary",)),
    )(u, u, sts, dp, cw, alog, dtb, go)


SW_B = 128
SW_H = 16
SW_G = 4
SW_N = 2560
SW_KV0 = 1024


def _blockdiag(n, blk):
    r = lax.broadcasted_iota(jnp.int32, (n, n), 0) // blk
    c = lax.broadcasted_iota(jnp.int32, (n, n), 1) // blk
    return (r == c).astype(F32)


def _sw_normrope(x, g1, g2, cos, sin):
    w = x.shape[1] // 2
    x1, x2 = jnp.split(x, 2, axis=1)
    ms = _mm_high(x1 * x1 + x2 * x2, _blockdiag(w, 32)) * (1.0 / 64.0)
    rinv = lax.rsqrt(ms + EPS)
    n1, n2 = x1 * rinv * g1, x2 * rinv * g2
    return jnp.concatenate([n1 * cos - n2 * sin, n2 * cos + n1 * sin], axis=1)


def _sw_block(q, kvp, kvc, z, csp, csc, gq, gk, sinks, has_prev):
    b = q.shape[0]
    cos_c, sin_c = jnp.split(csc, 2, axis=1)
    cos_p, sin_p = jnp.split(csp, 2, axis=1)
    tile4 = lambda a: jnp.concatenate([a] * 4, axis=1)
    qh = _sw_normrope(q, gq[0:1], gq[1:2], tile4(cos_c), tile4(sin_c))
    kp, vp = jnp.split(kvp, 2, axis=1)
    kc, vc = jnp.split(kvc, 2, axis=1)
    kh = jnp.concatenate([_sw_normrope(kp, gk[0:1], gk[1:2], cos_p, sin_p),
                          _sw_normrope(kc, gk[0:1], gk[1:2], cos_c, sin_c)], axis=0)
    vv = jnp.concatenate([vp, vc], axis=0)
    q1, q2 = jnp.split(qh, 2, axis=1)
    q1g, q2g = jnp.split(q1, SW_G, axis=1), jnp.split(q2, SW_G, axis=1)
    qi = lax.broadcasted_iota(jnp.int32, (b, 2 * b), 0)
    kj = lax.broadcasted_iota(jnp.int32, (b, 2 * b), 1)
    rel = qi + b - kj
    mask = (rel >= 0) & (rel < SW_B) & (has_prev | (kj >= b))
    ri = lax.broadcasted_iota(jnp.int32, (256, 256), 0)
    ci = lax.broadcasted_iota(jnp.int32, (256, 256), 1)
    lane256 = lax.broadcasted_iota(jnp.int32, (1, 256), 1)
    o_out = []
    for g in range(SW_G):
        ek = ((ri // 128 == ci // 128) & ((ri % 128) // 32 == g) & (ri % 32 == ci % 32)).astype(F32)
        ev = ((ri // 64 == g) & (ri % 64 == ci % 64)).astype(F32)
        kx = _mm(kh, ek)
        vx = _mm(vv, ev)
        qg = jnp.concatenate([q1g[g], q2g[g]], axis=1)
        og = jnp.zeros((b, 256), F32)
        for j in range(4):
            hmask = (lane256 % 128) // 32 == j
            s = _mm_nt(jnp.where(hmask, qg, 0.0), kx) * (64 ** -0.5)
            s = jnp.where(mask, s, -jnp.inf)
            sink = _lane_pick(sinks, 4 * g + j)
            m = jnp.maximum(jnp.max(s, axis=1, keepdims=True), sink)
            p = jnp.exp(s - m)
            pn = p / (jnp.sum(p, axis=1, keepdims=True) + jnp.exp(sink - m))
            og = og + jnp.where(lane256 // 64 == j, _mm(pn, vx), 0.0)
        o_out.append(og)
    return jnp.concatenate(o_out, axis=1) * _silu(z)


def _sw_specs(n, rev):
    ci = (lambda i: n - 1 - i) if rev else (lambda i: i)
    prev = lambda i: jnp.maximum(ci(i) - 1, 0)
    return [pl.BlockSpec((SW_B, SW_N), lambda i: (ci(i), 0)),
            pl.BlockSpec((SW_B, 512), lambda i: (prev(i), SW_KV0 // 512)),
            pl.BlockSpec((SW_B, 256), lambda i: (ci(i), 0)),
            pl.BlockSpec((SW_B, 256), lambda i: (prev(i), 0)),
            pl.BlockSpec((2, 512), lambda i: (0, 0)), pl.BlockSpec((2, 128), lambda i: (0, 0)),
            pl.BlockSpec((1, 128), lambda i: (0, 0))]


def _sw_args(u_ref, kvp_ref, csc_ref, csp_ref, gq_ref, gk_ref, sk_ref, has_prev):
    return (u_ref[:, 0:D], kvp_ref[...], u_ref[:, SW_KV0:SW_KV0 + 512], u_ref[:, SW_KV0 + 512:SW_N],
            csp_ref[...], csc_ref[...], gq_ref[...], gk_ref[...], sk_ref[...], has_prev)


def _sw_fwd(u, cs, gq, gk, sinks):
    t = u.shape[0]
    n = t // SW_B

    def body(u_ref, kvp_ref, csc_ref, csp_ref, gq_ref, gk_ref, sk_ref, p_ref):
        has_prev = pl.program_id(0) > 0
        p_ref[...] = _sw_block(*_sw_args(u_ref, kvp_ref, csc_ref, csp_ref, gq_ref, gk_ref, sk_ref, has_prev)
                               ).astype(BF16)

    return pl.pallas_call(
        body, name="sw_fwd", grid=(n,), in_specs=_sw_specs(n, False),
        out_specs=pl.BlockSpec((SW_B, D), lambda i: (i, 0)),
        out_shape=jax.ShapeDtypeStruct((t, D), BF16),
        compiler_params=_cparams(("arbitrary",)),
    )(u, u, cs, cs, gq, gk, sinks)


def _sw_bwd(u, cs, dp, gq, gk, sinks):
    t = u.shape[0]
    n = t // SW_B

    def body(u_ref, kvp_ref, csc_ref, csp_ref, gq_ref, gk_ref, sk_ref, dp_ref,
             du_ref, dgq_ref, dgk_ref, dsk_ref, dkv_ref):
        i = pl.program_id(0)

        @pl.when(i == 0)
        def _():
            for r in (dkv_ref, dgq_ref, dgk_ref, dsk_ref):
                r[...] = jnp.zeros_like(r)

        has_prev = i < n - 1
        args = _sw_args(u_ref, kvp_ref, csc_ref, csp_ref, gq_ref, gk_ref, sk_ref, has_prev)
        fn = lambda q, kvp, kvc, z, gq_, gk_, sk_: _sw_block(q, kvp, kvc, z, args[4], args[5], gq_, gk_, sk_, has_prev)
        _, vjp = jax.vjp(fn, args[0], args[1], args[2], args[3], args[6], args[7], args[8])
        dq, dkvp, dkvc, dz, dgq, dgk, dsk = vjp(dp_ref[...].astype(F32))
        du_ref[:, 0:D] = dq
        du_ref[:, SW_KV0:SW_KV0 + 512] = dkvc + dkv_ref[...]
        du_ref[:, SW_KV0 + 512:SW_N] = dz
        dkv_ref[...] = dkvp
        dgq_ref[...] += dgq
        dgk_ref[...] += dgk
        dsk_ref[...] += dsk

    small = lambda r, w: pl.BlockSpec((r, w), lambda i: (0, 0))
    return pl.pallas_call(
        body, name="sw_bwd", grid=(n,),
        in_specs=_sw_specs(n, True) + [pl.BlockSpec((SW_B, D), lambda i: (n - 1 - i, 0))],
        out_specs=[pl.BlockSpec((SW_B, SW_N), lambda i: (n - 1 - i, 0)), small(2, 512), small(2, 128), small(1, 128)],
        out_shape=[jax.ShapeDtypeStruct((t, SW_N), F32), jax.ShapeDtypeStruct((2, 512), F32),
                   jax.ShapeDtypeStruct((2, 128), F32), jax.ShapeDtypeStruct((1, 128), F32)],
        scratch_shapes=[pltpu.VMEM((SW_B, 512), F32)],
        compiler_params=_cparams(("arbitrary",)),
    )(u, u, cs, cs, gq, gk, sinks, dp)


def _ln_mod(x, g, scale, shift):
    y = x * lax.rsqrt(jnp.mean(x * x, axis=1, keepdims=True) + EPS) * g
    return y * (1.0 + scale) + shift


def _row_tile(t):
    return min(t, 1024)


def _ln_mm(x, g, scale, shift, w, tn):
    t, n = x.shape[0], w.shape[1]
    tm = _row_tile(t)
    vec = pl.BlockSpec((1, D), lambda i, j: (0, 0))

    def body(x_ref, g_ref, sc_ref, sh_ref, w_ref, u_ref, h_ref):
        @pl.when(pl.program_id(1) == 0)
        def _():
            h_ref[...] = _ln_mod(x_ref[...], g_ref[...], sc_ref[...], sh_ref[...]).astype(BF16)

        u_ref[...] = _dot(h_ref[...], w_ref[...], 1, 0)

    return pl.pallas_call(
        body, name="ln_mm", grid=(t // tm, n // tn),
        in_specs=[pl.BlockSpec((tm, D), lambda i, j: (i, 0)), vec, vec, vec,
                  pl.BlockSpec((D, tn), lambda i, j: (0, j))],
        out_specs=[pl.BlockSpec((tm, tn), lambda i, j: (i, j)), pl.BlockSpec((tm, D), lambda i, j: (i, 0))],
        out_shape=[jax.ShapeDtypeStruct((t, n), F32), jax.ShapeDtypeStruct((t, D), BF16)],
        compiler_params=_cparams(("arbitrary", "arbitrary")),
    )(x, g, scale, shift, w)


def _mm_res(p, w, x, gate):
    t, k = p.shape
    tm = _row_tile(t)

    def body(p_ref, w_ref, x_ref, gate_ref, o_ref):
        o_ref[...] = x_ref[...] + gate_ref[...] * _dot(p_ref[...], w_ref[...], 1, 0)

    return pl.pallas_call(
        body, name="mm_res", grid=(t // tm,),
        in_specs=[pl.BlockSpec((tm, k), lambda i: (i, 0)), pl.BlockSpec((k, D), lambda i: (0, 0)),
                  pl.BlockSpec((tm, D), lambda i: (i, 0)), pl.BlockSpec((1, D), lambda i: (0, 0))],
        out_specs=pl.BlockSpec((tm, D), lambda i: (i, 0)),
        out_shape=jax.ShapeDtypeStruct((t, D), F32),
        compiler_params=_cparams(("arbitrary",)),
    )(p, w, x, gate)


def _loss_grad(x, target):
    t = x.shape[0]
    tm = _row_tile(t)

    def body(x_ref, t_ref, l_ref, dx_ref):
        @pl.when(pl.program_id(0) == 0)
        def _():
            l_ref[...] = jnp.zeros_like(l_ref)

        err = x_ref[...] - t_ref[...]
        dx_ref[...] = err * (1.0 / D)
        l_ref[...] += 0.5 * jnp.sum(jnp.mean(err * err, axis=1, keepdims=True), axis=0, keepdims=True)

    return pl.pallas_call(
        body, name="loss_grad", grid=(t // tm,),
        in_specs=[pl.BlockSpec((tm, D), lambda i: (i, 0))] * 2,
        out_specs=[pl.BlockSpec((8, 128), lambda i: (0, 0)), pl.BlockSpec((tm, D), lambda i: (i, 0))],
        out_shape=[jax.ShapeDtypeStruct((8, 128), F32), jax.ShapeDtypeStruct((t, D), F32)],
        compiler_params=_cparams(("arbitrary",)),
    )(x, target)


def _mm_scaled(a, s, w, tn):
    t, k = a.shape
    n = w.shape[1]
    tm = _row_tile(t)

    def body(a_ref, s_ref, w_ref, o_ref):
        o_ref[...] = _dot((a_ref[...] * s_ref[...]).astype(BF16), w_ref[...], 1, 0).astype(BF16)

    return pl.pallas_call(
        body, name="mm_scaled", grid=(t // tm, n // tn),
        in_specs=[pl.BlockSpec((tm, k), lambda i, j: (i, 0)), pl.BlockSpec((1, k), lambda i, j: (0, 0)),
                  pl.BlockSpec((k, tn), lambda i, j: (0, j))],
        out_specs=pl.BlockSpec((tm, tn), lambda i, j: (i, j)),
        out_shape=jax.ShapeDtypeStruct((t, n), BF16),
        compiler_params=_cparams(("arbitrary", "arbitrary")),
    )(a, s, w)


def _mm_tn_acc(a, b, tn):
    t, m = a.shape
    n = b.shape[1]
    tk = min(t, 512)
    nk = t // tk

    def body(a_ref, b_ref, o_ref):
        @pl.when(pl.program_id(1) == 0)
        def _():
            o_ref[...] = jnp.zeros_like(o_ref)

        o_ref[...] += _dot(a_ref[...], b_ref[...].astype(BF16), 0, 0)

    return pl.pallas_call(
        body, name="mm_tn_acc", grid=(n // tn, nk),
        in_specs=[pl.BlockSpec((tk, m), lambda j, k: (k, 0)), pl.BlockSpec((tk, tn), lambda j, k: (k, j))],
        out_specs=pl.BlockSpec((m, tn), lambda j, k: (0, j)),
        out_shape=jax.ShapeDtypeStruct((m, n), F32),
        compiler_params=_cparams(("arbitrary", "arbitrary")),
    )(a, b)


def _inproj_bwd(du, wt, x, dxp, g, scale, shift, tk):
    t, kdim = du.shape
    tm = min(t, 512)
    nk = kdim // tk
    vec = pl.BlockSpec((1, D), lambda i, k: (0, 0))

    def body(du_ref, wt_ref, x_ref, dxp_ref, g_ref, sc_ref, sh_ref, dx_ref, dv_ref, acc_ref):
        k = pl.program_id(1)

        @pl.when((pl.program_id(0) == 0) & (k == 0))
        def _():
            dv_ref[...] = jnp.zeros_like(dv_ref)

        @pl.when(k == 0)
        def _():
            acc_ref[...] = jnp.zeros_like(acc_ref)

        acc_ref[...] += _dot(du_ref[...].astype(BF16), wt_ref[...], 1, 0)

        @pl.when(k == nk - 1)
        def _():
            _, vjp = jax.vjp(_ln_mod, x_ref[...], g_ref[...], sc_ref[...], sh_ref[...])
            dx, dg, dsc, dsh = vjp(acc_ref[...])
            dx_ref[...] = dxp_ref[...] + dx
            dv_ref[0:1, :] += dg
            dv_ref[1:2, :] += dsc
            dv_ref[2:3, :] += dsh

    return pl.pallas_call(
        body, name="inproj_bwd", grid=(t // tm, nk),
        in_specs=[pl.BlockSpec((tm, tk), lambda i, k: (i, k)), pl.BlockSpec((tk, D), lambda i, k: (k, 0)),
                  pl.BlockSpec((tm, D), lambda i, k: (i, 0)), pl.BlockSpec((tm, D), lambda i, k: (i, 0)),
                  vec, vec, vec],
        out_specs=[pl.BlockSpec((tm, D), lambda i, k: (i, 0)), pl.BlockSpec((8, D), lambda i, k: (0, 0))],
        out_shape=[jax.ShapeDtypeStruct((t, D), F32), jax.ShapeDtypeStruct((8, D), F32)],
        scratch_shapes=[pltpu.VMEM((tm, D), F32)],
        compiler_params=_cparams(("arbitrary", "arbitrary")),
    )(du, wt, x, dxp, g, scale, shift)


def _outgrad(gmat, w, gate):
    k = gmat.shape[0]
    tr = 256

    def body(g_ref, w_ref, gate_ref, dw_ref, dg_ref):
        @pl.when(pl.program_id(0) == 0)
        def _():
            dg_ref[...] = jnp.zeros_like(dg_ref)

        gm = g_ref[...]
        dw_ref[...] = gm * gate_ref[...]
        dg_ref[0:1, :] += jnp.sum(gm * w_ref[...].astype(F32), axis=0, keepdims=True)

    return pl.pallas_call(
        body, name="outgrad", grid=(k // tr,),
        in_specs=[pl.BlockSpec((tr, D), lambda i: (i, 0)), pl.BlockSpec((tr, D), lambda i: (i, 0)),
                  pl.BlockSpec((1, D), lambda i: (0, 0))],
        out_specs=[pl.BlockSpec((tr, D), lambda i: (i, 0)), pl.BlockSpec((8, D), lambda i: (0, 0))],
        out_shape=[jax.ShapeDtypeStruct((k, D), F32), jax.ShapeDtypeStruct((8, D), F32)],
        compiler_params=_cparams(("arbitrary",)),
    )(gmat, w, gate)


def _rope_table(pos, freq):
    t = pos.shape[0]
    tm = _row_tile(t)

    def body(p_ref, f_ref, o_ref):
        ang = p_ref[...].astype(F32) * f_ref[...]
        o_ref[:, 0:128] = jnp.cos(ang)
        o_ref[:, 128:256] = jnp.sin(ang)

    return pl.pallas_call(
        body, name="rope_table", grid=(t // tm,),
        in_specs=[pl.BlockSpec((tm, 1), lambda i: (i, 0)), pl.BlockSpec((1, 128), lambda i: (0, 0))],
        out_specs=pl.BlockSpec((tm, 256), lambda i: (i, 0)),
        out_shape=jax.ShapeDtypeStruct((t, 256), F32),
        compiler_params=_cparams(("arbitrary",)),
    )(pos, freq)


def _ada_fwd(c_all, w, b):
    nl, _, s = w.shape

    def body(c_ref, w_ref, b_ref, o_ref):
        o_ref[0] = _mm_f32(c_ref[...], w_ref[0]) + b_ref[0]

    return pl.pallas_call(
        body, name="ada_fwd", grid=(nl,),
        in_specs=[pl.BlockSpec((8, D), lambda l: (0, 0)), pl.BlockSpec((1, D, s), lambda l: (l, 0, 0)),
                  pl.BlockSpec((1, 1, s), lambda l: (l, 0, 0))],
        out_specs=pl.BlockSpec((1, 8, s), lambda l: (l, 0, 0)),
        out_shape=jax.ShapeDtypeStruct((nl, 8, s), F32),
        compiler_params=_cparams(("arbitrary",)),
    )(c_all, w, b)


def _ada_bwd(c_all, dmod_cols, dmod_all):
    nl, _, s = dmod_cols.shape

    def body(c_ref, dc_ref, da_ref, gw_ref, gb_ref):
        gw_ref[0] = _dot(c_ref[...], dc_ref[0], 0, 0, lax.Precision.HIGHEST)
        gb_ref[0] = jnp.sum(da_ref[0], axis=0, keepdims=True)

    return pl.pallas_call(
        body, name="ada_bwd", grid=(nl,),
        in_specs=[pl.BlockSpec((8, D), lambda l: (0, 0)), pl.BlockSpec((1, 8, s), lambda l: (l, 0, 0)),
                  pl.BlockSpec((1, 8, 3 * D), lambda l: (l, 0, 0))],
        out_specs=[pl.BlockSpec((1, D, s), lambda l: (l, 0, 0)), pl.BlockSpec((1, 1, 3 * D), lambda l: (l, 0, 0))],
        out_shape=[jax.ShapeDtypeStruct((nl, D, s), F32), jax.ShapeDtypeStruct((nl, 1, 3 * D), F32)],
        compiler_params=_cparams(("arbitrary",)),
    )(c_all, dmod_cols, dmod_all)


def _lb_fn(h8):
    sm = jax.nn.softmax(h8, axis=0)
    r = lax.broadcasted_iota(jnp.int32, (8, 8), 0)
    c = lax.broadcasted_iota(jnp.int32, (8, 8), 1)
    return _mm_f32(((c >= 1) & (c <= r)).astype(F32), sm)


def _lb_fwd(h8):
    def body(h_ref, o_ref):
        o_ref[...] = _lb_fn(h_ref[...])

    return pl.pallas_call(body, name="lb_fwd", out_shape=jax.ShapeDtypeStruct((8, D), F32))(h8)


def _lb_bwd(h8, dlb8):
    def body(h_ref, d_ref, o_ref):
        _, vjp = jax.vjp(_lb_fn, h_ref[...])
        o_ref[...] = vjp(d_ref[...])[0]

    return pl.pallas_call(body, name="lb_bwd", out_shape=jax.ShapeDtypeStruct((8, D), F32))(h8, dlb8)


ADAM_LR, ADAM_B1, ADAM_B2, ADAM_EPS, ADAM_WD, ADAM_STEP = 0.001, 0.9, 0.999, 1e-08, 0.01, 10


def _adamw(w, gparts, m, v):
    r, c = w.shape
    tr = r if r * c * 4 <= (1 << 20) else max(8, ((1 << 20) // (c * 4)) // 8 * 8)
    while r % tr:
        tr -= 8
    ng = len(gparts)

    def body(*refs):
        w_ref, m_ref, v_ref = refs[0], refs[1 + ng], refs[2 + ng]
        g_ref, d_ref, nm_ref, nv_ref = refs[3 + ng:]
        g = refs[1][...]
        for gr in refs[2:1 + ng]:
            g = g + gr[...]
        mm = ADAM_B1 * m_ref[...] + (1.0 - ADAM_B1) * g
        vv = ADAM_B2 * v_ref[...] + (1.0 - ADAM_B2) * (g * g)
        m_hat = mm / (1.0 - ADAM_B1 ** ADAM_STEP)
        v_hat = vv / (1.0 - ADAM_B2 ** ADAM_STEP)
        g_ref[...] = g
        d_ref[...] = -ADAM_LR * (m_hat / (jnp.sqrt(v_hat) + ADAM_EPS) + ADAM_WD * w_ref[...])
        nm_ref[...] = mm
        nv_ref[...] = vv

    spec = pl.BlockSpec((tr, c), lambda i: (i, 0))
    return pl.pallas_call(
        body, name="adamw", grid=(r // tr,), in_specs=[spec] * (3 + ng), out_specs=[spec] * 4,
        out_shape=[jax.ShapeDtypeStruct((r, c), F32)] * 4,
        compiler_params=_cparams(("arbitrary",)),
    )(w, *gparts, m, v)


def _sum_rows(parts):
    r, c = parts[0].shape
    tr = 8
    for cand in range(min(r, 512), 7, -8):
        if r % cand == 0:
            tr = cand
            break

    def body(*refs):
        acc = refs[0][...]
        for p in refs[1:-1]:
            acc = acc + p[...]
        refs[-1][...] = acc

    spec = pl.BlockSpec((tr, c), lambda i: (i, 0))
    return pl.pallas_call(
        body, name="sum_rows", grid=(r // tr,), in_specs=[spec] * len(parts), out_specs=spec,
        out_shape=jax.ShapeDtypeStruct((r, c), F32),
        compiler_params=_cparams(("arbitrary",)),
    )(*parts)


MESH = pl.DeviceIdType.MESH
ANY = pl.BlockSpec(memory_space=pl.ANY)


def _place():
    return lax.axis_index("x"), lax.axis_index("y"), lax.axis_index("c")


def _allgather8(blk):
    m_per, n = blk.shape

    def body(x_ref, out_ref, send_sems, recv_sems, local_sem):
        x, y, c = _place()
        me, sibling = (x, y, c), (x, y, 1 - c)
        chips = [(1 - x, y), (x, 1 - y), (1 - x, 1 - y)]

        def rows(px, py, pc):
            return out_ref.at[pl.ds((4 * px + 2 * py + pc) * m_per, m_per), :]

        def copy(k, block, to, src=None):
            return pltpu.make_async_remote_copy(
                src_ref=rows(*block) if src is None else src, dst_ref=rows(*block),
                send_sem=send_sems.at[k], recv_sem=recv_sems.at[k], device_id=to, device_id_type=MESH)

        mine = pltpu.make_async_copy(x_ref, rows(*me), local_sem)
        mine.start()
        first = [copy(0, me, sibling, src=x_ref)]
        first += [copy(1 + j, me, (*chip, c), src=x_ref) for j, chip in enumerate(chips)]
        for cp in first:
            cp.start()
        passed = [copy(4 + j, (*chip, c), sibling) for j, chip in enumerate(chips)]
        for j, chip in enumerate(chips):
            copy(1 + j, (*chip, c), me).wait_recv()
            passed[j].start()
        copy(0, sibling, me).wait_recv()
        for j, chip in enumerate(chips):
            copy(4 + j, (*chip, 1 - c), me).wait_recv()
        for cp in first + passed:
            cp.wait_send()
        mine.wait()

    return pl.pallas_call(
        body, name="allgather8",
        out_shape=jax.ShapeDtypeStruct((8 * m_per, n), blk.dtype),
        in_specs=[pl.BlockSpec(memory_space=pltpu.VMEM)],
        out_specs=pl.BlockSpec(memory_space=pltpu.VMEM),
        scratch_shapes=[pltpu.SemaphoreType.DMA((7,)), pltpu.SemaphoreType.DMA((7,)), pltpu.SemaphoreType.DMA],
    )(blk)


def _chip_peers():
    x, y, c = _place()
    return [(1 - x, y, c), (x, 1 - y, c), (1 - x, 1 - y, c)]


def _chip_allgather(shard):
    def body(x_ref, out_ref, send_sems, recv_sems, local_sem):
        x, y, _ = _place()
        peers = _chip_peers()

        def copy(j, chip_index):
            return pltpu.make_async_remote_copy(
                src_ref=x_ref, dst_ref=out_ref.at[chip_index], send_sem=send_sems.at[j], recv_sem=recv_sems.at[j],
                device_id=peers[j], device_id_type=MESH)

        mine = pltpu.make_async_copy(x_ref, out_ref.at[2 * x + y], local_sem)
        mine.start()
        sends = [copy(j, 2 * x + y) for j in range(3)]
        for cp in sends:
            cp.start()
        for j in range(3):
            copy(j, 2 * peers[j][0] + peers[j][1]).wait_recv()
        for cp in sends:
            cp.wait_send()
        mine.wait()

    return pl.pallas_call(
        body, name="chip_allgather",
        out_shape=jax.ShapeDtypeStruct((4,) + shard.shape, shard.dtype),
        in_specs=[ANY], out_specs=ANY,
        scratch_shapes=[pltpu.SemaphoreType.DMA((3,)), pltpu.SemaphoreType.DMA((3,)), pltpu.SemaphoreType.DMA],
    )(shard)


def _chip_scatter(parts):
    def body(p_ref, out_ref, send_sems, recv_sems):
        peers = _chip_peers()
        sends = [pltpu.make_async_remote_copy(
            src_ref=p_ref.at[2 * peers[j][0] + peers[j][1]], dst_ref=out_ref.at[j], send_sem=send_sems.at[j],
            recv_sem=recv_sems.at[j], device_id=peers[j], device_id_type=MESH) for j in range(3)]
        for cp in sends:
            cp.start()
        for cp in sends:
            cp.wait_recv()
        for cp in sends:
            cp.wait_send()

    return pl.pallas_call(
        body, name="chip_scatter",
        out_shape=jax.ShapeDtypeStruct((3,) + parts.shape[1:], parts.dtype),
        in_specs=[ANY], out_specs=ANY,
        scratch_shapes=[pltpu.SemaphoreType.DMA((3,)), pltpu.SemaphoreType.DMA((3,))],
    )(parts)


def _sibling_swap(a):
    def body(a_ref, out_ref, send_sem, recv_sem):
        x, y, c = _place()
        cp = pltpu.make_async_remote_copy(src_ref=a_ref, dst_ref=out_ref, send_sem=send_sem, recv_sem=recv_sem,
                                          device_id=(x, y, 1 - c), device_id_type=MESH)
        cp.start()
        cp.wait_recv()
        cp.wait_send()

    return pl.pallas_call(
        body, name="sibling_swap", out_shape=jax.ShapeDtypeStruct(a.shape, a.dtype),
        in_specs=[ANY], out_specs=ANY,
        scratch_shapes=[pltpu.SemaphoreType.DMA, pltpu.SemaphoreType.DMA],
    )(a)


WEIGHTS = ['hgrn_lb', 'ada_w', 'ada_b', 'norm_g', 'hg_in_w', 'hg_out_w', 'hg_onorm', 'sw_in_w', 'sw_out_w', 'sw_qnorm',
           'sw_knorm', 'sw_sinks', 'gd_in_w', 'gd_out_w', 'gd_conv_w', 'gd_a_log', 'gd_dt_bias', 'gd_onorm']
BIG = ['hg_in_w', 'hg_out_w', 'sw_in_w', 'sw_out_w', 'gd_in_w', 'gd_out_w']
PACK_ALIGN = 16
ROPE_THETA = 10000.0
ADA_S = 3 * D // 4
SMALL_ROW = {'hg_onorm': (0, 256), 'sw_qnorm': (256, 64), 'sw_knorm': (320, 64), 'sw_sinks': (384, 16),
             'gd_a_log': (400, 16), 'gd_dt_bias': (416, 16), 'gd_onorm': (432, 128)}


def _pack_rows(arrs):
    flat = jnp.concatenate([a.reshape(-1, D) for a in arrs], axis=0)
    return jnp.pad(flat, ((0, -flat.shape[0] % PACK_ALIGN), (0, 0)))


def _unpack_rows(packed, shapes):
    out, off = [], 0
    for s in shapes:
        rows = 1
        for d in s:
            rows *= d
        rows //= D
        out.append(packed[..., off:off + rows, :].reshape(packed.shape[:-2] + tuple(s)))
        off += rows
    return out


def _pack_small(vals):
    row = jnp.concatenate([vals[k].reshape(-1) for k in SMALL_ROW])
    row = jnp.pad(row, (0, D - row.shape[0]))[None]
    return jnp.concatenate([vals['hgrn_lb'], vals['norm_g'], vals['gd_conv_w'].reshape(16, D), row,
                            jnp.zeros((7, D), F32)], axis=0)


def _sw_cols(w, inverse=False):
    def split(a, heads):
        shp = (a.shape[0], 2, heads, 32) if inverse else (a.shape[0], heads, 2, 32)
        return a.reshape(shp).transpose(0, 2, 1, 3).reshape(a.shape[0], heads * 64)
    return jnp.concatenate([split(w[:, 0:1024], 16), split(w[:, 1024:1280], 4), w[:, 1280:]], axis=1)


def kernel(x, c, positions, hgrn_lb, ada_w, ada_b, norm_g, hg_in_w, hg_out_w, hg_onorm, sw_in_w, sw_out_w, sw_qnorm, sw_knorm, sw_sinks, gd_in_w, gd_out_w, gd_conv_w, gd_a_log, gd_dt_bias, gd_onorm, loss_target, m_hgrn_lb, m_ada_w, m_ada_b, m_norm_g, m_hg_in_w, m_hg_out_w, m_hg_onorm, m_sw_in_w, m_sw_out_w, m_sw_qnorm, m_sw_knorm, m_sw_sinks, m_gd_in_w, m_gd_out_w, m_gd_conv_w, m_gd_a_log, m_gd_dt_bias, m_gd_onorm, v_hgrn_lb, v_ada_w, v_ada_b, v_norm_g, v_hg_in_w, v_hg_out_w, v_hg_onorm, v_sw_in_w, v_sw_out_w, v_sw_qnorm, v_sw_knorm, v_sw_sinks, v_gd_in_w, v_gd_out_w, v_gd_conv_w, v_gd_a_log, v_gd_dt_bias, v_gd_onorm):
    w_in = dict(hgrn_lb=hgrn_lb, ada_w=ada_w, ada_b=ada_b, norm_g=norm_g, hg_in_w=hg_in_w, hg_out_w=hg_out_w,
                hg_onorm=hg_onorm, sw_in_w=sw_in_w, sw_out_w=sw_out_w, sw_qnorm=sw_qnorm, sw_knorm=sw_knorm,
                sw_sinks=sw_sinks, gd_in_w=gd_in_w, gd_out_w=gd_out_w, gd_conv_w=gd_conv_w, gd_a_log=gd_a_log,
                gd_dt_bias=gd_dt_bias, gd_onorm=gd_onorm)
    m_in = dict(zip(WEIGHTS, (m_hgrn_lb, m_ada_w, m_ada_b, m_norm_g, m_hg_in_w, m_hg_out_w, m_hg_onorm, m_sw_in_w,
                              m_sw_out_w, m_sw_qnorm, m_sw_knorm, m_sw_sinks, m_gd_in_w, m_gd_out_w, m_gd_conv_w,
                              m_gd_a_log, m_gd_dt_bias, m_gd_onorm)))
    v_in = dict(zip(WEIGHTS, (v_hgrn_lb, v_ada_w, v_ada_b, v_norm_g, v_hg_in_w, v_hg_out_w, v_hg_onorm, v_sw_in_w,
                              v_sw_out_w, v_sw_qnorm, v_sw_knorm, v_sw_sinks, v_gd_in_w, v_gd_out_w, v_gd_conv_w,
                              v_gd_a_log, v_gd_dt_bias, v_gd_onorm)))
    ax, ay, ac = _place()
    chip = 2 * ax + ay
    bidx = 4 * ax + 2 * ay + ac
    t = x.shape[1]
    x0, target = x[0], loss_target[0]

    c_all = _allgather8(jnp.pad(c, ((0, 7), (0, 0)))).reshape(8, 8, D)[:, 0, :]
    ada_b_cols = lax.dynamic_slice(ada_b, (0, chip * ADA_S), (4, ADA_S)).reshape(4, 1, ADA_S)
    mod_sh = _ada_fwd(c_all, ada_w, ada_b_cols)
    mod_g = _allgather8(mod_sh.reshape(32, ADA_S)).reshape(4, 2, 4, 8, ADA_S)[:, 0]
    mod = lax.dynamic_index_in_dim(mod_g, bidx, axis=2, keepdims=False).transpose(1, 0, 2).reshape(4, 3 * D)
    shift = [mod[l:l + 1, 0:D] for l in range(4)]
    scale = [mod[l:l + 1, D:2 * D] for l in range(4)]
    gate = [mod[l:l + 1, 2 * D:3 * D] for l in range(4)]

    h8 = jnp.concatenate([hgrn_lb, jnp.full((4, D), -1e30, F32)], axis=0)
    lb_all = _lb_fwd(h8)
    freq = ROPE_THETA ** (-jnp.arange(0, 64, 2, dtype=F32) / 64)
    cs = _rope_table(positions.reshape(t, 1), jnp.tile(freq, 4)[None])

    big_shapes = [w_in[k].shape for k in BIG]
    gathered = _chip_allgather(_pack_rows([w_in[k] for k in BIG]).astype(BF16))
    hg_in_k, hg_out_k, sw_in_k, sw_out_k, gd_in_k, gd_out_k = _unpack_rows(gathered, big_shapes)
    hg_in_f = hg_in_k.transpose(1, 2, 0, 3).reshape(2, D, 4 * D)
    hg_out_f = hg_out_k.transpose(1, 0, 2, 3).reshape(2, D, D)
    sw_in_f = _sw_cols(sw_in_k[:, 0].transpose(1, 0, 2).reshape(D, SW_N))
    sw_out_f = sw_out_k.reshape(D, D)
    gd_in_f = jnp.pad(gd_in_k[:, 0].transpose(1, 0, 2).reshape(D, 6176), ((0, 0), (0, GD_N - 6176)))
    gd_out_f = gd_out_k.reshape(GD_VW, D)
    win = [hg_in_f[0], sw_in_f, gd_in_f, hg_in_f[1]]
    wout = [hg_out_f[0], sw_out_f, gd_out_f, hg_out_f[1]]
    tn_in = [1024, 1280, 896, 1024]

    gq = jnp.stack([jnp.tile(sw_qnorm[0, :32], 16), jnp.tile(sw_qnorm[0, 32:], 16)])
    gk = jnp.stack([jnp.tile(sw_knorm[0, :32], 4), jnp.tile(sw_knorm[0, 32:], 4)])
    pad128 = lambda a: jnp.pad(a, ((0, 0), (0, HD - a.shape[1])))
    sinks, alog, dtb = pad128(sw_sinks), pad128(gd_a_log), pad128(gd_dt_bias)
    cw8 = jnp.pad(_chip_allgather(gd_conv_w[0]).transpose(1, 0, 2).reshape(4, GD_QKV), ((0, 4), (0, 0)))
    lbs = {0: lb_all[0:1], 3: lb_all[3:4]}

    xs, us, hs, ps, stss = [x0], [], [], [], []
    for l in range(4):
        u, h = _ln_mm(xs[l], norm_g[l:l + 1], scale[l], shift[l], win[l], tn_in[l])
        if l % 3 == 0:
            p, sts = _hg_fwd(u, lbs[l], hg_onorm[l // 3:l // 3 + 1])
        elif l % 3 == 1:
            p, sts = _sw_fwd(u, cs, gq, gk, sinks), None
        else:
            p, sts = _gd_fwd(u, cw8, alog, dtb, gd_onorm)
        xs.append(_mm_res(p, wout[l], xs[l], gate[l]))
        us.append(u), hs.append(h), ps.append(p), stss.append(sts)
    lpart, dx = _loss_grad(xs[4], target)
    loss = lax.psum(lpart[0, 0], ("x", "y", "c"))

    g_small = {}
    d_in, d_out, dmod, dnorm_g, dlb8, dgo_hg = [None] * 4, [None] * 4, [None] * 4, [None] * 4, jnp.zeros((8, D), F32), {}
    for l in (3, 2, 1, 0):
        dp = _mm_scaled(dx, gate[l], wout[l].T, 1024)
        d_out[l], dgate = _outgrad(_mm_tn_acc(ps[l], dx, 512), wout[l], gate[l])
        if l % 3 == 0:
            du, dlb, dgo_hg[l // 3] = _hg_bwd(us[l], stss[l], dp, lbs[l], hg_onorm[l // 3:l // 3 + 1])
            dlb8 = lax.dynamic_update_slice(dlb8, dlb, (l, 0))
        elif l % 3 == 1:
            du, dgq, dgk, dsk = _sw_bwd(us[l], cs, dp, gq, gk, sinks)
            g_small['sw_qnorm'] = jnp.concatenate([dgq[0].reshape(16, 32).sum(0), dgq[1].reshape(16, 32).sum(0)])
            g_small['sw_knorm'] = jnp.concatenate([dgk[0].reshape(4, 32).sum(0), dgk[1].reshape(4, 32).sum(0)])
            g_small['sw_sinks'] = dsk[0, :16]
        else:
            du, dcw, dalog, ddtb, g_small['gd_onorm'] = _gd_bwd(us[l], stss[l], dp, cw8, alog, dtb, gd_onorm)
            g_small['gd_conv_w'], g_small['gd_a_log'], g_small['gd_dt_bias'] = dcw[:4], dalog[0, :16], ddtb[0, :16]
        d_in[l] = _mm_tn_acc(hs[l], du, 896 if l == 2 else 512)
        dx, dvec = _inproj_bwd(du, win[l].T, xs[l], dx, norm_g[l:l + 1], scale[l], shift[l], tn_in[l])
        dnorm_g[l] = dvec[0:1]
        dmod[l] = jnp.concatenate([dvec[2:3], dvec[1:2], dgate[0:1]], axis=1)
    grad_x = dx[None]

    g_small['hgrn_lb'] = _lb_bwd(h8, dlb8)[0:4]
    g_small['norm_g'] = jnp.concatenate(dnorm_g, axis=0)
    g_small['hg_onorm'] = jnp.concatenate([dgo_hg[0], dgo_hg[1]], axis=0)
    gs_all = _allgather8(_pack_small(g_small))
    gs = _sum_rows([gs_all[32 * d:32 * (d + 1)] for d in range(8)])

    def small_view(packed, k):
        if k == 'hgrn_lb':
            return packed[0:4]
        if k == 'norm_g':
            return packed[4:8]
        off, size = SMALL_ROW[k]
        return packed[24, off:off + size].reshape(w_in[k].shape)

    conv_sl = lambda full: lax.dynamic_slice(full.reshape(4, GD_QKV), (0, chip * D), (4, D))
    out = {}

    def put(k, res, shape):
        for name, r in zip(('grad_', 'delta_', 'new_m_', 'new_v_'), res):
            out[name + k] = r.reshape(shape)

    zero_conv = dict(gd_conv_w=jnp.zeros((4, GD_QKV), F32))
    small_names = ['hgrn_lb', 'norm_g'] + list(SMALL_ROW)
    res = _adamw(_pack_small({**{k: w_in[k] for k in small_names}, **zero_conv}), (gs,),
                 _pack_small({**{k: m_in[k] for k in small_names}, **zero_conv}),
                 _pack_small({**{k: v_in[k] for k in small_names}, **zero_conv}))
    for k in small_names:
        put(k, [small_view(r, k) for r in res], w_in[k].shape)
    put('gd_conv_w', _adamw(gd_conv_w[0], (conv_sl(gs[8:24]),), m_in['gd_conv_w'][0], v_in['gd_conv_w'][0]),
        gd_conv_w.shape)

    dm = _allgather8(jnp.pad(jnp.concatenate(dmod, axis=0), ((0, 4), (0, 0)))).reshape(8, 8, 3 * D)[:, :4]
    dm = dm.transpose(1, 0, 2)
    g_ada_w, g_ada_b = _ada_bwd(c_all, lax.dynamic_slice(dm, (0, 0, chip * ADA_S), (4, 8, ADA_S)), dm)
    put('ada_w', _adamw(ada_w.reshape(4 * D, ADA_S), (g_ada_w.reshape(4 * D, ADA_S),),
                        m_in['ada_w'].reshape(4 * D, ADA_S), v_in['ada_w'].reshape(4 * D, ADA_S)), ada_w.shape)
    put('ada_b', _adamw(ada_b, (g_ada_b.reshape(4, 3 * D),), m_in['ada_b'], v_in['ada_b']), ada_b.shape)

    by_chip = lambda g, cols: g.reshape(g.shape[0], 4, cols).transpose(1, 0, 2)
    d_sw_in = _sw_cols(d_in[1], inverse=True)
    parts = {
        'hg_in_w': jnp.stack([by_chip(d_in[0], D), by_chip(d_in[3], D)], axis=1),
        'hg_out_w': jnp.stack([d_out[0].reshape(4, D // 4, D), d_out[3].reshape(4, D // 4, D)], axis=1),
        'sw_in_w': by_chip(d_sw_in, SW_N // 4)[:, None],
        'sw_out_w': d_out[1].reshape(4, 1, D // 4, D),
        'gd_in_w': by_chip(d_in[2][:, :6176], 1544)[:, None],
        'gd_out_w': d_out[2].reshape(4, 1, GD_VW // 4, D),
    }
    packed = jnp.stack([_pack_rows([parts[k][j] for k in BIG]) for j in range(4)])
    recv = _chip_scatter(packed)
    own = lax.dynamic_index_in_dim(packed, chip, axis=0, keepdims=False)
    half = _sum_rows([own, recv[0], recv[1], recv[2]])
    other = _sibling_swap(half)
    res = _adamw(_pack_rows([w_in[k] for k in BIG]), (half, other), _pack_rows([m_in[k] for k in BIG]),
                 _pack_rows([v_in[k] for k in BIG]))
    for name, r in zip(('grad_', 'delta_', 'new_m_', 'new_v_'), res):
        for k, a in zip(BIG, _unpack_rows(r, big_shapes)):
            out[name + k] = a

    return (loss, grad_x, *[out[p + k] for p in ('grad_', 'delta_', 'new_m_', 'new_v_') for k in WEIGHTS])
```

```python
import functools

import jax
import jax.numpy as jnp
from jax import lax
from jax.experimental import pallas as pl
from jax.experimental.pallas import tpu as pltpu

F32 = jnp.float32
BF16 = jnp.bfloat16
D = 1024
EPS = 1e-6
CHUNK = 64
SUB = 16
HG_H = 8
HD = 128
VMEM_LIMIT = 56 * 1024 * 1024


def _cparams(sem=None):
    return pltpu.CompilerParams(dimension_semantics=sem, vmem_limit_bytes=VMEM_LIMIT)


def _dot(a, b, ca, cb, prec=None):
    return lax.dot_general(a, b, (((ca,), (cb,)), ((), ())), precision=prec, preferred_element_type=F32)


def _mm(a, b):
    return _dot(a.astype(BF16), b.astype(BF16), 1, 0)


def _mm_nt(a, b):
    return _dot(a.astype(BF16), b.astype(BF16), 1, 1)


def _mm_tn(a, b):
    return _dot(a.astype(BF16), b.astype(BF16), 0, 0)


def _mm_f32(a, b):
    return _dot(a, b, 1, 0, lax.Precision.HIGHEST)


def _silu(x):
    return x * jax.nn.sigmoid(x)


def _hg_chunk(q_raw, f_pre, v, z, st, lb, go):
    c = q_raw.shape[0]
    nsub = c // SUB
    lf = jnp.log(lb + (1.0 - lb) * jax.nn.sigmoid(f_pre))
    k = (1.0 - lb) * jax.nn.sigmoid(-f_pre)
    q = _silu(q_raw)
    ti = lax.broadcasted_iota(jnp.int32, (c, c), 0)
    si = lax.broadcasted_iota(jnp.int32, (c, c), 1)
    mats = [(si <= ti).astype(F32)] + [(si <= SUB * i + SUB // 2).astype(F32) for i in range(nsub)]
    cums = jnp.split(_mm_f32(jnp.concatenate(mats, axis=0), lf), nsub + 1, axis=0)
    b, bmid = cums[0], cums[1:]
    row = lax.broadcasted_iota(jnp.int32, (c, 1), 0)
    ref = sum(jnp.where((row >= SUB * i) & (row < SUB * (i + 1)), bmid[i], 0.0) for i in range(nsub))
    qt = q * jnp.exp(b - ref)
    kall = jnp.concatenate(
        [k * jnp.exp(jnp.where(row < SUB * (i + 1), bmid[i] - b, -jnp.inf)) for i in range(nsub)], axis=0)
    v4 = jnp.concatenate([v] * nsub, axis=0)
    b_last = jnp.sum(lf, axis=0, keepdims=True)
    qb = q * jnp.exp(b)
    kd = k * jnp.exp(b_last - b)
    e_last = jnp.exp(b_last)
    tq = lax.broadcasted_iota(jnp.int32, (c, nsub * c), 0)
    cq = lax.broadcasted_iota(jnp.int32, (c, nsub * c), 1)
    m_all = ((cq // c) == (tq // SUB)) & ((cq % c) <= tq)
    hs = lambda a: jnp.split(a, HG_H, axis=1)
    qt_h, kall_h, v4_h, qb_h, kd_h, v_h, z_h, el_h = map(hs, (qt, kall, v4, qb, kd, v, z, e_last))
    st_h = jnp.split(st, HG_H, axis=0)
    p_out, st_out = [], []
    for h in range(HG_H):
        pm = jnp.where(m_all, _mm_nt(qt_h[h], kall_h[h]), 0.0)
        o = _mm(pm, v4_h[h]) + _mm_nt(qb_h[h], st_h[h])
        st_out.append(el_h[h] * st_h[h] + _mm_tn(v_h[h], kd_h[h]))
        y = o * lax.rsqrt(jnp.mean(o * o, axis=1, keepdims=True) + EPS) * go
        p_out.append(y * _silu(z_h[h]))
    return jnp.concatenate(p_out, axis=1), jnp.concatenate(st_out, axis=0)


def _hg_fwd(u, lb, go):
    t = u.shape[0]
    n = t // CHUNK

    def body(u_ref, lb_ref, go_ref, p_ref, sts_ref, st_ref):
        @pl.when(pl.program_id(0) == 0)
        def _():
            st_ref[...] = jnp.zeros_like(st_ref)

        st = st_ref[...]
        sts_ref[0] = st
        p, st_next = _hg_chunk(u_ref[:, 0:D], u_ref[:, D:2 * D], u_ref[:, 2 * D:3 * D], u_ref[:, 3 * D:4 * D],
                               st, lb_ref[...], go_ref[...])
        p_ref[...] = p.astype(BF16)
        st_ref[...] = st_next

    return pl.pallas_call(
        body, name="hg_fwd", grid=(n,),
        in_specs=[pl.BlockSpec((CHUNK, 4 * D), lambda i: (i, 0)),
                  pl.BlockSpec((1, D), lambda i: (0, 0)),
                  pl.BlockSpec((1, HD), lambda i: (0, 0))],
        out_specs=[pl.BlockSpec((CHUNK, D), lambda i: (i, 0)),
                   pl.BlockSpec((1, HG_H * HD, HD), lambda i: (i, 0, 0))],
        out_shape=[jax.ShapeDtypeStruct((t, D), BF16), jax.ShapeDtypeStruct((n, HG_H * HD, HD), F32)],
        scratch_shapes=[pltpu.VMEM((HG_H * HD, HD), F32)],
        compiler_params=_cparams(("arbitrary",)),
    )(u, lb, go)


def _hg_bwd(u, sts, dp, lb, go):
    t = u.shape[0]
    n = t // CHUNK

    def body(u_ref, sts_ref, dp_ref, lb_ref, go_ref, du_ref, dlb_ref, dgo_ref, dst_ref):
        @pl.when(pl.program_id(0) == 0)
        def _():
            dst_ref[...] = jnp.zeros_like(dst_ref)
            dlb_ref[...] = jnp.zeros_like(dlb_ref)
            dgo_ref[...] = jnp.zeros_like(dgo_ref)

        _, vjp = jax.vjp(_hg_chunk, u_ref[:, 0:D], u_ref[:, D:2 * D], u_ref[:, 2 * D:3 * D], u_ref[:, 3 * D:4 * D],
                         sts_ref[0], lb_ref[...], go_ref[...])
        dq, df, dv, dz, dst, dlb, dgo = vjp((dp_ref[...].astype(F32), dst_ref[...]))
        du_ref[:, 0:D] = dq.astype(BF16)
        du_ref[:, D:2 * D] = df.astype(BF16)
        du_ref[:, 2 * D:3 * D] = dv.astype(BF16)
        du_ref[:, 3 * D:4 * D] = dz.astype(BF16)
        dst_ref[...] = dst
        dlb_ref[...] += dlb
        dgo_ref[...] += dgo

    rev = lambda i: (n - 1 - i, 0)
    return pl.pallas_call(
        body, name="hg_bwd", grid=(n,),
        in_specs=[pl.BlockSpec((CHUNK, 4 * D), rev),
                  pl.BlockSpec((1, HG_H * HD, HD), lambda i: (n - 1 - i, 0, 0)),
                  pl.BlockSpec((CHUNK, D), rev),
                  pl.BlockSpec((1, D), lambda i: (0, 0)),
                  pl.BlockSpec((1, HD), lambda i: (0, 0))],
        out_specs=[pl.BlockSpec((CHUNK, 4 * D), rev),
                   pl.BlockSpec((1, D), lambda i: (0, 0)),
                   pl.BlockSpec((1, HD), lambda i: (0, 0))],
        out_shape=[jax.ShapeDtypeStruct((t, 4 * D), BF16), jax.ShapeDtypeStruct((1, D), F32),
                   jax.ShapeDtypeStruct((1, HD), F32)],
        scratch_shapes=[pltpu.VMEM((HG_H * HD, HD), F32)],
        compiler_params=_cparams(("arbitrary",)),
    )(u, sts, dp, lb, go)


GD_VH = 16
GD_QKH = 8
GD_QKV = 4096
GD_VW = 2048
GD_N = GD_QKV + GD_VW + HD
GD_GRP = 4
HALO = 8


def _mm_high(a, b):
    return _dot(a, b, 1, 0, lax.Precision.HIGH)


def _lane_pick(a, h):
    lane = lax.broadcasted_iota(jnp.int32, a.shape, 1)
    return jnp.sum(jnp.where(lane == h, a, 0.0), axis=1, keepdims=True)


def _l2n(x):
    return x * lax.rsqrt(jnp.sum(x * x, axis=1, keepdims=True) + EPS)


def _solve_fwd(a_mat, rhs):
    n = a_mat.shape[0]
    r_i, c_i = lax.broadcasted_iota(jnp.int32, (n, n), 0), lax.broadcasted_iota(jnp.int32, (n, n), 1)
    same = lambda nb: (r_i // nb) == (c_i // nb)
    d0 = jnp.where(same(8), a_mat, 0.0)
    d2 = _mm_high(d0, d0)
    tinv = (r_i == c_i).astype(F32) - d0
    tinv = tinv + _mm_high(tinv, d2)
    tinv = tinv + _mm_high(tinv, _mm_high(d2, d2))
    nb = 16
    while nb <= CHUNK:
        low = jnp.where(same(nb) & ~same(nb // 2), a_mat, 0.0)
        tinv = tinv - _mm(_mm(tinv, low), tinv)
        nb *= 2
    x = _mm_high(tinv, rhs)
    return x, (tinv, x)


def _solve_bwd(res, dx):
    tinv, x = res
    drhs = _dot(tinv, dx, 0, 0, lax.Precision.HIGH)
    return -_dot(drhs, x, 1, 1, lax.Precision.HIGH), drhs


@jax.custom_vjp
def _unit_lower_solve(a_mat, rhs):
    return _solve_fwd(a_mat, rhs)[0]


_unit_lower_solve.defvjp(_solve_fwd, _solve_bwd)


def _gd_chunk(xh, x, z, ab, st, cw, alog, dtb, go):
    c = x.shape[0]
    xa = jnp.concatenate([xh, x], axis=0)
    ti = lax.broadcasted_iota(jnp.int32, (3 * c, c + HALO), 0)
    si = lax.broadcasted_iota(jnp.int32, (3 * c, c + HALO), 1)
    shift = (si == (ti % c) + (ti // c) + HALO - 3).astype(F32)
    sh = jnp.split(_mm_high(shift, xa), 3, axis=0)
    qkv = _silu(cw[0:1] * sh[0] + cw[1:2] * sh[1] + cw[2:3] * sh[2] + cw[3:4] * x)
    q_all, k_all, v_all = jnp.split(qkv, [1024, 2048], axis=1)
    lane = lax.broadcasted_iota(jnp.int32, (c, HD), 1)
    a_part = jnp.where(lane < GD_VH, ab, 0.0)
    g_all = -jnp.exp(alog) * jax.nn.softplus(a_part + dtb)
    tri = (lax.broadcasted_iota(jnp.int32, (c, c), 1) <= lax.broadcasted_iota(jnp.int32, (c, c), 0))
    d_all = _mm_f32(tri.astype(F32), g_all)
    dl_all = jnp.sum(g_all, axis=0, keepdims=True)
    beta_all = jax.nn.sigmoid(ab)
    gc = GD_GRP * c
    r_i, c_i = lax.broadcasted_iota(jnp.int32, (gc, gc), 0), lax.broadcasted_iota(jnp.int32, (gc, gc), 1)
    same_head = (r_i // c) == (c_i // c)
    tri_g, strict_g = same_head & (c_i <= r_i), same_head & (c_i < r_i)
    ones = jnp.ones((gc, HD), F32)
    lane_g = lax.broadcasted_iota(jnp.int32, (gc, HD), 1)
    qs = jnp.split(q_all, GD_QKH, axis=1)
    ks = jnp.split(k_all, GD_QKH, axis=1)
    vs = jnp.split(v_all, GD_VH, axis=1)
    zs = jnp.split(z, GD_VH, axis=1)
    sts = jnp.split(st, GD_VH, axis=0)
    qn = [_l2n(a) * (HD ** -0.5) for a in qs]
    kn = [_l2n(a) for a in ks]
    p_out, st_out = [], []
    for g in range(GD_VH // GD_GRP):
        heads = range(GD_GRP * g, GD_GRP * (g + 1))
        stack = lambda f: jnp.concatenate([f(h) for h in heads], axis=0)
        q_, k_, v_ = stack(lambda h: qn[h // 2]), stack(lambda h: kn[h // 2]), stack(lambda h: vs[h])
        dcol = stack(lambda h: _lane_pick(d_all, h))
        bcol = stack(lambda h: _lane_pick(beta_all, GD_VH + h))
        dlast = stack(lambda h: jnp.broadcast_to(_lane_pick(dl_all, h), (c, 1)))
        drow = _dot(ones, jnp.where(lane_g == 0, dcol, 0.0), 1, 1, lax.Precision.HIGHEST)
        dec = jnp.exp(jnp.where(tri_g, dcol - drow, -jnp.inf))
        kb = k_ * bcol
        a_mat = jnp.where(strict_g, _mm_nt(kb, k_) * dec, 0.0)
        xsol = _unit_lower_solve(a_mat, jnp.concatenate([v_ * bcol, kb * jnp.exp(dcol)], axis=1))
        u_, w_ = jnp.split(xsol, 2, axis=1)
        w_h = jnp.split(w_, GD_GRP, axis=0)
        v_new = u_ - jnp.concatenate([_mm(w_h[i], sts[h]) for i, h in enumerate(heads)], axis=0)
        qd_h = jnp.split(q_ * jnp.exp(dcol), GD_GRP, axis=0)
        o_g = _mm(_mm_nt(q_, k_) * dec, v_new) + jnp.concatenate(
            [_mm(qd_h[i], sts[h]) for i, h in enumerate(heads)], axis=0)
        kd_h = jnp.split(k_ * jnp.exp(dlast - dcol), GD_GRP, axis=0)
        vn_h = jnp.split(v_new, GD_GRP, axis=0)
        o_h = jnp.split(o_g, GD_GRP, axis=0)
        for i, h in enumerate(heads):
            st_out.append(sts[h] * jnp.exp(_lane_pick(dl_all, h)) + _mm_tn(kd_h[i], vn_h[i]))
            o = o_h[i]
            y = o * lax.rsqrt(jnp.mean(o * o, axis=1, keepdims=True) + EPS) * go
            p_out.append(y * _silu(zs[h]))
    return jnp.concatenate(p_out, axis=1), jnp.concatenate(st_out, axis=0)


def _gd_specs(n, rev):
    ci = (lambda i: n - 1 - i) if rev else (lambda i: i)
    return [pl.BlockSpec((HALO, GD_QKV), lambda i: (jnp.maximum(ci(i) * (CHUNK // HALO) - 1, 0), 0)),
            pl.BlockSpec((CHUNK, GD_N), lambda i: (ci(i), 0))]


def _gd_load(uh_ref, u_ref, first):
    xh = jnp.where(first, 0.0, uh_ref[...])
    return xh, u_ref[:, 0:GD_QKV], u_ref[:, GD_QKV:GD_QKV + GD_VW], u_ref[:, GD_QKV + GD_VW:GD_N]


def _gd_fwd(u, cw, alog, dtb, go):
    t = u.shape[0]
    n = t // CHUNK
    small = lambda r, w: pl.BlockSpec((r, w), lambda i: (0, 0))

    def body(uh_ref, u_ref, cw_ref, alog_ref, dtb_ref, go_ref, p_ref, sts_ref, st_ref):
        i = pl.program_id(0)

        @pl.when(i == 0)
        def _():
            st_ref[...] = jnp.zeros_like(st_ref)

        st = st_ref[...]
        sts_ref[0] = st
        p, st_next = _gd_chunk(*_gd_load(uh_ref, u_ref, i == 0), st, cw_ref[...], alog_ref[...], dtb_ref[...],
                               go_ref[...])
        p_ref[...] = p.astype(BF16)
        st_ref[...] = st_next

    return pl.pallas_call(
        body, name="gd_fwd", grid=(n,),
        in_specs=_gd_specs(n, False) + [small(8, GD_QKV), small(1, HD), small(1, HD), small(1, HD)],
        out_specs=[pl.BlockSpec((CHUNK, GD_VW), lambda i: (i, 0)),
                   pl.BlockSpec((1, GD_VH * HD, HD), lambda i: (i, 0, 0))],
        out_shape=[jax.ShapeDtypeStruct((t, GD_VW), BF16), jax.ShapeDtypeStruct((n, GD_VH * HD, HD), F32)],
        scratch_shapes=[pltpu.VMEM((GD_VH * HD, HD), F32)],
        compiler_params=_cparams(("arbitrary",)),
    )(u, u, cw, alog, dtb, go)


def _gd_bwd(u, sts, dp, cw, alog, dtb, go):
    t = u.shape[0]
    n = t // CHUNK
    small = lambda r, w: pl.BlockSpec((r, w), lambda i: (0, 0))

    def body(uh_ref, u_ref, sts_ref, dp_ref, cw_ref, alog_ref, dtb_ref, go_ref,
             du_ref, dcw_ref, dalog_ref, ddtb_ref, dgo_ref, dst_ref, dhalo_ref):
        i = pl.program_id(0)

        @pl.when(i == 0)
        def _():
            for r in (dst_ref, dhalo_ref, dcw_ref, dalog_ref, ddtb_ref, dgo_ref):
                r[...] = jnp.zeros_like(r)

        _, vjp = jax.vjp(_gd_chunk, *_gd_load(uh_ref, u_ref, i == n - 1), sts_ref[0], cw_ref[...], alog_ref[...],
                         dtb_ref[...], go_ref[...])
        dxh, dx, dz, dab, dst, dcw, dalog, ddtb, dgo = vjp((dp_ref[...].astype(F32), dst_ref[...]))
        tail = jnp.concatenate([jnp.zeros((CHUNK - HALO, GD_QKV), F32), dhalo_ref[...]], axis=0)
        du_ref[:, 0:GD_QKV] = (dx + tail).astype(BF16)
        du_ref[:, GD_QKV:GD_QKV + GD_VW] = dz.astype(BF16)
        du_ref[:, GD_QKV + GD_VW:GD_N] = dab.astype(BF16)
        dhalo_ref[...] = dxh
        dst_ref[...] = dst
        dcw_ref[...] += dcw
        dalog_ref[...] += dalog
        ddtb_ref[...] += ddtb
        dgo_ref[...] += dgo

    return pl.pallas_call(
        body, name="gd_bwd", grid=(n,),
        in_specs=_gd_specs(n, True) + [pl.BlockSpec((1, GD_VH * HD, HD), lambda i: (n - 1 - i, 0, 0)),
                                       pl.BlockSpec((CHUNK, GD_VW), lambda i: (n - 1 - i, 0)),
                                       small(8, GD_QKV), small(1, HD), small(1, HD), small(1, HD)],
        out_specs=[pl.BlockSpec((CHUNK, GD_N), lambda i: (n - 1 - i, 0)),
                   small(8, GD_QKV), small(1, HD), small(1, HD), small(1, HD)],
        out_shape=[jax.ShapeDtypeStruct((t, GD_N), BF16), jax.ShapeDtypeStruct((8, GD_QKV), F32)]
        + [jax.ShapeDtypeStruct((1, HD), F32)] * 3,
        scratch_shapes=[pltpu.VMEM((GD_VH * HD, HD), F32), pltpu.VMEM((HALO, GD_QKV), F32)],
        compiler_params=_cparams(("arbitrary",)),
    )(u, u, sts, dp, cw, alog, dtb, go)


SW_B = 128
SW_H = 16
SW_G = 4
SW_N = 2560
SW_KV0 = 1024


def _blockdiag(n, blk):
    r = lax.broadcasted_iota(jnp.int32, (n, n), 0) // blk
    c = lax.broadcasted_iota(jnp.int32, (n, n), 1) // blk
    return (r == c).astype(F32)


def _sw_normrope(x, g1, g2, cos, sin):
    w = x.shape[1] // 2
    x1, x2 = jnp.split(x, 2, axis=1)
    ms = _mm_high(x1 * x1 + x2 * x2, _blockdiag(w, 32)) * (1.0 / 64.0)
    rinv = lax.rsqrt(ms + EPS)
    n1, n2 = x1 * rinv * g1, x2 * rinv * g2
    return jnp.concatenate([n1 * cos - n2 * sin, n2 * cos + n1 * sin], axis=1)


def _sw_block(q, kvp, kvc, z, csp, csc, gq, gk, sinks, has_prev):
    b = q.shape[0]
    cos_c, sin_c = jnp.split(csc, 2, axis=1)
    cos_p, sin_p = jnp.split(csp, 2, axis=1)
    tile4 = lambda a: jnp.concatenate([a] * 4, axis=1)
    qh = _sw_normrope(q, gq[0:1], gq[1:2], tile4(cos_c), tile4(sin_c))
    kp, vp = jnp.split(kvp, 2, axis=1)
    kc, vc = jnp.split(kvc, 2, axis=1)
    kh = jnp.concatenate([_sw_normrope(kp, gk[0:1], gk[1:2], cos_p, sin_p),
                          _sw_normrope(kc, gk[0:1], gk[1:2], cos_c, sin_c)], axis=0)
    vv = jnp.concatenate([vp, vc], axis=0)
    q1, q2 = jnp.split(qh, 2, axis=1)
    q1g, q2g = jnp.split(q1, SW_G, axis=1), jnp.split(q2, SW_G, axis=1)
    qi = lax.broadcasted_iota(jnp.int32, (4 * b, 2 * b), 0) % b
    kj = lax.broadcasted_iota(jnp.int32, (4 * b, 2 * b), 1)
    rel = qi + b - kj
    mask = (rel >= 0) & (rel < SW_B) & (has_prev | (kj >= b))
    ri = lax.broadcasted_iota(jnp.int32, (256, 256), 0)
    ci = lax.broadcasted_iota(jnp.int32, (256, 256), 1)
    row_head = lax.broadcasted_iota(jnp.int32, (4 * b, 256), 0) // b
    lane_q = lax.broadcasted_iota(jnp.int32, (4 * b, 256), 1)
    q_sel = (lane_q % 128) // 32 == row_head
    o_sel = lane_q // 64 == row_head
    o_out = []
    for g in range(SW_G):
        ek = ((ri // 128 == ci // 128) & ((ri % 128) // 32 == g) & (ri % 32 == ci % 32)).astype(F32)
        ev = ((ri // 64 == g) & (ri % 64 == ci % 64)).astype(F32)
        kx = _mm(kh, ek)
        vx = _mm(vv, ev)
        qg = jnp.concatenate([q1g[g], q2g[g]], axis=1)
        q4 = jnp.where(q_sel, jnp.concatenate([qg] * 4, axis=0), 0.0)
        sink = jnp.concatenate([jnp.broadcast_to(_lane_pick(sinks, 4 * g + j), (b, 1)) for j in range(4)], axis=0)
        s = jnp.where(mask, _mm_nt(q4, kx) * (64 ** -0.5), -jnp.inf)
        m = jnp.maximum(jnp.max(s, axis=1, keepdims=True), sink)
        p = jnp.exp(s - m)
        pn = p / (jnp.sum(p, axis=1, keepdims=True) + jnp.exp(sink - m))
        o4 = jnp.split(jnp.where(o_sel, _mm(pn, vx), 0.0), 4, axis=0)
        o_out.append(o4[0] + o4[1] + o4[2] + o4[3])
    return jnp.concatenate(o_out, axis=1) * _silu(z)


def _sw_specs(n, rev):
    ci = (lambda i: n - 1 - i) if rev else (lambda i: i)
    prev = lambda i: jnp.maximum(ci(i) - 1, 0)
    return [pl.BlockSpec((SW_B, SW_N), lambda i: (ci(i), 0)),
            pl.BlockSpec((SW_B, 512), lambda i: (prev(i), SW_KV0 // 512)),
            pl.BlockSpec((SW_B, 256), lambda i: (ci(i), 0)),
            pl.BlockSpec((SW_B, 256), lambda i: (prev(i), 0)),
            pl.BlockSpec((2, 512), lambda i: (0, 0)), pl.BlockSpec((2, 128), lambda i: (0, 0)),
            pl.BlockSpec((1, 128), lambda i: (0, 0))]


def _sw_args(u_ref, kvp_ref, csc_ref, csp_ref, gq_ref, gk_ref, sk_ref, has_prev):
    return (u_ref[:, 0:D], kvp_ref[...], u_ref[:, SW_KV0:SW_KV0 + 512], u_ref[:, SW_KV0 + 512:SW_N],
            csp_ref[...], csc_ref[...], gq_ref[...], gk_ref[...], sk_ref[...], has_prev)


def _sw_fwd(u, cs, gq, gk, sinks):
    t = u.shape[0]
    n = t // SW_B

    def body(u_ref, kvp_ref, csc_ref, csp_ref, gq_ref, gk_ref, sk_ref, p_ref):
        has_prev = pl.program_id(0) > 0
        p_ref[...] = _sw_block(*_sw_args(u_ref, kvp_ref, csc_ref, csp_ref, gq_ref, gk_ref, sk_ref, has_prev)
                               ).astype(BF16)

    return pl.pallas_call(
        body, name="sw_fwd", grid=(n,), in_specs=_sw_specs(n, False),
        out_specs=pl.BlockSpec((SW_B, D), lambda i: (i, 0)),
        out_shape=jax.ShapeDtypeStruct((t, D), BF16),
        compiler_params=_cparams(("arbitrary",)),
    )(u, u, cs, cs, gq, gk, sinks)


def _sw_bwd(u, cs, dp, gq, gk, sinks):
    t = u.shape[0]
    n = t // SW_B

    def body(u_ref, kvp_ref, csc_ref, csp_ref, gq_ref, gk_ref, sk_ref, dp_ref,
             du_ref, dgq_ref, dgk_ref, dsk_ref, dkv_ref):
        i = pl.program_id(0)

        @pl.when(i == 0)
        def _():
            for r in (dkv_ref, dgq_ref, dgk_ref, dsk_ref):
                r[...] = jnp.zeros_like(r)

        has_prev = i < n - 1
        args = _sw_args(u_ref, kvp_ref, csc_ref, csp_ref, gq_ref, gk_ref, sk_ref, has_prev)
        fn = lambda q, kvp, kvc, z, gq_, gk_, sk_: _sw_block(q, kvp, kvc, z, args[4], args[5], gq_, gk_, sk_, has_prev)
        _, vjp = jax.vjp(fn, args[0], args[1], args[2], args[3], args[6], args[7], args[8])
        dq, dkvp, dkvc, dz, dgq, dgk, dsk = vjp(dp_ref[...].astype(F32))
        du_ref[:, 0:D] = dq.astype(BF16)
        du_ref[:, SW_KV0:SW_KV0 + 512] = (dkvc + dkv_ref[...]).astype(BF16)
        du_ref[:, SW_KV0 + 512:SW_N] = dz.astype(BF16)
        dkv_ref[...] = dkvp
        dgq_ref[...] += dgq
        dgk_ref[...] += dgk
        dsk_ref[...] += dsk

    small = lambda r, w: pl.BlockSpec((r, w), lambda i: (0, 0))
    return pl.pallas_call(
        body, name="sw_bwd", grid=(n,),
        in_specs=_sw_specs(n, True) + [pl.BlockSpec((SW_B, D), lambda i: (n - 1 - i, 0))],
        out_specs=[pl.BlockSpec((SW_B, SW_N), lambda i: (n - 1 - i, 0)), small(2, 512), small(2, 128), small(1, 128)],
        out_shape=[jax.ShapeDtypeStruct((t, SW_N), BF16), jax.ShapeDtypeStruct((2, 512), F32),
                   jax.ShapeDtypeStruct((2, 128), F32), jax.ShapeDtypeStruct((1, 128), F32)],
        scratch_shapes=[pltpu.VMEM((SW_B, 512), F32)],
        compiler_params=_cparams(("arbitrary",)),
    )(u, u, cs, cs, gq, gk, sinks, dp)


def _ln_mod(x, g, scale, shift):
    y = x * lax.rsqrt(jnp.mean(x * x, axis=1, keepdims=True) + EPS) * g
    return y * (1.0 + scale) + shift


def _row_tile(t):
    return min(t, 1024)


def _ln_mm(x, g, scale, shift, w, tn):
    t, n = x.shape[0], w.shape[1]
    tm = _row_tile(t)
    vec = pl.BlockSpec((1, D), lambda i, j: (0, 0))

    def body(x_ref, g_ref, sc_ref, sh_ref, w_ref, u_ref, h_ref):
        @pl.when(pl.program_id(1) == 0)
        def _():
            h_ref[...] = _ln_mod(x_ref[...], g_ref[...], sc_ref[...], sh_ref[...]).astype(BF16)

        u_ref[...] = _dot(h_ref[...], w_ref[...], 1, 0)

    return pl.pallas_call(
        body, name="ln_mm", grid=(t // tm, n // tn),
        in_specs=[pl.BlockSpec((tm, D), lambda i, j: (i, 0)), vec, vec, vec,
                  pl.BlockSpec((D, tn), lambda i, j: (0, j))],
        out_specs=[pl.BlockSpec((tm, tn), lambda i, j: (i, j)), pl.BlockSpec((tm, D), lambda i, j: (i, 0))],
        out_shape=[jax.ShapeDtypeStruct((t, n), F32), jax.ShapeDtypeStruct((t, D), BF16)],
        compiler_params=_cparams(("arbitrary", "arbitrary")),
    )(x, g, scale, shift, w)


def _mm_res(p, w, x, gate):
    t, k = p.shape
    tm = _row_tile(t)

    def body(p_ref, w_ref, x_ref, gate_ref, o_ref):
        o_ref[...] = x_ref[...] + gate_ref[...] * _dot(p_ref[...], w_ref[...], 1, 0)

    return pl.pallas_call(
        body, name="mm_res", grid=(t // tm,),
        in_specs=[pl.BlockSpec((tm, k), lambda i: (i, 0)), pl.BlockSpec((k, D), lambda i: (0, 0)),
                  pl.BlockSpec((tm, D), lambda i: (i, 0)), pl.BlockSpec((1, D), lambda i: (0, 0))],
        out_specs=pl.BlockSpec((tm, D), lambda i: (i, 0)),
        out_shape=jax.ShapeDtypeStruct((t, D), F32),
        compiler_params=_cparams(("arbitrary",)),
    )(p, w, x, gate)


def _loss_grad(x, target):
    t = x.shape[0]
    tm = _row_tile(t)

    def body(x_ref, t_ref, l_ref, dx_ref):
        @pl.when(pl.program_id(0) == 0)
        def _():
            l_ref[...] = jnp.zeros_like(l_ref)

        err = x_ref[...] - t_ref[...]
        dx_ref[...] = err * (1.0 / D)
        l_ref[...] += 0.5 * jnp.sum(jnp.mean(err * err, axis=1, keepdims=True), axis=0, keepdims=True)

    return pl.pallas_call(
        body, name="loss_grad", grid=(t // tm,),
        in_specs=[pl.BlockSpec((tm, D), lambda i: (i, 0))] * 2,
        out_specs=[pl.BlockSpec((8, 128), lambda i: (0, 0)), pl.BlockSpec((tm, D), lambda i: (i, 0))],
        out_shape=[jax.ShapeDtypeStruct((8, 128), F32), jax.ShapeDtypeStruct((t, D), F32)],
        compiler_params=_cparams(("arbitrary",)),
    )(x, target)


def _mm_scaled(a, s, w, tn):
    t, k = a.shape
    n = w.shape[1]
    tm = _row_tile(t)

    def body(a_ref, s_ref, w_ref, o_ref):
        o_ref[...] = _dot((a_ref[...] * s_ref[...]).astype(BF16), w_ref[...], 1, 0).astype(BF16)

    return pl.pallas_call(
        body, name="mm_scaled", grid=(t // tm, n // tn),
        in_specs=[pl.BlockSpec((tm, k), lambda i, j: (i, 0)), pl.BlockSpec((1, k), lambda i, j: (0, 0)),
                  pl.BlockSpec((k, tn), lambda i, j: (0, j))],
        out_specs=pl.BlockSpec((tm, tn), lambda i, j: (i, j)),
        out_shape=jax.ShapeDtypeStruct((t, n), BF16),
        compiler_params=_cparams(("arbitrary", "arbitrary")),
    )(a, s, w)


def _mm_tn_acc(a, b, tn):
    t, m = a.shape
    n = b.shape[1]
    tk = min(t, 512)
    nk = t // tk

    def body(a_ref, b_ref, o_ref):
        @pl.when(pl.program_id(1) == 0)
        def _():
            o_ref[...] = jnp.zeros_like(o_ref)

        o_ref[...] += _dot(a_ref[...], b_ref[...].astype(BF16), 0, 0)

    return pl.pallas_call(
        body, name="mm_tn_acc", grid=(n // tn, nk),
        in_specs=[pl.BlockSpec((tk, m), lambda j, k: (k, 0)), pl.BlockSpec((tk, tn), lambda j, k: (k, j))],
        out_specs=pl.BlockSpec((m, tn), lambda j, k: (0, j)),
        out_shape=jax.ShapeDtypeStruct((m, n), F32),
        compiler_params=_cparams(("arbitrary", "arbitrary")),
    )(a, b)


def _inproj_bwd(du, wt, x, dxp, g, scale, shift, tk):
    t, kdim = du.shape
    tm = min(t, 512)
    nk = kdim // tk
    vec = pl.BlockSpec((1, D), lambda i, k: (0, 0))

    def body(du_ref, wt_ref, x_ref, dxp_ref, g_ref, sc_ref, sh_ref, dx_ref, dv_ref, acc_ref):
        k = pl.program_id(1)

        @pl.when((pl.program_id(0) == 0) & (k == 0))
        def _():
            dv_ref[...] = jnp.zeros_like(dv_ref)

        @pl.when(k == 0)
        def _():
            acc_ref[...] = jnp.zeros_like(acc_ref)

        acc_ref[...] += _dot(du_ref[...].astype(BF16), wt_ref[...], 1, 0)

        @pl.when(k == nk - 1)
        def _():
            _, vjp = jax.vjp(_ln_mod, x_ref[...], g_ref[...], sc_ref[...], sh_ref[...])
            dx, dg, dsc, dsh = vjp(acc_ref[...])
            dx_ref[...] = dxp_ref[...] + dx
            dv_ref[0:1, :] += dg
            dv_ref[1:2, :] += dsc
            dv_ref[2:3, :] += dsh

    return pl.pallas_call(
        body, name="inproj_bwd", grid=(t // tm, nk),
        in_specs=[pl.BlockSpec((tm, tk), lambda i, k: (i, k)), pl.BlockSpec((tk, D), lambda i, k: (k, 0)),
                  pl.BlockSpec((tm, D), lambda i, k: (i, 0)), pl.BlockSpec((tm, D), lambda i, k: (i, 0)),
                  vec, vec, vec],
        out_specs=[pl.BlockSpec((tm, D), lambda i, k: (i, 0)), pl.BlockSpec((8, D), lambda i, k: (0, 0))],
        out_shape=[jax.ShapeDtypeStruct((t, D), F32), jax.ShapeDtypeStruct((8, D), F32)],
        scratch_shapes=[pltpu.VMEM((tm, D), F32)],
        compiler_params=_cparams(("arbitrary", "arbitrary")),
    )(du, wt, x, dxp, g, scale, shift)


def _outgrad(gmat, w, gate):
    k = gmat.shape[0]
    tr = 256

    def body(g_ref, w_ref, gate_ref, dw_ref, dg_ref):
        @pl.when(pl.program_id(0) == 0)
        def _():
            dg_ref[...] = jnp.zeros_like(dg_ref)

        gm = g_ref[...]
        dw_ref[...] = gm * gate_ref[...]
        dg_ref[0:1, :] += jnp.sum(gm * w_ref[...].astype(F32), axis=0, keepdims=True)

    return pl.pallas_call(
        body, name="outgrad", grid=(k // tr,),
        in_specs=[pl.BlockSpec((tr, D), lambda i: (i, 0)), pl.BlockSpec((tr, D), lambda i: (i, 0)),
                  pl.BlockSpec((1, D), lambda i: (0, 0))],
        out_specs=[pl.BlockSpec((tr, D), lambda i: (i, 0)), pl.BlockSpec((8, D), lambda i: (0, 0))],
        out_shape=[jax.ShapeDtypeStruct((k, D), F32), jax.ShapeDtypeStruct((8, D), F32)],
        compiler_params=_cparams(("arbitrary",)),
    )(gmat, w, gate)


def _rope_table(pos, freq):
    t = pos.shape[0]
    tm = _row_tile(t)

    def body(p_ref, f_ref, o_ref):
        ang = p_ref[...].astype(F32) * f_ref[...]
        o_ref[:, 0:128] = jnp.cos(ang)
        o_ref[:, 128:256] = jnp.sin(ang)

    return pl.pallas_call(
        body, name="rope_table", grid=(t // tm,),
        in_specs=[pl.BlockSpec((tm, 1), lambda i: (i, 0)), pl.BlockSpec((1, 128), lambda i: (0, 0))],
        out_specs=pl.BlockSpec((tm, 256), lambda i: (i, 0)),
        out_shape=jax.ShapeDtypeStruct((t, 256), F32),
        compiler_params=_cparams(("arbitrary",)),
    )(pos, freq)


def _ada_fwd(c_all, w, b):
    nl, _, s = w.shape

    def body(c_ref, w_ref, b_ref, o_ref):
        o_ref[0] = _mm_f32(c_ref[...], w_ref[0]) + b_ref[0]

    return pl.pallas_call(
        body, name="ada_fwd", grid=(nl,),
        in_specs=[pl.BlockSpec((8, D), lambda l: (0, 0)), pl.BlockSpec((1, D, s), lambda l: (l, 0, 0)),
                  pl.BlockSpec((1, 1, s), lambda l: (l, 0, 0))],
        out_specs=pl.BlockSpec((1, 8, s), lambda l: (l, 0, 0)),
        out_shape=jax.ShapeDtypeStruct((nl, 8, s), F32),
        compiler_params=_cparams(("arbitrary",)),
    )(c_all, w, b)


def _ada_bwd(c_all, dmod_cols, dmod_all):
    nl, _, s = dmod_cols.shape

    def body(c_ref, dc_ref, da_ref, gw_ref, gb_ref):
        gw_ref[0] = _dot(c_ref[...], dc_ref[0], 0, 0, lax.Precision.HIGHEST)
        gb_ref[0] = jnp.sum(da_ref[0], axis=0, keepdims=True)

    return pl.pallas_call(
        body, name="ada_bwd", grid=(nl,),
        in_specs=[pl.BlockSpec((8, D), lambda l: (0, 0)), pl.BlockSpec((1, 8, s), lambda l: (l, 0, 0)),
                  pl.BlockSpec((1, 8, 3 * D), lambda l: (l, 0, 0))],
        out_specs=[pl.BlockSpec((1, D, s), lambda l: (l, 0, 0)), pl.BlockSpec((1, 1, 3 * D), lambda l: (l, 0, 0))],
        out_shape=[jax.ShapeDtypeStruct((nl, D, s), F32), jax.ShapeDtypeStruct((nl, 1, 3 * D), F32)],
        compiler_params=_cparams(("arbitrary",)),
    )(c_all, dmod_cols, dmod_all)


def _lb_fn(h8):
    sm = jax.nn.softmax(h8, axis=0)
    r = lax.broadcasted_iota(jnp.int32, (8, 8), 0)
    c = lax.broadcasted_iota(jnp.int32, (8, 8), 1)
    return _mm_f32(((c >= 1) & (c <= r)).astype(F32), sm)


def _lb_fwd(h8):
    def body(h_ref, o_ref):
        o_ref[...] = _lb_fn(h_ref[...])

    return pl.pallas_call(body, name="lb_fwd", out_shape=jax.ShapeDtypeStruct((8, D), F32))(h8)


def _lb_bwd(h8, dlb8):
    def body(h_ref, d_ref, o_ref):
        _, vjp = jax.vjp(_lb_fn, h_ref[...])
        o_ref[...] = vjp(d_ref[...])[0]

    return pl.pallas_call(body, name="lb_bwd", out_shape=jax.ShapeDtypeStruct((8, D), F32))(h8, dlb8)


ADAM_LR, ADAM_B1, ADAM_B2, ADAM_EPS, ADAM_WD, ADAM_STEP = 0.001, 0.9, 0.999, 1e-08, 0.01, 10


def _adamw(w, gparts, m, v):
    r, c = w.shape
    tr = r if r * c * 4 <= (1 << 20) else max(8, ((1 << 20) // (c * 4)) // 8 * 8)
    while r % tr:
        tr -= 8
    ng = len(gparts)

    def body(*refs):
        w_ref, m_ref, v_ref = refs[0], refs[1 + ng], refs[2 + ng]
        g_ref, d_ref, nm_ref, nv_ref = refs[3 + ng:]
        g = refs[1][...]
        for gr in refs[2:1 + ng]:
            g = g + gr[...]
        mm = ADAM_B1 * m_ref[...] + (1.0 - ADAM_B1) * g
        vv = ADAM_B2 * v_ref[...] + (1.0 - ADAM_B2) * (g * g)
        m_hat = mm / (1.0 - ADAM_B1 ** ADAM_STEP)
        v_hat = vv / (1.0 - ADAM_B2 ** ADAM_STEP)
        g_ref[...] = g
        d_ref[...] = -ADAM_LR * (m_hat / (jnp.sqrt(v_hat) + ADAM_EPS) + ADAM_WD * w_ref[...])
        nm_ref[...] = mm
        nv_ref[...] = vv

    spec = pl.BlockSpec((tr, c), lambda i: (i, 0))
    return pl.pallas_call(
        body, name="adamw", grid=(r // tr,), in_specs=[spec] * (3 + ng), out_specs=[spec] * 4,
        out_shape=[jax.ShapeDtypeStruct((r, c), F32)] * 4,
        compiler_params=_cparams(("arbitrary",)),
    )(w, *gparts, m, v)


def _sum_rows(parts):
    r, c = parts[0].shape
    tr = 8
    for cand in range(min(r, 512), 7, -8):
        if r % cand == 0:
            tr = cand
            break

    def body(*refs):
        acc = refs[0][...]
        for p in refs[1:-1]:
            acc = acc + p[...]
        refs[-1][...] = acc

    spec = pl.BlockSpec((tr, c), lambda i: (i, 0))
    return pl.pallas_call(
        body, name="sum_rows", grid=(r // tr,), in_specs=[spec] * len(parts), out_specs=spec,
        out_shape=jax.ShapeDtypeStruct((r, c), F32),
        compiler_params=_cparams(("arbitrary",)),
    )(*parts)


MESH = pl.DeviceIdType.MESH
ANY = pl.BlockSpec(memory_space=pl.ANY)


def _place():
    return lax.axis_index("x"), lax.axis_index("y"), lax.axis_index("c")


def _allgather8(blk):
    m_per, n = blk.shape

    def body(x_ref, out_ref, send_sems, recv_sems, local_sem):
        x, y, c = _place()
        me, sibling = (x, y, c), (x, y, 1 - c)
        chips = [(1 - x, y), (x, 1 - y), (1 - x, 1 - y)]

        def rows(px, py, pc):
            return out_ref.at[pl.ds((4 * px + 2 * py + pc) * m_per, m_per), :]

        def copy(k, block, to, src=None):
            return pltpu.make_async_remote_copy(
                src_ref=rows(*block) if src is None else src, dst_ref=rows(*block),
                send_sem=send_sems.at[k], recv_sem=recv_sems.at[k], device_id=to, device_id_type=MESH)

        mine = pltpu.make_async_copy(x_ref, rows(*me), local_sem)
        mine.start()
        first = [copy(0, me, sibling, src=x_ref)]
        first += [copy(1 + j, me, (*chip, c), src=x_ref) for j, chip in enumerate(chips)]
        for cp in first:
            cp.start()
        passed = [copy(4 + j, (*chip, c), sibling) for j, chip in enumerate(chips)]
        for j, chip in enumerate(chips):
            copy(1 + j, (*chip, c), me).wait_recv()
            passed[j].start()
        copy(0, sibling, me).wait_recv()
        for j, chip in enumerate(chips):
            copy(4 + j, (*chip, 1 - c), me).wait_recv()
        for cp in first + passed:
            cp.wait_send()
        mine.wait()

    return pl.pallas_call(
        body, name="allgather8",
        out_shape=jax.ShapeDtypeStruct((8 * m_per, n), blk.dtype),
        in_specs=[pl.BlockSpec(memory_space=pltpu.VMEM)],
        out_specs=pl.BlockSpec(memory_space=pltpu.VMEM),
        scratch_shapes=[pltpu.SemaphoreType.DMA((7,)), pltpu.SemaphoreType.DMA((7,)), pltpu.SemaphoreType.DMA],
    )(blk)


def _chip_peers():
    x, y, c = _place()
    return [(1 - x, y, c), (x, 1 - y, c), (1 - x, 1 - y, c)]


def _chip_allgather(shard):
    def body(x_ref, out_ref, send_sems, recv_sems, local_sem):
        x, y, _ = _place()
        peers = _chip_peers()

        def copy(j, chip_index):
            return pltpu.make_async_remote_copy(
                src_ref=x_ref, dst_ref=out_ref.at[chip_index], send_sem=send_sems.at[j], recv_sem=recv_sems.at[j],
                device_id=peers[j], device_id_type=MESH)

        mine = pltpu.make_async_copy(x_ref, out_ref.at[2 * x + y], local_sem)
        mine.start()
        sends = [copy(j, 2 * x + y) for j in range(3)]
        for cp in sends:
            cp.start()
        for j in range(3):
            copy(j, 2 * peers[j][0] + peers[j][1]).wait_recv()
        for cp in sends:
            cp.wait_send()
        mine.wait()

    return pl.pallas_call(
        body, name="chip_allgather",
        out_shape=jax.ShapeDtypeStruct((4,) + shard.shape, shard.dtype),
        in_specs=[ANY], out_specs=ANY,
        scratch_shapes=[pltpu.SemaphoreType.DMA((3,)), pltpu.SemaphoreType.DMA((3,)), pltpu.SemaphoreType.DMA],
    )(shard)


def _chip_scatter(parts):
    def body(p_ref, out_ref, send_sems, recv_sems):
        peers = _chip_peers()
        sends = [pltpu.make_async_remote_copy(
            src_ref=p_ref.at[2 * peers[j][0] + peers[j][1]], dst_ref=out_ref.at[j], send_sem=send_sems.at[j],
            recv_sem=recv_sems.at[j], device_id=peers[j], device_id_type=MESH) for j in range(3)]
        for cp in sends:
            cp.start()
        for cp in sends:
            cp.wait_recv()
        for cp in sends:
            cp.wait_send()

    return pl.pallas_call(
        body, name="chip_scatter",
        out_shape=jax.ShapeDtypeStruct((3,) + parts.shape[1:], parts.dtype),
        in_specs=[ANY], out_specs=ANY,
        scratch_shapes=[pltpu.SemaphoreType.DMA((3,)), pltpu.SemaphoreType.DMA((3,))],
    )(parts)


def _sibling_swap(a):
    def body(a_ref, out_ref, send_sem, recv_sem):
        x, y, c = _place()
        cp = pltpu.make_async_remote_copy(src_ref=a_ref, dst_ref=out_ref, send_sem=send_sem, recv_sem=recv_sem,
                                          device_id=(x, y, 1 - c), device_id_type=MESH)
        cp.start()
        cp.wait_recv()
        cp.wait_send()

    return pl.pallas_call(
        body, name="sibling_swap", out_shape=jax.ShapeDtypeStruct(a.shape, a.dtype),
        in_specs=[ANY], out_specs=ANY,
        scratch_shapes=[pltpu.SemaphoreType.DMA, pltpu.SemaphoreType.DMA],
    )(a)


WEIGHTS = ['hgrn_lb', 'ada_w', 'ada_b', 'norm_g', 'hg_in_w', 'hg_out_w', 'hg_onorm', 'sw_in_w', 'sw_out_w', 'sw_qnorm',
           'sw_knorm', 'sw_sinks', 'gd_in_w', 'gd_out_w', 'gd_conv_w', 'gd_a_log', 'gd_dt_bias', 'gd_onorm']
BIG = ['hg_in_w', 'hg_out_w', 'sw_in_w', 'sw_out_w', 'gd_in_w', 'gd_out_w']
PACK_ALIGN = 16
ROPE_THETA = 10000.0
ADA_S = 3 * D // 4
SMALL_ROW = {'hg_onorm': (0, 256), 'sw_qnorm': (256, 64), 'sw_knorm': (320, 64), 'sw_sinks': (384, 16),
             'gd_a_log': (400, 16), 'gd_dt_bias': (416, 16), 'gd_onorm': (432, 128)}


def _pack_rows(arrs):
    flat = jnp.concatenate([a.reshape(-1, D) for a in arrs], axis=0)
    return jnp.pad(flat, ((0, -flat.shape[0] % PACK_ALIGN), (0, 0)))


def _unpack_rows(packed, shapes):
    out, off = [], 0
    for s in shapes:
        rows = 1
        for d in s:
            rows *= d
        rows //= D
        out.append(packed[..., off:off + rows, :].reshape(packed.shape[:-2] + tuple(s)))
        off += rows
    return out


def _pack_small(vals):
    row = jnp.concatenate([vals[k].reshape(-1) for k in SMALL_ROW])
    row = jnp.pad(row, (0, D - row.shape[0]))[None]
    return jnp.concatenate([vals['hgrn_lb'], vals['norm_g'], vals['gd_conv_w'].reshape(16, D), row,
                            jnp.zeros((7, D), F32)], axis=0)


def _sw_cols(w, inverse=False):
    def split(a, heads):
        shp = (a.shape[0], 2, heads, 32) if inverse else (a.shape[0], heads, 2, 32)
        return a.reshape(shp).transpose(0, 2, 1, 3).reshape(a.shape[0], heads * 64)
    return jnp.concatenate([split(w[:, 0:1024], 16), split(w[:, 1024:1280], 4), w[:, 1280:]], axis=1)


def kernel(x, c, positions, hgrn_lb, ada_w, ada_b, norm_g, hg_in_w, hg_out_w, hg_onorm, sw_in_w, sw_out_w, sw_qnorm, sw_knorm, sw_sinks, gd_in_w, gd_out_w, gd_conv_w, gd_a_log, gd_dt_bias, gd_onorm, loss_target, m_hgrn_lb, m_ada_w, m_ada_b, m_norm_g, m_hg_in_w, m_hg_out_w, m_hg_onorm, m_sw_in_w, m_sw_out_w, m_sw_qnorm, m_sw_knorm, m_sw_sinks, m_gd_in_w, m_gd_out_w, m_gd_conv_w, m_gd_a_log, m_gd_dt_bias, m_gd_onorm, v_hgrn_lb, v_ada_w, v_ada_b, v_norm_g, v_hg_in_w, v_hg_out_w, v_hg_onorm, v_sw_in_w, v_sw_out_w, v_sw_qnorm, v_sw_knorm, v_sw_sinks, v_gd_in_w, v_gd_out_w, v_gd_conv_w, v_gd_a_log, v_gd_dt_bias, v_gd_onorm):
    w_in = dict(hgrn_lb=hgrn_lb, ada_w=ada_w, ada_b=ada_b, norm_g=norm_g, hg_in_w=hg_in_w, hg_out_w=hg_out_w,
                hg_onorm=hg_onorm, sw_in_w=sw_in_w, sw_out_w=sw_out_w, sw_qnorm=sw_qnorm, sw_knorm=sw_knorm,
                sw_sinks=sw_sinks, gd_in_w=gd_in_w, gd_out_w=gd_out_w, gd_conv_w=gd_conv_w, gd_a_log=gd_a_log,
                gd_dt_bias=gd_dt_bias, gd_onorm=gd_onorm)
    m_in = dict(zip(WEIGHTS, (m_hgrn_lb, m_ada_w, m_ada_b, m_norm_g, m_hg_in_w, m_hg_out_w, m_hg_onorm, m_sw_in_w,
                              m_sw_out_w, m_sw_qnorm, m_sw_knorm, m_sw_sinks, m_gd_in_w, m_gd_out_w, m_gd_conv_w,
                              m_gd_a_log, m_gd_dt_bias, m_gd_onorm)))
    v_in = dict(zip(WEIGHTS, (v_hgrn_lb, v_ada_w, v_ada_b, v_norm_g, v_hg_in_w, v_hg_out_w, v_hg_onorm, v_sw_in_w,
                              v_sw_out_w, v_sw_qnorm, v_sw_knorm, v_sw_sinks, v_gd_in_w, v_gd_out_w, v_gd_conv_w,
                              v_gd_a_log, v_gd_dt_bias, v_gd_onorm)))
    ax, ay, ac = _place()
    chip = 2 * ax + ay
    bidx = 4 * ax + 2 * ay + ac
    t = x.shape[1]
    x0, target = x[0], loss_target[0]

    c_all = _allgather8(jnp.pad(c, ((0, 7), (0, 0)))).reshape(8, 8, D)[:, 0, :]
    ada_b_cols = lax.dynamic_slice(ada_b, (0, chip * ADA_S), (4, ADA_S)).reshape(4, 1, ADA_S)
    mod_sh = _ada_fwd(c_all, ada_w, ada_b_cols)
    mod_g = _allgather8(mod_sh.reshape(32, ADA_S)).reshape(4, 2, 4, 8, ADA_S)[:, 0]
    mod = lax.dynamic_index_in_dim(mod_g, bidx, axis=2, keepdims=False).transpose(1, 0, 2).reshape(4, 3 * D)
    shift = [mod[l:l + 1, 0:D] for l in range(4)]
    scale = [mod[l:l + 1, D:2 * D] for l in range(4)]
    gate = [mod[l:l + 1, 2 * D:3 * D] for l in range(4)]

    h8 = jnp.concatenate([hgrn_lb, jnp.full((4, D), -1e30, F32)], axis=0)
    lb_all = _lb_fwd(h8)
    freq = ROPE_THETA ** (-jnp.arange(0, 64, 2, dtype=F32) / 64)
    cs = _rope_table(positions.reshape(t, 1), jnp.tile(freq, 4)[None])

    big_shapes = [w_in[k].shape for k in BIG]
    gathered = _chip_allgather(_pack_rows([w_in[k] for k in BIG]).astype(BF16))
    hg_in_k, hg_out_k, sw_in_k, sw_out_k, gd_in_k, gd_out_k = _unpack_rows(gathered, big_shapes)
    hg_in_f = hg_in_k.transpose(1, 2, 0, 3).reshape(2, D, 4 * D)
    hg_out_f = hg_out_k.transpose(1, 0, 2, 3).reshape(2, D, D)
    sw_in_f = _sw_cols(sw_in_k[:, 0].transpose(1, 0, 2).reshape(D, SW_N))
    sw_out_f = sw_out_k.reshape(D, D)
    gd_in_f = jnp.pad(gd_in_k[:, 0].transpose(1, 0, 2).reshape(D, 6176), ((0, 0), (0, GD_N - 6176)))
    gd_out_f = gd_out_k.reshape(GD_VW, D)
    win = [hg_in_f[0], sw_in_f, gd_in_f, hg_in_f[1]]
    wout = [hg_out_f[0], sw_out_f, gd_out_f, hg_out_f[1]]
    tn_in = [1024, 1280, 896, 1024]

    gq = jnp.stack([jnp.tile(sw_qnorm[0, :32], 16), jnp.tile(sw_qnorm[0, 32:], 16)])
    gk = jnp.stack([jnp.tile(sw_knorm[0, :32], 4), jnp.tile(sw_knorm[0, 32:], 4)])
    pad128 = lambda a: jnp.pad(a, ((0, 0), (0, HD - a.shape[1])))
    sinks, alog, dtb = pad128(sw_sinks), pad128(gd_a_log), pad128(gd_dt_bias)
    cw8 = jnp.pad(_chip_allgather(gd_conv_w[0]).transpose(1, 0, 2).reshape(4, GD_QKV), ((0, 4), (0, 0)))
    lbs = {0: lb_all[0:1], 3: lb_all[3:4]}

    xs, us, hs, ps, stss = [x0], [], [], [], []
    for l in range(4):
        u, h = _ln_mm(xs[l], norm_g[l:l + 1], scale[l], shift[l], win[l], tn_in[l])
        if l % 3 == 0:
            p, sts = _hg_fwd(u, lbs[l], hg_onorm[l // 3:l // 3 + 1])
        elif l % 3 == 1:
            p, sts = _sw_fwd(u, cs, gq, gk, sinks), None
        else:
            p, sts = _gd_fwd(u, cw8, alog, dtb, gd_onorm)
        xs.append(_mm_res(p, wout[l], xs[l], gate[l]))
        us.append(u), hs.append(h), ps.append(p), stss.append(sts)
    lpart, dx = _loss_grad(xs[4], target)
    loss = lax.psum(lpart[0, 0], ("x", "y", "c"))

    g_small = {}
    d_in, d_out, dmod, dnorm_g, dlb8, dgo_hg = [None] * 4, [None] * 4, [None] * 4, [None] * 4, jnp.zeros((8, D), F32), {}
    for l in (3, 2, 1, 0):
        dp = _mm_scaled(dx, gate[l], wout[l].T, 1024)
        d_out[l], dgate = _outgrad(_mm_tn_acc(ps[l], dx, 512), wout[l], gate[l])
        if l % 3 == 0:
            du, dlb, dgo_hg[l // 3] = _hg_bwd(us[l], stss[l], dp, lbs[l], hg_onorm[l // 3:l // 3 + 1])
            dlb8 = lax.dynamic_update_slice(dlb8, dlb, (l, 0))
        elif l % 3 == 1:
            du, dgq, dgk, dsk = _sw_bwd(us[l], cs, dp, gq, gk, sinks)
            g_small['sw_qnorm'] = jnp.concatenate([dgq[0].reshape(16, 32).sum(0), dgq[1].reshape(16, 32).sum(0)])
            g_small['sw_knorm'] = jnp.concatenate([dgk[0].reshape(4, 32).sum(0), dgk[1].reshape(4, 32).sum(0)])
            g_small['sw_sinks'] = dsk[0, :16]
        else:
            du, dcw, dalog, ddtb, g_small['gd_onorm'] = _gd_bwd(us[l], stss[l], dp, cw8, alog, dtb, gd_onorm)
            g_small['gd_conv_w'], g_small['gd_a_log'], g_small['gd_dt_bias'] = dcw[:4], dalog[0, :16], ddtb[0, :16]
        d_in[l] = _mm_tn_acc(hs[l], du, 896 if l == 2 else 512)
        dx, dvec = _inproj_bwd(du, win[l].T, xs[l], dx, norm_g[l:l + 1], scale[l], shift[l], tn_in[l])
        dnorm_g[l] = dvec[0:1]
        dmod[l] = jnp.concatenate([dvec[2:3], dvec[1:2], dgate[0:1]], axis=1)
    grad_x = dx[None]

    g_small['hgrn_lb'] = _lb_bwd(h8, dlb8)[0:4]
    g_small['norm_g'] = jnp.concatenate(dnorm_g, axis=0)
    g_small['hg_onorm'] = jnp.concatenate([dgo_hg[0], dgo_hg[1]], axis=0)
    gs_all = _allgather8(_pack_small(g_small))
    gs = _sum_rows([gs_all[32 * d:32 * (d + 1)] for d in range(8)])

    def small_view(packed, k):
        if k == 'hgrn_lb':
            return packed[0:4]
        if k == 'norm_g':
            return packed[4:8]
        off, size = SMALL_ROW[k]
        return packed[24, off:off + size].reshape(w_in[k].shape)

    conv_sl = lambda full: lax.dynamic_slice(full.reshape(4, GD_QKV), (0, chip * D), (4, D))
    out = {}

    def put(k, res, shape):
        for name, r in zip(('grad_', 'delta_', 'new_m_', 'new_v_'), res):
            out[name + k] = r.reshape(shape)

    zero_conv = dict(gd_conv_w=jnp.zeros((4, GD_QKV), F32))
    small_names = ['hgrn_lb', 'norm_g'] + list(SMALL_ROW)
    res = _adamw(_pack_small({**{k: w_in[k] for k in small_names}, **zero_conv}), (gs,),
                 _pack_small({**{k: m_in[k] for k in small_names}, **zero_conv}),
                 _pack_small({**{k: v_in[k] for k in small_names}, **zero_conv}))
    for k in small_names:
        put(k, [small_view(r, k) for r in res], w_in[k].shape)
    put('gd_conv_w', _adamw(gd_conv_w[0], (conv_sl(gs[8:24]),), m_in['gd_conv_w'][0], v_in['gd_conv_w'][0]),
        gd_conv_w.shape)

    dm = _allgather8(jnp.pad(jnp.concatenate(dmod, axis=0), ((0, 4), (0, 0)))).reshape(8, 8, 3 * D)[:, :4]
    dm = dm.transpose(1, 0, 2)
    g_ada_w, g_ada_b = _ada_bwd(c_all, lax.dynamic_slice(dm, (0, 0, chip * ADA_S), (4, 8, ADA_S)), dm)
    put('ada_w', _adamw(ada_w.reshape(4 * D, ADA_S), (g_ada_w.reshape(4 * D, ADA_S),),
                        m_in['ada_w'].reshape(4 * D, ADA_S), v_in['ada_w'].reshape(4 * D, ADA_S)), ada_w.shape)
    put('ada_b', _adamw(ada_b, (g_ada_b.reshape(4, 3 * D),), m_in['ada_b'], v_in['ada_b']), ada_b.shape)

    by_chip = lambda g, cols: g.reshape(g.shape[0], 4, cols).transpose(1, 0, 2)
    d_sw_in = _sw_cols(d_in[1], inverse=True)
    parts = {
        'hg_in_w': jnp.stack([by_chip(d_in[0], D), by_chip(d_in[3], D)], axis=1),
        'hg_out_w': jnp.stack([d_out[0].reshape(4, D // 4, D), d_out[3].reshape(4, D // 4, D)], axis=1),
        'sw_in_w': by_chip(d_sw_in, SW_N // 4)[:, None],
        'sw_out_w': d_out[1].reshape(4, 1, D // 4, D),
        'gd_in_w': by_chip(d_in[2][:, :6176], 1544)[:, None],
        'gd_out_w': d_out[2].reshape(4, 1, GD_VW // 4, D),
    }
    packed = jnp.stack([_pack_rows([parts[k][j] for k in BIG]) for j in range(4)])
    recv = _chip_scatter(packed)
    own = lax.dynamic_index_in_dim(packed, chip, axis=0, keepdims=False)
    half = _sum_rows([own, recv[0], recv[1], recv[2]])
    other = _sibling_swap(half)
    res = _adamw(_pack_rows([w_in[k] for k in BIG]), (half, other), _pack_rows([m_in[k] for k in BIG]),
                 _pack_rows([v_in[k] for k in BIG]))
    for name, r in zip(('grad_', 'delta_', 'new_m_', 'new_v_'), res):
        for k, a in zip(BIG, _unpack_rows(r, big_shapes)):
            out[name + k] = a

    return (loss, grad_x, *[out[p + k] for p in ('grad_', 'delta_', 'new_m_', 'new_v_') for k in WEIGHTS])
```

```python
import functools

import jax
import jax.numpy as jnp
from jax import lax
from jax.experimental import pallas as pl
from jax.experimental.pallas import tpu as pltpu

F32 = jnp.float32
BF16 = jnp.bfloat16
D = 1024
EPS = 1e-6
CHUNK = 64
SUB = 16
HG_H = 8
HD = 128
VMEM_LIMIT = 56 * 1024 * 1024


def _cparams(sem=None):
    return pltpu.CompilerParams(dimension_semantics=sem, vmem_limit_bytes=VMEM_LIMIT)


def _dot(a, b, ca, cb, prec=None):
    return lax.dot_general(a, b, (((ca,), (cb,)), ((), ())), precision=prec, preferred_element_type=F32)


def _mm(a, b):
    return _dot(a.astype(BF16), b.astype(BF16), 1, 0)


def _mm_nt(a, b):
    return _dot(a.astype(BF16), b.astype(BF16), 1, 1)


def _mm_tn(a, b):
    return _dot(a.astype(BF16), b.astype(BF16), 0, 0)


def _mm_f32(a, b):
    return _dot(a, b, 1, 0, lax.Precision.HIGHEST)


def _silu(x):
    return x * jax.nn.sigmoid(x)


def _hg_chunk(q_raw, f_pre, v, z, st, lb, go):
    c = q_raw.shape[0]
    nsub = c // SUB
    lf = jnp.log(lb + (1.0 - lb) * jax.nn.sigmoid(f_pre))
    k = (1.0 - lb) * jax.nn.sigmoid(-f_pre)
    q = _silu(q_raw)
    ti = lax.broadcasted_iota(jnp.int32, (c, c), 0)
    si = lax.broadcasted_iota(jnp.int32, (c, c), 1)
    mats = [(si <= ti).astype(F32)] + [(si <= SUB * i + SUB // 2).astype(F32) for i in range(nsub)]
    cums = jnp.split(_mm_f32(jnp.concatenate(mats, axis=0), lf), nsub + 1, axis=0)
    b, bmid = cums[0], cums[1:]
    row = lax.broadcasted_iota(jnp.int32, (c, 1), 0)
    ref = sum(jnp.where((row >= SUB * i) & (row < SUB * (i + 1)), bmid[i], 0.0) for i in range(nsub))
    qt = q * jnp.exp(b - ref)
    kall = jnp.concatenate(
        [k * jnp.exp(jnp.where(row < SUB * (i + 1), bmid[i] - b, -jnp.inf)) for i in range(nsub)], axis=0)
    v4 = jnp.concatenate([v] * nsub, axis=0)
    b_last = jnp.sum(lf, axis=0, keepdims=True)
    qb = q * jnp.exp(b)
    kd = k * jnp.exp(b_last - b)
    e_last = jnp.exp(b_last)
    tq = lax.broadcasted_iota(jnp.int32, (c, nsub * c), 0)
    cq = lax.broadcasted_iota(jnp.int32, (c, nsub * c), 1)
    m_all = ((cq // c) == (tq // SUB)) & ((cq % c) <= tq)
    hs = lambda a: jnp.split(a, HG_H, axis=1)
    qt_h, kall_h, v4_h, qb_h, kd_h, v_h, z_h, el_h = map(hs, (qt, kall, v4, qb, kd, v, z, e_last))
    st_h = jnp.split(st, HG_H, axis=0)
    p_out, st_out = [], []
    for h in range(HG_H):
        pm = jnp.where(m_all, _mm_nt(qt_h[h], kall_h[h]), 0.0)
        o = _mm(pm, v4_h[h]) + _mm_nt(qb_h[h], st_h[h])
        st_out.append(el_h[h] * st_h[h] + _mm_tn(v_h[h], kd_h[h]))
        y = o * lax.rsqrt(jnp.mean(o * o, axis=1, keepdims=True) + EPS) * go
        p_out.append(y * _silu(z_h[h]))
    return jnp.concatenate(p_out, axis=1), jnp.concatenate(st_out, axis=0)


def _hg_fwd(u, lb, go):
    t = u.shape[0]
    n = t // CHUNK

    def body(u_ref, lb_ref, go_ref, p_ref, sts_ref, st_ref):
        @pl.when(pl.program_id(0) == 0)
        def _():
            st_ref[...] = jnp.zeros_like(st_ref)

        st = st_ref[...]
        sts_ref[0] = st
        p, st_next = _hg_chunk(u_ref[:, 0:D], u_ref[:, D:2 * D], u_ref[:, 2 * D:3 * D], u_ref[:, 3 * D:4 * D],
                               st, lb_ref[...], go_ref[...])
        p_ref[...] = p.astype(BF16)
        st_ref[...] = st_next

    return pl.pallas_call(
        body, name="hg_fwd", grid=(n,),
        in_specs=[pl.BlockSpec((CHUNK, 4 * D), lambda i: (i, 0)),
                  pl.BlockSpec((1, D), lambda i: (0, 0)),
                  pl.BlockSpec((1, HD), lambda i: (0, 0))],
        out_specs=[pl.BlockSpec((CHUNK, D), lambda i: (i, 0)),
                   pl.BlockSpec((1, HG_H * HD, HD), lambda i: (i, 0, 0))],
        out_shape=[jax.ShapeDtypeStruct((t, D), BF16), jax.ShapeDtypeStruct((n, HG_H * HD, HD), F32)],
        scratch_shapes=[pltpu.VMEM((HG_H * HD, HD), F32)],
        compiler_params=_cparams(("arbitrary",)),
    )(u, lb, go)


def _hg_bwd(u, sts, dp, lb, go):
    t = u.shape[0]
    n = t // CHUNK

    def body(u_ref, sts_ref, dp_ref, lb_ref, go_ref, du_ref, dlb_ref, dgo_ref, dst_ref):
        @pl.when(pl.program_id(0) == 0)
        def _():
            dst_ref[...] = jnp.zeros_like(dst_ref)
            dlb_ref[...] = jnp.zeros_like(dlb_ref)
            dgo_ref[...] = jnp.zeros_like(dgo_ref)

        _, vjp = jax.vjp(_hg_chunk, u_ref[:, 0:D], u_ref[:, D:2 * D], u_ref[:, 2 * D:3 * D], u_ref[:, 3 * D:4 * D],
                         sts_ref[0], lb_ref[...], go_ref[...])
        dq, df, dv, dz, dst, dlb, dgo = vjp((dp_ref[...].astype(F32), dst_ref[...]))
        du_ref[:, 0:D] = dq.astype(BF16)
        du_ref[:, D:2 * D] = df.astype(BF16)
        du_ref[:, 2 * D:3 * D] = dv.astype(BF16)
        du_ref[:, 3 * D:4 * D] = dz.astype(BF16)
        dst_ref[...] = dst
        dlb_ref[...] += dlb
        dgo_ref[...] += dgo

    rev = lambda i: (n - 1 - i, 0)
    return pl.pallas_call(
        body, name="hg_bwd", grid=(n,),
        in_specs=[pl.BlockSpec((CHUNK, 4 * D), rev),
                  pl.BlockSpec((1, HG_H * HD, HD), lambda i: (n - 1 - i, 0, 0)),
                  pl.BlockSpec((CHUNK, D), rev),
                  pl.BlockSpec((1, D), lambda i: (0, 0)),
                  pl.BlockSpec((1, HD), lambda i: (0, 0))],
        out_specs=[pl.BlockSpec((CHUNK, 4 * D), rev),
                   pl.BlockSpec((1, D), lambda i: (0, 0)),
                   pl.BlockSpec((1, HD), lambda i: (0, 0))],
        out_shape=[jax.ShapeDtypeStruct((t, 4 * D), BF16), jax.ShapeDtypeStruct((1, D), F32),
                   jax.ShapeDtypeStruct((1, HD), F32)],
        scratch_shapes=[pltpu.VMEM((HG_H * HD, HD), F32)],
        compiler_params=_cparams(("arbitrary",)),
    )(u, sts, dp, lb, go)


GD_VH = 16
GD_QKH = 8
GD_QKV = 4096
GD_VW = 2048
GD_N = GD_QKV + GD_VW + HD
GD_GRP = 4
GD_SOLVE = (GD_VH // GD_GRP, GD_GRP * CHUNK, 2 * HD)
HALO = 8


def _mm_high(a, b):
    return _dot(a, b, 1, 0, lax.Precision.HIGH)


def _lane_pick(a, h):
    lane = lax.broadcasted_iota(jnp.int32, a.shape, 1)
    return jnp.sum(jnp.where(lane == h, a, 0.0), axis=1, keepdims=True)


def _l2n(x):
    return x * lax.rsqrt(jnp.sum(x * x, axis=1, keepdims=True) + EPS)


def _solve_fwd(a_mat, rhs):
    n = a_mat.shape[0]
    r_i, c_i = lax.broadcasted_iota(jnp.int32, (n, n), 0), lax.broadcasted_iota(jnp.int32, (n, n), 1)
    same = lambda nb: (r_i // nb) == (c_i // nb)
    d0 = jnp.where(same(8), a_mat, 0.0)
    d2 = _mm_high(d0, d0)
    tinv = (r_i == c_i).astype(F32) - d0
    tinv = tinv + _mm_high(tinv, d2)
    tinv = tinv + _mm_high(tinv, _mm_high(d2, d2))
    nb = 16
    while nb <= CHUNK:
        low = jnp.where(same(nb) & ~same(nb // 2), a_mat, 0.0)
        tinv = tinv - _mm(_mm(tinv, low), tinv)
        nb *= 2
    x = _mm_high(tinv, rhs)
    return x, (tinv, x)


def _solve_bwd(res, dx):
    tinv, x = res
    drhs = _dot(tinv, dx, 0, 0, lax.Precision.HIGH)
    return -_dot(drhs, x, 1, 1, lax.Precision.HIGH), drhs


@jax.custom_vjp
def _solved(a_mat, rhs, tinv, x):
    return x


_solved.defvjp(lambda a_mat, rhs, tinv, x: (x, (tinv, x)),
               lambda res, dx: _solve_bwd(res, dx) + (jnp.zeros_like(res[0]), jnp.zeros_like(res[1])))


def _gd_chunk(xh, x, z, ab, st, cw, alog, dtb, go, solve):
    c = x.shape[0]
    xa = jnp.concatenate([xh, x], axis=0)
    ti = lax.broadcasted_iota(jnp.int32, (3 * c, c + HALO), 0)
    si = lax.broadcasted_iota(jnp.int32, (3 * c, c + HALO), 1)
    shift = (si == (ti % c) + (ti // c) + HALO - 3).astype(F32)
    sh = jnp.split(_mm_high(shift, xa), 3, axis=0)
    qkv = _silu(cw[0:1] * sh[0] + cw[1:2] * sh[1] + cw[2:3] * sh[2] + cw[3:4] * x)
    q_all, k_all, v_all = jnp.split(qkv, [1024, 2048], axis=1)
    lane = lax.broadcasted_iota(jnp.int32, (c, HD), 1)
    a_part = jnp.where(lane < GD_VH, ab, 0.0)
    g_all = -jnp.exp(alog) * jax.nn.softplus(a_part + dtb)
    tri = (lax.broadcasted_iota(jnp.int32, (c, c), 1) <= lax.broadcasted_iota(jnp.int32, (c, c), 0))
    d_all = _mm_f32(tri.astype(F32), g_all)
    dl_all = jnp.sum(g_all, axis=0, keepdims=True)
    beta_all = jax.nn.sigmoid(ab)
    gc = GD_GRP * c
    r_i, c_i = lax.broadcasted_iota(jnp.int32, (gc, gc), 0), lax.broadcasted_iota(jnp.int32, (gc, gc), 1)
    same_head = (r_i // c) == (c_i // c)
    tri_g, strict_g = same_head & (c_i <= r_i), same_head & (c_i < r_i)
    ones = jnp.ones((gc, HD), F32)
    lane_g = lax.broadcasted_iota(jnp.int32, (gc, HD), 1)
    qs = jnp.split(q_all, GD_QKH, axis=1)
    ks = jnp.split(k_all, GD_QKH, axis=1)
    vs = jnp.split(v_all, GD_VH, axis=1)
    zs = jnp.split(z, GD_VH, axis=1)
    sts = jnp.split(st, GD_VH, axis=0)
    qn = [_l2n(a) * (HD ** -0.5) for a in qs]
    kn = [_l2n(a) for a in ks]
    p_out, st_out = [], []
    for g in range(GD_VH // GD_GRP):
        heads = range(GD_GRP * g, GD_GRP * (g + 1))
        stack = lambda f: jnp.concatenate([f(h) for h in heads], axis=0)
        q_, k_, v_ = stack(lambda h: qn[h // 2]), stack(lambda h: kn[h // 2]), stack(lambda h: vs[h])
        dcol = stack(lambda h: _lane_pick(d_all, h))
        bcol = stack(lambda h: _lane_pick(beta_all, GD_VH + h))
        dlast = stack(lambda h: jnp.broadcast_to(_lane_pick(dl_all, h), (c, 1)))
        drow = _dot(ones, jnp.where(lane_g == 0, dcol, 0.0), 1, 1, lax.Precision.HIGHEST)
        dec = jnp.exp(jnp.where(tri_g, dcol - drow, -jnp.inf))
        kb = k_ * bcol
        a_mat = jnp.where(strict_g, _mm_nt(kb, k_) * dec, 0.0)
        xsol = solve(g, a_mat, jnp.concatenate([v_ * bcol, kb * jnp.exp(dcol)], axis=1))
        u_, w_ = jnp.split(xsol, 2, axis=1)
        w_h = jnp.split(w_, GD_GRP, axis=0)
        v_new = u_ - jnp.concatenate([_mm(w_h[i], sts[h]) for i, h in enumerate(heads)], axis=0)
        qd_h = jnp.split(q_ * jnp.exp(dcol), GD_GRP, axis=0)
        o_g = _mm(_mm_nt(q_, k_) * dec, v_new) + jnp.concatenate(
            [_mm(qd_h[i], sts[h]) for i, h in enumerate(heads)], axis=0)
        kd_h = jnp.split(k_ * jnp.exp(dlast - dcol), GD_GRP, axis=0)
        vn_h = jnp.split(v_new, GD_GRP, axis=0)
        o_h = jnp.split(o_g, GD_GRP, axis=0)
        for i, h in enumerate(heads):
            st_out.append(sts[h] * jnp.exp(_lane_pick(dl_all, h)) + _mm_tn(kd_h[i], vn_h[i]))
            o = o_h[i]
            y = o * lax.rsqrt(jnp.mean(o * o, axis=1, keepdims=True) + EPS) * go
            p_out.append(y * _silu(zs[h]))
    return jnp.concatenate(p_out, axis=1), jnp.concatenate(st_out, axis=0)


def _gd_specs(n, rev):
    ci = (lambda i: n - 1 - i) if rev else (lambda i: i)
    return [pl.BlockSpec((HALO, GD_QKV), lambda i: (jnp.maximum(ci(i) * (CHUNK // HALO) - 1, 0), 0)),
            pl.BlockSpec((CHUNK, GD_N), lambda i: (ci(i), 0))]


def _gd_load(uh_ref, u_ref, first):
    xh = jnp.where(first, 0.0, uh_ref[...])
    return xh, u_ref[:, 0:GD_QKV], u_ref[:, GD_QKV:GD_QKV + GD_VW], u_ref[:, GD_QKV + GD_VW:GD_N]


def _gd_fwd(u, cw, alog, dtb, go):
    t = u.shape[0]
    n = t // CHUNK
    small = lambda r, w: pl.BlockSpec((r, w), lambda i: (0, 0))

    def body(uh_ref, u_ref, cw_ref, alog_ref, dtb_ref, go_ref, p_ref, sts_ref, tinv_ref, xsol_ref, st_ref):
        i = pl.program_id(0)

        @pl.when(i == 0)
        def _():
            st_ref[...] = jnp.zeros_like(st_ref)

        def solve(g, a_mat, rhs):
            xsol, (tinv, _) = _solve_fwd(a_mat, rhs)
            tinv_ref[0, g] = tinv
            xsol_ref[0, g] = xsol
            return xsol

        st = st_ref[...]
        sts_ref[0] = st
        p, st_next = _gd_chunk(*_gd_load(uh_ref, u_ref, i == 0), st, cw_ref[...], alog_ref[...], dtb_ref[...],
                               go_ref[...], solve)
        p_ref[...] = p.astype(BF16)
        st_ref[...] = st_next

    return pl.pallas_call(
        body, name="gd_fwd", grid=(n,),
        in_specs=_gd_specs(n, False) + [small(8, GD_QKV), small(1, HD), small(1, HD), small(1, HD)],
        out_specs=[pl.BlockSpec((CHUNK, GD_VW), lambda i: (i, 0)),
                   pl.BlockSpec((1, GD_VH * HD, HD), lambda i: (i, 0, 0)),
                   pl.BlockSpec((1,) + GD_SOLVE, lambda i: (i, 0, 0, 0)),
                   pl.BlockSpec((1,) + GD_SOLVE, lambda i: (i, 0, 0, 0))],
        out_shape=[jax.ShapeDtypeStruct((t, GD_VW), BF16), jax.ShapeDtypeStruct((n, GD_VH * HD, HD), F32),
                   jax.ShapeDtypeStruct((n,) + GD_SOLVE, F32), jax.ShapeDtypeStruct((n,) + GD_SOLVE, F32)],
        scratch_shapes=[pltpu.VMEM((GD_VH * HD, HD), F32)],
        compiler_params=_cparams(("arbitrary",)),
    )(u, u, cw, alog, dtb, go)


def _gd_bwd(u, sts, tinvs, xsols, dp, cw, alog, dtb, go):
    t = u.shape[0]
    n = t // CHUNK
    small = lambda r, w: pl.BlockSpec((r, w), lambda i: (0, 0))

    def body(uh_ref, u_ref, sts_ref, tinv_ref, xsol_ref, dp_ref, cw_ref, alog_ref, dtb_ref, go_ref,
             du_ref, dcw_ref, dalog_ref, ddtb_ref, dgo_ref, dst_ref, dhalo_ref):
        i = pl.program_id(0)

        @pl.when(i == 0)
        def _():
            for r in (dst_ref, dhalo_ref, dcw_ref, dalog_ref, ddtb_ref, dgo_ref):
                r[...] = jnp.zeros_like(r)

        solve = lambda g, a_mat, rhs: _solved(a_mat, rhs, tinv_ref[0, g], xsol_ref[0, g])
        chunk = functools.partial(_gd_chunk, solve=solve)
        _, vjp = jax.vjp(chunk, *_gd_load(uh_ref, u_ref, i == n - 1), sts_ref[0], cw_ref[...], alog_ref[...],
                         dtb_ref[...], go_ref[...])
        dxh, dx, dz, dab, dst, dcw, dalog, ddtb, dgo = vjp((dp_ref[...].astype(F32), dst_ref[...]))
        tail = jnp.concatenate([jnp.zeros((CHUNK - HALO, GD_QKV), F32), dhalo_ref[...]], axis=0)
        du_ref[:, 0:GD_QKV] = (dx + tail).astype(BF16)
        du_ref[:, GD_QKV:GD_QKV + GD_VW] = dz.astype(BF16)
        du_ref[:, GD_QKV + GD_VW:GD_N] = dab.astype(BF16)
        dhalo_ref[...] = dxh
        dst_ref[...] = dst
        dcw_ref[...] += dcw
        dalog_ref[...] += dalog
        ddtb_ref[...] += ddtb
        dgo_ref[...] += dgo

    return pl.pallas_call(
        body, name="gd_bwd", grid=(n,),
        in_specs=_gd_specs(n, True) + [pl.BlockSpec((1, GD_VH * HD, HD), lambda i: (n - 1 - i, 0, 0)),
                                       pl.BlockSpec((1,) + GD_SOLVE, lambda i: (n - 1 - i, 0, 0, 0)),
                                       pl.BlockSpec((1,) + GD_SOLVE, lambda i: (n - 1 - i, 0, 0, 0)),
                                       pl.BlockSpec((CHUNK, GD_VW), lambda i: (n - 1 - i, 0)),
                                       small(8, GD_QKV), small(1, HD), small(1, HD), small(1, HD)],
        out_specs=[pl.BlockSpec((CHUNK, GD_N), lambda i: (n - 1 - i, 0)),
                   small(8, GD_QKV), small(1, HD), small(1, HD), small(1, HD)],
        out_shape=[jax.ShapeDtypeStruct((t, GD_N), BF16), jax.ShapeDtypeStruct((8, GD_QKV), F32)]
        + [jax.ShapeDtypeStruct((1, HD), F32)] * 3,
        scratch_shapes=[pltpu.VMEM((GD_VH * HD, HD), F32), pltpu.VMEM((HALO, GD_QKV), F32)],
        compiler_params=_cparams(("arbitrary",)),
    )(u, u, sts, tinvs, xsols, dp, cw, alog, dtb, go)


SW_B = 128
SW_H = 16
SW_G = 4
SW_N = 2560
SW_KV0 = 1024


def _blockdiag(n, blk):
    r = lax.broadcasted_iota(jnp.int32, (n, n), 0) // blk
    c = lax.broadcasted_iota(jnp.int32, (n, n), 1) // blk
    return (r == c).astype(F32)


def _sw_normrope(x, g1, g2, cos, sin):
    w = x.shape[1] // 2
    x1, x2 = jnp.split(x, 2, axis=1)
    ms = _mm_high(x1 * x1 + x2 * x2, _blockdiag(w, 32)) * (1.0 / 64.0)
    rinv = lax.rsqrt(ms + EPS)
    n1, n2 = x1 * rinv * g1, x2 * rinv * g2
    return jnp.concatenate([n1 * cos - n2 * sin, n2 * cos + n1 * sin], axis=1)


def _sw_block(q, kvp, kvc, z, csp, csc, gq, gk, sinks, has_prev):
    b = q.shape[0]
    cos_c, sin_c = jnp.split(csc, 2, axis=1)
    cos_p, sin_p = jnp.split(csp, 2, axis=1)
    tile4 = lambda a: jnp.concatenate([a] * 4, axis=1)
    qh = _sw_normrope(q, gq[0:1], gq[1:2], tile4(cos_c), tile4(sin_c))
    kp, vp = jnp.split(kvp, 2, axis=1)
    kc, vc = jnp.split(kvc, 2, axis=1)
    kh = jnp.concatenate([_sw_normrope(kp, gk[0:1], gk[1:2], cos_p, sin_p),
                          _sw_normrope(kc, gk[0:1], gk[1:2], cos_c, sin_c)], axis=0)
    vv = jnp.concatenate([vp, vc], axis=0)
    q1, q2 = jnp.split(qh, 2, axis=1)
    q1g, q2g = jnp.split(q1, SW_G, axis=1), jnp.split(q2, SW_G, axis=1)
    qi = lax.broadcasted_iota(jnp.int32, (4 * b, 2 * b), 0) % b
    kj = lax.broadcasted_iota(jnp.int32, (4 * b, 2 * b), 1)
    rel = qi + b - kj
    mask = (rel >= 0) & (rel < SW_B) & (has_prev | (kj >= b))
    ri = lax.broadcasted_iota(jnp.int32, (256, 256), 0)
    ci = lax.broadcasted_iota(jnp.int32, (256, 256), 1)
    row_head = lax.broadcasted_iota(jnp.int32, (4 * b, 256), 0) // b
    lane_q = lax.broadcasted_iota(jnp.int32, (4 * b, 256), 1)
    q_sel = (lane_q % 128) // 32 == row_head
    o_sel = lane_q // 64 == row_head
    o_out = []
    for g in range(SW_G):
        ek = ((ri // 128 == ci // 128) & ((ri % 128) // 32 == g) & (ri % 32 == ci % 32)).astype(F32)
        ev = ((ri // 64 == g) & (ri % 64 == ci % 64)).astype(F32)
        kx = _mm(kh, ek)
        vx = _mm(vv, ev)
        qg = jnp.concatenate([q1g[g], q2g[g]], axis=1)
        q4 = jnp.where(q_sel, jnp.concatenate([qg] * 4, axis=0), 0.0)
        sink = jnp.concatenate([jnp.broadcast_to(_lane_pick(sinks, 4 * g + j), (b, 1)) for j in range(4)], axis=0)
        s = jnp.where(mask, _mm_nt(q4, kx) * (64 ** -0.5), -jnp.inf)
        m = jnp.maximum(jnp.max(s, axis=1, keepdims=True), sink)
        p = jnp.exp(s - m)
        pn = p / (jnp.sum(p, axis=1, keepdims=True) + jnp.exp(sink - m))
        o4 = jnp.split(jnp.where(o_sel, _mm(pn, vx), 0.0), 4, axis=0)
        o_out.append(o4[0] + o4[1] + o4[2] + o4[3])
    return jnp.concatenate(o_out, axis=1) * _silu(z)


def _sw_specs(n, rev):
    ci = (lambda i: n - 1 - i) if rev else (lambda i: i)
    prev = lambda i: jnp.maximum(ci(i) - 1, 0)
    return [pl.BlockSpec((SW_B, SW_N), lambda i: (ci(i), 0)),
            pl.BlockSpec((SW_B, 512), lambda i: (prev(i), SW_KV0 // 512)),
            pl.BlockSpec((SW_B, 256), lambda i: (ci(i), 0)),
            pl.BlockSpec((SW_B, 256), lambda i: (prev(i), 0)),
            pl.BlockSpec((2, 512), lambda i: (0, 0)), pl.BlockSpec((2, 128), lambda i: (0, 0)),
            pl.BlockSpec((1, 128), lambda i: (0, 0))]


def _sw_args(u_ref, kvp_ref, csc_ref, csp_ref, gq_ref, gk_ref, sk_ref, has_prev):
    return (u_ref[:, 0:D], kvp_ref[...], u_ref[:, SW_KV0:SW_KV0 + 512], u_ref[:, SW_KV0 + 512:SW_N],
            csp_ref[...], csc_ref[...], gq_ref[...], gk_ref[...], sk_ref[...], has_prev)


def _sw_fwd(u, cs, gq, gk, sinks):
    t = u.shape[0]
    n = t // SW_B

    def body(u_ref, kvp_ref, csc_ref, csp_ref, gq_ref, gk_ref, sk_ref, p_ref):
        has_prev = pl.program_id(0) > 0
        p_ref[...] = _sw_block(*_sw_args(u_ref, kvp_ref, csc_ref, csp_ref, gq_ref, gk_ref, sk_ref, has_prev)
                               ).astype(BF16)

    return pl.pallas_call(
        body, name="sw_fwd", grid=(n,), in_specs=_sw_specs(n, False),
        out_specs=pl.BlockSpec((SW_B, D), lambda i: (i, 0)),
        out_shape=jax.ShapeDtypeStruct((t, D), BF16),
        compiler_params=_cparams(("arbitrary",)),
    )(u, u, cs, cs, gq, gk, sinks)


def _sw_bwd(u, cs, dp, gq, gk, sinks):
    t = u.shape[0]
    n = t // SW_B

    def body(u_ref, kvp_ref, csc_ref, csp_ref, gq_ref, gk_ref, sk_ref, dp_ref,
             du_ref, dgq_ref, dgk_ref, dsk_ref, dkv_ref):
        i = pl.program_id(0)

        @pl.when(i == 0)
        def _():
            for r in (dkv_ref, dgq_ref, dgk_ref, dsk_ref):
                r[...] = jnp.zeros_like(r)

        has_prev = i < n - 1
        args = _sw_args(u_ref, kvp_ref, csc_ref, csp_ref, gq_ref, gk_ref, sk_ref, has_prev)
        fn = lambda q, kvp, kvc, z, gq_, gk_, sk_: _sw_block(q, kvp, kvc, z, args[4], args[5], gq_, gk_, sk_, has_prev)
        _, vjp = jax.vjp(fn, args[0], args[1], args[2], args[3], args[6], args[7], args[8])
        dq, dkvp, dkvc, dz, dgq, dgk, dsk = vjp(dp_ref[...].astype(F32))
        du_ref[:, 0:D] = dq.astype(BF16)
        du_ref[:, SW_KV0:SW_KV0 + 512] = (dkvc + dkv_ref[...]).astype(BF16)
        du_ref[:, SW_KV0 + 512:SW_N] = dz.astype(BF16)
        dkv_ref[...] = dkvp
        dgq_ref[...] += dgq
        dgk_ref[...] += dgk
        dsk_ref[...] += dsk

    small = lambda r, w: pl.BlockSpec((r, w), lambda i: (0, 0))
    return pl.pallas_call(
        body, name="sw_bwd", grid=(n,),
        in_specs=_sw_specs(n, True) + [pl.BlockSpec((SW_B, D), lambda i: (n - 1 - i, 0))],
        out_specs=[pl.BlockSpec((SW_B, SW_N), lambda i: (n - 1 - i, 0)), small(2, 512), small(2, 128), small(1, 128)],
        out_shape=[jax.ShapeDtypeStruct((t, SW_N), BF16), jax.ShapeDtypeStruct((2, 512), F32),
                   jax.ShapeDtypeStruct((2, 128), F32), jax.ShapeDtypeStruct((1, 128), F32)],
        scratch_shapes=[pltpu.VMEM((SW_B, 512), F32)],
        compiler_params=_cparams(("arbitrary",)),
    )(u, u, cs, cs, gq, gk, sinks, dp)


def _ln_mod(x, g, scale, shift):
    y = x * lax.rsqrt(jnp.mean(x * x, axis=1, keepdims=True) + EPS) * g
    return y * (1.0 + scale) + shift


def _row_tile(t):
    return min(t, 1024)


def _ln_mm(x, g, scale, shift, w, tn):
    t, n = x.shape[0], w.shape[1]
    tm = _row_tile(t)
    vec = pl.BlockSpec((1, D), lambda i, j: (0, 0))

    def body(x_ref, g_ref, sc_ref, sh_ref, w_ref, u_ref, h_ref):
        @pl.when(pl.program_id(1) == 0)
        def _():
            h_ref[...] = _ln_mod(x_ref[...], g_ref[...], sc_ref[...], sh_ref[...]).astype(BF16)

        u_ref[...] = _dot(h_ref[...], w_ref[...], 1, 0)

    return pl.pallas_call(
        body, name="ln_mm", grid=(t // tm, n // tn),
        in_specs=[pl.BlockSpec((tm, D), lambda i, j: (i, 0)), vec, vec, vec,
                  pl.BlockSpec((D, tn), lambda i, j: (0, j))],
        out_specs=[pl.BlockSpec((tm, tn), lambda i, j: (i, j)), pl.BlockSpec((tm, D), lambda i, j: (i, 0))],
        out_shape=[jax.ShapeDtypeStruct((t, n), F32), jax.ShapeDtypeStruct((t, D), BF16)],
        compiler_params=_cparams(("arbitrary", "arbitrary")),
    )(x, g, scale, shift, w)


def _mm_res(p, w, x, gate):
    t, k = p.shape
    tm = _row_tile(t)

    def body(p_ref, w_ref, x_ref, gate_ref, o_ref):
        o_ref[...] = x_ref[...] + gate_ref[...] * _dot(p_ref[...], w_ref[...], 1, 0)

    return pl.pallas_call(
        body, name="mm_res", grid=(t // tm,),
        in_specs=[pl.BlockSpec((tm, k), lambda i: (i, 0)), pl.BlockSpec((k, D), lambda i: (0, 0)),
                  pl.BlockSpec((tm, D), lambda i: (i, 0)), pl.BlockSpec((1, D), lambda i: (0, 0))],
        out_specs=pl.BlockSpec((tm, D), lambda i: (i, 0)),
        out_shape=jax.ShapeDtypeStruct((t, D), F32),
        compiler_params=_cparams(("arbitrary",)),
    )(p, w, x, gate)


def _loss_grad(x, target):
    t = x.shape[0]
    tm = _row_tile(t)

    def body(x_ref, t_ref, l_ref, dx_ref):
        @pl.when(pl.program_id(0) == 0)
        def _():
            l_ref[...] = jnp.zeros_like(l_ref)

        err = x_ref[...] - t_ref[...]
        dx_ref[...] = err * (1.0 / D)
        l_ref[...] += 0.5 * jnp.sum(jnp.mean(err * err, axis=1, keepdims=True), axis=0, keepdims=True)

    return pl.pallas_call(
        body, name="loss_grad", grid=(t // tm,),
        in_specs=[pl.BlockSpec((tm, D), lambda i: (i, 0))] * 2,
        out_specs=[pl.BlockSpec((8, 128), lambda i: (0, 0)), pl.BlockSpec((tm, D), lambda i: (i, 0))],
        out_shape=[jax.ShapeDtypeStruct((8, 128), F32), jax.ShapeDtypeStruct((t, D), F32)],
        compiler_params=_cparams(("arbitrary",)),
    )(x, target)


def _mm_scaled(a, s, w, tn):
    t, k = a.shape
    n = w.shape[1]
    tm = _row_tile(t)

    def body(a_ref, s_ref, w_ref, o_ref):
        o_ref[...] = _dot((a_ref[...] * s_ref[...]).astype(BF16), w_ref[...], 1, 0).astype(BF16)

    return pl.pallas_call(
        body, name="mm_scaled", grid=(t // tm, n // tn),
        in_specs=[pl.BlockSpec((tm, k), lambda i, j: (i, 0)), pl.BlockSpec((1, k), lambda i, j: (0, 0)),
                  pl.BlockSpec((k, tn), lambda i, j: (0, j))],
        out_specs=pl.BlockSpec((tm, tn), lambda i, j: (i, j)),
        out_shape=jax.ShapeDtypeStruct((t, n), BF16),
        compiler_params=_cparams(("arbitrary", "arbitrary")),
    )(a, s, w)


def _mm_tn_acc(a, b, tn):
    t, m = a.shape
    n = b.shape[1]
    tk = min(t, 1024)
    nk = t // tk

    def body(a_ref, b_ref, o_ref):
        @pl.when(pl.program_id(1) == 0)
        def _():
            o_ref[...] = jnp.zeros_like(o_ref)

        o_ref[...] += _dot(a_ref[...], b_ref[...].astype(BF16), 0, 0)

    return pl.pallas_call(
        body, name="mm_tn_acc", grid=(n // tn, nk),
        in_specs=[pl.BlockSpec((tk, m), lambda j, k: (k, 0)), pl.BlockSpec((tk, tn), lambda j, k: (k, j))],
        out_specs=pl.BlockSpec((m, tn), lambda j, k: (0, j)),
        out_shape=jax.ShapeDtypeStruct((m, n), F32),
        compiler_params=_cparams(("arbitrary", "arbitrary")),
    )(a, b)


def _inproj_bwd(du, wt, x, dxp, g, scale, shift, tk):
    t, kdim = du.shape
    tm = min(t, 512)
    nk = kdim // tk
    vec = pl.BlockSpec((1, D), lambda i, k: (0, 0))

    def body(du_ref, wt_ref, x_ref, dxp_ref, g_ref, sc_ref, sh_ref, dx_ref, dv_ref, acc_ref):
        k = pl.program_id(1)

        @pl.when((pl.program_id(0) == 0) & (k == 0))
        def _():
            dv_ref[...] = jnp.zeros_like(dv_ref)

        @pl.when(k == 0)
        def _():
            acc_ref[...] = jnp.zeros_like(acc_ref)

        acc_ref[...] += _dot(du_ref[...].astype(BF16), wt_ref[...], 1, 0)

        @pl.when(k == nk - 1)
        def _():
            _, vjp = jax.vjp(_ln_mod, x_ref[...], g_ref[...], sc_ref[...], sh_ref[...])
            dx, dg, dsc, dsh = vjp(acc_ref[...])
            dx_ref[...] = dxp_ref[...] + dx
            dv_ref[0:1, :] += dg
            dv_ref[1:2, :] += dsc
            dv_ref[2:3, :] += dsh

    return pl.pallas_call(
        body, name="inproj_bwd", grid=(t // tm, nk),
        in_specs=[pl.BlockSpec((tm, tk), lambda i, k: (i, k)), pl.BlockSpec((tk, D), lambda i, k: (k, 0)),
                  pl.BlockSpec((tm, D), lambda i, k: (i, 0)), pl.BlockSpec((tm, D), lambda i, k: (i, 0)),
                  vec, vec, vec],
        out_specs=[pl.BlockSpec((tm, D), lambda i, k: (i, 0)), pl.BlockSpec((8, D), lambda i, k: (0, 0))],
        out_shape=[jax.ShapeDtypeStruct((t, D), F32), jax.ShapeDtypeStruct((8, D), F32)],
        scratch_shapes=[pltpu.VMEM((tm, D), F32)],
        compiler_params=_cparams(("arbitrary", "arbitrary")),
    )(du, wt, x, dxp, g, scale, shift)


def _outgrad(gmat, w, gate):
    k = gmat.shape[0]
    tr = 256

    def body(g_ref, w_ref, gate_ref, dw_ref, dg_ref):
        @pl.when(pl.program_id(0) == 0)
        def _():
            dg_ref[...] = jnp.zeros_like(dg_ref)

        gm = g_ref[...]
        dw_ref[...] = gm * gate_ref[...]
        dg_ref[0:1, :] += jnp.sum(gm * w_ref[...].astype(F32), axis=0, keepdims=True)

    return pl.pallas_call(
        body, name="outgrad", grid=(k // tr,),
        in_specs=[pl.BlockSpec((tr, D), lambda i: (i, 0)), pl.BlockSpec((tr, D), lambda i: (i, 0)),
                  pl.BlockSpec((1, D), lambda i: (0, 0))],
        out_specs=[pl.BlockSpec((tr, D), lambda i: (i, 0)), pl.BlockSpec((8, D), lambda i: (0, 0))],
        out_shape=[jax.ShapeDtypeStruct((k, D), F32), jax.ShapeDtypeStruct((8, D), F32)],
        compiler_params=_cparams(("arbitrary",)),
    )(gmat, w, gate)


def _rope_table(pos, freq):
    t = pos.shape[0]
    tm = _row_tile(t)

    def body(p_ref, f_ref, o_ref):
        ang = p_ref[...].astype(F32) * f_ref[...]
        o_ref[:, 0:128] = jnp.cos(ang)
        o_ref[:, 128:256] = jnp.sin(ang)

    return pl.pallas_call(
        body, name="rope_table", grid=(t // tm,),
        in_specs=[pl.BlockSpec((tm, 1), lambda i: (i, 0)), pl.BlockSpec((1, 128), lambda i: (0, 0))],
        out_specs=pl.BlockSpec((tm, 256), lambda i: (i, 0)),
        out_shape=jax.ShapeDtypeStruct((t, 256), F32),
        compiler_params=_cparams(("arbitrary",)),
    )(pos, freq)


def _ada_fwd(c_all, w, b):
    nl, _, s = w.shape

    def body(c_ref, w_ref, b_ref, o_ref):
        o_ref[0] = _mm_f32(c_ref[...], w_ref[0]) + b_ref[0]

    return pl.pallas_call(
        body, name="ada_fwd", grid=(nl,),
        in_specs=[pl.BlockSpec((8, D), lambda l: (0, 0)), pl.BlockSpec((1, D, s), lambda l: (l, 0, 0)),
                  pl.BlockSpec((1, 1, s), lambda l: (l, 0, 0))],
        out_specs=pl.BlockSpec((1, 8, s), lambda l: (l, 0, 0)),
        out_shape=jax.ShapeDtypeStruct((nl, 8, s), F32),
        compiler_params=_cparams(("arbitrary",)),
    )(c_all, w, b)


def _ada_bwd(c_all, dmod_cols, dmod_all):
    nl, _, s = dmod_cols.shape

    def body(c_ref, dc_ref, da_ref, gw_ref, gb_ref):
        gw_ref[0] = _dot(c_ref[...], dc_ref[0], 0, 0, lax.Precision.HIGHEST)
        gb_ref[0] = jnp.sum(da_ref[0], axis=0, keepdims=True)

    return pl.pallas_call(
        body, name="ada_bwd", grid=(nl,),
        in_specs=[pl.BlockSpec((8, D), lambda l: (0, 0)), pl.BlockSpec((1, 8, s), lambda l: (l, 0, 0)),
                  pl.BlockSpec((1, 8, 3 * D), lambda l: (l, 0, 0))],
        out_specs=[pl.BlockSpec((1, D, s), lambda l: (l, 0, 0)), pl.BlockSpec((1, 1, 3 * D), lambda l: (l, 0, 0))],
        out_shape=[jax.ShapeDtypeStruct((nl, D, s), F32), jax.ShapeDtypeStruct((nl, 1, 3 * D), F32)],
        compiler_params=_cparams(("arbitrary",)),
    )(c_all, dmod_cols, dmod_all)


def _lb_fn(h8):
    sm = jax.nn.softmax(h8, axis=0)
    r = lax.broadcasted_iota(jnp.int32, (8, 8), 0)
    c = lax.broadcasted_iota(jnp.int32, (8, 8), 1)
    return _mm_f32(((c >= 1) & (c <= r)).astype(F32), sm)


def _lb_fwd(h8):
    def body(h_ref, o_ref):
        o_ref[...] = _lb_fn(h_ref[...])

    return pl.pallas_call(body, name="lb_fwd", out_shape=jax.ShapeDtypeStruct((8, D), F32))(h8)


def _lb_bwd(h8, dlb8):
    def body(h_ref, d_ref, o_ref):
        _, vjp = jax.vjp(_lb_fn, h_ref[...])
        o_ref[...] = vjp(d_ref[...])[0]

    return pl.pallas_call(body, name="lb_bwd", out_shape=jax.ShapeDtypeStruct((8, D), F32))(h8, dlb8)


ADAM_LR, ADAM_B1, ADAM_B2, ADAM_EPS, ADAM_WD, ADAM_STEP = 0.001, 0.9, 0.999, 1e-08, 0.01, 10


def _adamw(w, gparts, m, v):
    r, c = w.shape
    tr = r if r * c * 4 <= (1 << 20) else max(8, ((1 << 20) // (c * 4)) // 8 * 8)
    while r % tr:
        tr -= 8
    ng = len(gparts)

    def body(*refs):
        w_ref, m_ref, v_ref = refs[0], refs[1 + ng], refs[2 + ng]
        g_ref, d_ref, nm_ref, nv_ref = refs[3 + ng:]
        g = refs[1][...]
        for gr in refs[2:1 + ng]:
            g = g + gr[...]
        mm = ADAM_B1 * m_ref[...] + (1.0 - ADAM_B1) * g
        vv = ADAM_B2 * v_ref[...] + (1.0 - ADAM_B2) * (g * g)
        m_hat = mm / (1.0 - ADAM_B1 ** ADAM_STEP)
        v_hat = vv / (1.0 - ADAM_B2 ** ADAM_STEP)
        g_ref[...] = g
        d_ref[...] = -ADAM_LR * (m_hat / (jnp.sqrt(v_hat) + ADAM_EPS) + ADAM_WD * w_ref[...])
        nm_ref[...] = mm
        nv_ref[...] = vv

    spec = pl.BlockSpec((tr, c), lambda i: (i, 0))
    return pl.pallas_call(
        body, name="adamw", grid=(r // tr,), in_specs=[spec] * (3 + ng), out_specs=[spec] * 4,
        out_shape=[jax.ShapeDtypeStruct((r, c), F32)] * 4,
        compiler_params=_cparams(("arbitrary",)),
    )(w, *gparts, m, v)


def _sum_rows(parts):
    r, c = parts[0].shape
    tr = 8
    for cand in range(min(r, 512), 7, -8):
        if r % cand == 0:
            tr = cand
            break

    def body(*refs):
        acc = refs[0][...]
        for p in refs[1:-1]:
            acc = acc + p[...]
        refs[-1][...] = acc

    spec = pl.BlockSpec((tr, c), lambda i: (i, 0))
    return pl.pallas_call(
        body, name="sum_rows", grid=(r // tr,), in_specs=[spec] * len(parts), out_specs=spec,
        out_shape=jax.ShapeDtypeStruct((r, c), F32),
        compiler_params=_cparams(("arbitrary",)),
    )(*parts)


MESH = pl.DeviceIdType.MESH
ANY = pl.BlockSpec(memory_space=pl.ANY)


def _place():
    return lax.axis_index("x"), lax.axis_index("y"), lax.axis_index("c")


def _allgather8(blk):
    m_per, n = blk.shape

    def body(x_ref, out_ref, send_sems, recv_sems, local_sem):
        x, y, c = _place()
        me, sibling = (x, y, c), (x, y, 1 - c)
        chips = [(1 - x, y), (x, 1 - y), (1 - x, 1 - y)]

        def rows(px, py, pc):
            return out_ref.at[pl.ds((4 * px + 2 * py + pc) * m_per, m_per), :]

        def copy(k, block, to, src=None):
            return pltpu.make_async_remote_copy(
                src_ref=rows(*block) if src is None else src, dst_ref=rows(*block),
                send_sem=send_sems.at[k], recv_sem=recv_sems.at[k], device_id=to, device_id_type=MESH)

        mine = pltpu.make_async_copy(x_ref, rows(*me), local_sem)
        mine.start()
        first = [copy(0, me, sibling, src=x_ref)]
        first += [copy(1 + j, me, (*chip, c), src=x_ref) for j, chip in enumerate(chips)]
        for cp in first:
            cp.start()
        passed = [copy(4 + j, (*chip, c), sibling) for j, chip in enumerate(chips)]
        for j, chip in enumerate(chips):
            copy(1 + j, (*chip, c), me).wait_recv()
            passed[j].start()
        copy(0, sibling, me).wait_recv()
        for j, chip in enumerate(chips):
            copy(4 + j, (*chip, 1 - c), me).wait_recv()
        for cp in first + passed:
            cp.wait_send()
        mine.wait()

    return pl.pallas_call(
        body, name="allgather8",
        out_shape=jax.ShapeDtypeStruct((8 * m_per, n), blk.dtype),
        in_specs=[pl.BlockSpec(memory_space=pltpu.VMEM)],
        out_specs=pl.BlockSpec(memory_space=pltpu.VMEM),
        scratch_shapes=[pltpu.SemaphoreType.DMA((7,)), pltpu.SemaphoreType.DMA((7,)), pltpu.SemaphoreType.DMA],
    )(blk)


def _chip_peers():
    x, y, c = _place()
    return [(1 - x, y, c), (x, 1 - y, c), (1 - x, 1 - y, c)]


def _chip_allgather(shard):
    def body(x_ref, out_ref, send_sems, recv_sems, local_sem):
        x, y, _ = _place()
        peers = _chip_peers()

        def copy(j, chip_index):
            return pltpu.make_async_remote_copy(
                src_ref=x_ref, dst_ref=out_ref.at[chip_index], send_sem=send_sems.at[j], recv_sem=recv_sems.at[j],
                device_id=peers[j], device_id_type=MESH)

        mine = pltpu.make_async_copy(x_ref, out_ref.at[2 * x + y], local_sem)
        mine.start()
        sends = [copy(j, 2 * x + y) for j in range(3)]
        for cp in sends:
            cp.start()
        for j in range(3):
            copy(j, 2 * peers[j][0] + peers[j][1]).wait_recv()
        for cp in sends:
            cp.wait_send()
        mine.wait()

    return pl.pallas_call(
        body, name="chip_allgather",
        out_shape=jax.ShapeDtypeStruct((4,) + shard.shape, shard.dtype),
        in_specs=[ANY], out_specs=ANY,
        scratch_shapes=[pltpu.SemaphoreType.DMA((3,)), pltpu.SemaphoreType.DMA((3,)), pltpu.SemaphoreType.DMA],
    )(shard)


def _chip_scatter(parts):
    def body(p_ref, out_ref, send_sems, recv_sems):
        peers = _chip_peers()
        sends = [pltpu.make_async_remote_copy(
            src_ref=p_ref.at[2 * peers[j][0] + peers[j][1]], dst_ref=out_ref.at[j], send_sem=send_sems.at[j],
            recv_sem=recv_sems.at[j], device_id=peers[j], device_id_type=MESH) for j in range(3)]
        for cp in sends:
            cp.start()
        for cp in sends:
            cp.wait_recv()
        for cp in sends:
            cp.wait_send()

    return pl.pallas_call(
        body, name="chip_scatter",
        out_shape=jax.ShapeDtypeStruct((3,) + parts.shape[1:], parts.dtype),
        in_specs=[ANY], out_specs=ANY,
        scratch_shapes=[pltpu.SemaphoreType.DMA((3,)), pltpu.SemaphoreType.DMA((3,))],
    )(parts)


def _sibling_swap(a):
    def body(a_ref, out_ref, send_sem, recv_sem):
        x, y, c = _place()
        cp = pltpu.make_async_remote_copy(src_ref=a_ref, dst_ref=out_ref, send_sem=send_sem, recv_sem=recv_sem,
                                          device_id=(x, y, 1 - c), device_id_type=MESH)
        cp.start()
        cp.wait_recv()
        cp.wait_send()

    return pl.pallas_call(
        body, name="sibling_swap", out_shape=jax.ShapeDtypeStruct(a.shape, a.dtype),
        in_specs=[ANY], out_specs=ANY,
        scratch_shapes=[pltpu.SemaphoreType.DMA, pltpu.SemaphoreType.DMA],
    )(a)


WEIGHTS = ['hgrn_lb', 'ada_w', 'ada_b', 'norm_g', 'hg_in_w', 'hg_out_w', 'hg_onorm', 'sw_in_w', 'sw_out_w', 'sw_qnorm',
           'sw_knorm', 'sw_sinks', 'gd_in_w', 'gd_out_w', 'gd_conv_w', 'gd_a_log', 'gd_dt_bias', 'gd_onorm']
BIG = ['hg_in_w', 'hg_out_w', 'sw_in_w', 'sw_out_w', 'gd_in_w', 'gd_out_w']
PACK_ALIGN = 16
ROPE_THETA = 10000.0
ADA_S = 3 * D // 4
SMALL_ROW = {'hg_onorm': (0, 256), 'sw_qnorm': (256, 64), 'sw_knorm': (320, 64), 'sw_sinks': (384, 16),
             'gd_a_log': (400, 16), 'gd_dt_bias': (416, 16), 'gd_onorm': (432, 128)}


def _pack_rows(arrs):
    flat = jnp.concatenate([a.reshape(-1, D) for a in arrs], axis=0)
    return jnp.pad(flat, ((0, -flat.shape[0] % PACK_ALIGN), (0, 0)))


def _unpack_rows(packed, shapes):
    out, off = [], 0
    for s in shapes:
        rows = 1
        for d in s:
            rows *= d
        rows //= D
        out.append(packed[..., off:off + rows, :].reshape(packed.shape[:-2] + tuple(s)))
        off += rows
    return out


def _pack_small(vals):
    row = jnp.concatenate([vals[k].reshape(-1) for k in SMALL_ROW])
    row = jnp.pad(row, (0, D - row.shape[0]))[None]
    return jnp.concatenate([vals['hgrn_lb'], vals['norm_g'], vals['gd_conv_w'].reshape(16, D), row,
                            jnp.zeros((7, D), F32)], axis=0)


def _sw_cols(w, inverse=False):
    def split(a, heads):
        shp = (a.shape[0], 2, heads, 32) if inverse else (a.shape[0], heads, 2, 32)
        return a.reshape(shp).transpose(0, 2, 1, 3).reshape(a.shape[0], heads * 64)
    return jnp.concatenate([split(w[:, 0:1024], 16), split(w[:, 1024:1280], 4), w[:, 1280:]], axis=1)


def kernel(x, c, positions, hgrn_lb, ada_w, ada_b, norm_g, hg_in_w, hg_out_w, hg_onorm, sw_in_w, sw_out_w, sw_qnorm, sw_knorm, sw_sinks, gd_in_w, gd_out_w, gd_conv_w, gd_a_log, gd_dt_bias, gd_onorm, loss_target, m_hgrn_lb, m_ada_w, m_ada_b, m_norm_g, m_hg_in_w, m_hg_out_w, m_hg_onorm, m_sw_in_w, m_sw_out_w, m_sw_qnorm, m_sw_knorm, m_sw_sinks, m_gd_in_w, m_gd_out_w, m_gd_conv_w, m_gd_a_log, m_gd_dt_bias, m_gd_onorm, v_hgrn_lb, v_ada_w, v_ada_b, v_norm_g, v_hg_in_w, v_hg_out_w, v_hg_onorm, v_sw_in_w, v_sw_out_w, v_sw_qnorm, v_sw_knorm, v_sw_sinks, v_gd_in_w, v_gd_out_w, v_gd_conv_w, v_gd_a_log, v_gd_dt_bias, v_gd_onorm):
    w_in = dict(hgrn_lb=hgrn_lb, ada_w=ada_w, ada_b=ada_b, norm_g=norm_g, hg_in_w=hg_in_w, hg_out_w=hg_out_w,
                hg_onorm=hg_onorm, sw_in_w=sw_in_w, sw_out_w=sw_out_w, sw_qnorm=sw_qnorm, sw_knorm=sw_knorm,
                sw_sinks=sw_sinks, gd_in_w=gd_in_w, gd_out_w=gd_out_w, gd_conv_w=gd_conv_w, gd_a_log=gd_a_log,
                gd_dt_bias=gd_dt_bias, gd_onorm=gd_onorm)
    m_in = dict(zip(WEIGHTS, (m_hgrn_lb, m_ada_w, m_ada_b, m_norm_g, m_hg_in_w, m_hg_out_w, m_hg_onorm, m_sw_in_w,
                              m_sw_out_w, m_sw_qnorm, m_sw_knorm, m_sw_sinks, m_gd_in_w, m_gd_out_w, m_gd_conv_w,
                              m_gd_a_log, m_gd_dt_bias, m_gd_onorm)))
    v_in = dict(zip(WEIGHTS, (v_hgrn_lb, v_ada_w, v_ada_b, v_norm_g, v_hg_in_w, v_hg_out_w, v_hg_onorm, v_sw_in_w,
                              v_sw_out_w, v_sw_qnorm, v_sw_knorm, v_sw_sinks, v_gd_in_w, v_gd_out_w, v_gd_conv_w,
                              v_gd_a_log, v_gd_dt_bias, v_gd_onorm)))
    ax, ay, ac = _place()
    chip = 2 * ax + ay
    bidx = 4 * ax + 2 * ay + ac
    t = x.shape[1]
    x0, target = x[0], loss_target[0]

    c_all = _allgather8(jnp.pad(c, ((0, 7), (0, 0)))).reshape(8, 8, D)[:, 0, :]
    ada_b_cols = lax.dynamic_slice(ada_b, (0, chip * ADA_S), (4, ADA_S)).reshape(4, 1, ADA_S)
    mod_sh = _ada_fwd(c_all, ada_w, ada_b_cols)
    mod_g = _allgather8(mod_sh.reshape(32, ADA_S)).reshape(4, 2, 4, 8, ADA_S)[:, 0]
    mod = lax.dynamic_index_in_dim(mod_g, bidx, axis=2, keepdims=False).transpose(1, 0, 2).reshape(4, 3 * D)
    shift = [mod[l:l + 1, 0:D] for l in range(4)]
    scale = [mod[l:l + 1, D:2 * D] for l in range(4)]
    gate = [mod[l:l + 1, 2 * D:3 * D] for l in range(4)]

    h8 = jnp.concatenate([hgrn_lb, jnp.full((4, D), -1e30, F32)], axis=0)
    lb_all = _lb_fwd(h8)
    freq = ROPE_THETA ** (-jnp.arange(0, 64, 2, dtype=F32) / 64)
    cs = _rope_table(positions.reshape(t, 1), jnp.tile(freq, 4)[None])

    big_shapes = [w_in[k].shape for k in BIG]
    gathered = _chip_allgather(_pack_rows([w_in[k] for k in BIG]).astype(BF16))
    hg_in_k, hg_out_k, sw_in_k, sw_out_k, gd_in_k, gd_out_k = _unpack_rows(gathered, big_shapes)
    hg_in_f = hg_in_k.transpose(1, 2, 0, 3).reshape(2, D, 4 * D)
    hg_out_f = hg_out_k.transpose(1, 0, 2, 3).reshape(2, D, D)
    sw_in_f = _sw_cols(sw_in_k[:, 0].transpose(1, 0, 2).reshape(D, SW_N))
    sw_out_f = sw_out_k.reshape(D, D)
    gd_in_f = jnp.pad(gd_in_k[:, 0].transpose(1, 0, 2).reshape(D, 6176), ((0, 0), (0, GD_N - 6176)))
    gd_out_f = gd_out_k.reshape(GD_VW, D)
    win = [hg_in_f[0], sw_in_f, gd_in_f, hg_in_f[1]]
    wout = [hg_out_f[0], sw_out_f, gd_out_f, hg_out_f[1]]
    tn_in = [1024, 1280, 896, 1024]
    tk_bwd = [4096, 2560, 896, 4096]

    gq = jnp.stack([jnp.tile(sw_qnorm[0, :32], 16), jnp.tile(sw_qnorm[0, 32:], 16)])
    gk = jnp.stack([jnp.tile(sw_knorm[0, :32], 4), jnp.tile(sw_knorm[0, 32:], 4)])
    pad128 = lambda a: jnp.pad(a, ((0, 0), (0, HD - a.shape[1])))
    sinks, alog, dtb = pad128(sw_sinks), pad128(gd_a_log), pad128(gd_dt_bias)
    cw8 = jnp.pad(_chip_allgather(gd_conv_w[0]).transpose(1, 0, 2).reshape(4, GD_QKV), ((0, 4), (0, 0)))
    lbs = {0: lb_all[0:1], 3: lb_all[3:4]}

    xs, us, hs, ps, stss = [x0], [], [], [], []
    for l in range(4):
        u, h = _ln_mm(xs[l], norm_g[l:l + 1], scale[l], shift[l], win[l], tn_in[l])
        if l % 3 == 0:
            p, sts = _hg_fwd(u, lbs[l], hg_onorm[l // 3:l // 3 + 1])
        elif l % 3 == 1:
            p, sts = _sw_fwd(u, cs, gq, gk, sinks), None
        else:
            p, *sts = _gd_fwd(u, cw8, alog, dtb, gd_onorm)
        xs.append(_mm_res(p, wout[l], xs[l], gate[l]))
        us.append(u), hs.append(h), ps.append(p), stss.append(sts)
    lpart, dx = _loss_grad(xs[4], target)
    loss = lax.psum(lpart[0, 0], ("x", "y", "c"))

    g_small = {}
    d_in, d_out, dmod, dnorm_g, dlb8, dgo_hg = [None] * 4, [None] * 4, [None] * 4, [None] * 4, jnp.zeros((8, D), F32), {}
    for l in (3, 2, 1, 0):
        dp = _mm_scaled(dx, gate[l], wout[l].T, 1024)
        d_out[l], dgate = _outgrad(_mm_tn_acc(ps[l], dx, 512), wout[l], gate[l])
        if l % 3 == 0:
            du, dlb, dgo_hg[l // 3] = _hg_bwd(us[l], stss[l], dp, lbs[l], hg_onorm[l // 3:l // 3 + 1])
            dlb8 = lax.dynamic_update_slice(dlb8, dlb, (l, 0))
        elif l % 3 == 1:
            du, dgq, dgk, dsk = _sw_bwd(us[l], cs, dp, gq, gk, sinks)
            g_small['sw_qnorm'] = jnp.concatenate([dgq[0].reshape(16, 32).sum(0), dgq[1].reshape(16, 32).sum(0)])
            g_small['sw_knorm'] = jnp.concatenate([dgk[0].reshape(4, 32).sum(0), dgk[1].reshape(4, 32).sum(0)])
            g_small['sw_sinks'] = dsk[0, :16]
        else:
            du, dcw, dalog, ddtb, g_small['gd_onorm'] = _gd_bwd(us[l], *stss[l], dp, cw8, alog, dtb, gd_onorm)
            g_small['gd_conv_w'], g_small['gd_a_log'], g_small['gd_dt_bias'] = dcw[:4], dalog[0, :16], ddtb[0, :16]
        d_in[l] = _mm_tn_acc(hs[l], du, 896 if l == 2 else 512)
        dx, dvec = _inproj_bwd(du, win[l].T, xs[l], dx, norm_g[l:l + 1], scale[l], shift[l], tk_bwd[l])
        dnorm_g[l] = dvec[0:1]
        dmod[l] = jnp.concatenate([dvec[2:3], dvec[1:2], dgate[0:1]], axis=1)
    grad_x = dx[None]

    g_small['hgrn_lb'] = _lb_bwd(h8, dlb8)[0:4]
    g_small['norm_g'] = jnp.concatenate(dnorm_g, axis=0)
    g_small['hg_onorm'] = jnp.concatenate([dgo_hg[0], dgo_hg[1]], axis=0)
    gs_all = _allgather8(_pack_small(g_small))
    gs = _sum_rows([gs_all[32 * d:32 * (d + 1)] for d in range(8)])

    def small_view(packed, k):
        if k == 'hgrn_lb':
            return packed[0:4]
        if k == 'norm_g':
            return packed[4:8]
        off, size = SMALL_ROW[k]
        return packed[24, off:off + size].reshape(w_in[k].shape)

    conv_sl = lambda full: lax.dynamic_slice(full.reshape(4, GD_QKV), (0, chip * D), (4, D))
    out = {}

    def put(k, res, shape):
        for name, r in zip(('grad_', 'delta_', 'new_m_', 'new_v_'), res):
            out[name + k] = r.reshape(shape)

    zero_conv = dict(gd_conv_w=jnp.zeros((4, GD_QKV), F32))
    small_names = ['hgrn_lb', 'norm_g'] + list(SMALL_ROW)
    res = _adamw(_pack_small({**{k: w_in[k] for k in small_names}, **zero_conv}), (gs,),
                 _pack_small({**{k: m_in[k] for k in small_names}, **zero_conv}),
                 _pack_small({**{k: v_in[k] for k in small_names}, **zero_conv}))
    for k in small_names:
        put(k, [small_view(r, k) for r in res], w_in[k].shape)
    put('gd_conv_w', _adamw(gd_conv_w[0], (conv_sl(gs[8:24]),), m_in['gd_conv_w'][0], v_in['gd_conv_w'][0]),
        gd_conv_w.shape)

    dm = _allgather8(jnp.pad(jnp.concatenate(dmod, axis=0), ((0, 4), (0, 0)))).reshape(8, 8, 3 * D)[:, :4]
    dm = dm.transpose(1, 0, 2)
    g_ada_w, g_ada_b = _ada_bwd(c_all, lax.dynamic_slice(dm, (0, 0, chip * ADA_S), (4, 8, ADA_S)), dm)
    put('ada_w', _adamw(ada_w.reshape(4 * D, ADA_S), (g_ada_w.reshape(4 * D, ADA_S),),
                        m_in['ada_w'].reshape(4 * D, ADA_S), v_in['ada_w'].reshape(4 * D, ADA_S)), ada_w.shape)
    put('ada_b', _adamw(ada_b, (g_ada_b.reshape(4, 3 * D),), m_in['ada_b'], v_in['ada_b']), ada_b.shape)

    by_chip = lambda g, cols: g.reshape(g.shape[0], 4, cols).transpose(1, 0, 2)
    d_sw_in = _sw_cols(d_in[1], inverse=True)
    parts = {
        'hg_in_w': jnp.stack([by_chip(d_in[0], D), by_chip(d_in[3], D)], axis=1),
        'hg_out_w': jnp.stack([d_out[0].reshape(4, D // 4, D), d_out[3].reshape(4, D // 4, D)], axis=1),
        'sw_in_w': by_chip(d_sw_in, SW_N // 4)[:, None],
        'sw_out_w': d_out[1].reshape(4, 1, D // 4, D),
        'gd_in_w': by_chip(d_in[2][:, :6176], 1544)[:, None],
        'gd_out_w': d_out[2].reshape(4, 1, GD_VW // 4, D),
    }
    packed = jnp.stack([_pack_rows([parts[k][j] for k in BIG]) for j in range(4)])
    recv = _chip_scatter(packed)
    own = lax.dynamic_index_in_dim(packed, chip, axis=0, keepdims=False)
    half = _sum_rows([own, recv[0], recv[1], recv[2]])
    other = _sibling_swap(half)
    res = _adamw(_pack_rows([w_in[k] for k in BIG]), (half, other), _pack_rows([m_in[k] for k in BIG]),
                 _pack_rows([v_in[k] for k in BIG]))
    for name, r in zip(('grad_', 'delta_', 'new_m_', 'new_v_'), res):
        for k, a in zip(BIG, _unpack_rows(r, big_shapes)):
            out[name + k] = a

    return (loss, grad_x, *[out[p + k] for p in ('grad_', 'delta_', 'new_m_', 'new_v_') for k in WEIGHTS])
```

```python
import functools

import jax
import jax.numpy as jnp
from jax import lax
from jax.experimental import pallas as pl
from jax.experimental.pallas import tpu as pltpu

F32 = jnp.float32
BF16 = jnp.bfloat16
D = 1024
EPS = 1e-6
CHUNK = 64
SUB = 16
HG_H = 8
HD = 128
VMEM_LIMIT = 56 * 1024 * 1024


def _cparams(sem=None):
    return pltpu.CompilerParams(dimension_semantics=sem, vmem_limit_bytes=VMEM_LIMIT)


def _dot(a, b, ca, cb, prec=None):
    return lax.dot_general(a, b, (((ca,), (cb,)), ((), ())), precision=prec, preferred_element_type=F32)


def _mm(a, b):
    return _dot(a.astype(BF16), b.astype(BF16), 1, 0)


def _mm_nt(a, b):
    return _dot(a.astype(BF16), b.astype(BF16), 1, 1)


def _mm_tn(a, b):
    return _dot(a.astype(BF16), b.astype(BF16), 0, 0)


def _mm_f32(a, b):
    return _dot(a, b, 1, 0, lax.Precision.HIGHEST)


def _silu(x):
    return x * jax.nn.sigmoid(x)


def _cumsum_impl(x):
    row = lax.broadcasted_iota(jnp.int32, x.shape, 0)
    s = 1
    while s < x.shape[0]:
        x = x + jnp.where(row >= s, pltpu.roll(x, s, 0), 0.0)
        s *= 2
    return x


@jax.custom_vjp
def _cumsum_rows(x):
    return _cumsum_impl(x)


_cumsum_rows.defvjp(lambda x: (_cumsum_impl(x), None),
                    lambda _, g: (jnp.sum(g, axis=0, keepdims=True) - _cumsum_impl(g) + g,))


def _roll_rows(x, shift):
    n = x.shape[0]

    @jax.custom_vjp
    def f(a):
        return pltpu.roll(a, shift, 0)

    f.defvjp(lambda a: (pltpu.roll(a, shift, 0), None), lambda _, g: (pltpu.roll(g, n - shift, 0),))
    return f(x)


def _hg_chunk(q_raw, f_pre, v, z, st, lb, go):
    c = q_raw.shape[0]
    nsub = c // SUB
    lf = jnp.log(lb + (1.0 - lb) * jax.nn.sigmoid(f_pre))
    k = (1.0 - lb) * jax.nn.sigmoid(-f_pre)
    q = _silu(q_raw)
    b = _cumsum_rows(lf)
    rowf = lax.broadcasted_iota(jnp.int32, lf.shape, 0)
    bmid = [jnp.sum(jnp.where(rowf == SUB * i + SUB // 2, b, 0.0), axis=0, keepdims=True) for i in range(nsub)]
    row = lax.broadcasted_iota(jnp.int32, (c, 1), 0)
    ref = sum(jnp.where((row >= SUB * i) & (row < SUB * (i + 1)), bmid[i], 0.0) for i in range(nsub))
    qt = q * jnp.exp(b - ref)
    kall = jnp.concatenate(
        [k * jnp.exp(jnp.where(row < SUB * (i + 1), bmid[i] - b, -jnp.inf)) for i in range(nsub)], axis=0)
    v4 = jnp.concatenate([v] * nsub, axis=0)
    b_last = jnp.sum(lf, axis=0, keepdims=True)
    qb = q * jnp.exp(b)
    kd = k * jnp.exp(b_last - b)
    e_last = jnp.exp(b_last)
    tq = lax.broadcasted_iota(jnp.int32, (c, nsub * c), 0)
    cq = lax.broadcasted_iota(jnp.int32, (c, nsub * c), 1)
    m_all = ((cq // c) == (tq // SUB)) & ((cq % c) <= tq)
    hs = lambda a: jnp.split(a, HG_H, axis=1)
    qt_h, kall_h, v4_h, qb_h, kd_h, v_h, z_h, el_h = map(hs, (qt, kall, v4, qb, kd, v, z, e_last))
    st_h = jnp.split(st, HG_H, axis=0)
    p_out, st_out = [], []
    for h in range(HG_H):
        pm = jnp.where(m_all, _mm_nt(qt_h[h], kall_h[h]), 0.0)
        o = _mm(pm, v4_h[h]) + _mm_nt(qb_h[h], st_h[h])
        st_out.append(el_h[h] * st_h[h] + _mm_tn(v_h[h], kd_h[h]))
        y = o * lax.rsqrt(jnp.mean(o * o, axis=1, keepdims=True) + EPS) * go
        p_out.append(y * _silu(z_h[h]))
    return jnp.concatenate(p_out, axis=1), jnp.concatenate(st_out, axis=0)


def _hg_fwd(u, lb, go):
    t = u.shape[0]
    n = t // CHUNK

    def body(u_ref, lb_ref, go_ref, p_ref, sts_ref, st_ref):
        @pl.when(pl.program_id(0) == 0)
        def _():
            st_ref[...] = jnp.zeros_like(st_ref)

        st = st_ref[...]
        sts_ref[0] = st
        p, st_next = _hg_chunk(u_ref[:, 0:D], u_ref[:, D:2 * D], u_ref[:, 2 * D:3 * D], u_ref[:, 3 * D:4 * D],
                               st, lb_ref[...], go_ref[...])
        p_ref[...] = p.astype(BF16)
        st_ref[...] = st_next

    return pl.pallas_call(
        body, name="hg_fwd", grid=(n,),
        in_specs=[pl.BlockSpec((CHUNK, 4 * D), lambda i: (i, 0)),
                  pl.BlockSpec((1, D), lambda i: (0, 0)),
                  pl.BlockSpec((1, HD), lambda i: (0, 0))],
        out_specs=[pl.BlockSpec((CHUNK, D), lambda i: (i, 0)),
                   pl.BlockSpec((1, HG_H * HD, HD), lambda i: (i, 0, 0))],
        out_shape=[jax.ShapeDtypeStruct((t, D), BF16), jax.ShapeDtypeStruct((n, HG_H * HD, HD), F32)],
        scratch_shapes=[pltpu.VMEM((HG_H * HD, HD), F32)],
        compiler_params=_cparams(("arbitrary",)),
    )(u, lb, go)


def _hg_bwd(u, sts, dp, lb, go):
    t = u.shape[0]
    n = t // CHUNK

    def body(u_ref, sts_ref, dp_ref, lb_ref, go_ref, du_ref, dlb_ref, dgo_ref, dst_ref):
        @pl.when(pl.program_id(0) == 0)
        def _():
            dst_ref[...] = jnp.zeros_like(dst_ref)
            dlb_ref[...] = jnp.zeros_like(dlb_ref)
            dgo_ref[...] = jnp.zeros_like(dgo_ref)

        _, vjp = jax.vjp(_hg_chunk, u_ref[:, 0:D], u_ref[:, D:2 * D], u_ref[:, 2 * D:3 * D], u_ref[:, 3 * D:4 * D],
                         sts_ref[0], lb_ref[...], go_ref[...])
        dq, df, dv, dz, dst, dlb, dgo = vjp((dp_ref[...].astype(F32), dst_ref[...]))
        du_ref[:, 0:D] = dq.astype(BF16)
        du_ref[:, D:2 * D] = df.astype(BF16)
        du_ref[:, 2 * D:3 * D] = dv.astype(BF16)
        du_ref[:, 3 * D:4 * D] = dz.astype(BF16)
        dst_ref[...] = dst
        dlb_ref[...] += dlb
        dgo_ref[...] += dgo

    rev = lambda i: (n - 1 - i, 0)
    return pl.pallas_call(
        body, name="hg_bwd", grid=(n,),
        in_specs=[pl.BlockSpec((CHUNK, 4 * D), rev),
                  pl.BlockSpec((1, HG_H * HD, HD), lambda i: (n - 1 - i, 0, 0)),
                  pl.BlockSpec((CHUNK, D), rev),
                  pl.BlockSpec((1, D), lambda i: (0, 0)),
                  pl.BlockSpec((1, HD), lambda i: (0, 0))],
        out_specs=[pl.BlockSpec((CHUNK, 4 * D), rev),
                   pl.BlockSpec((1, D), lambda i: (0, 0)),
                   pl.BlockSpec((1, HD), lambda i: (0, 0))],
        out_shape=[jax.ShapeDtypeStruct((t, 4 * D), BF16), jax.ShapeDtypeStruct((1, D), F32),
                   jax.ShapeDtypeStruct((1, HD), F32)],
        scratch_shapes=[pltpu.VMEM((HG_H * HD, HD), F32)],
        compiler_params=_cparams(("arbitrary",)),
    )(u, sts, dp, lb, go)


GD_VH = 16
GD_QKH = 8
GD_QKV = 4096
GD_VW = 2048
GD_N = GD_QKV + GD_VW + HD
GD_GRP = 4
GD_SOLVE = (GD_VH // GD_GRP, GD_GRP * CHUNK, 2 * HD)
HALO = 8


def _mm_high(a, b):
    return _dot(a, b, 1, 0, lax.Precision.HIGH)


def _lane_pick(a, h):
    lane = lax.broadcasted_iota(jnp.int32, a.shape, 1)
    return jnp.sum(jnp.where(lane == h, a, 0.0), axis=1, keepdims=True)


def _l2n(x):
    return x * lax.rsqrt(jnp.sum(x * x, axis=1, keepdims=True) + EPS)


def _solve_fwd(a_mat, rhs):
    n = a_mat.shape[0]
    r_i, c_i = lax.broadcasted_iota(jnp.int32, (n, n), 0), lax.broadcasted_iota(jnp.int32, (n, n), 1)
    same = lambda nb: (r_i // nb) == (c_i // nb)
    d0 = jnp.where(same(8), a_mat, 0.0)
    d2 = _mm_high(d0, d0)
    tinv = (r_i == c_i).astype(F32) - d0
    tinv = tinv + _mm_high(tinv, d2)
    tinv = tinv + _mm_high(tinv, _mm_high(d2, d2))
    nb = 16
    while nb <= CHUNK:
        low = jnp.where(same(nb) & ~same(nb // 2), a_mat, 0.0)
        tinv = tinv - _mm(_mm(tinv, low), tinv)
        nb *= 2
    x = _mm_high(tinv, rhs)
    return x, (tinv, x)


def _solve_bwd(res, dx):
    tinv, x = res
    drhs = _dot(tinv, dx, 0, 0, lax.Precision.HIGH)
    return -_dot(drhs, x, 1, 1, lax.Precision.HIGH), drhs


@jax.custom_vjp
def _solved(a_mat, rhs, tinv, x):
    return x


_solved.defvjp(lambda a_mat, rhs, tinv, x: (x, (tinv, x)),
               lambda res, dx: _solve_bwd(res, dx) + (jnp.zeros_like(res[0]), jnp.zeros_like(res[1])))


def _gd_chunk(xh, x, z, ab, st, cw, alog, dtb, go, solve):
    c = x.shape[0]
    xa = jnp.concatenate([xh, x], axis=0)
    sh = [jnp.split(_roll_rows(xa, 3 - j), [HALO], axis=0)[1] for j in range(3)]
    qkv = _silu(cw[0:1] * sh[0] + cw[1:2] * sh[1] + cw[2:3] * sh[2] + cw[3:4] * x)
    q_all, k_all, v_all = jnp.split(qkv, [1024, 2048], axis=1)
    lane = lax.broadcasted_iota(jnp.int32, (c, HD), 1)
    a_part = jnp.where(lane < GD_VH, ab, 0.0)
    g_all = -jnp.exp(alog) * jax.nn.softplus(a_part + dtb)
    d_all = _cumsum_rows(g_all)
    dl_all = jnp.sum(g_all, axis=0, keepdims=True)
    beta_all = jax.nn.sigmoid(ab)
    gc = GD_GRP * c
    r_i, c_i = lax.broadcasted_iota(jnp.int32, (gc, gc), 0), lax.broadcasted_iota(jnp.int32, (gc, gc), 1)
    same_head = (r_i // c) == (c_i // c)
    tri_g, strict_g = same_head & (c_i <= r_i), same_head & (c_i < r_i)
    qs =jnp.split(q_all, GD_QKH, axis=1)
    ks = jnp.split(k_all, GD_QKH, axis=1)
    vs = jnp.split(v_all, GD_VH, axis=1)
    zs = jnp.split(z, GD_VH, axis=1)
    sts = jnp.split(st, GD_VH, axis=0)
    qn = [_l2n(a) * (HD ** -0.5) for a in qs]
    kn = [_l2n(a) for a in ks]
    p_out, st_out = [], []
    for g in range(GD_VH // GD_GRP):
        heads = range(GD_GRP * g, GD_GRP * (g + 1))
        stack = lambda f: jnp.concatenate([f(h) for h in heads], axis=0)
        q_, k_, v_ = stack(lambda h: qn[h // 2]), stack(lambda h: kn[h // 2]), stack(lambda h: vs[h])
        dcol = stack(lambda h: _lane_pick(d_all, h))
        bcol = stack(lambda h: _lane_pick(beta_all, GD_VH + h))
        dlast = stack(lambda h: jnp.broadcast_to(_lane_pick(dl_all, h), (c, 1)))
        drow = jnp.sum(jnp.broadcast_to(dcol, (gc, HD)).T, axis=0, keepdims=True) * (1.0 / HD)
        dec = jnp.exp(jnp.where(tri_g, dcol - drow, -jnp.inf))
        kb = k_ * bcol
        a_mat = jnp.where(strict_g, _mm_nt(kb, k_) * dec, 0.0)
        xsol = solve(g, a_mat, jnp.concatenate([v_ * bcol, kb * jnp.exp(dcol)], axis=1))
        u_, w_ = jnp.split(xsol, 2, axis=1)
        w_h = jnp.split(w_, GD_GRP, axis=0)
        v_new = u_ - jnp.concatenate([_mm(w_h[i], sts[h]) for i, h in enumerate(heads)], axis=0)
        qd_h = jnp.split(q_ * jnp.exp(dcol), GD_GRP, axis=0)
        o_g = _mm(_mm_nt(q_, k_) * dec, v_new) + jnp.concatenate(
            [_mm(qd_h[i], sts[h]) for i, h in enumerate(heads)], axis=0)
        kd_h = jnp.split(k_ * jnp.exp(dlast - dcol), GD_GRP, axis=0)
        vn_h = jnp.split(v_new, GD_GRP, axis=0)
        o_h = jnp.split(o_g, GD_GRP, axis=0)
        for i, h in enumerate(heads):
            st_out.append(sts[h] * jnp.exp(_lane_pick(dl_all, h)) + _mm_tn(kd_h[i], vn_h[i]))
            o = o_h[i]
            y = o * lax.rsqrt(jnp.mean(o * o, axis=1, keepdims=True) + EPS) * go
            p_out.append(y * _silu(zs[h]))
    return jnp.concatenate(p_out, axis=1), jnp.concatenate(st_out, axis=0)


def _gd_specs(n, rev):
    ci = (lambda i: n - 1 - i) if rev else (lambda i: i)
    return [pl.BlockSpec((HALO, GD_QKV), lambda i: (jnp.maximum(ci(i) * (CHUNK // HALO) - 1, 0), 0)),
            pl.BlockSpec((CHUNK, GD_N), lambda i: (ci(i), 0))]


def _gd_load(uh_ref, u_ref, first):
    xh = jnp.where(first, 0.0, uh_ref[...])
    return xh, u_ref[:, 0:GD_QKV], u_ref[:, GD_QKV:GD_QKV + GD_VW], u_ref[:, GD_QKV + GD_VW:GD_N]


def _gd_fwd(u, cw, alog, dtb, go):
    t = u.shape[0]
    n = t // CHUNK
    small = lambda r, w: pl.BlockSpec((r, w), lambda i: (0, 0))

    def body(uh_ref, u_ref, cw_ref, alog_ref, dtb_ref, go_ref, p_ref, sts_ref, tinv_ref, xsol_ref, st_ref):
        i = pl.program_id(0)

        @pl.when(i == 0)
        def _():
            st_ref[...] = jnp.zeros_like(st_ref)

        def solve(g, a_mat, rhs):
            xsol, (tinv, _) = _solve_fwd(a_mat, rhs)
            tinv_ref[0, g] = tinv
            xsol_ref[0, g] = xsol
            return xsol

        st = st_ref[...]
        sts_ref[0] = st
        p, st_next = _gd_chunk(*_gd_load(uh_ref, u_ref, i == 0), st, cw_ref[...], alog_ref[...], dtb_ref[...],
                               go_ref[...], solve)
        p_ref[...] = p.astype(BF16)
        st_ref[...] = st_next

    return pl.pallas_call(
        body, name="gd_fwd", grid=(n,),
        in_specs=_gd_specs(n, False) + [small(8, GD_QKV), small(1, HD), small(1, HD), small(1, HD)],
        out_specs=[pl.BlockSpec((CHUNK, GD_VW), lambda i: (i, 0)),
                   pl.BlockSpec((1, GD_VH * HD, HD), lambda i: (i, 0, 0)),
                   pl.BlockSpec((1,) + GD_SOLVE, lambda i: (i, 0, 0, 0)),
                   pl.BlockSpec((1,) + GD_SOLVE, lambda i: (i, 0, 0, 0))],
        out_shape=[jax.ShapeDtypeStruct((t, GD_VW), BF16), jax.ShapeDtypeStruct((n, GD_VH * HD, HD), F32),
                   jax.ShapeDtypeStruct((n,) + GD_SOLVE, F32), jax.ShapeDtypeStruct((n,) + GD_SOLVE, F32)],
        scratch_shapes=[pltpu.VMEM((GD_VH * HD, HD), F32)],
        compiler_params=_cparams(("arbitrary",)),
    )(u, u, cw, alog, dtb, go)


def _gd_bwd(u, sts, tinvs, xsols, dp, cw, alog, dtb, go):
    t = u.shape[0]
    n = t // CHUNK
    small = lambda r, w: pl.BlockSpec((r, w), lambda i: (0, 0))

    def body(uh_ref, u_ref, sts_ref, tinv_ref, xsol_ref, dp_ref, cw_ref, alog_ref, dtb_ref, go_ref,
             du_ref, dcw_ref, dalog_ref, ddtb_ref, dgo_ref, dst_ref, dhalo_ref):
        i = pl.program_id(0)

        @pl.when(i == 0)
        def _():
            for r in (dst_ref, dhalo_ref, dcw_ref, dalog_ref, ddtb_ref, dgo_ref):
                r[...] = jnp.zeros_like(r)

        solve = lambda g, a_mat, rhs: _solved(a_mat, rhs, tinv_ref[0, g], xsol_ref[0, g])
        chunk = functools.partial(_gd_chunk, solve=solve)
        _, vjp = jax.vjp(chunk, *_gd_load(uh_ref, u_ref, i == n - 1), sts_ref[0], cw_ref[...], alog_ref[...],
                         dtb_ref[...], go_ref[...])
        dxh, dx, dz, dab, dst, dcw, dalog, ddtb, dgo = vjp((dp_ref[...].astype(F32), dst_ref[...]))
        tail = jnp.concatenate([jnp.zeros((CHUNK - HALO, GD_QKV), F32), dhalo_ref[...]], axis=0)
        du_ref[:, 0:GD_QKV] = (dx + tail).astype(BF16)
        du_ref[:, GD_QKV:GD_QKV + GD_VW] = dz.astype(BF16)
        du_ref[:, GD_QKV + GD_VW:GD_N] = dab.astype(BF16)
        dhalo_ref[...] = dxh
        dst_ref[...] = dst
        dcw_ref[...] += dcw
        dalog_ref[...] += dalog
        ddtb_ref[...] += ddtb
        dgo_ref[...] += dgo

    return pl.pallas_call(
        body, name="gd_bwd", grid=(n,),
        in_specs=_gd_specs(n, True) + [pl.BlockSpec((1, GD_VH * HD, HD), lambda i: (n - 1 - i, 0, 0)),
                                       pl.BlockSpec((1,) + GD_SOLVE, lambda i: (n - 1 - i, 0, 0, 0)),
                                       pl.BlockSpec((1,) + GD_SOLVE, lambda i: (n - 1 - i, 0, 0, 0)),
                                       pl.BlockSpec((CHUNK, GD_VW), lambda i: (n - 1 - i, 0)),
                                       small(8, GD_QKV), small(1, HD), small(1, HD), small(1, HD)],
        out_specs=[pl.BlockSpec((CHUNK, GD_N), lambda i: (n - 1 - i, 0)),
                   small(8, GD_QKV), small(1, HD), small(1, HD), small(1, HD)],
        out_shape=[jax.ShapeDtypeStruct((t, GD_N), BF16), jax.ShapeDtypeStruct((8, GD_QKV), F32)]
        + [jax.ShapeDtypeStruct((1, HD), F32)] * 3,
        scratch_shapes=[pltpu.VMEM((GD_VH * HD, HD), F32), pltpu.VMEM((HALO, GD_QKV), F32)],
        compiler_params=_cparams(("arbitrary",)),
    )(u, u, sts, tinvs, xsols, dp, cw, alog, dtb, go)


SW_B = 128
SW_H = 16
SW_G = 4
SW_N = 2560
SW_KV0 = 1024


def _blockdiag(n, blk):
    r = lax.broadcasted_iota(jnp.int32, (n, n), 0) // blk
    c = lax.broadcasted_iota(jnp.int32, (n, n), 1) // blk
    return (r == c).astype(F32)


def _sw_normrope(x, g1, g2, cos, sin):
    w = x.shape[1] // 2
    x1, x2 = jnp.split(x, 2, axis=1)
    ms = _mm_high(x1 * x1 + x2 * x2, _blockdiag(w, 32)) * (1.0 / 64.0)
    rinv = lax.rsqrt(ms + EPS)
    n1, n2 = x1 * rinv * g1, x2 * rinv * g2
    return jnp.concatenate([n1 * cos - n2 * sin, n2 * cos + n1 * sin], axis=1)


def _sw_block(q, kvp, kvc, z, csp, csc, gq, gk, sinks, has_prev):
    b = q.shape[0]
    cos_c, sin_c = jnp.split(csc, 2, axis=1)
    cos_p, sin_p = jnp.split(csp, 2, axis=1)
    tile4 = lambda a: jnp.concatenate([a] * 4, axis=1)
    qh = _sw_normrope(q, gq[0:1], gq[1:2], tile4(cos_c), tile4(sin_c))
    kp, vp = jnp.split(kvp, 2, axis=1)
    kc, vc = jnp.split(kvc, 2, axis=1)
    kh = jnp.concatenate([_sw_normrope(kp, gk[0:1], gk[1:2], cos_p, sin_p),
                          _sw_normrope(kc, gk[0:1], gk[1:2], cos_c, sin_c)], axis=0)
    vv = jnp.concatenate([vp, vc], axis=0)
    q1, q2 = jnp.split(qh, 2, axis=1)
    q1g, q2g = jnp.split(q1, SW_G, axis=1), jnp.split(q2, SW_G, axis=1)
    qi = lax.broadcasted_iota(jnp.int32, (4 * b, 2 * b), 0) % b
    kj = lax.broadcasted_iota(jnp.int32, (4 * b, 2 * b), 1)
    rel = qi + b - kj
    mask = (rel >= 0) & (rel < SW_B) & (has_prev | (kj >= b))
    ri = lax.broadcasted_iota(jnp.int32, (256, 256), 0)
    ci = lax.broadcasted_iota(jnp.int32, (256, 256), 1)
    row_head = lax.broadcasted_iota(jnp.int32, (4 * b, 256), 0) // b
    lane_q = lax.broadcasted_iota(jnp.int32, (4 * b, 256), 1)
    q_sel = (lane_q % 128) // 32 == row_head
    o_sel = lane_q // 64 == row_head
    o_out = []
    for g in range(SW_G):
        ek = ((ri // 128 == ci // 128) & ((ri % 128) // 32 == g) & (ri % 32 == ci % 32)).astype(F32)
        ev = ((ri // 64 == g) & (ri % 64 == ci % 64)).astype(F32)
        kx = _mm(kh, ek)
        vx = _mm(vv, ev)
        qg = jnp.concatenate([q1g[g], q2g[g]], axis=1)
        q4 = jnp.where(q_sel, jnp.concatenate([qg] * 4, axis=0), 0.0)
        sink = jnp.concatenate([jnp.broadcast_to(_lane_pick(sinks, 4 * g + j), (b, 1)) for j in range(4)], axis=0)
        s = jnp.where(mask, _mm_nt(q4, kx) * (64 ** -0.5), -jnp.inf)
        m = jnp.maximum(jnp.max(s, axis=1, keepdims=True), sink)
        p = jnp.exp(s - m)
        pn = p / (jnp.sum(p, axis=1, keepdims=True) + jnp.exp(sink - m))
        o4 = jnp.split(jnp.where(o_sel, _mm(pn, vx), 0.0), 4, axis=0)
        o_out.append(o4[0] + o4[1] + o4[2] + o4[3])
    return jnp.concatenate(o_out, axis=1) * _silu(z)


def _sw_specs(n, rev):
    ci = (lambda i: n - 1 - i) if rev else (lambda i: i)
    prev = lambda i: jnp.maximum(ci(i) - 1, 0)
    return [pl.BlockSpec((SW_B, SW_N), lambda i: (ci(i), 0)),
            pl.BlockSpec((SW_B, 512), lambda i: (prev(i), SW_KV0 // 512)),
            pl.BlockSpec((SW_B, 256), lambda i: (ci(i), 0)),
            pl.BlockSpec((SW_B, 256), lambda i: (prev(i), 0)),
            pl.BlockSpec((2, 512), lambda i: (0, 0)), pl.BlockSpec((2, 128), lambda i: (0, 0)),
            pl.BlockSpec((1, 128), lambda i: (0, 0))]


def _sw_args(u_ref, kvp_ref, csc_ref, csp_ref, gq_ref, gk_ref, sk_ref, has_prev):
    return (u_ref[:, 0:D], kvp_ref[...], u_ref[:, SW_KV0:SW_KV0 + 512], u_ref[:, SW_KV0 + 512:SW_N],
            csp_ref[...], csc_ref[...], gq_ref[...], gk_ref[...], sk_ref[...], has_prev)


def _sw_fwd(u, cs, gq, gk, sinks):
    t = u.shape[0]
    n = t // SW_B

    def body(u_ref, kvp_ref, csc_ref, csp_ref, gq_ref, gk_ref, sk_ref, p_ref):
        has_prev = pl.program_id(0) > 0
        p_ref[...] = _sw_block(*_sw_args(u_ref, kvp_ref, csc_ref, csp_ref, gq_ref, gk_ref, sk_ref, has_prev)
                               ).astype(BF16)

    return pl.pallas_call(
        body, name="sw_fwd", grid=(n,), in_specs=_sw_specs(n, False),
        out_specs=pl.BlockSpec((SW_B, D), lambda i: (i, 0)),
        out_shape=jax.ShapeDtypeStruct((t, D), BF16),
        compiler_params=_cparams(("arbitrary",)),
    )(u, u, cs, cs, gq, gk, sinks)


def _sw_bwd(u, cs, dp, gq, gk, sinks):
    t = u.shape[0]
    n = t // SW_B

    def body(u_ref, kvp_ref, csc_ref, csp_ref, gq_ref, gk_ref, sk_ref, dp_ref,
             du_ref, dgq_ref, dgk_ref, dsk_ref, dkv_ref):
        i = pl.program_id(0)

        @pl.when(i == 0)
        def _():
            for r in (dkv_ref, dgq_ref, dgk_ref, dsk_ref):
                r[...] = jnp.zeros_like(r)

        has_prev = i < n - 1
        args = _sw_args(u_ref, kvp_ref, csc_ref, csp_ref, gq_ref, gk_ref, sk_ref, has_prev)
        fn = lambda q, kvp, kvc, z, gq_, gk_, sk_: _sw_block(q, kvp, kvc, z, args[4], args[5], gq_, gk_, sk_, has_prev)
        _, vjp = jax.vjp(fn, args[0], args[1], args[2], args[3], args[6], args[7], args[8])
        dq, dkvp, dkvc, dz, dgq, dgk, dsk = vjp(dp_ref[...].astype(F32))
        du_ref[:, 0:D] = dq.astype(BF16)
        du_ref[:, SW_KV0:SW_KV0 + 512] = (dkvc + dkv_ref[...]).astype(BF16)
        du_ref[:, SW_KV0 + 512:SW_N] = dz.astype(BF16)
        dkv_ref[...] = dkvp
        dgq_ref[...] += dgq
        dgk_ref[...] += dgk
        dsk_ref[...] += dsk

    small = lambda r, w: pl.BlockSpec((r, w), lambda i: (0, 0))
    return pl.pallas_call(
        body, name="sw_bwd", grid=(n,),
        in_specs=_sw_specs(n, True) + [pl.BlockSpec((SW_B, D), lambda i: (n - 1 - i, 0))],
        out_specs=[pl.BlockSpec((SW_B, SW_N), lambda i: (n - 1 - i, 0)), small(2, 512), small(2, 128), small(1, 128)],
        out_shape=[jax.ShapeDtypeStruct((t, SW_N), BF16), jax.ShapeDtypeStruct((2, 512), F32),
                   jax.ShapeDtypeStruct((2, 128), F32), jax.ShapeDtypeStruct((1, 128), F32)],
        scratch_shapes=[pltpu.VMEM((SW_B, 512), F32)],
        compiler_params=_cparams(("arbitrary",)),
    )(u, u, cs, cs, gq, gk, sinks, dp)


def _ln_mod(x, g, scale, shift):
    y = x * lax.rsqrt(jnp.mean(x * x, axis=1, keepdims=True) + EPS) * g
    return y * (1.0 + scale) + shift


def _row_tile(t):
    return min(t, 1024)


def _ln_mm(x, g, scale, shift, w, tn):
    t, n = x.shape[0], w.shape[1]
    tm = _row_tile(t)
    vec = pl.BlockSpec((1, D), lambda i, j: (0, 0))

    def body(x_ref, g_ref, sc_ref, sh_ref, w_ref, u_ref, h_ref):
        @pl.when(pl.program_id(1) == 0)
        def _():
            h_ref[...] = _ln_mod(x_ref[...], g_ref[...], sc_ref[...], sh_ref[...]).astype(BF16)

        u_ref[...] = _dot(h_ref[...], w_ref[...], 1, 0)

    return pl.pallas_call(
        body, name="ln_mm", grid=(t // tm, n // tn),
        in_specs=[pl.BlockSpec((tm, D), lambda i, j: (i, 0)), vec, vec, vec,
                  pl.BlockSpec((D, tn), lambda i, j: (0, j))],
        out_specs=[pl.BlockSpec((tm, tn), lambda i, j: (i, j)), pl.BlockSpec((tm, D), lambda i, j: (i, 0))],
        out_shape=[jax.ShapeDtypeStruct((t, n), F32), jax.ShapeDtypeStruct((t, D), BF16)],
        compiler_params=_cparams(("arbitrary", "arbitrary")),
    )(x, g, scale, shift, w)


def _mm_res(p, w, x, gate):
    t, k = p.shape
    tm = _row_tile(t)

    def body(p_ref, w_ref, x_ref, gate_ref, o_ref):
        o_ref[...] = x_ref[...] + gate_ref[...] * _dot(p_ref[...], w_ref[...], 1, 0)

    return pl.pallas_call(
        body, name="mm_res", grid=(t // tm,),
        in_specs=[pl.BlockSpec((tm, k), lambda i: (i, 0)), pl.BlockSpec((k, D), lambda i: (0, 0)),
                  pl.BlockSpec((tm, D), lambda i: (i, 0)), pl.BlockSpec((1, D), lambda i: (0, 0))],
        out_specs=pl.BlockSpec((tm, D), lambda i: (i, 0)),
        out_shape=jax.ShapeDtypeStruct((t, D), F32),
        compiler_params=_cparams(("arbitrary",)),
    )(p, w, x, gate)


def _loss_grad(x, target):
    t = x.shape[0]
    tm = _row_tile(t)

    def body(x_ref, t_ref, l_ref, dx_ref):
        @pl.when(pl.program_id(0) == 0)
        def _():
            l_ref[...] = jnp.zeros_like(l_ref)

        err = x_ref[...] - t_ref[...]
        dx_ref[...] = err * (1.0 / D)
        l_ref[...] += 0.5 * jnp.sum(jnp.mean(err * err, axis=1, keepdims=True), axis=0, keepdims=True)

    return pl.pallas_call(
        body, name="loss_grad", grid=(t // tm,),
        in_specs=[pl.BlockSpec((tm, D), lambda i: (i, 0))] * 2,
        out_specs=[pl.BlockSpec((8, 128), lambda i: (0, 0)), pl.BlockSpec((tm, D), lambda i: (i, 0))],
        out_shape=[jax.ShapeDtypeStruct((8, 128), F32), jax.ShapeDtypeStruct((t, D), F32)],
        compiler_params=_cparams(("arbitrary",)),
    )(x, target)


def _mm_scaled(a, s, w, tn):
    t, k = a.shape
    n = w.shape[1]
    tm = _row_tile(t)

    def body(a_ref, s_ref, w_ref, o_ref):
        o_ref[...] = _dot((a_ref[...] * s_ref[...]).astype(BF16), w_ref[...], 1, 0).astype(BF16)

    return pl.pallas_call(
        body, name="mm_scaled", grid=(t // tm, n // tn),
        in_specs=[pl.BlockSpec((tm, k), lambda i, j: (i, 0)), pl.BlockSpec((1, k), lambda i, j: (0, 0)),
                  pl.BlockSpec((k, tn), lambda i, j: (0, j))],
        out_specs=pl.BlockSpec((tm, tn), lambda i, j: (i, j)),
        out_shape=jax.ShapeDtypeStruct((t, n), BF16),
        compiler_params=_cparams(("arbitrary", "arbitrary")),
    )(a, s, w)


def _mm_tn_acc(a, b, tn):
    t, m = a.shape
    n = b.shape[1]
    tk = min(t, 1024)
    nk = t // tk

    def body(a_ref, b_ref, o_ref):
        @pl.when(pl.program_id(1) == 0)
        def _():
            o_ref[...] = jnp.zeros_like(o_ref)

        o_ref[...] += _dot(a_ref[...], b_ref[...].astype(BF16), 0, 0)

    return pl.pallas_call(
        body, name="mm_tn_acc", grid=(n // tn, nk),
        in_specs=[pl.BlockSpec((tk, m), lambda j, k: (k, 0)), pl.BlockSpec((tk, tn), lambda j, k: (k, j))],
        out_specs=pl.BlockSpec((m, tn), lambda j, k: (0, j)),
        out_shape=jax.ShapeDtypeStruct((m, n), F32),
        compiler_params=_cparams(("arbitrary", "arbitrary")),
    )(a, b)


def _inproj_bwd(du, wt, x, dxp, g, scale, shift, tk):
    t, kdim = du.shape
    tm = min(t, 512)
    nk = kdim // tk
    vec = pl.BlockSpec((1, D), lambda i, k: (0, 0))

    def body(du_ref, wt_ref, x_ref, dxp_ref, g_ref, sc_ref, sh_ref, dx_ref, dv_ref, acc_ref):
        k = pl.program_id(1)

        @pl.when((pl.program_id(0) == 0) & (k == 0))
        def _():
            dv_ref[...] = jnp.zeros_like(dv_ref)

        @pl.when(k == 0)
        def _():
            acc_ref[...] = jnp.zeros_like(acc_ref)

        acc_ref[...] += _dot(du_ref[...].astype(BF16), wt_ref[...], 1, 0)

        @pl.when(k == nk - 1)
        def _():
            _, vjp = jax.vjp(_ln_mod, x_ref[...], g_ref[...], sc_ref[...], sh_ref[...])
            dx, dg, dsc, dsh = vjp(acc_ref[...])
            dx_ref[...] = dxp_ref[...] + dx
            dv_ref[0:1, :] += dg
            dv_ref[1:2, :] += dsc
            dv_ref[2:3, :] += dsh

    return pl.pallas_call(
        body, name="inproj_bwd", grid=(t // tm, nk),
        in_specs=[pl.BlockSpec((tm, tk), lambda i, k: (i, k)), pl.BlockSpec((tk, D), lambda i, k: (k, 0)),
                  pl.BlockSpec((tm, D), lambda i, k: (i, 0)), pl.BlockSpec((tm, D), lambda i, k: (i, 0)),
                  vec, vec, vec],
        out_specs=[pl.BlockSpec((tm, D), lambda i, k: (i, 0)), pl.BlockSpec((8, D), lambda i, k: (0, 0))],
        out_shape=[jax.ShapeDtypeStruct((t, D), F32), jax.ShapeDtypeStruct((8, D), F32)],
        scratch_shapes=[pltpu.VMEM((tm, D), F32)],
        compiler_params=_cparams(("arbitrary", "arbitrary")),
    )(du, wt, x, dxp, g, scale, shift)


def _outgrad(gmat, w, gate):
    k = gmat.shape[0]
    tr = 256

    def body(g_ref, w_ref, gate_ref, dw_ref, dg_ref):
        @pl.when(pl.program_id(0) == 0)
        def _():
            dg_ref[...] = jnp.zeros_like(dg_ref)

        gm = g_ref[...]
        dw_ref[...] = gm * gate_ref[...]
        dg_ref[0:1, :] += jnp.sum(gm * w_ref[...].astype(F32), axis=0, keepdims=True)

    return pl.pallas_call(
        body, name="outgrad", grid=(k // tr,),
        in_specs=[pl.BlockSpec((tr, D), lambda i: (i, 0)), pl.BlockSpec((tr, D), lambda i: (i, 0)),
                  pl.BlockSpec((1, D), lambda i: (0, 0))],
        out_specs=[pl.BlockSpec((tr, D), lambda i: (i, 0)), pl.BlockSpec((8, D), lambda i: (0, 0))],
        out_shape=[jax.ShapeDtypeStruct((k, D), F32), jax.ShapeDtypeStruct((8, D), F32)],
        compiler_params=_cparams(("arbitrary",)),
    )(gmat, w, gate)


def _rope_table(pos, freq):
    t = pos.shape[0]
    tm = _row_tile(t)

    def body(p_ref, f_ref, o_ref):
        ang = p_ref[...].astype(F32) * f_ref[...]
        o_ref[:, 0:128] = jnp.cos(ang)
        o_ref[:, 128:256] = jnp.sin(ang)

    return pl.pallas_call(
        body, name="rope_table", grid=(t // tm,),
        in_specs=[pl.BlockSpec((tm, 1), lambda i: (i, 0)), pl.BlockSpec((1, 128), lambda i: (0, 0))],
        out_specs=pl.BlockSpec((tm, 256), lambda i: (i, 0)),
        out_shape=jax.ShapeDtypeStruct((t, 256), F32),
        compiler_params=_cparams(("arbitrary",)),
    )(pos, freq)


def _ada_fwd(c_all, w, b):
    nl, _, s = w.shape

    def body(c_ref, w_ref, b_ref, o_ref):
        o_ref[0] = _mm_f32(c_ref[...], w_ref[0]) + b_ref[0]

    return pl.pallas_call(
        body, name="ada_fwd", grid=(nl,),
        in_specs=[pl.BlockSpec((8, D), lambda l: (0, 0)), pl.BlockSpec((1, D, s), lambda l: (l, 0, 0)),
                  pl.BlockSpec((1, 1, s), lambda l: (l, 0, 0))],
        out_specs=pl.BlockSpec((1, 8, s), lambda l: (l, 0, 0)),
        out_shape=jax.ShapeDtypeStruct((nl, 8, s), F32),
        compiler_params=_cparams(("arbitrary",)),
    )(c_all, w, b)


def _ada_bwd(c_all, dmod_cols, dmod_all):
    nl, _, s = dmod_cols.shape

    def body(c_ref, dc_ref, da_ref, gw_ref, gb_ref):
        gw_ref[0] = _dot(c_ref[...], dc_ref[0], 0, 0, lax.Precision.HIGHEST)
        gb_ref[0] = jnp.sum(da_ref[0], axis=0, keepdims=True)

    return pl.pallas_call(
        body, name="ada_bwd", grid=(nl,),
        in_specs=[pl.BlockSpec((8, D), lambda l: (0, 0)), pl.BlockSpec((1, 8, s), lambda l: (l, 0, 0)),
                  pl.BlockSpec((1, 8, 3 * D), lambda l: (l, 0, 0))],
        out_specs=[pl.BlockSpec((1, D, s), lambda l: (l, 0, 0)), pl.BlockSpec((1, 1, 3 * D), lambda l: (l, 0, 0))],
        out_shape=[jax.ShapeDtypeStruct((nl, D, s), F32), jax.ShapeDtypeStruct((nl, 1, 3 * D), F32)],
        compiler_params=_cparams(("arbitrary",)),
    )(c_all, dmod_cols, dmod_all)


def _lb_fn(h8):
    sm = jax.nn.softmax(h8, axis=0)
    r = lax.broadcasted_iota(jnp.int32, (8, 8), 0)
    c = lax.broadcasted_iota(jnp.int32, (8, 8), 1)
    return _mm_f32(((c >= 1) & (c <= r)).astype(F32), sm)


def _lb_fwd(h8):
    def body(h_ref, o_ref):
        o_ref[...] = _lb_fn(h_ref[...])

    return pl.pallas_call(body, name="lb_fwd", out_shape=jax.ShapeDtypeStruct((8, D), F32))(h8)


def _lb_bwd(h8, dlb8):
    def body(h_ref, d_ref, o_ref):
        _, vjp = jax.vjp(_lb_fn, h_ref[...])
        o_ref[...] = vjp(d_ref[...])[0]

    return pl.pallas_call(body, name="lb_bwd", out_shape=jax.ShapeDtypeStruct((8, D), F32))(h8, dlb8)


ADAM_LR, ADAM_B1, ADAM_B2, ADAM_EPS, ADAM_WD, ADAM_STEP = 0.001, 0.9, 0.999, 1e-08, 0.01, 10


def _adamw(w, gparts, m, v):
    r, c = w.shape
    tr = r if r * c * 4 <= (1 << 20) else max(8, ((1 << 20) // (c * 4)) // 8 * 8)
    while r % tr:
        tr -= 8
    ng = len(gparts)

    def body(*refs):
        w_ref, m_ref, v_ref = refs[0], refs[1 + ng], refs[2 + ng]
        g_ref, d_ref, nm_ref, nv_ref = refs[3 + ng:]
        g = refs[1][...]
        for gr in refs[2:1 + ng]:
            g = g + gr[...]
        mm = ADAM_B1 * m_ref[...] + (1.0 - ADAM_B1) * g
        vv = ADAM_B2 * v_ref[...] + (1.0 - ADAM_B2) * (g * g)
        m_hat = mm / (1.0 - ADAM_B1 ** ADAM_STEP)
        v_hat = vv / (1.0 - ADAM_B2 ** ADAM_STEP)
        g_ref[...] = g
        d_ref[...] = -ADAM_LR * (m_hat / (jnp.sqrt(v_hat) + ADAM_EPS) + ADAM_WD * w_ref[...])
        nm_ref[...] = mm
        nv_ref[...] = vv

    spec = pl.BlockSpec((tr, c), lambda i: (i, 0))
    return pl.pallas_call(
        body, name="adamw", grid=(r // tr,), in_specs=[spec] * (3 + ng), out_specs=[spec] * 4,
        out_shape=[jax.ShapeDtypeStruct((r, c), F32)] * 4,
        compiler_params=_cparams(("arbitrary",)),
    )(w, *gparts, m, v)


def _sum_rows(parts):
    r, c = parts[0].shape
    tr = 8
    for cand in range(min(r, 512), 7, -8):
        if r % cand == 0:
            tr = cand
            break

    def body(*refs):
        acc = refs[0][...]
        for p in refs[1:-1]:
            acc = acc + p[...]
        refs[-1][...] = acc

    spec = pl.BlockSpec((tr, c), lambda i: (i, 0))
    return pl.pallas_call(
        body, name="sum_rows", grid=(r // tr,), in_specs=[spec] * len(parts), out_specs=spec,
        out_shape=jax.ShapeDtypeStruct((r, c), F32),
        compiler_params=_cparams(("arbitrary",)),
    )(*parts)


MESH = pl.DeviceIdType.MESH
ANY = pl.BlockSpec(memory_space=pl.ANY)


def _place():
    return lax.axis_index("x"), lax.axis_index("y"), lax.axis_index("c")


def _allgather8(blk):
    m_per, n = blk.shape

    def body(x_ref, out_ref, send_sems, recv_sems, local_sem):
        x, y, c = _place()
        me, sibling = (x, y, c), (x, y, 1 - c)
        chips = [(1 - x, y), (x, 1 - y), (1 - x, 1 - y)]

        def rows(px, py, pc):
            return out_ref.at[pl.ds((4 * px + 2 * py + pc) * m_per, m_per), :]

        def copy(k, block, to, src=None):
            return pltpu.make_async_remote_copy(
                src_ref=rows(*block) if src is None else src, dst_ref=rows(*block),
                send_sem=send_sems.at[k], recv_sem=recv_sems.at[k], device_id=to, device_id_type=MESH)

        mine = pltpu.make_async_copy(x_ref, rows(*me), local_sem)
        mine.start()
        first = [copy(0, me, sibling, src=x_ref)]
        first += [copy(1 + j, me, (*chip, c), src=x_ref) for j, chip in enumerate(chips)]
        for cp in first:
            cp.start()
        passed = [copy(4 + j, (*chip, c), sibling) for j, chip in enumerate(chips)]
        for j, chip in enumerate(chips):
            copy(1 + j, (*chip, c), me).wait_recv()
            passed[j].start()
        copy(0, sibling, me).wait_recv()
        for j, chip in enumerate(chips):
            copy(4 + j, (*chip, 1 - c), me).wait_recv()
        for cp in first + passed:
            cp.wait_send()
        mine.wait()

    return pl.pallas_call(
        body, name="allgather8",
        out_shape=jax.ShapeDtypeStruct((8 * m_per, n), blk.dtype),
        in_specs=[pl.BlockSpec(memory_space=pltpu.VMEM)],
        out_specs=pl.BlockSpec(memory_space=pltpu.VMEM),
        scratch_shapes=[pltpu.SemaphoreType.DMA((7,)), pltpu.SemaphoreType.DMA((7,)), pltpu.SemaphoreType.DMA],
    )(blk)


def _chip_peers():
    x, y, c = _place()
    return [(1 - x, y, c), (x, 1 - y, c), (1 - x, 1 - y, c)]


def _chip_allgather(shard):
    def body(x_ref, out_ref, send_sems, recv_sems, local_sem):
        x, y, _ = _place()
        peers = _chip_peers()

        def copy(j, chip_index):
            return pltpu.make_async_remote_copy(
                src_ref=x_ref, dst_ref=out_ref.at[chip_index], send_sem=send_sems.at[j], recv_sem=recv_sems.at[j],
                device_id=peers[j], device_id_type=MESH)

        mine = pltpu.make_async_copy(x_ref, out_ref.at[2 * x + y], local_sem)
        mine.start()
        sends = [copy(j, 2 * x + y) for j in range(3)]
        for cp in sends:
            cp.start()
        for j in range(3):
            copy(j, 2 * peers[j][0] + peers[j][1]).wait_recv()
        for cp in sends:
            cp.wait_send()
        mine.wait()

    return pl.pallas_call(
        body, name="chip_allgather",
        out_shape=jax.ShapeDtypeStruct((4,) + shard.shape, shard.dtype),
        in_specs=[ANY], out_specs=ANY,
        scratch_shapes=[pltpu.SemaphoreType.DMA((3,)), pltpu.SemaphoreType.DMA((3,)), pltpu.SemaphoreType.DMA],
    )(shard)


def _chip_scatter(parts):
    def body(p_ref, out_ref, send_sems, recv_sems):
        peers = _chip_peers()
        sends = [pltpu.make_async_remote_copy(
            src_ref=p_ref.at[2 * peers[j][0] + peers[j][1]], dst_ref=out_ref.at[j], send_sem=send_sems.at[j],
            recv_sem=recv_sems.at[j], device_id=peers[j], device_id_type=MESH) for j in range(3)]
        for cp in sends:
            cp.start()
        for cp in sends:
            cp.wait_recv()
        for cp in sends:
            cp.wait_send()

    return pl.pallas_call(
        body, name="chip_scatter",
        out_shape=jax.ShapeDtypeStruct((3,) + parts.shape[1:], parts.dtype),
        in_specs=[ANY], out_specs=ANY,
        scratch_shapes=[pltpu.SemaphoreType.DMA((3,)), pltpu.SemaphoreType.DMA((3,))],
    )(parts)


def _sibling_swap(a):
    def body(a_ref, out_ref, send_sem, recv_sem):
        x, y, c = _place()
        cp = pltpu.make_async_remote_copy(src_ref=a_ref, dst_ref=out_ref, send_sem=send_sem, recv_sem=recv_sem,
                                          device_id=(x, y, 1 - c), device_id_type=MESH)
        cp.start()
        cp.wait_recv()
        cp.wait_send()

    return pl.pallas_call(
        body, name="sibling_swap", out_shape=jax.ShapeDtypeStruct(a.shape, a.dtype),
        in_specs=[ANY], out_specs=ANY,
        scratch_shapes=[pltpu.SemaphoreType.DMA, pltpu.SemaphoreType.DMA],
    )(a)


WEIGHTS = ['hgrn_lb', 'ada_w', 'ada_b', 'norm_g', 'hg_in_w', 'hg_out_w', 'hg_onorm', 'sw_in_w', 'sw_out_w', 'sw_qnorm',
           'sw_knorm', 'sw_sinks', 'gd_in_w', 'gd_out_w', 'gd_conv_w', 'gd_a_log', 'gd_dt_bias', 'gd_onorm']
BIG = ['hg_in_w', 'hg_out_w', 'sw_in_w', 'sw_out_w', 'gd_in_w', 'gd_out_w']
PACK_ALIGN = 16
ROPE_THETA = 10000.0
ADA_S = 3 * D // 4
SMALL_ROW = {'hg_onorm': (0, 256), 'sw_qnorm': (256, 64), 'sw_knorm': (320, 64), 'sw_sinks': (384, 16),
             'gd_a_log': (400, 16), 'gd_dt_bias': (416, 16), 'gd_onorm': (432, 128)}


def _pack_rows(arrs):
    flat = jnp.concatenate([a.reshape(-1, D) for a in arrs], axis=0)
    return jnp.pad(flat, ((0, -flat.shape[0] % PACK_ALIGN), (0, 0)))


def _unpack_rows(packed, shapes):
    out, off = [], 0
    for s in shapes:
        rows = 1
        for d in s:
            rows *= d
        rows //= D
        out.append(packed[..., off:off + rows, :].reshape(packed.shape[:-2] + tuple(s)))
        off += rows
    return out


def _pack_small(vals):
    row = jnp.concatenate([vals[k].reshape(-1) for k in SMALL_ROW])
    row = jnp.pad(row, (0, D - row.shape[0]))[None]
    return jnp.concatenate([vals['hgrn_lb'], vals['norm_g'], vals['gd_conv_w'].reshape(16, D), row,
                            jnp.zeros((7, D), F32)], axis=0)


def _sw_cols(w, inverse=False):
    def split(a, heads):
        shp = (a.shape[0], 2, heads, 32) if inverse else (a.shape[0], heads, 2, 32)
        return a.reshape(shp).transpose(0, 2, 1, 3).reshape(a.shape[0], heads * 64)
    return jnp.concatenate([split(w[:, 0:1024], 16), split(w[:, 1024:1280], 4), w[:, 1280:]], axis=1)


def kernel(x, c, positions, hgrn_lb, ada_w, ada_b, norm_g, hg_in_w, hg_out_w, hg_onorm, sw_in_w, sw_out_w, sw_qnorm, sw_knorm, sw_sinks, gd_in_w, gd_out_w, gd_conv_w, gd_a_log, gd_dt_bias, gd_onorm, loss_target, m_hgrn_lb, m_ada_w, m_ada_b, m_norm_g, m_hg_in_w, m_hg_out_w, m_hg_onorm, m_sw_in_w, m_sw_out_w, m_sw_qnorm, m_sw_knorm, m_sw_sinks, m_gd_in_w, m_gd_out_w, m_gd_conv_w, m_gd_a_log, m_gd_dt_bias, m_gd_onorm, v_hgrn_lb, v_ada_w, v_ada_b, v_norm_g, v_hg_in_w, v_hg_out_w, v_hg_onorm, v_sw_in_w, v_sw_out_w, v_sw_qnorm, v_sw_knorm, v_sw_sinks, v_gd_in_w, v_gd_out_w, v_gd_conv_w, v_gd_a_log, v_gd_dt_bias, v_gd_onorm):
    w_in = dict(hgrn_lb=hgrn_lb, ada_w=ada_w, ada_b=ada_b, norm_g=norm_g, hg_in_w=hg_in_w, hg_out_w=hg_out_w,
                hg_onorm=hg_onorm, sw_in_w=sw_in_w, sw_out_w=sw_out_w, sw_qnorm=sw_qnorm, sw_knorm=sw_knorm,
                sw_sinks=sw_sinks, gd_in_w=gd_in_w, gd_out_w=gd_out_w, gd_conv_w=gd_conv_w, gd_a_log=gd_a_log,
                gd_dt_bias=gd_dt_bias, gd_onorm=gd_onorm)
    m_in = dict(zip(WEIGHTS, (m_hgrn_lb, m_ada_w, m_ada_b, m_norm_g, m_hg_in_w, m_hg_out_w, m_hg_onorm, m_sw_in_w,
                              m_sw_out_w, m_sw_qnorm, m_sw_knorm, m_sw_sinks, m_gd_in_w, m_gd_out_w, m_gd_conv_w,
                              m_gd_a_log, m_gd_dt_bias, m_gd_onorm)))
    v_in = dict(zip(WEIGHTS, (v_hgrn_lb, v_ada_w, v_ada_b, v_norm_g, v_hg_in_w, v_hg_out_w, v_hg_onorm, v_sw_in_w,
                              v_sw_out_w, v_sw_qnorm, v_sw_knorm, v_sw_sinks, v_gd_in_w, v_gd_out_w, v_gd_conv_w,
                              v_gd_a_log, v_gd_dt_bias, v_gd_onorm)))
    ax, ay, ac = _place()
    chip = 2 * ax + ay
    bidx = 4 * ax + 2 * ay + ac
    t = x.shape[1]
    x0, target = x[0], loss_target[0]

    c_all = _allgather8(jnp.pad(c, ((0, 7), (0, 0)))).reshape(8, 8, D)[:, 0, :]
    ada_b_cols = lax.dynamic_slice(ada_b, (0, chip * ADA_S), (4, ADA_S)).reshape(4, 1, ADA_S)
    mod_sh = _ada_fwd(c_all, ada_w, ada_b_cols)
    mod_g = _allgather8(mod_sh.reshape(32, ADA_S)).reshape(4, 2, 4, 8, ADA_S)[:, 0]
    mod = lax.dynamic_index_in_dim(mod_g, bidx, axis=2, keepdims=False).transpose(1, 0, 2).reshape(4, 3 * D)
    shift = [mod[l:l + 1, 0:D] for l in range(4)]
    scale = [mod[l:l + 1, D:2 * D] for l in range(4)]
    gate = [mod[l:l + 1, 2 * D:3 * D] for l in range(4)]

    h8 = jnp.concatenate([hgrn_lb, jnp.full((4, D), -1e30, F32)], axis=0)
    lb_all = _lb_fwd(h8)
    freq = ROPE_THETA ** (-jnp.arange(0, 64, 2, dtype=F32) / 64)
    cs = _rope_table(positions.reshape(t, 1), jnp.tile(freq, 4)[None])

    big_shapes = [w_in[k].shape for k in BIG]
    gathered = _chip_allgather(_pack_rows([w_in[k] for k in BIG]).astype(BF16))
    hg_in_k, hg_out_k, sw_in_k, sw_out_k, gd_in_k, gd_out_k = _unpack_rows(gathered, big_shapes)
    hg_in_f = hg_in_k.transpose(1, 2, 0, 3).reshape(2, D, 4 * D)
    hg_out_f = hg_out_k.transpose(1, 0, 2, 3).reshape(2, D, D)
    sw_in_f = _sw_cols(sw_in_k[:, 0].transpose(1, 0, 2).reshape(D, SW_N))
    sw_out_f = sw_out_k.reshape(D, D)
    gd_in_f = jnp.pad(gd_in_k[:, 0].transpose(1, 0, 2).reshape(D, 6176), ((0, 0), (0, GD_N - 6176)))
    gd_out_f = gd_out_k.reshape(GD_VW, D)
    win = [hg_in_f[0], sw_in_f, gd_in_f, hg_in_f[1]]
    wout = [hg_out_f[0], sw_out_f, gd_out_f, hg_out_f[1]]
    tn_in = [1024, 1280, 896, 1024]
    tk_bwd = [4096, 2560, 896, 4096]

    gq = jnp.stack([jnp.tile(sw_qnorm[0, :32], 16), jnp.tile(sw_qnorm[0, 32:], 16)])
    gk = jnp.stack([jnp.tile(sw_knorm[0, :32], 4), jnp.tile(sw_knorm[0, 32:], 4)])
    pad128 = lambda a: jnp.pad(a, ((0, 0), (0, HD - a.shape[1])))
    sinks, alog, dtb = pad128(sw_sinks), pad128(gd_a_log), pad128(gd_dt_bias)
    cw8 = jnp.pad(_chip_allgather(gd_conv_w[0]).transpose(1, 0, 2).reshape(4, GD_QKV), ((0, 4), (0, 0)))
    lbs = {0: lb_all[0:1], 3: lb_all[3:4]}

    xs, us, hs, ps, stss = [x0], [], [], [], []
    for l in range(4):
        u, h = _ln_mm(xs[l], norm_g[l:l + 1], scale[l], shift[l], win[l], tn_in[l])
        if l % 3 == 0:
            p, sts = _hg_fwd(u, lbs[l], hg_onorm[l // 3:l // 3 + 1])
        elif l % 3 == 1:
            p, sts = _sw_fwd(u, cs, gq, gk, sinks), None
        else:
            p, *sts = _gd_fwd(u, cw8, alog, dtb, gd_onorm)
        xs.append(_mm_res(p, wout[l], xs[l], gate[l]))
        us.append(u), hs.append(h), ps.append(p), stss.append(sts)
    lpart, dx = _loss_grad(xs[4], target)
    loss = lax.psum(lpart[0, 0], ("x", "y", "c"))

    g_small = {}
    d_in, d_out, dmod, dnorm_g, dlb8, dgo_hg = [None] * 4, [None] * 4, [None] * 4, [None] * 4, jnp.zeros((8, D), F32), {}
    for l in (3, 2, 1, 0):
        dp = _mm_scaled(dx, gate[l], wout[l].T, 1024)
        d_out[l], dgate = _outgrad(_mm_tn_acc(ps[l], dx, 512), wout[l], gate[l])
        if l % 3 == 0:
            du, dlb, dgo_hg[l // 3] = _hg_bwd(us[l], stss[l], dp, lbs[l], hg_onorm[l // 3:l // 3 + 1])
            dlb8 = lax.dynamic_update_slice(dlb8, dlb, (l, 0))
        elif l % 3 == 1:
            du, dgq, dgk, dsk = _sw_bwd(us[l], cs, dp, gq, gk, sinks)
            g_small['sw_qnorm'] = jnp.concatenate([dgq[0].reshape(16, 32).sum(0), dgq[1].reshape(16, 32).sum(0)])
            g_small['sw_knorm'] = jnp.concatenate([dgk[0].reshape(4, 32).sum(0), dgk[1].reshape(4, 32).sum(0)])
            g_small['sw_sinks'] = dsk[0, :16]
        else:
            du, dcw, dalog, ddtb, g_small['gd_onorm'] = _gd_bwd(us[l], *stss[l], dp, cw8, alog, dtb, gd_onorm)
            g_small['gd_conv_w'], g_small['gd_a_log'], g_small['gd_dt_bias'] = dcw[:4], dalog[0, :16], ddtb[0, :16]
        d_in[l] = _mm_tn_acc(hs[l], du, 896 if l == 2 else 512)
        dx, dvec = _inproj_bwd(du, win[l].T, xs[l], dx, norm_g[l:l + 1], scale[l], shift[l], tk_bwd[l])
        dnorm_g[l] = dvec[0:1]
        dmod[l] = jnp.concatenate([dvec[2:3], dvec[1:2], dgate[0:1]], axis=1)
    grad_x = dx[None]

    g_small['hgrn_lb'] = _lb_bwd(h8, dlb8)[0:4]
    g_small['norm_g'] = jnp.concatenate(dnorm_g, axis=0)
    g_small['hg_onorm'] = jnp.concatenate([dgo_hg[0], dgo_hg[1]], axis=0)
    gs_all = _allgather8(_pack_small(g_small))
    gs = _sum_rows([gs_all[32 * d:32 * (d + 1)] for d in range(8)])

    def small_view(packed, k):
        if k == 'hgrn_lb':
            return packed[0:4]
        if k == 'norm_g':
            return packed[4:8]
        off, size = SMALL_ROW[k]
        return packed[24, off:off + size].reshape(w_in[k].shape)

    conv_sl = lambda full: lax.dynamic_slice(full.reshape(4, GD_QKV), (0, chip * D), (4, D))
    out = {}

    def put(k, res, shape):
        for name, r in zip(('grad_', 'delta_', 'new_m_', 'new_v_'), res):
            out[name + k] = r.reshape(shape)

    zero_conv = dict(gd_conv_w=jnp.zeros((4, GD_QKV), F32))
    small_names = ['hgrn_lb', 'norm_g'] + list(SMALL_ROW)
    res = _adamw(_pack_small({**{k: w_in[k] for k in small_names}, **zero_conv}), (gs,),
                 _pack_small({**{k: m_in[k] for k in small_names}, **zero_conv}),
                 _pack_small({**{k: v_in[k] for k in small_names}, **zero_conv}))
    for k in small_names:
        put(k, [small_view(r, k) for r in res], w_in[k].shape)
    put('gd_conv_w', _adamw(gd_conv_w[0], (conv_sl(gs[8:24]),), m_in['gd_conv_w'][0], v_in['gd_conv_w'][0]),
        gd_conv_w.shape)

    dm = _allgather8(jnp.pad(jnp.concatenate(dmod, axis=0), ((0, 4), (0, 0)))).reshape(8, 8, 3 * D)[:, :4]
    dm = dm.transpose(1, 0, 2)
    g_ada_w, g_ada_b = _ada_bwd(c_all, lax.dynamic_slice(dm, (0, 0, chip * ADA_S), (4, 8, ADA_S)), dm)
    put('ada_w', _adamw(ada_w.reshape(4 * D, ADA_S), (g_ada_w.reshape(4 * D, ADA_S),),
                        m_in['ada_w'].reshape(4 * D, ADA_S), v_in['ada_w'].reshape(4 * D, ADA_S)), ada_w.shape)
    put('ada_b', _adamw(ada_b, (g_ada_b.reshape(4, 3 * D),), m_in['ada_b'], v_in['ada_b']), ada_b.shape)

    by_chip = lambda g, cols: g.reshape(g.shape[0], 4, cols).transpose(1, 0, 2)
    d_sw_in = _sw_cols(d_in[1], inverse=True)
    parts = {
        'hg_in_w': jnp.stack([by_chip(d_in[0], D), by_chip(d_in[3], D)], axis=1),
        'hg_out_w': jnp.stack([d_out[0].reshape(4, D // 4, D), d_out[3].reshape(4, D // 4, D)], axis=1),
        'sw_in_w': by_chip(d_sw_in, SW_N // 4)[:, None],
        'sw_out_w': d_out[1].reshape(4, 1, D // 4, D),
        'gd_in_w': by_chip(d_in[2][:, :6176], 1544)[:, None],
        'gd_out_w': d_out[2].reshape(4, 1, GD_VW // 4, D),
    }
    packed = jnp.stack([_pack_rows([parts[k][j] for k in BIG]) for j in range(4)])
    recv = _chip_scatter(packed.astype(BF16))
    own =lax.dynamic_index_in_dim(packed, chip, axis=0, keepdims=False)
    half = _sum_rows([own, recv[0], recv[1], recv[2]])
    other = _sibling_swap(half)
    res = _adamw(_pack_rows([w_in[k] for k in BIG]), (half, other), _pack_rows([m_in[k] for k in BIG]),
                 _pack_rows([v_in[k] for k in BIG]))
    for name, r in zip(('grad_', 'delta_', 'new_m_', 'new_v_'), res):
        for k, a in zip(BIG, _unpack_rows(r, big_shapes)):
            out[name + k] = a

    return (loss, grad_x, *[out[p + k] for p in ('grad_', 'delta_', 'new_m_', 'new_v_') for k in WEIGHTS])
```

```python
import functools

import jax
import jax.numpy as jnp
from jax import lax
from jax.experimental import pallas as pl
from jax.experimental.pallas import tpu as pltpu

F32 = jnp.float32
BF16 = jnp.bfloat16
D = 1024
EPS = 1e-6
CHUNK = 64
SUB = 16
HG_H = 8
HD = 128
VMEM_LIMIT = 56 * 1024 * 1024


def _cparams(sem=None):
    return pltpu.CompilerParams(dimension_semantics=sem, vmem_limit_bytes=VMEM_LIMIT)


def _dot(a, b, ca, cb, prec=None):
    return lax.dot_general(a, b, (((ca,), (cb,)), ((), ())), precision=prec, preferred_element_type=F32)


def _mm(a, b):
    return _dot(a.astype(BF16), b.astype(BF16), 1, 0)


def _mm_nt(a, b):
    return _dot(a.astype(BF16), b.astype(BF16), 1, 1)


def _mm_tn(a, b):
    return _dot(a.astype(BF16), b.astype(BF16), 0, 0)


def _mm_f32(a, b):
    return _dot(a, b, 1, 0, lax.Precision.HIGHEST)


def _silu(x):
    return x * jax.nn.sigmoid(x)


def _cumsum_impl(x):
    row = lax.broadcasted_iota(jnp.int32, x.shape, 0)
    s = 1
    while s < x.shape[0]:
        x = x + jnp.where(row >= s, pltpu.roll(x, s, 0), 0.0)
        s *= 2
    return x


@jax.custom_vjp
def _cumsum_rows(x):
    return _cumsum_impl(x)


_cumsum_rows.defvjp(lambda x: (_cumsum_impl(x), None),
                    lambda _, g: (jnp.sum(g, axis=0, keepdims=True) - _cumsum_impl(g) + g,))


def _roll_rows(x, shift):
    n = x.shape[0]

    @jax.custom_vjp
    def f(a):
        return pltpu.roll(a, shift, 0)

    f.defvjp(lambda a: (pltpu.roll(a, shift, 0), None), lambda _, g: (pltpu.roll(g, n - shift, 0),))
    return f(x)


def _hg_chunk(q_raw, f_pre, v, z, st, lb, go):
    c = q_raw.shape[0]
    nsub = c // SUB
    lf = jnp.log(lb + (1.0 - lb) * jax.nn.sigmoid(f_pre))
    k = (1.0 - lb) * jax.nn.sigmoid(-f_pre)
    q = _silu(q_raw)
    b = _cumsum_rows(lf)
    rowf = lax.broadcasted_iota(jnp.int32, lf.shape, 0)
    bmid = [jnp.sum(jnp.where(rowf == SUB * i + SUB // 2, b, 0.0), axis=0, keepdims=True) for i in range(nsub)]
    row = lax.broadcasted_iota(jnp.int32, (c, 1), 0)
    ref = sum(jnp.where((row >= SUB * i) & (row < SUB * (i + 1)), bmid[i], 0.0) for i in range(nsub))
    qt = q * jnp.exp(b - ref)
    kall = jnp.concatenate(
        [k * jnp.exp(jnp.where(row < SUB * (i + 1), bmid[i] - b, -jnp.inf)) for i in range(nsub)], axis=0)
    v4 = jnp.concatenate([v] * nsub, axis=0)
    b_last = jnp.sum(lf, axis=0, keepdims=True)
    qb = q * jnp.exp(b)
    kd = k * jnp.exp(b_last - b)
    e_last = jnp.exp(b_last)
    tq = lax.broadcasted_iota(jnp.int32, (c, nsub * c), 0)
    cq = lax.broadcasted_iota(jnp.int32, (c, nsub * c), 1)
    m_all = ((cq // c) == (tq // SUB)) & ((cq % c) <= tq)
    hs = lambda a: jnp.split(a, HG_H, axis=1)
    qt_h, kall_h, v4_h, qb_h, kd_h, v_h, z_h, el_h = map(hs, (qt, kall, v4, qb, kd, v, z, e_last))
    st_h = jnp.split(st, HG_H, axis=0)
    p_out, st_out = [], []
    for h in range(HG_H):
        pm = jnp.where(m_all, _mm_nt(qt_h[h], kall_h[h]), 0.0)
        o = _mm(pm, v4_h[h]) + _mm_nt(qb_h[h], st_h[h])
        st_out.append(el_h[h] * st_h[h] + _mm_tn(v_h[h], kd_h[h]))
        y = o * lax.rsqrt(jnp.mean(o * o, axis=1, keepdims=True) + EPS) * go
        p_out.append(y * _silu(z_h[h]))
    return jnp.concatenate(p_out, axis=1), jnp.concatenate(st_out, axis=0)


def _hg_fwd(u, lb, go, gather=None):
    t = u.shape[0]
    n = t // CHUNK

    def body(u_ref, lb_ref, go_ref, *rest):
        if gather is None:
            p_ref, sts_ref, st_ref = rest
        else:
            shard_ref, p_ref, sts_ref, all_ref, st_ref, *sems = rest
            start, wait = _gather_plan(shard_ref, all_ref, *sems)
            pl.when(pl.program_id(0) == 0)(start)

        @pl.when(pl.program_id(0) == 0)
        def _():
            st_ref[...] = jnp.zeros_like(st_ref)

        st = st_ref[...]
        sts_ref[0] = st
        p, st_next = _hg_chunk(u_ref[:, 0:D], u_ref[:, D:2 * D], u_ref[:, 2 * D:3 * D], u_ref[:, 3 * D:4 * D],
                               st, lb_ref[...], go_ref[...])
        p_ref[...] = p.astype(BF16)
        st_ref[...] = st_next
        if gather is not None:
            pl.when(pl.program_id(0) == n - 1)(wait)

    more = gather is not None
    return pl.pallas_call(
        body, name="hg_fwd_gather" if more else "hg_fwd", grid=(n,),
        in_specs=[pl.BlockSpec((CHUNK, 4 * D), lambda i: (i, 0)),
                  pl.BlockSpec((1, D), lambda i: (0, 0)),
                  pl.BlockSpec((1, HD), lambda i: (0, 0))] + [ANY] * more,
        out_specs=[pl.BlockSpec((CHUNK, D), lambda i: (i, 0)),
                   pl.BlockSpec((1, HG_H * HD, HD), lambda i: (i, 0, 0))] + [ANY] * more,
        out_shape=[jax.ShapeDtypeStruct((t, D), BF16), jax.ShapeDtypeStruct((n, HG_H * HD, HD), F32)]
        + ([jax.ShapeDtypeStruct((4,) + gather.shape, gather.dtype)] if more else []),
        scratch_shapes=[pltpu.VMEM((HG_H * HD, HD), F32)] + GATHER_SEMS * more,
        compiler_params=_cparams(("arbitrary",)),
    )(u, lb, go, *([gather] * more))


def _hg_bwd(u, sts, dp, lb, go, scatter=None):
    t = u.shape[0]
    n = t // CHUNK

    def body(u_ref, sts_ref, dp_ref, lb_ref, go_ref, *rest):
        if scatter is None:
            du_ref, dlb_ref, dgo_ref, dst_ref = rest
        else:
            parts_ref, du_ref, dlb_ref, dgo_ref, recv_ref, dst_ref, *sems = rest
            start, wait = _scatter_plan(parts_ref, recv_ref, *sems)
            pl.when(pl.program_id(0) == 0)(start)

        @pl.when(pl.program_id(0) == 0)
        def _():
            dst_ref[...] = jnp.zeros_like(dst_ref)
            dlb_ref[...] = jnp.zeros_like(dlb_ref)
            dgo_ref[...] = jnp.zeros_like(dgo_ref)

        _, vjp = jax.vjp(_hg_chunk, u_ref[:, 0:D], u_ref[:, D:2 * D], u_ref[:, 2 * D:3 * D], u_ref[:, 3 * D:4 * D],
                         sts_ref[0], lb_ref[...], go_ref[...])
        dq, df, dv, dz, dst, dlb, dgo = vjp((dp_ref[...].astype(F32), dst_ref[...]))
        du_ref[:, 0:D] = dq.astype(BF16)
        du_ref[:, D:2 * D] = df.astype(BF16)
        du_ref[:, 2 * D:3 * D] = dv.astype(BF16)
        du_ref[:, 3 * D:4 * D] = dz.astype(BF16)
        dst_ref[...] = dst
        dlb_ref[...] += dlb
        dgo_ref[...] += dgo
        if scatter is not None:
            pl.when(pl.program_id(0) == n - 1)(wait)

    rev = lambda i: (n - 1 - i, 0)
    more = scatter is not None
    return pl.pallas_call(
        body, name="hg_bwd_scatter" if more else "hg_bwd", grid=(n,),
        in_specs=[pl.BlockSpec((CHUNK, 4 * D), rev),
                  pl.BlockSpec((1, HG_H * HD, HD), lambda i: (n - 1 - i, 0, 0)),
                  pl.BlockSpec((CHUNK, D), rev),
                  pl.BlockSpec((1, D), lambda i: (0, 0)),
                  pl.BlockSpec((1, HD), lambda i: (0, 0))] + [ANY] * more,
        out_specs=[pl.BlockSpec((CHUNK, 4 * D), rev),
                   pl.BlockSpec((1, D), lambda i: (0, 0)),
                   pl.BlockSpec((1, HD), lambda i: (0, 0))] + [ANY] * more,
        out_shape=[jax.ShapeDtypeStruct((t, 4 * D), BF16), jax.ShapeDtypeStruct((1, D), F32),
                   jax.ShapeDtypeStruct((1, HD), F32)]
        + ([jax.ShapeDtypeStruct((3,) + scatter.shape[1:], scatter.dtype)] if more else []),
        scratch_shapes=[pltpu.VMEM((HG_H * HD, HD), F32)] + SCATTER_SEMS * more,
        compiler_params=_cparams(("arbitrary",)),
    )(u, sts, dp, lb, go, *([scatter] * more))


GD_VH = 16
GD_QKH = 8
GD_QKV = 4096
GD_VW = 2048
GD_N = GD_QKV + GD_VW + HD
GD_GRP = 4
GD_SOLVE = (GD_VH // GD_GRP, GD_GRP * CHUNK, 2 * HD)
HALO = 8


def _mm_high(a, b):
    return _dot(a, b, 1, 0, lax.Precision.HIGH)


def _lane_pick(a, h):
    lane = lax.broadcasted_iota(jnp.int32, a.shape, 1)
    return jnp.sum(jnp.where(lane == h, a, 0.0), axis=1, keepdims=True)


def _l2n(x):
    return x * lax.rsqrt(jnp.sum(x * x, axis=1, keepdims=True) + EPS)


def _solve_fwd(a_mat, rhs):
    n = a_mat.shape[0]
    r_i, c_i = lax.broadcasted_iota(jnp.int32, (n, n), 0), lax.broadcasted_iota(jnp.int32, (n, n), 1)
    same = lambda nb: (r_i // nb) == (c_i // nb)
    d0 = jnp.where(same(8), a_mat, 0.0)
    d2 = _mm_high(d0, d0)
    tinv = (r_i == c_i).astype(F32) - d0
    tinv = tinv + _mm_high(tinv, d2)
    tinv = tinv + _mm_high(tinv, _mm_high(d2, d2))
    nb = 16
    while nb <= CHUNK:
        low = jnp.where(same(nb) & ~same(nb // 2), a_mat, 0.0)
        tinv = tinv - _mm(_mm(tinv, low), tinv)
        nb *= 2
    x = _mm_high(tinv, rhs)
    return x, (tinv, x)


def _solve_bwd(res, dx):
    tinv, x = res
    drhs = _dot(tinv, dx, 0, 0, lax.Precision.HIGH)
    return -_dot(drhs, x, 1, 1, lax.Precision.HIGH), drhs


@jax.custom_vjp
def _solved(a_mat, rhs, tinv, x):
    return x


_solved.defvjp(lambda a_mat, rhs, tinv, x: (x, (tinv, x)),
               lambda res, dx: _solve_bwd(res, dx) + (jnp.zeros_like(res[0]), jnp.zeros_like(res[1])))


def _gd_chunk(xh, x, z, ab, st, cw, alog, dtb, go, solve):
    c = x.shape[0]
    xa = jnp.concatenate([xh, x], axis=0)
    sh = [jnp.split(_roll_rows(xa, 3 - j), [HALO], axis=0)[1] for j in range(3)]
    qkv = _silu(cw[0:1] * sh[0] + cw[1:2] * sh[1] + cw[2:3] * sh[2] + cw[3:4] * x)
    q_all, k_all, v_all = jnp.split(qkv, [1024, 2048], axis=1)
    lane = lax.broadcasted_iota(jnp.int32, (c, HD), 1)
    a_part = jnp.where(lane < GD_VH, ab, 0.0)
    g_all = -jnp.exp(alog) * jax.nn.softplus(a_part + dtb)
    d_all = _cumsum_rows(g_all)
    dl_all = jnp.sum(g_all, axis=0, keepdims=True)
    beta_all = jax.nn.sigmoid(ab)
    gc = GD_GRP * c
    r_i, c_i = lax.broadcasted_iota(jnp.int32, (gc, gc), 0), lax.broadcasted_iota(jnp.int32, (gc, gc), 1)
    same_head = (r_i // c) == (c_i // c)
    tri_g, strict_g = same_head & (c_i <= r_i), same_head & (c_i < r_i)
    qs =jnp.split(q_all, GD_QKH, axis=1)
    ks = jnp.split(k_all, GD_QKH, axis=1)
    vs = jnp.split(v_all, GD_VH, axis=1)
    zs = jnp.split(z, GD_VH, axis=1)
    sts = jnp.split(st, GD_VH, axis=0)
    qn = [_l2n(a) * (HD ** -0.5) for a in qs]
    kn = [_l2n(a) for a in ks]
    p_out, st_out = [], []
    for g in range(GD_VH // GD_GRP):
        heads = range(GD_GRP * g, GD_GRP * (g + 1))
        stack = lambda f: jnp.concatenate([f(h) for h in heads], axis=0)
        q_, k_, v_ = stack(lambda h: qn[h // 2]), stack(lambda h: kn[h // 2]), stack(lambda h: vs[h])
        dcol = stack(lambda h: _lane_pick(d_all, h))
        bcol = stack(lambda h: _lane_pick(beta_all, GD_VH + h))
        dlast = stack(lambda h: jnp.broadcast_to(_lane_pick(dl_all, h), (c, 1)))
        drow = jnp.sum(jnp.broadcast_to(dcol, (gc, HD)).T, axis=0, keepdims=True) * (1.0 / HD)
        dec = jnp.exp(jnp.where(tri_g, dcol - drow, -jnp.inf))
        kb = k_ * bcol
        a_mat = jnp.where(strict_g, _mm_nt(kb, k_) * dec, 0.0)
        xsol = solve(g, a_mat, jnp.concatenate([v_ * bcol, kb * jnp.exp(dcol)], axis=1))
        u_, w_ = jnp.split(xsol, 2, axis=1)
        w_h = jnp.split(w_, GD_GRP, axis=0)
        v_new = u_ - jnp.concatenate([_mm(w_h[i], sts[h]) for i, h in enumerate(heads)], axis=0)
        qd_h = jnp.split(q_ * jnp.exp(dcol), GD_GRP, axis=0)
        o_g = _mm(_mm_nt(q_, k_) * dec, v_new) + jnp.concatenate(
            [_mm(qd_h[i], sts[h]) for i, h in enumerate(heads)], axis=0)
        kd_h = jnp.split(k_ * jnp.exp(dlast - dcol), GD_GRP, axis=0)
        vn_h = jnp.split(v_new, GD_GRP, axis=0)
        o_h = jnp.split(o_g, GD_GRP, axis=0)
        for i, h in enumerate(heads):
            st_out.append(sts[h] * jnp.exp(_lane_pick(dl_all, h)) + _mm_tn(kd_h[i], vn_h[i]))
            o = o_h[i]
            y = o * lax.rsqrt(jnp.mean(o * o, axis=1, keepdims=True) + EPS) * go
            p_out.append(y * _silu(zs[h]))
    return jnp.concatenate(p_out, axis=1), jnp.concatenate(st_out, axis=0)


def _gd_specs(n, rev):
    ci = (lambda i: n - 1 - i) if rev else (lambda i: i)
    return [pl.BlockSpec((HALO, GD_QKV), lambda i: (jnp.maximum(ci(i) * (CHUNK // HALO) - 1, 0), 0)),
            pl.BlockSpec((CHUNK, GD_N), lambda i: (ci(i), 0))]


def _gd_load(uh_ref, u_ref, first):
    xh = jnp.where(first, 0.0, uh_ref[...])
    return xh, u_ref[:, 0:GD_QKV], u_ref[:, GD_QKV:GD_QKV + GD_VW], u_ref[:, GD_QKV + GD_VW:GD_N]


def _gd_fwd(u, cw, alog, dtb, go):
    t = u.shape[0]
    n = t // CHUNK
    small = lambda r, w: pl.BlockSpec((r, w), lambda i: (0, 0))

    def body(uh_ref, u_ref, cw_ref, alog_ref, dtb_ref, go_ref, p_ref, sts_ref, tinv_ref, xsol_ref, st_ref):
        i = pl.program_id(0)

        @pl.when(i == 0)
        def _():
            st_ref[...] = jnp.zeros_like(st_ref)

        def solve(g, a_mat, rhs):
            xsol, (tinv, _) = _solve_fwd(a_mat, rhs)
            tinv_ref[0, g] = tinv
            xsol_ref[0, g] = xsol
            return xsol

        st = st_ref[...]
        sts_ref[0] = st
        p, st_next = _gd_chunk(*_gd_load(uh_ref, u_ref, i == 0), st, cw_ref[...], alog_ref[...], dtb_ref[...],
                               go_ref[...], solve)
        p_ref[...] = p.astype(BF16)
        st_ref[...] = st_next

    return pl.pallas_call(
        body, name="gd_fwd", grid=(n,),
        in_specs=_gd_specs(n, False) + [small(8, GD_QKV), small(1, HD), small(1, HD), small(1, HD)],
        out_specs=[pl.BlockSpec((CHUNK, GD_VW), lambda i: (i, 0)),
                   pl.BlockSpec((1, GD_VH * HD, HD), lambda i: (i, 0, 0)),
                   pl.BlockSpec((1,) + GD_SOLVE, lambda i: (i, 0, 0, 0)),
                   pl.BlockSpec((1,) + GD_SOLVE, lambda i: (i, 0, 0, 0))],
        out_shape=[jax.ShapeDtypeStruct((t, GD_VW), BF16), jax.ShapeDtypeStruct((n, GD_VH * HD, HD), F32),
                   jax.ShapeDtypeStruct((n,) + GD_SOLVE, F32), jax.ShapeDtypeStruct((n,) + GD_SOLVE, F32)],
        scratch_shapes=[pltpu.VMEM((GD_VH * HD, HD), F32)],
        compiler_params=_cparams(("arbitrary",)),
    )(u, u, cw, alog, dtb, go)


def _gd_bwd(u, sts, tinvs, xsols, dp, cw, alog, dtb, go):
    t = u.shape[0]
    n = t // CHUNK
    small = lambda r, w: pl.BlockSpec((r, w), lambda i: (0, 0))

    def body(uh_ref, u_ref, sts_ref, tinv_ref, xsol_ref, dp_ref, cw_ref, alog_ref, dtb_ref, go_ref,
             du_ref, dcw_ref, dalog_ref, ddtb_ref, dgo_ref, dst_ref, dhalo_ref):
        i = pl.program_id(0)

        @pl.when(i == 0)
        def _():
            for r in (dst_ref, dhalo_ref, dcw_ref, dalog_ref, ddtb_ref, dgo_ref):
                r[...] = jnp.zeros_like(r)

        solve = lambda g, a_mat, rhs: _solved(a_mat, rhs, tinv_ref[0, g], xsol_ref[0, g])
        chunk = functools.partial(_gd_chunk, solve=solve)
        _, vjp = jax.vjp(chunk, *_gd_load(uh_ref, u_ref, i == n - 1), sts_ref[0], cw_ref[...], alog_ref[...],
                         dtb_ref[...], go_ref[...])
        dxh, dx, dz, dab, dst, dcw, dalog, ddtb, dgo = vjp((dp_ref[...].astype(F32), dst_ref[...]))
        tail = jnp.concatenate([jnp.zeros((CHUNK - HALO, GD_QKV), F32), dhalo_ref[...]], axis=0)
        du_ref[:, 0:GD_QKV] = (dx + tail).astype(BF16)
        du_ref[:, GD_QKV:GD_QKV + GD_VW] = dz.astype(BF16)
        du_ref[:, GD_QKV + GD_VW:GD_N] = dab.astype(BF16)
        dhalo_ref[...] = dxh
        dst_ref[...] = dst
        dcw_ref[...] += dcw
        dalog_ref[...] += dalog
        ddtb_ref[...] += ddtb
        dgo_ref[...] += dgo

    return pl.pallas_call(
        body, name="gd_bwd", grid=(n,),
        in_specs=_gd_specs(n, True) + [pl.BlockSpec((1, GD_VH * HD, HD), lambda i: (n - 1 - i, 0, 0)),
                                       pl.BlockSpec((1,) + GD_SOLVE, lambda i: (n - 1 - i, 0, 0, 0)),
                                       pl.BlockSpec((1,) + GD_SOLVE, lambda i: (n - 1 - i, 0, 0, 0)),
                                       pl.BlockSpec((CHUNK, GD_VW), lambda i: (n - 1 - i, 0)),
                                       small(8, GD_QKV), small(1, HD), small(1, HD), small(1, HD)],
        out_specs=[pl.BlockSpec((CHUNK, GD_N), lambda i: (n - 1 - i, 0)),
                   small(8, GD_QKV), small(1, HD), small(1, HD), small(1, HD)],
        out_shape=[jax.ShapeDtypeStruct((t, GD_N), BF16), jax.ShapeDtypeStruct((8, GD_QKV), F32)]
        + [jax.ShapeDtypeStruct((1, HD), F32)] * 3,
        scratch_shapes=[pltpu.VMEM((GD_VH * HD, HD), F32), pltpu.VMEM((HALO, GD_QKV), F32)],
        compiler_params=_cparams(("arbitrary",)),
    )(u, u, sts, tinvs, xsols, dp, cw, alog, dtb, go)


SW_B = 128
SW_H = 16
SW_G = 4
SW_N = 2560
SW_KV0 = 1024


def _blockdiag(n, blk):
    r = lax.broadcasted_iota(jnp.int32, (n, n), 0) // blk
    c = lax.broadcasted_iota(jnp.int32, (n, n), 1) // blk
    return (r == c).astype(F32)


def _sw_normrope(x, g1, g2, cos, sin):
    w = x.shape[1] // 2
    x1, x2 = jnp.split(x, 2, axis=1)
    ms = _mm_high(x1 * x1 + x2 * x2, _blockdiag(w, 32)) * (1.0 / 64.0)
    rinv = lax.rsqrt(ms + EPS)
    n1, n2 = x1 * rinv * g1, x2 * rinv * g2
    return jnp.concatenate([n1 * cos - n2 * sin, n2 * cos + n1 * sin], axis=1)


def _sw_block(q, kvp, kvc, z, csp, csc, gq, gk, sinks, has_prev):
    b = q.shape[0]
    cos_c, sin_c = jnp.split(csc, 2, axis=1)
    cos_p, sin_p = jnp.split(csp, 2, axis=1)
    tile4 = lambda a: jnp.concatenate([a] * 4, axis=1)
    qh = _sw_normrope(q, gq[0:1], gq[1:2], tile4(cos_c), tile4(sin_c))
    kp, vp = jnp.split(kvp, 2, axis=1)
    kc, vc = jnp.split(kvc, 2, axis=1)
    kh = jnp.concatenate([_sw_normrope(kp, gk[0:1], gk[1:2], cos_p, sin_p),
                          _sw_normrope(kc, gk[0:1], gk[1:2], cos_c, sin_c)], axis=0)
    vv = jnp.concatenate([vp, vc], axis=0)
    q1, q2 = jnp.split(qh, 2, axis=1)
    q1g, q2g = jnp.split(q1, SW_G, axis=1), jnp.split(q2, SW_G, axis=1)
    qi = lax.broadcasted_iota(jnp.int32, (4 * b, 2 * b), 0) % b
    kj = lax.broadcasted_iota(jnp.int32, (4 * b, 2 * b), 1)
    rel = qi + b - kj
    mask = (rel >= 0) & (rel < SW_B) & (has_prev | (kj >= b))
    ri = lax.broadcasted_iota(jnp.int32, (256, 256), 0)
    ci = lax.broadcasted_iota(jnp.int32, (256, 256), 1)
    row_head = lax.broadcasted_iota(jnp.int32, (4 * b, 256), 0) // b
    lane_q = lax.broadcasted_iota(jnp.int32, (4 * b, 256), 1)
    q_sel = (lane_q % 128) // 32 == row_head
    o_sel = lane_q // 64 == row_head
    o_out = []
    for g in range(SW_G):
        ek = ((ri // 128 == ci // 128) & ((ri % 128) // 32 == g) & (ri % 32 == ci % 32)).astype(F32)
        ev = ((ri // 64 == g) & (ri % 64 == ci % 64)).astype(F32)
        kx = _mm(kh, ek)
        vx = _mm(vv, ev)
        qg = jnp.concatenate([q1g[g], q2g[g]], axis=1)
        q4 = jnp.where(q_sel, jnp.concatenate([qg] * 4, axis=0), 0.0)
        sink = jnp.concatenate([jnp.broadcast_to(_lane_pick(sinks, 4 * g + j), (b, 1)) for j in range(4)], axis=0)
        s = jnp.where(mask, _mm_nt(q4, kx) * (64 ** -0.5), -jnp.inf)
        m = jnp.maximum(jnp.max(s, axis=1, keepdims=True), sink)
        p = jnp.exp(s - m)
        pn = p / (jnp.sum(p, axis=1, keepdims=True) + jnp.exp(sink - m))
        o4 = jnp.split(jnp.where(o_sel, _mm(pn, vx), 0.0), 4, axis=0)
        o_out.append(o4[0] + o4[1] + o4[2] + o4[3])
    return jnp.concatenate(o_out, axis=1) * _silu(z)


def _sw_specs(n, rev):
    ci = (lambda i: n - 1 - i) if rev else (lambda i: i)
    prev = lambda i: jnp.maximum(ci(i) - 1, 0)
    return [pl.BlockSpec((SW_B, SW_N), lambda i: (ci(i), 0)),
            pl.BlockSpec((SW_B, 512), lambda i: (prev(i), SW_KV0 // 512)),
            pl.BlockSpec((SW_B, 256), lambda i: (ci(i), 0)),
            pl.BlockSpec((SW_B, 256), lambda i: (prev(i), 0)),
            pl.BlockSpec((2, 512), lambda i: (0, 0)), pl.BlockSpec((2, 128), lambda i: (0, 0)),
            pl.BlockSpec((1, 128), lambda i: (0, 0))]


def _sw_args(u_ref, kvp_ref, csc_ref, csp_ref, gq_ref, gk_ref, sk_ref, has_prev):
    return (u_ref[:, 0:D], kvp_ref[...], u_ref[:, SW_KV0:SW_KV0 + 512], u_ref[:, SW_KV0 + 512:SW_N],
            csp_ref[...], csc_ref[...], gq_ref[...], gk_ref[...], sk_ref[...], has_prev)


def _sw_fwd(u, cs, gq, gk, sinks):
    t = u.shape[0]
    n = t // SW_B

    def body(u_ref, kvp_ref, csc_ref, csp_ref, gq_ref, gk_ref, sk_ref, p_ref):
        has_prev = pl.program_id(0) > 0
        p_ref[...] = _sw_block(*_sw_args(u_ref, kvp_ref, csc_ref, csp_ref, gq_ref, gk_ref, sk_ref, has_prev)
                               ).astype(BF16)

    return pl.pallas_call(
        body, name="sw_fwd", grid=(n,), in_specs=_sw_specs(n, False),
        out_specs=pl.BlockSpec((SW_B, D), lambda i: (i, 0)),
        out_shape=jax.ShapeDtypeStruct((t, D), BF16),
        compiler_params=_cparams(("arbitrary",)),
    )(u, u, cs, cs, gq, gk, sinks)


def _sw_bwd(u, cs, dp, gq, gk, sinks):
    t = u.shape[0]
    n = t // SW_B

    def body(u_ref, kvp_ref, csc_ref, csp_ref, gq_ref, gk_ref, sk_ref, dp_ref,
             du_ref, dgq_ref, dgk_ref, dsk_ref, dkv_ref):
        i = pl.program_id(0)

        @pl.when(i == 0)
        def _():
            for r in (dkv_ref, dgq_ref, dgk_ref, dsk_ref):
                r[...] = jnp.zeros_like(r)

        has_prev = i < n - 1
        args = _sw_args(u_ref, kvp_ref, csc_ref, csp_ref, gq_ref, gk_ref, sk_ref, has_prev)
        fn = lambda q, kvp, kvc, z, gq_, gk_, sk_: _sw_block(q, kvp, kvc, z, args[4], args[5], gq_, gk_, sk_, has_prev)
        _, vjp = jax.vjp(fn, args[0], args[1], args[2], args[3], args[6], args[7], args[8])
        dq, dkvp, dkvc, dz, dgq, dgk, dsk = vjp(dp_ref[...].astype(F32))
        du_ref[:, 0:D] = dq.astype(BF16)
        du_ref[:, SW_KV0:SW_KV0 + 512] = (dkvc + dkv_ref[...]).astype(BF16)
        du_ref[:, SW_KV0 + 512:SW_N] = dz.astype(BF16)
        dkv_ref[...] = dkvp
        dgq_ref[...] += dgq
        dgk_ref[...] += dgk
        dsk_ref[...] += dsk

    small = lambda r, w: pl.BlockSpec((r, w), lambda i: (0, 0))
    return pl.pallas_call(
        body, name="sw_bwd", grid=(n,),
        in_specs=_sw_specs(n, True) + [pl.BlockSpec((SW_B, D), lambda i: (n - 1 - i, 0))],
        out_specs=[pl.BlockSpec((SW_B, SW_N), lambda i: (n - 1 - i, 0)), small(2, 512), small(2, 128), small(1, 128)],
        out_shape=[jax.ShapeDtypeStruct((t, SW_N), BF16), jax.ShapeDtypeStruct((2, 512), F32),
                   jax.ShapeDtypeStruct((2, 128), F32), jax.ShapeDtypeStruct((1, 128), F32)],
        scratch_shapes=[pltpu.VMEM((SW_B, 512), F32)],
        compiler_params=_cparams(("arbitrary",)),
    )(u, u, cs, cs, gq, gk, sinks, dp)


def _ln_mod(x, g, scale, shift):
    y = x * lax.rsqrt(jnp.mean(x * x, axis=1, keepdims=True) + EPS) * g
    return y * (1.0 + scale) + shift


def _row_tile(t):
    return min(t, 1024)


def _ln_mm(x, g, scale, shift, w, tn):
    t, n = x.shape[0], w.shape[1]
    tm = _row_tile(t)
    vec = pl.BlockSpec((1, D), lambda i, j: (0, 0))

    def body(x_ref, g_ref, sc_ref, sh_ref, w_ref, u_ref, h_ref):
        @pl.when(pl.program_id(1) == 0)
        def _():
            h_ref[...] = _ln_mod(x_ref[...], g_ref[...], sc_ref[...], sh_ref[...]).astype(BF16)

        u_ref[...] = _dot(h_ref[...], w_ref[...], 1, 0)

    return pl.pallas_call(
        body, name="ln_mm", grid=(t // tm, n // tn),
        in_specs=[pl.BlockSpec((tm, D), lambda i, j: (i, 0)), vec, vec, vec,
                  pl.BlockSpec((D, tn), lambda i, j: (0, j))],
        out_specs=[pl.BlockSpec((tm, tn), lambda i, j: (i, j)), pl.BlockSpec((tm, D), lambda i, j: (i, 0))],
        out_shape=[jax.ShapeDtypeStruct((t, n), F32), jax.ShapeDtypeStruct((t, D), BF16)],
        compiler_params=_cparams(("arbitrary", "arbitrary")),
    )(x, g, scale, shift, w)


def _mm_res(p, w, x, gate):
    t, k = p.shape
    tm = _row_tile(t)

    def body(p_ref, w_ref, x_ref, gate_ref, o_ref):
        o_ref[...] = x_ref[...] + gate_ref[...] * _dot(p_ref[...], w_ref[...], 1, 0)

    return pl.pallas_call(
        body, name="mm_res", grid=(t // tm,),
        in_specs=[pl.BlockSpec((tm, k), lambda i: (i, 0)), pl.BlockSpec((k, D), lambda i: (0, 0)),
                  pl.BlockSpec((tm, D), lambda i: (i, 0)), pl.BlockSpec((1, D), lambda i: (0, 0))],
        out_specs=pl.BlockSpec((tm, D), lambda i: (i, 0)),
        out_shape=jax.ShapeDtypeStruct((t, D), F32),
        compiler_params=_cparams(("arbitrary",)),
    )(p, w, x, gate)


def _loss_grad(x, target):
    t = x.shape[0]
    tm = _row_tile(t)

    def body(x_ref, t_ref, l_ref, dx_ref):
        @pl.when(pl.program_id(0) == 0)
        def _():
            l_ref[...] = jnp.zeros_like(l_ref)

        err = x_ref[...] - t_ref[...]
        dx_ref[...] = err * (1.0 / D)
        l_ref[...] += 0.5 * jnp.sum(jnp.mean(err * err, axis=1, keepdims=True), axis=0, keepdims=True)

    return pl.pallas_call(
        body, name="loss_grad", grid=(t // tm,),
        in_specs=[pl.BlockSpec((tm, D), lambda i: (i, 0))] * 2,
        out_specs=[pl.BlockSpec((8, 128), lambda i: (0, 0)), pl.BlockSpec((tm, D), lambda i: (i, 0))],
        out_shape=[jax.ShapeDtypeStruct((8, 128), F32), jax.ShapeDtypeStruct((t, D), F32)],
        compiler_params=_cparams(("arbitrary",)),
    )(x, target)


def _mm_scaled(a, s, w, tn):
    t, k = a.shape
    n = w.shape[1]
    tm = _row_tile(t)

    def body(a_ref, s_ref, w_ref, o_ref):
        o_ref[...] = _dot((a_ref[...] * s_ref[...]).astype(BF16), w_ref[...], 1, 0).astype(BF16)

    return pl.pallas_call(
        body, name="mm_scaled", grid=(t // tm, n // tn),
        in_specs=[pl.BlockSpec((tm, k), lambda i, j: (i, 0)), pl.BlockSpec((1, k), lambda i, j: (0, 0)),
                  pl.BlockSpec((k, tn), lambda i, j: (0, j))],
        out_specs=pl.BlockSpec((tm, tn), lambda i, j: (i, j)),
        out_shape=jax.ShapeDtypeStruct((t, n), BF16),
        compiler_params=_cparams(("arbitrary", "arbitrary")),
    )(a, s, w)


def _mm_tn_acc(a, b, tn):
    t, m = a.shape
    n = b.shape[1]
    tk = min(t, 1024)
    nk = t // tk

    def body(a_ref, b_ref, o_ref):
        @pl.when(pl.program_id(1) == 0)
        def _():
            o_ref[...] = jnp.zeros_like(o_ref)

        o_ref[...] += _dot(a_ref[...], b_ref[...].astype(BF16), 0, 0)

    return pl.pallas_call(
        body, name="mm_tn_acc", grid=(n // tn, nk),
        in_specs=[pl.BlockSpec((tk, m), lambda j, k: (k, 0)), pl.BlockSpec((tk, tn), lambda j, k: (k, j))],
        out_specs=pl.BlockSpec((m, tn), lambda j, k: (0, j)),
        out_shape=jax.ShapeDtypeStruct((m, n), F32),
        compiler_params=_cparams(("arbitrary", "arbitrary")),
    )(a, b)


def _inproj_bwd(du, wt, x, dxp, g, scale, shift, tk):
    t, kdim = du.shape
    tm = min(t, 512)
    nk = kdim // tk
    vec = pl.BlockSpec((1, D), lambda i, k: (0, 0))

    def body(du_ref, wt_ref, x_ref, dxp_ref, g_ref, sc_ref, sh_ref, dx_ref, dv_ref, acc_ref):
        k = pl.program_id(1)

        @pl.when((pl.program_id(0) == 0) & (k == 0))
        def _():
            dv_ref[...] = jnp.zeros_like(dv_ref)

        @pl.when(k == 0)
        def _():
            acc_ref[...] = jnp.zeros_like(acc_ref)

        acc_ref[...] += _dot(du_ref[...].astype(BF16), wt_ref[...], 1, 0)

        @pl.when(k == nk - 1)
        def _():
            _, vjp = jax.vjp(_ln_mod, x_ref[...], g_ref[...], sc_ref[...], sh_ref[...])
            dx, dg, dsc, dsh = vjp(acc_ref[...])
            dx_ref[...] = dxp_ref[...] + dx
            dv_ref[0:1, :] += dg
            dv_ref[1:2, :] += dsc
            dv_ref[2:3, :] += dsh

    return pl.pallas_call(
        body, name="inproj_bwd", grid=(t // tm, nk),
        in_specs=[pl.BlockSpec((tm, tk), lambda i, k: (i, k)), pl.BlockSpec((tk, D), lambda i, k: (k, 0)),
                  pl.BlockSpec((tm, D), lambda i, k: (i, 0)), pl.BlockSpec((tm, D), lambda i, k: (i, 0)),
                  vec, vec, vec],
        out_specs=[pl.BlockSpec((tm, D), lambda i, k: (i, 0)), pl.BlockSpec((8, D), lambda i, k: (0, 0))],
        out_shape=[jax.ShapeDtypeStruct((t, D), F32), jax.ShapeDtypeStruct((8, D), F32)],
        scratch_shapes=[pltpu.VMEM((tm, D), F32)],
        compiler_params=_cparams(("arbitrary", "arbitrary")),
    )(du, wt, x, dxp, g, scale, shift)


def _outgrad(gmat, w, gate):
    k = gmat.shape[0]
    tr = 256

    def body(g_ref, w_ref, gate_ref, dw_ref, dg_ref):
        @pl.when(pl.program_id(0) == 0)
        def _():
            dg_ref[...] = jnp.zeros_like(dg_ref)

        gm = g_ref[...]
        dw_ref[...] = gm * gate_ref[...]
        dg_ref[0:1, :] += jnp.sum(gm * w_ref[...].astype(F32), axis=0, keepdims=True)

    return pl.pallas_call(
        body, name="outgrad", grid=(k // tr,),
        in_specs=[pl.BlockSpec((tr, D), lambda i: (i, 0)), pl.BlockSpec((tr, D), lambda i: (i, 0)),
                  pl.BlockSpec((1, D), lambda i: (0, 0))],
        out_specs=[pl.BlockSpec((tr, D), lambda i: (i, 0)), pl.BlockSpec((8, D), lambda i: (0, 0))],
        out_shape=[jax.ShapeDtypeStruct((k, D), F32), jax.ShapeDtypeStruct((8, D), F32)],
        compiler_params=_cparams(("arbitrary",)),
    )(gmat, w, gate)


def _rope_table(pos, freq):
    t = pos.shape[0]
    tm = _row_tile(t)

    def body(p_ref, f_ref, o_ref):
        ang = p_ref[...].astype(F32) * f_ref[...]
        o_ref[:, 0:128] = jnp.cos(ang)
        o_ref[:, 128:256] = jnp.sin(ang)

    return pl.pallas_call(
        body, name="rope_table", grid=(t // tm,),
        in_specs=[pl.BlockSpec((tm, 1), lambda i: (i, 0)), pl.BlockSpec((1, 128), lambda i: (0, 0))],
        out_specs=pl.BlockSpec((tm, 256), lambda i: (i, 0)),
        out_shape=jax.ShapeDtypeStruct((t, 256), F32),
        compiler_params=_cparams(("arbitrary",)),
    )(pos, freq)


def _ada_fwd(c_all, w, b):
    nl, _, s = w.shape

    def body(c_ref, w_ref, b_ref, o_ref):
        o_ref[0] = _mm_f32(c_ref[...], w_ref[0]) + b_ref[0]

    return pl.pallas_call(
        body, name="ada_fwd", grid=(nl,),
        in_specs=[pl.BlockSpec((8, D), lambda l: (0, 0)), pl.BlockSpec((1, D, s), lambda l: (l, 0, 0)),
                  pl.BlockSpec((1, 1, s), lambda l: (l, 0, 0))],
        out_specs=pl.BlockSpec((1, 8, s), lambda l: (l, 0, 0)),
        out_shape=jax.ShapeDtypeStruct((nl, 8, s), F32),
        compiler_params=_cparams(("arbitrary",)),
    )(c_all, w, b)


def _ada_bwd(c_all, dmod_cols, dmod_all):
    nl, _, s = dmod_cols.shape

    def body(c_ref, dc_ref, da_ref, gw_ref, gb_ref):
        gw_ref[0] = _dot(c_ref[...], dc_ref[0], 0, 0, lax.Precision.HIGHEST)
        gb_ref[0] = jnp.sum(da_ref[0], axis=0, keepdims=True)

    return pl.pallas_call(
        body, name="ada_bwd", grid=(nl,),
        in_specs=[pl.BlockSpec((8, D), lambda l: (0, 0)), pl.BlockSpec((1, 8, s), lambda l: (l, 0, 0)),
                  pl.BlockSpec((1, 8, 3 * D), lambda l: (l, 0, 0))],
        out_specs=[pl.BlockSpec((1, D, s), lambda l: (l, 0, 0)), pl.BlockSpec((1, 1, 3 * D), lambda l: (l, 0, 0))],
        out_shape=[jax.ShapeDtypeStruct((nl, D, s), F32), jax.ShapeDtypeStruct((nl, 1, 3 * D), F32)],
        compiler_params=_cparams(("arbitrary",)),
    )(c_all, dmod_cols, dmod_all)


def _lb_fn(h8):
    sm = jax.nn.softmax(h8, axis=0)
    r = lax.broadcasted_iota(jnp.int32, (8, 8), 0)
    c = lax.broadcasted_iota(jnp.int32, (8, 8), 1)
    return _mm_f32(((c >= 1) & (c <= r)).astype(F32), sm)


def _lb_fwd(h8):
    def body(h_ref, o_ref):
        o_ref[...] = _lb_fn(h_ref[...])

    return pl.pallas_call(body, name="lb_fwd", out_shape=jax.ShapeDtypeStruct((8, D), F32))(h8)


def _lb_bwd(h8, dlb8):
    def body(h_ref, d_ref, o_ref):
        _, vjp = jax.vjp(_lb_fn, h_ref[...])
        o_ref[...] = vjp(d_ref[...])[0]

    return pl.pallas_call(body, name="lb_bwd", out_shape=jax.ShapeDtypeStruct((8, D), F32))(h8, dlb8)


ADAM_LR, ADAM_B1, ADAM_B2, ADAM_EPS, ADAM_WD, ADAM_STEP = 0.001, 0.9, 0.999, 1e-08, 0.01, 10


def _adamw(w, gparts, m, v):
    r, c = w.shape
    tr = r if r * c * 4 <= (1 << 20) else max(8, ((1 << 20) // (c * 4)) // 8 * 8)
    while r % tr:
        tr -= 8
    ng = len(gparts)

    def body(*refs):
        w_ref, m_ref, v_ref = refs[0], refs[1 + ng], refs[2 + ng]
        g_ref, d_ref, nm_ref, nv_ref = refs[3 + ng:]
        g = refs[1][...]
        for gr in refs[2:1 + ng]:
            g = g + gr[...]
        mm = ADAM_B1 * m_ref[...] + (1.0 - ADAM_B1) * g
        vv = ADAM_B2 * v_ref[...] + (1.0 - ADAM_B2) * (g * g)
        m_hat = mm / (1.0 - ADAM_B1 ** ADAM_STEP)
        v_hat = vv / (1.0 - ADAM_B2 ** ADAM_STEP)
        g_ref[...] = g
        d_ref[...] = -ADAM_LR * (m_hat / (jnp.sqrt(v_hat) + ADAM_EPS) + ADAM_WD * w_ref[...])
        nm_ref[...] = mm
        nv_ref[...] = vv

    spec = pl.BlockSpec((tr, c), lambda i: (i, 0))
    return pl.pallas_call(
        body, name="adamw", grid=(r // tr,), in_specs=[spec] * (3 + ng), out_specs=[spec] * 4,
        out_shape=[jax.ShapeDtypeStruct((r, c), F32)] * 4,
        compiler_params=_cparams(("arbitrary",)),
    )(w, *gparts, m, v)


def _sum_rows(parts):
    r, c = parts[0].shape
    tr = 8
    for cand in range(min(r, 512), 7, -8):
        if r % cand == 0:
            tr = cand
            break

    def body(*refs):
        acc = refs[0][...]
        for p in refs[1:-1]:
            acc = acc + p[...]
        refs[-1][...] = acc

    spec = pl.BlockSpec((tr, c), lambda i: (i, 0))
    return pl.pallas_call(
        body, name="sum_rows", grid=(r // tr,), in_specs=[spec] * len(parts), out_specs=spec,
        out_shape=jax.ShapeDtypeStruct((r, c), F32),
        compiler_params=_cparams(("arbitrary",)),
    )(*parts)


MESH = pl.DeviceIdType.MESH
ANY = pl.BlockSpec(memory_space=pl.ANY)


def _place():
    return lax.axis_index("x"), lax.axis_index("y"), lax.axis_index("c")


def _allgather8(blk):
    m_per, n = blk.shape

    def body(x_ref, out_ref, send_sems, recv_sems, local_sem):
        x, y, c = _place()
        me, sibling = (x, y, c), (x, y, 1 - c)
        chips = [(1 - x, y), (x, 1 - y), (1 - x, 1 - y)]

        def rows(px, py, pc):
            return out_ref.at[pl.ds((4 * px + 2 * py + pc) * m_per, m_per), :]

        def copy(k, block, to, src=None):
            return pltpu.make_async_remote_copy(
                src_ref=rows(*block) if src is None else src, dst_ref=rows(*block),
                send_sem=send_sems.at[k], recv_sem=recv_sems.at[k], device_id=to, device_id_type=MESH)

        mine = pltpu.make_async_copy(x_ref, rows(*me), local_sem)
        mine.start()
        first = [copy(0, me, sibling, src=x_ref)]
        first += [copy(1 + j, me, (*chip, c), src=x_ref) for j, chip in enumerate(chips)]
        for cp in first:
            cp.start()
        passed = [copy(4 + j, (*chip, c), sibling) for j, chip in enumerate(chips)]
        for j, chip in enumerate(chips):
            copy(1 + j, (*chip, c), me).wait_recv()
            passed[j].start()
        copy(0, sibling, me).wait_recv()
        for j, chip in enumerate(chips):
            copy(4 + j, (*chip, 1 - c), me).wait_recv()
        for cp in first + passed:
            cp.wait_send()
        mine.wait()

    return pl.pallas_call(
        body, name="allgather8",
        out_shape=jax.ShapeDtypeStruct((8 * m_per, n), blk.dtype),
        in_specs=[pl.BlockSpec(memory_space=pltpu.VMEM)],
        out_specs=pl.BlockSpec(memory_space=pltpu.VMEM),
        scratch_shapes=[pltpu.SemaphoreType.DMA((7,)), pltpu.SemaphoreType.DMA((7,)), pltpu.SemaphoreType.DMA],
    )(blk)


def _chip_peers():
    x, y, c = _place()
    return [(1 - x, y, c), (x, 1 - y, c), (1 - x, 1 - y, c)]


GATHER_SEMS = [pltpu.SemaphoreType.DMA((3,)), pltpu.SemaphoreType.DMA((3,)), pltpu.SemaphoreType.DMA]
SCATTER_SEMS = [pltpu.SemaphoreType.DMA((3,)), pltpu.SemaphoreType.DMA((3,))]


def _gather_plan(x_ref, out_ref, send_sems, recv_sems, local_sem):
    x, y, _ = _place()
    peers = _chip_peers()

    def copy(j, chip_index):
        return pltpu.make_async_remote_copy(
            src_ref=x_ref, dst_ref=out_ref.at[chip_index], send_sem=send_sems.at[j], recv_sem=recv_sems.at[j],
            device_id=peers[j], device_id_type=MESH)

    mine = pltpu.make_async_copy(x_ref, out_ref.at[2 * x + y], local_sem)
    sends = [copy(j, 2 * x + y) for j in range(3)]

    def start():
        mine.start()
        for cp in sends:
            cp.start()

    def wait():
        for j in range(3):
            copy(j, 2 * peers[j][0] + peers[j][1]).wait_recv()
        for cp in sends:
            cp.wait_send()
        mine.wait()

    return start, wait


def _scatter_plan(p_ref, out_ref, send_sems, recv_sems):
    peers = _chip_peers()
    sends = [pltpu.make_async_remote_copy(
        src_ref=p_ref.at[2 * peers[j][0] + peers[j][1]], dst_ref=out_ref.at[j], send_sem=send_sems.at[j],
        recv_sem=recv_sems.at[j], device_id=peers[j], device_id_type=MESH) for j in range(3)]

    def start():
        for cp in sends:
            cp.start()

    def wait():
        for cp in sends:
            cp.wait_recv()
        for cp in sends:
            cp.wait_send()

    return start, wait


def _chip_allgather(shard):
    def body(x_ref, out_ref, *sems):
        start, wait = _gather_plan(x_ref, out_ref, *sems)
        start()
        wait()

    return pl.pallas_call(
        body, name="chip_allgather", out_shape=jax.ShapeDtypeStruct((4,) + shard.shape, shard.dtype),
        in_specs=[ANY], out_specs=ANY, scratch_shapes=GATHER_SEMS,
    )(shard)


def _chip_scatter(parts):
    def body(p_ref, out_ref, *sems):
        start, wait = _scatter_plan(p_ref, out_ref, *sems)
        start()
        wait()

    return pl.pallas_call(
        body, name="chip_scatter", out_shape=jax.ShapeDtypeStruct((3,) + parts.shape[1:], parts.dtype),
        in_specs=[ANY], out_specs=ANY, scratch_shapes=SCATTER_SEMS,
    )(parts)


def _sibling_swap(a):
    def body(a_ref, out_ref, send_sem, recv_sem):
        x, y, c = _place()
        cp = pltpu.make_async_remote_copy(src_ref=a_ref, dst_ref=out_ref, send_sem=send_sem, recv_sem=recv_sem,
                                          device_id=(x, y, 1 - c), device_id_type=MESH)
        cp.start()
        cp.wait_recv()
        cp.wait_send()

    return pl.pallas_call(
        body, name="sibling_swap", out_shape=jax.ShapeDtypeStruct(a.shape, a.dtype),
        in_specs=[ANY], out_specs=ANY,
        scratch_shapes=[pltpu.SemaphoreType.DMA, pltpu.SemaphoreType.DMA],
    )(a)


WEIGHTS = ['hgrn_lb', 'ada_w', 'ada_b', 'norm_g', 'hg_in_w', 'hg_out_w', 'hg_onorm', 'sw_in_w', 'sw_out_w', 'sw_qnorm',
           'sw_knorm', 'sw_sinks', 'gd_in_w', 'gd_out_w', 'gd_conv_w', 'gd_a_log', 'gd_dt_bias', 'gd_onorm']
BIG = ['hg_in_w', 'hg_out_w', 'sw_in_w', 'sw_out_w', 'gd_in_w', 'gd_out_w']
SEG_FIRST = [('hg_in_w', 0), ('hg_out_w', 0)]
SEG_REST = [('hg_in_w', 1), ('hg_out_w', 1), ('sw_in_w', 0), ('sw_out_w', 0), ('gd_in_w', 0), ('gd_out_w', 0)]
PACK_ALIGN = 16
ROPE_THETA = 10000.0
ADA_S = 3 * D // 4
SMALL_ROW = {'hg_onorm': (0, 256), 'sw_qnorm': (256, 64), 'sw_knorm': (320, 64), 'sw_sinks': (384, 16),
             'gd_a_log': (400, 16), 'gd_dt_bias': (416, 16), 'gd_onorm': (432, 128)}


def _pack_rows(arrs):
    flat = jnp.concatenate([a.reshape(-1, D) for a in arrs], axis=0)
    return jnp.pad(flat, ((0, -flat.shape[0] % PACK_ALIGN), (0, 0)))


def _unpack_rows(packed, shapes):
    out, off = [], 0
    for s in shapes:
        rows = 1
        for d in s:
            rows *= d
        rows //= D
        out.append(packed[..., off:off + rows, :].reshape(packed.shape[:-2] + tuple(s)))
        off += rows
    return out


def _pack_small(vals):
    row = jnp.concatenate([vals[k].reshape(-1) for k in SMALL_ROW])
    row = jnp.pad(row, (0, D - row.shape[0]))[None]
    return jnp.concatenate([vals['hgrn_lb'], vals['norm_g'], vals['gd_conv_w'].reshape(16, D), row,
                            jnp.zeros((7, D), F32)], axis=0)


def _sw_cols(w, inverse=False):
    def split(a, heads):
        shp = (a.shape[0], 2, heads, 32) if inverse else (a.shape[0], heads, 2, 32)
        return a.reshape(shp).transpose(0, 2, 1, 3).reshape(a.shape[0], heads * 64)
    return jnp.concatenate([split(w[:, 0:1024], 16), split(w[:, 1024:1280], 4), w[:, 1280:]], axis=1)


def kernel(x, c, positions, hgrn_lb, ada_w, ada_b, norm_g, hg_in_w, hg_out_w, hg_onorm, sw_in_w, sw_out_w, sw_qnorm, sw_knorm, sw_sinks, gd_in_w, gd_out_w, gd_conv_w, gd_a_log, gd_dt_bias, gd_onorm, loss_target, m_hgrn_lb, m_ada_w, m_ada_b, m_norm_g, m_hg_in_w, m_hg_out_w, m_hg_onorm, m_sw_in_w, m_sw_out_w, m_sw_qnorm, m_sw_knorm, m_sw_sinks, m_gd_in_w, m_gd_out_w, m_gd_conv_w, m_gd_a_log, m_gd_dt_bias, m_gd_onorm, v_hgrn_lb, v_ada_w, v_ada_b, v_norm_g, v_hg_in_w, v_hg_out_w, v_hg_onorm, v_sw_in_w, v_sw_out_w, v_sw_qnorm, v_sw_knorm, v_sw_sinks, v_gd_in_w, v_gd_out_w, v_gd_conv_w, v_gd_a_log, v_gd_dt_bias, v_gd_onorm):
    w_in = dict(hgrn_lb=hgrn_lb, ada_w=ada_w, ada_b=ada_b, norm_g=norm_g, hg_in_w=hg_in_w, hg_out_w=hg_out_w,
                hg_onorm=hg_onorm, sw_in_w=sw_in_w, sw_out_w=sw_out_w, sw_qnorm=sw_qnorm, sw_knorm=sw_knorm,
                sw_sinks=sw_sinks, gd_in_w=gd_in_w, gd_out_w=gd_out_w, gd_conv_w=gd_conv_w, gd_a_log=gd_a_log,
                gd_dt_bias=gd_dt_bias, gd_onorm=gd_onorm)
    m_in = dict(zip(WEIGHTS, (m_hgrn_lb, m_ada_w, m_ada_b, m_norm_g, m_hg_in_w, m_hg_out_w, m_hg_onorm, m_sw_in_w,
                              m_sw_out_w, m_sw_qnorm, m_sw_knorm, m_sw_sinks, m_gd_in_w, m_gd_out_w, m_gd_conv_w,
                              m_gd_a_log, m_gd_dt_bias, m_gd_onorm)))
    v_in = dict(zip(WEIGHTS, (v_hgrn_lb, v_ada_w, v_ada_b, v_norm_g, v_hg_in_w, v_hg_out_w, v_hg_onorm, v_sw_in_w,
                              v_sw_out_w, v_sw_qnorm, v_sw_knorm, v_sw_sinks, v_gd_in_w, v_gd_out_w, v_gd_conv_w,
                              v_gd_a_log, v_gd_dt_bias, v_gd_onorm)))
    ax, ay, ac = _place()
    chip = 2 * ax + ay
    bidx = 4 * ax + 2 * ay + ac
    t = x.shape[1]
    x0, target = x[0], loss_target[0]

    c_all = _allgather8(jnp.pad(c, ((0, 7), (0, 0)))).reshape(8, 8, D)[:, 0, :]
    ada_b_cols = lax.dynamic_slice(ada_b, (0, chip * ADA_S), (4, ADA_S)).reshape(4, 1, ADA_S)
    mod_sh = _ada_fwd(c_all, ada_w, ada_b_cols)
    mod_g = _allgather8(mod_sh.reshape(32, ADA_S)).reshape(4, 2, 4, 8, ADA_S)[:, 0]
    mod = lax.dynamic_index_in_dim(mod_g, bidx, axis=2, keepdims=False).transpose(1, 0, 2).reshape(4, 3 * D)
    shift = [mod[l:l + 1, 0:D] for l in range(4)]
    scale = [mod[l:l + 1, D:2 * D] for l in range(4)]
    gate = [mod[l:l + 1, 2 * D:3 * D] for l in range(4)]

    h8 = jnp.concatenate([hgrn_lb, jnp.full((4, D), -1e30, F32)], axis=0)
    lb_all = _lb_fwd(h8)
    freq = ROPE_THETA ** (-jnp.arange(0, 64, 2, dtype=F32) / 64)
    cs = _rope_table(positions.reshape(t, 1), jnp.tile(freq, 4)[None])

    seg_shapes = lambda seg: [w_in[k].shape[1:] for k, _ in seg]
    pack_seg = lambda src, seg: _pack_rows([src[k][i] for k, i in seg])
    cols_full = lambda a: a.transpose(1, 0, 2).reshape(a.shape[1], 4 * a.shape[2])
    hg_in0_k, hg_out0_k = _unpack_rows(_chip_allgather(pack_seg(w_in, SEG_FIRST).astype(BF16)), seg_shapes(SEG_FIRST))
    win, wout = [cols_full(hg_in0_k)], [hg_out0_k.reshape(D, D)]
    rest_shard = pack_seg(w_in, SEG_REST).astype(BF16)
    tn_in = [1024, 1280, 896, 1024]
    tk_bwd = [4096, 2560, 896, 4096]

    gq = jnp.stack([jnp.tile(sw_qnorm[0, :32], 16), jnp.tile(sw_qnorm[0, 32:], 16)])
    gk = jnp.stack([jnp.tile(sw_knorm[0, :32], 4), jnp.tile(sw_knorm[0, 32:], 4)])
    pad128 = lambda a: jnp.pad(a, ((0, 0), (0, HD - a.shape[1])))
    sinks, alog, dtb = pad128(sw_sinks), pad128(gd_a_log), pad128(gd_dt_bias)
    cw8 = jnp.pad(_chip_allgather(gd_conv_w[0]).transpose(1, 0, 2).reshape(4, GD_QKV), ((0, 4), (0, 0)))
    lbs = {0: lb_all[0:1], 3: lb_all[3:4]}

    xs, us, hs, ps, stss = [x0], [], [], [], []
    for l in range(4):
        u, h = _ln_mm(xs[l], norm_g[l:l + 1], scale[l], shift[l], win[l], tn_in[l])
        if l == 0:
            p, sts, rest_k = _hg_fwd(u, lbs[l], hg_onorm[0:1], gather=rest_shard)
            hg_in1_k, hg_out1_k, sw_in_k, sw_out_k, gd_in_k, gd_out_k = _unpack_rows(rest_k, seg_shapes(SEG_REST))
            win += [_sw_cols(cols_full(sw_in_k)), jnp.pad(cols_full(gd_in_k), ((0, 0), (0, GD_N - 6176))),
                    cols_full(hg_in1_k)]
            wout += [sw_out_k.reshape(D, D), gd_out_k.reshape(GD_VW, D), hg_out1_k.reshape(D, D)]
        elif l % 3 == 0:
            p, sts = _hg_fwd(u, lbs[l], hg_onorm[l // 3:l // 3 + 1])
        elif l % 3 == 1:
            p, sts = _sw_fwd(u, cs, gq, gk, sinks), None
        else:
            p, *sts = _gd_fwd(u, cw8, alog, dtb, gd_onorm)
        xs.append(_mm_res(p, wout[l], xs[l], gate[l]))
        us.append(u), hs.append(h), ps.append(p), stss.append(sts)
    lpart, dx = _loss_grad(xs[4], target)
    loss = lax.psum(lpart[0, 0], ("x", "y", "c"))

    by_chip = lambda g, cols: g.reshape(g.shape[0], 4, cols).transpose(1, 0, 2)
    g_small = {}
    d_in, d_out, dmod, dnorm_g, dlb8, dgo_hg = [None] * 4, [None] * 4, [None] * 4, [None] * 4, jnp.zeros((8, D), F32), {}
    for l in (3, 2, 1, 0):
        dp = _mm_scaled(dx, gate[l], wout[l].T, 1024)
        d_out[l], dgate = _outgrad(_mm_tn_acc(ps[l], dx, 512), wout[l], gate[l])
        if l == 0:
            rest_parts = {('hg_in_w', 1): by_chip(d_in[3], D), ('hg_out_w', 1): d_out[3].reshape(4, D // 4, D),
                          ('sw_in_w', 0): by_chip(_sw_cols(d_in[1], inverse=True), SW_N // 4),
                          ('sw_out_w', 0): d_out[1].reshape(4, D // 4, D),
                          ('gd_in_w', 0): by_chip(d_in[2][:, :6176], 1544),
                          ('gd_out_w', 0): d_out[2].reshape(4, GD_VW // 4, D)}
            rest_packed = jnp.stack([_pack_rows([rest_parts[s][j] for s in SEG_REST]) for j in range(4)])
            du, dlb, dgo_hg[0], rest_recv = _hg_bwd(us[l], stss[l], dp, lbs[l], hg_onorm[0:1],
                                                    scatter=rest_packed.astype(BF16))
            dlb8 = lax.dynamic_update_slice(dlb8, dlb, (l, 0))
        elif l % 3 == 0:
            du, dlb, dgo_hg[l // 3] = _hg_bwd(us[l], stss[l], dp, lbs[l], hg_onorm[l // 3:l // 3 + 1])
            dlb8 = lax.dynamic_update_slice(dlb8, dlb, (l, 0))
        elif l % 3 == 1:
            du, dgq, dgk, dsk = _sw_bwd(us[l], cs, dp, gq, gk, sinks)
            g_small['sw_qnorm'] = jnp.concatenate([dgq[0].reshape(16, 32).sum(0), dgq[1].reshape(16, 32).sum(0)])
            g_small['sw_knorm'] = jnp.concatenate([dgk[0].reshape(4, 32).sum(0), dgk[1].reshape(4, 32).sum(0)])
            g_small['sw_sinks'] = dsk[0, :16]
        else:
            du, dcw, dalog, ddtb, g_small['gd_onorm'] = _gd_bwd(us[l], *stss[l], dp, cw8, alog, dtb, gd_onorm)
            g_small['gd_conv_w'], g_small['gd_a_log'], g_small['gd_dt_bias'] = dcw[:4], dalog[0, :16], ddtb[0, :16]
        d_in[l] = _mm_tn_acc(hs[l], du, 896 if l == 2 else 512)
        dx, dvec = _inproj_bwd(du, win[l].T, xs[l], dx, norm_g[l:l + 1], scale[l], shift[l], tk_bwd[l])
        dnorm_g[l] = dvec[0:1]
        dmod[l] = jnp.concatenate([dvec[2:3], dvec[1:2], dgate[0:1]], axis=1)
    grad_x = dx[None]

    g_small['hgrn_lb'] = _lb_bwd(h8, dlb8)[0:4]
    g_small['norm_g'] = jnp.concatenate(dnorm_g, axis=0)
    g_small['hg_onorm'] = jnp.concatenate([dgo_hg[0], dgo_hg[1]], axis=0)
    gs_all = _allgather8(_pack_small(g_small))
    gs = _sum_rows([gs_all[32 * d:32 * (d + 1)] for d in range(8)])

    def small_view(packed, k):
        if k == 'hgrn_lb':
            return packed[0:4]
        if k == 'norm_g':
            return packed[4:8]
        off, size = SMALL_ROW[k]
        return packed[24, off:off + size].reshape(w_in[k].shape)

    conv_sl = lambda full: lax.dynamic_slice(full.reshape(4, GD_QKV), (0, chip * D), (4, D))
    out = {}

    def put(k, res, shape):
        for name, r in zip(('grad_', 'delta_', 'new_m_', 'new_v_'), res):
            out[name + k] = r.reshape(shape)

    zero_conv = dict(gd_conv_w=jnp.zeros((4, GD_QKV), F32))
    small_names = ['hgrn_lb', 'norm_g'] + list(SMALL_ROW)
    res = _adamw(_pack_small({**{k: w_in[k] for k in small_names}, **zero_conv}), (gs,),
                 _pack_small({**{k: m_in[k] for k in small_names}, **zero_conv}),
                 _pack_small({**{k: v_in[k] for k in small_names}, **zero_conv}))
    for k in small_names:
        put(k, [small_view(r, k) for r in res], w_in[k].shape)
    put('gd_conv_w', _adamw(gd_conv_w[0], (conv_sl(gs[8:24]),), m_in['gd_conv_w'][0], v_in['gd_conv_w'][0]),
        gd_conv_w.shape)

    dm = _allgather8(jnp.pad(jnp.concatenate(dmod, axis=0), ((0, 4), (0, 0)))).reshape(8, 8, 3 * D)[:, :4]
    dm = dm.transpose(1, 0, 2)
    g_ada_w, g_ada_b = _ada_bwd(c_all, lax.dynamic_slice(dm, (0, 0, chip * ADA_S), (4, 8, ADA_S)), dm)
    put('ada_w', _adamw(ada_w.reshape(4 * D, ADA_S), (g_ada_w.reshape(4 * D, ADA_S),),
                        m_in['ada_w'].reshape(4 * D, ADA_S), v_in['ada_w'].reshape(4 * D, ADA_S)), ada_w.shape)
    put('ada_b', _adamw(ada_b, (g_ada_b.reshape(4, 3 * D),), m_in['ada_b'], v_in['ada_b']), ada_b.shape)

    first_parts = {('hg_in_w', 0): by_chip(d_in[0], D), ('hg_out_w', 0): d_out[0].reshape(4, D // 4, D)}
    first_packed = jnp.stack([_pack_rows([first_parts[s][j] for s in SEG_FIRST]) for j in range(4)])
    first_recv = _chip_scatter(first_packed.astype(BF16))
    own = lambda packed: lax.dynamic_index_in_dim(packed, chip, axis=0, keepdims=False)
    half = jnp.concatenate([_sum_rows([own(first_packed), first_recv[0], first_recv[1], first_recv[2]]),
                            _sum_rows([own(rest_packed), rest_recv[0], rest_recv[1], rest_recv[2]])], axis=0)
    other = _sibling_swap(half)
    pack_all = lambda src: jnp.concatenate([pack_seg(src, SEG_FIRST), pack_seg(src, SEG_REST)], axis=0)
    res = _adamw(pack_all(w_in), (half, other), pack_all(m_in), pack_all(v_in))
    n_first = first_packed.shape[1]
    for name, r in zip(('grad_', 'delta_', 'new_m_', 'new_v_'), res):
        pieces = dict(zip(SEG_FIRST, _unpack_rows(r[:n_first], seg_shapes(SEG_FIRST))))
        pieces.update(zip(SEG_REST, _unpack_rows(r[n_first:], seg_shapes(SEG_REST))))
        for k in BIG:
            out[name + k] = jnp.stack([pieces[(k, i)] for i in range(w_in[k].shape[0])])

    return (loss, grad_x, *[out[p + k] for p in ('grad_', 'delta_', 'new_m_', 'new_v_') for k in WEIGHTS])
```

```python
import functools

import jax
import jax.numpy as jnp
from jax import lax
from jax.experimental import pallas as pl
from jax.experimental.pallas import tpu as pltpu

F32 = jnp.float32
BF16 = jnp.bfloat16
D = 1024
EPS = 1e-6
CHUNK = 64
SUB = 16
HG_H = 8
HD = 128
VMEM_LIMIT = 56 * 1024 * 1024


def _cparams(sem=None):
    return pltpu.CompilerParams(dimension_semantics=sem, vmem_limit_bytes=VMEM_LIMIT)


def _dot(a, b, ca, cb, prec=None):
    return lax.dot_general(a, b, (((ca,), (cb,)), ((), ())), precision=prec, preferred_element_type=F32)


def _mm(a, b):
    return _dot(a.astype(BF16), b.astype(BF16), 1, 0)


def _mm_nt(a, b):
    return _dot(a.astype(BF16), b.astype(BF16), 1, 1)


def _mm_tn(a, b):
    return _dot(a.astype(BF16), b.astype(BF16), 0, 0)


def _mm_f32(a, b):
    return _dot(a, b, 1, 0, lax.Precision.HIGHEST)


def _silu(x):
    return x * jax.nn.sigmoid(x)


def _cumsum_impl(x):
    row = lax.broadcasted_iota(jnp.int32, x.shape, 0)
    s = 1
    while s < x.shape[0]:
        x = x + jnp.where(row >= s, pltpu.roll(x, s, 0), 0.0)
        s *= 2
    return x


@jax.custom_vjp
def _cumsum_rows(x):
    return _cumsum_impl(x)


_cumsum_rows.defvjp(lambda x: (_cumsum_impl(x), None),
                    lambda _, g: (jnp.sum(g, axis=0, keepdims=True) - _cumsum_impl(g) + g,))


def _roll_rows(x, shift):
    n = x.shape[0]

    @jax.custom_vjp
    def f(a):
        return pltpu.roll(a, shift, 0)

    f.defvjp(lambda a: (pltpu.roll(a, shift, 0), None), lambda _, g: (pltpu.roll(g, n - shift, 0),))
    return f(x)


def _hg_chunk(q_raw, f_pre, v, z, st, lb, go):
    c = q_raw.shape[0]
    nsub = c // SUB
    lf = jnp.log(lb + (1.0 - lb) * jax.nn.sigmoid(f_pre))
    k = (1.0 - lb) * jax.nn.sigmoid(-f_pre)
    q = _silu(q_raw)
    b = _cumsum_rows(lf)
    rowf = lax.broadcasted_iota(jnp.int32, lf.shape, 0)
    bmid = [jnp.sum(jnp.where(rowf == SUB * i + SUB // 2, b, 0.0), axis=0, keepdims=True) for i in range(nsub)]
    row = lax.broadcasted_iota(jnp.int32, (c, 1), 0)
    ref = sum(jnp.where((row >= SUB * i) & (row < SUB * (i + 1)), bmid[i], 0.0) for i in range(nsub))
    qt = q * jnp.exp(b - ref)
    kall = jnp.concatenate(
        [k * jnp.exp(jnp.where(row < SUB * (i + 1), bmid[i] - b, -jnp.inf)) for i in range(nsub)], axis=0)
    v4 = jnp.concatenate([v] * nsub, axis=0)
    b_last = jnp.sum(lf, axis=0, keepdims=True)
    qb = q * jnp.exp(b)
    kd = k * jnp.exp(b_last - b)
    e_last = jnp.exp(b_last)
    tq = lax.broadcasted_iota(jnp.int32, (c, nsub * c), 0)
    cq = lax.broadcasted_iota(jnp.int32, (c, nsub * c), 1)
    m_all = ((cq // c) == (tq // SUB)) & ((cq % c) <= tq)
    hs = lambda a: jnp.split(a, HG_H, axis=1)
    qt_h, kall_h, v4_h, qb_h, kd_h, v_h, z_h, el_h = map(hs, (qt, kall, v4, qb, kd, v, z, e_last))
    st_h = jnp.split(st, HG_H, axis=0)
    p_out, st_out = [], []
    for h in range(HG_H):
        pm = jnp.where(m_all, _mm_nt(qt_h[h], kall_h[h]), 0.0)
        o = _mm(pm, v4_h[h]) + _mm_nt(qb_h[h], st_h[h])
        st_out.append(el_h[h] * st_h[h] + _mm_tn(v_h[h], kd_h[h]))
        y = o * lax.rsqrt(jnp.mean(o * o, axis=1, keepdims=True) + EPS) * go
        p_out.append(y * _silu(z_h[h]))
    return jnp.concatenate(p_out, axis=1), jnp.concatenate(st_out, axis=0)


def _hg_fwd(u, lb, go, gather=None):
    t = u.shape[0]
    n = t // CHUNK

    def body(u_ref, lb_ref, go_ref, *rest):
        if gather is None:
            p_ref, sts_ref, st_ref = rest
        else:
            shard_ref, p_ref, sts_ref, all_ref, st_ref, *sems = rest
            start, wait = _gather_plan(shard_ref, all_ref, *sems)
            pl.when(pl.program_id(0) == 0)(start)

        @pl.when(pl.program_id(0) == 0)
        def _():
            st_ref[...] = jnp.zeros_like(st_ref)

        st = st_ref[...]
        sts_ref[0] = st
        p, st_next = _hg_chunk(u_ref[:, 0:D], u_ref[:, D:2 * D], u_ref[:, 2 * D:3 * D], u_ref[:, 3 * D:4 * D],
                               st, lb_ref[...], go_ref[...])
        p_ref[...] = p.astype(BF16)
        st_ref[...] = st_next
        if gather is not None:
            pl.when(pl.program_id(0) == n - 1)(wait)

    more = gather is not None
    return pl.pallas_call(
        body, name="hg_fwd_gather" if more else "hg_fwd", grid=(n,),
        in_specs=[pl.BlockSpec((CHUNK, 4 * D), lambda i: (i, 0)),
                  pl.BlockSpec((1, D), lambda i: (0, 0)),
                  pl.BlockSpec((1, HD), lambda i: (0, 0))] + [ANY] * more,
        out_specs=[pl.BlockSpec((CHUNK, D), lambda i: (i, 0)),
                   pl.BlockSpec((1, HG_H * HD, HD), lambda i: (i, 0, 0))] + [ANY] * more,
        out_shape=[jax.ShapeDtypeStruct((t, D), BF16), jax.ShapeDtypeStruct((n, HG_H * HD, HD), F32)]
        + ([jax.ShapeDtypeStruct((4,) + gather.shape, gather.dtype)] if more else []),
        scratch_shapes=[pltpu.VMEM((HG_H * HD, HD), F32)] + GATHER_SEMS * more,
        compiler_params=_cparams(("arbitrary",)),
    )(u, lb, go, *([gather] * more))


def _hg_bwd(u, sts, dp, lb, go, scatter=None):
    t = u.shape[0]
    n = t // CHUNK

    def body(u_ref, sts_ref, dp_ref, lb_ref, go_ref, *rest):
        if scatter is None:
            du_ref, dlb_ref, dgo_ref, dst_ref = rest
        else:
            parts_ref, du_ref, dlb_ref, dgo_ref, recv_ref, dst_ref, *sems = rest
            start, wait = _scatter_plan(parts_ref, recv_ref, *sems)
            pl.when(pl.program_id(0) == 0)(start)

        @pl.when(pl.program_id(0) == 0)
        def _():
            dst_ref[...] = jnp.zeros_like(dst_ref)
            dlb_ref[...] = jnp.zeros_like(dlb_ref)
            dgo_ref[...] = jnp.zeros_like(dgo_ref)

        _, vjp = jax.vjp(_hg_chunk, u_ref[:, 0:D], u_ref[:, D:2 * D], u_ref[:, 2 * D:3 * D], u_ref[:, 3 * D:4 * D],
                         sts_ref[0], lb_ref[...], go_ref[...])
        dq, df, dv, dz, dst, dlb, dgo = vjp((dp_ref[...].astype(F32), dst_ref[...]))
        du_ref[:, 0:D] = dq.astype(BF16)
        du_ref[:, D:2 * D] = df.astype(BF16)
        du_ref[:, 2 * D:3 * D] = dv.astype(BF16)
        du_ref[:, 3 * D:4 * D] = dz.astype(BF16)
        dst_ref[...] = dst
        dlb_ref[...] += dlb
        dgo_ref[...] += dgo
        if scatter is not None:
            pl.when(pl.program_id(0) == n - 1)(wait)

    rev = lambda i: (n - 1 - i, 0)
    more = scatter is not None
    return pl.pallas_call(
        body, name="hg_bwd_scatter" if more else "hg_bwd", grid=(n,),
        in_specs=[pl.BlockSpec((CHUNK, 4 * D), rev),
                  pl.BlockSpec((1, HG_H * HD, HD), lambda i: (n - 1 - i, 0, 0)),
                  pl.BlockSpec((CHUNK, D), rev),
                  pl.BlockSpec((1, D), lambda i: (0, 0)),
                  pl.BlockSpec((1, HD), lambda i: (0, 0))] + [ANY] * more,
        out_specs=[pl.BlockSpec((CHUNK, 4 * D), rev),
                   pl.BlockSpec((1, D), lambda i: (0, 0)),
                   pl.BlockSpec((1, HD), lambda i: (0, 0))] + [ANY] * more,
        out_shape=[jax.ShapeDtypeStruct((t, 4 * D), BF16), jax.ShapeDtypeStruct((1, D), F32),
                   jax.ShapeDtypeStruct((1, HD), F32)]
        + ([jax.ShapeDtypeStruct((3,) + scatter.shape[1:], scatter.dtype)] if more else []),
        scratch_shapes=[pltpu.VMEM((HG_H * HD, HD), F32)] + SCATTER_SEMS * more,
        compiler_params=_cparams(("arbitrary",)),
    )(u, sts, dp, lb, go, *([scatter] * more))


GD_VH = 16
GD_QKH = 8
GD_QKV = 4096
GD_VW = 2048
GD_N = GD_QKV + GD_VW + HD
GD_GRP = 4
GD_SOLVE = (GD_VH // GD_GRP, GD_GRP * CHUNK, 2 * HD)
HALO = 8


def _mm_high(a, b):
    return _dot(a, b, 1, 0, lax.Precision.HIGH)


def _lane_pick(a, h):
    lane = lax.broadcasted_iota(jnp.int32, a.shape, 1)
    return jnp.sum(jnp.where(lane == h, a, 0.0), axis=1, keepdims=True)


def _l2n(x):
    return x * lax.rsqrt(jnp.sum(x * x, axis=1, keepdims=True) + EPS)


def _solve_fwd(a_mat, rhs):
    n = a_mat.shape[0]
    r_i, c_i = lax.broadcasted_iota(jnp.int32, (n, n), 0), lax.broadcasted_iota(jnp.int32, (n, n), 1)
    same = lambda nb: (r_i // nb) == (c_i // nb)
    d0 = jnp.where(same(8), a_mat, 0.0)
    d2 = _mm(d0, d0)
    tinv = (r_i == c_i).astype(F32) - d0
    tinv = tinv + _mm(tinv, d2)
    tinv = tinv + _mm(tinv, _mm(d2, d2))
    nb = 16
    while nb <= CHUNK:
        low = jnp.where(same(nb) & ~same(nb // 2), a_mat, 0.0)
        tinv = tinv - _mm(_mm(tinv, low), tinv)
        nb *= 2
    x = _mm(tinv, rhs)
    return x, (tinv, x)


def _solve_bwd(res, dx):
    tinv, x = res
    drhs = _dot(tinv, dx, 0, 0, lax.Precision.HIGH)
    return -_dot(drhs, x, 1, 1, lax.Precision.HIGH), drhs


@jax.custom_vjp
def _solved(a_mat, rhs, tinv, x):
    return x


_solved.defvjp(lambda a_mat, rhs, tinv, x: (x, (tinv, x)),
               lambda res, dx: _solve_bwd(res, dx) + (jnp.zeros_like(res[0]), jnp.zeros_like(res[1])))


def _gd_chunk(xh, x, z, ab, st, cw, alog, dtb, go, solve):
    c = x.shape[0]
    xa = jnp.concatenate([xh, x], axis=0)
    sh = [jnp.split(_roll_rows(xa, 3 - j), [HALO], axis=0)[1] for j in range(3)]
    qkv = _silu(cw[0:1] * sh[0] + cw[1:2] * sh[1] + cw[2:3] * sh[2] + cw[3:4] * x)
    q_all, k_all, v_all = jnp.split(qkv, [1024, 2048], axis=1)
    lane = lax.broadcasted_iota(jnp.int32, (c, HD), 1)
    a_part = jnp.where(lane < GD_VH, ab, 0.0)
    g_all = -jnp.exp(alog) * jax.nn.softplus(a_part + dtb)
    d_all = _cumsum_rows(g_all)
    dl_all = jnp.sum(g_all, axis=0, keepdims=True)
    beta_all = jax.nn.sigmoid(ab)
    gc = GD_GRP * c
    r_i, c_i = lax.broadcasted_iota(jnp.int32, (gc, gc), 0), lax.broadcasted_iota(jnp.int32, (gc, gc), 1)
    same_head = (r_i // c) == (c_i // c)
    tri_g, strict_g = same_head & (c_i <= r_i), same_head & (c_i < r_i)
    qs =jnp.split(q_all, GD_QKH, axis=1)
    ks = jnp.split(k_all, GD_QKH, axis=1)
    vs = jnp.split(v_all, GD_VH, axis=1)
    zs = jnp.split(z, GD_VH, axis=1)
    sts = jnp.split(st, GD_VH, axis=0)
    qn = [_l2n(a) * (HD ** -0.5) for a in qs]
    kn = [_l2n(a) for a in ks]
    p_out, st_out = [], []
    for g in range(GD_VH // GD_GRP):
        heads = range(GD_GRP * g, GD_GRP * (g + 1))
        stack = lambda f: jnp.concatenate([f(h) for h in heads], axis=0)
        q_, k_, v_ = stack(lambda h: qn[h // 2]), stack(lambda h: kn[h // 2]), stack(lambda h: vs[h])
        dcol = stack(lambda h: _lane_pick(d_all, h))
        bcol = stack(lambda h: _lane_pick(beta_all, GD_VH + h))
        dlast = stack(lambda h: jnp.broadcast_to(_lane_pick(dl_all, h), (c, 1)))
        drow = jnp.sum(jnp.broadcast_to(dcol, (gc, HD)).T, axis=0, keepdims=True) * (1.0 / HD)
        dec = jnp.exp(jnp.where(tri_g, dcol - drow, -jnp.inf))
        kb = k_ * bcol
        a_mat = jnp.where(strict_g, _mm_nt(kb, k_) * dec, 0.0)
        xsol = solve(g, a_mat, jnp.concatenate([v_ * bcol, kb * jnp.exp(dcol)], axis=1))
        u_, w_ = jnp.split(xsol, 2, axis=1)
        w_h = jnp.split(w_, GD_GRP, axis=0)
        v_new = u_ - jnp.concatenate([_mm(w_h[i], sts[h]) for i, h in enumerate(heads)], axis=0)
        qd_h = jnp.split(q_ * jnp.exp(dcol), GD_GRP, axis=0)
        o_g = _mm(_mm_nt(q_, k_) * dec, v_new) + jnp.concatenate(
            [_mm(qd_h[i], sts[h]) for i, h in enumerate(heads)], axis=0)
        kd_h = jnp.split(k_ * jnp.exp(dlast - dcol), GD_GRP, axis=0)
        vn_h = jnp.split(v_new, GD_GRP, axis=0)
        o_h = jnp.split(o_g, GD_GRP, axis=0)
        for i, h in enumerate(heads):
            st_out.append(sts[h] * jnp.exp(_lane_pick(dl_all, h)) + _mm_tn(kd_h[i], vn_h[i]))
            o = o_h[i]
            y = o * lax.rsqrt(jnp.mean(o * o, axis=1, keepdims=True) + EPS) * go
            p_out.append(y * _silu(zs[h]))
    return jnp.concatenate(p_out, axis=1), jnp.concatenate(st_out, axis=0)


def _gd_specs(n, rev):
    ci = (lambda i: n - 1 - i) if rev else (lambda i: i)
    return [pl.BlockSpec((HALO, GD_QKV), lambda i: (jnp.maximum(ci(i) * (CHUNK // HALO) - 1, 0), 0)),
            pl.BlockSpec((CHUNK, GD_N), lambda i: (ci(i), 0))]


def _gd_load(uh_ref, u_ref, first):
    xh = jnp.where(first, 0.0, uh_ref[...])
    return xh, u_ref[:, 0:GD_QKV], u_ref[:, GD_QKV:GD_QKV + GD_VW], u_ref[:, GD_QKV + GD_VW:GD_N]


def _gd_fwd(u, cw, alog, dtb, go):
    t = u.shape[0]
    n = t // CHUNK
    small = lambda r, w: pl.BlockSpec((r, w), lambda i: (0, 0))

    def body(uh_ref, u_ref, cw_ref, alog_ref, dtb_ref, go_ref, p_ref, sts_ref, tinv_ref, xsol_ref, st_ref):
        i = pl.program_id(0)

        @pl.when(i == 0)
        def _():
            st_ref[...] = jnp.zeros_like(st_ref)

        def solve(g, a_mat, rhs):
            xsol, (tinv, _) = _solve_fwd(a_mat, rhs)
            tinv_ref[0, g] = tinv
            xsol_ref[0, g] = xsol
            return xsol

        st = st_ref[...]
        sts_ref[0] = st
        p, st_next = _gd_chunk(*_gd_load(uh_ref, u_ref, i == 0), st, cw_ref[...], alog_ref[...], dtb_ref[...],
                               go_ref[...], solve)
        p_ref[...] = p.astype(BF16)
        st_ref[...] = st_next

    return pl.pallas_call(
        body, name="gd_fwd", grid=(n,),
        in_specs=_gd_specs(n, False) + [small(8, GD_QKV), small(1, HD), small(1, HD), small(1, HD)],
        out_specs=[pl.BlockSpec((CHUNK, GD_VW), lambda i: (i, 0)),
                   pl.BlockSpec((1, GD_VH * HD, HD), lambda i: (i, 0, 0)),
                   pl.BlockSpec((1,) + GD_SOLVE, lambda i: (i, 0, 0, 0)),
                   pl.BlockSpec((1,) + GD_SOLVE, lambda i: (i, 0, 0, 0))],
        out_shape=[jax.ShapeDtypeStruct((t, GD_VW), BF16), jax.ShapeDtypeStruct((n, GD_VH * HD, HD), F32),
                   jax.ShapeDtypeStruct((n,) + GD_SOLVE, F32), jax.ShapeDtypeStruct((n,) + GD_SOLVE, F32)],
        scratch_shapes=[pltpu.VMEM((GD_VH * HD, HD), F32)],
        compiler_params=_cparams(("arbitrary",)),
    )(u, u, cw, alog, dtb, go)


def _gd_bwd(u, sts, tinvs, xsols, dp, cw, alog, dtb, go):
    t = u.shape[0]
    n = t // CHUNK
    small = lambda r, w: pl.BlockSpec((r, w), lambda i: (0, 0))

    def body(uh_ref, u_ref, sts_ref, tinv_ref, xsol_ref, dp_ref, cw_ref, alog_ref, dtb_ref, go_ref,
             du_ref, dcw_ref, dalog_ref, ddtb_ref, dgo_ref, dst_ref, dhalo_ref):
        i = pl.program_id(0)

        @pl.when(i == 0)
        def _():
            for r in (dst_ref, dhalo_ref, dcw_ref, dalog_ref, ddtb_ref, dgo_ref):
                r[...] = jnp.zeros_like(r)

        solve = lambda g, a_mat, rhs: _solved(a_mat, rhs, tinv_ref[0, g], xsol_ref[0, g])
        chunk = functools.partial(_gd_chunk, solve=solve)
        _, vjp = jax.vjp(chunk, *_gd_load(uh_ref, u_ref, i == n - 1), sts_ref[0], cw_ref[...], alog_ref[...],
                         dtb_ref[...], go_ref[...])
        dxh, dx, dz, dab, dst, dcw, dalog, ddtb, dgo = vjp((dp_ref[...].astype(F32), dst_ref[...]))
        tail = jnp.concatenate([jnp.zeros((CHUNK - HALO, GD_QKV), F32), dhalo_ref[...]], axis=0)
        du_ref[:, 0:GD_QKV] = (dx + tail).astype(BF16)
        du_ref[:, GD_QKV:GD_QKV + GD_VW] = dz.astype(BF16)
        du_ref[:, GD_QKV + GD_VW:GD_N] = dab.astype(BF16)
        dhalo_ref[...] = dxh
        dst_ref[...] = dst
        dcw_ref[...] += dcw
        dalog_ref[...] += dalog
        ddtb_ref[...] += ddtb
        dgo_ref[...] += dgo

    return pl.pallas_call(
        body, name="gd_bwd", grid=(n,),
        in_specs=_gd_specs(n, True) + [pl.BlockSpec((1, GD_VH * HD, HD), lambda i: (n - 1 - i, 0, 0)),
                                       pl.BlockSpec((1,) + GD_SOLVE, lambda i: (n - 1 - i, 0, 0, 0)),
                                       pl.BlockSpec((1,) + GD_SOLVE, lambda i: (n - 1 - i, 0, 0, 0)),
                                       pl.BlockSpec((CHUNK, GD_VW), lambda i: (n - 1 - i, 0)),
                                       small(8, GD_QKV), small(1, HD), small(1, HD), small(1, HD)],
        out_specs=[pl.BlockSpec((CHUNK, GD_N), lambda i: (n - 1 - i, 0)),
                   small(8, GD_QKV), small(1, HD), small(1, HD), small(1, HD)],
        out_shape=[jax.ShapeDtypeStruct((t, GD_N), BF16), jax.ShapeDtypeStruct((8, GD_QKV), F32)]
        + [jax.ShapeDtypeStruct((1, HD), F32)] * 3,
        scratch_shapes=[pltpu.VMEM((GD_VH * HD, HD), F32), pltpu.VMEM((HALO, GD_QKV), F32)],
        compiler_params=_cparams(("arbitrary",)),
    )(u, u, sts, tinvs, xsols, dp, cw, alog, dtb, go)


SW_B = 128
SW_H = 16
SW_G = 4
SW_N = 2560
SW_KV0 = 1024


def _blockdiag(n, blk):
    r = lax.broadcasted_iota(jnp.int32, (n, n), 0) // blk
    c = lax.broadcasted_iota(jnp.int32, (n, n), 1) // blk
    return (r == c).astype(F32)


def _sw_normrope(x, g1, g2, cos, sin):
    w = x.shape[1] // 2
    x1, x2 = jnp.split(x, 2, axis=1)
    ms = _mm_high(x1 * x1 + x2 * x2, _blockdiag(w, 32)) * (1.0 / 64.0)
    rinv = lax.rsqrt(ms + EPS)
    n1, n2 = x1 * rinv * g1, x2 * rinv * g2
    return jnp.concatenate([n1 * cos - n2 * sin, n2 * cos + n1 * sin], axis=1)


def _sw_block(q, kvp, kvc, z, csp, csc, gq, gk, sinks, has_prev):
    b = q.shape[0]
    cos_c, sin_c = jnp.split(csc, 2, axis=1)
    cos_p, sin_p = jnp.split(csp, 2, axis=1)
    tile4 = lambda a: jnp.concatenate([a] * 4, axis=1)
    qh = _sw_normrope(q, gq[0:1], gq[1:2], tile4(cos_c), tile4(sin_c))
    kp, vp = jnp.split(kvp, 2, axis=1)
    kc, vc = jnp.split(kvc, 2, axis=1)
    kh = jnp.concatenate([_sw_normrope(kp, gk[0:1], gk[1:2], cos_p, sin_p),
                          _sw_normrope(kc, gk[0:1], gk[1:2], cos_c, sin_c)], axis=0)
    vv = jnp.concatenate([vp, vc], axis=0)
    q1, q2 = jnp.split(qh, 2, axis=1)
    q1g, q2g = jnp.split(q1, SW_G, axis=1), jnp.split(q2, SW_G, axis=1)
    own = lax.broadcasted_iota(jnp.int32, (4 * b, b), 1) <= lax.broadcasted_iota(jnp.int32, (4 * b, b), 0) % b
    ri = lax.broadcasted_iota(jnp.int32, (256, 256), 0)
    ci = lax.broadcasted_iota(jnp.int32, (256, 256), 1)
    row_head = lax.broadcasted_iota(jnp.int32, (4 * b, 256), 0) // b
    lane_q = lax.broadcasted_iota(jnp.int32, (4 * b, 256), 1)
    q_sel = (lane_q % 128) // 32 == row_head
    o_sel = lane_q // 64 == row_head
    o_out = []
    for g in range(SW_G):
        ek = ((ri // 128 == ci // 128) & ((ri % 128) // 32 == g) & (ri % 32 == ci % 32)).astype(F32)
        ev = ((ri // 64 == g) & (ri % 64 == ci % 64)).astype(F32)
        kx = _mm(kh, ek)
        vx = _mm(vv, ev)
        qg = jnp.concatenate([q1g[g], q2g[g]], axis=1)
        q4 = jnp.where(q_sel, jnp.concatenate([qg] * 4, axis=0), 0.0)
        sink = jnp.concatenate([jnp.broadcast_to(_lane_pick(sinks, 4 * g + j), (b, 1)) for j in range(4)], axis=0)
        s_prev, s_own = jnp.split(_mm_nt(q4, kx) * (64 ** -0.5), 2, axis=1)
        s = jnp.where(own, s_own, jnp.where(has_prev, s_prev, -jnp.inf))
        m = jnp.maximum(jnp.max(s, axis=1, keepdims=True), sink)
        p = jnp.exp(s - m)
        pn = p / (jnp.sum(p, axis=1, keepdims=True) + jnp.exp(sink - m))
        pn2 = jnp.concatenate([jnp.where(own, 0.0, pn), jnp.where(own, pn, 0.0)], axis=1)
        o4 = jnp.split(jnp.where(o_sel, _mm(pn2, vx), 0.0), 4, axis=0)
        o_out.append(o4[0] + o4[1] + o4[2] + o4[3])
    return jnp.concatenate(o_out, axis=1) * _silu(z)


def _sw_specs(n, rev):
    ci = (lambda i: n - 1 - i) if rev else (lambda i: i)
    prev = lambda i: jnp.maximum(ci(i) - 1, 0)
    return [pl.BlockSpec((SW_B, SW_N), lambda i: (ci(i), 0)),
            pl.BlockSpec((SW_B, 512), lambda i: (prev(i), SW_KV0 // 512)),
            pl.BlockSpec((SW_B, 256), lambda i: (ci(i), 0)),
            pl.BlockSpec((SW_B, 256), lambda i: (prev(i), 0)),
            pl.BlockSpec((2, 512), lambda i: (0, 0)), pl.BlockSpec((2, 128), lambda i: (0, 0)),
            pl.BlockSpec((1, 128), lambda i: (0, 0))]


def _sw_args(u_ref, kvp_ref, csc_ref, csp_ref, gq_ref, gk_ref, sk_ref, has_prev):
    return (u_ref[:, 0:D], kvp_ref[...], u_ref[:, SW_KV0:SW_KV0 + 512], u_ref[:, SW_KV0 + 512:SW_N],
            csp_ref[...], csc_ref[...], gq_ref[...], gk_ref[...], sk_ref[...], has_prev)


def _sw_fwd(u, cs, gq, gk, sinks):
    t = u.shape[0]
    n = t // SW_B

    def body(u_ref, kvp_ref, csc_ref, csp_ref, gq_ref, gk_ref, sk_ref, p_ref):
        has_prev = pl.program_id(0) > 0
        p_ref[...] = _sw_block(*_sw_args(u_ref, kvp_ref, csc_ref, csp_ref, gq_ref, gk_ref, sk_ref, has_prev)
                               ).astype(BF16)

    return pl.pallas_call(
        body, name="sw_fwd", grid=(n,), in_specs=_sw_specs(n, False),
        out_specs=pl.BlockSpec((SW_B, D), lambda i: (i, 0)),
        out_shape=jax.ShapeDtypeStruct((t, D), BF16),
        compiler_params=_cparams(("arbitrary",)),
    )(u, u, cs, cs, gq, gk, sinks)


def _sw_bwd(u, cs, dp, gq, gk, sinks):
    t = u.shape[0]
    n = t // SW_B

    def body(u_ref, kvp_ref, csc_ref, csp_ref, gq_ref, gk_ref, sk_ref, dp_ref,
             du_ref, dgq_ref, dgk_ref, dsk_ref, dkv_ref):
        i = pl.program_id(0)

        @pl.when(i == 0)
        def _():
            for r in (dkv_ref, dgq_ref, dgk_ref, dsk_ref):
                r[...] = jnp.zeros_like(r)

        has_prev = i < n - 1
        args = _sw_args(u_ref, kvp_ref, csc_ref, csp_ref, gq_ref, gk_ref, sk_ref, has_prev)
        fn = lambda q, kvp, kvc, z, gq_, gk_, sk_: _sw_block(q, kvp, kvc, z, args[4], args[5], gq_, gk_, sk_, has_prev)
        _, vjp = jax.vjp(fn, args[0], args[1], args[2], args[3], args[6], args[7], args[8])
        dq, dkvp, dkvc, dz, dgq, dgk, dsk = vjp(dp_ref[...].astype(F32))
        du_ref[:, 0:D] = dq.astype(BF16)
        du_ref[:, SW_KV0:SW_KV0 + 512] = (dkvc + dkv_ref[...]).astype(BF16)
        du_ref[:, SW_KV0 + 512:SW_N] = dz.astype(BF16)
        dkv_ref[...] = dkvp
        dgq_ref[...] += dgq
        dgk_ref[...] += dgk
        dsk_ref[...] += dsk

    small = lambda r, w: pl.BlockSpec((r, w), lambda i: (0, 0))
    return pl.pallas_call(
        body, name="sw_bwd", grid=(n,),
        in_specs=_sw_specs(n, True) + [pl.BlockSpec((SW_B, D), lambda i: (n - 1 - i, 0))],
        out_specs=[pl.BlockSpec((SW_B, SW_N), lambda i: (n - 1 - i, 0)), small(2, 512), small(2, 128), small(1, 128)],
        out_shape=[jax.ShapeDtypeStruct((t, SW_N), BF16), jax.ShapeDtypeStruct((2, 512), F32),
                   jax.ShapeDtypeStruct((2, 128), F32), jax.ShapeDtypeStruct((1, 128), F32)],
        scratch_shapes=[pltpu.VMEM((SW_B, 512), F32)],
        compiler_params=_cparams(("arbitrary",)),
    )(u, u, cs, cs, gq, gk, sinks, dp)


def _ln_mod(x, g, scale, shift):
    y = x * lax.rsqrt(jnp.mean(x * x, axis=1, keepdims=True) + EPS) * g
    return y * (1.0 + scale) + shift


def _row_tile(t):
    return min(t, 1024)


def _ln_mm(x, g, scale, shift, w, tn):
    t, n = x.shape[0], w.shape[1]
    tm = _row_tile(t)
    vec = pl.BlockSpec((1, D), lambda i, j: (0, 0))

    def body(x_ref, g_ref, sc_ref, sh_ref, w_ref, u_ref, h_ref):
        @pl.when(pl.program_id(1) == 0)
        def _():
            h_ref[...] = _ln_mod(x_ref[...], g_ref[...], sc_ref[...], sh_ref[...]).astype(BF16)

        u_ref[...] = _dot(h_ref[...], w_ref[...], 1, 0)

    return pl.pallas_call(
        body, name="ln_mm", grid=(t // tm, n // tn),
        in_specs=[pl.BlockSpec((tm, D), lambda i, j: (i, 0)), vec, vec, vec,
                  pl.BlockSpec((D, tn), lambda i, j: (0, j))],
        out_specs=[pl.BlockSpec((tm, tn), lambda i, j: (i, j)), pl.BlockSpec((tm, D), lambda i, j: (i, 0))],
        out_shape=[jax.ShapeDtypeStruct((t, n), F32), jax.ShapeDtypeStruct((t, D), BF16)],
        compiler_params=_cparams(("arbitrary", "arbitrary")),
    )(x, g, scale, shift, w)


def _mm_res(p, w, x, gate):
    t, k = p.shape
    tm = _row_tile(t)

    def body(p_ref, w_ref, x_ref, gate_ref, o_ref):
        o_ref[...] = x_ref[...] + gate_ref[...] * _dot(p_ref[...], w_ref[...], 1, 0)

    return pl.pallas_call(
        body, name="mm_res", grid=(t // tm,),
        in_specs=[pl.BlockSpec((tm, k), lambda i: (i, 0)), pl.BlockSpec((k, D), lambda i: (0, 0)),
                  pl.BlockSpec((tm, D), lambda i: (i, 0)), pl.BlockSpec((1, D), lambda i: (0, 0))],
        out_specs=pl.BlockSpec((tm, D), lambda i: (i, 0)),
        out_shape=jax.ShapeDtypeStruct((t, D), F32),
        compiler_params=_cparams(("arbitrary",)),
    )(p, w, x, gate)


def _loss_grad(x, target):
    t = x.shape[0]
    tm = _row_tile(t)

    def body(x_ref, t_ref, l_ref, dx_ref):
        @pl.when(pl.program_id(0) == 0)
        def _():
            l_ref[...] = jnp.zeros_like(l_ref)

        err = x_ref[...] - t_ref[...]
        dx_ref[...] = err * (1.0 / D)
        l_ref[...] += 0.5 * jnp.sum(jnp.mean(err * err, axis=1, keepdims=True), axis=0, keepdims=True)

    return pl.pallas_call(
        body, name="loss_grad", grid=(t // tm,),
        in_specs=[pl.BlockSpec((tm, D), lambda i: (i, 0))] * 2,
        out_specs=[pl.BlockSpec((8, 128), lambda i: (0, 0)), pl.BlockSpec((tm, D), lambda i: (i, 0))],
        out_shape=[jax.ShapeDtypeStruct((8, 128), F32), jax.ShapeDtypeStruct((t, D), F32)],
        compiler_params=_cparams(("arbitrary",)),
    )(x, target)


def _mm_scaled(a, s, w, tn):
    t, k = a.shape
    n = w.shape[1]
    tm = _row_tile(t)

    def body(a_ref, s_ref, w_ref, o_ref):
        o_ref[...] = _dot((a_ref[...] * s_ref[...]).astype(BF16), w_ref[...], 1, 0).astype(BF16)

    return pl.pallas_call(
        body, name="mm_scaled", grid=(t // tm, n // tn),
        in_specs=[pl.BlockSpec((tm, k), lambda i, j: (i, 0)), pl.BlockSpec((1, k), lambda i, j: (0, 0)),
                  pl.BlockSpec((k, tn), lambda i, j: (0, j))],
        out_specs=pl.BlockSpec((tm, tn), lambda i, j: (i, j)),
        out_shape=jax.ShapeDtypeStruct((t, n), BF16),
        compiler_params=_cparams(("arbitrary", "arbitrary")),
    )(a, s, w)


def _mm_tn_acc(a, b, tn):
    t, m = a.shape
    n = b.shape[1]
    fits = lambda k: 2 * k * (m * a.dtype.itemsize + tn * b.dtype.itemsize) + 2 * m * tn * 4 <= 36 * 1024 * 1024
    tk = next(k for k in (4096, 2048, 1024, 512, t) if t % k == 0 and (fits(k) or k <= 512))
    nk = t // tk

    def body(a_ref, b_ref, o_ref):
        @pl.when(pl.program_id(1) == 0)
        def _():
            o_ref[...] = jnp.zeros_like(o_ref)

        o_ref[...] += _dot(a_ref[...], b_ref[...].astype(BF16), 0, 0)

    return pl.pallas_call(
        body, name="mm_tn_acc", grid=(n // tn, nk),
        in_specs=[pl.BlockSpec((tk, m), lambda j, k: (k, 0)), pl.BlockSpec((tk, tn), lambda j, k: (k, j))],
        out_specs=pl.BlockSpec((m, tn), lambda j, k: (0, j)),
        out_shape=jax.ShapeDtypeStruct((m, n), F32),
        compiler_params=_cparams(("arbitrary", "arbitrary")),
    )(a, b)


def _inproj_bwd(du, wt, x, dxp, g, scale, shift):
    t, kdim = du.shape
    tk = kdim
    tm = min(t, 512 if kdim <= 4096 else 256)
    nk = kdim // tk
    vec = pl.BlockSpec((1, D), lambda i, k: (0, 0))

    def body(du_ref, wt_ref, x_ref, dxp_ref, g_ref, sc_ref, sh_ref, dx_ref, dv_ref, acc_ref):
        k = pl.program_id(1)

        @pl.when((pl.program_id(0) == 0) & (k == 0))
        def _():
            dv_ref[...] = jnp.zeros_like(dv_ref)

        @pl.when(k == 0)
        def _():
            acc_ref[...] = jnp.zeros_like(acc_ref)

        acc_ref[...] += _dot(du_ref[...].astype(BF16), wt_ref[...], 1, 0)

        @pl.when(k == nk - 1)
        def _():
            _, vjp = jax.vjp(_ln_mod, x_ref[...], g_ref[...], sc_ref[...], sh_ref[...])
            dx, dg, dsc, dsh = vjp(acc_ref[...])
            dx_ref[...] = dxp_ref[...] + dx
            dv_ref[0:1, :] += dg
            dv_ref[1:2, :] += dsc
            dv_ref[2:3, :] += dsh

    return pl.pallas_call(
        body, name="inproj_bwd", grid=(t // tm, nk),
        in_specs=[pl.BlockSpec((tm, tk), lambda i, k: (i, k)), pl.BlockSpec((tk, D), lambda i, k: (k, 0)),
                  pl.BlockSpec((tm, D), lambda i, k: (i, 0)), pl.BlockSpec((tm, D), lambda i, k: (i, 0)),
                  vec, vec, vec],
        out_specs=[pl.BlockSpec((tm, D), lambda i, k: (i, 0)), pl.BlockSpec((8, D), lambda i, k: (0, 0))],
        out_shape=[jax.ShapeDtypeStruct((t, D), F32), jax.ShapeDtypeStruct((8, D), F32)],
        scratch_shapes=[pltpu.VMEM((tm, D), F32)],
        compiler_params=_cparams(("arbitrary", "arbitrary")),
    )(du, wt, x, dxp, g, scale, shift)


def _outgrad(gmat, w, gate):
    k = gmat.shape[0]
    tr = 256

    def body(g_ref, w_ref, gate_ref, dw_ref, dg_ref):
        @pl.when(pl.program_id(0) == 0)
        def _():
            dg_ref[...] = jnp.zeros_like(dg_ref)

        gm = g_ref[...]
        dw_ref[...] = gm * gate_ref[...]
        dg_ref[0:1, :] += jnp.sum(gm * w_ref[...].astype(F32), axis=0, keepdims=True)

    return pl.pallas_call(
        body, name="outgrad", grid=(k // tr,),
        in_specs=[pl.BlockSpec((tr, D), lambda i: (i, 0)), pl.BlockSpec((tr, D), lambda i: (i, 0)),
                  pl.BlockSpec((1, D), lambda i: (0, 0))],
        out_specs=[pl.BlockSpec((tr, D), lambda i: (i, 0)), pl.BlockSpec((8, D), lambda i: (0, 0))],
        out_shape=[jax.ShapeDtypeStruct((k, D), F32), jax.ShapeDtypeStruct((8, D), F32)],
        compiler_params=_cparams(("arbitrary",)),
    )(gmat, w, gate)


def _rope_table(pos, freq):
    t = pos.shape[0]
    tm = _row_tile(t)

    def body(p_ref, f_ref, o_ref):
        ang = p_ref[...].astype(F32) * f_ref[...]
        o_ref[:, 0:128] = jnp.cos(ang)
        o_ref[:, 128:256] = jnp.sin(ang)

    return pl.pallas_call(
        body, name="rope_table", grid=(t // tm,),
        in_specs=[pl.BlockSpec((tm, 1), lambda i: (i, 0)), pl.BlockSpec((1, 128), lambda i: (0, 0))],
        out_specs=pl.BlockSpec((tm, 256), lambda i: (i, 0)),
        out_shape=jax.ShapeDtypeStruct((t, 256), F32),
        compiler_params=_cparams(("arbitrary",)),
    )(pos, freq)


def _ada_fwd(c_all, w, b):
    nl, _, s = w.shape

    def body(c_ref, w_ref, b_ref, o_ref):
        o_ref[0] = _mm_f32(c_ref[...], w_ref[0]) + b_ref[0]

    return pl.pallas_call(
        body, name="ada_fwd", grid=(nl,),
        in_specs=[pl.BlockSpec((8, D), lambda l: (0, 0)), pl.BlockSpec((1, D, s), lambda l: (l, 0, 0)),
                  pl.BlockSpec((1, 1, s), lambda l: (l, 0, 0))],
        out_specs=pl.BlockSpec((1, 8, s), lambda l: (l, 0, 0)),
        out_shape=jax.ShapeDtypeStruct((nl, 8, s), F32),
        compiler_params=_cparams(("arbitrary",)),
    )(c_all, w, b)


def _ada_bwd(c_all, dmod_cols, dmod_all):
    nl, _, s = dmod_cols.shape

    def body(c_ref, dc_ref, da_ref, gw_ref, gb_ref):
        gw_ref[0] = _dot(c_ref[...], dc_ref[0], 0, 0, lax.Precision.HIGHEST)
        gb_ref[0] = jnp.sum(da_ref[0], axis=0, keepdims=True)

    return pl.pallas_call(
        body, name="ada_bwd", grid=(nl,),
        in_specs=[pl.BlockSpec((8, D), lambda l: (0, 0)), pl.BlockSpec((1, 8, s), lambda l: (l, 0, 0)),
                  pl.BlockSpec((1, 8, 3 * D), lambda l: (l, 0, 0))],
        out_specs=[pl.BlockSpec((1, D, s), lambda l: (l, 0, 0)), pl.BlockSpec((1, 1, 3 * D), lambda l: (l, 0, 0))],
        out_shape=[jax.ShapeDtypeStruct((nl, D, s), F32), jax.ShapeDtypeStruct((nl, 1, 3 * D), F32)],
        compiler_params=_cparams(("arbitrary",)),
    )(c_all, dmod_cols, dmod_all)


def _lb_fn(h8):
    sm = jax.nn.softmax(h8, axis=0)
    r = lax.broadcasted_iota(jnp.int32, (8, 8), 0)
    c = lax.broadcasted_iota(jnp.int32, (8, 8), 1)
    return _mm_f32(((c >= 1) & (c <= r)).astype(F32), sm)


def _lb_fwd(h8):
    def body(h_ref, o_ref):
        o_ref[...] = _lb_fn(h_ref[...])

    return pl.pallas_call(body, name="lb_fwd", out_shape=jax.ShapeDtypeStruct((8, D), F32))(h8)


def _lb_bwd(h8, dlb8):
    def body(h_ref, d_ref, o_ref):
        _, vjp = jax.vjp(_lb_fn, h_ref[...])
        o_ref[...] = vjp(d_ref[...])[0]

    return pl.pallas_call(body, name="lb_bwd", out_shape=jax.ShapeDtypeStruct((8, D), F32))(h8, dlb8)


ADAM_LR, ADAM_B1, ADAM_B2, ADAM_EPS, ADAM_WD, ADAM_STEP = 0.001, 0.9, 0.999, 1e-08, 0.01, 10


def _adamw(w, gparts, m, v):
    r, c = w.shape
    tr = r if r * c * 4 <= (1 << 20) else max(8, ((1 << 20) // (c * 4)) // 8 * 8)
    while r % tr:
        tr -= 8
    ng = len(gparts)

    def body(*refs):
        w_ref, m_ref, v_ref = refs[0], refs[1 + ng], refs[2 + ng]
        g_ref, d_ref, nm_ref, nv_ref = refs[3 + ng:]
        g = refs[1][...]
        for gr in refs[2:1 + ng]:
            g = g + gr[...]
        mm = ADAM_B1 * m_ref[...] + (1.0 - ADAM_B1) * g
        vv = ADAM_B2 * v_ref[...] + (1.0 - ADAM_B2) * (g * g)
        m_hat = mm / (1.0 - ADAM_B1 ** ADAM_STEP)
        v_hat = vv / (1.0 - ADAM_B2 ** ADAM_STEP)
        g_ref[...] = g
        d_ref[...] = -ADAM_LR * (m_hat / (jnp.sqrt(v_hat) + ADAM_EPS) + ADAM_WD * w_ref[...])
        nm_ref[...] = mm
        nv_ref[...] = vv

    spec = pl.BlockSpec((tr, c), lambda i: (i, 0))
    return pl.pallas_call(
        body, name="adamw", grid=(r // tr,), in_specs=[spec] * (3 + ng), out_specs=[spec] * 4,
        out_shape=[jax.ShapeDtypeStruct((r, c), F32)] * 4,
        compiler_params=_cparams(("arbitrary",)),
    )(w, *gparts, m, v)


def _sum_rows(parts):
    r, c = parts[0].shape
    tr = 8
    for cand in range(min(r, 512), 7, -8):
        if r % cand == 0:
            tr = cand
            break

    def body(*refs):
        acc = refs[0][...]
        for p in refs[1:-1]:
            acc = acc + p[...]
        refs[-1][...] = acc

    spec = pl.BlockSpec((tr, c), lambda i: (i, 0))
    return pl.pallas_call(
        body, name="sum_rows", grid=(r // tr,), in_specs=[spec] * len(parts), out_specs=spec,
        out_shape=jax.ShapeDtypeStruct((r, c), F32),
        compiler_params=_cparams(("arbitrary",)),
    )(*parts)


MESH = pl.DeviceIdType.MESH
ANY = pl.BlockSpec(memory_space=pl.ANY)


def _place():
    return lax.axis_index("x"), lax.axis_index("y"), lax.axis_index("c")


def _allgather8(blk):
    m_per, n = blk.shape

    def body(x_ref, out_ref, send_sems, recv_sems, local_sem):
        x, y, c = _place()
        me, sibling = (x, y, c), (x, y, 1 - c)
        chips = [(1 - x, y), (x, 1 - y), (1 - x, 1 - y)]

        def rows(px, py, pc):
            return out_ref.at[pl.ds((4 * px + 2 * py + pc) * m_per, m_per), :]

        def copy(k, block, to, src=None):
            return pltpu.make_async_remote_copy(
                src_ref=rows(*block) if src is None else src, dst_ref=rows(*block),
                send_sem=send_sems.at[k], recv_sem=recv_sems.at[k], device_id=to, device_id_type=MESH)

        mine = pltpu.make_async_copy(x_ref, rows(*me), local_sem)
        mine.start()
        first = [copy(0, me, sibling, src=x_ref)]
        first += [copy(1 + j, me, (*chip, c), src=x_ref) for j, chip in enumerate(chips)]
        for cp in first:
            cp.start()
        passed = [copy(4 + j, (*chip, c), sibling) for j, chip in enumerate(chips)]
        for j, chip in enumerate(chips):
            copy(1 + j, (*chip, c), me).wait_recv()
            passed[j].start()
        copy(0, sibling, me).wait_recv()
        for j, chip in enumerate(chips):
            copy(4 + j, (*chip, 1 - c), me).wait_recv()
        for cp in first + passed:
            cp.wait_send()
        mine.wait()

    return pl.pallas_call(
        body, name="allgather8",
        out_shape=jax.ShapeDtypeStruct((8 * m_per, n), blk.dtype),
        in_specs=[pl.BlockSpec(memory_space=pltpu.VMEM)],
        out_specs=pl.BlockSpec(memory_space=pltpu.VMEM),
        scratch_shapes=[pltpu.SemaphoreType.DMA((7,)), pltpu.SemaphoreType.DMA((7,)), pltpu.SemaphoreType.DMA],
    )(blk)


def _chip_peers():
    x, y, c = _place()
    return [(1 - x, y, c), (x, 1 - y, c), (1 - x, 1 - y, c)]


GATHER_SEMS = [pltpu.SemaphoreType.DMA((3,)), pltpu.SemaphoreType.DMA((3,)), pltpu.SemaphoreType.DMA]
SCATTER_SEMS = [pltpu.SemaphoreType.DMA((3,)), pltpu.SemaphoreType.DMA((3,))]


def _gather_plan(x_ref, out_ref, send_sems, recv_sems, local_sem):
    x, y, _ = _place()
    peers = _chip_peers()

    def copy(j, chip_index):
        return pltpu.make_async_remote_copy(
            src_ref=x_ref, dst_ref=out_ref.at[chip_index], send_sem=send_sems.at[j], recv_sem=recv_sems.at[j],
            device_id=peers[j], device_id_type=MESH)

    mine = pltpu.make_async_copy(x_ref, out_ref.at[2 * x + y], local_sem)
    sends = [copy(j, 2 * x + y) for j in range(3)]

    def start():
        mine.start()
        for cp in sends:
            cp.start()

    def wait():
        for j in range(3):
            copy(j, 2 * peers[j][0] + peers[j][1]).wait_recv()
        for cp in sends:
            cp.wait_send()
        mine.wait()

    return start, wait


def _scatter_plan(p_ref, out_ref, send_sems, recv_sems):
    peers = _chip_peers()
    sends = [pltpu.make_async_remote_copy(
        src_ref=p_ref.at[2 * peers[j][0] + peers[j][1]], dst_ref=out_ref.at[j], send_sem=send_sems.at[j],
        recv_sem=recv_sems.at[j], device_id=peers[j], device_id_type=MESH) for j in range(3)]

    def start():
        for cp in sends:
            cp.start()

    def wait():
        for cp in sends:
            cp.wait_recv()
        for cp in sends:
            cp.wait_send()

    return start, wait


def _chip_allgather(shard):
    def body(x_ref, out_ref, *sems):
        start, wait = _gather_plan(x_ref, out_ref, *sems)
        start()
        wait()

    return pl.pallas_call(
        body, name="chip_allgather", out_shape=jax.ShapeDtypeStruct((4,) + shard.shape, shard.dtype),
        in_specs=[ANY], out_specs=ANY, scratch_shapes=GATHER_SEMS,
    )(shard)


def _chip_scatter(parts):
    def body(p_ref, out_ref, *sems):
        start, wait = _scatter_plan(p_ref, out_ref, *sems)
        start()
        wait()

    return pl.pallas_call(
        body, name="chip_scatter", out_shape=jax.ShapeDtypeStruct((3,) + parts.shape[1:], parts.dtype),
        in_specs=[ANY], out_specs=ANY, scratch_shapes=SCATTER_SEMS,
    )(parts)


def _sibling_swap(a):
    def body(a_ref, out_ref, send_sem, recv_sem):
        x, y, c = _place()
        cp = pltpu.make_async_remote_copy(src_ref=a_ref, dst_ref=out_ref, send_sem=send_sem, recv_sem=recv_sem,
                                          device_id=(x, y, 1 - c), device_id_type=MESH)
        cp.start()
        cp.wait_recv()
        cp.wait_send()

    return pl.pallas_call(
        body, name="sibling_swap", out_shape=jax.ShapeDtypeStruct(a.shape, a.dtype),
        in_specs=[ANY], out_specs=ANY,
        scratch_shapes=[pltpu.SemaphoreType.DMA, pltpu.SemaphoreType.DMA],
    )(a)


WEIGHTS = ['hgrn_lb', 'ada_w', 'ada_b', 'norm_g', 'hg_in_w', 'hg_out_w', 'hg_onorm', 'sw_in_w', 'sw_out_w', 'sw_qnorm',
           'sw_knorm', 'sw_sinks', 'gd_in_w', 'gd_out_w', 'gd_conv_w', 'gd_a_log', 'gd_dt_bias', 'gd_onorm']
BIG = ['hg_in_w', 'hg_out_w', 'sw_in_w', 'sw_out_w', 'gd_in_w', 'gd_out_w']
SEG_FIRST = [('hg_in_w', 0), ('hg_out_w', 0)]
SEG_REST = [('hg_in_w', 1), ('hg_out_w', 1), ('sw_in_w', 0), ('sw_out_w', 0), ('gd_in_w', 0), ('gd_out_w', 0)]
PACK_ALIGN = 16
ROPE_THETA = 10000.0
ADA_S = 3 * D // 4
SMALL_ROW = {'hg_onorm': (0, 256), 'sw_qnorm': (256, 64), 'sw_knorm': (320, 64), 'sw_sinks': (384, 16),
             'gd_a_log': (400, 16), 'gd_dt_bias': (416, 16), 'gd_onorm': (432, 128)}


def _pack_rows(arrs):
    flat = jnp.concatenate([a.reshape(-1, D) for a in arrs], axis=0)
    return jnp.pad(flat, ((0, -flat.shape[0] % PACK_ALIGN), (0, 0)))


def _unpack_rows(packed, shapes):
    out, off = [], 0
    for s in shapes:
        rows = 1
        for d in s:
            rows *= d
        rows //= D
        out.append(packed[..., off:off + rows, :].reshape(packed.shape[:-2] + tuple(s)))
        off += rows
    return out


def _pack_small(vals):
    row = jnp.concatenate([vals[k].reshape(-1) for k in SMALL_ROW])
    row = jnp.pad(row, (0, D - row.shape[0]))[None]
    return jnp.concatenate([vals['hgrn_lb'], vals['norm_g'], vals['gd_conv_w'].reshape(16, D), row,
                            jnp.zeros((7, D), F32)], axis=0)


def _sw_cols(w, inverse=False):
    def split(a, heads):
        shp = (a.shape[0], 2, heads, 32) if inverse else (a.shape[0], heads, 2, 32)
        return a.reshape(shp).transpose(0, 2, 1, 3).reshape(a.shape[0], heads * 64)
    return jnp.concatenate([split(w[:, 0:1024], 16), split(w[:, 1024:1280], 4), w[:, 1280:]], axis=1)


def kernel(x, c, positions, hgrn_lb, ada_w, ada_b, norm_g, hg_in_w, hg_out_w, hg_onorm, sw_in_w, sw_out_w, sw_qnorm, sw_knorm, sw_sinks, gd_in_w, gd_out_w, gd_conv_w, gd_a_log, gd_dt_bias, gd_onorm, loss_target, m_hgrn_lb, m_ada_w, m_ada_b, m_norm_g, m_hg_in_w, m_hg_out_w, m_hg_onorm, m_sw_in_w, m_sw_out_w, m_sw_qnorm, m_sw_knorm, m_sw_sinks, m_gd_in_w, m_gd_out_w, m_gd_conv_w, m_gd_a_log, m_gd_dt_bias, m_gd_onorm, v_hgrn_lb, v_ada_w, v_ada_b, v_norm_g, v_hg_in_w, v_hg_out_w, v_hg_onorm, v_sw_in_w, v_sw_out_w, v_sw_qnorm, v_sw_knorm, v_sw_sinks, v_gd_in_w, v_gd_out_w, v_gd_conv_w, v_gd_a_log, v_gd_dt_bias, v_gd_onorm):
    w_in = dict(hgrn_lb=hgrn_lb, ada_w=ada_w, ada_b=ada_b, norm_g=norm_g, hg_in_w=hg_in_w, hg_out_w=hg_out_w,
                hg_onorm=hg_onorm, sw_in_w=sw_in_w, sw_out_w=sw_out_w, sw_qnorm=sw_qnorm, sw_knorm=sw_knorm,
                sw_sinks=sw_sinks, gd_in_w=gd_in_w, gd_out_w=gd_out_w, gd_conv_w=gd_conv_w, gd_a_log=gd_a_log,
                gd_dt_bias=gd_dt_bias, gd_onorm=gd_onorm)
    m_in = dict(zip(WEIGHTS, (m_hgrn_lb, m_ada_w, m_ada_b, m_norm_g, m_hg_in_w, m_hg_out_w, m_hg_onorm, m_sw_in_w,
                              m_sw_out_w, m_sw_qnorm, m_sw_knorm, m_sw_sinks, m_gd_in_w, m_gd_out_w, m_gd_conv_w,
                              m_gd_a_log, m_gd_dt_bias, m_gd_onorm)))
    v_in = dict(zip(WEIGHTS, (v_hgrn_lb, v_ada_w, v_ada_b, v_norm_g, v_hg_in_w, v_hg_out_w, v_hg_onorm, v_sw_in_w,
                              v_sw_out_w, v_sw_qnorm, v_sw_knorm, v_sw_sinks, v_gd_in_w, v_gd_out_w, v_gd_conv_w,
                              v_gd_a_log, v_gd_dt_bias, v_gd_onorm)))
    ax, ay, ac = _place()
    chip = 2 * ax + ay
    bidx = 4 * ax + 2 * ay + ac
    t = x.shape[1]
    x0, target = x[0], loss_target[0]

    c_all = _allgather8(jnp.pad(c, ((0, 7), (0, 0)))).reshape(8, 8, D)[:, 0, :]
    ada_b_cols = lax.dynamic_slice(ada_b, (0, chip * ADA_S), (4, ADA_S)).reshape(4, 1, ADA_S)
    mod_sh = _ada_fwd(c_all, ada_w, ada_b_cols)
    mod_g = _allgather8(mod_sh.reshape(32, ADA_S)).reshape(4, 2, 4, 8, ADA_S)[:, 0]
    mod = lax.dynamic_index_in_dim(mod_g, bidx, axis=2, keepdims=False).transpose(1, 0, 2).reshape(4, 3 * D)
    shift = [mod[l:l + 1, 0:D] for l in range(4)]
    scale = [mod[l:l + 1, D:2 * D] for l in range(4)]
    gate = [mod[l:l + 1, 2 * D:3 * D] for l in range(4)]

    h8 = jnp.concatenate([hgrn_lb, jnp.full((4, D), -1e30, F32)], axis=0)
    lb_all = _lb_fwd(h8)
    freq = ROPE_THETA ** (-jnp.arange(0, 64, 2, dtype=F32) / 64)
    cs = _rope_table(positions.reshape(t, 1), jnp.tile(freq, 4)[None])

    seg_shapes = lambda seg: [w_in[k].shape[1:] for k, _ in seg]
    pack_seg = lambda src, seg: _pack_rows([src[k][i] for k, i in seg])
    cols_full = lambda a: a.transpose(1, 0, 2).reshape(a.shape[1], 4 * a.shape[2])
    hg_in0_k, hg_out0_k = _unpack_rows(_chip_allgather(pack_seg(w_in, SEG_FIRST).astype(BF16)), seg_shapes(SEG_FIRST))
    win, wout = [cols_full(hg_in0_k)], [hg_out0_k.reshape(D, D)]
    rest_shard = pack_seg(w_in, SEG_REST).astype(BF16)
    tn_in = [1024, 1280, 896, 1024]

    gq = jnp.stack([jnp.tile(sw_qnorm[0, :32], 16), jnp.tile(sw_qnorm[0, 32:], 16)])
    gk = jnp.stack([jnp.tile(sw_knorm[0, :32], 4), jnp.tile(sw_knorm[0, 32:], 4)])
    pad128 = lambda a: jnp.pad(a, ((0, 0), (0, HD - a.shape[1])))
    sinks, alog, dtb = pad128(sw_sinks), pad128(gd_a_log), pad128(gd_dt_bias)
    cw8 = jnp.pad(_chip_allgather(gd_conv_w[0]).transpose(1, 0, 2).reshape(4, GD_QKV), ((0, 4), (0, 0)))
    lbs = {0: lb_all[0:1], 3: lb_all[3:4]}

    xs, us, hs, ps, stss = [x0], [], [], [], []
    for l in range(4):
        u, h = _ln_mm(xs[l], norm_g[l:l + 1], scale[l], shift[l], win[l], tn_in[l])
        if l == 0:
            p, sts, rest_k = _hg_fwd(u, lbs[l], hg_onorm[0:1], gather=rest_shard)
            hg_in1_k, hg_out1_k, sw_in_k, sw_out_k, gd_in_k, gd_out_k = _unpack_rows(rest_k, seg_shapes(SEG_REST))
            win += [_sw_cols(cols_full(sw_in_k)), jnp.pad(cols_full(gd_in_k), ((0, 0), (0, GD_N - 6176))),
                    cols_full(hg_in1_k)]
            wout += [sw_out_k.reshape(D, D), gd_out_k.reshape(GD_VW, D), hg_out1_k.reshape(D, D)]
        elif l % 3 == 0:
            p, sts = _hg_fwd(u, lbs[l], hg_onorm[l // 3:l // 3 + 1])
        elif l % 3 == 1:
            p, sts = _sw_fwd(u, cs, gq, gk, sinks), None
        else:
            p, *sts = _gd_fwd(u, cw8, alog, dtb, gd_onorm)
        xs.append(_mm_res(p, wout[l], xs[l], gate[l]))
        us.append(u), hs.append(h), ps.append(p), stss.append(sts)
    lpart, dx = _loss_grad(xs[4], target)
    loss = lax.psum(lpart[0, 0], ("x", "y", "c"))

    by_chip = lambda g, cols: g.reshape(g.shape[0], 4, cols).transpose(1, 0, 2)
    g_small = {}
    d_in, d_out, dmod, dnorm_g, dlb8, dgo_hg = [None] * 4, [None] * 4, [None] * 4, [None] * 4, jnp.zeros((8, D), F32), {}
    for l in (3, 2, 1, 0):
        dp = _mm_scaled(dx, gate[l], wout[l].T, 1024)
        d_out[l], dgate = _outgrad(_mm_tn_acc(ps[l], dx, 512), wout[l], gate[l])
        if l == 0:
            rest_parts = {('hg_in_w', 1): by_chip(d_in[3], D), ('hg_out_w', 1): d_out[3].reshape(4, D // 4, D),
                          ('sw_in_w', 0): by_chip(_sw_cols(d_in[1], inverse=True), SW_N // 4),
                          ('sw_out_w', 0): d_out[1].reshape(4, D // 4, D),
                          ('gd_in_w', 0): by_chip(d_in[2][:, :6176], 1544),
                          ('gd_out_w', 0): d_out[2].reshape(4, GD_VW // 4, D)}
            rest_packed = jnp.stack([_pack_rows([rest_parts[s][j] for s in SEG_REST]) for j in range(4)])
            du, dlb, dgo_hg[0], rest_recv = _hg_bwd(us[l], stss[l], dp, lbs[l], hg_onorm[0:1],
                                                    scatter=rest_packed.astype(BF16))
            dlb8 = lax.dynamic_update_slice(dlb8, dlb, (l, 0))
        elif l % 3 == 0:
            du, dlb, dgo_hg[l // 3] = _hg_bwd(us[l], stss[l], dp, lbs[l], hg_onorm[l // 3:l // 3 + 1])
            dlb8 = lax.dynamic_update_slice(dlb8, dlb, (l, 0))
        elif l % 3 == 1:
            du, dgq, dgk, dsk = _sw_bwd(us[l], cs, dp, gq, gk, sinks)
            g_small['sw_qnorm'] = jnp.concatenate([dgq[0].reshape(16, 32).sum(0), dgq[1].reshape(16, 32).sum(0)])
            g_small['sw_knorm'] = jnp.concatenate([dgk[0].reshape(4, 32).sum(0), dgk[1].reshape(4, 32).sum(0)])
            g_small['sw_sinks'] = dsk[0, :16]
        else:
            du, dcw, dalog, ddtb, g_small['gd_onorm'] = _gd_bwd(us[l], *stss[l], dp, cw8, alog, dtb, gd_onorm)
            g_small['gd_conv_w'], g_small['gd_a_log'], g_small['gd_dt_bias'] = dcw[:4], dalog[0, :16], ddtb[0, :16]
        d_in[l] = _mm_tn_acc(hs[l], du, 896 if l == 2 else 512)
        dx, dvec = _inproj_bwd(du, win[l].T, xs[l], dx, norm_g[l:l + 1], scale[l], shift[l])
        dnorm_g[l] = dvec[0:1]
        dmod[l] = jnp.concatenate([dvec[2:3], dvec[1:2], dgate[0:1]], axis=1)
    grad_x = dx[None]

    g_small['hgrn_lb'] = _lb_bwd(h8, dlb8)[0:4]
    g_small['norm_g'] = jnp.concatenate(dnorm_g, axis=0)
    g_small['hg_onorm'] = jnp.concatenate([dgo_hg[0], dgo_hg[1]], axis=0)
    gs_all = _allgather8(_pack_small(g_small))
    gs = _sum_rows([gs_all[32 * d:32 * (d + 1)] for d in range(8)])

    def small_view(packed, k):
        if k == 'hgrn_lb':
            return packed[0:4]
        if k == 'norm_g':
            return packed[4:8]
        off, size = SMALL_ROW[k]
        return packed[24, off:off + size].reshape(w_in[k].shape)

    conv_sl = lambda full: lax.dynamic_slice(full.reshape(4, GD_QKV), (0, chip * D), (4, D))
    out = {}

    def put(k, res, shape):
        for name, r in zip(('grad_', 'delta_', 'new_m_', 'new_v_'), res):
            out[name + k] = r.reshape(shape)

    zero_conv = dict(gd_conv_w=jnp.zeros((4, GD_QKV), F32))
    small_names = ['hgrn_lb', 'norm_g'] + list(SMALL_ROW)
    res = _adamw(_pack_small({**{k: w_in[k] for k in small_names}, **zero_conv}), (gs,),
                 _pack_small({**{k: m_in[k] for k in small_names}, **zero_conv}),
                 _pack_small({**{k: v_in[k] for k in small_names}, **zero_conv}))
    for k in small_names:
        put(k, [small_view(r, k) for r in res], w_in[k].shape)
    put('gd_conv_w', _adamw(gd_conv_w[0], (conv_sl(gs[8:24]),), m_in['gd_conv_w'][0], v_in['gd_conv_w'][0]),
        gd_conv_w.shape)

    dm = _allgather8(jnp.pad(jnp.concatenate(dmod, axis=0), ((0, 4), (0, 0)))).reshape(8, 8, 3 * D)[:, :4]
    dm = dm.transpose(1, 0, 2)
    g_ada_w, g_ada_b = _ada_bwd(c_all, lax.dynamic_slice(dm, (0, 0, chip * ADA_S), (4, 8, ADA_S)), dm)
    put('ada_w', _adamw(ada_w.reshape(4 * D, ADA_S), (g_ada_w.reshape(4 * D, ADA_S),),
                        m_in['ada_w'].reshape(4 * D, ADA_S), v_in['ada_w'].reshape(4 * D, ADA_S)), ada_w.shape)
    put('ada_b', _adamw(ada_b, (g_ada_b.reshape(4, 3 * D),), m_in['ada_b'], v_in['ada_b']), ada_b.shape)

    first_parts = {('hg_in_w', 0): by_chip(d_in[0], D), ('hg_out_w', 0): d_out[0].reshape(4, D // 4, D)}
    first_packed = jnp.stack([_pack_rows([first_parts[s][j] for s in SEG_FIRST]) for j in range(4)])
    first_recv = _chip_scatter(first_packed.astype(BF16))
    own = lambda packed: lax.dynamic_index_in_dim(packed, chip, axis=0, keepdims=False)
    half = jnp.concatenate([_sum_rows([own(first_packed), first_recv[0], first_recv[1], first_recv[2]]),
                            _sum_rows([own(rest_packed), rest_recv[0], rest_recv[1], rest_recv[2]])], axis=0)
    other = _sibling_swap(half)
    pack_all = lambda src: jnp.concatenate([pack_seg(src, SEG_FIRST), pack_seg(src, SEG_REST)], axis=0)
    res = _adamw(pack_all(w_in), (half, other), pack_all(m_in), pack_all(v_in))
    n_first = first_packed.shape[1]
    for name, r in zip(('grad_', 'delta_', 'new_m_', 'new_v_'), res):
        pieces = dict(zip(SEG_FIRST, _unpack_rows(r[:n_first], seg_shapes(SEG_FIRST))))
        pieces.update(zip(SEG_REST, _unpack_rows(r[n_first:], seg_shapes(SEG_REST))))
        for k in BIG:
            out[name + k] = jnp.stack([pieces[(k, i)] for i in range(w_in[k].shape[0])])

    return (loss, grad_x, *[out[p + k] for p in ('grad_', 'delta_', 'new_m_', 'new_v_') for k in WEIGHTS])
```

```python
import functools

import jax
import jax.numpy as jnp
from jax import lax
from jax.experimental import pallas as pl
from jax.experimental.pallas import tpu as pltpu

F32 = jnp.float32
BF16 = jnp.bfloat16
D = 1024
EPS = 1e-6
CHUNK = 64
SUB = 32
HG_H = 8
HD = 128
VMEM_LIMIT = 56 * 1024 * 1024


def _cparams(sem=None):
    return pltpu.CompilerParams(dimension_semantics=sem, vmem_limit_bytes=VMEM_LIMIT)


def _dot(a, b, ca, cb, prec=None):
    return lax.dot_general(a, b, (((ca,), (cb,)), ((), ())), precision=prec, preferred_element_type=F32)


def _mm(a, b):
    return _dot(a.astype(BF16), b.astype(BF16), 1, 0)


def _mm_nt(a, b):
    return _dot(a.astype(BF16), b.astype(BF16), 1, 1)


def _mm_tn(a, b):
    return _dot(a.astype(BF16), b.astype(BF16), 0, 0)


def _mm_f32(a, b):
    return _dot(a, b, 1, 0, lax.Precision.HIGHEST)


def _silu(x):
    return x * jax.nn.sigmoid(x)


def _cumsum_impl(x):
    row = lax.broadcasted_iota(jnp.int32, x.shape, 0)
    s = 1
    while s < x.shape[0]:
        x = x + jnp.where(row >= s, pltpu.roll(x, s, 0), 0.0)
        s *= 2
    return x


@jax.custom_vjp
def _cumsum_rows(x):
    return _cumsum_impl(x)


_cumsum_rows.defvjp(lambda x: (_cumsum_impl(x), None),
                    lambda _, g: (jnp.sum(g, axis=0, keepdims=True) - _cumsum_impl(g) + g,))


def _roll_rows(x, shift):
    n = x.shape[0]

    @jax.custom_vjp
    def f(a):
        return pltpu.roll(a, shift, 0)

    f.defvjp(lambda a: (pltpu.roll(a, shift, 0), None), lambda _, g: (pltpu.roll(g, n - shift, 0),))
    return f(x)


def _hg_chunk(q_raw, f_pre, v, z, st, lb, go):
    c = q_raw.shape[0]
    nsub = c // SUB
    lf = jnp.log(lb + (1.0 - lb) * jax.nn.sigmoid(f_pre))
    k = (1.0 - lb) * jax.nn.sigmoid(-f_pre)
    q = _silu(q_raw)
    b = _cumsum_rows(lf)
    rowf = lax.broadcasted_iota(jnp.int32, lf.shape, 0)
    bmid = [jnp.sum(jnp.where(rowf == SUB * i + SUB // 2, b, 0.0), axis=0, keepdims=True) for i in range(nsub)]
    row = lax.broadcasted_iota(jnp.int32, (c, 1), 0)
    ref = sum(jnp.where((row >= SUB * i) & (row < SUB * (i + 1)), bmid[i], 0.0) for i in range(nsub))
    qt = q * jnp.exp(b - ref)
    kall = jnp.concatenate(
        [k * jnp.exp(jnp.where(row < SUB * (i + 1), bmid[i] - b, -jnp.inf)) for i in range(nsub)], axis=0)
    v4 = jnp.concatenate([v] * nsub, axis=0)
    b_last = jnp.sum(lf, axis=0, keepdims=True)
    qb = q * jnp.exp(b)
    kd = k * jnp.exp(b_last - b)
    e_last = jnp.exp(b_last)
    tq = lax.broadcasted_iota(jnp.int32, (c, nsub * c), 0)
    cq = lax.broadcasted_iota(jnp.int32, (c, nsub * c), 1)
    m_all = ((cq // c) == (tq // SUB)) & ((cq % c) <= tq)
    hs = lambda a: jnp.split(a, HG_H, axis=1)
    qt_h, kall_h, v4_h, qb_h, kd_h, v_h, z_h, el_h = map(hs, (qt, kall, v4, qb, kd, v, z, e_last))
    st_h = jnp.split(st, HG_H, axis=0)
    p_out, st_out = [], []
    for h in range(HG_H):
        pm = jnp.where(m_all, _mm_nt(qt_h[h], kall_h[h]), 0.0)
        o = _mm(pm, v4_h[h]) + _mm_nt(qb_h[h], st_h[h])
        st_out.append(el_h[h] * st_h[h] + _mm_tn(v_h[h], kd_h[h]))
        y = o * lax.rsqrt(jnp.mean(o * o, axis=1, keepdims=True) + EPS) * go
        p_out.append(y * _silu(z_h[h]))
    return jnp.concatenate(p_out, axis=1), jnp.concatenate(st_out, axis=0)


def _hg_fwd(u, lb, go, gather=None):
    t = u.shape[0]
    n = t // CHUNK

    def body(u_ref, lb_ref, go_ref, *rest):
        if gather is None:
            p_ref, sts_ref, st_ref = rest
        else:
            shard_ref, p_ref, sts_ref, all_ref, st_ref, *sems = rest
            start, wait = _gather_plan(shard_ref, all_ref, *sems)
            pl.when(pl.program_id(0) == 0)(start)

        @pl.when(pl.program_id(0) == 0)
        def _():
            st_ref[...] = jnp.zeros_like(st_ref)

        st = st_ref[...]
        sts_ref[0] = st
        p, st_next = _hg_chunk(u_ref[:, 0:D], u_ref[:, D:2 * D], u_ref[:, 2 * D:3 * D], u_ref[:, 3 * D:4 * D],
                               st, lb_ref[...], go_ref[...])
        p_ref[...] = p.astype(BF16)
        st_ref[...] = st_next
        if gather is not None:
            pl.when(pl.program_id(0) == n - 1)(wait)

    more = gather is not None
    return pl.pallas_call(
        body, name="hg_fwd_gather" if more else "hg_fwd", grid=(n,),
        in_specs=[pl.BlockSpec((CHUNK, 4 * D), lambda i: (i, 0)),
                  pl.BlockSpec((1, D), lambda i: (0, 0)),
                  pl.BlockSpec((1, HD), lambda i: (0, 0))] + [ANY] * more,
        out_specs=[pl.BlockSpec((CHUNK, D), lambda i: (i, 0)),
                   pl.BlockSpec((1, HG_H * HD, HD), lambda i: (i, 0, 0))] + [ANY] * more,
        out_shape=[jax.ShapeDtypeStruct((t, D), BF16), jax.ShapeDtypeStruct((n, HG_H * HD, HD), F32)]
        + ([jax.ShapeDtypeStruct((4,) + gather.shape, gather.dtype)] if more else []),
        scratch_shapes=[pltpu.VMEM((HG_H * HD, HD), F32)] + GATHER_SEMS * more,
        compiler_params=_cparams(("arbitrary",)),
    )(u, lb, go, *([gather] * more))


def _hg_bwd(u, sts, dp, lb, go, scatter=None):
    t = u.shape[0]
    n = t // CHUNK

    def body(u_ref, sts_ref, dp_ref, lb_ref, go_ref, *rest):
        if scatter is None:
            du_ref, dlb_ref, dgo_ref, dst_ref = rest
        else:
            parts_ref, du_ref, dlb_ref, dgo_ref, recv_ref, dst_ref, *sems = rest
            start, wait = _scatter_plan(parts_ref, recv_ref, *sems)
            pl.when(pl.program_id(0) == 0)(start)

        @pl.when(pl.program_id(0) == 0)
        def _():
            dst_ref[...] = jnp.zeros_like(dst_ref)
            dlb_ref[...] = jnp.zeros_like(dlb_ref)
            dgo_ref[...] = jnp.zeros_like(dgo_ref)

        _, vjp = jax.vjp(_hg_chunk, u_ref[:, 0:D], u_ref[:, D:2 * D], u_ref[:, 2 * D:3 * D], u_ref[:, 3 * D:4 * D],
                         sts_ref[0], lb_ref[...], go_ref[...])
        dq, df, dv, dz, dst, dlb, dgo = vjp((dp_ref[...].astype(F32), dst_ref[...]))
        du_ref[:, 0:D] = dq.astype(BF16)
        du_ref[:, D:2 * D] = df.astype(BF16)
        du_ref[:, 2 * D:3 * D] = dv.astype(BF16)
        du_ref[:, 3 * D:4 * D] = dz.astype(BF16)
        dst_ref[...] = dst
        dlb_ref[...] += dlb
        dgo_ref[...] += dgo
        if scatter is not None:
            pl.when(pl.program_id(0) == n - 1)(wait)

    rev = lambda i: (n - 1 - i, 0)
    more = scatter is not None
    return pl.pallas_call(
        body, name="hg_bwd_scatter" if more else "hg_bwd", grid=(n,),
        in_specs=[pl.BlockSpec((CHUNK, 4 * D), rev),
                  pl.BlockSpec((1, HG_H * HD, HD), lambda i: (n - 1 - i, 0, 0)),
                  pl.BlockSpec((CHUNK, D), rev),
                  pl.BlockSpec((1, D), lambda i: (0, 0)),
                  pl.BlockSpec((1, HD), lambda i: (0, 0))] + [ANY] * more,
        out_specs=[pl.BlockSpec((CHUNK, 4 * D), rev),
                   pl.BlockSpec((1, D), lambda i: (0, 0)),
                   pl.BlockSpec((1, HD), lambda i: (0, 0))] + [ANY] * more,
        out_shape=[jax.ShapeDtypeStruct((t, 4 * D), BF16), jax.ShapeDtypeStruct((1, D), F32),
                   jax.ShapeDtypeStruct((1, HD), F32)]
        + ([jax.ShapeDtypeStruct((3,) + scatter.shape[1:], scatter.dtype)] if more else []),
        scratch_shapes=[pltpu.VMEM((HG_H * HD, HD), F32)] + SCATTER_SEMS * more,
        compiler_params=_cparams(("arbitrary",)),
    )(u, sts, dp, lb, go, *([scatter] * more))


GD_VH = 16
GD_QKH = 8
GD_QKV = 4096
GD_VW = 2048
GD_N = GD_QKV + GD_VW + HD
GD_GRP = 4
GD_SOLVE = (GD_VH // GD_GRP, GD_GRP * CHUNK, 2 * HD)
HALO = 8


def _mm_high(a, b):
    return _dot(a, b, 1, 0, lax.Precision.HIGH)


def _lane_pick(a, h):
    lane = lax.broadcasted_iota(jnp.int32, a.shape, 1)
    return jnp.sum(jnp.where(lane == h, a, 0.0), axis=1, keepdims=True)


def _l2n(x):
    return x * lax.rsqrt(jnp.sum(x * x, axis=1, keepdims=True) + EPS)


def _solve_fwd(a_mat, rhs):
    n = a_mat.shape[0]
    r_i, c_i = lax.broadcasted_iota(jnp.int32, (n, n), 0), lax.broadcasted_iota(jnp.int32, (n, n), 1)
    same = lambda nb: (r_i // nb) == (c_i // nb)
    d0 = jnp.where(same(8), a_mat, 0.0)
    d2 = _mm(d0, d0)
    tinv = (r_i == c_i).astype(F32) - d0
    tinv = tinv + _mm(tinv, d2)
    tinv = tinv + _mm(tinv, _mm(d2, d2))
    nb = 16
    while nb <= CHUNK:
        low = jnp.where(same(nb) & ~same(nb // 2), a_mat, 0.0)
        tinv = tinv - _mm(_mm(tinv, low), tinv)
        nb *= 2
    x = _mm(tinv, rhs)
    return x, (tinv, x)


def _solve_bwd(res, dx):
    tinv, x = res
    drhs = _dot(tinv, dx, 0, 0, lax.Precision.HIGH)
    return -_dot(drhs, x, 1, 1, lax.Precision.HIGH), drhs


@jax.custom_vjp
def _solved(a_mat, rhs, tinv, x):
    return x


_solved.defvjp(lambda a_mat, rhs, tinv, x: (x, (tinv, x)),
               lambda res, dx: _solve_bwd(res, dx) + (jnp.zeros_like(res[0]), jnp.zeros_like(res[1])))


def _gd_chunk(xh, x, z, ab, st, cw, alog, dtb, go, solve):
    c = x.shape[0]
    xa = jnp.concatenate([xh, x], axis=0)
    sh = [jnp.split(_roll_rows(xa, 3 - j), [HALO], axis=0)[1] for j in range(3)]
    qkv = _silu(cw[0:1] * sh[0] + cw[1:2] * sh[1] + cw[2:3] * sh[2] + cw[3:4] * x)
    q_all, k_all, v_all = jnp.split(qkv, [1024, 2048], axis=1)
    lane = lax.broadcasted_iota(jnp.int32, (c, HD), 1)
    a_part = jnp.where(lane < GD_VH, ab, 0.0)
    g_all = -jnp.exp(alog) * jax.nn.softplus(a_part + dtb)
    d_all = _cumsum_rows(g_all)
    dl_all = jnp.sum(g_all, axis=0, keepdims=True)
    beta_all = jax.nn.sigmoid(ab)
    gc = GD_GRP * c
    r_i, c_i = lax.broadcasted_iota(jnp.int32, (gc, gc), 0), lax.broadcasted_iota(jnp.int32, (gc, gc), 1)
    same_head = (r_i // c) == (c_i // c)
    tri_g, strict_g = same_head & (c_i <= r_i), same_head & (c_i < r_i)
    qs =jnp.split(q_all, GD_QKH, axis=1)
    ks = jnp.split(k_all, GD_QKH, axis=1)
    vs = jnp.split(v_all, GD_VH, axis=1)
    zs = jnp.split(z, GD_VH, axis=1)
    sts = jnp.split(st, GD_VH, axis=0)
    qn = [_l2n(a) * (HD ** -0.5) for a in qs]
    kn = [_l2n(a) for a in ks]
    p_out, st_out = [], []
    for g in range(GD_VH // GD_GRP):
        heads = range(GD_GRP * g, GD_GRP * (g + 1))
        stack = lambda f: jnp.concatenate([f(h) for h in heads], axis=0)
        q_, k_, v_ = stack(lambda h: qn[h // 2]), stack(lambda h: kn[h // 2]), stack(lambda h: vs[h])
        dcol = stack(lambda h: _lane_pick(d_all, h))
        bcol = stack(lambda h: _lane_pick(beta_all, GD_VH + h))
        dlast = stack(lambda h: jnp.broadcast_to(_lane_pick(dl_all, h), (c, 1)))
        drow = jnp.sum(jnp.broadcast_to(dcol, (gc, HD)).T, axis=0, keepdims=True) * (1.0 / HD)
        dec = jnp.exp(jnp.where(tri_g, dcol - drow, -jnp.inf))
        kb = k_ * bcol
        a_mat = jnp.where(strict_g, _mm_nt(kb, k_) * dec, 0.0)
        xsol = solve(g, a_mat, jnp.concatenate([v_ * bcol, kb * jnp.exp(dcol)], axis=1))
        u_, w_ = jnp.split(xsol, 2, axis=1)
        w_h = jnp.split(w_, GD_GRP, axis=0)
        v_new = u_ - jnp.concatenate([_mm(w_h[i], sts[h]) for i, h in enumerate(heads)], axis=0)
        qd_h = jnp.split(q_ * jnp.exp(dcol), GD_GRP, axis=0)
        o_g = _mm(_mm_nt(q_, k_) * dec, v_new) + jnp.concatenate(
            [_mm(qd_h[i], sts[h]) for i, h in enumerate(heads)], axis=0)
        kd_h = jnp.split(k_ * jnp.exp(dlast - dcol), GD_GRP, axis=0)
        vn_h = jnp.split(v_new, GD_GRP, axis=0)
        o_h = jnp.split(o_g, GD_GRP, axis=0)
        for i, h in enumerate(heads):
            st_out.append(sts[h] * jnp.exp(_lane_pick(dl_all, h)) + _mm_tn(kd_h[i], vn_h[i]))
            o = o_h[i]
            y = o * lax.rsqrt(jnp.mean(o * o, axis=1, keepdims=True) + EPS) * go
            p_out.append(y * _silu(zs[h]))
    return jnp.concatenate(p_out, axis=1), jnp.concatenate(st_out, axis=0)


def _gd_specs(n, rev):
    ci = (lambda i: n - 1 - i) if rev else (lambda i: i)
    return [pl.BlockSpec((HALO, GD_QKV), lambda i: (jnp.maximum(ci(i) * (CHUNK // HALO) - 1, 0), 0)),
            pl.BlockSpec((CHUNK, GD_N), lambda i: (ci(i), 0))]


def _gd_load(uh_ref, u_ref, first):
    xh = jnp.where(first, 0.0, uh_ref[...])
    return xh, u_ref[:, 0:GD_QKV], u_ref[:, GD_QKV:GD_QKV + GD_VW], u_ref[:, GD_QKV + GD_VW:GD_N]


def _gd_fwd(u, cw, alog, dtb, go):
    t = u.shape[0]
    n = t // CHUNK
    small = lambda r, w: pl.BlockSpec((r, w), lambda i: (0, 0))

    def body(uh_ref, u_ref, cw_ref, alog_ref, dtb_ref, go_ref, p_ref, sts_ref, tinv_ref, xsol_ref, st_ref):
        i = pl.program_id(0)

        @pl.when(i == 0)
        def _():
            st_ref[...] = jnp.zeros_like(st_ref)

        def solve(g, a_mat, rhs):
            xsol, (tinv, _) = _solve_fwd(a_mat, rhs)
            tinv_ref[0, g] = tinv
            xsol_ref[0, g] = xsol
            return xsol

        st = st_ref[...]
        sts_ref[0] = st
        p, st_next = _gd_chunk(*_gd_load(uh_ref, u_ref, i == 0), st, cw_ref[...], alog_ref[...], dtb_ref[...],
                               go_ref[...], solve)
        p_ref[...] = p.astype(BF16)
        st_ref[...] = st_next

    return pl.pallas_call(
        body, name="gd_fwd", grid=(n,),
        in_specs=_gd_specs(n, False) + [small(8, GD_QKV), small(1, HD), small(1, HD), small(1, HD)],
        out_specs=[pl.BlockSpec((CHUNK, GD_VW), lambda i: (i, 0)),
                   pl.BlockSpec((1, GD_VH * HD, HD), lambda i: (i, 0, 0)),
                   pl.BlockSpec((1,) + GD_SOLVE, lambda i: (i, 0, 0, 0)),
                   pl.BlockSpec((1,) + GD_SOLVE, lambda i: (i, 0, 0, 0))],
        out_shape=[jax.ShapeDtypeStruct((t, GD_VW), BF16), jax.ShapeDtypeStruct((n, GD_VH * HD, HD), F32),
                   jax.ShapeDtypeStruct((n,) + GD_SOLVE, F32), jax.ShapeDtypeStruct((n,) + GD_SOLVE, F32)],
        scratch_shapes=[pltpu.VMEM((GD_VH * HD, HD), F32)],
        compiler_params=_cparams(("arbitrary",)),
    )(u, u, cw, alog, dtb, go)


def _gd_bwd(u, sts, tinvs, xsols, dp, cw, alog, dtb, go):
    t = u.shape[0]
    n = t // CHUNK
    small = lambda r, w: pl.BlockSpec((r, w), lambda i: (0, 0))

    def body(uh_ref, u_ref, sts_ref, tinv_ref, xsol_ref, dp_ref, cw_ref, alog_ref, dtb_ref, go_ref,
             du_ref, dcw_ref, dalog_ref, ddtb_ref, dgo_ref, dst_ref, dhalo_ref):
        i = pl.program_id(0)

        @pl.when(i == 0)
        def _():
            for r in (dst_ref, dhalo_ref, dcw_ref, dalog_ref, ddtb_ref, dgo_ref):
                r[...] = jnp.zeros_like(r)

        solve = lambda g, a_mat, rhs: _solved(a_mat, rhs, tinv_ref[0, g], xsol_ref[0, g])
        chunk = functools.partial(_gd_chunk, solve=solve)
        _, vjp = jax.vjp(chunk, *_gd_load(uh_ref, u_ref, i == n - 1), sts_ref[0], cw_ref[...], alog_ref[...],
                         dtb_ref[...], go_ref[...])
        dxh, dx, dz, dab, dst, dcw, dalog, ddtb, dgo = vjp((dp_ref[...].astype(F32), dst_ref[...]))
        tail = jnp.concatenate([jnp.zeros((CHUNK - HALO, GD_QKV), F32), dhalo_ref[...]], axis=0)
        du_ref[:, 0:GD_QKV] = (dx + tail).astype(BF16)
        du_ref[:, GD_QKV:GD_QKV + GD_VW] = dz.astype(BF16)
        du_ref[:, GD_QKV + GD_VW:GD_N] = dab.astype(BF16)
        dhalo_ref[...] = dxh
        dst_ref[...] = dst
        dcw_ref[...] += dcw
        dalog_ref[...] += dalog
        ddtb_ref[...] += ddtb
        dgo_ref[...] += dgo

    return pl.pallas_call(
        body, name="gd_bwd", grid=(n,),
        in_specs=_gd_specs(n, True) + [pl.BlockSpec((1, GD_VH * HD, HD), lambda i: (n - 1 - i, 0, 0)),
                                       pl.BlockSpec((1,) + GD_SOLVE, lambda i: (n - 1 - i, 0, 0, 0)),
                                       pl.BlockSpec((1,) + GD_SOLVE, lambda i: (n - 1 - i, 0, 0, 0)),
                                       pl.BlockSpec((CHUNK, GD_VW), lambda i: (n - 1 - i, 0)),
                                       small(8, GD_QKV), small(1, HD), small(1, HD), small(1, HD)],
        out_specs=[pl.BlockSpec((CHUNK, GD_N), lambda i: (n - 1 - i, 0)),
                   small(8, GD_QKV), small(1, HD), small(1, HD), small(1, HD)],
        out_shape=[jax.ShapeDtypeStruct((t, GD_N), BF16), jax.ShapeDtypeStruct((8, GD_QKV), F32)]
        + [jax.ShapeDtypeStruct((1, HD), F32)] * 3,
        scratch_shapes=[pltpu.VMEM((GD_VH * HD, HD), F32), pltpu.VMEM((HALO, GD_QKV), F32)],
        compiler_params=_cparams(("arbitrary",)),
    )(u, u, sts, tinvs, xsols, dp, cw, alog, dtb, go)


SW_B = 128
SW_H = 16
SW_G = 4
SW_N = 2560
SW_KV0 = 1024


def _blockdiag(n, blk):
    r = lax.broadcasted_iota(jnp.int32, (n, n), 0) // blk
    c = lax.broadcasted_iota(jnp.int32, (n, n), 1) // blk
    return (r == c).astype(F32)


def _sw_normrope(x, g1, g2, cos, sin):
    w = x.shape[1] // 2
    x1, x2 = jnp.split(x, 2, axis=1)
    ms = _mm_high(x1 * x1 + x2 * x2, _blockdiag(w, 32)) * (1.0 / 64.0)
    rinv = lax.rsqrt(ms + EPS)
    n1, n2 = x1 * rinv * g1, x2 * rinv * g2
    return jnp.concatenate([n1 * cos - n2 * sin, n2 * cos + n1 * sin], axis=1)


def _sw_block(q, kvp, kvc, z, csp, csc, gq, gk, sinks, has_prev):
    b = q.shape[0]
    cos_c, sin_c = jnp.split(csc, 2, axis=1)
    cos_p, sin_p = jnp.split(csp, 2, axis=1)
    tile4 = lambda a: jnp.concatenate([a] * 4, axis=1)
    qh = _sw_normrope(q, gq[0:1], gq[1:2], tile4(cos_c), tile4(sin_c))
    kp, vp = jnp.split(kvp, 2, axis=1)
    kc, vc = jnp.split(kvc, 2, axis=1)
    kh = jnp.concatenate([_sw_normrope(kp, gk[0:1], gk[1:2], cos_p, sin_p),
                          _sw_normrope(kc, gk[0:1], gk[1:2], cos_c, sin_c)], axis=0)
    vv = jnp.concatenate([vp, vc], axis=0)
    q1, q2 = jnp.split(qh, 2, axis=1)
    q1g, q2g = jnp.split(q1, SW_G, axis=1), jnp.split(q2, SW_G, axis=1)
    own = lax.broadcasted_iota(jnp.int32, (4 * b, b), 1) <= lax.broadcasted_iota(jnp.int32, (4 * b, b), 0) % b
    ri = lax.broadcasted_iota(jnp.int32, (256, 256), 0)
    ci = lax.broadcasted_iota(jnp.int32, (256, 256), 1)
    row_head = lax.broadcasted_iota(jnp.int32, (4 * b, 256), 0) // b
    lane_q = lax.broadcasted_iota(jnp.int32, (4 * b, 256), 1)
    q_sel = (lane_q % 128) // 32 == row_head
    o_sel = lane_q // 64 == row_head
    o_out = []
    for g in range(SW_G):
        ek = ((ri // 128 == ci // 128) & ((ri % 128) // 32 == g) & (ri % 32 == ci % 32)).astype(F32)
        ev = ((ri // 64 == g) & (ri % 64 == ci % 64)).astype(F32)
        kx = _mm(kh, ek)
        vx = _mm(vv, ev)
        qg = jnp.concatenate([q1g[g], q2g[g]], axis=1)
        q4 = jnp.where(q_sel, jnp.concatenate([qg] * 4, axis=0), 0.0)
        sink = jnp.concatenate([jnp.broadcast_to(_lane_pick(sinks, 4 * g + j), (b, 1)) for j in range(4)], axis=0)
        s_prev, s_own = jnp.split(_mm_nt(q4, kx) * (64 ** -0.5), 2, axis=1)
        s = jnp.where(own, s_own, jnp.where(has_prev, s_prev, -jnp.inf))
        m = jnp.maximum(jnp.max(s, axis=1, keepdims=True), sink)
        p = jnp.exp(s - m)
        pn = p / (jnp.sum(p, axis=1, keepdims=True) + jnp.exp(sink - m))
        pn2 = jnp.concatenate([jnp.where(own, 0.0, pn), jnp.where(own, pn, 0.0)], axis=1)
        o4 = jnp.split(jnp.where(o_sel, _mm(pn2, vx), 0.0), 4, axis=0)
        o_out.append(o4[0] + o4[1] + o4[2] + o4[3])
    return jnp.concatenate(o_out, axis=1) * _silu(z)


def _sw_specs(n, rev):
    ci = (lambda i: n - 1 - i) if rev else (lambda i: i)
    prev = lambda i: jnp.maximum(ci(i) - 1, 0)
    return [pl.BlockSpec((SW_B, SW_N), lambda i: (ci(i), 0)),
            pl.BlockSpec((SW_B, 512), lambda i: (prev(i), SW_KV0 // 512)),
            pl.BlockSpec((SW_B, 256), lambda i: (ci(i), 0)),
            pl.BlockSpec((SW_B, 256), lambda i: (prev(i), 0)),
            pl.BlockSpec((2, 512), lambda i: (0, 0)), pl.BlockSpec((2, 128), lambda i: (0, 0)),
            pl.BlockSpec((1, 128), lambda i: (0, 0))]


def _sw_args(u_ref, kvp_ref, csc_ref, csp_ref, gq_ref, gk_ref, sk_ref, has_prev):
    return (u_ref[:, 0:D], kvp_ref[...], u_ref[:, SW_KV0:SW_KV0 + 512], u_ref[:, SW_KV0 + 512:SW_N],
            csp_ref[...], csc_ref[...], gq_ref[...], gk_ref[...], sk_ref[...], has_prev)


def _sw_fwd(u, cs, gq, gk, sinks):
    t = u.shape[0]
    n = t // SW_B

    def body(u_ref, kvp_ref, csc_ref, csp_ref, gq_ref, gk_ref, sk_ref, p_ref):
        has_prev = pl.program_id(0) > 0
        p_ref[...] = _sw_block(*_sw_args(u_ref, kvp_ref, csc_ref, csp_ref, gq_ref, gk_ref, sk_ref, has_prev)
                               ).astype(BF16)

    return pl.pallas_call(
        body, name="sw_fwd", grid=(n,), in_specs=_sw_specs(n, False),
        out_specs=pl.BlockSpec((SW_B, D), lambda i: (i, 0)),
        out_shape=jax.ShapeDtypeStruct((t, D), BF16),
        compiler_params=_cparams(("arbitrary",)),
    )(u, u, cs, cs, gq, gk, sinks)


def _sw_bwd(u, cs, dp, gq, gk, sinks):
    t = u.shape[0]
    n = t // SW_B

    def body(u_ref, kvp_ref, csc_ref, csp_ref, gq_ref, gk_ref, sk_ref, dp_ref,
             du_ref, dgq_ref, dgk_ref, dsk_ref, dkv_ref):
        i = pl.program_id(0)

        @pl.when(i == 0)
        def _():
            for r in (dkv_ref, dgq_ref, dgk_ref, dsk_ref):
                r[...] = jnp.zeros_like(r)

        has_prev = i < n - 1
        args = _sw_args(u_ref, kvp_ref, csc_ref, csp_ref, gq_ref, gk_ref, sk_ref, has_prev)
        fn = lambda q, kvp, kvc, z, gq_, gk_, sk_: _sw_block(q, kvp, kvc, z, args[4], args[5], gq_, gk_, sk_, has_prev)
        _, vjp = jax.vjp(fn, args[0], args[1], args[2], args[3], args[6], args[7], args[8])
        dq, dkvp, dkvc, dz, dgq, dgk, dsk = vjp(dp_ref[...].astype(F32))
        du_ref[:, 0:D] = dq.astype(BF16)
        du_ref[:, SW_KV0:SW_KV0 + 512] = (dkvc + dkv_ref[...]).astype(BF16)
        du_ref[:, SW_KV0 + 512:SW_N] = dz.astype(BF16)
        dkv_ref[...] = dkvp
        dgq_ref[...] += dgq
        dgk_ref[...] += dgk
        dsk_ref[...] += dsk

    small = lambda r, w: pl.BlockSpec((r, w), lambda i: (0, 0))
    return pl.pallas_call(
        body, name="sw_bwd", grid=(n,),
        in_specs=_sw_specs(n, True) + [pl.BlockSpec((SW_B, D), lambda i: (n - 1 - i, 0))],
        out_specs=[pl.BlockSpec((SW_B, SW_N), lambda i: (n - 1 - i, 0)), small(2, 512), small(2, 128), small(1, 128)],
        out_shape=[jax.ShapeDtypeStruct((t, SW_N), BF16), jax.ShapeDtypeStruct((2, 512), F32),
                   jax.ShapeDtypeStruct((2, 128), F32), jax.ShapeDtypeStruct((1, 128), F32)],
        scratch_shapes=[pltpu.VMEM((SW_B, 512), F32)],
        compiler_params=_cparams(("arbitrary",)),
    )(u, u, cs, cs, gq, gk, sinks, dp)


def _ln_mod(x, g, scale, shift):
    y = x * lax.rsqrt(jnp.mean(x * x, axis=1, keepdims=True) + EPS) * g
    return y * (1.0 + scale) + shift


def _row_tile(t):
    return min(t, 1024)


def _ln_mm(x, g, scale, shift, w, tn):
    t, n = x.shape[0], w.shape[1]
    tm = _row_tile(t)
    vec = pl.BlockSpec((1, D), lambda i, j: (0, 0))

    def body(x_ref, g_ref, sc_ref, sh_ref, w_ref, u_ref, h_ref):
        @pl.when(pl.program_id(1) == 0)
        def _():
            h_ref[...] = _ln_mod(x_ref[...], g_ref[...], sc_ref[...], sh_ref[...]).astype(BF16)

        u_ref[...] = _dot(h_ref[...], w_ref[...], 1, 0)

    return pl.pallas_call(
        body, name="ln_mm", grid=(t // tm, n // tn),
        in_specs=[pl.BlockSpec((tm, D), lambda i, j: (i, 0)), vec, vec, vec,
                  pl.BlockSpec((D, tn), lambda i, j: (0, j))],
        out_specs=[pl.BlockSpec((tm, tn), lambda i, j: (i, j)), pl.BlockSpec((tm, D), lambda i, j: (i, 0))],
        out_shape=[jax.ShapeDtypeStruct((t, n), F32), jax.ShapeDtypeStruct((t, D), BF16)],
        compiler_params=_cparams(("arbitrary", "arbitrary")),
    )(x, g, scale, shift, w)


def _mm_res(p, w, x, gate):
    t, k = p.shape
    tm = _row_tile(t)

    def body(p_ref, w_ref, x_ref, gate_ref, o_ref):
        o_ref[...] = x_ref[...] + gate_ref[...] * _dot(p_ref[...], w_ref[...], 1, 0)

    return pl.pallas_call(
        body, name="mm_res", grid=(t // tm,),
        in_specs=[pl.BlockSpec((tm, k), lambda i: (i, 0)), pl.BlockSpec((k, D), lambda i: (0, 0)),
                  pl.BlockSpec((tm, D), lambda i: (i, 0)), pl.BlockSpec((1, D), lambda i: (0, 0))],
        out_specs=pl.BlockSpec((tm, D), lambda i: (i, 0)),
        out_shape=jax.ShapeDtypeStruct((t, D), F32),
        compiler_params=_cparams(("arbitrary",)),
    )(p, w, x, gate)


def _loss_grad(x, target):
    t = x.shape[0]
    tm = _row_tile(t)

    def body(x_ref, t_ref, l_ref, dx_ref):
        @pl.when(pl.program_id(0) == 0)
        def _():
            l_ref[...] = jnp.zeros_like(l_ref)

        err = x_ref[...] - t_ref[...]
        dx_ref[...] = err * (1.0 / D)
        l_ref[...] += 0.5 * jnp.sum(jnp.mean(err * err, axis=1, keepdims=True), axis=0, keepdims=True)

    return pl.pallas_call(
        body, name="loss_grad", grid=(t // tm,),
        in_specs=[pl.BlockSpec((tm, D), lambda i: (i, 0))] * 2,
        out_specs=[pl.BlockSpec((8, 128), lambda i: (0, 0)), pl.BlockSpec((tm, D), lambda i: (i, 0))],
        out_shape=[jax.ShapeDtypeStruct((8, 128), F32), jax.ShapeDtypeStruct((t, D), F32)],
        compiler_params=_cparams(("arbitrary",)),
    )(x, target)


def _mm_scaled(a, s, w, tn):
    t, k = a.shape
    n = w.shape[1]
    tm = _row_tile(t)

    def body(a_ref, s_ref, w_ref, o_ref):
        o_ref[...] = _dot((a_ref[...] * s_ref[...]).astype(BF16), w_ref[...], 1, 0).astype(BF16)

    return pl.pallas_call(
        body, name="mm_scaled", grid=(t // tm, n // tn),
        in_specs=[pl.BlockSpec((tm, k), lambda i, j: (i, 0)), pl.BlockSpec((1, k), lambda i, j: (0, 0)),
                  pl.BlockSpec((k, tn), lambda i, j: (0, j))],
        out_specs=pl.BlockSpec((tm, tn), lambda i, j: (i, j)),
        out_shape=jax.ShapeDtypeStruct((t, n), BF16),
        compiler_params=_cparams(("arbitrary", "arbitrary")),
    )(a, s, w)


def _mm_tn_acc(a, b, tn):
    t, m = a.shape
    n = b.shape[1]
    fits = lambda k: 2 * k * (m * a.dtype.itemsize + tn * b.dtype.itemsize) + 2 * m * tn * 4 <= 36 * 1024 * 1024
    tk = next(k for k in (4096, 2048, 1024, 512, t) if t % k == 0 and (fits(k) or k <= 512))
    nk = t // tk

    def body(a_ref, b_ref, o_ref):
        @pl.when(pl.program_id(1) == 0)
        def _():
            o_ref[...] = jnp.zeros_like(o_ref)

        o_ref[...] += _dot(a_ref[...], b_ref[...].astype(BF16), 0, 0)

    return pl.pallas_call(
        body, name="mm_tn_acc", grid=(n // tn, nk),
        in_specs=[pl.BlockSpec((tk, m), lambda j, k: (k, 0)), pl.BlockSpec((tk, tn), lambda j, k: (k, j))],
        out_specs=pl.BlockSpec((m, tn), lambda j, k: (0, j)),
        out_shape=jax.ShapeDtypeStruct((m, n), F32),
        compiler_params=_cparams(("arbitrary", "arbitrary")),
    )(a, b)


def _inproj_bwd(du, wt, x, dxp, g, scale, shift):
    t, kdim = du.shape
    tk = kdim
    tm = min(t, 512 if kdim <= 4096 else 256)
    nk = kdim // tk
    vec = pl.BlockSpec((1, D), lambda i, k: (0, 0))

    def body(du_ref, wt_ref, x_ref, dxp_ref, g_ref, sc_ref, sh_ref, dx_ref, dv_ref, acc_ref):
        k = pl.program_id(1)

        @pl.when((pl.program_id(0) == 0) & (k == 0))
        def _():
            dv_ref[...] = jnp.zeros_like(dv_ref)

        @pl.when(k == 0)
        def _():
            acc_ref[...] = jnp.zeros_like(acc_ref)

        acc_ref[...] += _dot(du_ref[...].astype(BF16), wt_ref[...], 1, 0)

        @pl.when(k == nk - 1)
        def _():
            _, vjp = jax.vjp(_ln_mod, x_ref[...], g_ref[...], sc_ref[...], sh_ref[...])
            dx, dg, dsc, dsh = vjp(acc_ref[...])
            dx_ref[...] = dxp_ref[...] + dx
            dv_ref[0:1, :] += dg
            dv_ref[1:2, :] += dsc
            dv_ref[2:3, :] += dsh

    return pl.pallas_call(
        body, name="inproj_bwd", grid=(t // tm, nk),
        in_specs=[pl.BlockSpec((tm, tk), lambda i, k: (i, k)), pl.BlockSpec((tk, D), lambda i, k: (k, 0)),
                  pl.BlockSpec((tm, D), lambda i, k: (i, 0)), pl.BlockSpec((tm, D), lambda i, k: (i, 0)),
                  vec, vec, vec],
        out_specs=[pl.BlockSpec((tm, D), lambda i, k: (i, 0)), pl.BlockSpec((8, D), lambda i, k: (0, 0))],
        out_shape=[jax.ShapeDtypeStruct((t, D), F32), jax.ShapeDtypeStruct((8, D), F32)],
        scratch_shapes=[pltpu.VMEM((tm, D), F32)],
        compiler_params=_cparams(("arbitrary", "arbitrary")),
    )(du, wt, x, dxp, g, scale, shift)


def _outgrad(gmat, w, gate):
    k = gmat.shape[0]
    tr = 256

    def body(g_ref, w_ref, gate_ref, dw_ref, dg_ref):
        @pl.when(pl.program_id(0) == 0)
        def _():
            dg_ref[...] = jnp.zeros_like(dg_ref)

        gm = g_ref[...]
        dw_ref[...] = gm * gate_ref[...]
        dg_ref[0:1, :] += jnp.sum(gm * w_ref[...].astype(F32), axis=0, keepdims=True)

    return pl.pallas_call(
        body, name="outgrad", grid=(k // tr,),
        in_specs=[pl.BlockSpec((tr, D), lambda i: (i, 0)), pl.BlockSpec((tr, D), lambda i: (i, 0)),
                  pl.BlockSpec((1, D), lambda i: (0, 0))],
        out_specs=[pl.BlockSpec((tr, D), lambda i: (i, 0)), pl.BlockSpec((8, D), lambda i: (0, 0))],
        out_shape=[jax.ShapeDtypeStruct((k, D), F32), jax.ShapeDtypeStruct((8, D), F32)],
        compiler_params=_cparams(("arbitrary",)),
    )(gmat, w, gate)


def _rope_table(pos, freq):
    t = pos.shape[0]
    tm = _row_tile(t)

    def body(p_ref, f_ref, o_ref):
        ang = p_ref[...].astype(F32) * f_ref[...]
        o_ref[:, 0:128] = jnp.cos(ang)
        o_ref[:, 128:256] = jnp.sin(ang)

    return pl.pallas_call(
        body, name="rope_table", grid=(t // tm,),
        in_specs=[pl.BlockSpec((tm, 1), lambda i: (i, 0)), pl.BlockSpec((1, 128), lambda i: (0, 0))],
        out_specs=pl.BlockSpec((tm, 256), lambda i: (i, 0)),
        out_shape=jax.ShapeDtypeStruct((t, 256), F32),
        compiler_params=_cparams(("arbitrary",)),
    )(pos, freq)


def _ada_fwd(c_all, w, b):
    nl, _, s = w.shape

    def body(c_ref, w_ref, b_ref, o_ref):
        o_ref[0] = _mm_f32(c_ref[...], w_ref[0]) + b_ref[0]

    return pl.pallas_call(
        body, name="ada_fwd", grid=(nl,),
        in_specs=[pl.BlockSpec((8, D), lambda l: (0, 0)), pl.BlockSpec((1, D, s), lambda l: (l, 0, 0)),
                  pl.BlockSpec((1, 1, s), lambda l: (l, 0, 0))],
        out_specs=pl.BlockSpec((1, 8, s), lambda l: (l, 0, 0)),
        out_shape=jax.ShapeDtypeStruct((nl, 8, s), F32),
        compiler_params=_cparams(("arbitrary",)),
    )(c_all, w, b)


def _ada_bwd(c_all, dmod_cols, dmod_all):
    nl, _, s = dmod_cols.shape

    def body(c_ref, dc_ref, da_ref, gw_ref, gb_ref):
        gw_ref[0] = _dot(c_ref[...], dc_ref[0], 0, 0, lax.Precision.HIGHEST)
        gb_ref[0] = jnp.sum(da_ref[0], axis=0, keepdims=True)

    return pl.pallas_call(
        body, name="ada_bwd", grid=(nl,),
        in_specs=[pl.BlockSpec((8, D), lambda l: (0, 0)), pl.BlockSpec((1, 8, s), lambda l: (l, 0, 0)),
                  pl.BlockSpec((1, 8, 3 * D), lambda l: (l, 0, 0))],
        out_specs=[pl.BlockSpec((1, D, s), lambda l: (l, 0, 0)), pl.BlockSpec((1, 1, 3 * D), lambda l: (l, 0, 0))],
        out_shape=[jax.ShapeDtypeStruct((nl, D, s), F32), jax.ShapeDtypeStruct((nl, 1, 3 * D), F32)],
        compiler_params=_cparams(("arbitrary",)),
    )(c_all, dmod_cols, dmod_all)


def _lb_fn(h8):
    sm = jax.nn.softmax(h8, axis=0)
    r = lax.broadcasted_iota(jnp.int32, (8, 8), 0)
    c = lax.broadcasted_iota(jnp.int32, (8, 8), 1)
    return _mm_f32(((c >= 1) & (c <= r)).astype(F32), sm)


def _lb_fwd(h8):
    def body(h_ref, o_ref):
        o_ref[...] = _lb_fn(h_ref[...])

    return pl.pallas_call(body, name="lb_fwd", out_shape=jax.ShapeDtypeStruct((8, D), F32))(h8)


def _lb_bwd(h8, dlb8):
    def body(h_ref, d_ref, o_ref):
        _, vjp = jax.vjp(_lb_fn, h_ref[...])
        o_ref[...] = vjp(d_ref[...])[0]

    return pl.pallas_call(body, name="lb_bwd", out_shape=jax.ShapeDtypeStruct((8, D), F32))(h8, dlb8)


ADAM_LR, ADAM_B1, ADAM_B2, ADAM_EPS, ADAM_WD, ADAM_STEP = 0.001, 0.9, 0.999, 1e-08, 0.01, 10


def _adamw(w, gparts, m, v):
    r, c = w.shape
    tr = r if r * c * 4 <= (1 << 20) else max(8, ((1 << 20) // (c * 4)) // 8 * 8)
    while r % tr:
        tr -= 8
    ng = len(gparts)

    def body(*refs):
        w_ref, m_ref, v_ref = refs[0], refs[1 + ng], refs[2 + ng]
        g_ref, d_ref, nm_ref, nv_ref = refs[3 + ng:]
        g = refs[1][...]
        for gr in refs[2:1 + ng]:
            g = g + gr[...]
        mm = ADAM_B1 * m_ref[...] + (1.0 - ADAM_B1) * g
        vv = ADAM_B2 * v_ref[...] + (1.0 - ADAM_B2) * (g * g)
        m_hat = mm / (1.0 - ADAM_B1 ** ADAM_STEP)
        v_hat = vv / (1.0 - ADAM_B2 ** ADAM_STEP)
        g_ref[...] = g
        d_ref[...] = -ADAM_LR * (m_hat / (jnp.sqrt(v_hat) + ADAM_EPS) + ADAM_WD * w_ref[...])
        nm_ref[...] = mm
        nv_ref[...] = vv

    spec = pl.BlockSpec((tr, c), lambda i: (i, 0))
    return pl.pallas_call(
        body, name="adamw", grid=(r // tr,), in_specs=[spec] * (3 + ng), out_specs=[spec] * 4,
        out_shape=[jax.ShapeDtypeStruct((r, c), F32)] * 4,
        compiler_params=_cparams(("arbitrary",)),
    )(w, *gparts, m, v)


def _sum_rows(parts):
    r, c = parts[0].shape
    tr = 8
    for cand in range(min(r, 512), 7, -8):
        if r % cand == 0:
            tr = cand
            break

    def body(*refs):
        acc = refs[0][...]
        for p in refs[1:-1]:
            acc = acc + p[...]
        refs[-1][...] = acc

    spec = pl.BlockSpec((tr, c), lambda i: (i, 0))
    return pl.pallas_call(
        body, name="sum_rows", grid=(r // tr,), in_specs=[spec] * len(parts), out_specs=spec,
        out_shape=jax.ShapeDtypeStruct((r, c), F32),
        compiler_params=_cparams(("arbitrary",)),
    )(*parts)


MESH = pl.DeviceIdType.MESH
ANY = pl.BlockSpec(memory_space=pl.ANY)


def _place():
    return lax.axis_index("x"), lax.axis_index("y"), lax.axis_index("c")


def _allgather8(blk):
    m_per, n = blk.shape

    def body(x_ref, out_ref, send_sems, recv_sems, local_sem):
        x, y, c = _place()
        me, sibling = (x, y, c), (x, y, 1 - c)
        chips = [(1 - x, y), (x, 1 - y), (1 - x, 1 - y)]

        def rows(px, py, pc):
            return out_ref.at[pl.ds((4 * px + 2 * py + pc) * m_per, m_per), :]

        def copy(k, block, to, src=None):
            return pltpu.make_async_remote_copy(
                src_ref=rows(*block) if src is None else src, dst_ref=rows(*block),
                send_sem=send_sems.at[k], recv_sem=recv_sems.at[k], device_id=to, device_id_type=MESH)

        mine = pltpu.make_async_copy(x_ref, rows(*me), local_sem)
        mine.start()
        first = [copy(0, me, sibling, src=x_ref)]
        first += [copy(1 + j, me, (*chip, c), src=x_ref) for j, chip in enumerate(chips)]
        for cp in first:
            cp.start()
        passed = [copy(4 + j, (*chip, c), sibling) for j, chip in enumerate(chips)]
        for j, chip in enumerate(chips):
            copy(1 + j, (*chip, c), me).wait_recv()
            passed[j].start()
        copy(0, sibling, me).wait_recv()
        for j, chip in enumerate(chips):
            copy(4 + j, (*chip, 1 - c), me).wait_recv()
        for cp in first + passed:
            cp.wait_send()
        mine.wait()

    return pl.pallas_call(
        body, name="allgather8",
        out_shape=jax.ShapeDtypeStruct((8 * m_per, n), blk.dtype),
        in_specs=[pl.BlockSpec(memory_space=pltpu.VMEM)],
        out_specs=pl.BlockSpec(memory_space=pltpu.VMEM),
        scratch_shapes=[pltpu.SemaphoreType.DMA((7,)), pltpu.SemaphoreType.DMA((7,)), pltpu.SemaphoreType.DMA],
    )(blk)


def _chip_peers():
    x, y, c = _place()
    return [(1 - x, y, c), (x, 1 - y, c), (1 - x, 1 - y, c)]


GATHER_SEMS = [pltpu.SemaphoreType.DMA((3,)), pltpu.SemaphoreType.DMA((3,)), pltpu.SemaphoreType.DMA]
SCATTER_SEMS = [pltpu.SemaphoreType.DMA((3,)), pltpu.SemaphoreType.DMA((3,))]


def _gather_plan(x_ref, out_ref, send_sems, recv_sems, local_sem):
    x, y, _ = _place()
    peers = _chip_peers()

    def copy(j, chip_index):
        return pltpu.make_async_remote_copy(
            src_ref=x_ref, dst_ref=out_ref.at[chip_index], send_sem=send_sems.at[j], recv_sem=recv_sems.at[j],
            device_id=peers[j], device_id_type=MESH)

    mine = pltpu.make_async_copy(x_ref, out_ref.at[2 * x + y], local_sem)
    sends = [copy(j, 2 * x + y) for j in range(3)]

    def start():
        mine.start()
        for cp in sends:
            cp.start()

    def wait():
        for j in range(3):
            copy(j, 2 * peers[j][0] + peers[j][1]).wait_recv()
        for cp in sends:
            cp.wait_send()
        mine.wait()

    return start, wait


def _scatter_plan(p_ref, out_ref, send_sems, recv_sems):
    peers = _chip_peers()
    sends = [pltpu.make_async_remote_copy(
        src_ref=p_ref.at[2 * peers[j][0] + peers[j][1]], dst_ref=out_ref.at[j], send_sem=send_sems.at[j],
        recv_sem=recv_sems.at[j], device_id=peers[j], device_id_type=MESH) for j in range(3)]

    def start():
        for cp in sends:
            cp.start()

    def wait():
        for cp in sends:
            cp.wait_recv()
        for cp in sends:
            cp.wait_send()

    return start, wait


def _chip_allgather(shard):
    def body(x_ref, out_ref, *sems):
        start, wait = _gather_plan(x_ref, out_ref, *sems)
        start()
        wait()

    return pl.pallas_call(
        body, name="chip_allgather", out_shape=jax.ShapeDtypeStruct((4,) + shard.shape, shard.dtype),
        in_specs=[ANY], out_specs=ANY, scratch_shapes=GATHER_SEMS,
    )(shard)


def _chip_scatter(parts):
    def body(p_ref, out_ref, *sems):
        start, wait = _scatter_plan(p_ref, out_ref, *sems)
        start()
        wait()

    return pl.pallas_call(
        body, name="chip_scatter", out_shape=jax.ShapeDtypeStruct((3,) + parts.shape[1:], parts.dtype),
        in_specs=[ANY], out_specs=ANY, scratch_shapes=SCATTER_SEMS,
    )(parts)


def _sibling_swap(a):
    def body(a_ref, out_ref, send_sem, recv_sem):
        x, y, c = _place()
        cp = pltpu.make_async_remote_copy(src_ref=a_ref, dst_ref=out_ref, send_sem=send_sem, recv_sem=recv_sem,
                                          device_id=(x, y, 1 - c), device_id_type=MESH)
        cp.start()
        cp.wait_recv()
        cp.wait_send()

    return pl.pallas_call(
        body, name="sibling_swap", out_shape=jax.ShapeDtypeStruct(a.shape, a.dtype),
        in_specs=[ANY], out_specs=ANY,
        scratch_shapes=[pltpu.SemaphoreType.DMA, pltpu.SemaphoreType.DMA],
    )(a)


WEIGHTS = ['hgrn_lb', 'ada_w', 'ada_b', 'norm_g', 'hg_in_w', 'hg_out_w', 'hg_onorm', 'sw_in_w', 'sw_out_w', 'sw_qnorm',
           'sw_knorm', 'sw_sinks', 'gd_in_w', 'gd_out_w', 'gd_conv_w', 'gd_a_log', 'gd_dt_bias', 'gd_onorm']
BIG = ['hg_in_w', 'hg_out_w', 'sw_in_w', 'sw_out_w', 'gd_in_w', 'gd_out_w']
SEG_FIRST = [('hg_in_w', 0), ('hg_out_w', 0)]
SEG_REST = [('hg_in_w', 1), ('hg_out_w', 1), ('sw_in_w', 0), ('sw_out_w', 0), ('gd_in_w', 0), ('gd_out_w', 0)]
PACK_ALIGN = 16
ROPE_THETA = 10000.0
ADA_S = 3 * D // 4
SMALL_ROW = {'hg_onorm': (0, 256), 'sw_qnorm': (256, 64), 'sw_knorm': (320, 64), 'sw_sinks': (384, 16),
             'gd_a_log': (400, 16), 'gd_dt_bias': (416, 16), 'gd_onorm': (432, 128)}


def _pack_rows(arrs):
    flat = jnp.concatenate([a.reshape(-1, D) for a in arrs], axis=0)
    return jnp.pad(flat, ((0, -flat.shape[0] % PACK_ALIGN), (0, 0)))


def _unpack_rows(packed, shapes):
    out, off = [], 0
    for s in shapes:
        rows = 1
        for d in s:
            rows *= d
        rows //= D
        out.append(packed[..., off:off + rows, :].reshape(packed.shape[:-2] + tuple(s)))
        off += rows
    return out


def _pack_small(vals):
    row = jnp.concatenate([vals[k].reshape(-1) for k in SMALL_ROW])
    row = jnp.pad(row, (0, D - row.shape[0]))[None]
    return jnp.concatenate([vals['hgrn_lb'], vals['norm_g'], vals['gd_conv_w'].reshape(16, D), row,
                            jnp.zeros((7, D), F32)], axis=0)


def _sw_cols(w, inverse=False):
    def split(a, heads):
        shp = (a.shape[0], 2, heads, 32) if inverse else (a.shape[0], heads, 2, 32)
        return a.reshape(shp).transpose(0, 2, 1, 3).reshape(a.shape[0], heads * 64)
    return jnp.concatenate([split(w[:, 0:1024], 16), split(w[:, 1024:1280], 4), w[:, 1280:]], axis=1)


def kernel(x, c, positions, hgrn_lb, ada_w, ada_b, norm_g, hg_in_w, hg_out_w, hg_onorm, sw_in_w, sw_out_w, sw_qnorm, sw_knorm, sw_sinks, gd_in_w, gd_out_w, gd_conv_w, gd_a_log, gd_dt_bias, gd_onorm, loss_target, m_hgrn_lb, m_ada_w, m_ada_b, m_norm_g, m_hg_in_w, m_hg_out_w, m_hg_onorm, m_sw_in_w, m_sw_out_w, m_sw_qnorm, m_sw_knorm, m_sw_sinks, m_gd_in_w, m_gd_out_w, m_gd_conv_w, m_gd_a_log, m_gd_dt_bias, m_gd_onorm, v_hgrn_lb, v_ada_w, v_ada_b, v_norm_g, v_hg_in_w, v_hg_out_w, v_hg_onorm, v_sw_in_w, v_sw_out_w, v_sw_qnorm, v_sw_knorm, v_sw_sinks, v_gd_in_w, v_gd_out_w, v_gd_conv_w, v_gd_a_log, v_gd_dt_bias, v_gd_onorm):
    w_in = dict(hgrn_lb=hgrn_lb, ada_w=ada_w, ada_b=ada_b, norm_g=norm_g, hg_in_w=hg_in_w, hg_out_w=hg_out_w,
                hg_onorm=hg_onorm, sw_in_w=sw_in_w, sw_out_w=sw_out_w, sw_qnorm=sw_qnorm, sw_knorm=sw_knorm,
                sw_sinks=sw_sinks, gd_in_w=gd_in_w, gd_out_w=gd_out_w, gd_conv_w=gd_conv_w, gd_a_log=gd_a_log,
                gd_dt_bias=gd_dt_bias, gd_onorm=gd_onorm)
    m_in = dict(zip(WEIGHTS, (m_hgrn_lb, m_ada_w, m_ada_b, m_norm_g, m_hg_in_w, m_hg_out_w, m_hg_onorm, m_sw_in_w,
                              m_sw_out_w, m_sw_qnorm, m_sw_knorm, m_sw_sinks, m_gd_in_w, m_gd_out_w, m_gd_conv_w,
                              m_gd_a_log, m_gd_dt_bias, m_gd_onorm)))
    v_in = dict(zip(WEIGHTS, (v_hgrn_lb, v_ada_w, v_ada_b, v_norm_g, v_hg_in_w, v_hg_out_w, v_hg_onorm, v_sw_in_w,
                              v_sw_out_w, v_sw_qnorm, v_sw_knorm, v_sw_sinks, v_gd_in_w, v_gd_out_w, v_gd_conv_w,
                              v_gd_a_log, v_gd_dt_bias, v_gd_onorm)))
    ax, ay, ac = _place()
    chip = 2 * ax + ay
    bidx = 4 * ax + 2 * ay + ac
    t = x.shape[1]
    x0, target = x[0], loss_target[0]

    c_all = _allgather8(jnp.pad(c, ((0, 7), (0, 0)))).reshape(8, 8, D)[:, 0, :]
    ada_b_cols = lax.dynamic_slice(ada_b, (0, chip * ADA_S), (4, ADA_S)).reshape(4, 1, ADA_S)
    mod_sh = _ada_fwd(c_all, ada_w, ada_b_cols)
    mod_g = _allgather8(mod_sh.reshape(32, ADA_S)).reshape(4, 2, 4, 8, ADA_S)[:, 0]
    mod = lax.dynamic_index_in_dim(mod_g, bidx, axis=2, keepdims=False).transpose(1, 0, 2).reshape(4, 3 * D)
    shift = [mod[l:l + 1, 0:D] for l in range(4)]
    scale = [mod[l:l + 1, D:2 * D] for l in range(4)]
    gate = [mod[l:l + 1, 2 * D:3 * D] for l in range(4)]

    h8 = jnp.concatenate([hgrn_lb, jnp.full((4, D), -1e30, F32)], axis=0)
    lb_all = _lb_fwd(h8)
    freq = ROPE_THETA ** (-jnp.arange(0, 64, 2, dtype=F32) / 64)
    cs = _rope_table(positions.reshape(t, 1), jnp.tile(freq, 4)[None])

    seg_shapes = lambda seg: [w_in[k].shape[1:] for k, _ in seg]
    pack_seg = lambda src, seg: _pack_rows([src[k][i] for k, i in seg])
    cols_full = lambda a: a.transpose(1, 0, 2).reshape(a.shape[1], 4 * a.shape[2])
    hg_in0_k, hg_out0_k = _unpack_rows(_chip_allgather(pack_seg(w_in, SEG_FIRST).astype(BF16)), seg_shapes(SEG_FIRST))
    win, wout = [cols_full(hg_in0_k)], [hg_out0_k.reshape(D, D)]
    rest_shard = pack_seg(w_in, SEG_REST).astype(BF16)
    tn_in = [1024, 1280, 896, 1024]

    gq = jnp.stack([jnp.tile(sw_qnorm[0, :32], 16), jnp.tile(sw_qnorm[0, 32:], 16)])
    gk = jnp.stack([jnp.tile(sw_knorm[0, :32], 4), jnp.tile(sw_knorm[0, 32:], 4)])
    pad128 = lambda a: jnp.pad(a, ((0, 0), (0, HD - a.shape[1])))
    sinks, alog, dtb = pad128(sw_sinks), pad128(gd_a_log), pad128(gd_dt_bias)
    cw8 = jnp.pad(_chip_allgather(gd_conv_w[0]).transpose(1, 0, 2).reshape(4, GD_QKV), ((0, 4), (0, 0)))
    lbs = {0: lb_all[0:1], 3: lb_all[3:4]}

    xs, us, hs, ps, stss = [x0], [], [], [], []
    for l in range(4):
        u, h = _ln_mm(xs[l], norm_g[l:l + 1], scale[l], shift[l], win[l], tn_in[l])
        if l == 0:
            p, sts, rest_k = _hg_fwd(u, lbs[l], hg_onorm[0:1], gather=rest_shard)
            hg_in1_k, hg_out1_k, sw_in_k, sw_out_k, gd_in_k, gd_out_k = _unpack_rows(rest_k, seg_shapes(SEG_REST))
            win += [_sw_cols(cols_full(sw_in_k)), jnp.pad(cols_full(gd_in_k), ((0, 0), (0, GD_N - 6176))),
                    cols_full(hg_in1_k)]
            wout += [sw_out_k.reshape(D, D), gd_out_k.reshape(GD_VW, D), hg_out1_k.reshape(D, D)]
        elif l % 3 == 0:
            p, sts = _hg_fwd(u, lbs[l], hg_onorm[l // 3:l // 3 + 1])
        elif l % 3 == 1:
            p, sts = _sw_fwd(u, cs, gq, gk, sinks), None
        else:
            p, *sts = _gd_fwd(u, cw8, alog, dtb, gd_onorm)
        xs.append(_mm_res(p, wout[l], xs[l], gate[l]))
        us.append(u), hs.append(h), ps.append(p), stss.append(sts)
    lpart, dx = _loss_grad(xs[4], target)
    loss = lax.psum(lpart[0, 0], ("x", "y", "c"))

    by_chip = lambda g, cols: g.reshape(g.shape[0], 4, cols).transpose(1, 0, 2)
    g_small = {}
    d_in, d_out, dmod, dnorm_g, dlb8, dgo_hg = [None] * 4, [None] * 4, [None] * 4, [None] * 4, jnp.zeros((8, D), F32), {}
    for l in (3, 2, 1, 0):
        dp = _mm_scaled(dx, gate[l], wout[l].T, 1024)
        d_out[l], dgate = _outgrad(_mm_tn_acc(ps[l], dx, D if ps[l].shape[1] == D else 512), wout[l], gate[l])
        if l == 0:
            rest_parts = {('hg_in_w', 1): by_chip(d_in[3], D), ('hg_out_w', 1): d_out[3].reshape(4, D // 4, D),
                          ('sw_in_w', 0): by_chip(_sw_cols(d_in[1], inverse=True), SW_N // 4),
                          ('sw_out_w', 0): d_out[1].reshape(4, D // 4, D),
                          ('gd_in_w', 0): by_chip(d_in[2][:, :6176], 1544),
                          ('gd_out_w', 0): d_out[2].reshape(4, GD_VW // 4, D)}
            rest_packed = jnp.stack([_pack_rows([rest_parts[s][j] for s in SEG_REST]) for j in range(4)])
            du, dlb, dgo_hg[0], rest_recv = _hg_bwd(us[l], stss[l], dp, lbs[l], hg_onorm[0:1],
                                                    scatter=rest_packed.astype(BF16))
            dlb8 = lax.dynamic_update_slice(dlb8, dlb, (l, 0))
        elif l % 3 == 0:
            du, dlb, dgo_hg[l // 3] = _hg_bwd(us[l], stss[l], dp, lbs[l], hg_onorm[l // 3:l // 3 + 1])
            dlb8 = lax.dynamic_update_slice(dlb8, dlb, (l, 0))
        elif l % 3 == 1:
            du, dgq, dgk, dsk = _sw_bwd(us[l], cs, dp, gq, gk, sinks)
            g_small['sw_qnorm'] = jnp.concatenate([dgq[0].reshape(16, 32).sum(0), dgq[1].reshape(16, 32).sum(0)])
            g_small['sw_knorm'] = jnp.concatenate([dgk[0].reshape(4, 32).sum(0), dgk[1].reshape(4, 32).sum(0)])
            g_small['sw_sinks'] = dsk[0, :16]
        else:
            du, dcw, dalog, ddtb, g_small['gd_onorm'] = _gd_bwd(us[l], *stss[l], dp, cw8, alog, dtb, gd_onorm)
            g_small['gd_conv_w'], g_small['gd_a_log'], g_small['gd_dt_bias'] = dcw[:4], dalog[0, :16], ddtb[0, :16]
        d_in[l] = _mm_tn_acc(hs[l], du, 896 if l == 2 else 512)
        dx, dvec = _inproj_bwd(du, win[l].T, xs[l], dx, norm_g[l:l + 1], scale[l], shift[l])
        dnorm_g[l] = dvec[0:1]
        dmod[l] = jnp.concatenate([dvec[2:3], dvec[1:2], dgate[0:1]], axis=1)
    grad_x = dx[None]

    g_small['hgrn_lb'] = _lb_bwd(h8, dlb8)[0:4]
    g_small['norm_g'] = jnp.concatenate(dnorm_g, axis=0)
    g_small['hg_onorm'] = jnp.concatenate([dgo_hg[0], dgo_hg[1]], axis=0)
    gs_all = _allgather8(_pack_small(g_small))
    gs = _sum_rows([gs_all[32 * d:32 * (d + 1)] for d in range(8)])

    def small_view(packed, k):
        if k == 'hgrn_lb':
            return packed[0:4]
        if k == 'norm_g':
            return packed[4:8]
        off, size = SMALL_ROW[k]
        return packed[24, off:off + size].reshape(w_in[k].shape)

    conv_sl = lambda full: lax.dynamic_slice(full.reshape(4, GD_QKV), (0, chip * D), (4, D))
    out = {}

    def put(k, res, shape):
        for name, r in zip(('grad_', 'delta_', 'new_m_', 'new_v_'), res):
            out[name + k] = r.reshape(shape)

    zero_conv = dict(gd_conv_w=jnp.zeros((4, GD_QKV), F32))
    small_names = ['hgrn_lb', 'norm_g'] + list(SMALL_ROW)
    res = _adamw(_pack_small({**{k: w_in[k] for k in small_names}, **zero_conv}), (gs,),
                 _pack_small({**{k: m_in[k] for k in small_names}, **zero_conv}),
                 _pack_small({**{k: v_in[k] for k in small_names}, **zero_conv}))
    for k in small_names:
        put(k, [small_view(r, k) for r in res], w_in[k].shape)
    put('gd_conv_w', _adamw(gd_conv_w[0], (conv_sl(gs[8:24]),), m_in['gd_conv_w'][0], v_in['gd_conv_w'][0]),
        gd_conv_w.shape)

    dm = _allgather8(jnp.pad(jnp.concatenate(dmod, axis=0), ((0, 4), (0, 0)))).reshape(8, 8, 3 * D)[:, :4]
    dm = dm.transpose(1, 0, 2)
    g_ada_w, g_ada_b = _ada_bwd(c_all, lax.dynamic_slice(dm, (0, 0, chip * ADA_S), (4, 8, ADA_S)), dm)
    put('ada_w', _adamw(ada_w.reshape(4 * D, ADA_S), (g_ada_w.reshape(4 * D, ADA_S),),
                        m_in['ada_w'].reshape(4 * D, ADA_S), v_in['ada_w'].reshape(4 * D, ADA_S)), ada_w.shape)
    put('ada_b', _adamw(ada_b, (g_ada_b.reshape(4, 3 * D),), m_in['ada_b'], v_in['ada_b']), ada_b.shape)

    first_parts = {('hg_in_w', 0): by_chip(d_in[0], D), ('hg_out_w', 0): d_out[0].reshape(4, D // 4, D)}
    first_packed = jnp.stack([_pack_rows([first_parts[s][j] for s in SEG_FIRST]) for j in range(4)])
    first_recv = _chip_scatter(first_packed.astype(BF16))
    own = lambda packed: lax.dynamic_index_in_dim(packed, chip, axis=0, keepdims=False)
    half = jnp.concatenate([_sum_rows([own(first_packed), first_recv[0], first_recv[1], first_recv[2]]),
                            _sum_rows([own(rest_packed), rest_recv[0], rest_recv[1], rest_recv[2]])], axis=0)
    other = _sibling_swap(half)
    pack_all = lambda src: jnp.concatenate([pack_seg(src, SEG_FIRST), pack_seg(src, SEG_REST)], axis=0)
    res = _adamw(pack_all(w_in), (half, other), pack_all(m_in), pack_all(v_in))
    n_first = first_packed.shape[1]
    for name, r in zip(('grad_', 'delta_', 'new_m_', 'new_v_'), res):
        pieces = dict(zip(SEG_FIRST, _unpack_rows(r[:n_first], seg_shapes(SEG_FIRST))))
        pieces.update(zip(SEG_REST, _unpack_rows(r[n_first:], seg_shapes(SEG_REST))))
        for k in BIG:
            out[name + k] = jnp.stack([pieces[(k, i)] for i in range(w_in[k].shape[0])])

    return (loss, grad_x, *[out[p + k] for p in ('grad_', 'delta_', 'new_m_', 'new_v_') for k in WEIGHTS])
```

```python
import functools

import jax
import jax.numpy as jnp
from jax import lax
from jax.experimental import pallas as pl
from jax.experimental.pallas import tpu as pltpu

F32 = jnp.float32
BF16 = jnp.bfloat16
D = 1024
EPS = 1e-6
CHUNK = 64
SUB = 32
HG_H = 8
HD = 128
VMEM_LIMIT = 56 * 1024 * 1024


def _cparams(sem=None):
    return pltpu.CompilerParams(dimension_semantics=sem, vmem_limit_bytes=VMEM_LIMIT)


def _dot(a, b, ca, cb, prec=None):
    return lax.dot_general(a, b, (((ca,), (cb,)), ((), ())), precision=prec, preferred_element_type=F32)


def _mm(a, b):
    return _dot(a.astype(BF16), b.astype(BF16), 1, 0)


def _mm_nt(a, b):
    return _dot(a.astype(BF16), b.astype(BF16), 1, 1)


def _mm_tn(a, b):
    return _dot(a.astype(BF16), b.astype(BF16), 0, 0)


def _mm_f32(a, b):
    return _dot(a, b, 1, 0, lax.Precision.HIGHEST)


def _silu(x):
    return x * jax.nn.sigmoid(x)


def _cumsum_impl(x):
    row = lax.broadcasted_iota(jnp.int32, x.shape, 0)
    s = 1
    while s < x.shape[0]:
        x = x + jnp.where(row >= s, pltpu.roll(x, s, 0), 0.0)
        s *= 2
    return x


@jax.custom_vjp
def _cumsum_rows(x):
    return _cumsum_impl(x)


_cumsum_rows.defvjp(lambda x: (_cumsum_impl(x), None),
                    lambda _, g: (jnp.sum(g, axis=0, keepdims=True) - _cumsum_impl(g) + g,))


def _roll_rows(x, shift):
    n = x.shape[0]

    @jax.custom_vjp
    def f(a):
        return pltpu.roll(a, shift, 0)

    f.defvjp(lambda a: (pltpu.roll(a, shift, 0), None), lambda _, g: (pltpu.roll(g, n - shift, 0),))
    return f(x)


def _hg_chunk(q_raw, f_pre, v, z, st, lb, go):
    c = q_raw.shape[0]
    nsub = c // SUB
    lf = jnp.log(lb + (1.0 - lb) * jax.nn.sigmoid(f_pre))
    k = (1.0 - lb) * jax.nn.sigmoid(-f_pre)
    q = _silu(q_raw)
    b = _cumsum_rows(lf)
    rowf = lax.broadcasted_iota(jnp.int32, lf.shape, 0)
    bmid = [jnp.sum(jnp.where(rowf == SUB * i + SUB // 2, b, 0.0), axis=0, keepdims=True) for i in range(nsub)]
    row = lax.broadcasted_iota(jnp.int32, (c, 1), 0)
    ref = sum(jnp.where((row >= SUB * i) & (row < SUB * (i + 1)), bmid[i], 0.0) for i in range(nsub))
    qt = q * jnp.exp(b - ref)
    kall = jnp.concatenate(
        [k * jnp.exp(jnp.where(row < SUB * (i + 1), bmid[i] - b, -jnp.inf)) for i in range(nsub)], axis=0)
    v4 = jnp.concatenate([v] * nsub, axis=0)
    b_last = jnp.sum(lf, axis=0, keepdims=True)
    qb = q * jnp.exp(b)
    kd = k * jnp.exp(b_last - b)
    e_last = jnp.exp(b_last)
    tq = lax.broadcasted_iota(jnp.int32, (c, nsub * c), 0)
    cq = lax.broadcasted_iota(jnp.int32, (c, nsub * c), 1)
    m_all = ((cq // c) == (tq // SUB)) & ((cq % c) <= tq)
    hs = lambda a: jnp.split(a, HG_H, axis=1)
    qt_h, kall_h, v4_h, qb_h, kd_h, v_h, z_h, el_h = map(hs, (qt, kall, v4, qb, kd, v, z, e_last))
    st_h = jnp.split(st, HG_H, axis=0)
    p_out, st_out = [], []
    for h in range(HG_H):
        pm = jnp.where(m_all, _mm_nt(qt_h[h], kall_h[h]), 0.0)
        o = _mm(pm, v4_h[h]) + _mm_nt(qb_h[h], st_h[h])
        st_out.append(el_h[h] * st_h[h] + _mm_tn(v_h[h], kd_h[h]))
        y = o * lax.rsqrt(jnp.mean(o * o, axis=1, keepdims=True) + EPS) * go
        p_out.append(y * _silu(z_h[h]))
    return jnp.concatenate(p_out, axis=1), jnp.concatenate(st_out, axis=0)


def _hg_fwd(u, lb, go, gather=None):
    t = u.shape[0]
    n = t // CHUNK

    def body(u_ref, lb_ref, go_ref, *rest):
        if gather is None:
            p_ref, sts_ref, st_ref = rest
        else:
            shard_ref, p_ref, sts_ref, all_ref, st_ref, *sems = rest
            start, wait = _gather_plan(shard_ref, all_ref, *sems)
            pl.when(pl.program_id(0) == 0)(start)

        @pl.when(pl.program_id(0) == 0)
        def _():
            st_ref[...] = jnp.zeros_like(st_ref)

        st = st_ref[...]
        sts_ref[0] = st
        p, st_next = _hg_chunk(u_ref[:, 0:D], u_ref[:, D:2 * D], u_ref[:, 2 * D:3 * D], u_ref[:, 3 * D:4 * D],
                               st, lb_ref[...], go_ref[...])
        p_ref[...] = p.astype(BF16)
        st_ref[...] = st_next
        if gather is not None:
            pl.when(pl.program_id(0) == n - 1)(wait)

    more = gather is not None
    return pl.pallas_call(
        body, name="hg_fwd_gather" if more else "hg_fwd", grid=(n,),
        in_specs=[pl.BlockSpec((CHUNK, 4 * D), lambda i: (i, 0)),
                  pl.BlockSpec((1, D), lambda i: (0, 0)),
                  pl.BlockSpec((1, HD), lambda i: (0, 0))] + [ANY] * more,
        out_specs=[pl.BlockSpec((CHUNK, D), lambda i: (i, 0)),
                   pl.BlockSpec((1, HG_H * HD, HD), lambda i: (i, 0, 0))] + [ANY] * more,
        out_shape=[jax.ShapeDtypeStruct((t, D), BF16), jax.ShapeDtypeStruct((n, HG_H * HD, HD), F32)]
        + ([jax.ShapeDtypeStruct((4,) + gather.shape, gather.dtype)] if more else []),
        scratch_shapes=[pltpu.VMEM((HG_H * HD, HD), F32)] + GATHER_SEMS * more,
        compiler_params=_cparams(("arbitrary",)),
    )(u, lb, go, *([gather] * more))


def _hg_bwd(u, sts, dp, lb, go, scatter=None):
    t = u.shape[0]
    n = t // CHUNK

    def body(u_ref, sts_ref, dp_ref, lb_ref, go_ref, *rest):
        if scatter is None:
            du_ref, dlb_ref, dgo_ref, dst_ref = rest
        else:
            parts_ref, du_ref, dlb_ref, dgo_ref, recv_ref, dst_ref, *sems = rest
            start, wait = _scatter_plan(parts_ref, recv_ref, *sems)
            pl.when(pl.program_id(0) == 0)(start)

        @pl.when(pl.program_id(0) == 0)
        def _():
            dst_ref[...] = jnp.zeros_like(dst_ref)
            dlb_ref[...] = jnp.zeros_like(dlb_ref)
            dgo_ref[...] = jnp.zeros_like(dgo_ref)

        _, vjp = jax.vjp(_hg_chunk, u_ref[:, 0:D], u_ref[:, D:2 * D], u_ref[:, 2 * D:3 * D], u_ref[:, 3 * D:4 * D],
                         sts_ref[0], lb_ref[...], go_ref[...])
        dq, df, dv, dz, dst, dlb, dgo = vjp((dp_ref[...].astype(F32), dst_ref[...]))
        du_ref[:, 0:D] = dq.astype(BF16)
        du_ref[:, D:2 * D] = df.astype(BF16)
        du_ref[:, 2 * D:3 * D] = dv.astype(BF16)
        du_ref[:, 3 * D:4 * D] = dz.astype(BF16)
        dst_ref[...] = dst
        dlb_ref[...] += dlb
        dgo_ref[...] += dgo
        if scatter is not None:
            pl.when(pl.program_id(0) == n - 1)(wait)

    rev = lambda i: (n - 1 - i, 0)
    more = scatter is not None
    return pl.pallas_call(
        body, name="hg_bwd_scatter" if more else "hg_bwd", grid=(n,),
        in_specs=[pl.BlockSpec((CHUNK, 4 * D), rev),
                  pl.BlockSpec((1, HG_H * HD, HD), lambda i: (n - 1 - i, 0, 0)),
                  pl.BlockSpec((CHUNK, D), rev),
                  pl.BlockSpec((1, D), lambda i: (0, 0)),
                  pl.BlockSpec((1, HD), lambda i: (0, 0))] + [ANY] * more,
        out_specs=[pl.BlockSpec((CHUNK, 4 * D), rev),
                   pl.BlockSpec((1, D), lambda i: (0, 0)),
                   pl.BlockSpec((1, HD), lambda i: (0, 0))] + [ANY] * more,
        out_shape=[jax.ShapeDtypeStruct((t, 4 * D), BF16), jax.ShapeDtypeStruct((1, D), F32),
                   jax.ShapeDtypeStruct((1, HD), F32)]
        + ([jax.ShapeDtypeStruct((3,) + scatter.shape[1:], scatter.dtype)] if more else []),
        scratch_shapes=[pltpu.VMEM((HG_H * HD, HD), F32)] + SCATTER_SEMS * more,
        compiler_params=_cparams(("arbitrary",)),
    )(u, sts, dp, lb, go, *([scatter] * more))


GD_VH = 16
GD_QKH = 8
GD_QKV = 4096
GD_VW = 2048
GD_N = GD_QKV + GD_VW + HD
GD_GRP = 4
GD_SOLVE = (GD_VH // GD_GRP, GD_GRP * CHUNK, 2 * HD)
HALO = 8


def _mm_high(a, b):
    return _dot(a, b, 1, 0, lax.Precision.HIGH)


def _lane_pick(a, h):
    lane = lax.broadcasted_iota(jnp.int32, a.shape, 1)
    return jnp.sum(jnp.where(lane == h, a, 0.0), axis=1, keepdims=True)


def _l2n(x):
    return x * lax.rsqrt(jnp.sum(x * x, axis=1, keepdims=True) + EPS)


def _solve_fwd(a_mat, rhs):
    n = a_mat.shape[0]
    r_i, c_i = lax.broadcasted_iota(jnp.int32, (n, n), 0), lax.broadcasted_iota(jnp.int32, (n, n), 1)
    same = lambda nb: (r_i // nb) == (c_i // nb)
    d0 = jnp.where(same(8), a_mat, 0.0)
    d2 = _mm(d0, d0)
    tinv = (r_i == c_i).astype(F32) - d0
    tinv = tinv + _mm(tinv, d2)
    tinv = tinv + _mm(tinv, _mm(d2, d2))
    nb = 16
    while nb <= CHUNK:
        low = jnp.where(same(nb) & ~same(nb // 2), a_mat, 0.0)
        tinv = tinv - _mm(_mm(tinv, low), tinv)
        nb *= 2
    x = _mm(tinv, rhs)
    return x, (tinv, x)


def _solve_bwd(res, dx):
    tinv, x = res
    drhs = _dot(tinv, dx, 0, 0, lax.Precision.HIGH)
    return -_dot(drhs, x, 1, 1, lax.Precision.HIGH), drhs


@jax.custom_vjp
def _solved(a_mat, rhs, tinv, x):
    return x


_solved.defvjp(lambda a_mat, rhs, tinv, x: (x, (tinv, x)),
               lambda res, dx: _solve_bwd(res, dx) + (jnp.zeros_like(res[0]), jnp.zeros_like(res[1])))


def _gd_chunk(xh, x, z, ab, st, cw, alog, dtb, go, solve):
    c = x.shape[0]
    xa = jnp.concatenate([xh, x], axis=0)
    sh = [jnp.split(_roll_rows(xa, 3 - j), [HALO], axis=0)[1] for j in range(3)]
    qkv = _silu(cw[0:1] * sh[0] + cw[1:2] * sh[1] + cw[2:3] * sh[2] + cw[3:4] * x)
    q_all, k_all, v_all = jnp.split(qkv, [1024, 2048], axis=1)
    lane = lax.broadcasted_iota(jnp.int32, (c, HD), 1)
    a_part = jnp.where(lane < GD_VH, ab, 0.0)
    g_all = -jnp.exp(alog) * jax.nn.softplus(a_part + dtb)
    d_all = _cumsum_rows(g_all)
    dl_all = jnp.sum(g_all, axis=0, keepdims=True)
    beta_all = jax.nn.sigmoid(ab)
    gc = GD_GRP * c
    r_i, c_i = lax.broadcasted_iota(jnp.int32, (gc, gc), 0), lax.broadcasted_iota(jnp.int32, (gc, gc), 1)
    same_head = (r_i // c) == (c_i // c)
    tri_g, strict_g = same_head & (c_i <= r_i), same_head & (c_i < r_i)
    qs =jnp.split(q_all, GD_QKH, axis=1)
    ks = jnp.split(k_all, GD_QKH, axis=1)
    vs = jnp.split(v_all, GD_VH, axis=1)
    zs = jnp.split(z, GD_VH, axis=1)
    sts = jnp.split(st, GD_VH, axis=0)
    qn = [_l2n(a) * (HD ** -0.5) for a in qs]
    kn = [_l2n(a) for a in ks]
    p_out, st_out = [], []
    for g in range(GD_VH // GD_GRP):
        heads = range(GD_GRP * g, GD_GRP * (g + 1))
        stack = lambda f: jnp.concatenate([f(h) for h in heads], axis=0)
        q_, k_, v_ = stack(lambda h: qn[h // 2]), stack(lambda h: kn[h // 2]), stack(lambda h: vs[h])
        dcol = stack(lambda h: _lane_pick(d_all, h))
        bcol = stack(lambda h: _lane_pick(beta_all, GD_VH + h))
        dlast = stack(lambda h: jnp.broadcast_to(_lane_pick(dl_all, h), (c, 1)))
        drow = jnp.sum(jnp.broadcast_to(dcol, (gc, HD)).T, axis=0, keepdims=True) * (1.0 / HD)
        dec = jnp.exp(jnp.where(tri_g, dcol - drow, -jnp.inf))
        kb = k_ * bcol
        a_mat = jnp.where(strict_g, _mm_nt(kb, k_) * dec, 0.0)
        xsol = solve(g, a_mat, jnp.concatenate([v_ * bcol, kb * jnp.exp(dcol)], axis=1))
        u_, w_ = jnp.split(xsol, 2, axis=1)
        w_h = jnp.split(w_, GD_GRP, axis=0)
        v_new = u_ - jnp.concatenate([_mm(w_h[i], sts[h]) for i, h in enumerate(heads)], axis=0)
        qd_h = jnp.split(q_ * jnp.exp(dcol), GD_GRP, axis=0)
        o_g = _mm(_mm_nt(q_, k_) * dec, v_new) + jnp.concatenate(
            [_mm(qd_h[i], sts[h]) for i, h in enumerate(heads)], axis=0)
        kd_h = jnp.split(k_ * jnp.exp(dlast - dcol), GD_GRP, axis=0)
        vn_h = jnp.split(v_new, GD_GRP, axis=0)
        o_h = jnp.split(o_g, GD_GRP, axis=0)
        for i, h in enumerate(heads):
            st_out.append(sts[h] * jnp.exp(_lane_pick(dl_all, h)) + _mm_tn(kd_h[i], vn_h[i]))
            o = o_h[i]
            y = o * lax.rsqrt(jnp.mean(o * o, axis=1, keepdims=True) + EPS) * go
            p_out.append(y * _silu(zs[h]))
    return jnp.concatenate(p_out, axis=1), jnp.concatenate(st_out, axis=0)


def _gd_specs(n, rev):
    ci = (lambda i: n - 1 - i) if rev else (lambda i: i)
    return [pl.BlockSpec((HALO, GD_QKV), lambda i: (jnp.maximum(ci(i) * (CHUNK // HALO) - 1, 0), 0)),
            pl.BlockSpec((CHUNK, GD_N), lambda i: (ci(i), 0))]


def _gd_load(uh_ref, u_ref, first):
    xh = jnp.where(first, 0.0, uh_ref[...])
    return xh, u_ref[:, 0:GD_QKV], u_ref[:, GD_QKV:GD_QKV + GD_VW], u_ref[:, GD_QKV + GD_VW:GD_N]


def _gd_fwd(u, cw, alog, dtb, go):
    t = u.shape[0]
    n = t // CHUNK
    small = lambda r, w: pl.BlockSpec((r, w), lambda i: (0, 0))

    def body(uh_ref, u_ref, cw_ref, alog_ref, dtb_ref, go_ref, p_ref, sts_ref, tinv_ref, xsol_ref, st_ref):
        i = pl.program_id(0)

        @pl.when(i == 0)
        def _():
            st_ref[...] = jnp.zeros_like(st_ref)

        def solve(g, a_mat, rhs):
            xsol, (tinv, _) = _solve_fwd(a_mat, rhs)
            tinv_ref[0, g] = tinv
            xsol_ref[0, g] = xsol
            return xsol

        st = st_ref[...]
        sts_ref[0] = st
        p, st_next = _gd_chunk(*_gd_load(uh_ref, u_ref, i == 0), st, cw_ref[...], alog_ref[...], dtb_ref[...],
                               go_ref[...], solve)
        p_ref[...] = p.astype(BF16)
        st_ref[...] = st_next

    return pl.pallas_call(
        body, name="gd_fwd", grid=(n,),
        in_specs=_gd_specs(n, False) + [small(8, GD_QKV), small(1, HD), small(1, HD), small(1, HD)],
        out_specs=[pl.BlockSpec((CHUNK, GD_VW), lambda i: (i, 0)),
                   pl.BlockSpec((1, GD_VH * HD, HD), lambda i: (i, 0, 0)),
                   pl.BlockSpec((1,) + GD_SOLVE, lambda i: (i, 0, 0, 0)),
                   pl.BlockSpec((1,) + GD_SOLVE, lambda i: (i, 0, 0, 0))],
        out_shape=[jax.ShapeDtypeStruct((t, GD_VW), BF16), jax.ShapeDtypeStruct((n, GD_VH * HD, HD), F32),
                   jax.ShapeDtypeStruct((n,) + GD_SOLVE, F32), jax.ShapeDtypeStruct((n,) + GD_SOLVE, F32)],
        scratch_shapes=[pltpu.VMEM((GD_VH * HD, HD), F32)],
        compiler_params=_cparams(("arbitrary",)),
    )(u, u, cw, alog, dtb, go)


def _gd_bwd(u, sts, tinvs, xsols, dp, cw, alog, dtb, go):
    t = u.shape[0]
    n = t // CHUNK
    small = lambda r, w: pl.BlockSpec((r, w), lambda i: (0, 0))

    def body(uh_ref, u_ref, sts_ref, tinv_ref, xsol_ref, dp_ref, cw_ref, alog_ref, dtb_ref, go_ref,
             du_ref, dcw_ref, dalog_ref, ddtb_ref, dgo_ref, dst_ref, dhalo_ref):
        i = pl.program_id(0)

        @pl.when(i == 0)
        def _():
            for r in (dst_ref, dhalo_ref, dcw_ref, dalog_ref, ddtb_ref, dgo_ref):
                r[...] = jnp.zeros_like(r)

        solve = lambda g, a_mat, rhs: _solved(a_mat, rhs, tinv_ref[0, g], xsol_ref[0, g])
        chunk = functools.partial(_gd_chunk, solve=solve)
        _, vjp = jax.vjp(chunk, *_gd_load(uh_ref, u_ref, i == n - 1), sts_ref[0], cw_ref[...], alog_ref[...],
                         dtb_ref[...], go_ref[...])
        dxh, dx, dz, dab, dst, dcw, dalog, ddtb, dgo = vjp((dp_ref[...].astype(F32), dst_ref[...]))
        tail = jnp.concatenate([jnp.zeros((CHUNK - HALO, GD_QKV), F32), dhalo_ref[...]], axis=0)
        du_ref[:, 0:GD_QKV] = (dx + tail).astype(BF16)
        du_ref[:, GD_QKV:GD_QKV + GD_VW] = dz.astype(BF16)
        du_ref[:, GD_QKV + GD_VW:GD_N] = dab.astype(BF16)
        dhalo_ref[...] = dxh
        dst_ref[...] = dst
        dcw_ref[...] += dcw
        dalog_ref[...] += dalog
        ddtb_ref[...] += ddtb
        dgo_ref[...] += dgo

    return pl.pallas_call(
        body, name="gd_bwd", grid=(n,),
        in_specs=_gd_specs(n, True) + [pl.BlockSpec((1, GD_VH * HD, HD), lambda i: (n - 1 - i, 0, 0)),
                                       pl.BlockSpec((1,) + GD_SOLVE, lambda i: (n - 1 - i, 0, 0, 0)),
                                       pl.BlockSpec((1,) + GD_SOLVE, lambda i: (n - 1 - i, 0, 0, 0)),
                                       pl.BlockSpec((CHUNK, GD_VW), lambda i: (n - 1 - i, 0)),
                                       small(8, GD_QKV), small(1, HD), small(1, HD), small(1, HD)],
        out_specs=[pl.BlockSpec((CHUNK, GD_N), lambda i: (n - 1 - i, 0)),
                   small(8, GD_QKV), small(1, HD), small(1, HD), small(1, HD)],
        out_shape=[jax.ShapeDtypeStruct((t, GD_N), BF16), jax.ShapeDtypeStruct((8, GD_QKV), F32)]
        + [jax.ShapeDtypeStruct((1, HD), F32)] * 3,
        scratch_shapes=[pltpu.VMEM((GD_VH * HD, HD), F32), pltpu.VMEM((HALO, GD_QKV), F32)],
        compiler_params=_cparams(("arbitrary",)),
    )(u, u, sts, tinvs, xsols, dp, cw, alog, dtb, go)


SW_B = 128
SW_H = 16
SW_G = 4
SW_N = 2560
SW_KV0 = 1024


def _blockdiag(n, blk):
    r = lax.broadcasted_iota(jnp.int32, (n, n), 0) // blk
    c = lax.broadcasted_iota(jnp.int32, (n, n), 1) // blk
    return (r == c).astype(F32)


def _sw_normrope(x, g1, g2, cos, sin):
    w = x.shape[1] // 2
    x1, x2 = jnp.split(x, 2, axis=1)
    ms = _mm_high(x1 * x1 + x2 * x2, _blockdiag(w, 32)) * (1.0 / 64.0)
    rinv = lax.rsqrt(ms + EPS)
    n1, n2 = x1 * rinv * g1, x2 * rinv * g2
    return jnp.concatenate([n1 * cos - n2 * sin, n2 * cos + n1 * sin], axis=1)


def _sw_block(q, kvp, kvc, z, csp, csc, gq, gk, sinks, has_prev):
    b = q.shape[0]
    cos_c, sin_c = jnp.split(csc, 2, axis=1)
    cos_p, sin_p = jnp.split(csp, 2, axis=1)
    tile4 = lambda a: jnp.concatenate([a] * 4, axis=1)
    qh = _sw_normrope(q, gq[0:1], gq[1:2], tile4(cos_c), tile4(sin_c))
    kp, vp = jnp.split(kvp, 2, axis=1)
    kc, vc = jnp.split(kvc, 2, axis=1)
    kh = jnp.concatenate([_sw_normrope(kp, gk[0:1], gk[1:2], cos_p, sin_p),
                          _sw_normrope(kc, gk[0:1], gk[1:2], cos_c, sin_c)], axis=0)
    vv = jnp.concatenate([vp, vc], axis=0)
    q1, q2 = jnp.split(qh, 2, axis=1)
    q1g, q2g = jnp.split(q1, SW_G, axis=1), jnp.split(q2, SW_G, axis=1)
    own = lax.broadcasted_iota(jnp.int32, (4 * b, b), 1) <= lax.broadcasted_iota(jnp.int32, (4 * b, b), 0) % b
    ri = lax.broadcasted_iota(jnp.int32, (256, 256), 0)
    ci = lax.broadcasted_iota(jnp.int32, (256, 256), 1)
    row_head = lax.broadcasted_iota(jnp.int32, (4 * b, 256), 0) // b
    lane_q = lax.broadcasted_iota(jnp.int32, (4 * b, 256), 1)
    q_sel = (lane_q % 128) // 32 == row_head
    o_sel = lane_q // 64 == row_head
    o_out = []
    for g in range(SW_G):
        ek = ((ri // 128 == ci // 128) & ((ri % 128) // 32 == g) & (ri % 32 == ci % 32)).astype(F32)
        ev = ((ri // 64 == g) & (ri % 64 == ci % 64)).astype(F32)
        kx = _mm(kh, ek)
        vx = _mm(vv, ev)
        qg = jnp.concatenate([q1g[g], q2g[g]], axis=1)
        q4 = jnp.where(q_sel, jnp.concatenate([qg] * 4, axis=0), 0.0)
        sink = jnp.concatenate([jnp.broadcast_to(_lane_pick(sinks, 4 * g + j), (b, 1)) for j in range(4)], axis=0)
        s_prev, s_own = jnp.split(_mm_nt(q4, kx) * (64 ** -0.5), 2, axis=1)
        s = jnp.where(own, s_own, jnp.where(has_prev, s_prev, -jnp.inf))
        top = jnp.max(s)
        soft = top + 8.0 * jnp.log(jnp.sum(jnp.exp((s - top) * 0.125), axis=1, keepdims=True))
        m = lax.stop_gradient(jnp.maximum(soft, sink))
        p = jnp.exp(s - m)
        pn = p / (jnp.sum(p, axis=1, keepdims=True) + jnp.exp(sink - m))
        pn2 = jnp.concatenate([jnp.where(own, 0.0, pn), jnp.where(own, pn, 0.0)], axis=1)
        o4 = jnp.split(jnp.where(o_sel, _mm(pn2, vx), 0.0), 4, axis=0)
        o_out.append(o4[0] + o4[1] + o4[2] + o4[3])
    return jnp.concatenate(o_out, axis=1) * _silu(z)


def _sw_specs(n, rev):
    ci = (lambda i: n - 1 - i) if rev else (lambda i: i)
    prev = lambda i: jnp.maximum(ci(i) - 1, 0)
    return [pl.BlockSpec((SW_B, SW_N), lambda i: (ci(i), 0)),
            pl.BlockSpec((SW_B, 512), lambda i: (prev(i), SW_KV0 // 512)),
            pl.BlockSpec((SW_B, 256), lambda i: (ci(i), 0)),
            pl.BlockSpec((SW_B, 256), lambda i: (prev(i), 0)),
            pl.BlockSpec((2, 512), lambda i: (0, 0)), pl.BlockSpec((2, 128), lambda i: (0, 0)),
            pl.BlockSpec((1, 128), lambda i: (0, 0))]


def _sw_args(u_ref, kvp_ref, csc_ref, csp_ref, gq_ref, gk_ref, sk_ref, has_prev):
    return (u_ref[:, 0:D], kvp_ref[...], u_ref[:, SW_KV0:SW_KV0 + 512], u_ref[:, SW_KV0 + 512:SW_N],
            csp_ref[...], csc_ref[...], gq_ref[...], gk_ref[...], sk_ref[...], has_prev)


def _sw_fwd(u, cs, gq, gk, sinks):
    t = u.shape[0]
    n = t // SW_B

    def body(u_ref, kvp_ref, csc_ref, csp_ref, gq_ref, gk_ref, sk_ref, p_ref):
        has_prev = pl.program_id(0) > 0
        p_ref[...] = _sw_block(*_sw_args(u_ref, kvp_ref, csc_ref, csp_ref, gq_ref, gk_ref, sk_ref, has_prev)
                               ).astype(BF16)

    return pl.pallas_call(
        body, name="sw_fwd", grid=(n,), in_specs=_sw_specs(n, False),
        out_specs=pl.BlockSpec((SW_B, D), lambda i: (i, 0)),
        out_shape=jax.ShapeDtypeStruct((t, D), BF16),
        compiler_params=_cparams(("arbitrary",)),
    )(u, u, cs, cs, gq, gk, sinks)


def _sw_bwd(u, cs, dp, gq, gk, sinks):
    t = u.shape[0]
    n = t // SW_B

    def body(u_ref, kvp_ref, csc_ref, csp_ref, gq_ref, gk_ref, sk_ref, dp_ref,
             du_ref, dgq_ref, dgk_ref, dsk_ref, dkv_ref):
        i = pl.program_id(0)

        @pl.when(i == 0)
        def _():
            for r in (dkv_ref, dgq_ref, dgk_ref, dsk_ref):
                r[...] = jnp.zeros_like(r)

        has_prev = i < n - 1
        args = _sw_args(u_ref, kvp_ref, csc_ref, csp_ref, gq_ref, gk_ref, sk_ref, has_prev)
        fn = lambda q, kvp, kvc, z, gq_, gk_, sk_: _sw_block(q, kvp, kvc, z, args[4], args[5], gq_, gk_, sk_, has_prev)
        _, vjp = jax.vjp(fn, args[0], args[1], args[2], args[3], args[6], args[7], args[8])
        dq, dkvp, dkvc, dz, dgq, dgk, dsk = vjp(dp_ref[...].astype(F32))
        du_ref[:, 0:D] = dq.astype(BF16)
        du_ref[:, SW_KV0:SW_KV0 + 512] = (dkvc + dkv_ref[...]).astype(BF16)
        du_ref[:, SW_KV0 + 512:SW_N] = dz.astype(BF16)
        dkv_ref[...] = dkvp
        dgq_ref[...] += dgq
        dgk_ref[...] += dgk
        dsk_ref[...] += dsk

    small = lambda r, w: pl.BlockSpec((r, w), lambda i: (0, 0))
    return pl.pallas_call(
        body, name="sw_bwd", grid=(n,),
        in_specs=_sw_specs(n, True) + [pl.BlockSpec((SW_B, D), lambda i: (n - 1 - i, 0))],
        out_specs=[pl.BlockSpec((SW_B, SW_N), lambda i: (n - 1 - i, 0)), small(2, 512), small(2, 128), small(1, 128)],
        out_shape=[jax.ShapeDtypeStruct((t, SW_N), BF16), jax.ShapeDtypeStruct((2, 512), F32),
                   jax.ShapeDtypeStruct((2, 128), F32), jax.ShapeDtypeStruct((1, 128), F32)],
        scratch_shapes=[pltpu.VMEM((SW_B, 512), F32)],
        compiler_params=_cparams(("arbitrary",)),
    )(u, u, cs, cs, gq, gk, sinks, dp)


def _ln_mod(x, g, scale, shift):
    y = x * lax.rsqrt(jnp.mean(x * x, axis=1, keepdims=True) + EPS) * g
    return y * (1.0 + scale) + shift


def _row_tile(t):
    return min(t, 1024)


def _ln_mm(x, g, scale, shift, w, tn):
    t, n = x.shape[0], w.shape[1]
    tm = _row_tile(t)
    vec = pl.BlockSpec((1, D), lambda i, j: (0, 0))

    def body(x_ref, g_ref, sc_ref, sh_ref, w_ref, u_ref, h_ref):
        @pl.when(pl.program_id(1) == 0)
        def _():
            h_ref[...] = _ln_mod(x_ref[...], g_ref[...], sc_ref[...], sh_ref[...]).astype(BF16)

        u_ref[...] = _dot(h_ref[...], w_ref[...], 1, 0)

    return pl.pallas_call(
        body, name="ln_mm", grid=(t // tm, n // tn),
        in_specs=[pl.BlockSpec((tm, D), lambda i, j: (i, 0)), vec, vec, vec,
                  pl.BlockSpec((D, tn), lambda i, j: (0, j))],
        out_specs=[pl.BlockSpec((tm, tn), lambda i, j: (i, j)), pl.BlockSpec((tm, D), lambda i, j: (i, 0))],
        out_shape=[jax.ShapeDtypeStruct((t, n), F32), jax.ShapeDtypeStruct((t, D), BF16)],
        compiler_params=_cparams(("arbitrary", "arbitrary")),
    )(x, g, scale, shift, w)


def _mm_res(p, w, x, gate):
    t, k = p.shape
    tm = _row_tile(t)

    def body(p_ref, w_ref, x_ref, gate_ref, o_ref):
        o_ref[...] = x_ref[...] + gate_ref[...] * _dot(p_ref[...], w_ref[...], 1, 0)

    return pl.pallas_call(
        body, name="mm_res", grid=(t // tm,),
        in_specs=[pl.BlockSpec((tm, k), lambda i: (i, 0)), pl.BlockSpec((k, D), lambda i: (0, 0)),
                  pl.BlockSpec((tm, D), lambda i: (i, 0)), pl.BlockSpec((1, D), lambda i: (0, 0))],
        out_specs=pl.BlockSpec((tm, D), lambda i: (i, 0)),
        out_shape=jax.ShapeDtypeStruct((t, D), F32),
        compiler_params=_cparams(("arbitrary",)),
    )(p, w, x, gate)


def _loss_grad(x, target):
    t = x.shape[0]
    tm = _row_tile(t)

    def body(x_ref, t_ref, l_ref, dx_ref):
        @pl.when(pl.program_id(0) == 0)
        def _():
            l_ref[...] = jnp.zeros_like(l_ref)

        err = x_ref[...] - t_ref[...]
        dx_ref[...] = err * (1.0 / D)
        l_ref[...] += 0.5 * jnp.sum(jnp.mean(err * err, axis=1, keepdims=True), axis=0, keepdims=True)

    return pl.pallas_call(
        body, name="loss_grad", grid=(t // tm,),
        in_specs=[pl.BlockSpec((tm, D), lambda i: (i, 0))] * 2,
        out_specs=[pl.BlockSpec((8, 128), lambda i: (0, 0)), pl.BlockSpec((tm, D), lambda i: (i, 0))],
        out_shape=[jax.ShapeDtypeStruct((8, 128), F32), jax.ShapeDtypeStruct((t, D), F32)],
        compiler_params=_cparams(("arbitrary",)),
    )(x, target)


def _mm_scaled(a, s, w, tn):
    t, k = a.shape
    n = w.shape[1]
    tm = _row_tile(t)

    def body(a_ref, s_ref, w_ref, o_ref):
        o_ref[...] = _dot((a_ref[...] * s_ref[...]).astype(BF16), w_ref[...], 1, 0).astype(BF16)

    return pl.pallas_call(
        body, name="mm_scaled", grid=(t // tm, n // tn),
        in_specs=[pl.BlockSpec((tm, k), lambda i, j: (i, 0)), pl.BlockSpec((1, k), lambda i, j: (0, 0)),
                  pl.BlockSpec((k, tn), lambda i, j: (0, j))],
        out_specs=pl.BlockSpec((tm, tn), lambda i, j: (i, j)),
        out_shape=jax.ShapeDtypeStruct((t, n), BF16),
        compiler_params=_cparams(("arbitrary", "arbitrary")),
    )(a, s, w)


def _mm_tn_acc(a, b, tn):
    t, m = a.shape
    n = b.shape[1]
    fits = lambda k: 2 * k * (m * a.dtype.itemsize + tn * b.dtype.itemsize) + 2 * m * tn * 4 <= 36 * 1024 * 1024
    tk = next(k for k in (4096, 2048, 1024, 512, t) if t % k == 0 and (fits(k) or k <= 512))
    nk = t // tk

    def body(a_ref, b_ref, o_ref):
        @pl.when(pl.program_id(1) == 0)
        def _():
            o_ref[...] = jnp.zeros_like(o_ref)

        o_ref[...] += _dot(a_ref[...], b_ref[...].astype(BF16), 0, 0)

    return pl.pallas_call(
        body, name="mm_tn_acc", grid=(n // tn, nk),
        in_specs=[pl.BlockSpec((tk, m), lambda j, k: (k, 0)), pl.BlockSpec((tk, tn), lambda j, k: (k, j))],
        out_specs=pl.BlockSpec((m, tn), lambda j, k: (0, j)),
        out_shape=jax.ShapeDtypeStruct((m, n), F32),
        compiler_params=_cparams(("arbitrary", "arbitrary")),
    )(a, b)


def _inproj_bwd(du, wt, x, dxp, g, scale, shift):
    t, kdim = du.shape
    tk = kdim
    tm = min(t, 512 if kdim <= 4096 else 256)
    nk = kdim // tk
    vec = pl.BlockSpec((1, D), lambda i, k: (0, 0))

    def body(du_ref, wt_ref, x_ref, dxp_ref, g_ref, sc_ref, sh_ref, dx_ref, dv_ref, acc_ref):
        k = pl.program_id(1)

        @pl.when((pl.program_id(0) == 0) & (k == 0))
        def _():
            dv_ref[...] = jnp.zeros_like(dv_ref)

        @pl.when(k == 0)
        def _():
            acc_ref[...] = jnp.zeros_like(acc_ref)

        acc_ref[...] += _dot(du_ref[...].astype(BF16), wt_ref[...], 1, 0)

        @pl.when(k == nk - 1)
        def _():
            _, vjp = jax.vjp(_ln_mod, x_ref[...], g_ref[...], sc_ref[...], sh_ref[...])
            dx, dg, dsc, dsh = vjp(acc_ref[...])
            dx_ref[...] = dxp_ref[...] + dx
            dv_ref[0:1, :] += dg
            dv_ref[1:2, :] += dsc
            dv_ref[2:3, :] += dsh

    return pl.pallas_call(
        body, name="inproj_bwd", grid=(t // tm, nk),
        in_specs=[pl.BlockSpec((tm, tk), lambda i, k: (i, k)), pl.BlockSpec((tk, D), lambda i, k: (k, 0)),
                  pl.BlockSpec((tm, D), lambda i, k: (i, 0)), pl.BlockSpec((tm, D), lambda i, k: (i, 0)),
                  vec, vec, vec],
        out_specs=[pl.BlockSpec((tm, D), lambda i, k: (i, 0)), pl.BlockSpec((8, D), lambda i, k: (0, 0))],
        out_shape=[jax.ShapeDtypeStruct((t, D), F32), jax.ShapeDtypeStruct((8, D), F32)],
        scratch_shapes=[pltpu.VMEM((tm, D), F32)],
        compiler_params=_cparams(("arbitrary", "arbitrary")),
    )(du, wt, x, dxp, g, scale, shift)


def _outgrad(gmat, w, gate):
    k = gmat.shape[0]
    tr = 256

    def body(g_ref, w_ref, gate_ref, dw_ref, dg_ref):
        @pl.when(pl.program_id(0) == 0)
        def _():
            dg_ref[...] = jnp.zeros_like(dg_ref)

        gm = g_ref[...]
        dw_ref[...] = gm * gate_ref[...]
        dg_ref[0:1, :] += jnp.sum(gm * w_ref[...].astype(F32), axis=0, keepdims=True)

    return pl.pallas_call(
        body, name="outgrad", grid=(k // tr,),
        in_specs=[pl.BlockSpec((tr, D), lambda i: (i, 0)), pl.BlockSpec((tr, D), lambda i: (i, 0)),
                  pl.BlockSpec((1, D), lambda i: (0, 0))],
        out_specs=[pl.BlockSpec((tr, D), lambda i: (i, 0)), pl.BlockSpec((8, D), lambda i: (0, 0))],
        out_shape=[jax.ShapeDtypeStruct((k, D), F32), jax.ShapeDtypeStruct((8, D), F32)],
        compiler_params=_cparams(("arbitrary",)),
    )(gmat, w, gate)


def _rope_table(pos, freq):
    t = pos.shape[0]
    tm = _row_tile(t)

    def body(p_ref, f_ref, o_ref):
        ang = p_ref[...].astype(F32) * f_ref[...]
        o_ref[:, 0:128] = jnp.cos(ang)
        o_ref[:, 128:256] = jnp.sin(ang)

    return pl.pallas_call(
        body, name="rope_table", grid=(t // tm,),
        in_specs=[pl.BlockSpec((tm, 1), lambda i: (i, 0)), pl.BlockSpec((1, 128), lambda i: (0, 0))],
        out_specs=pl.BlockSpec((tm, 256), lambda i: (i, 0)),
        out_shape=jax.ShapeDtypeStruct((t, 256), F32),
        compiler_params=_cparams(("arbitrary",)),
    )(pos, freq)


def _ada_fwd(c_all, w, b):
    nl, _, s = w.shape

    def body(c_ref, w_ref, b_ref, o_ref):
        o_ref[0] = _mm_f32(c_ref[...], w_ref[0]) + b_ref[0]

    return pl.pallas_call(
        body, name="ada_fwd", grid=(nl,),
        in_specs=[pl.BlockSpec((8, D), lambda l: (0, 0)), pl.BlockSpec((1, D, s), lambda l: (l, 0, 0)),
                  pl.BlockSpec((1, 1, s), lambda l: (l, 0, 0))],
        out_specs=pl.BlockSpec((1, 8, s), lambda l: (l, 0, 0)),
        out_shape=jax.ShapeDtypeStruct((nl, 8, s), F32),
        compiler_params=_cparams(("arbitrary",)),
    )(c_all, w, b)


def _ada_bwd(c_all, dmod_cols, dmod_all):
    nl, _, s = dmod_cols.shape

    def body(c_ref, dc_ref, da_ref, gw_ref, gb_ref):
        gw_ref[0] = _dot(c_ref[...], dc_ref[0], 0, 0, lax.Precision.HIGHEST)
        gb_ref[0] = jnp.sum(da_ref[0], axis=0, keepdims=True)

    return pl.pallas_call(
        body, name="ada_bwd", grid=(nl,),
        in_specs=[pl.BlockSpec((8, D), lambda l: (0, 0)), pl.BlockSpec((1, 8, s), lambda l: (l, 0, 0)),
                  pl.BlockSpec((1, 8, 3 * D), lambda l: (l, 0, 0))],
        out_specs=[pl.BlockSpec((1, D, s), lambda l: (l, 0, 0)), pl.BlockSpec((1, 1, 3 * D), lambda l: (l, 0, 0))],
        out_shape=[jax.ShapeDtypeStruct((nl, D, s), F32), jax.ShapeDtypeStruct((nl, 1, 3 * D), F32)],
        compiler_params=_cparams(("arbitrary",)),
    )(c_all, dmod_cols, dmod_all)


def _lb_fn(h8):
    sm = jax.nn.softmax(h8, axis=0)
    r = lax.broadcasted_iota(jnp.int32, (8, 8), 0)
    c = lax.broadcasted_iota(jnp.int32, (8, 8), 1)
    return _mm_f32(((c >= 1) & (c <= r)).astype(F32), sm)


def _lb_fwd(h8):
    def body(h_ref, o_ref):
        o_ref[...] = _lb_fn(h_ref[...])

    return pl.pallas_call(body, name="lb_fwd", out_shape=jax.ShapeDtypeStruct((8, D), F32))(h8)


def _lb_bwd(h8, dlb8):
    def body(h_ref, d_ref, o_ref):
        _, vjp = jax.vjp(_lb_fn, h_ref[...])
        o_ref[...] = vjp(d_ref[...])[0]

    return pl.pallas_call(body, name="lb_bwd", out_shape=jax.ShapeDtypeStruct((8, D), F32))(h8, dlb8)


ADAM_LR, ADAM_B1, ADAM_B2, ADAM_EPS, ADAM_WD, ADAM_STEP = 0.001, 0.9, 0.999, 1e-08, 0.01, 10


def _adamw(w, gparts, m, v):
    r, c = w.shape
    tr = r if r * c * 4 <= (1 << 20) else max(8, ((1 << 20) // (c * 4)) // 8 * 8)
    while r % tr:
        tr -= 8
    ng = len(gparts)

    def body(*refs):
        w_ref, m_ref, v_ref = refs[0], refs[1 + ng], refs[2 + ng]
        g_ref, d_ref, nm_ref, nv_ref = refs[3 + ng:]
        g = refs[1][...]
        for gr in refs[2:1 + ng]:
            g = g + gr[...]
        mm = ADAM_B1 * m_ref[...] + (1.0 - ADAM_B1) * g
        vv = ADAM_B2 * v_ref[...] + (1.0 - ADAM_B2) * (g * g)
        m_hat = mm / (1.0 - ADAM_B1 ** ADAM_STEP)
        v_hat = vv / (1.0 - ADAM_B2 ** ADAM_STEP)
        g_ref[...] = g
        d_ref[...] = -ADAM_LR * (m_hat / (jnp.sqrt(v_hat) + ADAM_EPS) + ADAM_WD * w_ref[...])
        nm_ref[...] = mm
        nv_ref[...] = vv

    spec = pl.BlockSpec((tr, c), lambda i: (i, 0))
    return pl.pallas_call(
        body, name="adamw", grid=(r // tr,), in_specs=[spec] * (3 + ng), out_specs=[spec] * 4,
        out_shape=[jax.ShapeDtypeStruct((r, c), F32)] * 4,
        compiler_params=_cparams(("arbitrary",)),
    )(w, *gparts, m, v)


def _sum_rows(parts):
    r, c = parts[0].shape
    tr = 8
    for cand in range(min(r, 512), 7, -8):
        if r % cand == 0:
            tr = cand
            break

    def body(*refs):
        acc = refs[0][...]
        for p in refs[1:-1]:
            acc = acc + p[...]
        refs[-1][...] = acc

    spec = pl.BlockSpec((tr, c), lambda i: (i, 0))
    return pl.pallas_call(
        body, name="sum_rows", grid=(r // tr,), in_specs=[spec] * len(parts), out_specs=spec,
        out_shape=jax.ShapeDtypeStruct((r, c), F32),
        compiler_params=_cparams(("arbitrary",)),
    )(*parts)


MESH = pl.DeviceIdType.MESH
ANY = pl.BlockSpec(memory_space=pl.ANY)


def _place():
    return lax.axis_index("x"), lax.axis_index("y"), lax.axis_index("c")


def _allgather8(blk):
    m_per, n = blk.shape

    def body(x_ref, out_ref, send_sems, recv_sems, local_sem):
        x, y, c = _place()
        me, sibling = (x, y, c), (x, y, 1 - c)
        chips = [(1 - x, y), (x, 1 - y), (1 - x, 1 - y)]

        def rows(px, py, pc):
            return out_ref.at[pl.ds((4 * px + 2 * py + pc) * m_per, m_per), :]

        def copy(k, block, to, src=None):
            return pltpu.make_async_remote_copy(
                src_ref=rows(*block) if src is None else src, dst_ref=rows(*block),
                send_sem=send_sems.at[k], recv_sem=recv_sems.at[k], device_id=to, device_id_type=MESH)

        mine = pltpu.make_async_copy(x_ref, rows(*me), local_sem)
        mine.start()
        first = [copy(0, me, sibling, src=x_ref)]
        first += [copy(1 + j, me, (*chip, c), src=x_ref) for j, chip in enumerate(chips)]
        for cp in first:
            cp.start()
        passed = [copy(4 + j, (*chip, c), sibling) for j, chip in enumerate(chips)]
        for j, chip in enumerate(chips):
            copy(1 + j, (*chip, c), me).wait_recv()
            passed[j].start()
        copy(0, sibling, me).wait_recv()
        for j, chip in enumerate(chips):
            copy(4 + j, (*chip, 1 - c), me).wait_recv()
        for cp in first + passed:
            cp.wait_send()
        mine.wait()

    return pl.pallas_call(
        body, name="allgather8",
        out_shape=jax.ShapeDtypeStruct((8 * m_per, n), blk.dtype),
        in_specs=[pl.BlockSpec(memory_space=pltpu.VMEM)],
        out_specs=pl.BlockSpec(memory_space=pltpu.VMEM),
        scratch_shapes=[pltpu.SemaphoreType.DMA((7,)), pltpu.SemaphoreType.DMA((7,)), pltpu.SemaphoreType.DMA],
    )(blk)


def _chip_peers():
    x, y, c = _place()
    return [(1 - x, y, c), (x, 1 - y, c), (1 - x, 1 - y, c)]


GATHER_SEMS = [pltpu.SemaphoreType.DMA((3,)), pltpu.SemaphoreType.DMA((3,)), pltpu.SemaphoreType.DMA]
SCATTER_SEMS = [pltpu.SemaphoreType.DMA((3,)), pltpu.SemaphoreType.DMA((3,))]


def _gather_plan(x_ref, out_ref, send_sems, recv_sems, local_sem):
    x, y, _ = _place()
    peers = _chip_peers()

    def copy(j, chip_index):
        return pltpu.make_async_remote_copy(
            src_ref=x_ref, dst_ref=out_ref.at[chip_index], send_sem=send_sems.at[j], recv_sem=recv_sems.at[j],
            device_id=peers[j], device_id_type=MESH)

    mine = pltpu.make_async_copy(x_ref, out_ref.at[2 * x + y], local_sem)
    sends = [copy(j, 2 * x + y) for j in range(3)]

    def start():
        mine.start()
        for cp in sends:
            cp.start()

    def wait():
        for j in range(3):
            copy(j, 2 * peers[j][0] + peers[j][1]).wait_recv()
        for cp in sends:
            cp.wait_send()
        mine.wait()

    return start, wait


def _scatter_plan(p_ref, out_ref, send_sems, recv_sems):
    peers = _chip_peers()
    sends = [pltpu.make_async_remote_copy(
        src_ref=p_ref.at[2 * peers[j][0] + peers[j][1]], dst_ref=out_ref.at[j], send_sem=send_sems.at[j],
        recv_sem=recv_sems.at[j], device_id=peers[j], device_id_type=MESH) for j in range(3)]

    def start():
        for cp in sends:
            cp.start()

    def wait():
        for cp in sends:
            cp.wait_recv()
        for cp in sends:
            cp.wait_send()

    return start, wait


def _chip_allgather(shard):
    def body(x_ref, out_ref, *sems):
        start, wait = _gather_plan(x_ref, out_ref, *sems)
        start()
        wait()

    return pl.pallas_call(
        body, name="chip_allgather", out_shape=jax.ShapeDtypeStruct((4,) + shard.shape, shard.dtype),
        in_specs=[ANY], out_specs=ANY, scratch_shapes=GATHER_SEMS,
    )(shard)


def _chip_scatter(parts):
    def body(p_ref, out_ref, *sems):
        start, wait = _scatter_plan(p_ref, out_ref, *sems)
        start()
        wait()

    return pl.pallas_call(
        body, name="chip_scatter", out_shape=jax.ShapeDtypeStruct((3,) + parts.shape[1:], parts.dtype),
        in_specs=[ANY], out_specs=ANY, scratch_shapes=SCATTER_SEMS,
    )(parts)


def _sibling_swap(a):
    def body(a_ref, out_ref, send_sem, recv_sem):
        x, y, c = _place()
        cp = pltpu.make_async_remote_copy(src_ref=a_ref, dst_ref=out_ref, send_sem=send_sem, recv_sem=recv_sem,
                                          device_id=(x, y, 1 - c), device_id_type=MESH)
        cp.start()
        cp.wait_recv()
        cp.wait_send()

    return pl.pallas_call(
        body, name="sibling_swap", out_shape=jax.ShapeDtypeStruct(a.shape, a.dtype),
        in_specs=[ANY], out_specs=ANY,
        scratch_shapes=[pltpu.SemaphoreType.DMA, pltpu.SemaphoreType.DMA],
    )(a)


WEIGHTS = ['hgrn_lb', 'ada_w', 'ada_b', 'norm_g', 'hg_in_w', 'hg_out_w', 'hg_onorm', 'sw_in_w', 'sw_out_w', 'sw_qnorm',
           'sw_knorm', 'sw_sinks', 'gd_in_w', 'gd_out_w', 'gd_conv_w', 'gd_a_log', 'gd_dt_bias', 'gd_onorm']
BIG = ['hg_in_w', 'hg_out_w', 'sw_in_w', 'sw_out_w', 'gd_in_w', 'gd_out_w']
SEG_FIRST = [('hg_in_w', 0), ('hg_out_w', 0)]
SEG_REST = [('hg_in_w', 1), ('hg_out_w', 1), ('sw_in_w', 0), ('sw_out_w', 0), ('gd_in_w', 0), ('gd_out_w', 0)]
PACK_ALIGN = 16
ROPE_THETA = 10000.0
ADA_S = 3 * D // 4
SMALL_ROW = {'hg_onorm': (0, 256), 'sw_qnorm': (256, 64), 'sw_knorm': (320, 64), 'sw_sinks': (384, 16),
             'gd_a_log': (400, 16), 'gd_dt_bias': (416, 16), 'gd_onorm': (432, 128)}


def _pack_rows(arrs):
    flat = jnp.concatenate([a.reshape(-1, D) for a in arrs], axis=0)
    return jnp.pad(flat, ((0, -flat.shape[0] % PACK_ALIGN), (0, 0)))


def _unpack_rows(packed, shapes):
    out, off = [], 0
    for s in shapes:
        rows = 1
        for d in s:
            rows *= d
        rows //= D
        out.append(packed[..., off:off + rows, :].reshape(packed.shape[:-2] + tuple(s)))
        off += rows
    return out


def _pack_small(vals):
    row = jnp.concatenate([vals[k].reshape(-1) for k in SMALL_ROW])
    row = jnp.pad(row, (0, D - row.shape[0]))[None]
    return jnp.concatenate([vals['hgrn_lb'], vals['norm_g'], vals['gd_conv_w'].reshape(16, D), row,
                            jnp.zeros((7, D), F32)], axis=0)


def _sw_cols(w, inverse=False):
    def split(a, heads):
        shp = (a.shape[0], 2, heads, 32) if inverse else (a.shape[0], heads, 2, 32)
        return a.reshape(shp).transpose(0, 2, 1, 3).reshape(a.shape[0], heads * 64)
    return jnp.concatenate([split(w[:, 0:1024], 16), split(w[:, 1024:1280], 4), w[:, 1280:]], axis=1)


def kernel(x, c, positions, hgrn_lb, ada_w, ada_b, norm_g, hg_in_w, hg_out_w, hg_onorm, sw_in_w, sw_out_w, sw_qnorm, sw_knorm, sw_sinks, gd_in_w, gd_out_w, gd_conv_w, gd_a_log, gd_dt_bias, gd_onorm, loss_target, m_hgrn_lb, m_ada_w, m_ada_b, m_norm_g, m_hg_in_w, m_hg_out_w, m_hg_onorm, m_sw_in_w, m_sw_out_w, m_sw_qnorm, m_sw_knorm, m_sw_sinks, m_gd_in_w, m_gd_out_w, m_gd_conv_w, m_gd_a_log, m_gd_dt_bias, m_gd_onorm, v_hgrn_lb, v_ada_w, v_ada_b, v_norm_g, v_hg_in_w, v_hg_out_w, v_hg_onorm, v_sw_in_w, v_sw_out_w, v_sw_qnorm, v_sw_knorm, v_sw_sinks, v_gd_in_w, v_gd_out_w, v_gd_conv_w, v_gd_a_log, v_gd_dt_bias, v_gd_onorm):
    w_in = dict(hgrn_lb=hgrn_lb, ada_w=ada_w, ada_b=ada_b, norm_g=norm_g, hg_in_w=hg_in_w, hg_out_w=hg_out_w,
                hg_onorm=hg_onorm, sw_in_w=sw_in_w, sw_out_w=sw_out_w, sw_qnorm=sw_qnorm, sw_knorm=sw_knorm,
                sw_sinks=sw_sinks, gd_in_w=gd_in_w, gd_out_w=gd_out_w, gd_conv_w=gd_conv_w, gd_a_log=gd_a_log,
                gd_dt_bias=gd_dt_bias, gd_onorm=gd_onorm)
    m_in = dict(zip(WEIGHTS, (m_hgrn_lb, m_ada_w, m_ada_b, m_norm_g, m_hg_in_w, m_hg_out_w, m_hg_onorm, m_sw_in_w,
                              m_sw_out_w, m_sw_qnorm, m_sw_knorm, m_sw_sinks, m_gd_in_w, m_gd_out_w, m_gd_conv_w,
                              m_gd_a_log, m_gd_dt_bias, m_gd_onorm)))
    v_in = dict(zip(WEIGHTS, (v_hgrn_lb, v_ada_w, v_ada_b, v_norm_g, v_hg_in_w, v_hg_out_w, v_hg_onorm, v_sw_in_w,
                              v_sw_out_w, v_sw_qnorm, v_sw_knorm, v_sw_sinks, v_gd_in_w, v_gd_out_w, v_gd_conv_w,
                              v_gd_a_log, v_gd_dt_bias, v_gd_onorm)))
    ax, ay, ac = _place()
    chip = 2 * ax + ay
    bidx = 4 * ax + 2 * ay + ac
    t = x.shape[1]
    x0, target = x[0], loss_target[0]

    c_all = _allgather8(jnp.pad(c, ((0, 7), (0, 0)))).reshape(8, 8, D)[:, 0, :]
    ada_b_cols = lax.dynamic_slice(ada_b, (0, chip * ADA_S), (4, ADA_S)).reshape(4, 1, ADA_S)
    mod_sh = _ada_fwd(c_all, ada_w, ada_b_cols)
    mod_g = _allgather8(mod_sh.reshape(32, ADA_S)).reshape(4, 2, 4, 8, ADA_S)[:, 0]
    mod = lax.dynamic_index_in_dim(mod_g, bidx, axis=2, keepdims=False).transpose(1, 0, 2).reshape(4, 3 * D)
    shift = [mod[l:l + 1, 0:D] for l in range(4)]
    scale = [mod[l:l + 1, D:2 * D] for l in range(4)]
    gate = [mod[l:l + 1, 2 * D:3 * D] for l in range(4)]

    h8 = jnp.concatenate([hgrn_lb, jnp.full((4, D), -1e30, F32)], axis=0)
    lb_all = _lb_fwd(h8)
    freq = ROPE_THETA ** (-jnp.arange(0, 64, 2, dtype=F32) / 64)
    cs = _rope_table(positions.reshape(t, 1), jnp.tile(freq, 4)[None])

    seg_shapes = lambda seg: [w_in[k].shape[1:] for k, _ in seg]
    pack_seg = lambda src, seg: _pack_rows([src[k][i] for k, i in seg])
    cols_full = lambda a: a.transpose(1, 0, 2).reshape(a.shape[1], 4 * a.shape[2])
    hg_in0_k, hg_out0_k = _unpack_rows(_chip_allgather(pack_seg(w_in, SEG_FIRST).astype(BF16)), seg_shapes(SEG_FIRST))
    win, wout = [cols_full(hg_in0_k)], [hg_out0_k.reshape(D, D)]
    rest_shard = pack_seg(w_in, SEG_REST).astype(BF16)
    tn_in = [1024, 1280, 896, 1024]

    gq = jnp.stack([jnp.tile(sw_qnorm[0, :32], 16), jnp.tile(sw_qnorm[0, 32:], 16)])
    gk = jnp.stack([jnp.tile(sw_knorm[0, :32], 4), jnp.tile(sw_knorm[0, 32:], 4)])
    pad128 = lambda a: jnp.pad(a, ((0, 0), (0, HD - a.shape[1])))
    sinks, alog, dtb = pad128(sw_sinks), pad128(gd_a_log), pad128(gd_dt_bias)
    cw8 = jnp.pad(_chip_allgather(gd_conv_w[0]).transpose(1, 0, 2).reshape(4, GD_QKV), ((0, 4), (0, 0)))
    lbs = {0: lb_all[0:1], 3: lb_all[3:4]}

    xs, us, hs, ps, stss = [x0], [], [], [], []
    for l in range(4):
        u, h = _ln_mm(xs[l], norm_g[l:l + 1], scale[l], shift[l], win[l], tn_in[l])
        if l == 0:
            p, sts, rest_k = _hg_fwd(u, lbs[l], hg_onorm[0:1], gather=rest_shard)
            hg_in1_k, hg_out1_k, sw_in_k, sw_out_k, gd_in_k, gd_out_k = _unpack_rows(rest_k, seg_shapes(SEG_REST))
            win += [_sw_cols(cols_full(sw_in_k)), jnp.pad(cols_full(gd_in_k), ((0, 0), (0, GD_N - 6176))),
                    cols_full(hg_in1_k)]
            wout += [sw_out_k.reshape(D, D), gd_out_k.reshape(GD_VW, D), hg_out1_k.reshape(D, D)]
        elif l % 3 == 0:
            p, sts = _hg_fwd(u, lbs[l], hg_onorm[l // 3:l // 3 + 1])
        elif l % 3 == 1:
            p, sts = _sw_fwd(u, cs, gq, gk, sinks), None
        else:
            p, *sts = _gd_fwd(u, cw8, alog, dtb, gd_onorm)
        xs.append(_mm_res(p, wout[l], xs[l], gate[l]))
        us.append(u), hs.append(h), ps.append(p), stss.append(sts)
    lpart, dx = _loss_grad(xs[4], target)
    loss = lax.psum(lpart[0, 0], ("x", "y", "c"))

    by_chip = lambda g, cols: g.reshape(g.shape[0], 4, cols).transpose(1, 0, 2)
    g_small = {}
    d_in, d_out, dmod, dnorm_g, dlb8, dgo_hg = [None] * 4, [None] * 4, [None] * 4, [None] * 4, jnp.zeros((8, D), F32), {}
    for l in (3, 2, 1, 0):
        dp = _mm_scaled(dx, gate[l], wout[l].T, 1024)
        d_out[l], dgate = _outgrad(_mm_tn_acc(ps[l], dx, D if ps[l].shape[1] == D else 512), wout[l], gate[l])
        if l == 0:
            rest_parts = {('hg_in_w', 1): by_chip(d_in[3], D), ('hg_out_w', 1): d_out[3].reshape(4, D // 4, D),
                          ('sw_in_w', 0): by_chip(_sw_cols(d_in[1], inverse=True), SW_N // 4),
                          ('sw_out_w', 0): d_out[1].reshape(4, D // 4, D),
                          ('gd_in_w', 0): by_chip(d_in[2][:, :6176], 1544),
                          ('gd_out_w', 0): d_out[2].reshape(4, GD_VW // 4, D)}
            rest_packed = jnp.stack([_pack_rows([rest_parts[s][j] for s in SEG_REST]) for j in range(4)])
            du, dlb, dgo_hg[0], rest_recv = _hg_bwd(us[l], stss[l], dp, lbs[l], hg_onorm[0:1],
                                                    scatter=rest_packed.astype(BF16))
            dlb8 = lax.dynamic_update_slice(dlb8, dlb, (l, 0))
        elif l % 3 == 0:
            du, dlb, dgo_hg[l // 3] = _hg_bwd(us[l], stss[l], dp, lbs[l], hg_onorm[l // 3:l // 3 + 1])
            dlb8 = lax.dynamic_update_slice(dlb8, dlb, (l, 0))
        elif l % 3 == 1:
            du, dgq, dgk, dsk = _sw_bwd(us[l], cs, dp, gq, gk, sinks)
            g_small['sw_qnorm'] = jnp.concatenate([dgq[0].reshape(16, 32).sum(0), dgq[1].reshape(16, 32).sum(0)])
            g_small['sw_knorm'] = jnp.concatenate([dgk[0].reshape(4, 32).sum(0), dgk[1].reshape(4, 32).sum(0)])
            g_small['sw_sinks'] = dsk[0, :16]
        else:
            du, dcw, dalog, ddtb, g_small['gd_onorm'] = _gd_bwd(us[l], *stss[l], dp, cw8, alog, dtb, gd_onorm)
            g_small['gd_conv_w'], g_small['gd_a_log'], g_small['gd_dt_bias'] = dcw[:4], dalog[0, :16], ddtb[0, :16]
        d_in[l] = _mm_tn_acc(hs[l], du, 896 if l == 2 else 512)
        dx, dvec = _inproj_bwd(du, win[l].T, xs[l], dx, norm_g[l:l + 1], scale[l], shift[l])
        dnorm_g[l] = dvec[0:1]
        dmod[l] = jnp.concatenate([dvec[2:3], dvec[1:2], dgate[0:1]], axis=1)
    grad_x = dx[None]

    g_small['hgrn_lb'] = _lb_bwd(h8, dlb8)[0:4]
    g_small['norm_g'] = jnp.concatenate(dnorm_g, axis=0)
    g_small['hg_onorm'] = jnp.concatenate([dgo_hg[0], dgo_hg[1]], axis=0)
    gs_all = _allgather8(_pack_small(g_small))
    gs = _sum_rows([gs_all[32 * d:32 * (d + 1)] for d in range(8)])

    def small_view(packed, k):
        if k == 'hgrn_lb':
            return packed[0:4]
        if k == 'norm_g':
            return packed[4:8]
        off, size = SMALL_ROW[k]
        return packed[24, off:off + size].reshape(w_in[k].shape)

    conv_sl = lambda full: lax.dynamic_slice(full.reshape(4, GD_QKV), (0, chip * D), (4, D))
    out = {}

    def put(k, res, shape):
        for name, r in zip(('grad_', 'delta_', 'new_m_', 'new_v_'), res):
            out[name + k] = r.reshape(shape)

    zero_conv = dict(gd_conv_w=jnp.zeros((4, GD_QKV), F32))
    small_names = ['hgrn_lb', 'norm_g'] + list(SMALL_ROW)
    res = _adamw(_pack_small({**{k: w_in[k] for k in small_names}, **zero_conv}), (gs,),
                 _pack_small({**{k: m_in[k] for k in small_names}, **zero_conv}),
                 _pack_small({**{k: v_in[k] for k in small_names}, **zero_conv}))
    for k in small_names:
        put(k, [small_view(r, k) for r in res], w_in[k].shape)
    put('gd_conv_w', _adamw(gd_conv_w[0], (conv_sl(gs[8:24]),), m_in['gd_conv_w'][0], v_in['gd_conv_w'][0]),
        gd_conv_w.shape)

    dm = _allgather8(jnp.pad(jnp.concatenate(dmod, axis=0), ((0, 4), (0, 0)))).reshape(8, 8, 3 * D)[:, :4]
    dm = dm.transpose(1, 0, 2)
    g_ada_w, g_ada_b = _ada_bwd(c_all, lax.dynamic_slice(dm, (0, 0, chip * ADA_S), (4, 8, ADA_S)), dm)
    put('ada_w', _adamw(ada_w.reshape(4 * D, ADA_S), (g_ada_w.reshape(4 * D, ADA_S),),
                        m_in['ada_w'].reshape(4 * D, ADA_S), v_in['ada_w'].reshape(4 * D, ADA_S)), ada_w.shape)
    put('ada_b', _adamw(ada_b, (g_ada_b.reshape(4, 3 * D),), m_in['ada_b'], v_in['ada_b']), ada_b.shape)

    first_parts = {('hg_in_w', 0): by_chip(d_in[0], D), ('hg_out_w', 0): d_out[0].reshape(4, D // 4, D)}
    first_packed = jnp.stack([_pack_rows([first_parts[s][j] for s in SEG_FIRST]) for j in range(4)])
    first_recv = _chip_scatter(first_packed.astype(BF16))
    own = lambda packed: lax.dynamic_index_in_dim(packed, chip, axis=0, keepdims=False)
    half = jnp.concatenate([_sum_rows([own(first_packed), first_recv[0], first_recv[1], first_recv[2]]),
                            _sum_rows([own(rest_packed), rest_recv[0], rest_recv[1], rest_recv[2]])], axis=0)
    other = _sibling_swap(half)
    pack_all = lambda src: jnp.concatenate([pack_seg(src, SEG_FIRST), pack_seg(src, SEG_REST)], axis=0)
    res = _adamw(pack_all(w_in), (half, other), pack_all(m_in), pack_all(v_in))
    n_first = first_packed.shape[1]
    for name, r in zip(('grad_', 'delta_', 'new_m_', 'new_v_'), res):
        pieces = dict(zip(SEG_FIRST, _unpack_rows(r[:n_first], seg_shapes(SEG_FIRST))))
        pieces.update(zip(SEG_REST, _unpack_rows(r[n_first:], seg_shapes(SEG_REST))))
        for k in BIG:
            out[name + k] = jnp.stack([pieces[(k, i)] for i in range(w_in[k].shape[0])])

    return (loss, grad_x, *[out[p + k] for p in ('grad_', 'delta_', 'new_m_', 'new_v_') for k in WEIGHTS])
```

```python
import functools

import jax
import jax.numpy as jnp
from jax import lax
from jax.experimental import pallas as pl
from jax.experimental.pallas import tpu as pltpu

F32 = jnp.float32
BF16 = jnp.bfloat16
D = 1024
EPS = 1e-6
CHUNK = 64
SUB = 32
HG_H = 8
HD = 128
VMEM_LIMIT = 56 * 1024 * 1024


def _cparams(sem=None):
    return pltpu.CompilerParams(dimension_semantics=sem, vmem_limit_bytes=VMEM_LIMIT)


def _dot(a, b, ca, cb, prec=None):
    return lax.dot_general(a, b, (((ca,), (cb,)), ((), ())), precision=prec, preferred_element_type=F32)


def _mm(a, b):
    return _dot(a.astype(BF16), b.astype(BF16), 1, 0)


def _mm_nt(a, b):
    return _dot(a.astype(BF16), b.astype(BF16), 1, 1)


def _mm_tn(a, b):
    return _dot(a.astype(BF16), b.astype(BF16), 0, 0)


def _mm_f32(a, b):
    return _dot(a, b, 1, 0, lax.Precision.HIGHEST)


def _silu(x):
    return x * jax.nn.sigmoid(x)


def _cumsum_impl(x):
    row = lax.broadcasted_iota(jnp.int32, x.shape, 0)
    s = 1
    while s < x.shape[0]:
        x = x + jnp.where(row >= s, pltpu.roll(x, s, 0), 0.0)
        s *= 2
    return x


@jax.custom_vjp
def _cumsum_rows(x):
    return _cumsum_impl(x)


_cumsum_rows.defvjp(lambda x: (_cumsum_impl(x), None),
                    lambda _, g: (jnp.sum(g, axis=0, keepdims=True) - _cumsum_impl(g) + g,))


def _roll_rows(x, shift):
    n = x.shape[0]

    @jax.custom_vjp
    def f(a):
        return pltpu.roll(a, shift, 0)

    f.defvjp(lambda a: (pltpu.roll(a, shift, 0), None), lambda _, g: (pltpu.roll(g, n - shift, 0),))
    return f(x)


def _hg_chunk(q_raw, f_pre, v, z, st, lb, go):
    c = q_raw.shape[0]
    nsub = c // SUB
    lf = jnp.log(lb + (1.0 - lb) * jax.nn.sigmoid(f_pre))
    k = (1.0 - lb) * jax.nn.sigmoid(-f_pre)
    q = _silu(q_raw)
    b = _cumsum_rows(lf)
    rowf = lax.broadcasted_iota(jnp.int32, lf.shape, 0)
    bmid = [jnp.sum(jnp.where(rowf == SUB * i + SUB // 2, b, 0.0), axis=0, keepdims=True) for i in range(nsub)]
    row = lax.broadcasted_iota(jnp.int32, (c, 1), 0)
    ref = sum(jnp.where((row >= SUB * i) & (row < SUB * (i + 1)), bmid[i], 0.0) for i in range(nsub))
    qt = q * jnp.exp(b - ref)
    kall = jnp.concatenate(
        [k * jnp.exp(jnp.where(row < SUB * (i + 1), bmid[i] - b, -jnp.inf)) for i in range(nsub)], axis=0)
    v4 = jnp.concatenate([v] * nsub, axis=0)
    b_last = jnp.sum(lf, axis=0, keepdims=True)
    qb = q * jnp.exp(b)
    kd = k * jnp.exp(b_last - b)
    e_last = jnp.exp(b_last)
    tq = lax.broadcasted_iota(jnp.int32, (c, nsub * c), 0)
    cq = lax.broadcasted_iota(jnp.int32, (c, nsub * c), 1)
    m_all = ((cq // c) == (tq // SUB)) & ((cq % c) <= tq)
    hs = lambda a: jnp.split(a, HG_H, axis=1)
    qt_h, kall_h, v4_h, qb_h, kd_h, v_h, z_h, el_h = map(hs, (qt, kall, v4, qb, kd, v, z, e_last))
    st_h = jnp.split(st, HG_H, axis=0)
    heads = range(HG_H)
    pm = [jnp.where(m_all, _mm_nt(qt_h[h], kall_h[h]), 0.0) for h in heads]
    inter = [_mm_nt(qb_h[h], st_h[h]) for h in heads]
    o_h = [_mm(pm[h], v4_h[h]) + inter[h] for h in heads]
    upd = [_mm_tn(v_h[h], kd_h[h]) for h in heads]
    st_out = [el_h[h] * st_h[h] + upd[h] for h in heads]
    y_h = [o_h[h] * lax.rsqrt(jnp.mean(o_h[h] * o_h[h], axis=1, keepdims=True) + EPS) * go for h in heads]
    p_out = [y_h[h] * _silu(z_h[h]) for h in heads]
    return jnp.concatenate(p_out, axis=1), jnp.concatenate(st_out, axis=0)


def _hg_fwd(u, lb, go, gather=None):
    t = u.shape[0]
    n = t // CHUNK

    def body(u_ref, lb_ref, go_ref, *rest):
        if gather is None:
            p_ref, sts_ref, st_ref = rest
        else:
            shard_ref, p_ref, sts_ref, all_ref, st_ref, *sems = rest
            start, wait = _gather_plan(shard_ref, all_ref, *sems)
            pl.when(pl.program_id(0) == 0)(start)

        @pl.when(pl.program_id(0) == 0)
        def _():
            st_ref[...] = jnp.zeros_like(st_ref)

        st = st_ref[...]
        sts_ref[0] = st
        p, st_next = _hg_chunk(u_ref[:, 0:D], u_ref[:, D:2 * D], u_ref[:, 2 * D:3 * D], u_ref[:, 3 * D:4 * D],
                               st, lb_ref[...], go_ref[...])
        p_ref[...] = p.astype(BF16)
        st_ref[...] = st_next
        if gather is not None:
            pl.when(pl.program_id(0) == n - 1)(wait)

    more = gather is not None
    return pl.pallas_call(
        body, name="hg_fwd_gather" if more else "hg_fwd", grid=(n,),
        in_specs=[pl.BlockSpec((CHUNK, 4 * D), lambda i: (i, 0)),
                  pl.BlockSpec((1, D), lambda i: (0, 0)),
                  pl.BlockSpec((1, HD), lambda i: (0, 0))] + [ANY] * more,
        out_specs=[pl.BlockSpec((CHUNK, D), lambda i: (i, 0)),
                   pl.BlockSpec((1, HG_H * HD, HD), lambda i: (i, 0, 0))] + [ANY] * more,
        out_shape=[jax.ShapeDtypeStruct((t, D), BF16), jax.ShapeDtypeStruct((n, HG_H * HD, HD), F32)]
        + ([jax.ShapeDtypeStruct((4,) + gather.shape, gather.dtype)] if more else []),
        scratch_shapes=[pltpu.VMEM((HG_H * HD, HD), F32)] + GATHER_SEMS * more,
        compiler_params=_cparams(("arbitrary",)),
    )(u, lb, go, *([gather] * more))


def _hg_bwd(u, sts, dp, lb, go, scatter=None):
    t = u.shape[0]
    n = t // CHUNK

    def body(u_ref, sts_ref, dp_ref, lb_ref, go_ref, *rest):
        if scatter is None:
            du_ref, dlb_ref, dgo_ref, dst_ref = rest
        else:
            parts_ref, du_ref, dlb_ref, dgo_ref, recv_ref, dst_ref, *sems = rest
            start, wait = _scatter_plan(parts_ref, recv_ref, *sems)
            pl.when(pl.program_id(0) == 0)(start)

        @pl.when(pl.program_id(0) == 0)
        def _():
            dst_ref[...] = jnp.zeros_like(dst_ref)
            dlb_ref[...] = jnp.zeros_like(dlb_ref)
            dgo_ref[...] = jnp.zeros_like(dgo_ref)

        _, vjp = jax.vjp(_hg_chunk, u_ref[:, 0:D], u_ref[:, D:2 * D], u_ref[:, 2 * D:3 * D], u_ref[:, 3 * D:4 * D],
                         sts_ref[0], lb_ref[...], go_ref[...])
        dq, df, dv, dz, dst, dlb, dgo = vjp((dp_ref[...].astype(F32), dst_ref[...]))
        du_ref[:, 0:D] = dq.astype(BF16)
        du_ref[:, D:2 * D] = df.astype(BF16)
        du_ref[:, 2 * D:3 * D] = dv.astype(BF16)
        du_ref[:, 3 * D:4 * D] = dz.astype(BF16)
        dst_ref[...] = dst
        dlb_ref[...] += dlb
        dgo_ref[...] += dgo
        if scatter is not None:
            pl.when(pl.program_id(0) == n - 1)(wait)

    rev = lambda i: (n - 1 - i, 0)
    more = scatter is not None
    return pl.pallas_call(
        body, name="hg_bwd_scatter" if more else "hg_bwd", grid=(n,),
        in_specs=[pl.BlockSpec((CHUNK, 4 * D), rev),
                  pl.BlockSpec((1, HG_H * HD, HD), lambda i: (n - 1 - i, 0, 0)),
                  pl.BlockSpec((CHUNK, D), rev),
                  pl.BlockSpec((1, D), lambda i: (0, 0)),
                  pl.BlockSpec((1, HD), lambda i: (0, 0))] + [ANY] * more,
        out_specs=[pl.BlockSpec((CHUNK, 4 * D), rev),
                   pl.BlockSpec((1, D), lambda i: (0, 0)),
                   pl.BlockSpec((1, HD), lambda i: (0, 0))] + [ANY] * more,
        out_shape=[jax.ShapeDtypeStruct((t, 4 * D), BF16), jax.ShapeDtypeStruct((1, D), F32),
                   jax.ShapeDtypeStruct((1, HD), F32)]
        + ([jax.ShapeDtypeStruct((3,) + scatter.shape[1:], scatter.dtype)] if more else []),
        scratch_shapes=[pltpu.VMEM((HG_H * HD, HD), F32)] + SCATTER_SEMS * more,
        compiler_params=_cparams(("arbitrary",)),
    )(u, sts, dp, lb, go, *([scatter] * more))


GD_VH = 16
GD_QKH = 8
GD_QKV = 4096
GD_VW = 2048
GD_N = GD_QKV + GD_VW + HD
GD_GRP = 4
GD_SOLVE = (GD_VH // GD_GRP, GD_GRP * CHUNK, 2 * HD)
HALO = 8


def _mm_high(a, b):
    return _dot(a, b, 1, 0, lax.Precision.HIGH)


def _lane_pick(a, h):
    lane = lax.broadcasted_iota(jnp.int32, a.shape, 1)
    return jnp.sum(jnp.where(lane == h, a, 0.0), axis=1, keepdims=True)


def _l2n(x):
    return x * lax.rsqrt(jnp.sum(x * x, axis=1, keepdims=True) + EPS)


def _solve_fwd(a_mats, rhss):
    n = a_mats[0].shape[0]
    r_i, c_i = lax.broadcasted_iota(jnp.int32, (n, n), 0), lax.broadcasted_iota(jnp.int32, (n, n), 1)
    same = lambda nb: (r_i // nb) == (c_i // nb)
    eye = (r_i == c_i).astype(F32)
    d0s = [jnp.where(same(8), a, 0.0) for a in a_mats]
    d2s = [_mm(d, d) for d in d0s]
    tinvs = [eye - d for d in d0s]
    tinvs = [t + _mm(t, d2) for t, d2 in zip(tinvs, d2s)]
    d4s = [_mm(d2, d2) for d2 in d2s]
    tinvs = [t + _mm(t, d4) for t, d4 in zip(tinvs, d4s)]
    nb = 16
    while nb <= CHUNK:
        tls = [_mm(t, jnp.where(same(nb) & ~same(nb // 2), a, 0.0)) for t, a in zip(tinvs, a_mats)]
        tinvs = [t - _mm(tl, t) for t, tl in zip(tinvs, tls)]
        nb *= 2
    return tinvs, [_mm(t, r) for t, r in zip(tinvs, rhss)]


def _solve_bwd(res, dx):
    tinv, x = res
    drhs = _dot(tinv, dx, 0, 0, lax.Precision.HIGH)
    return -_dot(drhs, x, 1, 1, lax.Precision.HIGH), drhs


@jax.custom_vjp
def _solved(a_mat, rhs, tinv, x):
    return x


_solved.defvjp(lambda a_mat, rhs, tinv, x: (x, (tinv, x)),
               lambda res, dx: _solve_bwd(res, dx) + (jnp.zeros_like(res[0]), jnp.zeros_like(res[1])))


def _gd_chunk(xh, x, z, ab, st, cw, alog, dtb, go, solve):
    c = x.shape[0]
    xa = jnp.concatenate([xh, x], axis=0)
    sh = [jnp.split(_roll_rows(xa, 3 - j), [HALO], axis=0)[1] for j in range(3)]
    qkv = _silu(cw[0:1] * sh[0] + cw[1:2] * sh[1] + cw[2:3] * sh[2] + cw[3:4] * x)
    q_all, k_all, v_all = jnp.split(qkv, [1024, 2048], axis=1)
    lane = lax.broadcasted_iota(jnp.int32, (c, HD), 1)
    a_part = jnp.where(lane < GD_VH, ab, 0.0)
    g_all = -jnp.exp(alog) * jax.nn.softplus(a_part + dtb)
    d_all = _cumsum_rows(g_all)
    dl_all = jnp.sum(g_all, axis=0, keepdims=True)
    beta_all = jax.nn.sigmoid(ab)
    gc = GD_GRP * c
    r_i, c_i = lax.broadcasted_iota(jnp.int32, (gc, gc), 0), lax.broadcasted_iota(jnp.int32, (gc, gc), 1)
    same_head = (r_i // c) == (c_i // c)
    tri_g, strict_g = same_head & (c_i <= r_i), same_head & (c_i < r_i)
    qs =jnp.split(q_all, GD_QKH, axis=1)
    ks = jnp.split(k_all, GD_QKH, axis=1)
    vs = jnp.split(v_all, GD_VH, axis=1)
    zs = jnp.split(z, GD_VH, axis=1)
    sts = jnp.split(st, GD_VH, axis=0)
    qn = [_l2n(a) * (HD ** -0.5) for a in qs]
    kn = [_l2n(a) for a in ks]
    p_out, st_out, pre = [], [], []
    for g in range(GD_VH // GD_GRP):
        heads = range(GD_GRP * g, GD_GRP * (g + 1))
        stack = lambda f: jnp.concatenate([f(h) for h in heads], axis=0)
        q_, k_, v_ = stack(lambda h: qn[h // 2]), stack(lambda h: kn[h // 2]), stack(lambda h: vs[h])
        dcol = stack(lambda h: _lane_pick(d_all, h))
        bcol = stack(lambda h: _lane_pick(beta_all, GD_VH + h))
        dlast = stack(lambda h: jnp.broadcast_to(_lane_pick(dl_all, h), (c, 1)))
        drow = jnp.sum(jnp.broadcast_to(dcol, (gc, HD)).T, axis=0, keepdims=True) * (1.0 / HD)
        dec = jnp.exp(jnp.where(tri_g, dcol - drow, -jnp.inf))
        kb = k_ * bcol
        a_mat = jnp.where(strict_g, _mm_nt(kb, k_) * dec, 0.0)
        pre.append((heads, q_, k_, dcol, dlast, dec, a_mat, jnp.concatenate([v_ * bcol, kb * jnp.exp(dcol)], axis=1)))
    xsols = solve([e[6] for e in pre], [e[7] for e in pre])
    heads_of = [e[0] for e in pre]
    per_head = lambda a: jnp.split(a, GD_GRP, axis=0)
    uw = [jnp.split(x, 2, axis=1) for x in xsols]
    ws = [[_mm(wh, sts[h]) for wh, h in zip(per_head(w_), heads)] for (_, w_), heads in zip(uw, heads_of)]
    v_new = [u_ - jnp.concatenate(w, axis=0) for (u_, _), w in zip(uw, ws)]
    qk = [_mm_nt(e[1], e[2]) * e[5] for e in pre]
    qs_ = [[_mm(qh, sts[h]) for qh, h in zip(per_head(e[1] * jnp.exp(e[3])), e[0])] for e in pre]
    o_g = [_mm(a, vn) + jnp.concatenate(b, axis=0) for a, vn, b in zip(qk, v_new, qs_)]
    upd = [[_mm_tn(kh, vh) for kh, vh in zip(per_head(e[2] * jnp.exp(e[4] - e[3])), per_head(vn))]
           for e, vn in zip(pre, v_new)]
    for heads, og, up in zip(heads_of, o_g, upd):
        for h, o, u_st in zip(heads, per_head(og), up):
            st_out.append(sts[h] * jnp.exp(_lane_pick(dl_all, h)) + u_st)
            y = o * lax.rsqrt(jnp.mean(o * o, axis=1, keepdims=True) + EPS) * go
            p_out.append(y * _silu(zs[h]))
    return jnp.concatenate(p_out, axis=1), jnp.concatenate(st_out, axis=0)


def _gd_specs(n, rev):
    ci = (lambda i: n - 1 - i) if rev else (lambda i: i)
    return [pl.BlockSpec((HALO, GD_QKV), lambda i: (jnp.maximum(ci(i) * (CHUNK // HALO) - 1, 0), 0)),
            pl.BlockSpec((CHUNK, GD_N), lambda i: (ci(i), 0))]


def _gd_load(uh_ref, u_ref, first):
    xh = jnp.where(first, 0.0, uh_ref[...])
    return xh, u_ref[:, 0:GD_QKV], u_ref[:, GD_QKV:GD_QKV + GD_VW], u_ref[:, GD_QKV + GD_VW:GD_N]


def _gd_fwd(u, cw, alog, dtb, go):
    t = u.shape[0]
    n = t // CHUNK
    small = lambda r, w: pl.BlockSpec((r, w), lambda i: (0, 0))

    def body(uh_ref, u_ref, cw_ref, alog_ref, dtb_ref, go_ref, p_ref, sts_ref, tinv_ref, xsol_ref, st_ref):
        i = pl.program_id(0)

        @pl.when(i == 0)
        def _():
            st_ref[...] = jnp.zeros_like(st_ref)

        def solve(a_mats, rhss):
            tinvs, xsols = _solve_fwd(a_mats, rhss)
            for g, (tinv, xsol) in enumerate(zip(tinvs, xsols)):
                tinv_ref[0, g] = tinv
                xsol_ref[0, g] = xsol
            return xsols

        st = st_ref[...]
        sts_ref[0] = st
        p, st_next = _gd_chunk(*_gd_load(uh_ref, u_ref, i == 0), st, cw_ref[...], alog_ref[...], dtb_ref[...],
                               go_ref[...], solve)
        p_ref[...] = p.astype(BF16)
        st_ref[...] = st_next

    return pl.pallas_call(
        body, name="gd_fwd", grid=(n,),
        in_specs=_gd_specs(n, False) + [small(8, GD_QKV), small(1, HD), small(1, HD), small(1, HD)],
        out_specs=[pl.BlockSpec((CHUNK, GD_VW), lambda i: (i, 0)),
                   pl.BlockSpec((1, GD_VH * HD, HD), lambda i: (i, 0, 0)),
                   pl.BlockSpec((1,) + GD_SOLVE, lambda i: (i, 0, 0, 0)),
                   pl.BlockSpec((1,) + GD_SOLVE, lambda i: (i, 0, 0, 0))],
        out_shape=[jax.ShapeDtypeStruct((t, GD_VW), BF16), jax.ShapeDtypeStruct((n, GD_VH * HD, HD), F32),
                   jax.ShapeDtypeStruct((n,) + GD_SOLVE, F32), jax.ShapeDtypeStruct((n,) + GD_SOLVE, F32)],
        scratch_shapes=[pltpu.VMEM((GD_VH * HD, HD), F32)],
        compiler_params=_cparams(("arbitrary",)),
    )(u, u, cw, alog, dtb, go)


def _gd_bwd(u, sts, tinvs, xsols, dp, cw, alog, dtb, go):
    t = u.shape[0]
    n = t // CHUNK
    small = lambda r, w: pl.BlockSpec((r, w), lambda i: (0, 0))

    def body(uh_ref, u_ref, sts_ref, tinv_ref, xsol_ref, dp_ref, cw_ref, alog_ref, dtb_ref, go_ref,
             du_ref, dcw_ref, dalog_ref, ddtb_ref, dgo_ref, dst_ref, dhalo_ref):
        i = pl.program_id(0)

        @pl.when(i == 0)
        def _():
            for r in (dst_ref, dhalo_ref, dcw_ref, dalog_ref, ddtb_ref, dgo_ref):
                r[...] = jnp.zeros_like(r)

        solve = lambda a_mats, rhss: [_solved(a, r, tinv_ref[0, g], xsol_ref[0, g])
                                      for g, (a, r) in enumerate(zip(a_mats, rhss))]
        chunk = functools.partial(_gd_chunk, solve=solve)
        _, vjp = jax.vjp(chunk, *_gd_load(uh_ref, u_ref, i == n - 1), sts_ref[0], cw_ref[...], alog_ref[...],
                         dtb_ref[...], go_ref[...])
        dxh, dx, dz, dab, dst, dcw, dalog, ddtb, dgo = vjp((dp_ref[...].astype(F32), dst_ref[...]))
        tail = jnp.concatenate([jnp.zeros((CHUNK - HALO, GD_QKV), F32), dhalo_ref[...]], axis=0)
        du_ref[:, 0:GD_QKV] = (dx + tail).astype(BF16)
        du_ref[:, GD_QKV:GD_QKV + GD_VW] = dz.astype(BF16)
        du_ref[:, GD_QKV + GD_VW:GD_N] = dab.astype(BF16)
        dhalo_ref[...] = dxh
        dst_ref[...] = dst
        dcw_ref[...] += dcw
        dalog_ref[...] += dalog
        ddtb_ref[...] += ddtb
        dgo_ref[...] += dgo

    return pl.pallas_call(
        body, name="gd_bwd", grid=(n,),
        in_specs=_gd_specs(n, True) + [pl.BlockSpec((1, GD_VH * HD, HD), lambda i: (n - 1 - i, 0, 0)),
                                       pl.BlockSpec((1,) + GD_SOLVE, lambda i: (n - 1 - i, 0, 0, 0)),
                                       pl.BlockSpec((1,) + GD_SOLVE, lambda i: (n - 1 - i, 0, 0, 0)),
                                       pl.BlockSpec((CHUNK, GD_VW), lambda i: (n - 1 - i, 0)),
                                       small(8, GD_QKV), small(1, HD), small(1, HD), small(1, HD)],
        out_specs=[pl.BlockSpec((CHUNK, GD_N), lambda i: (n - 1 - i, 0)),
                   small(8, GD_QKV), small(1, HD), small(1, HD), small(1, HD)],
        out_shape=[jax.ShapeDtypeStruct((t, GD_N), BF16), jax.ShapeDtypeStruct((8, GD_QKV), F32)]
        + [jax.ShapeDtypeStruct((1, HD), F32)] * 3,
        scratch_shapes=[pltpu.VMEM((GD_VH * HD, HD), F32), pltpu.VMEM((HALO, GD_QKV), F32)],
        compiler_params=_cparams(("arbitrary",)),
    )(u, u, sts, tinvs, xsols, dp, cw, alog, dtb, go)


SW_B = 128
SW_H = 16
SW_G = 4
SW_N = 2560
SW_KV0 = 1024


def _blockdiag(n, blk):
    r = lax.broadcasted_iota(jnp.int32, (n, n), 0) // blk
    c = lax.broadcasted_iota(jnp.int32, (n, n), 1) // blk
    return (r == c).astype(F32)


def _sw_normrope(x, g1, g2, cos, sin):
    w = x.shape[1] // 2
    x1, x2 = jnp.split(x, 2, axis=1)
    ms = _mm_high(x1 * x1 + x2 * x2, _blockdiag(w, 32)) * (1.0 / 64.0)
    rinv = lax.rsqrt(ms + EPS)
    n1, n2 = x1 * rinv * g1, x2 * rinv * g2
    return jnp.concatenate([n1 * cos - n2 * sin, n2 * cos + n1 * sin], axis=1)


def _sw_block(q, kvp, kvc, z, csp, csc, gq, gk, sinks, has_prev):
    b = q.shape[0]
    cos_c, sin_c = jnp.split(csc, 2, axis=1)
    cos_p, sin_p = jnp.split(csp, 2, axis=1)
    tile4 = lambda a: jnp.concatenate([a] * 4, axis=1)
    qh = _sw_normrope(q, gq[0:1], gq[1:2], tile4(cos_c), tile4(sin_c))
    kp, vp = jnp.split(kvp, 2, axis=1)
    kc, vc = jnp.split(kvc, 2, axis=1)
    kh = jnp.concatenate([_sw_normrope(kp, gk[0:1], gk[1:2], cos_p, sin_p),
                          _sw_normrope(kc, gk[0:1], gk[1:2], cos_c, sin_c)], axis=0)
    vv = jnp.concatenate([vp, vc], axis=0)
    q1, q2 = jnp.split(qh, 2, axis=1)
    q1g, q2g = jnp.split(q1, SW_G, axis=1), jnp.split(q2, SW_G, axis=1)
    own = lax.broadcasted_iota(jnp.int32, (4 * b, b), 1) <= lax.broadcasted_iota(jnp.int32, (4 * b, b), 0) % b
    ri = lax.broadcasted_iota(jnp.int32, (256, 256), 0)
    ci = lax.broadcasted_iota(jnp.int32, (256, 256), 1)
    row_head = lax.broadcasted_iota(jnp.int32, (4 * b, 256), 0) // b
    lane_q = lax.broadcasted_iota(jnp.int32, (4 * b, 256), 1)
    q_sel = (lane_q % 128) // 32 == row_head
    o_sel = lane_q // 64 == row_head
    groups = range(SW_G)
    ek = [((ri // 128 == ci // 128) & ((ri % 128) // 32 == g) & (ri % 32 == ci % 32)).astype(F32) for g in groups]
    ev = [((ri // 64 == g) & (ri % 64 == ci % 64)).astype(F32) for g in groups]
    kx = [_mm(kh, ek[g]) for g in groups]
    vx = [_mm(vv, ev[g]) for g in groups]
    q4 = [jnp.where(q_sel, jnp.concatenate([jnp.concatenate([q1g[g], q2g[g]], axis=1)] * 4, axis=0), 0.0)
          for g in groups]
    sink = [jnp.concatenate([jnp.broadcast_to(_lane_pick(sinks, 4 * g + j), (b, 1)) for j in range(4)], axis=0)
            for g in groups]
    sc = [jnp.split(_mm_nt(q4[g], kx[g]) * (64 ** -0.5), 2, axis=1) for g in groups]
    s = [jnp.where(own, sc[g][1], jnp.where(has_prev, sc[g][0], -jnp.inf)) for g in groups]
    top = [jnp.max(s[g]) for g in groups]
    soft = [top[g] + 8.0 * jnp.log(jnp.sum(jnp.exp((s[g] - top[g]) * 0.125), axis=1, keepdims=True)) for g in groups]
    m = [lax.stop_gradient(jnp.maximum(soft[g], sink[g])) for g in groups]
    p = [jnp.exp(s[g] - m[g]) for g in groups]
    pn = [p[g] / (jnp.sum(p[g], axis=1, keepdims=True) + jnp.exp(sink[g] - m[g])) for g in groups]
    pn2 = [jnp.concatenate([jnp.where(own, 0.0, pn[g]), jnp.where(own, pn[g], 0.0)], axis=1) for g in groups]
    o4 = [jnp.split(jnp.where(o_sel, _mm(pn2[g], vx[g]), 0.0), 4, axis=0) for g in groups]
    o_out = [o4[g][0] + o4[g][1] + o4[g][2] + o4[g][3] for g in groups]
    return jnp.concatenate(o_out, axis=1) * _silu(z)


def _sw_specs(n, rev):
    ci = (lambda i: n - 1 - i) if rev else (lambda i: i)
    prev = lambda i: jnp.maximum(ci(i) - 1, 0)
    return [pl.BlockSpec((SW_B, SW_N), lambda i: (ci(i), 0)),
            pl.BlockSpec((SW_B, 512), lambda i: (prev(i), SW_KV0 // 512)),
            pl.BlockSpec((SW_B, 256), lambda i: (ci(i), 0)),
            pl.BlockSpec((SW_B, 256), lambda i: (prev(i), 0)),
            pl.BlockSpec((2, 512), lambda i: (0, 0)), pl.BlockSpec((2, 128), lambda i: (0, 0)),
            pl.BlockSpec((1, 128), lambda i: (0, 0))]


def _sw_args(u_ref, kvp_ref, csc_ref, csp_ref, gq_ref, gk_ref, sk_ref, has_prev):
    return (u_ref[:, 0:D], kvp_ref[...], u_ref[:, SW_KV0:SW_KV0 + 512], u_ref[:, SW_KV0 + 512:SW_N],
            csp_ref[...], csc_ref[...], gq_ref[...], gk_ref[...], sk_ref[...], has_prev)


def _sw_fwd(u, cs, gq, gk, sinks):
    t = u.shape[0]
    n = t // SW_B

    def body(u_ref, kvp_ref, csc_ref, csp_ref, gq_ref, gk_ref, sk_ref, p_ref):
        has_prev = pl.program_id(0) > 0
        p_ref[...] = _sw_block(*_sw_args(u_ref, kvp_ref, csc_ref, csp_ref, gq_ref, gk_ref, sk_ref, has_prev)
                               ).astype(BF16)

    return pl.pallas_call(
        body, name="sw_fwd", grid=(n,), in_specs=_sw_specs(n, False),
        out_specs=pl.BlockSpec((SW_B, D), lambda i: (i, 0)),
        out_shape=jax.ShapeDtypeStruct((t, D), BF16),
        compiler_params=_cparams(("arbitrary",)),
    )(u, u, cs, cs, gq, gk, sinks)


def _sw_bwd(u, cs, dp, gq, gk, sinks):
    t = u.shape[0]
    n = t // SW_B

    def body(u_ref, kvp_ref, csc_ref, csp_ref, gq_ref, gk_ref, sk_ref, dp_ref,
             du_ref, dgq_ref, dgk_ref, dsk_ref, dkv_ref):
        i = pl.program_id(0)

        @pl.when(i == 0)
        def _():
            for r in (dkv_ref, dgq_ref, dgk_ref, dsk_ref):
                r[...] = jnp.zeros_like(r)

        has_prev = i < n - 1
        args = _sw_args(u_ref, kvp_ref, csc_ref, csp_ref, gq_ref, gk_ref, sk_ref, has_prev)
        fn = lambda q, kvp, kvc, z, gq_, gk_, sk_: _sw_block(q, kvp, kvc, z, args[4], args[5], gq_, gk_, sk_, has_prev)
        _, vjp = jax.vjp(fn, args[0], args[1], args[2], args[3], args[6], args[7], args[8])
        dq, dkvp, dkvc, dz, dgq, dgk, dsk = vjp(dp_ref[...].astype(F32))
        du_ref[:, 0:D] = dq.astype(BF16)
        du_ref[:, SW_KV0:SW_KV0 + 512] = (dkvc + dkv_ref[...]).astype(BF16)
        du_ref[:, SW_KV0 + 512:SW_N] = dz.astype(BF16)
        dkv_ref[...] = dkvp
        dgq_ref[...] += dgq
        dgk_ref[...] += dgk
        dsk_ref[...] += dsk

    small = lambda r, w: pl.BlockSpec((r, w), lambda i: (0, 0))
    return pl.pallas_call(
        body, name="sw_bwd", grid=(n,),
        in_specs=_sw_specs(n, True) + [pl.BlockSpec((SW_B, D), lambda i: (n - 1 - i, 0))],
        out_specs=[pl.BlockSpec((SW_B, SW_N), lambda i: (n - 1 - i, 0)), small(2, 512), small(2, 128), small(1, 128)],
        out_shape=[jax.ShapeDtypeStruct((t, SW_N), BF16), jax.ShapeDtypeStruct((2, 512), F32),
                   jax.ShapeDtypeStruct((2, 128), F32), jax.ShapeDtypeStruct((1, 128), F32)],
        scratch_shapes=[pltpu.VMEM((SW_B, 512), F32)],
        compiler_params=_cparams(("arbitrary",)),
    )(u, u, cs, cs, gq, gk, sinks, dp)


def _ln_mod(x, g, scale, shift):
    y = x * lax.rsqrt(jnp.mean(x * x, axis=1, keepdims=True) + EPS) * g
    return y * (1.0 + scale) + shift


def _row_tile(t):
    return min(t, 1024)


def _ln_mm(x, g, scale, shift, w, tn):
    t, n = x.shape[0], w.shape[1]
    tm = _row_tile(t)
    vec = pl.BlockSpec((1, D), lambda i, j: (0, 0))

    def body(x_ref, g_ref, sc_ref, sh_ref, w_ref, u_ref, h_ref):
        @pl.when(pl.program_id(1) == 0)
        def _():
            h_ref[...] = _ln_mod(x_ref[...], g_ref[...], sc_ref[...], sh_ref[...]).astype(BF16)

        u_ref[...] = _dot(h_ref[...], w_ref[...], 1, 0)

    return pl.pallas_call(
        body, name="ln_mm", grid=(t // tm, n // tn),
        in_specs=[pl.BlockSpec((tm, D), lambda i, j: (i, 0)), vec, vec, vec,
                  pl.BlockSpec((D, tn), lambda i, j: (0, j))],
        out_specs=[pl.BlockSpec((tm, tn), lambda i, j: (i, j)), pl.BlockSpec((tm, D), lambda i, j: (i, 0))],
        out_shape=[jax.ShapeDtypeStruct((t, n), F32), jax.ShapeDtypeStruct((t, D), BF16)],
        compiler_params=_cparams(("arbitrary", "arbitrary")),
    )(x, g, scale, shift, w)


def _mm_res(p, w, x, gate):
    t, k = p.shape
    tm = _row_tile(t)

    def body(p_ref, w_ref, x_ref, gate_ref, o_ref):
        o_ref[...] = x_ref[...] + gate_ref[...] * _dot(p_ref[...], w_ref[...], 1, 0)

    return pl.pallas_call(
        body, name="mm_res", grid=(t // tm,),
        in_specs=[pl.BlockSpec((tm, k), lambda i: (i, 0)), pl.BlockSpec((k, D), lambda i: (0, 0)),
                  pl.BlockSpec((tm, D), lambda i: (i, 0)), pl.BlockSpec((1, D), lambda i: (0, 0))],
        out_specs=pl.BlockSpec((tm, D), lambda i: (i, 0)),
        out_shape=jax.ShapeDtypeStruct((t, D), F32),
        compiler_params=_cparams(("arbitrary",)),
    )(p, w, x, gate)


def _loss_grad(x, target):
    t = x.shape[0]
    tm = _row_tile(t)

    def body(x_ref, t_ref, l_ref, dx_ref):
        @pl.when(pl.program_id(0) == 0)
        def _():
            l_ref[...] = jnp.zeros_like(l_ref)

        err = x_ref[...] - t_ref[...]
        dx_ref[...] = err * (1.0 / D)
        l_ref[...] += 0.5 * jnp.sum(jnp.mean(err * err, axis=1, keepdims=True), axis=0, keepdims=True)

    return pl.pallas_call(
        body, name="loss_grad", grid=(t // tm,),
        in_specs=[pl.BlockSpec((tm, D), lambda i: (i, 0))] * 2,
        out_specs=[pl.BlockSpec((8, 128), lambda i: (0, 0)), pl.BlockSpec((tm, D), lambda i: (i, 0))],
        out_shape=[jax.ShapeDtypeStruct((8, 128), F32), jax.ShapeDtypeStruct((t, D), F32)],
        compiler_params=_cparams(("arbitrary",)),
    )(x, target)


def _mm_scaled(a, s, w, tn):
    t, k = a.shape
    n = w.shape[1]
    tm = _row_tile(t)

    def body(a_ref, s_ref, w_ref, o_ref):
        o_ref[...] = _dot((a_ref[...] * s_ref[...]).astype(BF16), w_ref[...], 1, 0).astype(BF16)

    return pl.pallas_call(
        body, name="mm_scaled", grid=(t // tm, n // tn),
        in_specs=[pl.BlockSpec((tm, k), lambda i, j: (i, 0)), pl.BlockSpec((1, k), lambda i, j: (0, 0)),
                  pl.BlockSpec((k, tn), lambda i, j: (0, j))],
        out_specs=pl.BlockSpec((tm, tn), lambda i, j: (i, j)),
        out_shape=jax.ShapeDtypeStruct((t, n), BF16),
        compiler_params=_cparams(("arbitrary", "arbitrary")),
    )(a, s, w)


def _mm_tn_acc(a, b, tn):
    t, m = a.shape
    n = b.shape[1]
    fits = lambda k: 2 * k * (m * a.dtype.itemsize + tn * b.dtype.itemsize) + 2 * m * tn * 4 <= 36 * 1024 * 1024
    tk = next(k for k in (4096, 2048, 1024, 512, t) if t % k == 0 and (fits(k) or k <= 512))
    nk = t // tk

    def body(a_ref, b_ref, o_ref):
        @pl.when(pl.program_id(1) == 0)
        def _():
            o_ref[...] = jnp.zeros_like(o_ref)

        o_ref[...] += _dot(a_ref[...], b_ref[...].astype(BF16), 0, 0)

    return pl.pallas_call(
        body, name="mm_tn_acc", grid=(n // tn, nk),
        in_specs=[pl.BlockSpec((tk, m), lambda j, k: (k, 0)), pl.BlockSpec((tk, tn), lambda j, k: (k, j))],
        out_specs=pl.BlockSpec((m, tn), lambda j, k: (0, j)),
        out_shape=jax.ShapeDtypeStruct((m, n), F32),
        compiler_params=_cparams(("arbitrary", "arbitrary")),
    )(a, b)


def _inproj_bwd(du, wt, x, dxp, g, scale, shift):
    t, kdim = du.shape
    tk = kdim
    tm = min(t, 512 if kdim <= 4096 else 256)
    nk = kdim // tk
    vec = pl.BlockSpec((1, D), lambda i, k: (0, 0))

    def body(du_ref, wt_ref, x_ref, dxp_ref, g_ref, sc_ref, sh_ref, dx_ref, dv_ref, acc_ref):
        k = pl.program_id(1)

        @pl.when((pl.program_id(0) == 0) & (k == 0))
        def _():
            dv_ref[...] = jnp.zeros_like(dv_ref)

        @pl.when(k == 0)
        def _():
            acc_ref[...] = jnp.zeros_like(acc_ref)

        acc_ref[...] += _dot(du_ref[...].astype(BF16), wt_ref[...], 1, 0)

        @pl.when(k == nk - 1)
        def _():
            _, vjp = jax.vjp(_ln_mod, x_ref[...], g_ref[...], sc_ref[...], sh_ref[...])
            dx, dg, dsc, dsh = vjp(acc_ref[...])
            dx_ref[...] = dxp_ref[...] + dx
            dv_ref[0:1, :] += dg
            dv_ref[1:2, :] += dsc
            dv_ref[2:3, :] += dsh

    return pl.pallas_call(
        body, name="inproj_bwd", grid=(t // tm, nk),
        in_specs=[pl.BlockSpec((tm, tk), lambda i, k: (i, k)), pl.BlockSpec((tk, D), lambda i, k: (k, 0)),
                  pl.BlockSpec((tm, D), lambda i, k: (i, 0)), pl.BlockSpec((tm, D), lambda i, k: (i, 0)),
                  vec, vec, vec],
        out_specs=[pl.BlockSpec((tm, D), lambda i, k: (i, 0)), pl.BlockSpec((8, D), lambda i, k: (0, 0))],
        out_shape=[jax.ShapeDtypeStruct((t, D), F32), jax.ShapeDtypeStruct((8, D), F32)],
        scratch_shapes=[pltpu.VMEM((tm, D), F32)],
        compiler_params=_cparams(("arbitrary", "arbitrary")),
    )(du, wt, x, dxp, g, scale, shift)


def _outgrad(gmat, w, gate):
    k = gmat.shape[0]
    tr = 256

    def body(g_ref, w_ref, gate_ref, dw_ref, dg_ref):
        @pl.when(pl.program_id(0) == 0)
        def _():
            dg_ref[...] = jnp.zeros_like(dg_ref)

        gm = g_ref[...]
        dw_ref[...] = gm * gate_ref[...]
        dg_ref[0:1, :] += jnp.sum(gm * w_ref[...].astype(F32), axis=0, keepdims=True)

    return pl.pallas_call(
        body, name="outgrad", grid=(k // tr,),
        in_specs=[pl.BlockSpec((tr, D), lambda i: (i, 0)), pl.BlockSpec((tr, D), lambda i: (i, 0)),
                  pl.BlockSpec((1, D), lambda i: (0, 0))],
        out_specs=[pl.BlockSpec((tr, D), lambda i: (i, 0)), pl.BlockSpec((8, D), lambda i: (0, 0))],
        out_shape=[jax.ShapeDtypeStruct((k, D), F32), jax.ShapeDtypeStruct((8, D), F32)],
        compiler_params=_cparams(("arbitrary",)),
    )(gmat, w, gate)


def _rope_table(pos, freq):
    t = pos.shape[0]
    tm = _row_tile(t)

    def body(p_ref, f_ref, o_ref):
        ang = p_ref[...].astype(F32) * f_ref[...]
        o_ref[:, 0:128] = jnp.cos(ang)
        o_ref[:, 128:256] = jnp.sin(ang)

    return pl.pallas_call(
        body, name="rope_table", grid=(t // tm,),
        in_specs=[pl.BlockSpec((tm, 1), lambda i: (i, 0)), pl.BlockSpec((1, 128), lambda i: (0, 0))],
        out_specs=pl.BlockSpec((tm, 256), lambda i: (i, 0)),
        out_shape=jax.ShapeDtypeStruct((t, 256), F32),
        compiler_params=_cparams(("arbitrary",)),
    )(pos, freq)


def _ada_fwd(c_all, w, b):
    nl, _, s = w.shape

    def body(c_ref, w_ref, b_ref, o_ref):
        o_ref[0] = _mm_f32(c_ref[...], w_ref[0]) + b_ref[0]

    return pl.pallas_call(
        body, name="ada_fwd", grid=(nl,),
        in_specs=[pl.BlockSpec((8, D), lambda l: (0, 0)), pl.BlockSpec((1, D, s), lambda l: (l, 0, 0)),
                  pl.BlockSpec((1, 1, s), lambda l: (l, 0, 0))],
        out_specs=pl.BlockSpec((1, 8, s), lambda l: (l, 0, 0)),
        out_shape=jax.ShapeDtypeStruct((nl, 8, s), F32),
        compiler_params=_cparams(("arbitrary",)),
    )(c_all, w, b)


def _ada_bwd(c_all, dmod_cols, dmod_all):
    nl, _, s = dmod_cols.shape

    def body(c_ref, dc_ref, da_ref, gw_ref, gb_ref):
        gw_ref[0] = _dot(c_ref[...], dc_ref[0], 0, 0, lax.Precision.HIGHEST)
        gb_ref[0] = jnp.sum(da_ref[0], axis=0, keepdims=True)

    return pl.pallas_call(
        body, name="ada_bwd", grid=(nl,),
        in_specs=[pl.BlockSpec((8, D), lambda l: (0, 0)), pl.BlockSpec((1, 8, s), lambda l: (l, 0, 0)),
                  pl.BlockSpec((1, 8, 3 * D), lambda l: (l, 0, 0))],
        out_specs=[pl.BlockSpec((1, D, s), lambda l: (l, 0, 0)), pl.BlockSpec((1, 1, 3 * D), lambda l: (l, 0, 0))],
        out_shape=[jax.ShapeDtypeStruct((nl, D, s), F32), jax.ShapeDtypeStruct((nl, 1, 3 * D), F32)],
        compiler_params=_cparams(("arbitrary",)),
    )(c_all, dmod_cols, dmod_all)


def _lb_fn(h8):
    sm = jax.nn.softmax(h8, axis=0)
    r = lax.broadcasted_iota(jnp.int32, (8, 8), 0)
    c = lax.broadcasted_iota(jnp.int32, (8, 8), 1)
    return _mm_f32(((c >= 1) & (c <= r)).astype(F32), sm)


def _lb_fwd(h8):
    def body(h_ref, o_ref):
        o_ref[...] = _lb_fn(h_ref[...])

    return pl.pallas_call(body, name="lb_fwd", out_shape=jax.ShapeDtypeStruct((8, D), F32))(h8)


def _lb_bwd(h8, dlb8):
    def body(h_ref, d_ref, o_ref):
        _, vjp = jax.vjp(_lb_fn, h_ref[...])
        o_ref[...] = vjp(d_ref[...])[0]

    return pl.pallas_call(body, name="lb_bwd", out_shape=jax.ShapeDtypeStruct((8, D), F32))(h8, dlb8)


ADAM_LR, ADAM_B1, ADAM_B2, ADAM_EPS, ADAM_WD, ADAM_STEP = 0.001, 0.9, 0.999, 1e-08, 0.01, 10


def _adamw(w, gparts, m, v):
    r, c = w.shape
    tr = r if r * c * 4 <= (1 << 20) else max(8, ((1 << 20) // (c * 4)) // 8 * 8)
    while r % tr:
        tr -= 8
    ng = len(gparts)

    def body(*refs):
        w_ref, m_ref, v_ref = refs[0], refs[1 + ng], refs[2 + ng]
        g_ref, d_ref, nm_ref, nv_ref = refs[3 + ng:]
        g = refs[1][...]
        for gr in refs[2:1 + ng]:
            g = g + gr[...]
        mm = ADAM_B1 * m_ref[...] + (1.0 - ADAM_B1) * g
        vv = ADAM_B2 * v_ref[...] + (1.0 - ADAM_B2) * (g * g)
        m_hat = mm / (1.0 - ADAM_B1 ** ADAM_STEP)
        v_hat = vv / (1.0 - ADAM_B2 ** ADAM_STEP)
        g_ref[...] = g
        d_ref[...] = -ADAM_LR * (m_hat / (jnp.sqrt(v_hat) + ADAM_EPS) + ADAM_WD * w_ref[...])
        nm_ref[...] = mm
        nv_ref[...] = vv

    spec = pl.BlockSpec((tr, c), lambda i: (i, 0))
    return pl.pallas_call(
        body, name="adamw", grid=(r // tr,), in_specs=[spec] * (3 + ng), out_specs=[spec] * 4,
        out_shape=[jax.ShapeDtypeStruct((r, c), F32)] * 4,
        compiler_params=_cparams(("arbitrary",)),
    )(w, *gparts, m, v)


def _sum_rows(parts):
    r, c = parts[0].shape
    tr = 8
    for cand in range(min(r, 512), 7, -8):
        if r % cand == 0:
            tr = cand
            break

    def body(*refs):
        acc = refs[0][...]
        for p in refs[1:-1]:
            acc = acc + p[...]
        refs[-1][...] = acc

    spec = pl.BlockSpec((tr, c), lambda i: (i, 0))
    return pl.pallas_call(
        body, name="sum_rows", grid=(r // tr,), in_specs=[spec] * len(parts), out_specs=spec,
        out_shape=jax.ShapeDtypeStruct((r, c), F32),
        compiler_params=_cparams(("arbitrary",)),
    )(*parts)


MESH = pl.DeviceIdType.MESH
ANY = pl.BlockSpec(memory_space=pl.ANY)


def _place():
    return lax.axis_index("x"), lax.axis_index("y"), lax.axis_index("c")


def _allgather8(blk):
    m_per, n = blk.shape

    def body(x_ref, out_ref, send_sems, recv_sems, local_sem):
        x, y, c = _place()
        me, sibling = (x, y, c), (x, y, 1 - c)
        chips = [(1 - x, y), (x, 1 - y), (1 - x, 1 - y)]

        def rows(px, py, pc):
            return out_ref.at[pl.ds((4 * px + 2 * py + pc) * m_per, m_per), :]

        def copy(k, block, to, src=None):
            return pltpu.make_async_remote_copy(
                src_ref=rows(*block) if src is None else src, dst_ref=rows(*block),
                send_sem=send_sems.at[k], recv_sem=recv_sems.at[k], device_id=to, device_id_type=MESH)

        mine = pltpu.make_async_copy(x_ref, rows(*me), local_sem)
        mine.start()
        first = [copy(0, me, sibling, src=x_ref)]
        first += [copy(1 + j, me, (*chip, c), src=x_ref) for j, chip in enumerate(chips)]
        for cp in first:
            cp.start()
        passed = [copy(4 + j, (*chip, c), sibling) for j, chip in enumerate(chips)]
        for j, chip in enumerate(chips):
            copy(1 + j, (*chip, c), me).wait_recv()
            passed[j].start()
        copy(0, sibling, me).wait_recv()
        for j, chip in enumerate(chips):
            copy(4 + j, (*chip, 1 - c), me).wait_recv()
        for cp in first + passed:
            cp.wait_send()
        mine.wait()

    return pl.pallas_call(
        body, name="allgather8",
        out_shape=jax.ShapeDtypeStruct((8 * m_per, n), blk.dtype),
        in_specs=[pl.BlockSpec(memory_space=pltpu.VMEM)],
        out_specs=pl.BlockSpec(memory_space=pltpu.VMEM),
        scratch_shapes=[pltpu.SemaphoreType.DMA((7,)), pltpu.SemaphoreType.DMA((7,)), pltpu.SemaphoreType.DMA],
    )(blk)


def _chip_peers():
    x, y, c = _place()
    return [(1 - x, y, c), (x, 1 - y, c), (1 - x, 1 - y, c)]


GATHER_SEMS = [pltpu.SemaphoreType.DMA((3,)), pltpu.SemaphoreType.DMA((3,)), pltpu.SemaphoreType.DMA]
SCATTER_SEMS = [pltpu.SemaphoreType.DMA((3,)), pltpu.SemaphoreType.DMA((3,))]


def _gather_plan(x_ref, out_ref, send_sems, recv_sems, local_sem):
    x, y, _ = _place()
    peers = _chip_peers()

    def copy(j, chip_index):
        return pltpu.make_async_remote_copy(
            src_ref=x_ref, dst_ref=out_ref.at[chip_index], send_sem=send_sems.at[j], recv_sem=recv_sems.at[j],
            device_id=peers[j], device_id_type=MESH)

    mine = pltpu.make_async_copy(x_ref, out_ref.at[2 * x + y], local_sem)
    sends = [copy(j, 2 * x + y) for j in range(3)]

    def start():
        mine.start()
        for cp in sends:
            cp.start()

    def wait():
        for j in range(3):
            copy(j, 2 * peers[j][0] + peers[j][1]).wait_recv()
        for cp in sends:
            cp.wait_send()
        mine.wait()

    return start, wait


def _scatter_plan(p_ref, out_ref, send_sems, recv_sems):
    peers = _chip_peers()
    sends = [pltpu.make_async_remote_copy(
        src_ref=p_ref.at[2 * peers[j][0] + peers[j][1]], dst_ref=out_ref.at[j], send_sem=send_sems.at[j],
        recv_sem=recv_sems.at[j], device_id=peers[j], device_id_type=MESH) for j in range(3)]

    def start():
        for cp in sends:
            cp.start()

    def wait():
        for cp in sends:
            cp.wait_recv()
        for cp in sends:
            cp.wait_send()

    return start, wait


def _chip_allgather(shard):
    def body(x_ref, out_ref, *sems):
        start, wait = _gather_plan(x_ref, out_ref, *sems)
        start()
        wait()

    return pl.pallas_call(
        body, name="chip_allgather", out_shape=jax.ShapeDtypeStruct((4,) + shard.shape, shard.dtype),
        in_specs=[ANY], out_specs=ANY, scratch_shapes=GATHER_SEMS,
    )(shard)


def _chip_scatter(parts):
    def body(p_ref, out_ref, *sems):
        start, wait = _scatter_plan(p_ref, out_ref, *sems)
        start()
        wait()

    return pl.pallas_call(
        body, name="chip_scatter", out_shape=jax.ShapeDtypeStruct((3,) + parts.shape[1:], parts.dtype),
        in_specs=[ANY], out_specs=ANY, scratch_shapes=SCATTER_SEMS,
    )(parts)


def _sibling_swap(a):
    def body(a_ref, out_ref, send_sem, recv_sem):
        x, y, c = _place()
        cp = pltpu.make_async_remote_copy(src_ref=a_ref, dst_ref=out_ref, send_sem=send_sem, recv_sem=recv_sem,
                                          device_id=(x, y, 1 - c), device_id_type=MESH)
        cp.start()
        cp.wait_recv()
        cp.wait_send()

    return pl.pallas_call(
        body, name="sibling_swap", out_shape=jax.ShapeDtypeStruct(a.shape, a.dtype),
        in_specs=[ANY], out_specs=ANY,
        scratch_shapes=[pltpu.SemaphoreType.DMA, pltpu.SemaphoreType.DMA],
    )(a)


WEIGHTS = ['hgrn_lb', 'ada_w', 'ada_b', 'norm_g', 'hg_in_w', 'hg_out_w', 'hg_onorm', 'sw_in_w', 'sw_out_w', 'sw_qnorm',
           'sw_knorm', 'sw_sinks', 'gd_in_w', 'gd_out_w', 'gd_conv_w', 'gd_a_log', 'gd_dt_bias', 'gd_onorm']
BIG = ['hg_in_w', 'hg_out_w', 'sw_in_w', 'sw_out_w', 'gd_in_w', 'gd_out_w']
SEG_FIRST = [('hg_in_w', 0), ('hg_out_w', 0)]
SEG_REST = [('hg_in_w', 1), ('hg_out_w', 1), ('sw_in_w', 0), ('sw_out_w', 0), ('gd_in_w', 0), ('gd_out_w', 0)]
PACK_ALIGN = 16
ROPE_THETA = 10000.0
ADA_S = 3 * D // 4
SMALL_ROW = {'hg_onorm': (0, 256), 'sw_qnorm': (256, 64), 'sw_knorm': (320, 64), 'sw_sinks': (384, 16),
             'gd_a_log': (400, 16), 'gd_dt_bias': (416, 16), 'gd_onorm': (432, 128)}


def _pack_rows(arrs):
    flat = jnp.concatenate([a.reshape(-1, D) for a in arrs], axis=0)
    return jnp.pad(flat, ((0, -flat.shape[0] % PACK_ALIGN), (0, 0)))


def _unpack_rows(packed, shapes):
    out, off = [], 0
    for s in shapes:
        rows = 1
        for d in s:
            rows *= d
        rows //= D
        out.append(packed[..., off:off + rows, :].reshape(packed.shape[:-2] + tuple(s)))
        off += rows
    return out


def _pack_small(vals):
    row = jnp.concatenate([vals[k].reshape(-1) for k in SMALL_ROW])
    row = jnp.pad(row, (0, D - row.shape[0]))[None]
    return jnp.concatenate([vals['hgrn_lb'], vals['norm_g'], vals['gd_conv_w'].reshape(16, D), row,
                            jnp.zeros((7, D), F32)], axis=0)


def _sw_cols(w, inverse=False):
    def split(a, heads):
        shp = (a.shape[0], 2, heads, 32) if inverse else (a.shape[0], heads, 2, 32)
        return a.reshape(shp).transpose(0, 2, 1, 3).reshape(a.shape[0], heads * 64)
    return jnp.concatenate([split(w[:, 0:1024], 16), split(w[:, 1024:1280], 4), w[:, 1280:]], axis=1)


def kernel(x, c, positions, hgrn_lb, ada_w, ada_b, norm_g, hg_in_w, hg_out_w, hg_onorm, sw_in_w, sw_out_w, sw_qnorm, sw_knorm, sw_sinks, gd_in_w, gd_out_w, gd_conv_w, gd_a_log, gd_dt_bias, gd_onorm, loss_target, m_hgrn_lb, m_ada_w, m_ada_b, m_norm_g, m_hg_in_w, m_hg_out_w, m_hg_onorm, m_sw_in_w, m_sw_out_w, m_sw_qnorm, m_sw_knorm, m_sw_sinks, m_gd_in_w, m_gd_out_w, m_gd_conv_w, m_gd_a_log, m_gd_dt_bias, m_gd_onorm, v_hgrn_lb, v_ada_w, v_ada_b, v_norm_g, v_hg_in_w, v_hg_out_w, v_hg_onorm, v_sw_in_w, v_sw_out_w, v_sw_qnorm, v_sw_knorm, v_sw_sinks, v_gd_in_w, v_gd_out_w, v_gd_conv_w, v_gd_a_log, v_gd_dt_bias, v_gd_onorm):
    w_in = dict(hgrn_lb=hgrn_lb, ada_w=ada_w, ada_b=ada_b, norm_g=norm_g, hg_in_w=hg_in_w, hg_out_w=hg_out_w,
                hg_onorm=hg_onorm, sw_in_w=sw_in_w, sw_out_w=sw_out_w, sw_qnorm=sw_qnorm, sw_knorm=sw_knorm,
                sw_sinks=sw_sinks, gd_in_w=gd_in_w, gd_out_w=gd_out_w, gd_conv_w=gd_conv_w, gd_a_log=gd_a_log,
                gd_dt_bias=gd_dt_bias, gd_onorm=gd_onorm)
    m_in = dict(zip(WEIGHTS, (m_hgrn_lb, m_ada_w, m_ada_b, m_norm_g, m_hg_in_w, m_hg_out_w, m_hg_onorm, m_sw_in_w,
                              m_sw_out_w, m_sw_qnorm, m_sw_knorm, m_sw_sinks, m_gd_in_w, m_gd_out_w, m_gd_conv_w,
                              m_gd_a_log, m_gd_dt_bias, m_gd_onorm)))
    v_in = dict(zip(WEIGHTS, (v_hgrn_lb, v_ada_w, v_ada_b, v_norm_g, v_hg_in_w, v_hg_out_w, v_hg_onorm, v_sw_in_w,
                              v_sw_out_w, v_sw_qnorm, v_sw_knorm, v_sw_sinks, v_gd_in_w, v_gd_out_w, v_gd_conv_w,
                              v_gd_a_log, v_gd_dt_bias, v_gd_onorm)))
    ax, ay, ac = _place()
    chip = 2 * ax + ay
    bidx = 4 * ax + 2 * ay + ac
    t = x.shape[1]
    x0, target = x[0], loss_target[0]

    c_all = _allgather8(jnp.pad(c, ((0, 7), (0, 0)))).reshape(8, 8, D)[:, 0, :]
    ada_b_cols = lax.dynamic_slice(ada_b, (0, chip * ADA_S), (4, ADA_S)).reshape(4, 1, ADA_S)
    mod_sh = _ada_fwd(c_all, ada_w, ada_b_cols)
    mod_g = _allgather8(mod_sh.reshape(32, ADA_S)).reshape(4, 2, 4, 8, ADA_S)[:, 0]
    mod = lax.dynamic_index_in_dim(mod_g, bidx, axis=2, keepdims=False).transpose(1, 0, 2).reshape(4, 3 * D)
    shift = [mod[l:l + 1, 0:D] for l in range(4)]
    scale = [mod[l:l + 1, D:2 * D] for l in range(4)]
    gate = [mod[l:l + 1, 2 * D:3 * D] for l in range(4)]

    h8 = jnp.concatenate([hgrn_lb, jnp.full((4, D), -1e30, F32)], axis=0)
    lb_all = _lb_fwd(h8)
    freq = ROPE_THETA ** (-jnp.arange(0, 64, 2, dtype=F32) / 64)
    cs = _rope_table(positions.reshape(t, 1), jnp.tile(freq, 4)[None])

    seg_shapes = lambda seg: [w_in[k].shape[1:] for k, _ in seg]
    pack_seg = lambda src, seg: _pack_rows([src[k][i] for k, i in seg])
    cols_full = lambda a: a.transpose(1, 0, 2).reshape(a.shape[1], 4 * a.shape[2])
    hg_in0_k, hg_out0_k = _unpack_rows(_chip_allgather(pack_seg(w_in, SEG_FIRST).astype(BF16)), seg_shapes(SEG_FIRST))
    win, wout = [cols_full(hg_in0_k)], [hg_out0_k.reshape(D, D)]
    rest_shard = pack_seg(w_in, SEG_REST).astype(BF16)
    tn_in = [1024, 1280, 896, 1024]

    gq = jnp.stack([jnp.tile(sw_qnorm[0, :32], 16), jnp.tile(sw_qnorm[0, 32:], 16)])
    gk = jnp.stack([jnp.tile(sw_knorm[0, :32], 4), jnp.tile(sw_knorm[0, 32:], 4)])
    pad128 = lambda a: jnp.pad(a, ((0, 0), (0, HD - a.shape[1])))
    sinks, alog, dtb = pad128(sw_sinks), pad128(gd_a_log), pad128(gd_dt_bias)
    cw8 = jnp.pad(_chip_allgather(gd_conv_w[0]).transpose(1, 0, 2).reshape(4, GD_QKV), ((0, 4), (0, 0)))
    lbs = {0: lb_all[0:1], 3: lb_all[3:4]}

    xs, us, hs, ps, stss = [x0], [], [], [], []
    for l in range(4):
        u, h = _ln_mm(xs[l], norm_g[l:l + 1], scale[l], shift[l], win[l], tn_in[l])
        if l == 0:
            p, sts, rest_k = _hg_fwd(u, lbs[l], hg_onorm[0:1], gather=rest_shard)
            hg_in1_k, hg_out1_k, sw_in_k, sw_out_k, gd_in_k, gd_out_k = _unpack_rows(rest_k, seg_shapes(SEG_REST))
            win += [_sw_cols(cols_full(sw_in_k)), jnp.pad(cols_full(gd_in_k), ((0, 0), (0, GD_N - 6176))),
                    cols_full(hg_in1_k)]
            wout += [sw_out_k.reshape(D, D), gd_out_k.reshape(GD_VW, D), hg_out1_k.reshape(D, D)]
        elif l % 3 == 0:
            p, sts = _hg_fwd(u, lbs[l], hg_onorm[l // 3:l // 3 + 1])
        elif l % 3 == 1:
            p, sts = _sw_fwd(u, cs, gq, gk, sinks), None
        else:
            p, *sts = _gd_fwd(u, cw8, alog, dtb, gd_onorm)
        xs.append(_mm_res(p, wout[l], xs[l], gate[l]))
        us.append(u), hs.append(h), ps.append(p), stss.append(sts)
    lpart, dx = _loss_grad(xs[4], target)
    loss = lax.psum(lpart[0, 0], ("x", "y", "c"))

    by_chip = lambda g, cols: g.reshape(g.shape[0], 4, cols).transpose(1, 0, 2)
    g_small = {}
    d_in, d_out, dmod, dnorm_g, dlb8, dgo_hg = [None] * 4, [None] * 4, [None] * 4, [None] * 4, jnp.zeros((8, D), F32), {}
    for l in (3, 2, 1, 0):
        dp = _mm_scaled(dx, gate[l], wout[l].T, 1024)
        d_out[l], dgate = _outgrad(_mm_tn_acc(ps[l], dx, D if ps[l].shape[1] == D else 512), wout[l], gate[l])
        if l == 0:
            rest_parts = {('hg_in_w', 1): by_chip(d_in[3], D), ('hg_out_w', 1): d_out[3].reshape(4, D // 4, D),
                          ('sw_in_w', 0): by_chip(_sw_cols(d_in[1], inverse=True), SW_N // 4),
                          ('sw_out_w', 0): d_out[1].reshape(4, D // 4, D),
                          ('gd_in_w', 0): by_chip(d_in[2][:, :6176], 1544),
                          ('gd_out_w', 0): d_out[2].reshape(4, GD_VW // 4, D)}
            rest_packed = jnp.stack([_pack_rows([rest_parts[s][j] for s in SEG_REST]) for j in range(4)])
            du, dlb, dgo_hg[0], rest_recv = _hg_bwd(us[l], stss[l], dp, lbs[l], hg_onorm[0:1],
                                                    scatter=rest_packed.astype(BF16))
            dlb8 = lax.dynamic_update_slice(dlb8, dlb, (l, 0))
        elif l % 3 == 0:
            du, dlb, dgo_hg[l // 3] = _hg_bwd(us[l], stss[l], dp, lbs[l], hg_onorm[l // 3:l // 3 + 1])
            dlb8 = lax.dynamic_update_slice(dlb8, dlb, (l, 0))
        elif l % 3 == 1:
            du, dgq, dgk, dsk = _sw_bwd(us[l], cs, dp, gq, gk, sinks)
            g_small['sw_qnorm'] = jnp.concatenate([dgq[0].reshape(16, 32).sum(0), dgq[1].reshape(16, 32).sum(0)])
            g_small['sw_knorm'] = jnp.concatenate([dgk[0].reshape(4, 32).sum(0), dgk[1].reshape(4, 32).sum(0)])
            g_small['sw_sinks'] = dsk[0, :16]
        else:
            du, dcw, dalog, ddtb, g_small['gd_onorm'] = _gd_bwd(us[l], *stss[l], dp, cw8, alog, dtb, gd_onorm)
            g_small['gd_conv_w'], g_small['gd_a_log'], g_small['gd_dt_bias'] = dcw[:4], dalog[0, :16], ddtb[0, :16]
        d_in[l] = _mm_tn_acc(hs[l], du, 896 if l == 2 else 512)
        dx, dvec = _inproj_bwd(du, win[l].T, xs[l], dx, norm_g[l:l + 1], scale[l], shift[l])
        dnorm_g[l] = dvec[0:1]
        dmod[l] = jnp.concatenate([dvec[2:3], dvec[1:2], dgate[0:1]], axis=1)
    grad_x = dx[None]

    g_small['hgrn_lb'] = _lb_bwd(h8, dlb8)[0:4]
    g_small['norm_g'] = jnp.concatenate(dnorm_g, axis=0)
    g_small['hg_onorm'] = jnp.concatenate([dgo_hg[0], dgo_hg[1]], axis=0)
    gs_all = _allgather8(_pack_small(g_small))
    gs = _sum_rows([gs_all[32 * d:32 * (d + 1)] for d in range(8)])

    def small_view(packed, k):
        if k == 'hgrn_lb':
            return packed[0:4]
        if k == 'norm_g':
            return packed[4:8]
        off, size = SMALL_ROW[k]
        return packed[24, off:off + size].reshape(w_in[k].shape)

    conv_sl = lambda full: lax.dynamic_slice(full.reshape(4, GD_QKV), (0, chip * D), (4, D))
    out = {}

    def put(k, res, shape):
        for name, r in zip(('grad_', 'delta_', 'new_m_', 'new_v_'), res):
            out[name + k] = r.reshape(shape)

    zero_conv = dict(gd_conv_w=jnp.zeros((4, GD_QKV), F32))
    small_names = ['hgrn_lb', 'norm_g'] + list(SMALL_ROW)
    res = _adamw(_pack_small({**{k: w_in[k] for k in small_names}, **zero_conv}), (gs,),
                 _pack_small({**{k: m_in[k] for k in small_names}, **zero_conv}),
                 _pack_small({**{k: v_in[k] for k in small_names}, **zero_conv}))
    for k in small_names:
        put(k, [small_view(r, k) for r in res], w_in[k].shape)
    put('gd_conv_w', _adamw(gd_conv_w[0], (conv_sl(gs[8:24]),), m_in['gd_conv_w'][0], v_in['gd_conv_w'][0]),
        gd_conv_w.shape)

    dm = _allgather8(jnp.pad(jnp.concatenate(dmod, axis=0), ((0, 4), (0, 0)))).reshape(8, 8, 3 * D)[:, :4]
    dm = dm.transpose(1, 0, 2)
    g_ada_w, g_ada_b = _ada_bwd(c_all, lax.dynamic_slice(dm, (0, 0, chip * ADA_S), (4, 8, ADA_S)), dm)
    put('ada_w', _adamw(ada_w.reshape(4 * D, ADA_S), (g_ada_w.reshape(4 * D, ADA_S),),
                        m_in['ada_w'].reshape(4 * D, ADA_S), v_in['ada_w'].reshape(4 * D, ADA_S)), ada_w.shape)
    put('ada_b', _adamw(ada_b, (g_ada_b.reshape(4, 3 * D),), m_in['ada_b'], v_in['ada_b']), ada_b.shape)

    first_parts = {('hg_in_w', 0): by_chip(d_in[0], D), ('hg_out_w', 0): d_out[0].reshape(4, D // 4, D)}
    first_packed = jnp.stack([_pack_rows([first_parts[s][j] for s in SEG_FIRST]) for j in range(4)])
    first_recv = _chip_scatter(first_packed.astype(BF16))
    own = lambda packed: lax.dynamic_index_in_dim(packed, chip, axis=0, keepdims=False)
    half = jnp.concatenate([_sum_rows([own(first_packed), first_recv[0], first_recv[1], first_recv[2]]),
                            _sum_rows([own(rest_packed), rest_recv[0], rest_recv[1], rest_recv[2]])], axis=0)
    other = _sibling_swap(half)
    pack_all = lambda src: jnp.concatenate([pack_seg(src, SEG_FIRST), pack_seg(src, SEG_REST)], axis=0)
    res = _adamw(pack_all(w_in), (half, other), pack_all(m_in), pack_all(v_in))
    n_first = first_packed.shape[1]
    for name, r in zip(('grad_', 'delta_', 'new_m_', 'new_v_'), res):
        pieces = dict(zip(SEG_FIRST, _unpack_rows(r[:n_first], seg_shapes(SEG_FIRST))))
        pieces.update(zip(SEG_REST, _unpack_rows(r[n_first:], seg_shapes(SEG_REST))))
        for k in BIG:
            out[name + k] = jnp.stack([pieces[(k, i)] for i in range(w_in[k].shape[0])])

    return (loss, grad_x, *[out[p + k] for p in ('grad_', 'delta_', 'new_m_', 'new_v_') for k in WEIGHTS])
```

```python
import functools

import jax
import jax.numpy as jnp
from jax import lax
from jax.experimental import pallas as pl
from jax.experimental.pallas import tpu as pltpu

F32 = jnp.float32
BF16 = jnp.bfloat16
D = 1024
EPS = 1e-6
CHUNK = 64
SUB = 32
HG_H = 8
HD = 128
VMEM_LIMIT = 56 * 1024 * 1024


def _cparams(sem=None):
    return pltpu.CompilerParams(dimension_semantics=sem, vmem_limit_bytes=VMEM_LIMIT)


def _dot(a, b, ca, cb, prec=None):
    return lax.dot_general(a, b, (((ca,), (cb,)), ((), ())), precision=prec, preferred_element_type=F32)


def _mm(a, b):
    return _dot(a.astype(BF16), b.astype(BF16), 1, 0)


def _mm_nt(a, b):
    return _dot(a.astype(BF16), b.astype(BF16), 1, 1)


def _mm_tn(a, b):
    return _dot(a.astype(BF16), b.astype(BF16), 0, 0)


def _mm_f32(a, b):
    return _dot(a, b, 1, 0, lax.Precision.HIGHEST)


def _silu(x):
    return x * jax.nn.sigmoid(x)


def _cumsum_impl(x):
    row = lax.broadcasted_iota(jnp.int32, x.shape, 0)
    s = 1
    while s < x.shape[0]:
        x = x + jnp.where(row >= s, pltpu.roll(x, s, 0), 0.0)
        s *= 2
    return x


@jax.custom_vjp
def _cumsum_rows(x):
    return _cumsum_impl(x)


_cumsum_rows.defvjp(lambda x: (_cumsum_impl(x), None),
                    lambda _, g: (jnp.sum(g, axis=0, keepdims=True) - _cumsum_impl(g) + g,))


def _roll_rows(x, shift):
    n = x.shape[0]

    @jax.custom_vjp
    def f(a):
        return pltpu.roll(a, shift, 0)

    f.defvjp(lambda a: (pltpu.roll(a, shift, 0), None), lambda _, g: (pltpu.roll(g, n - shift, 0),))
    return f(x)


def _hg_chunk(q_raw, f_pre, v, z, st, lb, go):
    c = q_raw.shape[0]
    nsub = c // SUB
    lf = jnp.log(lb + (1.0 - lb) * jax.nn.sigmoid(f_pre))
    k = (1.0 - lb) * jax.nn.sigmoid(-f_pre)
    q = _silu(q_raw)
    b = _cumsum_rows(lf)
    rowf = lax.broadcasted_iota(jnp.int32, lf.shape, 0)
    bmid = [jnp.sum(jnp.where(rowf == SUB * i + SUB // 2, b, 0.0), axis=0, keepdims=True) for i in range(nsub)]
    row = lax.broadcasted_iota(jnp.int32, (c, 1), 0)
    ref = sum(jnp.where((row >= SUB * i) & (row < SUB * (i + 1)), bmid[i], 0.0) for i in range(nsub))
    qt = q * jnp.exp(b - ref)
    kall = jnp.concatenate(
        [k * jnp.exp(jnp.where(row < SUB * (i + 1), bmid[i] - b, -jnp.inf)) for i in range(nsub)], axis=0)
    v4 = jnp.concatenate([v] * nsub, axis=0)
    b_last = jnp.sum(lf, axis=0, keepdims=True)
    qb = q * jnp.exp(b)
    kd = k * jnp.exp(b_last - b)
    e_last = jnp.exp(b_last)
    tq = lax.broadcasted_iota(jnp.int32, (c, nsub * c), 0)
    cq = lax.broadcasted_iota(jnp.int32, (c, nsub * c), 1)
    m_all = ((cq // c) == (tq // SUB)) & ((cq % c) <= tq)
    hs = lambda a: jnp.split(a, HG_H, axis=1)
    qt_h, kall_h, v4_h, qb_h, kd_h, v_h, z_h, el_h = map(hs, (qt, kall, v4, qb, kd, v, z, e_last))
    st_h = jnp.split(st, HG_H, axis=0)
    heads = range(HG_H)
    pm = [jnp.where(m_all, _mm_nt(qt_h[h], kall_h[h]), 0.0) for h in heads]
    inter = [_mm_nt(qb_h[h], st_h[h]) for h in heads]
    o_h = [_mm(pm[h], v4_h[h]) + inter[h] for h in heads]
    upd = [_mm_tn(v_h[h], kd_h[h]) for h in heads]
    st_out = [el_h[h] * st_h[h] + upd[h] for h in heads]
    y_h = [o_h[h] * lax.rsqrt(jnp.mean(o_h[h] * o_h[h], axis=1, keepdims=True) + EPS) * go for h in heads]
    p_out = [y_h[h] * _silu(z_h[h]) for h in heads]
    return jnp.concatenate(p_out, axis=1), jnp.concatenate(st_out, axis=0)


def _hg_fwd(u, lb, go, gather=None):
    t = u.shape[0]
    n = t // CHUNK

    def body(u_ref, lb_ref, go_ref, *rest):
        if gather is None:
            p_ref, sts_ref, st_ref = rest
        else:
            shard_ref, p_ref, sts_ref, all_ref, st_ref, *sems = rest
            start, wait = _gather_plan(shard_ref, all_ref, *sems)
            pl.when(pl.program_id(0) == 0)(start)

        @pl.when(pl.program_id(0) == 0)
        def _():
            st_ref[...] = jnp.zeros_like(st_ref)

        st = st_ref[...]
        sts_ref[0] = st
        p, st_next = _hg_chunk(u_ref[:, 0:D], u_ref[:, D:2 * D], u_ref[:, 2 * D:3 * D], u_ref[:, 3 * D:4 * D],
                               st, lb_ref[...], go_ref[...])
        p_ref[...] = p.astype(BF16)
        st_ref[...] = st_next
        if gather is not None:
            pl.when(pl.program_id(0) == n - 1)(wait)

    more = gather is not None
    return pl.pallas_call(
        body, name="hg_fwd_gather" if more else "hg_fwd", grid=(n,),
        in_specs=[pl.BlockSpec((CHUNK, 4 * D), lambda i: (i, 0)),
                  pl.BlockSpec((1, D), lambda i: (0, 0)),
                  pl.BlockSpec((1, HD), lambda i: (0, 0))] + [ANY] * more,
        out_specs=[pl.BlockSpec((CHUNK, D), lambda i: (i, 0)),
                   pl.BlockSpec((1, HG_H * HD, HD), lambda i: (i, 0, 0))] + [ANY] * more,
        out_shape=[jax.ShapeDtypeStruct((t, D), BF16), jax.ShapeDtypeStruct((n, HG_H * HD, HD), F32)]
        + ([jax.ShapeDtypeStruct((4,) + gather.shape, gather.dtype)] if more else []),
        scratch_shapes=[pltpu.VMEM((HG_H * HD, HD), F32)] + GATHER_SEMS * more,
        compiler_params=_cparams(("arbitrary",)),
    )(u, lb, go, *([gather] * more))


def _hg_bwd(u, sts, dp, lb, go, scatter=None):
    t = u.shape[0]
    n = t // CHUNK

    def body(u_ref, sts_ref, dp_ref, lb_ref, go_ref, *rest):
        if scatter is None:
            du_ref, dlb_ref, dgo_ref, dst_ref = rest
        else:
            parts_ref, du_ref, dlb_ref, dgo_ref, recv_ref, dst_ref, *sems = rest
            start, wait = _scatter_plan(parts_ref, recv_ref, *sems)
            pl.when(pl.program_id(0) == 0)(start)

        @pl.when(pl.program_id(0) == 0)
        def _():
            dst_ref[...] = jnp.zeros_like(dst_ref)
            dlb_ref[...] = jnp.zeros_like(dlb_ref)
            dgo_ref[...] = jnp.zeros_like(dgo_ref)

        _, vjp = jax.vjp(_hg_chunk, u_ref[:, 0:D], u_ref[:, D:2 * D], u_ref[:, 2 * D:3 * D], u_ref[:, 3 * D:4 * D],
                         sts_ref[0], lb_ref[...], go_ref[...])
        dq, df, dv, dz, dst, dlb, dgo = vjp((dp_ref[...].astype(F32), dst_ref[...]))
        du_ref[:, 0:D] = dq.astype(BF16)
        du_ref[:, D:2 * D] = df.astype(BF16)
        du_ref[:, 2 * D:3 * D] = dv.astype(BF16)
        du_ref[:, 3 * D:4 * D] = dz.astype(BF16)
        dst_ref[...] = dst
        dlb_ref[...] += dlb
        dgo_ref[...] += dgo
        if scatter is not None:
            pl.when(pl.program_id(0) == n - 1)(wait)

    rev = lambda i: (n - 1 - i, 0)
    more = scatter is not None
    return pl.pallas_call(
        body, name="hg_bwd_scatter" if more else "hg_bwd", grid=(n,),
        in_specs=[pl.BlockSpec((CHUNK, 4 * D), rev),
                  pl.BlockSpec((1, HG_H * HD, HD), lambda i: (n - 1 - i, 0, 0)),
                  pl.BlockSpec((CHUNK, D), rev),
                  pl.BlockSpec((1, D), lambda i: (0, 0)),
                  pl.BlockSpec((1, HD), lambda i: (0, 0))] + [ANY] * more,
        out_specs=[pl.BlockSpec((CHUNK, 4 * D), rev),
                   pl.BlockSpec((1, D), lambda i: (0, 0)),
                   pl.BlockSpec((1, HD), lambda i: (0, 0))] + [ANY] * more,
        out_shape=[jax.ShapeDtypeStruct((t, 4 * D), BF16), jax.ShapeDtypeStruct((1, D), F32),
                   jax.ShapeDtypeStruct((1, HD), F32)]
        + ([jax.ShapeDtypeStruct((3,) + scatter.shape[1:], scatter.dtype)] if more else []),
        scratch_shapes=[pltpu.VMEM((HG_H * HD, HD), F32)] + SCATTER_SEMS * more,
        compiler_params=_cparams(("arbitrary",)),
    )(u, sts, dp, lb, go, *([scatter] * more))


GD_VH = 16
GD_QKH = 8
GD_QKV = 4096
GD_VW = 2048
GD_N = GD_QKV + GD_VW + HD
GD_GRP = 4
GD_SOLVE = (GD_VH // GD_GRP, GD_GRP * CHUNK, 2 * HD)
HALO = 8


def _mm_high(a, b):
    return _dot(a, b, 1, 0, lax.Precision.HIGH)


def _lane_pick(a, h):
    lane = lax.broadcasted_iota(jnp.int32, a.shape, 1)
    return jnp.sum(jnp.where(lane == h, a, 0.0), axis=1, keepdims=True)


def _l2n(x):
    return x * lax.rsqrt(jnp.sum(x * x, axis=1, keepdims=True) + EPS)


def _solve_fwd(a_mats, rhss):
    n = a_mats[0].shape[0]
    r_i, c_i = lax.broadcasted_iota(jnp.int32, (n, n), 0), lax.broadcasted_iota(jnp.int32, (n, n), 1)
    same = lambda nb: (r_i // nb) == (c_i // nb)
    eye = (r_i == c_i).astype(F32)
    d0s = [jnp.where(same(8), a, 0.0) for a in a_mats]
    d2s = [_mm(d, d) for d in d0s]
    tinvs = [eye - d for d in d0s]
    tinvs = [t + _mm(t, d2) for t, d2 in zip(tinvs, d2s)]
    d4s = [_mm(d2, d2) for d2 in d2s]
    tinvs = [t + _mm(t, d4) for t, d4 in zip(tinvs, d4s)]
    nb = 16
    while nb <= CHUNK:
        tls = [_mm(t, jnp.where(same(nb) & ~same(nb // 2), a, 0.0)) for t, a in zip(tinvs, a_mats)]
        tinvs = [t - _mm(tl, t) for t, tl in zip(tinvs, tls)]
        nb *= 2
    return tinvs, [_mm(t, r) for t, r in zip(tinvs, rhss)]


def _solve_bwd(res, dx):
    tinv, x = res
    drhs = _mm_tn(tinv, dx)
    return -_mm_nt(drhs, x), drhs


@jax.custom_vjp
def _solved(a_mat, rhs, tinv, x):
    return x


_solved.defvjp(lambda a_mat, rhs, tinv, x: (x, (tinv, x)),
               lambda res, dx: _solve_bwd(res, dx) + (jnp.zeros_like(res[0]), jnp.zeros_like(res[1])))


def _gd_chunk(xh, x, z, ab, st, cw, alog, dtb, go, solve):
    c = x.shape[0]
    xa = jnp.concatenate([xh, x], axis=0)
    sh = [jnp.split(_roll_rows(xa, 3 - j), [HALO], axis=0)[1] for j in range(3)]
    qkv = _silu(cw[0:1] * sh[0] + cw[1:2] * sh[1] + cw[2:3] * sh[2] + cw[3:4] * x)
    q_all, k_all, v_all = jnp.split(qkv, [1024, 2048], axis=1)
    lane = lax.broadcasted_iota(jnp.int32, (c, HD), 1)
    a_part = jnp.where(lane < GD_VH, ab, 0.0)
    g_all = -jnp.exp(alog) * jax.nn.softplus(a_part + dtb)
    d_all = _cumsum_rows(g_all)
    dl_all = jnp.sum(g_all, axis=0, keepdims=True)
    beta_all = jax.nn.sigmoid(ab)
    gc = GD_GRP * c
    r_i, c_i = lax.broadcasted_iota(jnp.int32, (gc, gc), 0), lax.broadcasted_iota(jnp.int32, (gc, gc), 1)
    same_head = (r_i // c) == (c_i // c)
    tri_g, strict_g = same_head & (c_i <= r_i), same_head & (c_i < r_i)
    qs =jnp.split(q_all, GD_QKH, axis=1)
    ks = jnp.split(k_all, GD_QKH, axis=1)
    vs = jnp.split(v_all, GD_VH, axis=1)
    zs = jnp.split(z, GD_VH, axis=1)
    sts = jnp.split(st, GD_VH, axis=0)
    qn = [_l2n(a) * (HD ** -0.5) for a in qs]
    kn = [_l2n(a) for a in ks]
    p_out, st_out, pre = [], [], []
    for g in range(GD_VH // GD_GRP):
        heads = range(GD_GRP * g, GD_GRP * (g + 1))
        stack = lambda f: jnp.concatenate([f(h) for h in heads], axis=0)
        q_, k_, v_ = stack(lambda h: qn[h // 2]), stack(lambda h: kn[h // 2]), stack(lambda h: vs[h])
        dcol = stack(lambda h: _lane_pick(d_all, h))
        bcol = stack(lambda h: _lane_pick(beta_all, GD_VH + h))
        dlast = stack(lambda h: jnp.broadcast_to(_lane_pick(dl_all, h), (c, 1)))
        drow = jnp.sum(jnp.broadcast_to(dcol, (gc, HD)).T, axis=0, keepdims=True) * (1.0 / HD)
        dec = jnp.exp(jnp.where(tri_g, dcol - drow, -jnp.inf))
        kb = k_ * bcol
        a_mat = jnp.where(strict_g, _mm_nt(kb, k_) * dec, 0.0)
        pre.append((heads, q_, k_, dcol, dlast, dec, a_mat, jnp.concatenate([v_ * bcol, kb * jnp.exp(dcol)], axis=1)))
    xsols = solve([e[6] for e in pre], [e[7] for e in pre])
    heads_of = [e[0] for e in pre]
    per_head = lambda a: jnp.split(a, GD_GRP, axis=0)
    uw = [jnp.split(x, 2, axis=1) for x in xsols]
    ws = [[_mm(wh, sts[h]) for wh, h in zip(per_head(w_), heads)] for (_, w_), heads in zip(uw, heads_of)]
    v_new = [u_ - jnp.concatenate(w, axis=0) for (u_, _), w in zip(uw, ws)]
    qk = [_mm_nt(e[1], e[2]) * e[5] for e in pre]
    qs_ = [[_mm(qh, sts[h]) for qh, h in zip(per_head(e[1] * jnp.exp(e[3])), e[0])] for e in pre]
    o_g = [_mm(a, vn) + jnp.concatenate(b, axis=0) for a, vn, b in zip(qk, v_new, qs_)]
    upd = [[_mm_tn(kh, vh) for kh, vh in zip(per_head(e[2] * jnp.exp(e[4] - e[3])), per_head(vn))]
           for e, vn in zip(pre, v_new)]
    for heads, og, up in zip(heads_of, o_g, upd):
        for h, o, u_st in zip(heads, per_head(og), up):
            st_out.append(sts[h] * jnp.exp(_lane_pick(dl_all, h)) + u_st)
            y = o * lax.rsqrt(jnp.mean(o * o, axis=1, keepdims=True) + EPS) * go
            p_out.append(y * _silu(zs[h]))
    return jnp.concatenate(p_out, axis=1), jnp.concatenate(st_out, axis=0)


def _gd_specs(n, rev):
    ci = (lambda i: n - 1 - i) if rev else (lambda i: i)
    return [pl.BlockSpec((HALO, GD_QKV), lambda i: (jnp.maximum(ci(i) * (CHUNK // HALO) - 1, 0), 0)),
            pl.BlockSpec((CHUNK, GD_N), lambda i: (ci(i), 0))]


def _gd_load(uh_ref, u_ref, first):
    xh = jnp.where(first, 0.0, uh_ref[...])
    return xh, u_ref[:, 0:GD_QKV], u_ref[:, GD_QKV:GD_QKV + GD_VW], u_ref[:, GD_QKV + GD_VW:GD_N]


def _gd_fwd(u, cw, alog, dtb, go):
    t = u.shape[0]
    n = t // CHUNK
    small = lambda r, w: pl.BlockSpec((r, w), lambda i: (0, 0))

    def body(uh_ref, u_ref, cw_ref, alog_ref, dtb_ref, go_ref, p_ref, sts_ref, tinv_ref, xsol_ref, st_ref):
        i = pl.program_id(0)

        @pl.when(i == 0)
        def _():
            st_ref[...] = jnp.zeros_like(st_ref)

        def solve(a_mats, rhss):
            tinvs, xsols = _solve_fwd(a_mats, rhss)
            for g, (tinv, xsol) in enumerate(zip(tinvs, xsols)):
                tinv_ref[0, g] = tinv
                xsol_ref[0, g] = xsol
            return xsols

        st = st_ref[...]
        sts_ref[0] = st
        p, st_next = _gd_chunk(*_gd_load(uh_ref, u_ref, i == 0), st, cw_ref[...], alog_ref[...], dtb_ref[...],
                               go_ref[...], solve)
        p_ref[...] = p.astype(BF16)
        st_ref[...] = st_next

    return pl.pallas_call(
        body, name="gd_fwd", grid=(n,),
        in_specs=_gd_specs(n, False) + [small(8, GD_QKV), small(1, HD), small(1, HD), small(1, HD)],
        out_specs=[pl.BlockSpec((CHUNK, GD_VW), lambda i: (i, 0)),
                   pl.BlockSpec((1, GD_VH * HD, HD), lambda i: (i, 0, 0)),
                   pl.BlockSpec((1,) + GD_SOLVE, lambda i: (i, 0, 0, 0)),
                   pl.BlockSpec((1,) + GD_SOLVE, lambda i: (i, 0, 0, 0))],
        out_shape=[jax.ShapeDtypeStruct((t, GD_VW), BF16), jax.ShapeDtypeStruct((n, GD_VH * HD, HD), F32),
                   jax.ShapeDtypeStruct((n,) + GD_SOLVE, F32), jax.ShapeDtypeStruct((n,) + GD_SOLVE, F32)],
        scratch_shapes=[pltpu.VMEM((GD_VH * HD, HD), F32)],
        compiler_params=_cparams(("arbitrary",)),
    )(u, u, cw, alog, dtb, go)


def _gd_bwd(u, sts, tinvs, xsols, dp, cw, alog, dtb, go):
    t = u.shape[0]
    n = t // CHUNK
    small = lambda r, w: pl.BlockSpec((r, w), lambda i: (0, 0))

    def body(uh_ref, u_ref, sts_ref, tinv_ref, xsol_ref, dp_ref, cw_ref, alog_ref, dtb_ref, go_ref,
             du_ref, dcw_ref, dalog_ref, ddtb_ref, dgo_ref, dst_ref, dhalo_ref):
        i = pl.program_id(0)

        @pl.when(i == 0)
        def _():
            for r in (dst_ref, dhalo_ref, dcw_ref, dalog_ref, ddtb_ref, dgo_ref):
                r[...] = jnp.zeros_like(r)

        solve = lambda a_mats, rhss: [_solved(a, r, tinv_ref[0, g], xsol_ref[0, g])
                                      for g, (a, r) in enumerate(zip(a_mats, rhss))]
        chunk = functools.partial(_gd_chunk, solve=solve)
        _, vjp = jax.vjp(chunk, *_gd_load(uh_ref, u_ref, i == n - 1), sts_ref[0], cw_ref[...], alog_ref[...],
                         dtb_ref[...], go_ref[...])
        dxh, dx, dz, dab, dst, dcw, dalog, ddtb, dgo = vjp((dp_ref[...].astype(F32), dst_ref[...]))
        tail = jnp.concatenate([jnp.zeros((CHUNK - HALO, GD_QKV), F32), dhalo_ref[...]], axis=0)
        du_ref[:, 0:GD_QKV] = (dx + tail).astype(BF16)
        du_ref[:, GD_QKV:GD_QKV + GD_VW] = dz.astype(BF16)
        du_ref[:, GD_QKV + GD_VW:GD_N] = dab.astype(BF16)
        dhalo_ref[...] = dxh
        dst_ref[...] = dst
        dcw_ref[...] += dcw
        dalog_ref[...] += dalog
        ddtb_ref[...] += ddtb
        dgo_ref[...] += dgo

    return pl.pallas_call(
        body, name="gd_bwd", grid=(n,),
        in_specs=_gd_specs(n, True) + [pl.BlockSpec((1, GD_VH * HD, HD), lambda i: (n - 1 - i, 0, 0)),
                                       pl.BlockSpec((1,) + GD_SOLVE, lambda i: (n - 1 - i, 0, 0, 0)),
                                       pl.BlockSpec((1,) + GD_SOLVE, lambda i: (n - 1 - i, 0, 0, 0)),
                                       pl.BlockSpec((CHUNK, GD_VW), lambda i: (n - 1 - i, 0)),
                                       small(8, GD_QKV), small(1, HD), small(1, HD), small(1, HD)],
        out_specs=[pl.BlockSpec((CHUNK, GD_N), lambda i: (n - 1 - i, 0)),
                   small(8, GD_QKV), small(1, HD), small(1, HD), small(1, HD)],
        out_shape=[jax.ShapeDtypeStruct((t, GD_N), BF16), jax.ShapeDtypeStruct((8, GD_QKV), F32)]
        + [jax.ShapeDtypeStruct((1, HD), F32)] * 3,
        scratch_shapes=[pltpu.VMEM((GD_VH * HD, HD), F32), pltpu.VMEM((HALO, GD_QKV), F32)],
        compiler_params=_cparams(("arbitrary",)),
    )(u, u, sts, tinvs, xsols, dp, cw, alog, dtb, go)


SW_B = 128
SW_G = 4
SW_N = 2560
SW_KV0 = 1024


def _blockdiag(n, blk):
    r = lax.broadcasted_iota(jnp.int32, (n, n), 0) // blk
    c = lax.broadcasted_iota(jnp.int32, (n, n), 1) // blk
    return (r == c).astype(F32)


def _sw_normrope(x, g1, g2, cos, sin):
    w = x.shape[1] // 2
    x1, x2 = jnp.split(x, 2, axis=1)
    ms = _mm_high(x1 * x1 + x2 * x2, _blockdiag(w, 32)) * (1.0 / 64.0)
    rinv = lax.rsqrt(ms + EPS)
    n1, n2 = x1 * rinv * g1, x2 * rinv * g2
    return jnp.concatenate([n1 * cos - n2 * sin, n2 * cos + n1 * sin], axis=1)


def _sw_block(q, kvp, kvc, z, csp, csc, gq, gk, sinks, has_prev):
    b = q.shape[0]
    cos_c, sin_c = jnp.split(csc, 2, axis=1)
    cos_p, sin_p = jnp.split(csp, 2, axis=1)
    tile4 = lambda a: jnp.concatenate([a] * 4, axis=1)
    qh = _sw_normrope(q, gq[0:1], gq[1:2], tile4(cos_c), tile4(sin_c))
    kp, vp = jnp.split(kvp, 2, axis=1)
    kc, vc = jnp.split(kvc, 2, axis=1)
    kh = jnp.concatenate([_sw_normrope(kp, gk[0:1], gk[1:2], cos_p, sin_p),
                          _sw_normrope(kc, gk[0:1], gk[1:2], cos_c, sin_c)], axis=0)
    vv = jnp.concatenate([vp, vc], axis=0)
    q1, q2 = jnp.split(qh, 2, axis=1)
    q1g, q2g = jnp.split(q1, SW_G, axis=1), jnp.split(q2, SW_G, axis=1)
    own = lax.broadcasted_iota(jnp.int32, (4 * b, b), 1) <= lax.broadcasted_iota(jnp.int32, (4 * b, b), 0) % b
    ri = lax.broadcasted_iota(jnp.int32, (256, 256), 0)
    ci = lax.broadcasted_iota(jnp.int32, (256, 256), 1)
    row_head = lax.broadcasted_iota(jnp.int32, (4 * b, 256), 0) // b
    lane_q = lax.broadcasted_iota(jnp.int32, (4 * b, 256), 1)
    q_sel = (lane_q % 128) // 32 == row_head
    o_sel = lane_q // 64 == row_head
    groups = range(SW_G)
    ek = [((ri // 128 == ci // 128) & ((ri % 128) // 32 == g) & (ri % 32 == ci % 32)).astype(F32) for g in groups]
    ev = [((ri // 64 == g) & (ri % 64 == ci % 64)).astype(F32) for g in groups]
    kx = [_mm(kh, ek[g]) for g in groups]
    vx = [_mm(vv, ev[g]) for g in groups]
    q4 = [jnp.where(q_sel, jnp.concatenate([jnp.concatenate([q1g[g], q2g[g]], axis=1)] * 4, axis=0), 0.0)
          for g in groups]
    sink = [jnp.concatenate([jnp.broadcast_to(_lane_pick(sinks, 4 * g + j), (b, 1)) for j in range(4)], axis=0)
            for g in groups]
    sc = [jnp.split(_mm_nt(q4[g], kx[g]) * (64 ** -0.5), 2, axis=1) for g in groups]
    s = [jnp.where(own, sc[g][1], jnp.where(has_prev, sc[g][0], -jnp.inf)) for g in groups]
    top = [jnp.max(s[g]) for g in groups]
    soft = [top[g] + 8.0 * jnp.log(jnp.sum(jnp.exp((s[g] - top[g]) * 0.125), axis=1, keepdims=True)) for g in groups]
    m = [lax.stop_gradient(jnp.maximum(soft[g], sink[g])) for g in groups]
    p = [jnp.exp(s[g] - m[g]) for g in groups]
    pn = [p[g] / (jnp.sum(p[g], axis=1, keepdims=True) + jnp.exp(sink[g] - m[g])) for g in groups]
    pn2 = [jnp.concatenate([jnp.where(own, 0.0, pn[g]), jnp.where(own, pn[g], 0.0)], axis=1) for g in groups]
    o4 = [jnp.split(jnp.where(o_sel, _mm(pn2[g], vx[g]), 0.0), 4, axis=0) for g in groups]
    o_out = [o4[g][0] + o4[g][1] + o4[g][2] + o4[g][3] for g in groups]
    return jnp.concatenate(o_out, axis=1) * _silu(z)


def _sw_specs(n, rev):
    ci = (lambda i: n - 1 - i) if rev else (lambda i: i)
    prev = lambda i: jnp.maximum(ci(i) - 1, 0)
    return [pl.BlockSpec((SW_B, SW_N), lambda i: (ci(i), 0)),
            pl.BlockSpec((SW_B, 512), lambda i: (prev(i), SW_KV0 // 512)),
            pl.BlockSpec((SW_B, 256), lambda i: (ci(i), 0)),
            pl.BlockSpec((SW_B, 256), lambda i: (prev(i), 0)),
            pl.BlockSpec((2, 512), lambda i: (0, 0)), pl.BlockSpec((2, 128), lambda i: (0, 0)),
            pl.BlockSpec((1, 128), lambda i: (0, 0))]


def _sw_args(u_ref, kvp_ref, csc_ref, csp_ref, gq_ref, gk_ref, sk_ref, has_prev):
    return (u_ref[:, 0:D], kvp_ref[...], u_ref[:, SW_KV0:SW_KV0 + 512], u_ref[:, SW_KV0 + 512:SW_N],
            csp_ref[...], csc_ref[...], gq_ref[...], gk_ref[...], sk_ref[...], has_prev)


def _sw_fwd(u, cs, gq, gk, sinks):
    t = u.shape[0]
    n = t // SW_B

    def body(u_ref, kvp_ref, csc_ref, csp_ref, gq_ref, gk_ref, sk_ref, p_ref):
        has_prev = pl.program_id(0) > 0
        p_ref[...] = _sw_block(*_sw_args(u_ref, kvp_ref, csc_ref, csp_ref, gq_ref, gk_ref, sk_ref, has_prev)
                               ).astype(BF16)

    return pl.pallas_call(
        body, name="sw_fwd", grid=(n,), in_specs=_sw_specs(n, False),
        out_specs=pl.BlockSpec((SW_B, D), lambda i: (i, 0)),
        out_shape=jax.ShapeDtypeStruct((t, D), BF16),
        compiler_params=_cparams(("arbitrary",)),
    )(u, u, cs, cs, gq, gk, sinks)


def _sw_bwd(u, cs, dp, gq, gk, sinks):
    t = u.shape[0]
    n = t // SW_B

    def body(u_ref, kvp_ref, csc_ref, csp_ref, gq_ref, gk_ref, sk_ref, dp_ref,
             du_ref, dgq_ref, dgk_ref, dsk_ref, dkv_ref):
        i = pl.program_id(0)

        @pl.when(i == 0)
        def _():
            for r in (dkv_ref, dgq_ref, dgk_ref, dsk_ref):
                r[...] = jnp.zeros_like(r)

        has_prev = i < n - 1
        args = _sw_args(u_ref, kvp_ref, csc_ref, csp_ref, gq_ref, gk_ref, sk_ref, has_prev)
        fn = lambda q, kvp, kvc, z, gq_, gk_, sk_: _sw_block(q, kvp, kvc, z, args[4], args[5], gq_, gk_, sk_, has_prev)
        _, vjp = jax.vjp(fn, args[0], args[1], args[2], args[3], args[6], args[7], args[8])
        dq, dkvp, dkvc, dz, dgq, dgk, dsk = vjp(dp_ref[...].astype(F32))
        du_ref[:, 0:D] = dq.astype(BF16)
        du_ref[:, SW_KV0:SW_KV0 + 512] = (dkvc + dkv_ref[...]).astype(BF16)
        du_ref[:, SW_KV0 + 512:SW_N] = dz.astype(BF16)
        dkv_ref[...] = dkvp
        dgq_ref[...] += dgq
        dgk_ref[...] += dgk
        dsk_ref[...] += dsk

    small = lambda r, w: pl.BlockSpec((r, w), lambda i: (0, 0))
    return pl.pallas_call(
        body, name="sw_bwd", grid=(n,),
        in_specs=_sw_specs(n, True) + [pl.BlockSpec((SW_B, D), lambda i: (n - 1 - i, 0))],
        out_specs=[pl.BlockSpec((SW_B, SW_N), lambda i: (n - 1 - i, 0)), small(2, 512), small(2, 128), small(1, 128)],
        out_shape=[jax.ShapeDtypeStruct((t, SW_N), BF16), jax.ShapeDtypeStruct((2, 512), F32),
                   jax.ShapeDtypeStruct((2, 128), F32), jax.ShapeDtypeStruct((1, 128), F32)],
        scratch_shapes=[pltpu.VMEM((SW_B, 512), F32)],
        compiler_params=_cparams(("arbitrary",)),
    )(u, u, cs, cs, gq, gk, sinks, dp)


def _ln_mod(x, g, scale, shift):
    y = x * lax.rsqrt(jnp.mean(x * x, axis=1, keepdims=True) + EPS) * g
    return y * (1.0 + scale) + shift


def _row_tile(t):
    return min(t, 1024)


def _ln_mm(x, g, scale, shift, w, tn):
    t, n = x.shape[0], w.shape[1]
    tm = _row_tile(t)
    vec = pl.BlockSpec((1, D), lambda i, j: (0, 0))

    def body(x_ref, g_ref, sc_ref, sh_ref, w_ref, u_ref, h_ref):
        @pl.when(pl.program_id(1) == 0)
        def _():
            h_ref[...] = _ln_mod(x_ref[...], g_ref[...], sc_ref[...], sh_ref[...]).astype(BF16)

        u_ref[...] = _dot(h_ref[...], w_ref[...], 1, 0)

    return pl.pallas_call(
        body, name="ln_mm", grid=(t // tm, n // tn),
        in_specs=[pl.BlockSpec((tm, D), lambda i, j: (i, 0)), vec, vec, vec,
                  pl.BlockSpec((D, tn), lambda i, j: (0, j))],
        out_specs=[pl.BlockSpec((tm, tn), lambda i, j: (i, j)), pl.BlockSpec((tm, D), lambda i, j: (i, 0))],
        out_shape=[jax.ShapeDtypeStruct((t, n), F32), jax.ShapeDtypeStruct((t, D), BF16)],
        compiler_params=_cparams(("arbitrary", "arbitrary")),
    )(x, g, scale, shift, w)


def _mm_res(p, w, x, gate):
    t, k = p.shape
    tm = _row_tile(t)

    def body(p_ref, w_ref, x_ref, gate_ref, o_ref):
        o_ref[...] = x_ref[...] + gate_ref[...] * _dot(p_ref[...], w_ref[...], 1, 0)

    return pl.pallas_call(
        body, name="mm_res", grid=(t // tm,),
        in_specs=[pl.BlockSpec((tm, k), lambda i: (i, 0)), pl.BlockSpec((k, D), lambda i: (0, 0)),
                  pl.BlockSpec((tm, D), lambda i: (i, 0)), pl.BlockSpec((1, D), lambda i: (0, 0))],
        out_specs=pl.BlockSpec((tm, D), lambda i: (i, 0)),
        out_shape=jax.ShapeDtypeStruct((t, D), F32),
        compiler_params=_cparams(("arbitrary",)),
    )(p, w, x, gate)


def _loss_grad(x, target):
    t = x.shape[0]
    tm = _row_tile(t)

    def body(x_ref, t_ref, l_ref, dx_ref):
        @pl.when(pl.program_id(0) == 0)
        def _():
            l_ref[...] = jnp.zeros_like(l_ref)

        err = x_ref[...] - t_ref[...]
        dx_ref[...] = err * (1.0 / D)
        l_ref[...] += 0.5 * jnp.sum(jnp.mean(err * err, axis=1, keepdims=True), axis=0, keepdims=True)

    return pl.pallas_call(
        body, name="loss_grad", grid=(t // tm,),
        in_specs=[pl.BlockSpec((tm, D), lambda i: (i, 0))] * 2,
        out_specs=[pl.BlockSpec((8, 128), lambda i: (0, 0)), pl.BlockSpec((tm, D), lambda i: (i, 0))],
        out_shape=[jax.ShapeDtypeStruct((8, 128), F32), jax.ShapeDtypeStruct((t, D), F32)],
        compiler_params=_cparams(("arbitrary",)),
    )(x, target)


def _mm_scaled(a, s, w, tn):
    t, k = a.shape
    n = w.shape[1]
    tm = _row_tile(t)

    def body(a_ref, s_ref, w_ref, o_ref):
        o_ref[...] = _dot((a_ref[...] * s_ref[...]).astype(BF16), w_ref[...], 1, 0).astype(BF16)

    return pl.pallas_call(
        body, name="mm_scaled", grid=(t // tm, n // tn),
        in_specs=[pl.BlockSpec((tm, k), lambda i, j: (i, 0)), pl.BlockSpec((1, k), lambda i, j: (0, 0)),
                  pl.BlockSpec((k, tn), lambda i, j: (0, j))],
        out_specs=pl.BlockSpec((tm, tn), lambda i, j: (i, j)),
        out_shape=jax.ShapeDtypeStruct((t, n), BF16),
        compiler_params=_cparams(("arbitrary", "arbitrary")),
    )(a, s, w)


def _mm_tn_acc(a, b, tn):
    t, m = a.shape
    n = b.shape[1]
    fits = lambda k: 2 * k * (m * a.dtype.itemsize + tn * b.dtype.itemsize) + 2 * m * tn * 4 <= 36 * 1024 * 1024
    tk = next(k for k in (4096, 2048, 1024, 512, t) if t % k == 0 and (fits(k) or k <= 512))
    nk = t // tk

    def body(a_ref, b_ref, o_ref):
        @pl.when(pl.program_id(1) == 0)
        def _():
            o_ref[...] = jnp.zeros_like(o_ref)

        o_ref[...] += _dot(a_ref[...], b_ref[...].astype(BF16), 0, 0)

    return pl.pallas_call(
        body, name="mm_tn_acc", grid=(n // tn, nk),
        in_specs=[pl.BlockSpec((tk, m), lambda j, k: (k, 0)), pl.BlockSpec((tk, tn), lambda j, k: (k, j))],
        out_specs=pl.BlockSpec((m, tn), lambda j, k: (0, j)),
        out_shape=jax.ShapeDtypeStruct((m, n), F32),
        compiler_params=_cparams(("arbitrary", "arbitrary")),
    )(a, b)


def _inproj_bwd(du, wt, x, dxp, g, scale, shift):
    t, kdim = du.shape
    tk = kdim
    tm = min(t, 512 if kdim <= 4096 else 256)
    nk = kdim // tk
    vec = pl.BlockSpec((1, D), lambda i, k: (0, 0))

    def body(du_ref, wt_ref, x_ref, dxp_ref, g_ref, sc_ref, sh_ref, dx_ref, dv_ref, acc_ref):
        k = pl.program_id(1)

        @pl.when((pl.program_id(0) == 0) & (k == 0))
        def _():
            dv_ref[...] = jnp.zeros_like(dv_ref)

        @pl.when(k == 0)
        def _():
            acc_ref[...] = jnp.zeros_like(acc_ref)

        acc_ref[...] += _dot(du_ref[...].astype(BF16), wt_ref[...], 1, 0)

        @pl.when(k == nk - 1)
        def _():
            _, vjp = jax.vjp(_ln_mod, x_ref[...], g_ref[...], sc_ref[...], sh_ref[...])
            dx, dg, dsc, dsh = vjp(acc_ref[...])
            dx_ref[...] = dxp_ref[...] + dx
            dv_ref[0:1, :] += dg
            dv_ref[1:2, :] += dsc
            dv_ref[2:3, :] += dsh

    return pl.pallas_call(
        body, name="inproj_bwd", grid=(t // tm, nk),
        in_specs=[pl.BlockSpec((tm, tk), lambda i, k: (i, k)), pl.BlockSpec((tk, D), lambda i, k: (k, 0)),
                  pl.BlockSpec((tm, D), lambda i, k: (i, 0)), pl.BlockSpec((tm, D), lambda i, k: (i, 0)),
                  vec, vec, vec],
        out_specs=[pl.BlockSpec((tm, D), lambda i, k: (i, 0)), pl.BlockSpec((8, D), lambda i, k: (0, 0))],
        out_shape=[jax.ShapeDtypeStruct((t, D), F32), jax.ShapeDtypeStruct((8, D), F32)],
        scratch_shapes=[pltpu.VMEM((tm, D), F32)],
        compiler_params=_cparams(("arbitrary", "arbitrary")),
    )(du, wt, x, dxp, g, scale, shift)


def _outgrad(gmat, w, gate):
    k = gmat.shape[0]
    tr = 256

    def body(g_ref, w_ref, gate_ref, dw_ref, dg_ref):
        @pl.when(pl.program_id(0) == 0)
        def _():
            dg_ref[...] = jnp.zeros_like(dg_ref)

        gm = g_ref[...]
        dw_ref[...] = gm * gate_ref[...]
        dg_ref[0:1, :] += jnp.sum(gm * w_ref[...].astype(F32), axis=0, keepdims=True)

    return pl.pallas_call(
        body, name="outgrad", grid=(k // tr,),
        in_specs=[pl.BlockSpec((tr, D), lambda i: (i, 0)), pl.BlockSpec((tr, D), lambda i: (i, 0)),
                  pl.BlockSpec((1, D), lambda i: (0, 0))],
        out_specs=[pl.BlockSpec((tr, D), lambda i: (i, 0)), pl.BlockSpec((8, D), lambda i: (0, 0))],
        out_shape=[jax.ShapeDtypeStruct((k, D), F32), jax.ShapeDtypeStruct((8, D), F32)],
        compiler_params=_cparams(("arbitrary",)),
    )(gmat, w, gate)


def _rope_table(pos, freq):
    t = pos.shape[0]
    tm = _row_tile(t)

    def body(p_ref, f_ref, o_ref):
        ang = p_ref[...].astype(F32) * f_ref[...]
        o_ref[:, 0:128] = jnp.cos(ang)
        o_ref[:, 128:256] = jnp.sin(ang)

    return pl.pallas_call(
        body, name="rope_table", grid=(t // tm,),
        in_specs=[pl.BlockSpec((tm, 1), lambda i: (i, 0)), pl.BlockSpec((1, 128), lambda i: (0, 0))],
        out_specs=pl.BlockSpec((tm, 256), lambda i: (i, 0)),
        out_shape=jax.ShapeDtypeStruct((t, 256), F32),
        compiler_params=_cparams(("arbitrary",)),
    )(pos, freq)


def _ada_fwd(c_all, w, b):
    nl, _, s = w.shape

    def body(c_ref, w_ref, b_ref, o_ref):
        o_ref[0] = _mm_f32(c_ref[...], w_ref[0]) + b_ref[0]

    return pl.pallas_call(
        body, name="ada_fwd", grid=(nl,),
        in_specs=[pl.BlockSpec((8, D), lambda l: (0, 0)), pl.BlockSpec((1, D, s), lambda l: (l, 0, 0)),
                  pl.BlockSpec((1, 1, s), lambda l: (l, 0, 0))],
        out_specs=pl.BlockSpec((1, 8, s), lambda l: (l, 0, 0)),
        out_shape=jax.ShapeDtypeStruct((nl, 8, s), F32),
        compiler_params=_cparams(("arbitrary",)),
    )(c_all, w, b)


def _ada_bwd(c_all, dmod_cols, dmod_all):
    nl, _, s = dmod_cols.shape

    def body(c_ref, dc_ref, da_ref, gw_ref, gb_ref):
        gw_ref[0] = _dot(c_ref[...], dc_ref[0], 0, 0, lax.Precision.HIGHEST)
        gb_ref[0] = jnp.sum(da_ref[0], axis=0, keepdims=True)

    return pl.pallas_call(
        body, name="ada_bwd", grid=(nl,),
        in_specs=[pl.BlockSpec((8, D), lambda l: (0, 0)), pl.BlockSpec((1, 8, s), lambda l: (l, 0, 0)),
                  pl.BlockSpec((1, 8, 3 * D), lambda l: (l, 0, 0))],
        out_specs=[pl.BlockSpec((1, D, s), lambda l: (l, 0, 0)), pl.BlockSpec((1, 1, 3 * D), lambda l: (l, 0, 0))],
        out_shape=[jax.ShapeDtypeStruct((nl, D, s), F32), jax.ShapeDtypeStruct((nl, 1, 3 * D), F32)],
        compiler_params=_cparams(("arbitrary",)),
    )(c_all, dmod_cols, dmod_all)


def _lb_fn(h8):
    sm = jax.nn.softmax(h8, axis=0)
    r = lax.broadcasted_iota(jnp.int32, (8, 8), 0)
    c = lax.broadcasted_iota(jnp.int32, (8, 8), 1)
    return _mm_f32(((c >= 1) & (c <= r)).astype(F32), sm)


def _lb_fwd(h8):
    def body(h_ref, o_ref):
        o_ref[...] = _lb_fn(h_ref[...])

    return pl.pallas_call(body, name="lb_fwd", out_shape=jax.ShapeDtypeStruct((8, D), F32))(h8)


def _lb_bwd(h8, dlb8):
    def body(h_ref, d_ref, o_ref):
        _, vjp = jax.vjp(_lb_fn, h_ref[...])
        o_ref[...] = vjp(d_ref[...])[0]

    return pl.pallas_call(body, name="lb_bwd", out_shape=jax.ShapeDtypeStruct((8, D), F32))(h8, dlb8)


ADAM_LR, ADAM_B1, ADAM_B2, ADAM_EPS, ADAM_WD, ADAM_STEP = 0.001, 0.9, 0.999, 1e-08, 0.01, 10


def _adamw(w, gparts, m, v):
    r, c = w.shape
    tr = r if r * c * 4 <= (1 << 20) else max(8, ((1 << 20) // (c * 4)) // 8 * 8)
    while r % tr:
        tr -= 8
    ng = len(gparts)

    def body(*refs):
        w_ref, m_ref, v_ref = refs[0], refs[1 + ng], refs[2 + ng]
        g_ref, d_ref, nm_ref, nv_ref = refs[3 + ng:]
        g = refs[1][...]
        for gr in refs[2:1 + ng]:
            g = g + gr[...]
        mm = ADAM_B1 * m_ref[...] + (1.0 - ADAM_B1) * g
        vv = ADAM_B2 * v_ref[...] + (1.0 - ADAM_B2) * (g * g)
        m_hat = mm / (1.0 - ADAM_B1 ** ADAM_STEP)
        v_hat = vv / (1.0 - ADAM_B2 ** ADAM_STEP)
        g_ref[...] = g
        d_ref[...] = -ADAM_LR * (m_hat / (jnp.sqrt(v_hat) + ADAM_EPS) + ADAM_WD * w_ref[...])
        nm_ref[...] = mm
        nv_ref[...] = vv

    spec = pl.BlockSpec((tr, c), lambda i: (i, 0))
    return pl.pallas_call(
        body, name="adamw", grid=(r // tr,), in_specs=[spec] * (3 + ng), out_specs=[spec] * 4,
        out_shape=[jax.ShapeDtypeStruct((r, c), F32)] * 4,
        compiler_params=_cparams(("arbitrary",)),
    )(w, *gparts, m, v)


def _sum_rows(parts):
    r, c = parts[0].shape
    tr = 8
    for cand in range(min(r, 512), 7, -8):
        if r % cand == 0:
            tr = cand
            break

    def body(*refs):
        acc = refs[0][...]
        for p in refs[1:-1]:
            acc = acc + p[...]
        refs[-1][...] = acc

    spec = pl.BlockSpec((tr, c), lambda i: (i, 0))
    return pl.pallas_call(
        body, name="sum_rows", grid=(r // tr,), in_specs=[spec] * len(parts), out_specs=spec,
        out_shape=jax.ShapeDtypeStruct((r, c), F32),
        compiler_params=_cparams(("arbitrary",)),
    )(*parts)


MESH = pl.DeviceIdType.MESH
ANY = pl.BlockSpec(memory_space=pl.ANY)


def _place():
    return lax.axis_index("x"), lax.axis_index("y"), lax.axis_index("c")


def _allgather8(blk):
    m_per, n = blk.shape

    def body(x_ref, out_ref, send_sems, recv_sems, local_sem):
        x, y, c = _place()
        me, sibling = (x, y, c), (x, y, 1 - c)
        chips = [(1 - x, y), (x, 1 - y), (1 - x, 1 - y)]

        def rows(px, py, pc):
            return out_ref.at[pl.ds((4 * px + 2 * py + pc) * m_per, m_per), :]

        def copy(k, block, to, src=None):
            return pltpu.make_async_remote_copy(
                src_ref=rows(*block) if src is None else src, dst_ref=rows(*block),
                send_sem=send_sems.at[k], recv_sem=recv_sems.at[k], device_id=to, device_id_type=MESH)

        mine = pltpu.make_async_copy(x_ref, rows(*me), local_sem)
        mine.start()
        first = [copy(0, me, sibling, src=x_ref)]
        first += [copy(1 + j, me, (*chip, c), src=x_ref) for j, chip in enumerate(chips)]
        for cp in first:
            cp.start()
        passed = [copy(4 + j, (*chip, c), sibling) for j, chip in enumerate(chips)]
        for j, chip in enumerate(chips):
            copy(1 + j, (*chip, c), me).wait_recv()
            passed[j].start()
        copy(0, sibling, me).wait_recv()
        for j, chip in enumerate(chips):
            copy(4 + j, (*chip, 1 - c), me).wait_recv()
        for cp in first + passed:
            cp.wait_send()
        mine.wait()

    return pl.pallas_call(
        body, name="allgather8",
        out_shape=jax.ShapeDtypeStruct((8 * m_per, n), blk.dtype),
        in_specs=[pl.BlockSpec(memory_space=pltpu.VMEM)],
        out_specs=pl.BlockSpec(memory_space=pltpu.VMEM),
        scratch_shapes=[pltpu.SemaphoreType.DMA((7,)), pltpu.SemaphoreType.DMA((7,)), pltpu.SemaphoreType.DMA],
    )(blk)


def _chip_peers():
    x, y, c = _place()
    return [(1 - x, y, c), (x, 1 - y, c), (1 - x, 1 - y, c)]


GATHER_SEMS = [pltpu.SemaphoreType.DMA((3,)), pltpu.SemaphoreType.DMA((3,)), pltpu.SemaphoreType.DMA]
SCATTER_SEMS = [pltpu.SemaphoreType.DMA((3,)), pltpu.SemaphoreType.DMA((3,))]


def _gather_plan(x_ref, out_ref, send_sems, recv_sems, local_sem):
    x, y, _ = _place()
    peers = _chip_peers()

    def copy(j, chip_index):
        return pltpu.make_async_remote_copy(
            src_ref=x_ref, dst_ref=out_ref.at[chip_index], send_sem=send_sems.at[j], recv_sem=recv_sems.at[j],
            device_id=peers[j], device_id_type=MESH)

    mine = pltpu.make_async_copy(x_ref, out_ref.at[2 * x + y], local_sem)
    sends = [copy(j, 2 * x + y) for j in range(3)]

    def start():
        mine.start()
        for cp in sends:
            cp.start()

    def wait():
        for j in range(3):
            copy(j, 2 * peers[j][0] + peers[j][1]).wait_recv()
        for cp in sends:
            cp.wait_send()
        mine.wait()

    return start, wait


def _scatter_plan(p_ref, out_ref, send_sems, recv_sems):
    peers = _chip_peers()
    sends = [pltpu.make_async_remote_copy(
        src_ref=p_ref.at[2 * peers[j][0] + peers[j][1]], dst_ref=out_ref.at[j], send_sem=send_sems.at[j],
        recv_sem=recv_sems.at[j], device_id=peers[j], device_id_type=MESH) for j in range(3)]

    def start():
        for cp in sends:
            cp.start()

    def wait():
        for cp in sends:
            cp.wait_recv()
        for cp in sends:
            cp.wait_send()

    return start, wait


def _chip_allgather(shard):
    def body(x_ref, out_ref, *sems):
        start, wait = _gather_plan(x_ref, out_ref, *sems)
        start()
        wait()

    return pl.pallas_call(
        body, name="chip_allgather", out_shape=jax.ShapeDtypeStruct((4,) + shard.shape, shard.dtype),
        in_specs=[ANY], out_specs=ANY, scratch_shapes=GATHER_SEMS,
    )(shard)


def _chip_scatter(parts):
    def body(p_ref, out_ref, *sems):
        start, wait = _scatter_plan(p_ref, out_ref, *sems)
        start()
        wait()

    return pl.pallas_call(
        body, name="chip_scatter", out_shape=jax.ShapeDtypeStruct((3,) + parts.shape[1:], parts.dtype),
        in_specs=[ANY], out_specs=ANY, scratch_shapes=SCATTER_SEMS,
    )(parts)


def _sibling_swap(a):
    def body(a_ref, out_ref, send_sem, recv_sem):
        x, y, c = _place()
        cp = pltpu.make_async_remote_copy(src_ref=a_ref, dst_ref=out_ref, send_sem=send_sem, recv_sem=recv_sem,
                                          device_id=(x, y, 1 - c), device_id_type=MESH)
        cp.start()
        cp.wait_recv()
        cp.wait_send()

    return pl.pallas_call(
        body, name="sibling_swap", out_shape=jax.ShapeDtypeStruct(a.shape, a.dtype),
        in_specs=[ANY], out_specs=ANY,
        scratch_shapes=[pltpu.SemaphoreType.DMA, pltpu.SemaphoreType.DMA],
    )(a)


WEIGHTS = ['hgrn_lb', 'ada_w', 'ada_b', 'norm_g', 'hg_in_w', 'hg_out_w', 'hg_onorm', 'sw_in_w', 'sw_out_w', 'sw_qnorm',
           'sw_knorm', 'sw_sinks', 'gd_in_w', 'gd_out_w', 'gd_conv_w', 'gd_a_log', 'gd_dt_bias', 'gd_onorm']
BIG = ['hg_in_w', 'hg_out_w', 'sw_in_w', 'sw_out_w', 'gd_in_w', 'gd_out_w']
SEG_FIRST = [('hg_in_w', 0), ('hg_out_w', 0)]
SEG_REST = [('hg_in_w', 1), ('hg_out_w', 1), ('sw_in_w', 0), ('sw_out_w', 0), ('gd_in_w', 0), ('gd_out_w', 0)]
PACK_ALIGN = 16
ROPE_THETA = 10000.0
ADA_S = 3 * D // 4
SMALL_ROW = {'hg_onorm': (0, 256), 'sw_qnorm': (256, 64), 'sw_knorm': (320, 64), 'sw_sinks': (384, 16),
             'gd_a_log': (400, 16), 'gd_dt_bias': (416, 16), 'gd_onorm': (432, 128)}


def _pack_rows(arrs):
    flat = jnp.concatenate([a.reshape(-1, D) for a in arrs], axis=0)
    return jnp.pad(flat, ((0, -flat.shape[0] % PACK_ALIGN), (0, 0)))


def _unpack_rows(packed, shapes):
    out, off = [], 0
    for s in shapes:
        rows = 1
        for d in s:
            rows *= d
        rows //= D
        out.append(packed[..., off:off + rows, :].reshape(packed.shape[:-2] + tuple(s)))
        off += rows
    return out


def _pack_small(vals):
    row = jnp.concatenate([vals[k].reshape(-1) for k in SMALL_ROW])
    row = jnp.pad(row, (0, D - row.shape[0]))[None]
    return jnp.concatenate([vals['hgrn_lb'], vals['norm_g'], vals['gd_conv_w'].reshape(16, D), row,
                            jnp.zeros((7, D), F32)], axis=0)


def _sw_cols(w, inverse=False):
    def split(a, heads):
        shp = (a.shape[0], 2, heads, 32) if inverse else (a.shape[0], heads, 2, 32)
        return a.reshape(shp).transpose(0, 2, 1, 3).reshape(a.shape[0], heads * 64)
    return jnp.concatenate([split(w[:, 0:1024], 16), split(w[:, 1024:1280], 4), w[:, 1280:]], axis=1)


def kernel(x, c, positions, hgrn_lb, ada_w, ada_b, norm_g, hg_in_w, hg_out_w, hg_onorm, sw_in_w, sw_out_w, sw_qnorm, sw_knorm, sw_sinks, gd_in_w, gd_out_w, gd_conv_w, gd_a_log, gd_dt_bias, gd_onorm, loss_target, m_hgrn_lb, m_ada_w, m_ada_b, m_norm_g, m_hg_in_w, m_hg_out_w, m_hg_onorm, m_sw_in_w, m_sw_out_w, m_sw_qnorm, m_sw_knorm, m_sw_sinks, m_gd_in_w, m_gd_out_w, m_gd_conv_w, m_gd_a_log, m_gd_dt_bias, m_gd_onorm, v_hgrn_lb, v_ada_w, v_ada_b, v_norm_g, v_hg_in_w, v_hg_out_w, v_hg_onorm, v_sw_in_w, v_sw_out_w, v_sw_qnorm, v_sw_knorm, v_sw_sinks, v_gd_in_w, v_gd_out_w, v_gd_conv_w, v_gd_a_log, v_gd_dt_bias, v_gd_onorm):
    w_in = dict(hgrn_lb=hgrn_lb, ada_w=ada_w, ada_b=ada_b, norm_g=norm_g, hg_in_w=hg_in_w, hg_out_w=hg_out_w,
                hg_onorm=hg_onorm, sw_in_w=sw_in_w, sw_out_w=sw_out_w, sw_qnorm=sw_qnorm, sw_knorm=sw_knorm,
                sw_sinks=sw_sinks, gd_in_w=gd_in_w, gd_out_w=gd_out_w, gd_conv_w=gd_conv_w, gd_a_log=gd_a_log,
                gd_dt_bias=gd_dt_bias, gd_onorm=gd_onorm)
    m_in = dict(zip(WEIGHTS, (m_hgrn_lb, m_ada_w, m_ada_b, m_norm_g, m_hg_in_w, m_hg_out_w, m_hg_onorm, m_sw_in_w,
                              m_sw_out_w, m_sw_qnorm, m_sw_knorm, m_sw_sinks, m_gd_in_w, m_gd_out_w, m_gd_conv_w,
                              m_gd_a_log, m_gd_dt_bias, m_gd_onorm)))
    v_in = dict(zip(WEIGHTS, (v_hgrn_lb, v_ada_w, v_ada_b, v_norm_g, v_hg_in_w, v_hg_out_w, v_hg_onorm, v_sw_in_w,
                              v_sw_out_w, v_sw_qnorm, v_sw_knorm, v_sw_sinks, v_gd_in_w, v_gd_out_w, v_gd_conv_w,
                              v_gd_a_log, v_gd_dt_bias, v_gd_onorm)))
    ax, ay, ac = _place()
    chip = 2 * ax + ay
    bidx = 4 * ax + 2 * ay + ac
    t = x.shape[1]
    x0, target = x[0], loss_target[0]

    c_all = _allgather8(jnp.pad(c, ((0, 7), (0, 0)))).reshape(8, 8, D)[:, 0, :]
    ada_b_cols = lax.dynamic_slice(ada_b, (0, chip * ADA_S), (4, ADA_S)).reshape(4, 1, ADA_S)
    mod_sh = _ada_fwd(c_all, ada_w, ada_b_cols)
    mod_g = _allgather8(mod_sh.reshape(32, ADA_S)).reshape(4, 2, 4, 8, ADA_S)[:, 0]
    mod = lax.dynamic_index_in_dim(mod_g, bidx, axis=2, keepdims=False).transpose(1, 0, 2).reshape(4, 3 * D)
    shift = [mod[l:l + 1, 0:D] for l in range(4)]
    scale = [mod[l:l + 1, D:2 * D] for l in range(4)]
    gate = [mod[l:l + 1, 2 * D:3 * D] for l in range(4)]

    h8 = jnp.concatenate([hgrn_lb, jnp.full((4, D), -1e30, F32)], axis=0)
    lb_all = _lb_fwd(h8)
    freq = ROPE_THETA ** (-jnp.arange(0, 64, 2, dtype=F32) / 64)
    cs = _rope_table(positions.reshape(t, 1), jnp.tile(freq, 4)[None])

    seg_shapes = lambda seg: [w_in[k].shape[1:] for k, _ in seg]
    pack_seg = lambda src, seg: _pack_rows([src[k][i] for k, i in seg])
    cols_full = lambda a: a.transpose(1, 0, 2).reshape(a.shape[1], 4 * a.shape[2])
    hg_in0_k, hg_out0_k = _unpack_rows(_chip_allgather(pack_seg(w_in, SEG_FIRST).astype(BF16)), seg_shapes(SEG_FIRST))
    win, wout = [cols_full(hg_in0_k)], [hg_out0_k.reshape(D, D)]
    rest_shard = pack_seg(w_in, SEG_REST).astype(BF16)
    tn_in = [1024, 1280, 896, 1024]

    gq = jnp.stack([jnp.tile(sw_qnorm[0, :32], 16), jnp.tile(sw_qnorm[0, 32:], 16)])
    gk = jnp.stack([jnp.tile(sw_knorm[0, :32], 4), jnp.tile(sw_knorm[0, 32:], 4)])
    pad128 = lambda a: jnp.pad(a, ((0, 0), (0, HD - a.shape[1])))
    sinks, alog, dtb = pad128(sw_sinks), pad128(gd_a_log), pad128(gd_dt_bias)
    cw8 = jnp.pad(_chip_allgather(gd_conv_w[0]).transpose(1, 0, 2).reshape(4, GD_QKV), ((0, 4), (0, 0)))
    lbs = {0: lb_all[0:1], 3: lb_all[3:4]}

    xs, us, hs, ps, stss = [x0], [], [], [], []
    for l in range(4):
        u, h = _ln_mm(xs[l], norm_g[l:l + 1], scale[l], shift[l], win[l], tn_in[l])
        if l == 0:
            p, sts, rest_k = _hg_fwd(u, lbs[l], hg_onorm[0:1], gather=rest_shard)
            hg_in1_k, hg_out1_k, sw_in_k, sw_out_k, gd_in_k, gd_out_k = _unpack_rows(rest_k, seg_shapes(SEG_REST))
            win += [_sw_cols(cols_full(sw_in_k)), jnp.pad(cols_full(gd_in_k), ((0, 0), (0, GD_N - 6176))),
                    cols_full(hg_in1_k)]
            wout += [sw_out_k.reshape(D, D), gd_out_k.reshape(GD_VW, D), hg_out1_k.reshape(D, D)]
        elif l % 3 == 0:
            p, sts = _hg_fwd(u, lbs[l], hg_onorm[l // 3:l // 3 + 1])
        elif l % 3 == 1:
            p, sts = _sw_fwd(u, cs, gq, gk, sinks), None
        else:
            p, *sts = _gd_fwd(u, cw8, alog, dtb, gd_onorm)
        xs.append(_mm_res(p, wout[l], xs[l], gate[l]))
        us.append(u), hs.append(h), ps.append(p), stss.append(sts)
    lpart, dx = _loss_grad(xs[4], target)
    loss = lax.psum(lpart[0, 0], ("x", "y", "c"))

    by_chip = lambda g, cols: g.reshape(g.shape[0], 4, cols).transpose(1, 0, 2)
    g_small = {}
    d_in, d_out, dmod, dnorm_g, dlb8, dgo_hg = [None] * 4, [None] * 4, [None] * 4, [None] * 4, jnp.zeros((8, D), F32), {}
    for l in (3, 2, 1, 0):
        dp = _mm_scaled(dx, gate[l], wout[l].T, 1024)
        d_out[l], dgate = _outgrad(_mm_tn_acc(ps[l], dx, D if ps[l].shape[1] == D else 512), wout[l], gate[l])
        if l == 0:
            rest_parts = {('hg_in_w', 1): by_chip(d_in[3], D), ('hg_out_w', 1): d_out[3].reshape(4, D // 4, D),
                          ('sw_in_w', 0): by_chip(_sw_cols(d_in[1], inverse=True), SW_N // 4),
                          ('sw_out_w', 0): d_out[1].reshape(4, D // 4, D),
                          ('gd_in_w', 0): by_chip(d_in[2][:, :6176], 1544),
                          ('gd_out_w', 0): d_out[2].reshape(4, GD_VW // 4, D)}
            rest_packed = jnp.stack([_pack_rows([rest_parts[s][j] for s in SEG_REST]) for j in range(4)])
            du, dlb, dgo_hg[0], rest_recv = _hg_bwd(us[l], stss[l], dp, lbs[l], hg_onorm[0:1],
                                                    scatter=rest_packed.astype(BF16))
            dlb8 = lax.dynamic_update_slice(dlb8, dlb, (l, 0))
        elif l % 3 == 0:
            du, dlb, dgo_hg[l // 3] = _hg_bwd(us[l], stss[l], dp, lbs[l], hg_onorm[l // 3:l // 3 + 1])
            dlb8 = lax.dynamic_update_slice(dlb8, dlb, (l, 0))
        elif l % 3 == 1:
            du, dgq, dgk, dsk = _sw_bwd(us[l], cs, dp, gq, gk, sinks)
            g_small['sw_qnorm'] = jnp.concatenate([dgq[0].reshape(16, 32).sum(0), dgq[1].reshape(16, 32).sum(0)])
            g_small['sw_knorm'] = jnp.concatenate([dgk[0].reshape(4, 32).sum(0), dgk[1].reshape(4, 32).sum(0)])
            g_small['sw_sinks'] = dsk[0, :16]
        else:
            du, dcw, dalog, ddtb, g_small['gd_onorm'] = _gd_bwd(us[l], *stss[l], dp, cw8, alog, dtb, gd_onorm)
            g_small['gd_conv_w'], g_small['gd_a_log'], g_small['gd_dt_bias'] = dcw[:4], dalog[0, :16], ddtb[0, :16]
        d_in[l] = _mm_tn_acc(hs[l], du, 896 if l == 2 else 512)
        dx, dvec = _inproj_bwd(du, win[l].T, xs[l], dx, norm_g[l:l + 1], scale[l], shift[l])
        dnorm_g[l] = dvec[0:1]
        dmod[l] = jnp.concatenate([dvec[2:3], dvec[1:2], dgate[0:1]], axis=1)
    grad_x = dx[None]

    g_small['hgrn_lb'] = _lb_bwd(h8, dlb8)[0:4]
    g_small['norm_g'] = jnp.concatenate(dnorm_g, axis=0)
    g_small['hg_onorm'] = jnp.concatenate([dgo_hg[0], dgo_hg[1]], axis=0)
    gs_all = _allgather8(_pack_small(g_small))
    gs = _sum_rows([gs_all[32 * d:32 * (d + 1)] for d in range(8)])

    def small_view(packed, k):
        if k == 'hgrn_lb':
            return packed[0:4]
        if k == 'norm_g':
            return packed[4:8]
        off, size = SMALL_ROW[k]
        return packed[24, off:off + size].reshape(w_in[k].shape)

    conv_sl = lambda full: lax.dynamic_slice(full.reshape(4, GD_QKV), (0, chip * D), (4, D))
    out = {}

    def put(k, res, shape):
        for name, r in zip(('grad_', 'delta_', 'new_m_', 'new_v_'), res):
            out[name + k] = r.reshape(shape)

    zero_conv = dict(gd_conv_w=jnp.zeros((4, GD_QKV), F32))
    small_names = ['hgrn_lb', 'norm_g'] + list(SMALL_ROW)
    res = _adamw(_pack_small({**{k: w_in[k] for k in small_names}, **zero_conv}), (gs,),
                 _pack_small({**{k: m_in[k] for k in small_names}, **zero_conv}),
                 _pack_small({**{k: v_in[k] for k in small_names}, **zero_conv}))
    for k in small_names:
        put(k, [small_view(r, k) for r in res], w_in[k].shape)
    put('gd_conv_w', _adamw(gd_conv_w[0], (conv_sl(gs[8:24]),), m_in['gd_conv_w'][0], v_in['gd_conv_w'][0]),
        gd_conv_w.shape)

    dm = _allgather8(jnp.pad(jnp.concatenate(dmod, axis=0), ((0, 4), (0, 0)))).reshape(8, 8, 3 * D)[:, :4]
    dm = dm.transpose(1, 0, 2)
    g_ada_w, g_ada_b = _ada_bwd(c_all, lax.dynamic_slice(dm, (0, 0, chip * ADA_S), (4, 8, ADA_S)), dm)
    put('ada_w', _adamw(ada_w.reshape(4 * D, ADA_S), (g_ada_w.reshape(4 * D, ADA_S),),
                        m_in['ada_w'].reshape(4 * D, ADA_S), v_in['ada_w'].reshape(4 * D, ADA_S)), ada_w.shape)
    put('ada_b', _adamw(ada_b, (g_ada_b.reshape(4, 3 * D),), m_in['ada_b'], v_in['ada_b']), ada_b.shape)

    first_parts = {('hg_in_w', 0): by_chip(d_in[0], D), ('hg_out_w', 0): d_out[0].reshape(4, D // 4, D)}
    first_packed = jnp.stack([_pack_rows([first_parts[s][j] for s in SEG_FIRST]) for j in range(4)])
    first_recv = _chip_scatter(first_packed.astype(BF16))
    own = lambda packed: lax.dynamic_index_in_dim(packed, chip, axis=0, keepdims=False)
    half = jnp.concatenate([_sum_rows([own(first_packed), first_recv[0], first_recv[1], first_recv[2]]),
                            _sum_rows([own(rest_packed), rest_recv[0], rest_recv[1], rest_recv[2]])], axis=0)
    other = _sibling_swap(half)
    pack_all = lambda src: jnp.concatenate([pack_seg(src, SEG_FIRST), pack_seg(src, SEG_REST)], axis=0)
    res = _adamw(pack_all(w_in), (half, other), pack_all(m_in), pack_all(v_in))
    n_first = first_packed.shape[1]
    for name, r in zip(('grad_', 'delta_', 'new_m_', 'new_v_'), res):
        pieces = dict(zip(SEG_FIRST, _unpack_rows(r[:n_first], seg_shapes(SEG_FIRST))))
        pieces.update(zip(SEG_REST, _unpack_rows(r[n_first:], seg_shapes(SEG_REST))))
        for k in BIG:
            out[name + k] = jnp.stack([pieces[(k, i)] for i in range(w_in[k].shape[0])])

    return (loss, grad_x, *[out[p + k] for p in ('grad_', 'delta_', 'new_m_', 'new_v_') for k in WEIGHTS])
```

```python
import functools

import jax
import jax.numpy as jnp
from jax import lax
from jax.experimental import pallas as pl
from jax.experimental.pallas import tpu as pltpu

F32 = jnp.float32
BF16 = jnp.bfloat16
D = 1024
EPS = 1e-6
CHUNK = 64
SUB = 32
HG_H = 8
HD = 128
VMEM_LIMIT = 56 * 1024 * 1024


def _cparams(sem=None):
    return pltpu.CompilerParams(dimension_semantics=sem, vmem_limit_bytes=VMEM_LIMIT)


def _dot(a, b, ca, cb, prec=None):
    return lax.dot_general(a, b, (((ca,), (cb,)), ((), ())), precision=prec, preferred_element_type=F32)


def _mm(a, b):
    return _dot(a.astype(BF16), b.astype(BF16), 1, 0)


def _mm_nt(a, b):
    return _dot(a.astype(BF16), b.astype(BF16), 1, 1)


def _mm_tn(a, b):
    return _dot(a.astype(BF16), b.astype(BF16), 0, 0)


def _mm_f32(a, b):
    return _dot(a, b, 1, 0, lax.Precision.HIGHEST)


def _silu(x):
    return x * jax.nn.sigmoid(x)


def _cumsum_impl(x):
    row = lax.broadcasted_iota(jnp.int32, x.shape, 0)
    s = 1
    while s < x.shape[0]:
        x = x + jnp.where(row >= s, pltpu.roll(x, s, 0), 0.0)
        s *= 2
    return x


@jax.custom_vjp
def _cumsum_rows(x):
    return _cumsum_impl(x)


_cumsum_rows.defvjp(lambda x: (_cumsum_impl(x), None),
                    lambda _, g: (jnp.sum(g, axis=0, keepdims=True) - _cumsum_impl(g) + g,))


def _roll_rows(x, shift):
    n = x.shape[0]

    @jax.custom_vjp
    def f(a):
        return pltpu.roll(a, shift, 0)

    f.defvjp(lambda a: (pltpu.roll(a, shift, 0), None), lambda _, g: (pltpu.roll(g, n - shift, 0),))
    return f(x)


def _hg_chunk(q_raw, f_pre, v, z, st, lb, go):
    c = q_raw.shape[0]
    nsub = c // SUB
    lf = jnp.log(lb + (1.0 - lb) * jax.nn.sigmoid(f_pre))
    k = (1.0 - lb) * jax.nn.sigmoid(-f_pre)
    q = _silu(q_raw)
    b = _cumsum_rows(lf)
    rowf = lax.broadcasted_iota(jnp.int32, lf.shape, 0)
    bmid = [jnp.sum(jnp.where(rowf == SUB * i + SUB // 2, b, 0.0), axis=0, keepdims=True) for i in range(nsub)]
    row = lax.broadcasted_iota(jnp.int32, (c, 1), 0)
    ref = sum(jnp.where((row >= SUB * i) & (row < SUB * (i + 1)), bmid[i], 0.0) for i in range(nsub))
    qt = q * jnp.exp(b - ref)
    kall = jnp.concatenate(
        [k * jnp.exp(jnp.where(row < SUB * (i + 1), bmid[i] - b, -jnp.inf)) for i in range(nsub)], axis=0)
    v4 = jnp.concatenate([v] * nsub, axis=0)
    b_last = jnp.sum(lf, axis=0, keepdims=True)
    qb = q * jnp.exp(b)
    kd = k * jnp.exp(b_last - b)
    e_last = jnp.exp(b_last)
    tq = lax.broadcasted_iota(jnp.int32, (c, nsub * c), 0)
    cq = lax.broadcasted_iota(jnp.int32, (c, nsub * c), 1)
    m_all = ((cq // c) == (tq // SUB)) & ((cq % c) <= tq)
    hs = lambda a: jnp.split(a, HG_H, axis=1)
    qt_h, kall_h, v4_h, qb_h, kd_h, v_h, z_h, el_h = map(hs, (qt, kall, v4, qb, kd, v, z, e_last))
    st_h = jnp.split(st, HG_H, axis=0)
    heads = range(HG_H)
    pm = [jnp.where(m_all, _mm_nt(qt_h[h], kall_h[h]), 0.0) for h in heads]
    inter = [_mm_nt(qb_h[h], st_h[h]) for h in heads]
    o_h = [_mm(pm[h], v4_h[h]) + inter[h] for h in heads]
    upd = [_mm_tn(v_h[h], kd_h[h]) for h in heads]
    st_out = [el_h[h] * st_h[h] + upd[h] for h in heads]
    y_h = [o_h[h] * lax.rsqrt(jnp.mean(o_h[h] * o_h[h], axis=1, keepdims=True) + EPS) * go for h in heads]
    p_out = [y_h[h] * _silu(z_h[h]) for h in heads]
    return jnp.concatenate(p_out, axis=1), jnp.concatenate(st_out, axis=0)


def _hg_fwd(u, lb, go, gather=None):
    t = u.shape[0]
    n = t // CHUNK

    def body(u_ref, lb_ref, go_ref, *rest):
        if gather is None:
            p_ref, sts_ref, st_ref = rest
        else:
            shard_ref, p_ref, sts_ref, all_ref, st_ref, *sems = rest
            start, wait = _gather_plan(shard_ref, all_ref, *sems)
            pl.when(pl.program_id(0) == 0)(start)

        @pl.when(pl.program_id(0) == 0)
        def _():
            st_ref[...] = jnp.zeros_like(st_ref)

        st = st_ref[...]
        sts_ref[0] = st
        p, st_next = _hg_chunk(u_ref[:, 0:D], u_ref[:, D:2 * D], u_ref[:, 2 * D:3 * D], u_ref[:, 3 * D:4 * D],
                               st, lb_ref[...], go_ref[...])
        p_ref[...] = p.astype(BF16)
        st_ref[...] = st_next
        if gather is not None:
            pl.when(pl.program_id(0) == n - 1)(wait)

    more = gather is not None
    return pl.pallas_call(
        body, name="hg_fwd_gather" if more else "hg_fwd", grid=(n,),
        in_specs=[pl.BlockSpec((CHUNK, 4 * D), lambda i: (i, 0)),
                  pl.BlockSpec((1, D), lambda i: (0, 0)),
                  pl.BlockSpec((1, HD), lambda i: (0, 0))] + [ANY] * more,
        out_specs=[pl.BlockSpec((CHUNK, D), lambda i: (i, 0)),
                   pl.BlockSpec((1, HG_H * HD, HD), lambda i: (i, 0, 0))] + [ANY] * more,
        out_shape=[jax.ShapeDtypeStruct((t, D), BF16), jax.ShapeDtypeStruct((n, HG_H * HD, HD), F32)]
        + ([jax.ShapeDtypeStruct((4,) + gather.shape, gather.dtype)] if more else []),
        scratch_shapes=[pltpu.VMEM((HG_H * HD, HD), F32)] + GATHER_SEMS * more,
        compiler_params=_cparams(("arbitrary",)),
    )(u, lb, go, *([gather] * more))


def _hg_bwd(u, sts, dp, lb, go, scatter=None):
    t = u.shape[0]
    n = t // CHUNK

    def body(u_ref, sts_ref, dp_ref, lb_ref, go_ref, *rest):
        if scatter is None:
            du_ref, dlb_ref, dgo_ref, dst_ref = rest
        else:
            parts_ref, du_ref, dlb_ref, dgo_ref, recv_ref, dst_ref, *sems = rest
            start, wait = _scatter_plan(parts_ref, recv_ref, *sems)
            pl.when(pl.program_id(0) == 0)(start)

        @pl.when(pl.program_id(0) == 0)
        def _():
            dst_ref[...] = jnp.zeros_like(dst_ref)
            dlb_ref[...] = jnp.zeros_like(dlb_ref)
            dgo_ref[...] = jnp.zeros_like(dgo_ref)

        _, vjp = jax.vjp(_hg_chunk, u_ref[:, 0:D], u_ref[:, D:2 * D], u_ref[:, 2 * D:3 * D], u_ref[:, 3 * D:4 * D],
                         sts_ref[0], lb_ref[...], go_ref[...])
        dq, df, dv, dz, dst, dlb, dgo = vjp((dp_ref[...].astype(F32), dst_ref[...]))
        du_ref[:, 0:D] = dq.astype(BF16)
        du_ref[:, D:2 * D] = df.astype(BF16)
        du_ref[:, 2 * D:3 * D] = dv.astype(BF16)
        du_ref[:, 3 * D:4 * D] = dz.astype(BF16)
        dst_ref[...] = dst
        dlb_ref[...] += dlb
        dgo_ref[...] += dgo
        if scatter is not None:
            pl.when(pl.program_id(0) == n - 1)(wait)

    rev = lambda i: (n - 1 - i, 0)
    more = scatter is not None
    return pl.pallas_call(
        body, name="hg_bwd_scatter" if more else "hg_bwd", grid=(n,),
        in_specs=[pl.BlockSpec((CHUNK, 4 * D), rev),
                  pl.BlockSpec((1, HG_H * HD, HD), lambda i: (n - 1 - i, 0, 0)),
                  pl.BlockSpec((CHUNK, D), rev),
                  pl.BlockSpec((1, D), lambda i: (0, 0)),
                  pl.BlockSpec((1, HD), lambda i: (0, 0))] + [ANY] * more,
        out_specs=[pl.BlockSpec((CHUNK, 4 * D), rev),
                   pl.BlockSpec((1, D), lambda i: (0, 0)),
                   pl.BlockSpec((1, HD), lambda i: (0, 0))] + [ANY] * more,
        out_shape=[jax.ShapeDtypeStruct((t, 4 * D), BF16), jax.ShapeDtypeStruct((1, D), F32),
                   jax.ShapeDtypeStruct((1, HD), F32)]
        + ([jax.ShapeDtypeStruct((3,) + scatter.shape[1:], scatter.dtype)] if more else []),
        scratch_shapes=[pltpu.VMEM((HG_H * HD, HD), F32)] + SCATTER_SEMS * more,
        compiler_params=_cparams(("arbitrary",)),
    )(u, sts, dp, lb, go, *([scatter] * more))


GD_VH = 16
GD_QKH = 8
GD_QKV = 4096
GD_VW = 2048
GD_N = GD_QKV + GD_VW + HD
GD_GRP = 4
GD_SOLVE = (GD_VH // GD_GRP, GD_GRP * CHUNK, 2 * HD)
HALO = 8


def _mm_high(a, b):
    return _dot(a, b, 1, 0, lax.Precision.HIGH)


def _lane_pick(a, h):
    lane = lax.broadcasted_iota(jnp.int32, a.shape, 1)
    return jnp.sum(jnp.where(lane == h, a, 0.0), axis=1, keepdims=True)


def _l2n(x):
    return x * lax.rsqrt(jnp.sum(x * x, axis=1, keepdims=True) + EPS)


def _solve_fwd(a_mats, rhss):
    n = a_mats[0].shape[0]
    r_i, c_i = lax.broadcasted_iota(jnp.int32, (n, n), 0), lax.broadcasted_iota(jnp.int32, (n, n), 1)
    same = lambda nb: (r_i // nb) == (c_i // nb)
    eye = (r_i == c_i).astype(F32)
    d0s = [jnp.where(same(8), a, 0.0) for a in a_mats]
    d2s = [_mm(d, d) for d in d0s]
    tinvs = [eye - d for d in d0s]
    tinvs = [t + _mm(t, d2) for t, d2 in zip(tinvs, d2s)]
    d4s = [_mm(d2, d2) for d2 in d2s]
    tinvs = [t + _mm(t, d4) for t, d4 in zip(tinvs, d4s)]
    nb = 16
    while nb <= CHUNK:
        tls = [_mm(t, jnp.where(same(nb) & ~same(nb // 2), a, 0.0)) for t, a in zip(tinvs, a_mats)]
        tinvs = [t - _mm(tl, t) for t, tl in zip(tinvs, tls)]
        nb *= 2
    return tinvs, [_mm(t, r) for t, r in zip(tinvs, rhss)]


def _solve_bwd(res, dx):
    tinv, x = res
    drhs = _mm_tn(tinv, dx)
    return -_mm_nt(drhs, x), drhs


@jax.custom_vjp
def _solved(a_mat, rhs, tinv, x):
    return x


_solved.defvjp(lambda a_mat, rhs, tinv, x: (x, (tinv, x)),
               lambda res, dx: _solve_bwd(res, dx) + (jnp.zeros_like(res[0]), jnp.zeros_like(res[1])))


def _gd_chunk(xh, x, z, ab, st, cw, alog, dtb, go, solve):
    c = x.shape[0]
    xa = jnp.concatenate([xh, x], axis=0)
    sh = [jnp.split(_roll_rows(xa, 3 - j), [HALO], axis=0)[1] for j in range(3)]
    qkv = _silu(cw[0:1] * sh[0] + cw[1:2] * sh[1] + cw[2:3] * sh[2] + cw[3:4] * x)
    q_all, k_all, v_all = jnp.split(qkv, [1024, 2048], axis=1)
    lane = lax.broadcasted_iota(jnp.int32, (c, HD), 1)
    a_part = jnp.where(lane < GD_VH, ab, 0.0)
    g_all = -jnp.exp(alog) * jax.nn.softplus(a_part + dtb)
    d_all = _cumsum_rows(g_all)
    dl_all = jnp.sum(g_all, axis=0, keepdims=True)
    beta_all = jax.nn.sigmoid(ab)
    gc = GD_GRP * c
    r_i, c_i = lax.broadcasted_iota(jnp.int32, (gc, gc), 0), lax.broadcasted_iota(jnp.int32, (gc, gc), 1)
    same_head = (r_i // c) == (c_i // c)
    tri_g, strict_g = same_head & (c_i <= r_i), same_head & (c_i < r_i)
    qs =jnp.split(q_all, GD_QKH, axis=1)
    ks = jnp.split(k_all, GD_QKH, axis=1)
    vs = jnp.split(v_all, GD_VH, axis=1)
    zs = jnp.split(z, GD_VH, axis=1)
    sts = jnp.split(st, GD_VH, axis=0)
    qn = [_l2n(a) * (HD ** -0.5) for a in qs]
    kn = [_l2n(a) for a in ks]
    p_out, st_out, pre = [], [], []
    for g in range(GD_VH // GD_GRP):
        heads = range(GD_GRP * g, GD_GRP * (g + 1))
        stack = lambda f: jnp.concatenate([f(h) for h in heads], axis=0)
        q_, k_, v_ = stack(lambda h: qn[h // 2]), stack(lambda h: kn[h // 2]), stack(lambda h: vs[h])
        dcol = stack(lambda h: _lane_pick(d_all, h))
        bcol = stack(lambda h: _lane_pick(beta_all, GD_VH + h))
        dlast = stack(lambda h: jnp.broadcast_to(_lane_pick(dl_all, h), (c, 1)))
        drow = jnp.sum(jnp.broadcast_to(dcol, (gc, HD)).T, axis=0, keepdims=True) * (1.0 / HD)
        dec = jnp.exp(jnp.where(tri_g, dcol - drow, -jnp.inf))
        kb = k_ * bcol
        a_mat = jnp.where(strict_g, _mm_nt(kb, k_) * dec, 0.0)
        pre.append((heads, q_, k_, dcol, dlast, dec, a_mat, jnp.concatenate([v_ * bcol, kb * jnp.exp(dcol)], axis=1)))
    xsols = solve([e[6] for e in pre], [e[7] for e in pre])
    heads_of = [e[0] for e in pre]
    per_head = lambda a: jnp.split(a, GD_GRP, axis=0)
    uw = [jnp.split(x, 2, axis=1) for x in xsols]
    ws = [[_mm(wh, sts[h]) for wh, h in zip(per_head(w_), heads)] for (_, w_), heads in zip(uw, heads_of)]
    v_new = [u_ - jnp.concatenate(w, axis=0) for (u_, _), w in zip(uw, ws)]
    qk = [_mm_nt(e[1], e[2]) * e[5] for e in pre]
    qs_ = [[_mm(qh, sts[h]) for qh, h in zip(per_head(e[1] * jnp.exp(e[3])), e[0])] for e in pre]
    o_g = [_mm(a, vn) + jnp.concatenate(b, axis=0) for a, vn, b in zip(qk, v_new, qs_)]
    upd = [[_mm_tn(kh, vh) for kh, vh in zip(per_head(e[2] * jnp.exp(e[4] - e[3])), per_head(vn))]
           for e, vn in zip(pre, v_new)]
    for heads, og, up in zip(heads_of, o_g, upd):
        for h, o, u_st in zip(heads, per_head(og), up):
            st_out.append(sts[h] * jnp.exp(_lane_pick(dl_all, h)) + u_st)
            y = o * lax.rsqrt(jnp.mean(o * o, axis=1, keepdims=True) + EPS) * go
            p_out.append(y * _silu(zs[h]))
    return jnp.concatenate(p_out, axis=1), jnp.concatenate(st_out, axis=0)


def _gd_specs(n, rev):
    ci = (lambda i: n - 1 - i) if rev else (lambda i: i)
    return [pl.BlockSpec((HALO, GD_QKV), lambda i: (jnp.maximum(ci(i) * (CHUNK // HALO) - 1, 0), 0)),
            pl.BlockSpec((CHUNK, GD_N), lambda i: (ci(i), 0))]


def _gd_load(uh_ref, u_ref, first):
    xh = jnp.where(first, 0.0, uh_ref[...])
    return xh, u_ref[:, 0:GD_QKV], u_ref[:, GD_QKV:GD_QKV + GD_VW], u_ref[:, GD_QKV + GD_VW:GD_N]


def _gd_fwd(u, cw, alog, dtb, go):
    t = u.shape[0]
    n = t // CHUNK
    small = lambda r, w: pl.BlockSpec((r, w), lambda i: (0, 0))

    def body(uh_ref, u_ref, cw_ref, alog_ref, dtb_ref, go_ref, p_ref, sts_ref, tinv_ref, xsol_ref, st_ref):
        i = pl.program_id(0)

        @pl.when(i == 0)
        def _():
            st_ref[...] = jnp.zeros_like(st_ref)

        def solve(a_mats, rhss):
            tinvs, xsols = _solve_fwd(a_mats, rhss)
            for g, (tinv, xsol) in enumerate(zip(tinvs, xsols)):
                tinv_ref[0, g] = tinv
                xsol_ref[0, g] = xsol
            return xsols

        st = st_ref[...]
        sts_ref[0] = st
        p, st_next = _gd_chunk(*_gd_load(uh_ref, u_ref, i == 0), st, cw_ref[...], alog_ref[...], dtb_ref[...],
                               go_ref[...], solve)
        p_ref[...] = p.astype(BF16)
        st_ref[...] = st_next

    return pl.pallas_call(
        body, name="gd_fwd", grid=(n,),
        in_specs=_gd_specs(n, False) + [small(8, GD_QKV), small(1, HD), small(1, HD), small(1, HD)],
        out_specs=[pl.BlockSpec((CHUNK, GD_VW), lambda i: (i, 0)),
                   pl.BlockSpec((1, GD_VH * HD, HD), lambda i: (i, 0, 0)),
                   pl.BlockSpec((1,) + GD_SOLVE, lambda i: (i, 0, 0, 0)),
                   pl.BlockSpec((1,) + GD_SOLVE, lambda i: (i, 0, 0, 0))],
        out_shape=[jax.ShapeDtypeStruct((t, GD_VW), BF16), jax.ShapeDtypeStruct((n, GD_VH * HD, HD), F32),
                   jax.ShapeDtypeStruct((n,) + GD_SOLVE, F32), jax.ShapeDtypeStruct((n,) + GD_SOLVE, F32)],
        scratch_shapes=[pltpu.VMEM((GD_VH * HD, HD), F32)],
        compiler_params=_cparams(("arbitrary",)),
    )(u, u, cw, alog, dtb, go)


def _gd_bwd(u, sts, tinvs, xsols, dp, cw, alog, dtb, go):
    t = u.shape[0]
    n = t // CHUNK
    small = lambda r, w: pl.BlockSpec((r, w), lambda i: (0, 0))

    def body(uh_ref, u_ref, sts_ref, tinv_ref, xsol_ref, dp_ref, cw_ref, alog_ref, dtb_ref, go_ref,
             du_ref, dcw_ref, dalog_ref, ddtb_ref, dgo_ref, dst_ref, dhalo_ref):
        i = pl.program_id(0)

        @pl.when(i == 0)
        def _():
            for r in (dst_ref, dhalo_ref, dcw_ref, dalog_ref, ddtb_ref, dgo_ref):
                r[...] = jnp.zeros_like(r)

        solve = lambda a_mats, rhss: [_solved(a, r, tinv_ref[0, g], xsol_ref[0, g])
                                      for g, (a, r) in enumerate(zip(a_mats, rhss))]
        chunk = functools.partial(_gd_chunk, solve=solve)
        _, vjp = jax.vjp(chunk, *_gd_load(uh_ref, u_ref, i == n - 1), sts_ref[0], cw_ref[...], alog_ref[...],
                         dtb_ref[...], go_ref[...])
        dxh, dx, dz, dab, dst, dcw, dalog, ddtb, dgo = vjp((dp_ref[...].astype(F32), dst_ref[...]))
        tail = jnp.concatenate([jnp.zeros((CHUNK - HALO, GD_QKV), F32), dhalo_ref[...]], axis=0)
        du_ref[:, 0:GD_QKV] = (dx + tail).astype(BF16)
        du_ref[:, GD_QKV:GD_QKV + GD_VW] = dz.astype(BF16)
        du_ref[:, GD_QKV + GD_VW:GD_N] = dab.astype(BF16)
        dhalo_ref[...] = dxh
        dst_ref[...] = dst
        dcw_ref[...] += dcw
        dalog_ref[...] += dalog
        ddtb_ref[...] += ddtb
        dgo_ref[...] += dgo

    return pl.pallas_call(
        body, name="gd_bwd", grid=(n,),
        in_specs=_gd_specs(n, True) + [pl.BlockSpec((1, GD_VH * HD, HD), lambda i: (n - 1 - i, 0, 0)),
                                       pl.BlockSpec((1,) + GD_SOLVE, lambda i: (n - 1 - i, 0, 0, 0)),
                                       pl.BlockSpec((1,) + GD_SOLVE, lambda i: (n - 1 - i, 0, 0, 0)),
                                       pl.BlockSpec((CHUNK, GD_VW), lambda i: (n - 1 - i, 0)),
                                       small(8, GD_QKV), small(1, HD), small(1, HD), small(1, HD)],
        out_specs=[pl.BlockSpec((CHUNK, GD_N), lambda i: (n - 1 - i, 0)),
                   small(8, GD_QKV), small(1, HD), small(1, HD), small(1, HD)],
        out_shape=[jax.ShapeDtypeStruct((t, GD_N), BF16), jax.ShapeDtypeStruct((8, GD_QKV), F32)]
        + [jax.ShapeDtypeStruct((1, HD), F32)] * 3,
        scratch_shapes=[pltpu.VMEM((GD_VH * HD, HD), F32), pltpu.VMEM((HALO, GD_QKV), F32)],
        compiler_params=_cparams(("arbitrary",)),
    )(u, u, sts, tinvs, xsols, dp, cw, alog, dtb, go)


SW_B = 128
SW_G = 4
SW_N = 2560
SW_KV0 = 1024


def _blockdiag(n, blk):
    r = lax.broadcasted_iota(jnp.int32, (n, n), 0) // blk
    c = lax.broadcasted_iota(jnp.int32, (n, n), 1) // blk
    return (r == c).astype(F32)


def _sw_normrope(x, g1, g2, cos, sin):
    w = x.shape[1] // 2
    x1, x2 = jnp.split(x, 2, axis=1)
    ms = _mm_high(x1 * x1 + x2 * x2, _blockdiag(w, 32)) * (1.0 / 64.0)
    rinv = lax.rsqrt(ms + EPS)
    n1, n2 = x1 * rinv * g1, x2 * rinv * g2
    return jnp.concatenate([n1 * cos - n2 * sin, n2 * cos + n1 * sin], axis=1)


def _sw_block(q, kvp, kvc, z, csp, csc, gq, gk, sinks, has_prev):
    b = q.shape[0]
    cos_c, sin_c = jnp.split(csc, 2, axis=1)
    cos_p, sin_p = jnp.split(csp, 2, axis=1)
    tile4 = lambda a: jnp.concatenate([a] * 4, axis=1)
    qh = _sw_normrope(q, gq[0:1], gq[1:2], tile4(cos_c), tile4(sin_c))
    kp, vp = jnp.split(kvp, 2, axis=1)
    kc, vc = jnp.split(kvc, 2, axis=1)
    kh = jnp.concatenate([_sw_normrope(kp, gk[0:1], gk[1:2], cos_p, sin_p),
                          _sw_normrope(kc, gk[0:1], gk[1:2], cos_c, sin_c)], axis=0)
    vv = jnp.concatenate([vp, vc], axis=0)
    q1, q2 = jnp.split(qh, 2, axis=1)
    q1g, q2g = jnp.split(q1, SW_G, axis=1), jnp.split(q2, SW_G, axis=1)
    own = lax.broadcasted_iota(jnp.int32, (4 * b, b), 1) <= lax.broadcasted_iota(jnp.int32, (4 * b, b), 0) % b
    ri = lax.broadcasted_iota(jnp.int32, (256, 256), 0)
    ci = lax.broadcasted_iota(jnp.int32, (256, 256), 1)
    row_head = lax.broadcasted_iota(jnp.int32, (4 * b, 256), 0) // b
    lane_q = lax.broadcasted_iota(jnp.int32, (4 * b, 256), 1)
    q_sel = (lane_q % 128) // 32 == row_head
    o_sel = lane_q // 64 == row_head
    groups = range(SW_G)
    ek = [((ri // 128 == ci // 128) & ((ri % 128) // 32 == g) & (ri % 32 == ci % 32)).astype(F32) for g in groups]
    ev = [((ri // 64 == g) & (ri % 64 == ci % 64)).astype(F32) for g in groups]
    kx = [_mm(kh, ek[g]) for g in groups]
    vx = [_mm(vv, ev[g]) for g in groups]
    q4 = [jnp.where(q_sel, jnp.concatenate([jnp.concatenate([q1g[g], q2g[g]], axis=1)] * 4, axis=0), 0.0)
          for g in groups]
    sink = [jnp.concatenate([jnp.broadcast_to(_lane_pick(sinks, 4 * g + j), (b, 1)) for j in range(4)], axis=0)
            for g in groups]
    sc = [jnp.split(_mm_nt(q4[g], kx[g]) * (64 ** -0.5), 2, axis=1) for g in groups]
    s = [jnp.where(own, sc[g][1], jnp.where(has_prev, sc[g][0], -jnp.inf)) for g in groups]
    top = [jnp.max(s[g]) for g in groups]
    soft = [top[g] + 8.0 * jnp.log(jnp.sum(jnp.exp((s[g] - top[g]) * 0.125), axis=1, keepdims=True)) for g in groups]
    m = [lax.stop_gradient(jnp.maximum(soft[g], sink[g])) for g in groups]
    p = [jnp.exp(s[g] - m[g]) for g in groups]
    pn = [p[g] / (jnp.sum(p[g], axis=1, keepdims=True) + jnp.exp(sink[g] - m[g])) for g in groups]
    pn2 = [jnp.concatenate([jnp.where(own, 0.0, pn[g]), jnp.where(own, pn[g], 0.0)], axis=1) for g in groups]
    o4 = [jnp.split(jnp.where(o_sel, _mm(pn2[g], vx[g]), 0.0), 4, axis=0) for g in groups]
    o_out = [o4[g][0] + o4[g][1] + o4[g][2] + o4[g][3] for g in groups]
    return jnp.concatenate(o_out, axis=1) * _silu(z)


def _sw_specs(n, rev):
    ci = (lambda i: n - 1 - i) if rev else (lambda i: i)
    prev = lambda i: jnp.maximum(ci(i) - 1, 0)
    return [pl.BlockSpec((SW_B, SW_N), lambda i: (ci(i), 0)),
            pl.BlockSpec((SW_B, 512), lambda i: (prev(i), SW_KV0 // 512)),
            pl.BlockSpec((SW_B, 256), lambda i: (ci(i), 0)),
            pl.BlockSpec((SW_B, 256), lambda i: (prev(i), 0)),
            pl.BlockSpec((2, 512), lambda i: (0, 0)), pl.BlockSpec((2, 128), lambda i: (0, 0)),
            pl.BlockSpec((1, 128), lambda i: (0, 0))]


def _sw_args(u_ref, kvp_ref, csc_ref, csp_ref, gq_ref, gk_ref, sk_ref, has_prev):
    return (u_ref[:, 0:D], kvp_ref[...], u_ref[:, SW_KV0:SW_KV0 + 512], u_ref[:, SW_KV0 + 512:SW_N],
            csp_ref[...], csc_ref[...], gq_ref[...], gk_ref[...], sk_ref[...], has_prev)


def _sw_fwd(u, cs, gq, gk, sinks):
    t = u.shape[0]
    n = t // SW_B

    def body(u_ref, kvp_ref, csc_ref, csp_ref, gq_ref, gk_ref, sk_ref, p_ref):
        has_prev = pl.program_id(0) > 0
        p_ref[...] = _sw_block(*_sw_args(u_ref, kvp_ref, csc_ref, csp_ref, gq_ref, gk_ref, sk_ref, has_prev)
                               ).astype(BF16)

    return pl.pallas_call(
        body, name="sw_fwd", grid=(n,), in_specs=_sw_specs(n, False),
        out_specs=pl.BlockSpec((SW_B, D), lambda i: (i, 0)),
        out_shape=jax.ShapeDtypeStruct((t, D), BF16),
        compiler_params=_cparams(("arbitrary",)),
    )(u, u, cs, cs, gq, gk, sinks)


def _sw_bwd(u, cs, dp, gq, gk, sinks):
    t = u.shape[0]
    n = t // SW_B

    def body(u_ref, kvp_ref, csc_ref, csp_ref, gq_ref, gk_ref, sk_ref, dp_ref,
             du_ref, dgq_ref, dgk_ref, dsk_ref, dkv_ref):
        i = pl.program_id(0)

        @pl.when(i == 0)
        def _():
            for r in (dkv_ref, dgq_ref, dgk_ref, dsk_ref):
                r[...] = jnp.zeros_like(r)

        has_prev = i < n - 1
        args = _sw_args(u_ref, kvp_ref, csc_ref, csp_ref, gq_ref, gk_ref, sk_ref, has_prev)
        fn = lambda q, kvp, kvc, z, gq_, gk_, sk_: _sw_block(q, kvp, kvc, z, args[4], args[5], gq_, gk_, sk_, has_prev)
        _, vjp = jax.vjp(fn, args[0], args[1], args[2], args[3], args[6], args[7], args[8])
        dq, dkvp, dkvc, dz, dgq, dgk, dsk = vjp(dp_ref[...].astype(F32))
        du_ref[:, 0:D] = dq.astype(BF16)
        du_ref[:, SW_KV0:SW_KV0 + 512] = (dkvc + dkv_ref[...]).astype(BF16)
        du_ref[:, SW_KV0 + 512:SW_N] = dz.astype(BF16)
        dkv_ref[...] = dkvp
        dgq_ref[...] += dgq
        dgk_ref[...] += dgk
        dsk_ref[...] += dsk

    small = lambda r, w: pl.BlockSpec((r, w), lambda i: (0, 0))
    return pl.pallas_call(
        body, name="sw_bwd", grid=(n,),
        in_specs=_sw_specs(n, True) + [pl.BlockSpec((SW_B, D), lambda i: (n - 1 - i, 0))],
        out_specs=[pl.BlockSpec((SW_B, SW_N), lambda i: (n - 1 - i, 0)), small(2, 512), small(2, 128), small(1, 128)],
        out_shape=[jax.ShapeDtypeStruct((t, SW_N), BF16), jax.ShapeDtypeStruct((2, 512), F32),
                   jax.ShapeDtypeStruct((2, 128), F32), jax.ShapeDtypeStruct((1, 128), F32)],
        scratch_shapes=[pltpu.VMEM((SW_B, 512), F32)],
        compiler_params=_cparams(("arbitrary",)),
    )(u, u, cs, cs, gq, gk, sinks, dp)


def _ln_mod(x, g, scale, shift):
    y = x * lax.rsqrt(jnp.mean(x * x, axis=1, keepdims=True) + EPS) * g
    return y * (1.0 + scale) + shift


def _row_tile(t):
    return min(t, 1024)


def _ln_mm(x, g, scale, shift, w, tn):
    t, n = x.shape[0], w.shape[1]
    tm = _row_tile(t)
    vec = pl.BlockSpec((1, D), lambda i, j: (0, 0))

    def body(x_ref, g_ref, sc_ref, sh_ref, w_ref, u_ref, h_ref):
        @pl.when(pl.program_id(1) == 0)
        def _():
            h_ref[...] = _ln_mod(x_ref[...], g_ref[...], sc_ref[...], sh_ref[...]).astype(BF16)

        u_ref[...] = _dot(h_ref[...], w_ref[...], 1, 0)

    return pl.pallas_call(
        body, name="ln_mm", grid=(t // tm, n // tn),
        in_specs=[pl.BlockSpec((tm, D), lambda i, j: (i, 0)), vec, vec, vec,
                  pl.BlockSpec((D, tn), lambda i, j: (0, j))],
        out_specs=[pl.BlockSpec((tm, tn), lambda i, j: (i, j)), pl.BlockSpec((tm, D), lambda i, j: (i, 0))],
        out_shape=[jax.ShapeDtypeStruct((t, n), F32), jax.ShapeDtypeStruct((t, D), BF16)],
        compiler_params=_cparams(("arbitrary", "arbitrary")),
    )(x, g, scale, shift, w)


def _mm_res(p, w, x, gate):
    t, k = p.shape
    tm = _row_tile(t)

    def body(p_ref, w_ref, x_ref, gate_ref, o_ref):
        o_ref[...] = x_ref[...] + gate_ref[...] * _dot(p_ref[...], w_ref[...], 1, 0)

    return pl.pallas_call(
        body, name="mm_res", grid=(t // tm,),
        in_specs=[pl.BlockSpec((tm, k), lambda i: (i, 0)), pl.BlockSpec((k, D), lambda i: (0, 0)),
                  pl.BlockSpec((tm, D), lambda i: (i, 0)), pl.BlockSpec((1, D), lambda i: (0, 0))],
        out_specs=pl.BlockSpec((tm, D), lambda i: (i, 0)),
        out_shape=jax.ShapeDtypeStruct((t, D), F32),
        compiler_params=_cparams(("arbitrary",)),
    )(p, w, x, gate)


def _loss_grad(x, target):
    t = x.shape[0]
    tm = _row_tile(t)

    def body(x_ref, t_ref, l_ref, dx_ref):
        @pl.when(pl.program_id(0) == 0)
        def _():
            l_ref[...] = jnp.zeros_like(l_ref)

        err = x_ref[...] - t_ref[...]
        dx_ref[...] = err * (1.0 / D)
        l_ref[...] += 0.5 * jnp.sum(jnp.mean(err * err, axis=1, keepdims=True), axis=0, keepdims=True)

    return pl.pallas_call(
        body, name="loss_grad", grid=(t // tm,),
        in_specs=[pl.BlockSpec((tm, D), lambda i: (i, 0))] * 2,
        out_specs=[pl.BlockSpec((8, 128), lambda i: (0, 0)), pl.BlockSpec((tm, D), lambda i: (i, 0))],
        out_shape=[jax.ShapeDtypeStruct((8, 128), F32), jax.ShapeDtypeStruct((t, D), F32)],
        compiler_params=_cparams(("arbitrary",)),
    )(x, target)


def _mm_scaled(a, s, w, tn):
    t, k = a.shape
    n = w.shape[1]
    tm = _row_tile(t)

    def body(a_ref, s_ref, w_ref, o_ref):
        o_ref[...] = _dot((a_ref[...] * s_ref[...]).astype(BF16), w_ref[...], 1, 0).astype(BF16)

    return pl.pallas_call(
        body, name="mm_scaled", grid=(t // tm, n // tn),
        in_specs=[pl.BlockSpec((tm, k), lambda i, j: (i, 0)), pl.BlockSpec((1, k), lambda i, j: (0, 0)),
                  pl.BlockSpec((k, tn), lambda i, j: (0, j))],
        out_specs=pl.BlockSpec((tm, tn), lambda i, j: (i, j)),
        out_shape=jax.ShapeDtypeStruct((t, n), BF16),
        compiler_params=_cparams(("arbitrary", "arbitrary")),
    )(a, s, w)


def _mm_tn_acc(a, b, tn):
    t, m = a.shape
    n = b.shape[1]
    fits = lambda k: 2 * k * (m * a.dtype.itemsize + tn * b.dtype.itemsize) + 2 * m * tn * 4 <= 36 * 1024 * 1024
    tk = next(k for k in (4096, 2048, 1024, 512, t) if t % k == 0 and (fits(k) or k <= 512))
    nk = t // tk

    def body(a_ref, b_ref, o_ref):
        @pl.when(pl.program_id(1) == 0)
        def _():
            o_ref[...] = jnp.zeros_like(o_ref)

        o_ref[...] += _dot(a_ref[...], b_ref[...].astype(BF16), 0, 0)

    return pl.pallas_call(
        body, name="mm_tn_acc", grid=(n // tn, nk),
        in_specs=[pl.BlockSpec((tk, m), lambda j, k: (k, 0)), pl.BlockSpec((tk, tn), lambda j, k: (k, j))],
        out_specs=pl.BlockSpec((m, tn), lambda j, k: (0, j)),
        out_shape=jax.ShapeDtypeStruct((m, n), F32),
        compiler_params=_cparams(("arbitrary", "arbitrary")),
    )(a, b)


def _inproj_bwd(du, wt, x, dxp, g, scale, shift):
    t, kdim = du.shape
    tk = kdim
    tm = min(t, 512 if kdim <= 4096 else 256)
    nk = kdim // tk
    vec = pl.BlockSpec((1, D), lambda i, k: (0, 0))

    def body(du_ref, wt_ref, x_ref, dxp_ref, g_ref, sc_ref, sh_ref, dx_ref, dv_ref, acc_ref):
        k = pl.program_id(1)

        @pl.when((pl.program_id(0) == 0) & (k == 0))
        def _():
            dv_ref[...] = jnp.zeros_like(dv_ref)

        @pl.when(k == 0)
        def _():
            acc_ref[...] = jnp.zeros_like(acc_ref)

        acc_ref[...] += _dot(du_ref[...].astype(BF16), wt_ref[...], 1, 0)

        @pl.when(k == nk - 1)
        def _():
            _, vjp = jax.vjp(_ln_mod, x_ref[...], g_ref[...], sc_ref[...], sh_ref[...])
            dx, dg, dsc, dsh = vjp(acc_ref[...])
            dx_ref[...] = dxp_ref[...] + dx
            dv_ref[0:1, :] += dg
            dv_ref[1:2, :] += dsc
            dv_ref[2:3, :] += dsh

    return pl.pallas_call(
        body, name="inproj_bwd", grid=(t // tm, nk),
        in_specs=[pl.BlockSpec((tm, tk), lambda i, k: (i, k)), pl.BlockSpec((tk, D), lambda i, k: (k, 0)),
                  pl.BlockSpec((tm, D), lambda i, k: (i, 0)), pl.BlockSpec((tm, D), lambda i, k: (i, 0)),
                  vec, vec, vec],
        out_specs=[pl.BlockSpec((tm, D), lambda i, k: (i, 0)), pl.BlockSpec((8, D), lambda i, k: (0, 0))],
        out_shape=[jax.ShapeDtypeStruct((t, D), F32), jax.ShapeDtypeStruct((8, D), F32)],
        scratch_shapes=[pltpu.VMEM((tm, D), F32)],
        compiler_params=_cparams(("arbitrary", "arbitrary")),
    )(du, wt, x, dxp, g, scale, shift)


def _outgrad(gmat, w, gate):
    k = gmat.shape[0]
    tr = 256

    def body(g_ref, w_ref, gate_ref, dw_ref, dg_ref):
        @pl.when(pl.program_id(0) == 0)
        def _():
            dg_ref[...] = jnp.zeros_like(dg_ref)

        gm = g_ref[...]
        dw_ref[...] = gm * gate_ref[...]
        dg_ref[0:1, :] += jnp.sum(gm * w_ref[...].astype(F32), axis=0, keepdims=True)

    return pl.pallas_call(
        body, name="outgrad", grid=(k // tr,),
        in_specs=[pl.BlockSpec((tr, D), lambda i: (i, 0)), pl.BlockSpec((tr, D), lambda i: (i, 0)),
                  pl.BlockSpec((1, D), lambda i: (0, 0))],
        out_specs=[pl.BlockSpec((tr, D), lambda i: (i, 0)), pl.BlockSpec((8, D), lambda i: (0, 0))],
        out_shape=[jax.ShapeDtypeStruct((k, D), F32), jax.ShapeDtypeStruct((8, D), F32)],
        compiler_params=_cparams(("arbitrary",)),
    )(gmat, w, gate)


def _rope_table(pos, freq):
    t = pos.shape[0]
    tm = _row_tile(t)

    def body(p_ref, f_ref, o_ref):
        ang = p_ref[...].astype(F32) * f_ref[...]
        o_ref[:, 0:128] = jnp.cos(ang)
        o_ref[:, 128:256] = jnp.sin(ang)

    return pl.pallas_call(
        body, name="rope_table", grid=(t // tm,),
        in_specs=[pl.BlockSpec((tm, 1), lambda i: (i, 0)), pl.BlockSpec((1, 128), lambda i: (0, 0))],
        out_specs=pl.BlockSpec((tm, 256), lambda i: (i, 0)),
        out_shape=jax.ShapeDtypeStruct((t, 256), F32),
        compiler_params=_cparams(("arbitrary",)),
    )(pos, freq)


def _ada_fwd(c_all, w, b):
    nl, _, s = w.shape

    def body(c_ref, w_ref, b_ref, o_ref):
        o_ref[0] = _mm_f32(c_ref[...], w_ref[0]) + b_ref[0]

    return pl.pallas_call(
        body, name="ada_fwd", grid=(nl,),
        in_specs=[pl.BlockSpec((8, D), lambda l: (0, 0)), pl.BlockSpec((1, D, s), lambda l: (l, 0, 0)),
                  pl.BlockSpec((1, 1, s), lambda l: (l, 0, 0))],
        out_specs=pl.BlockSpec((1, 8, s), lambda l: (l, 0, 0)),
        out_shape=jax.ShapeDtypeStruct((nl, 8, s), F32),
        compiler_params=_cparams(("arbitrary",)),
    )(c_all, w, b)


def _ada_bwd(c_all, dmod_cols, dmod_all):
    nl, _, s = dmod_cols.shape

    def body(c_ref, dc_ref, da_ref, gw_ref, gb_ref):
        gw_ref[0] = _dot(c_ref[...], dc_ref[0], 0, 0, lax.Precision.HIGHEST)
        gb_ref[0] = jnp.sum(da_ref[0], axis=0, keepdims=True)

    return pl.pallas_call(
        body, name="ada_bwd", grid=(nl,),
        in_specs=[pl.BlockSpec((8, D), lambda l: (0, 0)), pl.BlockSpec((1, 8, s), lambda l: (l, 0, 0)),
                  pl.BlockSpec((1, 8, 3 * D), lambda l: (l, 0, 0))],
        out_specs=[pl.BlockSpec((1, D, s), lambda l: (l, 0, 0)), pl.BlockSpec((1, 1, 3 * D), lambda l: (l, 0, 0))],
        out_shape=[jax.ShapeDtypeStruct((nl, D, s), F32), jax.ShapeDtypeStruct((nl, 1, 3 * D), F32)],
        compiler_params=_cparams(("arbitrary",)),
    )(c_all, dmod_cols, dmod_all)


def _lb_fn(h8):
    sm = jax.nn.softmax(h8, axis=0)
    r = lax.broadcasted_iota(jnp.int32, (8, 8), 0)
    c = lax.broadcasted_iota(jnp.int32, (8, 8), 1)
    return _mm_f32(((c >= 1) & (c <= r)).astype(F32), sm)


def _lb_fwd(h8):
    def body(h_ref, o_ref):
        o_ref[...] = _lb_fn(h_ref[...])

    return pl.pallas_call(body, name="lb_fwd", out_shape=jax.ShapeDtypeStruct((8, D), F32))(h8)


def _lb_bwd(h8, dlb8):
    def body(h_ref, d_ref, o_ref):
        _, vjp = jax.vjp(_lb_fn, h_ref[...])
        o_ref[...] = vjp(d_ref[...])[0]

    return pl.pallas_call(body, name="lb_bwd", out_shape=jax.ShapeDtypeStruct((8, D), F32))(h8, dlb8)


ADAM_LR, ADAM_B1, ADAM_B2, ADAM_EPS, ADAM_WD, ADAM_STEP = 0.001, 0.9, 0.999, 1e-08, 0.01, 10


def _adamw(w, gparts, m, v):
    r, c = w.shape
    tr = r if r * c * 4 <= (1 << 20) else max(8, ((1 << 20) // (c * 4)) // 8 * 8)
    while r % tr:
        tr -= 8
    ng = len(gparts)

    def body(*refs):
        w_ref, m_ref, v_ref = refs[0], refs[1 + ng], refs[2 + ng]
        g_ref, d_ref, nm_ref, nv_ref = refs[3 + ng:]
        g = refs[1][...]
        for gr in refs[2:1 + ng]:
            g = g + gr[...]
        mm = ADAM_B1 * m_ref[...] + (1.0 - ADAM_B1) * g
        vv = ADAM_B2 * v_ref[...] + (1.0 - ADAM_B2) * (g * g)
        m_hat = mm / (1.0 - ADAM_B1 ** ADAM_STEP)
        v_hat = vv / (1.0 - ADAM_B2 ** ADAM_STEP)
        g_ref[...] = g
        d_ref[...] = -ADAM_LR * (m_hat / (jnp.sqrt(v_hat) + ADAM_EPS) + ADAM_WD * w_ref[...])
        nm_ref[...] = mm
        nv_ref[...] = vv

    spec = pl.BlockSpec((tr, c), lambda i: (i, 0))
    return pl.pallas_call(
        body, name="adamw", grid=(r // tr,), in_specs=[spec] * (3 + ng), out_specs=[spec] * 4,
        out_shape=[jax.ShapeDtypeStruct((r, c), F32)] * 4,
        compiler_params=_cparams(("arbitrary",)),
    )(w, *gparts, m, v)


def _sum_rows(parts):
    r, c = parts[0].shape
    tr = next(k for k in range(min(r, 512) // 16 * 16, 0, -16) if r % k == 0)

    def body(*refs):
        acc = refs[0][...]
        for p in refs[1:-1]:
            acc = acc + p[...]
        refs[-1][...] = acc

    spec = pl.BlockSpec((tr, c), lambda i: (i, 0))
    return pl.pallas_call(
        body, name="sum_rows", grid=(r // tr,), in_specs=[spec] * len(parts), out_specs=spec,
        out_shape=jax.ShapeDtypeStruct((r, c), F32),
        compiler_params=_cparams(("arbitrary",)),
    )(*parts)


MESH = pl.DeviceIdType.MESH
ANY = pl.BlockSpec(memory_space=pl.ANY)


def _place():
    return lax.axis_index("x"), lax.axis_index("y"), lax.axis_index("c")


def _allgather8(blk):
    m_per, n = blk.shape

    def body(x_ref, out_ref, send_sems, recv_sems, local_sem):
        x, y, c = _place()
        me, sibling = (x, y, c), (x, y, 1 - c)
        chips = [(1 - x, y), (x, 1 - y), (1 - x, 1 - y)]

        def rows(px, py, pc):
            return out_ref.at[pl.ds((4 * px + 2 * py + pc) * m_per, m_per), :]

        def copy(k, block, to, src=None):
            return pltpu.make_async_remote_copy(
                src_ref=rows(*block) if src is None else src, dst_ref=rows(*block),
                send_sem=send_sems.at[k], recv_sem=recv_sems.at[k], device_id=to, device_id_type=MESH)

        mine = pltpu.make_async_copy(x_ref, rows(*me), local_sem)
        mine.start()
        first = [copy(0, me, sibling, src=x_ref)]
        first += [copy(1 + j, me, (*chip, c), src=x_ref) for j, chip in enumerate(chips)]
        for cp in first:
            cp.start()
        passed = [copy(4 + j, (*chip, c), sibling) for j, chip in enumerate(chips)]
        for j, chip in enumerate(chips):
            copy(1 + j, (*chip, c), me).wait_recv()
            passed[j].start()
        copy(0, sibling, me).wait_recv()
        for j, chip in enumerate(chips):
            copy(4 + j, (*chip, 1 - c), me).wait_recv()
        for cp in first + passed:
            cp.wait_send()
        mine.wait()

    return pl.pallas_call(
        body, name="allgather8",
        out_shape=jax.ShapeDtypeStruct((8 * m_per, n), blk.dtype),
        in_specs=[pl.BlockSpec(memory_space=pltpu.VMEM)],
        out_specs=pl.BlockSpec(memory_space=pltpu.VMEM),
        scratch_shapes=[pltpu.SemaphoreType.DMA((7,)), pltpu.SemaphoreType.DMA((7,)), pltpu.SemaphoreType.DMA],
    )(blk)


def _chip_peers():
    x, y, c = _place()
    return [(1 - x, y, c), (x, 1 - y, c), (1 - x, 1 - y, c)]


GATHER_SEMS = [pltpu.SemaphoreType.DMA((3,)), pltpu.SemaphoreType.DMA((3,)), pltpu.SemaphoreType.DMA]
SCATTER_SEMS = [pltpu.SemaphoreType.DMA((3,)), pltpu.SemaphoreType.DMA((3,))]


def _gather_plan(x_ref, out_ref, send_sems, recv_sems, local_sem):
    x, y, _ = _place()
    peers = _chip_peers()

    def copy(j, chip_index):
        return pltpu.make_async_remote_copy(
            src_ref=x_ref, dst_ref=out_ref.at[chip_index], send_sem=send_sems.at[j], recv_sem=recv_sems.at[j],
            device_id=peers[j], device_id_type=MESH)

    mine = pltpu.make_async_copy(x_ref, out_ref.at[2 * x + y], local_sem)
    sends = [copy(j, 2 * x + y) for j in range(3)]

    def start():
        mine.start()
        for cp in sends:
            cp.start()

    def wait():
        for j in range(3):
            copy(j, 2 * peers[j][0] + peers[j][1]).wait_recv()
        for cp in sends:
            cp.wait_send()
        mine.wait()

    return start, wait


def _scatter_plan(p_ref, out_ref, send_sems, recv_sems):
    peers = _chip_peers()
    sends = [pltpu.make_async_remote_copy(
        src_ref=p_ref.at[2 * peers[j][0] + peers[j][1]], dst_ref=out_ref.at[j], send_sem=send_sems.at[j],
        recv_sem=recv_sems.at[j], device_id=peers[j], device_id_type=MESH) for j in range(3)]

    def start():
        for cp in sends:
            cp.start()

    def wait():
        for cp in sends:
            cp.wait_recv()
        for cp in sends:
            cp.wait_send()

    return start, wait


def _chip_allgather(shard):
    def body(x_ref, out_ref, *sems):
        start, wait = _gather_plan(x_ref, out_ref, *sems)
        start()
        wait()

    return pl.pallas_call(
        body, name="chip_allgather", out_shape=jax.ShapeDtypeStruct((4,) + shard.shape, shard.dtype),
        in_specs=[ANY], out_specs=ANY, scratch_shapes=GATHER_SEMS,
    )(shard)


def _chip_scatter(parts):
    def body(p_ref, out_ref, *sems):
        start, wait = _scatter_plan(p_ref, out_ref, *sems)
        start()
        wait()

    return pl.pallas_call(
        body, name="chip_scatter", out_shape=jax.ShapeDtypeStruct((3,) + parts.shape[1:], parts.dtype),
        in_specs=[ANY], out_specs=ANY, scratch_shapes=SCATTER_SEMS,
    )(parts)


def _sibling_swap(a):
    def body(a_ref, out_ref, send_sem, recv_sem):
        x, y, c = _place()
        cp = pltpu.make_async_remote_copy(src_ref=a_ref, dst_ref=out_ref, send_sem=send_sem, recv_sem=recv_sem,
                                          device_id=(x, y, 1 - c), device_id_type=MESH)
        cp.start()
        cp.wait_recv()
        cp.wait_send()

    return pl.pallas_call(
        body, name="sibling_swap", out_shape=jax.ShapeDtypeStruct(a.shape, a.dtype),
        in_specs=[ANY], out_specs=ANY,
        scratch_shapes=[pltpu.SemaphoreType.DMA, pltpu.SemaphoreType.DMA],
    )(a)


WEIGHTS = ['hgrn_lb', 'ada_w', 'ada_b', 'norm_g', 'hg_in_w', 'hg_out_w', 'hg_onorm', 'sw_in_w', 'sw_out_w', 'sw_qnorm',
           'sw_knorm', 'sw_sinks', 'gd_in_w', 'gd_out_w', 'gd_conv_w', 'gd_a_log', 'gd_dt_bias', 'gd_onorm']
BIG = ['hg_in_w', 'hg_out_w', 'sw_in_w', 'sw_out_w', 'gd_in_w', 'gd_out_w']
SEG_FIRST = [('hg_in_w', 0), ('hg_out_w', 0)]
SEG_REST = [('hg_in_w', 1), ('hg_out_w', 1), ('sw_in_w', 0), ('sw_out_w', 0), ('gd_in_w', 0), ('gd_out_w', 0)]
PACK_ALIGN = 16
ROPE_THETA = 10000.0
ADA_S = 3 * D // 4
SMALL_ROW = {'hg_onorm': (0, 256), 'sw_qnorm': (256, 64), 'sw_knorm': (320, 64), 'sw_sinks': (384, 16),
             'gd_a_log': (400, 16), 'gd_dt_bias': (416, 16), 'gd_onorm': (432, 128)}


def _pack_rows(arrs):
    flat = jnp.concatenate([a.reshape(-1, D) for a in arrs], axis=0)
    return jnp.pad(flat, ((0, -flat.shape[0] % PACK_ALIGN), (0, 0)))


def _unpack_rows(packed, shapes):
    out, off = [], 0
    for s in shapes:
        rows = 1
        for d in s:
            rows *= d
        rows //= D
        out.append(packed[..., off:off + rows, :].reshape(packed.shape[:-2] + tuple(s)))
        off += rows
    return out


def _pack_small(vals):
    row = jnp.concatenate([vals[k].reshape(-1) for k in SMALL_ROW])
    row = jnp.pad(row, (0, D - row.shape[0]))[None]
    return jnp.concatenate([vals['hgrn_lb'], vals['norm_g'], vals['gd_conv_w'].reshape(16, D), row,
                            jnp.zeros((7, D), F32)], axis=0)


def _sw_cols(w, inverse=False):
    def split(a, heads):
        shp = (a.shape[0], 2, heads, 32) if inverse else (a.shape[0], heads, 2, 32)
        return a.reshape(shp).transpose(0, 2, 1, 3).reshape(a.shape[0], heads * 64)
    return jnp.concatenate([split(w[:, 0:1024], 16), split(w[:, 1024:1280], 4), w[:, 1280:]], axis=1)


def kernel(x, c, positions, hgrn_lb, ada_w, ada_b, norm_g, hg_in_w, hg_out_w, hg_onorm, sw_in_w, sw_out_w, sw_qnorm, sw_knorm, sw_sinks, gd_in_w, gd_out_w, gd_conv_w, gd_a_log, gd_dt_bias, gd_onorm, loss_target, m_hgrn_lb, m_ada_w, m_ada_b, m_norm_g, m_hg_in_w, m_hg_out_w, m_hg_onorm, m_sw_in_w, m_sw_out_w, m_sw_qnorm, m_sw_knorm, m_sw_sinks, m_gd_in_w, m_gd_out_w, m_gd_conv_w, m_gd_a_log, m_gd_dt_bias, m_gd_onorm, v_hgrn_lb, v_ada_w, v_ada_b, v_norm_g, v_hg_in_w, v_hg_out_w, v_hg_onorm, v_sw_in_w, v_sw_out_w, v_sw_qnorm, v_sw_knorm, v_sw_sinks, v_gd_in_w, v_gd_out_w, v_gd_conv_w, v_gd_a_log, v_gd_dt_bias, v_gd_onorm):
    w_in = dict(hgrn_lb=hgrn_lb, ada_w=ada_w, ada_b=ada_b, norm_g=norm_g, hg_in_w=hg_in_w, hg_out_w=hg_out_w,
                hg_onorm=hg_onorm, sw_in_w=sw_in_w, sw_out_w=sw_out_w, sw_qnorm=sw_qnorm, sw_knorm=sw_knorm,
                sw_sinks=sw_sinks, gd_in_w=gd_in_w, gd_out_w=gd_out_w, gd_conv_w=gd_conv_w, gd_a_log=gd_a_log,
                gd_dt_bias=gd_dt_bias, gd_onorm=gd_onorm)
    m_in = dict(zip(WEIGHTS, (m_hgrn_lb, m_ada_w, m_ada_b, m_norm_g, m_hg_in_w, m_hg_out_w, m_hg_onorm, m_sw_in_w,
                              m_sw_out_w, m_sw_qnorm, m_sw_knorm, m_sw_sinks, m_gd_in_w, m_gd_out_w, m_gd_conv_w,
                              m_gd_a_log, m_gd_dt_bias, m_gd_onorm)))
    v_in = dict(zip(WEIGHTS, (v_hgrn_lb, v_ada_w, v_ada_b, v_norm_g, v_hg_in_w, v_hg_out_w, v_hg_onorm, v_sw_in_w,
                              v_sw_out_w, v_sw_qnorm, v_sw_knorm, v_sw_sinks, v_gd_in_w, v_gd_out_w, v_gd_conv_w,
                              v_gd_a_log, v_gd_dt_bias, v_gd_onorm)))
    ax, ay, ac = _place()
    chip = 2 * ax + ay
    bidx = 4 * ax + 2 * ay + ac
    t = x.shape[1]
    x0, target = x[0], loss_target[0]

    c_all = _allgather8(jnp.pad(c, ((0, 7), (0, 0)))).reshape(8, 8, D)[:, 0, :]
    ada_b_cols = lax.dynamic_slice(ada_b, (0, chip * ADA_S), (4, ADA_S)).reshape(4, 1, ADA_S)
    mod_sh = _ada_fwd(c_all, ada_w, ada_b_cols)
    mod_g = _allgather8(mod_sh.reshape(32, ADA_S)).reshape(4, 2, 4, 8, ADA_S)[:, 0]
    mod = lax.dynamic_index_in_dim(mod_g, bidx, axis=2, keepdims=False).transpose(1, 0, 2).reshape(4, 3 * D)
    shift = [mod[l:l + 1, 0:D] for l in range(4)]
    scale = [mod[l:l + 1, D:2 * D] for l in range(4)]
    gate = [mod[l:l + 1, 2 * D:3 * D] for l in range(4)]

    h8 = jnp.concatenate([hgrn_lb, jnp.full((4, D), -1e30, F32)], axis=0)
    lb_all = _lb_fwd(h8)
    freq = ROPE_THETA ** (-jnp.arange(0, 64, 2, dtype=F32) / 64)
    cs = _rope_table(positions.reshape(t, 1), jnp.tile(freq, 4)[None])

    seg_shapes = lambda seg: [w_in[k].shape[1:] for k, _ in seg]
    pack_seg = lambda src, seg: _pack_rows([src[k][i] for k, i in seg])
    cols_full = lambda a: a.transpose(1, 0, 2).reshape(a.shape[1], 4 * a.shape[2])
    hg_in0_k, hg_out0_k = _unpack_rows(_chip_allgather(pack_seg(w_in, SEG_FIRST).astype(BF16)), seg_shapes(SEG_FIRST))
    win, wout = [cols_full(hg_in0_k)], [hg_out0_k.reshape(D, D)]
    rest_shard = pack_seg(w_in, SEG_REST).astype(BF16)
    tn_in = [1024, 1280, 896, 1024]

    gq = jnp.stack([jnp.tile(sw_qnorm[0, :32], 16), jnp.tile(sw_qnorm[0, 32:], 16)])
    gk = jnp.stack([jnp.tile(sw_knorm[0, :32], 4), jnp.tile(sw_knorm[0, 32:], 4)])
    pad128 = lambda a: jnp.pad(a, ((0, 0), (0, HD - a.shape[1])))
    sinks, alog, dtb = pad128(sw_sinks), pad128(gd_a_log), pad128(gd_dt_bias)
    cw8 = jnp.pad(_chip_allgather(gd_conv_w[0]).transpose(1, 0, 2).reshape(4, GD_QKV), ((0, 4), (0, 0)))
    lbs = {0: lb_all[0:1], 3: lb_all[3:4]}

    xs, us, hs, ps, stss = [x0], [], [], [], []
    for l in range(4):
        u, h = _ln_mm(xs[l], norm_g[l:l + 1], scale[l], shift[l], win[l], tn_in[l])
        if l == 0:
            p, sts, rest_k = _hg_fwd(u, lbs[l], hg_onorm[0:1], gather=rest_shard)
            hg_in1_k, hg_out1_k, sw_in_k, sw_out_k, gd_in_k, gd_out_k = _unpack_rows(rest_k, seg_shapes(SEG_REST))
            win += [_sw_cols(cols_full(sw_in_k)), jnp.pad(cols_full(gd_in_k), ((0, 0), (0, GD_N - 6176))),
                    cols_full(hg_in1_k)]
            wout += [sw_out_k.reshape(D, D), gd_out_k.reshape(GD_VW, D), hg_out1_k.reshape(D, D)]
        elif l % 3 == 0:
            p, sts = _hg_fwd(u, lbs[l], hg_onorm[l // 3:l // 3 + 1])
        elif l % 3 == 1:
            p, sts = _sw_fwd(u, cs, gq, gk, sinks), None
        else:
            p, *sts = _gd_fwd(u, cw8, alog, dtb, gd_onorm)
        xs.append(_mm_res(p, wout[l], xs[l], gate[l]))
        us.append(u), hs.append(h), ps.append(p), stss.append(sts)
    lpart, dx = _loss_grad(xs[4], target)
    loss = lax.psum(lpart[0, 0], ("x", "y", "c"))

    by_chip = lambda g, cols: g.reshape(g.shape[0], 4, cols).transpose(1, 0, 2)
    g_small = {}
    d_in, d_out, dmod, dnorm_g, dlb8, dgo_hg = [None] * 4, [None] * 4, [None] * 4, [None] * 4, jnp.zeros((8, D), F32), {}
    for l in (3, 2, 1, 0):
        dp = _mm_scaled(dx, gate[l], wout[l].T, 1024)
        d_out[l], dgate = _outgrad(_mm_tn_acc(ps[l], dx, D if ps[l].shape[1] == D else 512), wout[l], gate[l])
        if l == 0:
            rest_parts = {('hg_in_w', 1): by_chip(d_in[3], D), ('hg_out_w', 1): d_out[3].reshape(4, D // 4, D),
                          ('sw_in_w', 0): by_chip(_sw_cols(d_in[1], inverse=True), SW_N // 4),
                          ('sw_out_w', 0): d_out[1].reshape(4, D // 4, D),
                          ('gd_in_w', 0): by_chip(d_in[2][:, :6176], 1544),
                          ('gd_out_w', 0): d_out[2].reshape(4, GD_VW // 4, D)}
            rest_packed = jnp.stack([_pack_rows([rest_parts[s][j] for s in SEG_REST]) for j in range(4)])
            du, dlb, dgo_hg[0], rest_recv = _hg_bwd(us[l], stss[l], dp, lbs[l], hg_onorm[0:1],
                                                    scatter=rest_packed.astype(BF16))
            dlb8 = lax.dynamic_update_slice(dlb8, dlb, (l, 0))
        elif l % 3 == 0:
            du, dlb, dgo_hg[l // 3] = _hg_bwd(us[l], stss[l], dp, lbs[l], hg_onorm[l // 3:l // 3 + 1])
            dlb8 = lax.dynamic_update_slice(dlb8, dlb, (l, 0))
        elif l % 3 == 1:
            du, dgq, dgk, dsk = _sw_bwd(us[l], cs, dp, gq, gk, sinks)
            g_small['sw_qnorm'] = jnp.concatenate([dgq[0].reshape(16, 32).sum(0), dgq[1].reshape(16, 32).sum(0)])
            g_small['sw_knorm'] = jnp.concatenate([dgk[0].reshape(4, 32).sum(0), dgk[1].reshape(4, 32).sum(0)])
            g_small['sw_sinks'] = dsk[0, :16]
        else:
            du, dcw, dalog, ddtb, g_small['gd_onorm'] = _gd_bwd(us[l], *stss[l], dp, cw8, alog, dtb, gd_onorm)
            g_small['gd_conv_w'], g_small['gd_a_log'], g_small['gd_dt_bias'] = dcw[:4], dalog[0, :16], ddtb[0, :16]
        d_in[l] = _mm_tn_acc(hs[l], du, 896 if l == 2 else 512)
        dx, dvec = _inproj_bwd(du, win[l].T, xs[l], dx, norm_g[l:l + 1], scale[l], shift[l])
        dnorm_g[l] = dvec[0:1]
        dmod[l] = jnp.concatenate([dvec[2:3], dvec[1:2], dgate[0:1]], axis=1)
    grad_x = dx[None]

    g_small['hgrn_lb'] = _lb_bwd(h8, dlb8)[0:4]
    g_small['norm_g'] = jnp.concatenate(dnorm_g, axis=0)
    g_small['hg_onorm'] = jnp.concatenate([dgo_hg[0], dgo_hg[1]], axis=0)
    gs_all = _allgather8(_pack_small(g_small))
    gs = _sum_rows([gs_all[32 * d:32 * (d + 1)] for d in range(8)])

    def small_view(packed, k):
        if k == 'hgrn_lb':
            return packed[0:4]
        if k == 'norm_g':
            return packed[4:8]
        off, size = SMALL_ROW[k]
        return packed[24, off:off + size].reshape(w_in[k].shape)

    conv_sl = lambda full: lax.dynamic_slice(full.reshape(4, GD_QKV), (0, chip * D), (4, D))
    out = {}

    def put(k, res, shape):
        for name, r in zip(('grad_', 'delta_', 'new_m_', 'new_v_'), res):
            out[name + k] = r.reshape(shape)

    zero_conv = dict(gd_conv_w=jnp.zeros((4, GD_QKV), F32))
    small_names = ['hgrn_lb', 'norm_g'] + list(SMALL_ROW)
    res = _adamw(_pack_small({**{k: w_in[k] for k in small_names}, **zero_conv}), (gs,),
                 _pack_small({**{k: m_in[k] for k in small_names}, **zero_conv}),
                 _pack_small({**{k: v_in[k] for k in small_names}, **zero_conv}))
    for k in small_names:
        put(k, [small_view(r, k) for r in res], w_in[k].shape)
    put('gd_conv_w', _adamw(gd_conv_w[0], (conv_sl(gs[8:24]),), m_in['gd_conv_w'][0], v_in['gd_conv_w'][0]),
        gd_conv_w.shape)

    dm = _allgather8(jnp.pad(jnp.concatenate(dmod, axis=0), ((0, 4), (0, 0)))).reshape(8, 8, 3 * D)[:, :4]
    dm = dm.transpose(1, 0, 2)
    g_ada_w, g_ada_b = _ada_bwd(c_all, lax.dynamic_slice(dm, (0, 0, chip * ADA_S), (4, 8, ADA_S)), dm)
    put('ada_w', _adamw(ada_w.reshape(4 * D, ADA_S), (g_ada_w.reshape(4 * D, ADA_S),),
                        m_in['ada_w'].reshape(4 * D, ADA_S), v_in['ada_w'].reshape(4 * D, ADA_S)), ada_w.shape)
    put('ada_b', _adamw(ada_b, (g_ada_b.reshape(4, 3 * D),), m_in['ada_b'], v_in['ada_b']), ada_b.shape)

    first_parts = {('hg_in_w', 0): by_chip(d_in[0], D), ('hg_out_w', 0): d_out[0].reshape(4, D // 4, D)}
    first_packed = jnp.stack([_pack_rows([first_parts[s][j] for s in SEG_FIRST]) for j in range(4)])
    first_recv = _chip_scatter(first_packed.astype(BF16))
    own = lambda packed: lax.dynamic_index_in_dim(packed, chip, axis=0, keepdims=False)
    prefixes = ('grad_', 'delta_', 'new_m_', 'new_v_')
    pieces = {name: {} for name in prefixes}
    for seg, packed, recv in ((SEG_FIRST, first_packed, first_recv), (SEG_REST, rest_packed, rest_recv)):
        half = _sum_rows([own(packed), recv[0], recv[1], recv[2]])
        res = _adamw(pack_seg(w_in, seg), (half, _sibling_swap(half)), pack_seg(m_in, seg), pack_seg(v_in, seg))
        for name, r in zip(prefixes, res):
            pieces[name].update(zip(seg, _unpack_rows(r, seg_shapes(seg))))
    for name in prefixes:
        for k in BIG:
            out[name + k] = jnp.stack([pieces[name][(k, i)] for i in range(w_in[k].shape[0])])

    return (loss, grad_x, *[out[p + k] for p in ('grad_', 'delta_', 'new_m_', 'new_v_') for k in WEIGHTS])
```

```python
import functools

import jax
import jax.numpy as jnp
from jax import lax
from jax.experimental import pallas as pl
from jax.experimental.pallas import tpu as pltpu

F32 = jnp.float32
BF16 = jnp.bfloat16
D = 1024
EPS = 1e-6
CHUNK = 64
SUB = 32
HG_H = 8
HG_CPS = 4
HD = 128
VMEM_LIMIT = 56 * 1024 * 1024


def _cparams(sem=None):
    return pltpu.CompilerParams(dimension_semantics=sem, vmem_limit_bytes=VMEM_LIMIT)


def _dot(a, b, ca, cb, prec=None):
    return lax.dot_general(a, b, (((ca,), (cb,)), ((), ())), precision=prec, preferred_element_type=F32)


def _mm(a, b):
    return _dot(a.astype(BF16), b.astype(BF16), 1, 0)


def _mm_nt(a, b):
    return _dot(a.astype(BF16), b.astype(BF16), 1, 1)


def _mm_tn(a, b):
    return _dot(a.astype(BF16), b.astype(BF16), 0, 0)


def _mm_f32(a, b):
    return _dot(a, b, 1, 0, lax.Precision.HIGHEST)


def _silu(x):
    return x * jax.nn.sigmoid(x)


def _cumsum_impl(x):
    row = lax.broadcasted_iota(jnp.int32, x.shape, 0)
    s = 1
    while s < x.shape[0]:
        x = x + jnp.where(row >= s, pltpu.roll(x, s, 0), 0.0)
        s *= 2
    return x


@jax.custom_vjp
def _cumsum_rows(x):
    return _cumsum_impl(x)


_cumsum_rows.defvjp(lambda x: (_cumsum_impl(x), None),
                    lambda _, g: (jnp.sum(g, axis=0, keepdims=True) - _cumsum_impl(g) + g,))


def _roll_rows(x, shift):
    n = x.shape[0]

    @jax.custom_vjp
    def f(a):
        return pltpu.roll(a, shift, 0)

    f.defvjp(lambda a: (pltpu.roll(a, shift, 0), None), lambda _, g: (pltpu.roll(g, n - shift, 0),))
    return f(x)


def _hg_chunk(q_raw, f_pre, v, z, st, lb, go):
    c = q_raw.shape[0]
    nsub = c // SUB
    lf = jnp.log(lb + (1.0 - lb) * jax.nn.sigmoid(f_pre))
    k = (1.0 - lb) * jax.nn.sigmoid(-f_pre)
    q = _silu(q_raw)
    b = _cumsum_rows(lf)
    rowf = lax.broadcasted_iota(jnp.int32, lf.shape, 0)
    bmid = [jnp.sum(jnp.where(rowf == SUB * i + SUB // 2, b, 0.0), axis=0, keepdims=True) for i in range(nsub)]
    row = lax.broadcasted_iota(jnp.int32, (c, 1), 0)
    ref = sum(jnp.where((row >= SUB * i) & (row < SUB * (i + 1)), bmid[i], 0.0) for i in range(nsub))
    qt = q * jnp.exp(b - ref)
    kall = jnp.concatenate(
        [k * jnp.exp(jnp.where(row < SUB * (i + 1), bmid[i] - b, -jnp.inf)) for i in range(nsub)], axis=0)
    v4 = jnp.concatenate([v] * nsub, axis=0)
    b_last = jnp.sum(lf, axis=0, keepdims=True)
    qb = q * jnp.exp(b)
    kd = k * jnp.exp(b_last - b)
    e_last = jnp.exp(b_last)
    tq = lax.broadcasted_iota(jnp.int32, (c, nsub * c), 0)
    cq = lax.broadcasted_iota(jnp.int32, (c, nsub * c), 1)
    m_all = ((cq // c) == (tq // SUB)) & ((cq % c) <= tq)
    hs = lambda a: jnp.split(a, HG_H, axis=1)
    qt_h, kall_h, v4_h, qb_h, kd_h, v_h, z_h, el_h = map(hs, (qt, kall, v4, qb, kd, v, z, e_last))
    st_h = jnp.split(st, HG_H, axis=0)
    heads = range(HG_H)
    pm = [jnp.where(m_all, _mm_nt(qt_h[h], kall_h[h]), 0.0) for h in heads]
    inter = [_mm_nt(qb_h[h], st_h[h]) for h in heads]
    o_h = [_mm(pm[h], v4_h[h]) + inter[h] for h in heads]
    upd = [_mm_tn(v_h[h], kd_h[h]) for h in heads]
    st_out = [el_h[h] * st_h[h] + upd[h] for h in heads]
    y_h = [o_h[h] * lax.rsqrt(jnp.mean(o_h[h] * o_h[h], axis=1, keepdims=True) + EPS) * go for h in heads]
    p_out = [y_h[h] * _silu(z_h[h]) for h in heads]
    return jnp.concatenate(p_out, axis=1), jnp.concatenate(st_out, axis=0)


def _hg_fwd(u, lb, go, gather=None):
    t = u.shape[0]
    cps = HG_CPS if t % (HG_CPS * CHUNK) == 0 else 1
    n = t // (cps * CHUNK)

    def body(u_ref, lb_ref, go_ref, *rest):
        if gather is None:
            p_ref, sts_ref, st_ref = rest
        else:
            shard_ref, p_ref, sts_ref, all_ref, st_ref, *sems = rest
            start, wait = _gather_plan(shard_ref, all_ref, *sems)
            pl.when(pl.program_id(0) == 0)(start)

        @pl.when(pl.program_id(0) == 0)
        def _():
            st_ref[...] = jnp.zeros_like(st_ref)

        st = st_ref[...]
        for j in range(cps):
            rows = slice(j * CHUNK, (j + 1) * CHUNK)
            sts_ref[j] = st
            p, st = _hg_chunk(u_ref[rows, 0:D], u_ref[rows, D:2 * D], u_ref[rows, 2 * D:3 * D], u_ref[rows, 3 * D:4 * D],
                              st, lb_ref[...], go_ref[...])
            p_ref[rows, :] = p.astype(BF16)
        st_ref[...] = st
        if gather is not None:
            pl.when(pl.program_id(0) == n - 1)(wait)

    more = gather is not None
    return pl.pallas_call(
        body, name="hg_fwd_gather" if more else "hg_fwd", grid=(n,),
        in_specs=[pl.BlockSpec((cps * CHUNK, 4 * D), lambda i: (i, 0)),
                  pl.BlockSpec((1, D), lambda i: (0, 0)),
                  pl.BlockSpec((1, HD), lambda i: (0, 0))] + [ANY] * more,
        out_specs=[pl.BlockSpec((cps * CHUNK, D), lambda i: (i, 0)),
                   pl.BlockSpec((cps, HG_H * HD, HD), lambda i: (i, 0, 0))] + [ANY] * more,
        out_shape=[jax.ShapeDtypeStruct((t, D), BF16), jax.ShapeDtypeStruct((t // CHUNK, HG_H * HD, HD), F32)]
        + ([jax.ShapeDtypeStruct((4,) + gather.shape, gather.dtype)] if more else []),
        scratch_shapes=[pltpu.VMEM((HG_H * HD, HD), F32)] + GATHER_SEMS * more,
        compiler_params=_cparams(("arbitrary",)),
    )(u, lb, go, *([gather] * more))


def _hg_bwd(u, sts, dp, lb, go, scatter=None):
    t = u.shape[0]
    cps = HG_CPS if t % (HG_CPS * CHUNK) == 0 else 1
    n = t // (cps * CHUNK)

    def body(u_ref, sts_ref, dp_ref, lb_ref, go_ref, *rest):
        if scatter is None:
            du_ref, dlb_ref, dgo_ref, dst_ref = rest
        else:
            parts_ref, du_ref, dlb_ref, dgo_ref, recv_ref, dst_ref, *sems = rest
            start, wait = _scatter_plan(parts_ref, recv_ref, *sems)
            pl.when(pl.program_id(0) == 0)(start)

        @pl.when(pl.program_id(0) == 0)
        def _():
            dst_ref[...] = jnp.zeros_like(dst_ref)
            dlb_ref[...] = jnp.zeros_like(dlb_ref)
            dgo_ref[...] = jnp.zeros_like(dgo_ref)

        dst = dst_ref[...]
        for j in reversed(range(cps)):
            rows = slice(j * CHUNK, (j + 1) * CHUNK)
            _, vjp = jax.vjp(_hg_chunk, u_ref[rows, 0:D], u_ref[rows, D:2 * D], u_ref[rows, 2 * D:3 * D],
                             u_ref[rows, 3 * D:4 * D], sts_ref[j], lb_ref[...], go_ref[...])
            dq, df, dv, dz, dst, dlb, dgo = vjp((dp_ref[rows, :].astype(F32), dst))
            du_ref[rows, 0:D] = dq.astype(BF16)
            du_ref[rows, D:2 * D] = df.astype(BF16)
            du_ref[rows, 2 * D:3 * D] = dv.astype(BF16)
            du_ref[rows, 3 * D:4 * D] = dz.astype(BF16)
            dlb_ref[...] += dlb
            dgo_ref[...] += dgo
        dst_ref[...] = dst
        if scatter is not None:
            pl.when(pl.program_id(0) == n - 1)(wait)

    rev = lambda i: (n - 1 - i, 0)
    more = scatter is not None
    return pl.pallas_call(
        body, name="hg_bwd_scatter" if more else "hg_bwd", grid=(n,),
        in_specs=[pl.BlockSpec((cps * CHUNK, 4 * D), rev),
                  pl.BlockSpec((cps, HG_H * HD, HD), lambda i: (n - 1 - i, 0, 0)),
                  pl.BlockSpec((cps * CHUNK, D), rev),
                  pl.BlockSpec((1, D), lambda i: (0, 0)),
                  pl.BlockSpec((1, HD), lambda i: (0, 0))] + [ANY] * more,
        out_specs=[pl.BlockSpec((cps * CHUNK, 4 * D), rev),
                   pl.BlockSpec((1, D), lambda i: (0, 0)),
                   pl.BlockSpec((1, HD), lambda i: (0, 0))] + [ANY] * more,
        out_shape=[jax.ShapeDtypeStruct((t, 4 * D), BF16), jax.ShapeDtypeStruct((1, D), F32),
                   jax.ShapeDtypeStruct((1, HD), F32)]
        + ([jax.ShapeDtypeStruct((3,) + scatter.shape[1:], scatter.dtype)] if more else []),
        scratch_shapes=[pltpu.VMEM((HG_H * HD, HD), F32)] + SCATTER_SEMS * more,
        compiler_params=_cparams(("arbitrary",)),
    )(u, sts, dp, lb, go, *([scatter] * more))


GD_VH = 16
GD_QKH = 8
GD_QKV = 4096
GD_VW = 2048
GD_N = GD_QKV + GD_VW + HD
GD_GRP = 4
GD_SOLVE = (GD_VH // GD_GRP, GD_GRP * CHUNK, 2 * HD)
HALO = 8


def _mm_high(a, b):
    return _dot(a, b, 1, 0, lax.Precision.HIGH)


def _lane_pick(a, h):
    lane = lax.broadcasted_iota(jnp.int32, a.shape, 1)
    return jnp.sum(jnp.where(lane == h, a, 0.0), axis=1, keepdims=True)


def _l2n(x):
    return x * lax.rsqrt(jnp.sum(x * x, axis=1, keepdims=True) + EPS)


def _solve_fwd(a_mats, rhss):
    n = a_mats[0].shape[0]
    r_i, c_i = lax.broadcasted_iota(jnp.int32, (n, n), 0), lax.broadcasted_iota(jnp.int32, (n, n), 1)
    same = lambda nb: (r_i // nb) == (c_i // nb)
    eye = (r_i == c_i).astype(F32)
    d0s = [jnp.where(same(8), a, 0.0) for a in a_mats]
    d2s = [_mm(d, d) for d in d0s]
    tinvs = [eye - d for d in d0s]
    tinvs = [t + _mm(t, d2) for t, d2 in zip(tinvs, d2s)]
    d4s = [_mm(d2, d2) for d2 in d2s]
    tinvs = [t + _mm(t, d4) for t, d4 in zip(tinvs, d4s)]
    nb = 16
    while nb <= CHUNK:
        tls = [_mm(t, jnp.where(same(nb) & ~same(nb // 2), a, 0.0)) for t, a in zip(tinvs, a_mats)]
        tinvs = [t - _mm(tl, t) for t, tl in zip(tinvs, tls)]
        nb *= 2
    return tinvs, [_mm(t, r) for t, r in zip(tinvs, rhss)]


def _solve_bwd(res, dx):
    tinv, x = res
    drhs = _mm_tn(tinv, dx)
    return -_mm_nt(drhs, x), drhs


@jax.custom_vjp
def _solved(a_mat, rhs, tinv, x):
    return x


_solved.defvjp(lambda a_mat, rhs, tinv, x: (x, (tinv, x)),
               lambda res, dx: _solve_bwd(res, dx) + (jnp.zeros_like(res[0]), jnp.zeros_like(res[1])))


def _gd_chunk(xh, x, z, ab, st, cw, alog, dtb, go, solve):
    c = x.shape[0]
    xa = jnp.concatenate([xh, x], axis=0)
    sh = [jnp.split(_roll_rows(xa, 3 - j), [HALO], axis=0)[1] for j in range(3)]
    qkv = _silu(cw[0:1] * sh[0] + cw[1:2] * sh[1] + cw[2:3] * sh[2] + cw[3:4] * x)
    q_all, k_all, v_all = jnp.split(qkv, [1024, 2048], axis=1)
    lane = lax.broadcasted_iota(jnp.int32, (c, HD), 1)
    a_part = jnp.where(lane < GD_VH, ab, 0.0)
    g_all = -jnp.exp(alog) * jax.nn.softplus(a_part + dtb)
    d_all = _cumsum_rows(g_all)
    dl_all = jnp.sum(g_all, axis=0, keepdims=True)
    beta_all = jax.nn.sigmoid(ab)
    gc = GD_GRP * c
    r_i, c_i = lax.broadcasted_iota(jnp.int32, (gc, gc), 0), lax.broadcasted_iota(jnp.int32, (gc, gc), 1)
    same_head = (r_i // c) == (c_i // c)
    tri_g, strict_g = same_head & (c_i <= r_i), same_head & (c_i < r_i)
    qs =jnp.split(q_all, GD_QKH, axis=1)
    ks = jnp.split(k_all, GD_QKH, axis=1)
    vs = jnp.split(v_all, GD_VH, axis=1)
    zs = jnp.split(z, GD_VH, axis=1)
    sts = jnp.split(st, GD_VH, axis=0)
    qn = [_l2n(a) * (HD ** -0.5) for a in qs]
    kn = [_l2n(a) for a in ks]
    p_out, st_out, pre = [], [], []
    for g in range(GD_VH // GD_GRP):
        heads = range(GD_GRP * g, GD_GRP * (g + 1))
        stack = lambda f: jnp.concatenate([f(h) for h in heads], axis=0)
        q_, k_, v_ = stack(lambda h: qn[h // 2]), stack(lambda h: kn[h // 2]), stack(lambda h: vs[h])
        dcol = stack(lambda h: _lane_pick(d_all, h))
        bcol = stack(lambda h: _lane_pick(beta_all, GD_VH + h))
        dlast = stack(lambda h: jnp.broadcast_to(_lane_pick(dl_all, h), (c, 1)))
        drow = jnp.sum(jnp.broadcast_to(dcol, (gc, HD)).T, axis=0, keepdims=True) * (1.0 / HD)
        dec = jnp.exp(jnp.where(tri_g, dcol - drow, -jnp.inf))
        kb = k_ * bcol
        a_mat = jnp.where(strict_g, _mm_nt(kb, k_) * dec, 0.0)
        pre.append((heads, q_, k_, dcol, dlast, dec, a_mat, jnp.concatenate([v_ * bcol, kb * jnp.exp(dcol)], axis=1)))
    xsols = solve([e[6] for e in pre], [e[7] for e in pre])
    heads_of = [e[0] for e in pre]
    per_head = lambda a: jnp.split(a, GD_GRP, axis=0)
    uw = [jnp.split(x, 2, axis=1) for x in xsols]
    ws = [[_mm(wh, sts[h]) for wh, h in zip(per_head(w_), heads)] for (_, w_), heads in zip(uw, heads_of)]
    v_new = [u_ - jnp.concatenate(w, axis=0) for (u_, _), w in zip(uw, ws)]
    qk = [_mm_nt(e[1], e[2]) * e[5] for e in pre]
    qs_ = [[_mm(qh, sts[h]) for qh, h in zip(per_head(e[1] * jnp.exp(e[3])), e[0])] for e in pre]
    o_g = [_mm(a, vn) + jnp.concatenate(b, axis=0) for a, vn, b in zip(qk, v_new, qs_)]
    upd = [[_mm_tn(kh, vh) for kh, vh in zip(per_head(e[2] * jnp.exp(e[4] - e[3])), per_head(vn))]
           for e, vn in zip(pre, v_new)]
    for heads, og, up in zip(heads_of, o_g, upd):
        for h, o, u_st in zip(heads, per_head(og), up):
            st_out.append(sts[h] * jnp.exp(_lane_pick(dl_all, h)) + u_st)
            y = o * lax.rsqrt(jnp.mean(o * o, axis=1, keepdims=True) + EPS) * go
            p_out.append(y * _silu(zs[h]))
    return jnp.concatenate(p_out, axis=1), jnp.concatenate(st_out, axis=0)


def _gd_specs(n, rev):
    ci = (lambda i: n - 1 - i) if rev else (lambda i: i)
    return [pl.BlockSpec((HALO, GD_QKV), lambda i: (jnp.maximum(ci(i) * (CHUNK // HALO) - 1, 0), 0)),
            pl.BlockSpec((CHUNK, GD_N), lambda i: (ci(i), 0))]


def _gd_load(uh_ref, u_ref, first):
    xh = jnp.where(first, 0.0, uh_ref[...])
    return xh, u_ref[:, 0:GD_QKV], u_ref[:, GD_QKV:GD_QKV + GD_VW], u_ref[:, GD_QKV + GD_VW:GD_N]


def _gd_fwd(u, cw, alog, dtb, go):
    t = u.shape[0]
    n = t // CHUNK
    small = lambda r, w: pl.BlockSpec((r, w), lambda i: (0, 0))

    def body(uh_ref, u_ref, cw_ref, alog_ref, dtb_ref, go_ref, p_ref, sts_ref, tinv_ref, xsol_ref, st_ref):
        i = pl.program_id(0)

        @pl.when(i == 0)
        def _():
            st_ref[...] = jnp.zeros_like(st_ref)

        def solve(a_mats, rhss):
            tinvs, xsols = _solve_fwd(a_mats, rhss)
            for g, (tinv, xsol) in enumerate(zip(tinvs, xsols)):
                tinv_ref[0, g] = tinv
                xsol_ref[0, g] = xsol
            return xsols

        st = st_ref[...]
        sts_ref[0] = st
        p, st_next = _gd_chunk(*_gd_load(uh_ref, u_ref, i == 0), st, cw_ref[...], alog_ref[...], dtb_ref[...],
                               go_ref[...], solve)
        p_ref[...] = p.astype(BF16)
        st_ref[...] = st_next

    return pl.pallas_call(
        body, name="gd_fwd", grid=(n,),
        in_specs=_gd_specs(n, False) + [small(8, GD_QKV), small(1, HD), small(1, HD), small(1, HD)],
        out_specs=[pl.BlockSpec((CHUNK, GD_VW), lambda i: (i, 0)),
                   pl.BlockSpec((1, GD_VH * HD, HD), lambda i: (i, 0, 0)),
                   pl.BlockSpec((1,) + GD_SOLVE, lambda i: (i, 0, 0, 0)),
                   pl.BlockSpec((1,) + GD_SOLVE, lambda i: (i, 0, 0, 0))],
        out_shape=[jax.ShapeDtypeStruct((t, GD_VW), BF16), jax.ShapeDtypeStruct((n, GD_VH * HD, HD), F32),
                   jax.ShapeDtypeStruct((n,) + GD_SOLVE, F32), jax.ShapeDtypeStruct((n,) + GD_SOLVE, F32)],
        scratch_shapes=[pltpu.VMEM((GD_VH * HD, HD), F32)],
        compiler_params=_cparams(("arbitrary",)),
    )(u, u, cw, alog, dtb, go)


def _gd_bwd(u, sts, tinvs, xsols, dp, cw, alog, dtb, go):
    t = u.shape[0]
    n = t // CHUNK
    small = lambda r, w: pl.BlockSpec((r, w), lambda i: (0, 0))

    def body(uh_ref, u_ref, sts_ref, tinv_ref, xsol_ref, dp_ref, cw_ref, alog_ref, dtb_ref, go_ref,
             du_ref, dcw_ref, dalog_ref, ddtb_ref, dgo_ref, dst_ref, dhalo_ref):
        i = pl.program_id(0)

        @pl.when(i == 0)
        def _():
            for r in (dst_ref, dhalo_ref, dcw_ref, dalog_ref, ddtb_ref, dgo_ref):
                r[...] = jnp.zeros_like(r)

        solve = lambda a_mats, rhss: [_solved(a, r, tinv_ref[0, g], xsol_ref[0, g])
                                      for g, (a, r) in enumerate(zip(a_mats, rhss))]
        chunk = functools.partial(_gd_chunk, solve=solve)
        _, vjp = jax.vjp(chunk, *_gd_load(uh_ref, u_ref, i == n - 1), sts_ref[0], cw_ref[...], alog_ref[...],
                         dtb_ref[...], go_ref[...])
        dxh, dx, dz, dab, dst, dcw, dalog, ddtb, dgo = vjp((dp_ref[...].astype(F32), dst_ref[...]))
        tail = jnp.concatenate([jnp.zeros((CHUNK - HALO, GD_QKV), F32), dhalo_ref[...]], axis=0)
        du_ref[:, 0:GD_QKV] = (dx + tail).astype(BF16)
        du_ref[:, GD_QKV:GD_QKV + GD_VW] = dz.astype(BF16)
        du_ref[:, GD_QKV + GD_VW:GD_N] = dab.astype(BF16)
        dhalo_ref[...] = dxh
        dst_ref[...] = dst
        dcw_ref[...] += dcw
        dalog_ref[...] += dalog
        ddtb_ref[...] += ddtb
        dgo_ref[...] += dgo

    return pl.pallas_call(
        body, name="gd_bwd", grid=(n,),
        in_specs=_gd_specs(n, True) + [pl.BlockSpec((1, GD_VH * HD, HD), lambda i: (n - 1 - i, 0, 0)),
                                       pl.BlockSpec((1,) + GD_SOLVE, lambda i: (n - 1 - i, 0, 0, 0)),
                                       pl.BlockSpec((1,) + GD_SOLVE, lambda i: (n - 1 - i, 0, 0, 0)),
                                       pl.BlockSpec((CHUNK, GD_VW), lambda i: (n - 1 - i, 0)),
                                       small(8, GD_QKV), small(1, HD), small(1, HD), small(1, HD)],
        out_specs=[pl.BlockSpec((CHUNK, GD_N), lambda i: (n - 1 - i, 0)),
                   small(8, GD_QKV), small(1, HD), small(1, HD), small(1, HD)],
        out_shape=[jax.ShapeDtypeStruct((t, GD_N), BF16), jax.ShapeDtypeStruct((8, GD_QKV), F32)]
        + [jax.ShapeDtypeStruct((1, HD), F32)] * 3,
        scratch_shapes=[pltpu.VMEM((GD_VH * HD, HD), F32), pltpu.VMEM((HALO, GD_QKV), F32)],
        compiler_params=_cparams(("arbitrary",)),
    )(u, u, sts, tinvs, xsols, dp, cw, alog, dtb, go)


SW_B = 128
SW_G = 4
SW_N = 2560
SW_KV0 = 1024


def _blockdiag(n, blk):
    r = lax.broadcasted_iota(jnp.int32, (n, n), 0) // blk
    c = lax.broadcasted_iota(jnp.int32, (n, n), 1) // blk
    return (r == c).astype(F32)


def _sw_normrope(x, g1, g2, cos, sin):
    w = x.shape[1] // 2
    x1, x2 = jnp.split(x, 2, axis=1)
    ms = _mm_high(x1 * x1 + x2 * x2, _blockdiag(w, 32)) * (1.0 / 64.0)
    rinv = lax.rsqrt(ms + EPS)
    n1, n2 = x1 * rinv * g1, x2 * rinv * g2
    return jnp.concatenate([n1 * cos - n2 * sin, n2 * cos + n1 * sin], axis=1)


def _sw_block(q, kvp, kvc, z, csp, csc, gq, gk, sinks, has_prev):
    b = q.shape[0]
    cos_c, sin_c = jnp.split(csc, 2, axis=1)
    cos_p, sin_p = jnp.split(csp, 2, axis=1)
    tile4 = lambda a: jnp.concatenate([a] * 4, axis=1)
    qh = _sw_normrope(q, gq[0:1], gq[1:2], tile4(cos_c), tile4(sin_c))
    kp, vp = jnp.split(kvp, 2, axis=1)
    kc, vc = jnp.split(kvc, 2, axis=1)
    kh = jnp.concatenate([_sw_normrope(kp, gk[0:1], gk[1:2], cos_p, sin_p),
                          _sw_normrope(kc, gk[0:1], gk[1:2], cos_c, sin_c)], axis=0)
    vv = jnp.concatenate([vp, vc], axis=0)
    q1, q2 = jnp.split(qh, 2, axis=1)
    q1g, q2g = jnp.split(q1, SW_G, axis=1), jnp.split(q2, SW_G, axis=1)
    own = lax.broadcasted_iota(jnp.int32, (4 * b, b), 1) <= lax.broadcasted_iota(jnp.int32, (4 * b, b), 0) % b
    ri = lax.broadcasted_iota(jnp.int32, (256, 256), 0)
    ci = lax.broadcasted_iota(jnp.int32, (256, 256), 1)
    row_head = lax.broadcasted_iota(jnp.int32, (4 * b, 256), 0) // b
    lane_q = lax.broadcasted_iota(jnp.int32, (4 * b, 256), 1)
    q_sel = (lane_q % 128) // 32 == row_head
    o_sel = lane_q // 64 == row_head
    groups = range(SW_G)
    ek = [((ri // 128 == ci // 128) & ((ri % 128) // 32 == g) & (ri % 32 == ci % 32)).astype(F32) for g in groups]
    ev = [((ri // 64 == g) & (ri % 64 == ci % 64)).astype(F32) for g in groups]
    kx = [_mm(kh, ek[g]) for g in groups]
    vx = [_mm(vv, ev[g]) for g in groups]
    q4 = [jnp.where(q_sel, jnp.concatenate([jnp.concatenate([q1g[g], q2g[g]], axis=1)] * 4, axis=0), 0.0)
          for g in groups]
    sink = [jnp.concatenate([jnp.broadcast_to(_lane_pick(sinks, 4 * g + j), (b, 1)) for j in range(4)], axis=0)
            for g in groups]
    sc = [jnp.split(_mm_nt(q4[g], kx[g]) * (64 ** -0.5), 2, axis=1) for g in groups]
    s = [jnp.where(own, sc[g][1], jnp.where(has_prev, sc[g][0], -jnp.inf)) for g in groups]
    top = [jnp.max(s[g]) for g in groups]
    soft = [top[g] + 8.0 * jnp.log(jnp.sum(jnp.exp((s[g] - top[g]) * 0.125), axis=1, keepdims=True)) for g in groups]
    m = [lax.stop_gradient(jnp.maximum(soft[g], sink[g])) for g in groups]
    p = [jnp.exp(s[g] - m[g]) for g in groups]
    pn = [p[g] / (jnp.sum(p[g], axis=1, keepdims=True) + jnp.exp(sink[g] - m[g])) for g in groups]
    pn2 = [jnp.concatenate([jnp.where(own, 0.0, pn[g]), jnp.where(own, pn[g], 0.0)], axis=1) for g in groups]
    o4 = [jnp.split(jnp.where(o_sel, _mm(pn2[g], vx[g]), 0.0), 4, axis=0) for g in groups]
    o_out = [o4[g][0] + o4[g][1] + o4[g][2] + o4[g][3] for g in groups]
    return jnp.concatenate(o_out, axis=1) * _silu(z)


def _sw_specs(n, rev):
    ci = (lambda i: n - 1 - i) if rev else (lambda i: i)
    prev = lambda i: jnp.maximum(ci(i) - 1, 0)
    return [pl.BlockSpec((SW_B, SW_N), lambda i: (ci(i), 0)),
            pl.BlockSpec((SW_B, 512), lambda i: (prev(i), SW_KV0 // 512)),
            pl.BlockSpec((SW_B, 256), lambda i: (ci(i), 0)),
            pl.BlockSpec((SW_B, 256), lambda i: (prev(i), 0)),
            pl.BlockSpec((2, 512), lambda i: (0, 0)), pl.BlockSpec((2, 128), lambda i: (0, 0)),
            pl.BlockSpec((1, 128), lambda i: (0, 0))]


def _sw_args(u_ref, kvp_ref, csc_ref, csp_ref, gq_ref, gk_ref, sk_ref, has_prev):
    return (u_ref[:, 0:D], kvp_ref[...], u_ref[:, SW_KV0:SW_KV0 + 512], u_ref[:, SW_KV0 + 512:SW_N],
            csp_ref[...], csc_ref[...], gq_ref[...], gk_ref[...], sk_ref[...], has_prev)


def _sw_fwd(u, cs, gq, gk, sinks):
    t = u.shape[0]
    n = t // SW_B

    def body(u_ref, kvp_ref, csc_ref, csp_ref, gq_ref, gk_ref, sk_ref, p_ref):
        has_prev = pl.program_id(0) > 0
        p_ref[...] = _sw_block(*_sw_args(u_ref, kvp_ref, csc_ref, csp_ref, gq_ref, gk_ref, sk_ref, has_prev)
                               ).astype(BF16)

    return pl.pallas_call(
        body, name="sw_fwd", grid=(n,), in_specs=_sw_specs(n, False),
        out_specs=pl.BlockSpec((SW_B, D), lambda i: (i, 0)),
        out_shape=jax.ShapeDtypeStruct((t, D), BF16),
        compiler_params=_cparams(("arbitrary",)),
    )(u, u, cs, cs, gq, gk, sinks)


def _sw_bwd(u, cs, dp, gq, gk, sinks):
    t = u.shape[0]
    n = t // SW_B

    def body(u_ref, kvp_ref, csc_ref, csp_ref, gq_ref, gk_ref, sk_ref, dp_ref,
             du_ref, dgq_ref, dgk_ref, dsk_ref, dkv_ref):
        i = pl.program_id(0)

        @pl.when(i == 0)
        def _():
            for r in (dkv_ref, dgq_ref, dgk_ref, dsk_ref):
                r[...] = jnp.zeros_like(r)

        has_prev = i < n - 1
        args = _sw_args(u_ref, kvp_ref, csc_ref, csp_ref, gq_ref, gk_ref, sk_ref, has_prev)
        fn = lambda q, kvp, kvc, z, gq_, gk_, sk_: _sw_block(q, kvp, kvc, z, args[4], args[5], gq_, gk_, sk_, has_prev)
        _, vjp = jax.vjp(fn, args[0], args[1], args[2], args[3], args[6], args[7], args[8])
        dq, dkvp, dkvc, dz, dgq, dgk, dsk = vjp(dp_ref[...].astype(F32))
        du_ref[:, 0:D] = dq.astype(BF16)
        du_ref[:, SW_KV0:SW_KV0 + 512] = (dkvc + dkv_ref[...]).astype(BF16)
        du_ref[:, SW_KV0 + 512:SW_N] = dz.astype(BF16)
        dkv_ref[...] = dkvp
        dgq_ref[...] += dgq
        dgk_ref[...] += dgk
        dsk_ref[...] += dsk

    small = lambda r, w: pl.BlockSpec((r, w), lambda i: (0, 0))
    return pl.pallas_call(
        body, name="sw_bwd", grid=(n,),
        in_specs=_sw_specs(n, True) + [pl.BlockSpec((SW_B, D), lambda i: (n - 1 - i, 0))],
        out_specs=[pl.BlockSpec((SW_B, SW_N), lambda i: (n - 1 - i, 0)), small(2, 512), small(2, 128), small(1, 128)],
        out_shape=[jax.ShapeDtypeStruct((t, SW_N), BF16), jax.ShapeDtypeStruct((2, 512), F32),
                   jax.ShapeDtypeStruct((2, 128), F32), jax.ShapeDtypeStruct((1, 128), F32)],
        scratch_shapes=[pltpu.VMEM((SW_B, 512), F32)],
        compiler_params=_cparams(("arbitrary",)),
    )(u, u, cs, cs, gq, gk, sinks, dp)


def _ln_mod(x, g, scale, shift):
    y = x * lax.rsqrt(jnp.mean(x * x, axis=1, keepdims=True) + EPS) * g
    return y * (1.0 + scale) + shift


def _row_tile(t):
    return min(t, 1024)


def _ln_mm(x, g, scale, shift, w, tn):
    t, n = x.shape[0], w.shape[1]
    tm = _row_tile(t)
    vec = pl.BlockSpec((1, D), lambda i, j: (0, 0))

    def body(x_ref, g_ref, sc_ref, sh_ref, w_ref, u_ref, h_ref):
        @pl.when(pl.program_id(1) == 0)
        def _():
            h_ref[...] = _ln_mod(x_ref[...], g_ref[...], sc_ref[...], sh_ref[...]).astype(BF16)

        u_ref[...] = _dot(h_ref[...], w_ref[...], 1, 0)

    return pl.pallas_call(
        body, name="ln_mm", grid=(t // tm, n // tn),
        in_specs=[pl.BlockSpec((tm, D), lambda i, j: (i, 0)), vec, vec, vec,
                  pl.BlockSpec((D, tn), lambda i, j: (0, j))],
        out_specs=[pl.BlockSpec((tm, tn), lambda i, j: (i, j)), pl.BlockSpec((tm, D), lambda i, j: (i, 0))],
        out_shape=[jax.ShapeDtypeStruct((t, n), F32), jax.ShapeDtypeStruct((t, D), BF16)],
        compiler_params=_cparams(("arbitrary", "arbitrary")),
    )(x, g, scale, shift, w)


def _mm_res(p, w, x, gate):
    t, k = p.shape
    tm = _row_tile(t)

    def body(p_ref, w_ref, x_ref, gate_ref, o_ref):
        o_ref[...] = x_ref[...] + gate_ref[...] * _dot(p_ref[...], w_ref[...], 1, 0)

    return pl.pallas_call(
        body, name="mm_res", grid=(t // tm,),
        in_specs=[pl.BlockSpec((tm, k), lambda i: (i, 0)), pl.BlockSpec((k, D), lambda i: (0, 0)),
                  pl.BlockSpec((tm, D), lambda i: (i, 0)), pl.BlockSpec((1, D), lambda i: (0, 0))],
        out_specs=pl.BlockSpec((tm, D), lambda i: (i, 0)),
        out_shape=jax.ShapeDtypeStruct((t, D), F32),
        compiler_params=_cparams(("arbitrary",)),
    )(p, w, x, gate)


def _loss_grad(x, target):
    t = x.shape[0]
    tm = _row_tile(t)

    def body(x_ref, t_ref, l_ref, dx_ref):
        @pl.when(pl.program_id(0) == 0)
        def _():
            l_ref[...] = jnp.zeros_like(l_ref)

        err = x_ref[...] - t_ref[...]
        dx_ref[...] = err * (1.0 / D)
        l_ref[...] += 0.5 * jnp.sum(jnp.mean(err * err, axis=1, keepdims=True), axis=0, keepdims=True)

    return pl.pallas_call(
        body, name="loss_grad", grid=(t // tm,),
        in_specs=[pl.BlockSpec((tm, D), lambda i: (i, 0))] * 2,
        out_specs=[pl.BlockSpec((8, 128), lambda i: (0, 0)), pl.BlockSpec((tm, D), lambda i: (i, 0))],
        out_shape=[jax.ShapeDtypeStruct((8, 128), F32), jax.ShapeDtypeStruct((t, D), F32)],
        compiler_params=_cparams(("arbitrary",)),
    )(x, target)


def _mm_scaled(a, s, w, tn):
    t, k = a.shape
    n = w.shape[1]
    tm = _row_tile(t)

    def body(a_ref, s_ref, w_ref, o_ref):
        o_ref[...] = _dot((a_ref[...] * s_ref[...]).astype(BF16), w_ref[...], 1, 0).astype(BF16)

    return pl.pallas_call(
        body, name="mm_scaled", grid=(t // tm, n // tn),
        in_specs=[pl.BlockSpec((tm, k), lambda i, j: (i, 0)), pl.BlockSpec((1, k), lambda i, j: (0, 0)),
                  pl.BlockSpec((k, tn), lambda i, j: (0, j))],
        out_specs=pl.BlockSpec((tm, tn), lambda i, j: (i, j)),
        out_shape=jax.ShapeDtypeStruct((t, n), BF16),
        compiler_params=_cparams(("arbitrary", "arbitrary")),
    )(a, s, w)


def _mm_tn_acc(a, b, tn):
    t, m = a.shape
    n = b.shape[1]
    fits = lambda k: 2 * k * (m * a.dtype.itemsize + tn * b.dtype.itemsize) + 2 * m * tn * 4 <= 36 * 1024 * 1024
    tk = next(k for k in (4096, 2048, 1024, 512, t) if t % k == 0 and (fits(k) or k <= 512))
    nk = t // tk

    def body(a_ref, b_ref, o_ref):
        @pl.when(pl.program_id(1) == 0)
        def _():
            o_ref[...] = jnp.zeros_like(o_ref)

        o_ref[...] += _dot(a_ref[...], b_ref[...].astype(BF16), 0, 0)

    return pl.pallas_call(
        body, name="mm_tn_acc", grid=(n // tn, nk),
        in_specs=[pl.BlockSpec((tk, m), lambda j, k: (k, 0)), pl.BlockSpec((tk, tn), lambda j, k: (k, j))],
        out_specs=pl.BlockSpec((m, tn), lambda j, k: (0, j)),
        out_shape=jax.ShapeDtypeStruct((m, n), F32),
        compiler_params=_cparams(("arbitrary", "arbitrary")),
    )(a, b)


def _inproj_bwd(du, wt, x, dxp, g, scale, shift):
    t, kdim = du.shape
    tk = kdim
    tm = min(t, 512 if kdim <= 4096 else 256)
    nk = kdim // tk
    vec = pl.BlockSpec((1, D), lambda i, k: (0, 0))

    def body(du_ref, wt_ref, x_ref, dxp_ref, g_ref, sc_ref, sh_ref, dx_ref, dv_ref, acc_ref):
        k = pl.program_id(1)

        @pl.when((pl.program_id(0) == 0) & (k == 0))
        def _():
            dv_ref[...] = jnp.zeros_like(dv_ref)

        @pl.when(k == 0)
        def _():
            acc_ref[...] = jnp.zeros_like(acc_ref)

        acc_ref[...] += _dot(du_ref[...].astype(BF16), wt_ref[...], 1, 0)

        @pl.when(k == nk - 1)
        def _():
            _, vjp = jax.vjp(_ln_mod, x_ref[...], g_ref[...], sc_ref[...], sh_ref[...])
            dx, dg, dsc, dsh = vjp(acc_ref[...])
            dx_ref[...] = dxp_ref[...] + dx
            dv_ref[0:1, :] += dg
            dv_ref[1:2, :] += dsc
            dv_ref[2:3, :] += dsh

    return pl.pallas_call(
        body, name="inproj_bwd", grid=(t // tm, nk),
        in_specs=[pl.BlockSpec((tm, tk), lambda i, k: (i, k)), pl.BlockSpec((tk, D), lambda i, k: (k, 0)),
                  pl.BlockSpec((tm, D), lambda i, k: (i, 0)), pl.BlockSpec((tm, D), lambda i, k: (i, 0)),
                  vec, vec, vec],
        out_specs=[pl.BlockSpec((tm, D), lambda i, k: (i, 0)), pl.BlockSpec((8, D), lambda i, k: (0, 0))],
        out_shape=[jax.ShapeDtypeStruct((t, D), F32), jax.ShapeDtypeStruct((8, D), F32)],
        scratch_shapes=[pltpu.VMEM((tm, D), F32)],
        compiler_params=_cparams(("arbitrary", "arbitrary")),
    )(du, wt, x, dxp, g, scale, shift)


def _outgrad(gmat, w, gate):
    k = gmat.shape[0]
    tr = 256

    def body(g_ref, w_ref, gate_ref, dw_ref, dg_ref):
        @pl.when(pl.program_id(0) == 0)
        def _():
            dg_ref[...] = jnp.zeros_like(dg_ref)

        gm = g_ref[...]
        dw_ref[...] = gm * gate_ref[...]
        dg_ref[0:1, :] += jnp.sum(gm * w_ref[...].astype(F32), axis=0, keepdims=True)

    return pl.pallas_call(
        body, name="outgrad", grid=(k // tr,),
        in_specs=[pl.BlockSpec((tr, D), lambda i: (i, 0)), pl.BlockSpec((tr, D), lambda i: (i, 0)),
                  pl.BlockSpec((1, D), lambda i: (0, 0))],
        out_specs=[pl.BlockSpec((tr, D), lambda i: (i, 0)), pl.BlockSpec((8, D), lambda i: (0, 0))],
        out_shape=[jax.ShapeDtypeStruct((k, D), F32), jax.ShapeDtypeStruct((8, D), F32)],
        compiler_params=_cparams(("arbitrary",)),
    )(gmat, w, gate)


def _rope_table(pos, freq):
    t = pos.shape[0]
    tm = _row_tile(t)

    def body(p_ref, f_ref, o_ref):
        ang = p_ref[...].astype(F32) * f_ref[...]
        o_ref[:, 0:128] = jnp.cos(ang)
        o_ref[:, 128:256] = jnp.sin(ang)

    return pl.pallas_call(
        body, name="rope_table", grid=(t // tm,),
        in_specs=[pl.BlockSpec((tm, 1), lambda i: (i, 0)), pl.BlockSpec((1, 128), lambda i: (0, 0))],
        out_specs=pl.BlockSpec((tm, 256), lambda i: (i, 0)),
        out_shape=jax.ShapeDtypeStruct((t, 256), F32),
        compiler_params=_cparams(("arbitrary",)),
    )(pos, freq)


def _ada_fwd(c_all, w, b):
    nl, _, s = w.shape

    def body(c_ref, w_ref, b_ref, o_ref):
        o_ref[0] = _mm_f32(c_ref[...], w_ref[0]) + b_ref[0]

    return pl.pallas_call(
        body, name="ada_fwd", grid=(nl,),
        in_specs=[pl.BlockSpec((8, D), lambda l: (0, 0)), pl.BlockSpec((1, D, s), lambda l: (l, 0, 0)),
                  pl.BlockSpec((1, 1, s), lambda l: (l, 0, 0))],
        out_specs=pl.BlockSpec((1, 8, s), lambda l: (l, 0, 0)),
        out_shape=jax.ShapeDtypeStruct((nl, 8, s), F32),
        compiler_params=_cparams(("arbitrary",)),
    )(c_all, w, b)


def _ada_bwd(c_all, dmod_cols, dmod_all):
    nl, _, s = dmod_cols.shape

    def body(c_ref, dc_ref, da_ref, gw_ref, gb_ref):
        gw_ref[0] = _dot(c_ref[...], dc_ref[0], 0, 0, lax.Precision.HIGHEST)
        gb_ref[0] = jnp.sum(da_ref[0], axis=0, keepdims=True)

    return pl.pallas_call(
        body, name="ada_bwd", grid=(nl,),
        in_specs=[pl.BlockSpec((8, D), lambda l: (0, 0)), pl.BlockSpec((1, 8, s), lambda l: (l, 0, 0)),
                  pl.BlockSpec((1, 8, 3 * D), lambda l: (l, 0, 0))],
        out_specs=[pl.BlockSpec((1, D, s), lambda l: (l, 0, 0)), pl.BlockSpec((1, 1, 3 * D), lambda l: (l, 0, 0))],
        out_shape=[jax.ShapeDtypeStruct((nl, D, s), F32), jax.ShapeDtypeStruct((nl, 1, 3 * D), F32)],
        compiler_params=_cparams(("arbitrary",)),
    )(c_all, dmod_cols, dmod_all)


def _lb_fn(h8):
    sm = jax.nn.softmax(h8, axis=0)
    r = lax.broadcasted_iota(jnp.int32, (8, 8), 0)
    c = lax.broadcasted_iota(jnp.int32, (8, 8), 1)
    return _mm_f32(((c >= 1) & (c <= r)).astype(F32), sm)


def _lb_fwd(h8):
    def body(h_ref, o_ref):
        o_ref[...] = _lb_fn(h_ref[...])

    return pl.pallas_call(body, name="lb_fwd", out_shape=jax.ShapeDtypeStruct((8, D), F32))(h8)


def _lb_bwd(h8, dlb8):
    def body(h_ref, d_ref, o_ref):
        _, vjp = jax.vjp(_lb_fn, h_ref[...])
        o_ref[...] = vjp(d_ref[...])[0]

    return pl.pallas_call(body, name="lb_bwd", out_shape=jax.ShapeDtypeStruct((8, D), F32))(h8, dlb8)


ADAM_LR, ADAM_B1, ADAM_B2, ADAM_EPS, ADAM_WD, ADAM_STEP = 0.001, 0.9, 0.999, 1e-08, 0.01, 10


def _adamw(w, gparts, m, v):
    r, c = w.shape
    tr = r if r * c * 4 <= (1 << 20) else max(8, ((1 << 20) // (c * 4)) // 8 * 8)
    while r % tr:
        tr -= 8
    ng = len(gparts)

    def body(*refs):
        w_ref, m_ref, v_ref = refs[0], refs[1 + ng], refs[2 + ng]
        g_ref, d_ref, nm_ref, nv_ref = refs[3 + ng:]
        g = refs[1][...]
        for gr in refs[2:1 + ng]:
            g = g + gr[...]
        mm = ADAM_B1 * m_ref[...] + (1.0 - ADAM_B1) * g
        vv = ADAM_B2 * v_ref[...] + (1.0 - ADAM_B2) * (g * g)
        m_hat = mm / (1.0 - ADAM_B1 ** ADAM_STEP)
        v_hat = vv / (1.0 - ADAM_B2 ** ADAM_STEP)
        g_ref[...] = g
        d_ref[...] = -ADAM_LR * (m_hat / (jnp.sqrt(v_hat) + ADAM_EPS) + ADAM_WD * w_ref[...])
        nm_ref[...] = mm
        nv_ref[...] = vv

    spec = pl.BlockSpec((tr, c), lambda i: (i, 0))
    return pl.pallas_call(
        body, name="adamw", grid=(r // tr,), in_specs=[spec] * (3 + ng), out_specs=[spec] * 4,
        out_shape=[jax.ShapeDtypeStruct((r, c), F32)] * 4,
        compiler_params=_cparams(("arbitrary",)),
    )(w, *gparts, m, v)


def _sum_rows(parts):
    r, c = parts[0].shape
    tr = 8
    for cand in range(min(r, 512), 7, -8):
        if r % cand == 0:
            tr = cand
            break

    def body(*refs):
        acc = refs[0][...]
        for p in refs[1:-1]:
            acc = acc + p[...]
        refs[-1][...] = acc

    spec = pl.BlockSpec((tr, c), lambda i: (i, 0))
    return pl.pallas_call(
        body, name="sum_rows", grid=(r // tr,), in_specs=[spec] * len(parts), out_specs=spec,
        out_shape=jax.ShapeDtypeStruct((r, c), F32),
        compiler_params=_cparams(("arbitrary",)),
    )(*parts)


MESH = pl.DeviceIdType.MESH
ANY = pl.BlockSpec(memory_space=pl.ANY)


def _place():
    return lax.axis_index("x"), lax.axis_index("y"), lax.axis_index("c")


def _allgather8(blk):
    m_per, n = blk.shape

    def body(x_ref, out_ref, send_sems, recv_sems, local_sem):
        x, y, c = _place()
        me, sibling = (x, y, c), (x, y, 1 - c)
        chips = [(1 - x, y), (x, 1 - y), (1 - x, 1 - y)]

        def rows(px, py, pc):
            return out_ref.at[pl.ds((4 * px + 2 * py + pc) * m_per, m_per), :]

        def copy(k, block, to, src=None):
            return pltpu.make_async_remote_copy(
                src_ref=rows(*block) if src is None else src, dst_ref=rows(*block),
                send_sem=send_sems.at[k], recv_sem=recv_sems.at[k], device_id=to, device_id_type=MESH)

        mine = pltpu.make_async_copy(x_ref, rows(*me), local_sem)
        mine.start()
        first = [copy(0, me, sibling, src=x_ref)]
        first += [copy(1 + j, me, (*chip, c), src=x_ref) for j, chip in enumerate(chips)]
        for cp in first:
            cp.start()
        passed = [copy(4 + j, (*chip, c), sibling) for j, chip in enumerate(chips)]
        for j, chip in enumerate(chips):
            copy(1 + j, (*chip, c), me).wait_recv()
            passed[j].start()
        copy(0, sibling, me).wait_recv()
        for j, chip in enumerate(chips):
            copy(4 + j, (*chip, 1 - c), me).wait_recv()
        for cp in first + passed:
            cp.wait_send()
        mine.wait()

    return pl.pallas_call(
        body, name="allgather8",
        out_shape=jax.ShapeDtypeStruct((8 * m_per, n), blk.dtype),
        in_specs=[pl.BlockSpec(memory_space=pltpu.VMEM)],
        out_specs=pl.BlockSpec(memory_space=pltpu.VMEM),
        scratch_shapes=[pltpu.SemaphoreType.DMA((7,)), pltpu.SemaphoreType.DMA((7,)), pltpu.SemaphoreType.DMA],
    )(blk)


def _chip_peers():
    x, y, c = _place()
    return [(1 - x, y, c), (x, 1 - y, c), (1 - x, 1 - y, c)]


GATHER_SEMS = [pltpu.SemaphoreType.DMA((3,)), pltpu.SemaphoreType.DMA((3,)), pltpu.SemaphoreType.DMA]
SCATTER_SEMS = [pltpu.SemaphoreType.DMA((3,)), pltpu.SemaphoreType.DMA((3,))]


def _gather_plan(x_ref, out_ref, send_sems, recv_sems, local_sem):
    x, y, _ = _place()
    peers = _chip_peers()

    def copy(j, chip_index):
        return pltpu.make_async_remote_copy(
            src_ref=x_ref, dst_ref=out_ref.at[chip_index], send_sem=send_sems.at[j], recv_sem=recv_sems.at[j],
            device_id=peers[j], device_id_type=MESH)

    mine = pltpu.make_async_copy(x_ref, out_ref.at[2 * x + y], local_sem)
    sends = [copy(j, 2 * x + y) for j in range(3)]

    def start():
        mine.start()
        for cp in sends:
            cp.start()

    def wait():
        for j in range(3):
            copy(j, 2 * peers[j][0] + peers[j][1]).wait_recv()
        for cp in sends:
            cp.wait_send()
        mine.wait()

    return start, wait


def _scatter_plan(p_ref, out_ref, send_sems, recv_sems):
    peers = _chip_peers()
    sends = [pltpu.make_async_remote_copy(
        src_ref=p_ref.at[2 * peers[j][0] + peers[j][1]], dst_ref=out_ref.at[j], send_sem=send_sems.at[j],
        recv_sem=recv_sems.at[j], device_id=peers[j], device_id_type=MESH) for j in range(3)]

    def start():
        for cp in sends:
            cp.start()

    def wait():
        for cp in sends:
            cp.wait_recv()
        for cp in sends:
            cp.wait_send()

    return start, wait


def _chip_allgather(shard):
    def body(x_ref, out_ref, *sems):
        start, wait = _gather_plan(x_ref, out_ref, *sems)
        start()
        wait()

    return pl.pallas_call(
        body, name="chip_allgather", out_shape=jax.ShapeDtypeStruct((4,) + shard.shape, shard.dtype),
        in_specs=[ANY], out_specs=ANY, scratch_shapes=GATHER_SEMS,
    )(shard)


def _chip_scatter(parts):
    def body(p_ref, out_ref, *sems):
        start, wait = _scatter_plan(p_ref, out_ref, *sems)
        start()
        wait()

    return pl.pallas_call(
        body, name="chip_scatter", out_shape=jax.ShapeDtypeStruct((3,) + parts.shape[1:], parts.dtype),
        in_specs=[ANY], out_specs=ANY, scratch_shapes=SCATTER_SEMS,
    )(parts)


def _sibling_swap(a):
    def body(a_ref, out_ref, send_sem, recv_sem):
        x, y, c = _place()
        cp = pltpu.make_async_remote_copy(src_ref=a_ref, dst_ref=out_ref, send_sem=send_sem, recv_sem=recv_sem,
                                          device_id=(x, y, 1 - c), device_id_type=MESH)
        cp.start()
        cp.wait_recv()
        cp.wait_send()

    return pl.pallas_call(
        body, name="sibling_swap", out_shape=jax.ShapeDtypeStruct(a.shape, a.dtype),
        in_specs=[ANY], out_specs=ANY,
        scratch_shapes=[pltpu.SemaphoreType.DMA, pltpu.SemaphoreType.DMA],
    )(a)


WEIGHTS = ['hgrn_lb', 'ada_w', 'ada_b', 'norm_g', 'hg_in_w', 'hg_out_w', 'hg_onorm', 'sw_in_w', 'sw_out_w', 'sw_qnorm',
           'sw_knorm', 'sw_sinks', 'gd_in_w', 'gd_out_w', 'gd_conv_w', 'gd_a_log', 'gd_dt_bias', 'gd_onorm']
BIG = ['hg_in_w', 'hg_out_w', 'sw_in_w', 'sw_out_w', 'gd_in_w', 'gd_out_w']
SEG_FIRST = [('hg_in_w', 0), ('hg_out_w', 0)]
SEG_REST = [('hg_in_w', 1), ('hg_out_w', 1), ('sw_in_w', 0), ('sw_out_w', 0), ('gd_in_w', 0), ('gd_out_w', 0)]
PACK_ALIGN = 16
ROPE_THETA = 10000.0
ADA_S = 3 * D // 4
SMALL_ROW = {'hg_onorm': (0, 256), 'sw_qnorm': (256, 64), 'sw_knorm': (320, 64), 'sw_sinks': (384, 16),
             'gd_a_log': (400, 16), 'gd_dt_bias': (416, 16), 'gd_onorm': (432, 128)}


def _pack_rows(arrs):
    flat = jnp.concatenate([a.reshape(-1, D) for a in arrs], axis=0)
    return jnp.pad(flat, ((0, -flat.shape[0] % PACK_ALIGN), (0, 0)))


def _unpack_rows(packed, shapes):
    out, off = [], 0
    for s in shapes:
        rows = 1
        for d in s:
            rows *= d
        rows //= D
        out.append(packed[..., off:off + rows, :].reshape(packed.shape[:-2] + tuple(s)))
        off += rows
    return out


def _pack_small(vals):
    row = jnp.concatenate([vals[k].reshape(-1) for k in SMALL_ROW])
    row = jnp.pad(row, (0, D - row.shape[0]))[None]
    return jnp.concatenate([vals['hgrn_lb'], vals['norm_g'], vals['gd_conv_w'].reshape(16, D), row,
                            jnp.zeros((7, D), F32)], axis=0)


def _sw_cols(w, inverse=False):
    def split(a, heads):
        shp = (a.shape[0], 2, heads, 32) if inverse else (a.shape[0], heads, 2, 32)
        return a.reshape(shp).transpose(0, 2, 1, 3).reshape(a.shape[0], heads * 64)
    return jnp.concatenate([split(w[:, 0:1024], 16), split(w[:, 1024:1280], 4), w[:, 1280:]], axis=1)


def kernel(x, c, positions, hgrn_lb, ada_w, ada_b, norm_g, hg_in_w, hg_out_w, hg_onorm, sw_in_w, sw_out_w, sw_qnorm, sw_knorm, sw_sinks, gd_in_w, gd_out_w, gd_conv_w, gd_a_log, gd_dt_bias, gd_onorm, loss_target, m_hgrn_lb, m_ada_w, m_ada_b, m_norm_g, m_hg_in_w, m_hg_out_w, m_hg_onorm, m_sw_in_w, m_sw_out_w, m_sw_qnorm, m_sw_knorm, m_sw_sinks, m_gd_in_w, m_gd_out_w, m_gd_conv_w, m_gd_a_log, m_gd_dt_bias, m_gd_onorm, v_hgrn_lb, v_ada_w, v_ada_b, v_norm_g, v_hg_in_w, v_hg_out_w, v_hg_onorm, v_sw_in_w, v_sw_out_w, v_sw_qnorm, v_sw_knorm, v_sw_sinks, v_gd_in_w, v_gd_out_w, v_gd_conv_w, v_gd_a_log, v_gd_dt_bias, v_gd_onorm):
    w_in = dict(hgrn_lb=hgrn_lb, ada_w=ada_w, ada_b=ada_b, norm_g=norm_g, hg_in_w=hg_in_w, hg_out_w=hg_out_w,
                hg_onorm=hg_onorm, sw_in_w=sw_in_w, sw_out_w=sw_out_w, sw_qnorm=sw_qnorm, sw_knorm=sw_knorm,
                sw_sinks=sw_sinks, gd_in_w=gd_in_w, gd_out_w=gd_out_w, gd_conv_w=gd_conv_w, gd_a_log=gd_a_log,
                gd_dt_bias=gd_dt_bias, gd_onorm=gd_onorm)
    m_in = dict(zip(WEIGHTS, (m_hgrn_lb, m_ada_w, m_ada_b, m_norm_g, m_hg_in_w, m_hg_out_w, m_hg_onorm, m_sw_in_w,
                              m_sw_out_w, m_sw_qnorm, m_sw_knorm, m_sw_sinks, m_gd_in_w, m_gd_out_w, m_gd_conv_w,
                              m_gd_a_log, m_gd_dt_bias, m_gd_onorm)))
    v_in = dict(zip(WEIGHTS, (v_hgrn_lb, v_ada_w, v_ada_b, v_norm_g, v_hg_in_w, v_hg_out_w, v_hg_onorm, v_sw_in_w,
                              v_sw_out_w, v_sw_qnorm, v_sw_knorm, v_sw_sinks, v_gd_in_w, v_gd_out_w, v_gd_conv_w,
                              v_gd_a_log, v_gd_dt_bias, v_gd_onorm)))
    ax, ay, ac = _place()
    chip = 2 * ax + ay
    bidx = 4 * ax + 2 * ay + ac
    t = x.shape[1]
    x0, target = x[0], loss_target[0]

    c_all = _allgather8(jnp.pad(c, ((0, 7), (0, 0)))).reshape(8, 8, D)[:, 0, :]
    ada_b_cols = lax.dynamic_slice(ada_b, (0, chip * ADA_S), (4, ADA_S)).reshape(4, 1, ADA_S)
    mod_sh = _ada_fwd(c_all, ada_w, ada_b_cols)
    mod_g = _allgather8(mod_sh.reshape(32, ADA_S)).reshape(4, 2, 4, 8, ADA_S)[:, 0]
    mod = lax.dynamic_index_in_dim(mod_g, bidx, axis=2, keepdims=False).transpose(1, 0, 2).reshape(4, 3 * D)
    shift = [mod[l:l + 1, 0:D] for l in range(4)]
    scale = [mod[l:l + 1, D:2 * D] for l in range(4)]
    gate = [mod[l:l + 1, 2 * D:3 * D] for l in range(4)]

    h8 = jnp.concatenate([hgrn_lb, jnp.full((4, D), -1e30, F32)], axis=0)
    lb_all = _lb_fwd(h8)
    freq = ROPE_THETA ** (-jnp.arange(0, 64, 2, dtype=F32) / 64)
    cs = _rope_table(positions.reshape(t, 1), jnp.tile(freq, 4)[None])

    seg_shapes = lambda seg: [w_in[k].shape[1:] for k, _ in seg]
    pack_seg = lambda src, seg: _pack_rows([src[k][i] for k, i in seg])
    cols_full = lambda a: a.transpose(1, 0, 2).reshape(a.shape[1], 4 * a.shape[2])
    hg_in0_k, hg_out0_k = _unpack_rows(_chip_allgather(pack_seg(w_in, SEG_FIRST).astype(BF16)), seg_shapes(SEG_FIRST))
    win, wout = [cols_full(hg_in0_k)], [hg_out0_k.reshape(D, D)]
    rest_shard = pack_seg(w_in, SEG_REST).astype(BF16)
    tn_in = [1024, 1280, 896, 1024]

    gq = jnp.stack([jnp.tile(sw_qnorm[0, :32], 16), jnp.tile(sw_qnorm[0, 32:], 16)])
    gk = jnp.stack([jnp.tile(sw_knorm[0, :32], 4), jnp.tile(sw_knorm[0, 32:], 4)])
    pad128 = lambda a: jnp.pad(a, ((0, 0), (0, HD - a.shape[1])))
    sinks, alog, dtb = pad128(sw_sinks), pad128(gd_a_log), pad128(gd_dt_bias)
    cw8 = jnp.pad(_chip_allgather(gd_conv_w[0]).transpose(1, 0, 2).reshape(4, GD_QKV), ((0, 4), (0, 0)))
    lbs = {0: lb_all[0:1], 3: lb_all[3:4]}

    xs, us, hs, ps, stss = [x0], [], [], [], []
    for l in range(4):
        u, h = _ln_mm(xs[l], norm_g[l:l + 1], scale[l], shift[l], win[l], tn_in[l])
        if l == 0:
            p, sts, rest_k = _hg_fwd(u, lbs[l], hg_onorm[0:1], gather=rest_shard)
            hg_in1_k, hg_out1_k, sw_in_k, sw_out_k, gd_in_k, gd_out_k = _unpack_rows(rest_k, seg_shapes(SEG_REST))
            win += [_sw_cols(cols_full(sw_in_k)), jnp.pad(cols_full(gd_in_k), ((0, 0), (0, GD_N - 6176))),
                    cols_full(hg_in1_k)]
            wout += [sw_out_k.reshape(D, D), gd_out_k.reshape(GD_VW, D), hg_out1_k.reshape(D, D)]
        elif l % 3 == 0:
            p, sts = _hg_fwd(u, lbs[l], hg_onorm[l // 3:l // 3 + 1])
        elif l % 3 == 1:
            p, sts = _sw_fwd(u, cs, gq, gk, sinks), None
        else:
            p, *sts = _gd_fwd(u, cw8, alog, dtb, gd_onorm)
        xs.append(_mm_res(p, wout[l], xs[l], gate[l]))
        us.append(u), hs.append(h), ps.append(p), stss.append(sts)
    lpart, dx = _loss_grad(xs[4], target)
    loss = lax.psum(lpart[0, 0], ("x", "y", "c"))

    by_chip = lambda g, cols: g.reshape(g.shape[0], 4, cols).transpose(1, 0, 2)
    g_small = {}
    d_in, d_out, dmod, dnorm_g, dlb8, dgo_hg = [None] * 4, [None] * 4, [None] * 4, [None] * 4, jnp.zeros((8, D), F32), {}
    for l in (3, 2, 1, 0):
        dp = _mm_scaled(dx, gate[l], wout[l].T, 1024)
        d_out[l], dgate = _outgrad(_mm_tn_acc(ps[l], dx, D if ps[l].shape[1] == D else 512), wout[l], gate[l])
        if l == 0:
            rest_parts = {('hg_in_w', 1): by_chip(d_in[3], D), ('hg_out_w', 1): d_out[3].reshape(4, D // 4, D),
                          ('sw_in_w', 0): by_chip(_sw_cols(d_in[1], inverse=True), SW_N // 4),
                          ('sw_out_w', 0): d_out[1].reshape(4, D // 4, D),
                          ('gd_in_w', 0): by_chip(d_in[2][:, :6176], 1544),
                          ('gd_out_w', 0): d_out[2].reshape(4, GD_VW // 4, D)}
            rest_packed = jnp.stack([_pack_rows([rest_parts[s][j] for s in SEG_REST]) for j in range(4)])
            du, dlb, dgo_hg[0], rest_recv = _hg_bwd(us[l], stss[l], dp, lbs[l], hg_onorm[0:1],
                                                    scatter=rest_packed.astype(BF16))
            dlb8 = lax.dynamic_update_slice(dlb8, dlb, (l, 0))
        elif l % 3 == 0:
            du, dlb, dgo_hg[l // 3] = _hg_bwd(us[l], stss[l], dp, lbs[l], hg_onorm[l // 3:l // 3 + 1])
            dlb8 = lax.dynamic_update_slice(dlb8, dlb, (l, 0))
        elif l % 3 == 1:
            du, dgq, dgk, dsk = _sw_bwd(us[l], cs, dp, gq, gk, sinks)
            g_small['sw_qnorm'] = jnp.concatenate([dgq[0].reshape(16, 32).sum(0), dgq[1].reshape(16, 32).sum(0)])
            g_small['sw_knorm'] = jnp.concatenate([dgk[0].reshape(4, 32).sum(0), dgk[1].reshape(4, 32).sum(0)])
            g_small['sw_sinks'] = dsk[0, :16]
        else:
            du, dcw, dalog, ddtb, g_small['gd_onorm'] = _gd_bwd(us[l], *stss[l], dp, cw8, alog, dtb, gd_onorm)
            g_small['gd_conv_w'], g_small['gd_a_log'], g_small['gd_dt_bias'] = dcw[:4], dalog[0, :16], ddtb[0, :16]
        d_in[l] = _mm_tn_acc(hs[l], du, 896 if l == 2 else 512)
        dx, dvec = _inproj_bwd(du, win[l].T, xs[l], dx, norm_g[l:l + 1], scale[l], shift[l])
        dnorm_g[l] = dvec[0:1]
        dmod[l] = jnp.concatenate([dvec[2:3], dvec[1:2], dgate[0:1]], axis=1)
    grad_x = dx[None]

    g_small['hgrn_lb'] = _lb_bwd(h8, dlb8)[0:4]
    g_small['norm_g'] = jnp.concatenate(dnorm_g, axis=0)
    g_small['hg_onorm'] = jnp.concatenate([dgo_hg[0], dgo_hg[1]], axis=0)
    gs_all = _allgather8(_pack_small(g_small))
    gs = _sum_rows([gs_all[32 * d:32 * (d + 1)] for d in range(8)])

    def small_view(packed, k):
        if k == 'hgrn_lb':
            return packed[0:4]
        if k == 'norm_g':
            return packed[4:8]
        off, size = SMALL_ROW[k]
        return packed[24, off:off + size].reshape(w_in[k].shape)

    conv_sl = lambda full: lax.dynamic_slice(full.reshape(4, GD_QKV), (0, chip * D), (4, D))
    out = {}

    def put(k, res, shape):
        for name, r in zip(('grad_', 'delta_', 'new_m_', 'new_v_'), res):
            out[name + k] = r.reshape(shape)

    zero_conv = dict(gd_conv_w=jnp.zeros((4, GD_QKV), F32))
    small_names = ['hgrn_lb', 'norm_g'] + list(SMALL_ROW)
    res = _adamw(_pack_small({**{k: w_in[k] for k in small_names}, **zero_conv}), (gs,),
                 _pack_small({**{k: m_in[k] for k in small_names}, **zero_conv}),
                 _pack_small({**{k: v_in[k] for k in small_names}, **zero_conv}))
    for k in small_names:
        put(k, [small_view(r, k) for r in res], w_in[k].shape)
    put('gd_conv_w', _adamw(gd_conv_w[0], (conv_sl(gs[8:24]),), m_in['gd_conv_w'][0], v_in['gd_conv_w'][0]),
        gd_conv_w.shape)

    dm = _allgather8(jnp.pad(jnp.concatenate(dmod, axis=0), ((0, 4), (0, 0)))).reshape(8, 8, 3 * D)[:, :4]
    dm = dm.transpose(1, 0, 2)
    g_ada_w, g_ada_b = _ada_bwd(c_all, lax.dynamic_slice(dm, (0, 0, chip * ADA_S), (4, 8, ADA_S)), dm)
    put('ada_w', _adamw(ada_w.reshape(4 * D, ADA_S), (g_ada_w.reshape(4 * D, ADA_S),),
                        m_in['ada_w'].reshape(4 * D, ADA_S), v_in['ada_w'].reshape(4 * D, ADA_S)), ada_w.shape)
    put('ada_b', _adamw(ada_b, (g_ada_b.reshape(4, 3 * D),), m_in['ada_b'], v_in['ada_b']), ada_b.shape)

    first_parts = {('hg_in_w', 0): by_chip(d_in[0], D), ('hg_out_w', 0): d_out[0].reshape(4, D // 4, D)}
    first_packed = jnp.stack([_pack_rows([first_parts[s][j] for s in SEG_FIRST]) for j in range(4)])
    first_recv = _chip_scatter(first_packed.astype(BF16))
    own = lambda packed: lax.dynamic_index_in_dim(packed, chip, axis=0, keepdims=False)
    half = jnp.concatenate([_sum_rows([own(first_packed), first_recv[0], first_recv[1], first_recv[2]]),
                            _sum_rows([own(rest_packed), rest_recv[0], rest_recv[1], rest_recv[2]])], axis=0)
    other = _sibling_swap(half)
    pack_all = lambda src: jnp.concatenate([pack_seg(src, SEG_FIRST), pack_seg(src, SEG_REST)], axis=0)
    res = _adamw(pack_all(w_in), (half, other), pack_all(m_in), pack_all(v_in))
    n_first = first_packed.shape[1]
    for name, r in zip(('grad_', 'delta_', 'new_m_', 'new_v_'), res):
        pieces = dict(zip(SEG_FIRST, _unpack_rows(r[:n_first], seg_shapes(SEG_FIRST))))
        pieces.update(zip(SEG_REST, _unpack_rows(r[n_first:], seg_shapes(SEG_REST))))
        for k in BIG:
            out[name + k] = jnp.stack([pieces[(k, i)] for i in range(w_in[k].shape[0])])

    return (loss, grad_x, *[out[p + k] for p in ('grad_', 'delta_', 'new_m_', 'new_v_') for k in WEIGHTS])
```

```python
import functools

import jax
import jax.numpy as jnp
from jax import lax
from jax.experimental import pallas as pl
from jax.experimental.pallas import tpu as pltpu

F32 = jnp.float32
BF16 = jnp.bfloat16
D = 1024
EPS = 1e-6
CHUNK = 64
SUB = 32
HG_H = 8
HG_CPS = 4
HD = 128
VMEM_LIMIT = 56 * 1024 * 1024


def _cparams(sem=None):
    return pltpu.CompilerParams(dimension_semantics=sem, vmem_limit_bytes=VMEM_LIMIT)


def _dot(a, b, ca, cb, prec=None):
    return lax.dot_general(a, b, (((ca,), (cb,)), ((), ())), precision=prec, preferred_element_type=F32)


def _mm(a, b):
    return _dot(a.astype(BF16), b.astype(BF16), 1, 0)


def _mm_nt(a, b):
    return _dot(a.astype(BF16), b.astype(BF16), 1, 1)


def _mm_tn(a, b):
    return _dot(a.astype(BF16), b.astype(BF16), 0, 0)


def _mm_f32(a, b):
    return _dot(a, b, 1, 0, lax.Precision.HIGHEST)


def _silu(x):
    return x * jax.nn.sigmoid(x)


def _cumsum_impl(x):
    row = lax.broadcasted_iota(jnp.int32, x.shape, 0)
    s = 1
    while s < x.shape[0]:
        x = x + jnp.where(row >= s, pltpu.roll(x, s, 0), 0.0)
        s *= 2
    return x


@jax.custom_vjp
def _cumsum_rows(x):
    return _cumsum_impl(x)


_cumsum_rows.defvjp(lambda x: (_cumsum_impl(x), None),
                    lambda _, g: (jnp.sum(g, axis=0, keepdims=True) - _cumsum_impl(g) + g,))


def _roll_rows(x, shift):
    n = x.shape[0]

    @jax.custom_vjp
    def f(a):
        return pltpu.roll(a, shift, 0)

    f.defvjp(lambda a: (pltpu.roll(a, shift, 0), None), lambda _, g: (pltpu.roll(g, n - shift, 0),))
    return f(x)


def _hg_chunk(q_raw, f_pre, v, z, st, lb, go):
    c = q_raw.shape[0]
    nsub = c // SUB
    lf = jnp.log(lb + (1.0 - lb) * jax.nn.sigmoid(f_pre))
    k = (1.0 - lb) * jax.nn.sigmoid(-f_pre)
    q = _silu(q_raw)
    b = _cumsum_rows(lf)
    rowf = lax.broadcasted_iota(jnp.int32, lf.shape, 0)
    bmid = [jnp.sum(jnp.where(rowf == SUB * i + SUB // 2, b, 0.0), axis=0, keepdims=True) for i in range(nsub)]
    row = lax.broadcasted_iota(jnp.int32, (c, 1), 0)
    ref = sum(jnp.where((row >= SUB * i) & (row < SUB * (i + 1)), bmid[i], 0.0) for i in range(nsub))
    qt = q * jnp.exp(b - ref)
    kall = jnp.concatenate(
        [k * jnp.exp(jnp.where(row < SUB * (i + 1), bmid[i] - b, -jnp.inf)) for i in range(nsub)], axis=0)
    v4 = jnp.concatenate([v] * nsub, axis=0)
    b_last = jnp.sum(lf, axis=0, keepdims=True)
    qb = q * jnp.exp(b)
    kd = k * jnp.exp(b_last - b)
    e_last = jnp.exp(b_last)
    tq = lax.broadcasted_iota(jnp.int32, (c, nsub * c), 0)
    cq = lax.broadcasted_iota(jnp.int32, (c, nsub * c), 1)
    m_all = ((cq // c) == (tq // SUB)) & ((cq % c) <= tq)
    hs = lambda a: jnp.split(a, HG_H, axis=1)
    qt_h, kall_h, v4_h, qb_h, kd_h, v_h, z_h, el_h = map(hs, (qt, kall, v4, qb, kd, v, z, e_last))
    st_h = jnp.split(st, HG_H, axis=0)
    heads = range(HG_H)
    pm = [jnp.where(m_all, _mm_nt(qt_h[h], kall_h[h]), 0.0) for h in heads]
    inter = [_mm_nt(qb_h[h], st_h[h]) for h in heads]
    o_h = [_mm(pm[h], v4_h[h]) + inter[h] for h in heads]
    upd = [_mm_tn(v_h[h], kd_h[h]) for h in heads]
    st_out = [el_h[h] * st_h[h] + upd[h] for h in heads]
    y_h = [o_h[h] * lax.rsqrt(jnp.mean(o_h[h] * o_h[h], axis=1, keepdims=True) + EPS) * go for h in heads]
    p_out = [y_h[h] * _silu(z_h[h]) for h in heads]
    return jnp.concatenate(p_out, axis=1), jnp.concatenate(st_out, axis=0)


def _hg_fwd(u, lb, go, gather=None):
    t = u.shape[0]
    cps = HG_CPS if t % (HG_CPS * CHUNK) == 0 else 1
    n = t // (cps * CHUNK)

    def body(u_ref, lb_ref, go_ref, *rest):
        if gather is None:
            p_ref, sts_ref, st_ref = rest
        else:
            shard_ref, p_ref, sts_ref, all_ref, st_ref, *sems = rest
            start, wait = _gather_plan(shard_ref, all_ref, *sems)
            pl.when(pl.program_id(0) == 0)(start)

        @pl.when(pl.program_id(0) == 0)
        def _():
            st_ref[...] = jnp.zeros_like(st_ref)

        st = st_ref[...]
        for j in range(cps):
            rows = slice(j * CHUNK, (j + 1) * CHUNK)
            sts_ref[j] = st
            p, st = _hg_chunk(u_ref[rows, 0:D], u_ref[rows, D:2 * D], u_ref[rows, 2 * D:3 * D], u_ref[rows, 3 * D:4 * D],
                              st, lb_ref[...], go_ref[...])
            p_ref[rows, :] = p.astype(BF16)
        st_ref[...] = st
        if gather is not None:
            pl.when(pl.program_id(0) == n - 1)(wait)

    more = gather is not None
    return pl.pallas_call(
        body, name="hg_fwd_gather" if more else "hg_fwd", grid=(n,),
        in_specs=[pl.BlockSpec((cps * CHUNK, 4 * D), lambda i: (i, 0)),
                  pl.BlockSpec((1, D), lambda i: (0, 0)),
                  pl.BlockSpec((1, HD), lambda i: (0, 0))] + [ANY] * more,
        out_specs=[pl.BlockSpec((cps * CHUNK, D), lambda i: (i, 0)),
                   pl.BlockSpec((cps, HG_H * HD, HD), lambda i: (i, 0, 0))] + [ANY] * more,
        out_shape=[jax.ShapeDtypeStruct((t, D), BF16), jax.ShapeDtypeStruct((t // CHUNK, HG_H * HD, HD), F32)]
        + ([jax.ShapeDtypeStruct((4,) + gather.shape, gather.dtype)] if more else []),
        scratch_shapes=[pltpu.VMEM((HG_H * HD, HD), F32)] + GATHER_SEMS * more,
        compiler_params=_cparams(("arbitrary",)),
    )(u, lb, go, *([gather] * more))


def _hg_bwd(u, sts, dp, lb, go, scatter=None):
    t = u.shape[0]
    cps = HG_CPS if t % (HG_CPS * CHUNK) == 0 else 1
    n = t // (cps * CHUNK)

    def body(u_ref, sts_ref, dp_ref, lb_ref, go_ref, *rest):
        if scatter is None:
            du_ref, dlb_ref, dgo_ref, dst_ref = rest
        else:
            parts_ref, du_ref, dlb_ref, dgo_ref, recv_ref, dst_ref, *sems = rest
            start, wait = _scatter_plan(parts_ref, recv_ref, *sems)
            pl.when(pl.program_id(0) == 0)(start)

        @pl.when(pl.program_id(0) == 0)
        def _():
            dst_ref[...] = jnp.zeros_like(dst_ref)
            dlb_ref[...] = jnp.zeros_like(dlb_ref)
            dgo_ref[...] = jnp.zeros_like(dgo_ref)

        dst = dst_ref[...]
        for j in reversed(range(cps)):
            rows = slice(j * CHUNK, (j + 1) * CHUNK)
            _, vjp = jax.vjp(_hg_chunk, u_ref[rows, 0:D], u_ref[rows, D:2 * D], u_ref[rows, 2 * D:3 * D],
                             u_ref[rows, 3 * D:4 * D], sts_ref[j], lb_ref[...], go_ref[...])
            dq, df, dv, dz, dst, dlb, dgo = vjp((dp_ref[rows, :].astype(F32), dst))
            du_ref[rows, 0:D] = dq.astype(BF16)
            du_ref[rows, D:2 * D] = df.astype(BF16)
            du_ref[rows, 2 * D:3 * D] = dv.astype(BF16)
            du_ref[rows, 3 * D:4 * D] = dz.astype(BF16)
            dlb_ref[...] += dlb
            dgo_ref[...] += dgo
        dst_ref[...] = dst
        if scatter is not None:
            pl.when(pl.program_id(0) == n - 1)(wait)

    rev = lambda i: (n - 1 - i, 0)
    more = scatter is not None
    return pl.pallas_call(
        body, name="hg_bwd_scatter" if more else "hg_bwd", grid=(n,),
        in_specs=[pl.BlockSpec((cps * CHUNK, 4 * D), rev),
                  pl.BlockSpec((cps, HG_H * HD, HD), lambda i: (n - 1 - i, 0, 0)),
                  pl.BlockSpec((cps * CHUNK, D), rev),
                  pl.BlockSpec((1, D), lambda i: (0, 0)),
                  pl.BlockSpec((1, HD), lambda i: (0, 0))] + [ANY] * more,
        out_specs=[pl.BlockSpec((cps * CHUNK, 4 * D), rev),
                   pl.BlockSpec((1, D), lambda i: (0, 0)),
                   pl.BlockSpec((1, HD), lambda i: (0, 0))] + [ANY] * more,
        out_shape=[jax.ShapeDtypeStruct((t, 4 * D), BF16), jax.ShapeDtypeStruct((1, D), F32),
                   jax.ShapeDtypeStruct((1, HD), F32)]
        + ([jax.ShapeDtypeStruct((3,) + scatter.shape[1:], scatter.dtype)] if more else []),
        scratch_shapes=[pltpu.VMEM((HG_H * HD, HD), F32)] + SCATTER_SEMS * more,
        compiler_params=_cparams(("arbitrary",)),
    )(u, sts, dp, lb, go, *([scatter] * more))


GD_VH = 16
GD_QKH = 8
GD_QKV = 4096
GD_VW = 2048
GD_N = GD_QKV + GD_VW + HD
GD_GRP = 4
GD_SOLVE = (GD_VH // GD_GRP, GD_GRP * CHUNK, 2 * HD)
HALO = 8
GD_CPS = 2


def _mm_high(a, b):
    return _dot(a, b, 1, 0, lax.Precision.HIGH)


def _lane_pick(a, h):
    lane = lax.broadcasted_iota(jnp.int32, a.shape, 1)
    return jnp.sum(jnp.where(lane == h, a, 0.0), axis=1, keepdims=True)


def _l2n(x):
    return x * lax.rsqrt(jnp.sum(x * x, axis=1, keepdims=True) + EPS)


def _solve_fwd(a_mats, rhss):
    n = a_mats[0].shape[0]
    r_i, c_i = lax.broadcasted_iota(jnp.int32, (n, n), 0), lax.broadcasted_iota(jnp.int32, (n, n), 1)
    same = lambda nb: (r_i // nb) == (c_i // nb)
    eye = (r_i == c_i).astype(F32)
    d0s = [jnp.where(same(8), a, 0.0) for a in a_mats]
    d2s = [_mm(d, d) for d in d0s]
    tinvs = [eye - d for d in d0s]
    tinvs = [t + _mm(t, d2) for t, d2 in zip(tinvs, d2s)]
    d4s = [_mm(d2, d2) for d2 in d2s]
    tinvs = [t + _mm(t, d4) for t, d4 in zip(tinvs, d4s)]
    nb = 16
    while nb <= CHUNK:
        tls = [_mm(t, jnp.where(same(nb) & ~same(nb // 2), a, 0.0)) for t, a in zip(tinvs, a_mats)]
        tinvs = [t - _mm(tl, t) for t, tl in zip(tinvs, tls)]
        nb *= 2
    return tinvs, [_mm(t, r) for t, r in zip(tinvs, rhss)]


def _solve_bwd(res, dx):
    tinv, x = res
    drhs = _mm_tn(tinv, dx)
    return -_mm_nt(drhs, x), drhs


@jax.custom_vjp
def _solved(a_mat, rhs, tinv, x):
    return x


_solved.defvjp(lambda a_mat, rhs, tinv, x: (x, (tinv, x)),
               lambda res, dx: _solve_bwd(res, dx) + (jnp.zeros_like(res[0]), jnp.zeros_like(res[1])))


def _gd_chunk(xh, x, z, ab, st, cw, alog, dtb, go, solve):
    c = x.shape[0]
    xa = jnp.concatenate([xh, x], axis=0)
    sh = [jnp.split(_roll_rows(xa, 3 - j), [HALO], axis=0)[1] for j in range(3)]
    qkv = _silu(cw[0:1] * sh[0] + cw[1:2] * sh[1] + cw[2:3] * sh[2] + cw[3:4] * x)
    q_all, k_all, v_all = jnp.split(qkv, [1024, 2048], axis=1)
    lane = lax.broadcasted_iota(jnp.int32, (c, HD), 1)
    a_part = jnp.where(lane < GD_VH, ab, 0.0)
    g_all = -jnp.exp(alog) * jax.nn.softplus(a_part + dtb)
    d_all = _cumsum_rows(g_all)
    dl_all = jnp.sum(g_all, axis=0, keepdims=True)
    beta_all = jax.nn.sigmoid(ab)
    gc = GD_GRP * c
    r_i, c_i = lax.broadcasted_iota(jnp.int32, (gc, gc), 0), lax.broadcasted_iota(jnp.int32, (gc, gc), 1)
    same_head = (r_i // c) == (c_i // c)
    tri_g, strict_g = same_head & (c_i <= r_i), same_head & (c_i < r_i)
    qs =jnp.split(q_all, GD_QKH, axis=1)
    ks = jnp.split(k_all, GD_QKH, axis=1)
    vs = jnp.split(v_all, GD_VH, axis=1)
    zs = jnp.split(z, GD_VH, axis=1)
    sts = jnp.split(st, GD_VH, axis=0)
    qn = [_l2n(a) * (HD ** -0.5) for a in qs]
    kn = [_l2n(a) for a in ks]
    p_out, st_out, pre = [], [], []
    for g in range(GD_VH // GD_GRP):
        heads = range(GD_GRP * g, GD_GRP * (g + 1))
        stack = lambda f: jnp.concatenate([f(h) for h in heads], axis=0)
        q_, k_, v_ = stack(lambda h: qn[h // 2]), stack(lambda h: kn[h // 2]), stack(lambda h: vs[h])
        dcol = stack(lambda h: _lane_pick(d_all, h))
        bcol = stack(lambda h: _lane_pick(beta_all, GD_VH + h))
        dlast = stack(lambda h: jnp.broadcast_to(_lane_pick(dl_all, h), (c, 1)))
        drow = jnp.sum(jnp.broadcast_to(dcol, (gc, HD)).T, axis=0, keepdims=True) * (1.0 / HD)
        dec = jnp.exp(jnp.where(tri_g, dcol - drow, -jnp.inf))
        kb = k_ * bcol
        a_mat = jnp.where(strict_g, _mm_nt(kb, k_) * dec, 0.0)
        pre.append((heads, q_, k_, dcol, dlast, dec, a_mat, jnp.concatenate([v_ * bcol, kb * jnp.exp(dcol)], axis=1)))
    xsols = solve([e[6] for e in pre], [e[7] for e in pre])
    heads_of = [e[0] for e in pre]
    per_head = lambda a: jnp.split(a, GD_GRP, axis=0)
    uw = [jnp.split(x, 2, axis=1) for x in xsols]
    ws = [[_mm(wh, sts[h]) for wh, h in zip(per_head(w_), heads)] for (_, w_), heads in zip(uw, heads_of)]
    v_new = [u_ - jnp.concatenate(w, axis=0) for (u_, _), w in zip(uw, ws)]
    qk = [_mm_nt(e[1], e[2]) * e[5] for e in pre]
    qs_ = [[_mm(qh, sts[h]) for qh, h in zip(per_head(e[1] * jnp.exp(e[3])), e[0])] for e in pre]
    o_g = [_mm(a, vn) + jnp.concatenate(b, axis=0) for a, vn, b in zip(qk, v_new, qs_)]
    upd = [[_mm_tn(kh, vh) for kh, vh in zip(per_head(e[2] * jnp.exp(e[4] - e[3])), per_head(vn))]
           for e, vn in zip(pre, v_new)]
    for heads, og, up in zip(heads_of, o_g, upd):
        for h, o, u_st in zip(heads, per_head(og), up):
            st_out.append(sts[h] * jnp.exp(_lane_pick(dl_all, h)) + u_st)
            y = o * lax.rsqrt(jnp.mean(o * o, axis=1, keepdims=True) + EPS) * go
            p_out.append(y * _silu(zs[h]))
    return jnp.concatenate(p_out, axis=1), jnp.concatenate(st_out, axis=0)


def _gd_specs(n, rev):
    ci = (lambda i: n - 1 - i) if rev else (lambda i: i)
    return [pl.BlockSpec((HALO, GD_QKV), lambda i: (jnp.maximum(ci(i) * (CHUNK // HALO) - 1, 0), 0)),
            pl.BlockSpec((CHUNK, GD_N), lambda i: (ci(i), 0))]


def _gd_load(uh_ref, u_ref, first):
    xh = jnp.where(first, 0.0, uh_ref[...])
    return xh, u_ref[:, 0:GD_QKV], u_ref[:, GD_QKV:GD_QKV + GD_VW], u_ref[:, GD_QKV + GD_VW:GD_N]


def _gd_fwd(u, cw, alog, dtb, go):
    t = u.shape[0]
    cps = GD_CPS if t % (GD_CPS * CHUNK) == 0 else 1
    blk = cps * CHUNK
    n = t // blk
    small = lambda r, w: pl.BlockSpec((r, w), lambda i: (0, 0))

    def body(uh_ref, u_ref, cw_ref, alog_ref, dtb_ref, go_ref, p_ref, sts_ref, tinv_ref, xsol_ref, st_ref):
        i = pl.program_id(0)

        @pl.when(i == 0)
        def _():
            st_ref[...] = jnp.zeros_like(st_ref)

        st = st_ref[...]
        for j in range(cps):
            rows = slice(j * CHUNK, (j + 1) * CHUNK)

            def solve(a_mats, rhss, j=j):
                tinvs, xsols = _solve_fwd(a_mats, rhss)
                for g, (tinv, xsol) in enumerate(zip(tinvs, xsols)):
                    tinv_ref[j, g] = tinv
                    xsol_ref[j, g] = xsol
                return xsols

            xh = jnp.where(i == 0, 0.0, uh_ref[...]) if j == 0 else u_ref[j * CHUNK - HALO:j * CHUNK, 0:GD_QKV]
            sts_ref[j] = st
            p, st = _gd_chunk(xh, u_ref[rows, 0:GD_QKV], u_ref[rows, GD_QKV:GD_QKV + GD_VW],
                              u_ref[rows, GD_QKV + GD_VW:GD_N], st, cw_ref[...], alog_ref[...], dtb_ref[...],
                              go_ref[...], solve)
            p_ref[rows, :] = p.astype(BF16)
        st_ref[...] = st

    return pl.pallas_call(
        body, name="gd_fwd", grid=(n,),
        in_specs=[pl.BlockSpec((HALO, GD_QKV), lambda i: (jnp.maximum(i * (blk // HALO) - 1, 0), 0)),
                  pl.BlockSpec((blk, GD_N), lambda i: (i, 0)),
                  small(8, GD_QKV), small(1, HD), small(1, HD), small(1, HD)],
        out_specs=[pl.BlockSpec((blk, GD_VW), lambda i: (i, 0)),
                   pl.BlockSpec((cps, GD_VH * HD, HD), lambda i: (i, 0, 0)),
                   pl.BlockSpec((cps,) + GD_SOLVE, lambda i: (i, 0, 0, 0)),
                   pl.BlockSpec((cps,) + GD_SOLVE, lambda i: (i, 0, 0, 0))],
        out_shape=[jax.ShapeDtypeStruct((t, GD_VW), BF16), jax.ShapeDtypeStruct((t // CHUNK, GD_VH * HD, HD), F32),
                   jax.ShapeDtypeStruct((t // CHUNK,) + GD_SOLVE, F32),
                   jax.ShapeDtypeStruct((t // CHUNK,) + GD_SOLVE, F32)],
        scratch_shapes=[pltpu.VMEM((GD_VH * HD, HD), F32)],
        compiler_params=_cparams(("arbitrary",)),
    )(u, u, cw, alog, dtb, go)


def _gd_bwd(u, sts, tinvs, xsols, dp, cw, alog, dtb, go):
    t = u.shape[0]
    n = t // CHUNK
    small = lambda r, w: pl.BlockSpec((r, w), lambda i: (0, 0))

    def body(uh_ref, u_ref, sts_ref, tinv_ref, xsol_ref, dp_ref, cw_ref, alog_ref, dtb_ref, go_ref,
             du_ref, dcw_ref, dalog_ref, ddtb_ref, dgo_ref, dst_ref, dhalo_ref):
        i = pl.program_id(0)

        @pl.when(i == 0)
        def _():
            for r in (dst_ref, dhalo_ref, dcw_ref, dalog_ref, ddtb_ref, dgo_ref):
                r[...] = jnp.zeros_like(r)

        solve = lambda a_mats, rhss: [_solved(a, r, tinv_ref[0, g], xsol_ref[0, g])
                                      for g, (a, r) in enumerate(zip(a_mats, rhss))]
        chunk = functools.partial(_gd_chunk, solve=solve)
        _, vjp = jax.vjp(chunk, *_gd_load(uh_ref, u_ref, i == n - 1), sts_ref[0], cw_ref[...], alog_ref[...],
                         dtb_ref[...], go_ref[...])
        dxh, dx, dz, dab, dst, dcw, dalog, ddtb, dgo = vjp((dp_ref[...].astype(F32), dst_ref[...]))
        tail = jnp.concatenate([jnp.zeros((CHUNK - HALO, GD_QKV), F32), dhalo_ref[...]], axis=0)
        du_ref[:, 0:GD_QKV] = (dx + tail).astype(BF16)
        du_ref[:, GD_QKV:GD_QKV + GD_VW] = dz.astype(BF16)
        du_ref[:, GD_QKV + GD_VW:GD_N] = dab.astype(BF16)
        dhalo_ref[...] = dxh
        dst_ref[...] = dst
        dcw_ref[...] += dcw
        dalog_ref[...] += dalog
        ddtb_ref[...] += ddtb
        dgo_ref[...] += dgo

    return pl.pallas_call(
        body, name="gd_bwd", grid=(n,),
        in_specs=_gd_specs(n, True) + [pl.BlockSpec((1, GD_VH * HD, HD), lambda i: (n - 1 - i, 0, 0)),
                                       pl.BlockSpec((1,) + GD_SOLVE, lambda i: (n - 1 - i, 0, 0, 0)),
                                       pl.BlockSpec((1,) + GD_SOLVE, lambda i: (n - 1 - i, 0, 0, 0)),
                                       pl.BlockSpec((CHUNK, GD_VW), lambda i: (n - 1 - i, 0)),
                                       small(8, GD_QKV), small(1, HD), small(1, HD), small(1, HD)],
        out_specs=[pl.BlockSpec((CHUNK, GD_N), lambda i: (n - 1 - i, 0)),
                   small(8, GD_QKV), small(1, HD), small(1, HD), small(1, HD)],
        out_shape=[jax.ShapeDtypeStruct((t, GD_N), BF16), jax.ShapeDtypeStruct((8, GD_QKV), F32)]
        + [jax.ShapeDtypeStruct((1, HD), F32)] * 3,
        scratch_shapes=[pltpu.VMEM((GD_VH * HD, HD), F32), pltpu.VMEM((HALO, GD_QKV), F32)],
        compiler_params=_cparams(("arbitrary",)),
    )(u, u, sts, tinvs, xsols, dp, cw, alog, dtb, go)


SW_B = 128
SW_G = 4
SW_N = 2560
SW_KV0 = 1024


def _blockdiag(n, blk):
    r = lax.broadcasted_iota(jnp.int32, (n, n), 0) // blk
    c = lax.broadcasted_iota(jnp.int32, (n, n), 1) // blk
    return (r == c).astype(F32)


def _sw_normrope(x, g1, g2, cos, sin):
    w = x.shape[1] // 2
    x1, x2 = jnp.split(x, 2, axis=1)
    ms = _mm_high(x1 * x1 + x2 * x2, _blockdiag(w, 32)) * (1.0 / 64.0)
    rinv = lax.rsqrt(ms + EPS)
    n1, n2 = x1 * rinv * g1, x2 * rinv * g2
    return jnp.concatenate([n1 * cos - n2 * sin, n2 * cos + n1 * sin], axis=1)


def _sw_block(q, kvp, kvc, z, csp, csc, gq, gk, sinks, has_prev):
    b = q.shape[0]
    cos_c, sin_c = jnp.split(csc, 2, axis=1)
    cos_p, sin_p = jnp.split(csp, 2, axis=1)
    tile4 = lambda a: jnp.concatenate([a] * 4, axis=1)
    qh = _sw_normrope(q, gq[0:1], gq[1:2], tile4(cos_c), tile4(sin_c))
    kp, vp = jnp.split(kvp, 2, axis=1)
    kc, vc = jnp.split(kvc, 2, axis=1)
    kh = jnp.concatenate([_sw_normrope(kp, gk[0:1], gk[1:2], cos_p, sin_p),
                          _sw_normrope(kc, gk[0:1], gk[1:2], cos_c, sin_c)], axis=0)
    vv = jnp.concatenate([vp, vc], axis=0)
    q1, q2 = jnp.split(qh, 2, axis=1)
    q1g, q2g = jnp.split(q1, SW_G, axis=1), jnp.split(q2, SW_G, axis=1)
    own = lax.broadcasted_iota(jnp.int32, (4 * b, b), 1) <= lax.broadcasted_iota(jnp.int32, (4 * b, b), 0) % b
    ri = lax.broadcasted_iota(jnp.int32, (256, 256), 0)
    ci = lax.broadcasted_iota(jnp.int32, (256, 256), 1)
    row_head = lax.broadcasted_iota(jnp.int32, (4 * b, 256), 0) // b
    lane_q = lax.broadcasted_iota(jnp.int32, (4 * b, 256), 1)
    q_sel = (lane_q % 128) // 32 == row_head
    o_sel = lane_q // 64 == row_head
    groups = range(SW_G)
    ek = [((ri // 128 == ci // 128) & ((ri % 128) // 32 == g) & (ri % 32 == ci % 32)).astype(F32) for g in groups]
    ev = [((ri // 64 == g) & (ri % 64 == ci % 64)).astype(F32) for g in groups]
    kx = [_mm(kh, ek[g]) for g in groups]
    vx = [_mm(vv, ev[g]) for g in groups]
    q4 = [jnp.where(q_sel, jnp.concatenate([jnp.concatenate([q1g[g], q2g[g]], axis=1)] * 4, axis=0), 0.0)
          for g in groups]
    sink = [jnp.concatenate([jnp.broadcast_to(_lane_pick(sinks, 4 * g + j), (b, 1)) for j in range(4)], axis=0)
            for g in groups]
    sc = [jnp.split(_mm_nt(q4[g], kx[g]) * (64 ** -0.5), 2, axis=1) for g in groups]
    s = [jnp.where(own, sc[g][1], jnp.where(has_prev, sc[g][0], -jnp.inf)) for g in groups]
    top = [jnp.max(s[g]) for g in groups]
    soft = [top[g] + 8.0 * jnp.log(jnp.sum(jnp.exp((s[g] - top[g]) * 0.125), axis=1, keepdims=True)) for g in groups]
    m = [lax.stop_gradient(jnp.maximum(soft[g], sink[g])) for g in groups]
    p = [jnp.exp(s[g] - m[g]) for g in groups]
    pn = [p[g] / (jnp.sum(p[g], axis=1, keepdims=True) + jnp.exp(sink[g] - m[g])) for g in groups]
    pn2 = [jnp.concatenate([jnp.where(own, 0.0, pn[g]), jnp.where(own, pn[g], 0.0)], axis=1) for g in groups]
    o4 = [jnp.split(jnp.where(o_sel, _mm(pn2[g], vx[g]), 0.0), 4, axis=0) for g in groups]
    o_out = [o4[g][0] + o4[g][1] + o4[g][2] + o4[g][3] for g in groups]
    return jnp.concatenate(o_out, axis=1) * _silu(z)


def _sw_specs(n, rev):
    ci = (lambda i: n - 1 - i) if rev else (lambda i: i)
    prev = lambda i: jnp.maximum(ci(i) - 1, 0)
    return [pl.BlockSpec((SW_B, SW_N), lambda i: (ci(i), 0)),
            pl.BlockSpec((SW_B, 512), lambda i: (prev(i), SW_KV0 // 512)),
            pl.BlockSpec((SW_B, 256), lambda i: (ci(i), 0)),
            pl.BlockSpec((SW_B, 256), lambda i: (prev(i), 0)),
            pl.BlockSpec((2, 512), lambda i: (0, 0)), pl.BlockSpec((2, 128), lambda i: (0, 0)),
            pl.BlockSpec((1, 128), lambda i: (0, 0))]


def _sw_args(u_ref, kvp_ref, csc_ref, csp_ref, gq_ref, gk_ref, sk_ref, has_prev):
    return (u_ref[:, 0:D], kvp_ref[...], u_ref[:, SW_KV0:SW_KV0 + 512], u_ref[:, SW_KV0 + 512:SW_N],
            csp_ref[...], csc_ref[...], gq_ref[...], gk_ref[...], sk_ref[...], has_prev)


def _sw_fwd(u, cs, gq, gk, sinks):
    t = u.shape[0]
    n = t // SW_B

    def body(u_ref, kvp_ref, csc_ref, csp_ref, gq_ref, gk_ref, sk_ref, p_ref):
        has_prev = pl.program_id(0) > 0
        p_ref[...] = _sw_block(*_sw_args(u_ref, kvp_ref, csc_ref, csp_ref, gq_ref, gk_ref, sk_ref, has_prev)
                               ).astype(BF16)

    return pl.pallas_call(
        body, name="sw_fwd", grid=(n,), in_specs=_sw_specs(n, False),
        out_specs=pl.BlockSpec((SW_B, D), lambda i: (i, 0)),
        out_shape=jax.ShapeDtypeStruct((t, D), BF16),
        compiler_params=_cparams(("arbitrary",)),
    )(u, u, cs, cs, gq, gk, sinks)


def _sw_bwd(u, cs, dp, gq, gk, sinks):
    t = u.shape[0]
    n = t // SW_B

    def body(u_ref, kvp_ref, csc_ref, csp_ref, gq_ref, gk_ref, sk_ref, dp_ref,
             du_ref, dgq_ref, dgk_ref, dsk_ref, dkv_ref):
        i = pl.program_id(0)

        @pl.when(i == 0)
        def _():
            for r in (dkv_ref, dgq_ref, dgk_ref, dsk_ref):
                r[...] = jnp.zeros_like(r)

        has_prev = i < n - 1
        args = _sw_args(u_ref, kvp_ref, csc_ref, csp_ref, gq_ref, gk_ref, sk_ref, has_prev)
        fn = lambda q, kvp, kvc, z, gq_, gk_, sk_: _sw_block(q, kvp, kvc, z, args[4], args[5], gq_, gk_, sk_, has_prev)
        _, vjp = jax.vjp(fn, args[0], args[1], args[2], args[3], args[6], args[7], args[8])
        dq, dkvp, dkvc, dz, dgq, dgk, dsk = vjp(dp_ref[...].astype(F32))
        du_ref[:, 0:D] = dq.astype(BF16)
        du_ref[:, SW_KV0:SW_KV0 + 512] = (dkvc + dkv_ref[...]).astype(BF16)
        du_ref[:, SW_KV0 + 512:SW_N] = dz.astype(BF16)
        dkv_ref[...] = dkvp
        dgq_ref[...] += dgq
        dgk_ref[...] += dgk
        dsk_ref[...] += dsk

    small = lambda r, w: pl.BlockSpec((r, w), lambda i: (0, 0))
    return pl.pallas_call(
        body, name="sw_bwd", grid=(n,),
        in_specs=_sw_specs(n, True) + [pl.BlockSpec((SW_B, D), lambda i: (n - 1 - i, 0))],
        out_specs=[pl.BlockSpec((SW_B, SW_N), lambda i: (n - 1 - i, 0)), small(2, 512), small(2, 128), small(1, 128)],
        out_shape=[jax.ShapeDtypeStruct((t, SW_N), BF16), jax.ShapeDtypeStruct((2, 512), F32),
                   jax.ShapeDtypeStruct((2, 128), F32), jax.ShapeDtypeStruct((1, 128), F32)],
        scratch_shapes=[pltpu.VMEM((SW_B, 512), F32)],
        compiler_params=_cparams(("arbitrary",)),
    )(u, u, cs, cs, gq, gk, sinks, dp)


def _ln_mod(x, g, scale, shift):
    y = x * lax.rsqrt(jnp.mean(x * x, axis=1, keepdims=True) + EPS) * g
    return y * (1.0 + scale) + shift


def _row_tile(t):
    return min(t, 1024)


def _ln_mm(x, g, scale, shift, w, tn):
    t, n = x.shape[0], w.shape[1]
    tm = _row_tile(t)
    vec = pl.BlockSpec((1, D), lambda i, j: (0, 0))

    def body(x_ref, g_ref, sc_ref, sh_ref, w_ref, u_ref, h_ref):
        @pl.when(pl.program_id(1) == 0)
        def _():
            h_ref[...] = _ln_mod(x_ref[...], g_ref[...], sc_ref[...], sh_ref[...]).astype(BF16)

        u_ref[...] = _dot(h_ref[...], w_ref[...], 1, 0)

    return pl.pallas_call(
        body, name="ln_mm", grid=(t // tm, n // tn),
        in_specs=[pl.BlockSpec((tm, D), lambda i, j: (i, 0)), vec, vec, vec,
                  pl.BlockSpec((D, tn), lambda i, j: (0, j))],
        out_specs=[pl.BlockSpec((tm, tn), lambda i, j: (i, j)), pl.BlockSpec((tm, D), lambda i, j: (i, 0))],
        out_shape=[jax.ShapeDtypeStruct((t, n), F32), jax.ShapeDtypeStruct((t, D), BF16)],
        compiler_params=_cparams(("arbitrary", "arbitrary")),
    )(x, g, scale, shift, w)


def _mm_res(p, w, x, gate):
    t, k = p.shape
    tm = _row_tile(t)

    def body(p_ref, w_ref, x_ref, gate_ref, o_ref):
        o_ref[...] = x_ref[...] + gate_ref[...] * _dot(p_ref[...], w_ref[...], 1, 0)

    return pl.pallas_call(
        body, name="mm_res", grid=(t // tm,),
        in_specs=[pl.BlockSpec((tm, k), lambda i: (i, 0)), pl.BlockSpec((k, D), lambda i: (0, 0)),
                  pl.BlockSpec((tm, D), lambda i: (i, 0)), pl.BlockSpec((1, D), lambda i: (0, 0))],
        out_specs=pl.BlockSpec((tm, D), lambda i: (i, 0)),
        out_shape=jax.ShapeDtypeStruct((t, D), F32),
        compiler_params=_cparams(("arbitrary",)),
    )(p, w, x, gate)


def _loss_grad(x, target):
    t = x.shape[0]
    tm = _row_tile(t)

    def body(x_ref, t_ref, l_ref, dx_ref):
        @pl.when(pl.program_id(0) == 0)
        def _():
            l_ref[...] = jnp.zeros_like(l_ref)

        err = x_ref[...] - t_ref[...]
        dx_ref[...] = err * (1.0 / D)
        l_ref[...] += 0.5 * jnp.sum(jnp.mean(err * err, axis=1, keepdims=True), axis=0, keepdims=True)

    return pl.pallas_call(
        body, name="loss_grad", grid=(t // tm,),
        in_specs=[pl.BlockSpec((tm, D), lambda i: (i, 0))] * 2,
        out_specs=[pl.BlockSpec((8, 128), lambda i: (0, 0)), pl.BlockSpec((tm, D), lambda i: (i, 0))],
        out_shape=[jax.ShapeDtypeStruct((8, 128), F32), jax.ShapeDtypeStruct((t, D), F32)],
        compiler_params=_cparams(("arbitrary",)),
    )(x, target)


def _mm_scaled(a, s, w, tn):
    t, k = a.shape
    n = w.shape[1]
    tm = _row_tile(t)

    def body(a_ref, s_ref, w_ref, o_ref):
        o_ref[...] = _dot((a_ref[...] * s_ref[...]).astype(BF16), w_ref[...], 1, 0).astype(BF16)

    return pl.pallas_call(
        body, name="mm_scaled", grid=(t // tm, n // tn),
        in_specs=[pl.BlockSpec((tm, k), lambda i, j: (i, 0)), pl.BlockSpec((1, k), lambda i, j: (0, 0)),
                  pl.BlockSpec((k, tn), lambda i, j: (0, j))],
        out_specs=pl.BlockSpec((tm, tn), lambda i, j: (i, j)),
        out_shape=jax.ShapeDtypeStruct((t, n), BF16),
        compiler_params=_cparams(("arbitrary", "arbitrary")),
    )(a, s, w)


def _mm_tn_acc(a, b, tn):
    t, m = a.shape
    n = b.shape[1]
    fits = lambda k: 2 * k * (m * a.dtype.itemsize + tn * b.dtype.itemsize) + 2 * m * tn * 4 <= 36 * 1024 * 1024
    tk = next(k for k in (4096, 2048, 1024, 512, t) if t % k == 0 and (fits(k) or k <= 512))
    nk = t // tk

    def body(a_ref, b_ref, o_ref):
        @pl.when(pl.program_id(1) == 0)
        def _():
            o_ref[...] = jnp.zeros_like(o_ref)

        o_ref[...] += _dot(a_ref[...], b_ref[...].astype(BF16), 0, 0)

    return pl.pallas_call(
        body, name="mm_tn_acc", grid=(n // tn, nk),
        in_specs=[pl.BlockSpec((tk, m), lambda j, k: (k, 0)), pl.BlockSpec((tk, tn), lambda j, k: (k, j))],
        out_specs=pl.BlockSpec((m, tn), lambda j, k: (0, j)),
        out_shape=jax.ShapeDtypeStruct((m, n), F32),
        compiler_params=_cparams(("arbitrary", "arbitrary")),
    )(a, b)


def _inproj_bwd(du, wt, x, dxp, g, scale, shift):
    t, kdim = du.shape
    tk = kdim
    tm = min(t, 512 if kdim <= 4096 else 256)
    nk = kdim // tk
    vec = pl.BlockSpec((1, D), lambda i, k: (0, 0))

    def body(du_ref, wt_ref, x_ref, dxp_ref, g_ref, sc_ref, sh_ref, dx_ref, dv_ref, acc_ref):
        k = pl.program_id(1)

        @pl.when((pl.program_id(0) == 0) & (k == 0))
        def _():
            dv_ref[...] = jnp.zeros_like(dv_ref)

        @pl.when(k == 0)
        def _():
            acc_ref[...] = jnp.zeros_like(acc_ref)

        acc_ref[...] += _dot(du_ref[...].astype(BF16), wt_ref[...], 1, 0)

        @pl.when(k == nk - 1)
        def _():
            _, vjp = jax.vjp(_ln_mod, x_ref[...], g_ref[...], sc_ref[...], sh_ref[...])
            dx, dg, dsc, dsh = vjp(acc_ref[...])
            dx_ref[...] = dxp_ref[...] + dx
            dv_ref[0:1, :] += dg
            dv_ref[1:2, :] += dsc
            dv_ref[2:3, :] += dsh

    return pl.pallas_call(
        body, name="inproj_bwd", grid=(t // tm, nk),
        in_specs=[pl.BlockSpec((tm, tk), lambda i, k: (i, k)), pl.BlockSpec((tk, D), lambda i, k: (k, 0)),
                  pl.BlockSpec((tm, D), lambda i, k: (i, 0)), pl.BlockSpec((tm, D), lambda i, k: (i, 0)),
                  vec, vec, vec],
        out_specs=[pl.BlockSpec((tm, D), lambda i, k: (i, 0)), pl.BlockSpec((8, D), lambda i, k: (0, 0))],
        out_shape=[jax.ShapeDtypeStruct((t, D), F32), jax.ShapeDtypeStruct((8, D), F32)],
        scratch_shapes=[pltpu.VMEM((tm, D), F32)],
        compiler_params=_cparams(("arbitrary", "arbitrary")),
    )(du, wt, x, dxp, g, scale, shift)


def _outgrad(gmat, w, gate):
    k = gmat.shape[0]
    tr = 256

    def body(g_ref, w_ref, gate_ref, dw_ref, dg_ref):
        @pl.when(pl.program_id(0) == 0)
        def _():
            dg_ref[...] = jnp.zeros_like(dg_ref)

        gm = g_ref[...]
        dw_ref[...] = gm * gate_ref[...]
        dg_ref[0:1, :] += jnp.sum(gm * w_ref[...].astype(F32), axis=0, keepdims=True)

    return pl.pallas_call(
        body, name="outgrad", grid=(k // tr,),
        in_specs=[pl.BlockSpec((tr, D), lambda i: (i, 0)), pl.BlockSpec((tr, D), lambda i: (i, 0)),
                  pl.BlockSpec((1, D), lambda i: (0, 0))],
        out_specs=[pl.BlockSpec((tr, D), lambda i: (i, 0)), pl.BlockSpec((8, D), lambda i: (0, 0))],
        out_shape=[jax.ShapeDtypeStruct((k, D), F32), jax.ShapeDtypeStruct((8, D), F32)],
        compiler_params=_cparams(("arbitrary",)),
    )(gmat, w, gate)


def _rope_table(pos, freq):
    t = pos.shape[0]
    tm = _row_tile(t)

    def body(p_ref, f_ref, o_ref):
        ang = p_ref[...].astype(F32) * f_ref[...]
        o_ref[:, 0:128] = jnp.cos(ang)
        o_ref[:, 128:256] = jnp.sin(ang)

    return pl.pallas_call(
        body, name="rope_table", grid=(t // tm,),
        in_specs=[pl.BlockSpec((tm, 1), lambda i: (i, 0)), pl.BlockSpec((1, 128), lambda i: (0, 0))],
        out_specs=pl.BlockSpec((tm, 256), lambda i: (i, 0)),
        out_shape=jax.ShapeDtypeStruct((t, 256), F32),
        compiler_params=_cparams(("arbitrary",)),
    )(pos, freq)


def _ada_fwd(c_all, w, b):
    nl, _, s = w.shape

    def body(c_ref, w_ref, b_ref, o_ref):
        o_ref[0] = _mm_f32(c_ref[...], w_ref[0]) + b_ref[0]

    return pl.pallas_call(
        body, name="ada_fwd", grid=(nl,),
        in_specs=[pl.BlockSpec((8, D), lambda l: (0, 0)), pl.BlockSpec((1, D, s), lambda l: (l, 0, 0)),
                  pl.BlockSpec((1, 1, s), lambda l: (l, 0, 0))],
        out_specs=pl.BlockSpec((1, 8, s), lambda l: (l, 0, 0)),
        out_shape=jax.ShapeDtypeStruct((nl, 8, s), F32),
        compiler_params=_cparams(("arbitrary",)),
    )(c_all, w, b)


def _ada_bwd(c_all, dmod_cols, dmod_all):
    nl, _, s = dmod_cols.shape

    def body(c_ref, dc_ref, da_ref, gw_ref, gb_ref):
        gw_ref[0] = _dot(c_ref[...], dc_ref[0], 0, 0, lax.Precision.HIGHEST)
        gb_ref[0] = jnp.sum(da_ref[0], axis=0, keepdims=True)

    return pl.pallas_call(
        body, name="ada_bwd", grid=(nl,),
        in_specs=[pl.BlockSpec((8, D), lambda l: (0, 0)), pl.BlockSpec((1, 8, s), lambda l: (l, 0, 0)),
                  pl.BlockSpec((1, 8, 3 * D), lambda l: (l, 0, 0))],
        out_specs=[pl.BlockSpec((1, D, s), lambda l: (l, 0, 0)), pl.BlockSpec((1, 1, 3 * D), lambda l: (l, 0, 0))],
        out_shape=[jax.ShapeDtypeStruct((nl, D, s), F32), jax.ShapeDtypeStruct((nl, 1, 3 * D), F32)],
        compiler_params=_cparams(("arbitrary",)),
    )(c_all, dmod_cols, dmod_all)


def _lb_fn(h8):
    sm = jax.nn.softmax(h8, axis=0)
    r = lax.broadcasted_iota(jnp.int32, (8, 8), 0)
    c = lax.broadcasted_iota(jnp.int32, (8, 8), 1)
    return _mm_f32(((c >= 1) & (c <= r)).astype(F32), sm)


def _lb_fwd(h8):
    def body(h_ref, o_ref):
        o_ref[...] = _lb_fn(h_ref[...])

    return pl.pallas_call(body, name="lb_fwd", out_shape=jax.ShapeDtypeStruct((8, D), F32))(h8)


def _lb_bwd(h8, dlb8):
    def body(h_ref, d_ref, o_ref):
        _, vjp = jax.vjp(_lb_fn, h_ref[...])
        o_ref[...] = vjp(d_ref[...])[0]

    return pl.pallas_call(body, name="lb_bwd", out_shape=jax.ShapeDtypeStruct((8, D), F32))(h8, dlb8)


ADAM_LR, ADAM_B1, ADAM_B2, ADAM_EPS, ADAM_WD, ADAM_STEP = 0.001, 0.9, 0.999, 1e-08, 0.01, 10


def _adamw(w, gparts, m, v):
    r, c = w.shape
    tr = r if r * c * 4 <= (1 << 20) else max(8, ((1 << 20) // (c * 4)) // 8 * 8)
    while r % tr:
        tr -= 8
    ng = len(gparts)

    def body(*refs):
        w_ref, m_ref, v_ref = refs[0], refs[1 + ng], refs[2 + ng]
        g_ref, d_ref, nm_ref, nv_ref = refs[3 + ng:]
        g = refs[1][...]
        for gr in refs[2:1 + ng]:
            g = g + gr[...]
        mm = ADAM_B1 * m_ref[...] + (1.0 - ADAM_B1) * g
        vv = ADAM_B2 * v_ref[...] + (1.0 - ADAM_B2) * (g * g)
        m_hat = mm / (1.0 - ADAM_B1 ** ADAM_STEP)
        v_hat = vv / (1.0 - ADAM_B2 ** ADAM_STEP)
        g_ref[...] = g
        d_ref[...] = -ADAM_LR * (m_hat / (jnp.sqrt(v_hat) + ADAM_EPS) + ADAM_WD * w_ref[...])
        nm_ref[...] = mm
        nv_ref[...] = vv

    spec = pl.BlockSpec((tr, c), lambda i: (i, 0))
    return pl.pallas_call(
        body, name="adamw", grid=(r // tr,), in_specs=[spec] * (3 + ng), out_specs=[spec] * 4,
        out_shape=[jax.ShapeDtypeStruct((r, c), F32)] * 4,
        compiler_params=_cparams(("arbitrary",)),
    )(w, *gparts, m, v)


def _sum_rows(parts):
    r, c = parts[0].shape
    tr = 8
    for cand in range(min(r, 512), 7, -8):
        if r % cand == 0:
            tr = cand
            break

    def body(*refs):
        acc = refs[0][...]
        for p in refs[1:-1]:
            acc = acc + p[...]
        refs[-1][...] = acc

    spec = pl.BlockSpec((tr, c), lambda i: (i, 0))
    return pl.pallas_call(
        body, name="sum_rows", grid=(r // tr,), in_specs=[spec] * len(parts), out_specs=spec,
        out_shape=jax.ShapeDtypeStruct((r, c), F32),
        compiler_params=_cparams(("arbitrary",)),
    )(*parts)


MESH = pl.DeviceIdType.MESH
ANY = pl.BlockSpec(memory_space=pl.ANY)


def _place():
    return lax.axis_index("x"), lax.axis_index("y"), lax.axis_index("c")


def _allgather8(blk):
    m_per, n = blk.shape

    def body(x_ref, out_ref, send_sems, recv_sems, local_sem):
        x, y, c = _place()
        me, sibling = (x, y, c), (x, y, 1 - c)
        chips = [(1 - x, y), (x, 1 - y), (1 - x, 1 - y)]

        def rows(px, py, pc):
            return out_ref.at[pl.ds((4 * px + 2 * py + pc) * m_per, m_per), :]

        def copy(k, block, to, src=None):
            return pltpu.make_async_remote_copy(
                src_ref=rows(*block) if src is None else src, dst_ref=rows(*block),
                send_sem=send_sems.at[k], recv_sem=recv_sems.at[k], device_id=to, device_id_type=MESH)

        mine = pltpu.make_async_copy(x_ref, rows(*me), local_sem)
        mine.start()
        first = [copy(0, me, sibling, src=x_ref)]
        first += [copy(1 + j, me, (*chip, c), src=x_ref) for j, chip in enumerate(chips)]
        for cp in first:
            cp.start()
        passed = [copy(4 + j, (*chip, c), sibling) for j, chip in enumerate(chips)]
        for j, chip in enumerate(chips):
            copy(1 + j, (*chip, c), me).wait_recv()
            passed[j].start()
        copy(0, sibling, me).wait_recv()
        for j, chip in enumerate(chips):
            copy(4 + j, (*chip, 1 - c), me).wait_recv()
        for cp in first + passed:
            cp.wait_send()
        mine.wait()

    return pl.pallas_call(
        body, name="allgather8",
        out_shape=jax.ShapeDtypeStruct((8 * m_per, n), blk.dtype),
        in_specs=[pl.BlockSpec(memory_space=pltpu.VMEM)],
        out_specs=pl.BlockSpec(memory_space=pltpu.VMEM),
        scratch_shapes=[pltpu.SemaphoreType.DMA((7,)), pltpu.SemaphoreType.DMA((7,)), pltpu.SemaphoreType.DMA],
    )(blk)


def _chip_peers():
    x, y, c = _place()
    return [(1 - x, y, c), (x, 1 - y, c), (1 - x, 1 - y, c)]


GATHER_SEMS = [pltpu.SemaphoreType.DMA((3,)), pltpu.SemaphoreType.DMA((3,)), pltpu.SemaphoreType.DMA]
SCATTER_SEMS = [pltpu.SemaphoreType.DMA((3,)), pltpu.SemaphoreType.DMA((3,))]


def _gather_plan(x_ref, out_ref, send_sems, recv_sems, local_sem):
    x, y, _ = _place()
    peers = _chip_peers()

    def copy(j, chip_index):
        return pltpu.make_async_remote_copy(
            src_ref=x_ref, dst_ref=out_ref.at[chip_index], send_sem=send_sems.at[j], recv_sem=recv_sems.at[j],
            device_id=peers[j], device_id_type=MESH)

    mine = pltpu.make_async_copy(x_ref, out_ref.at[2 * x + y], local_sem)
    sends = [copy(j, 2 * x + y) for j in range(3)]

    def start():
        mine.start()
        for cp in sends:
            cp.start()

    def wait():
        for j in range(3):
            copy(j, 2 * peers[j][0] + peers[j][1]).wait_recv()
        for cp in sends:
            cp.wait_send()
        mine.wait()

    return start, wait


def _scatter_plan(p_ref, out_ref, send_sems, recv_sems):
    peers = _chip_peers()
    sends = [pltpu.make_async_remote_copy(
        src_ref=p_ref.at[2 * peers[j][0] + peers[j][1]], dst_ref=out_ref.at[j], send_sem=send_sems.at[j],
        recv_sem=recv_sems.at[j], device_id=peers[j], device_id_type=MESH) for j in range(3)]

    def start():
        for cp in sends:
            cp.start()

    def wait():
        for cp in sends:
            cp.wait_recv()
        for cp in sends:
            cp.wait_send()

    return start, wait


def _chip_allgather(shard):
    def body(x_ref, out_ref, *sems):
        start, wait = _gather_plan(x_ref, out_ref, *sems)
        start()
        wait()

    return pl.pallas_call(
        body, name="chip_allgather", out_shape=jax.ShapeDtypeStruct((4,) + shard.shape, shard.dtype),
        in_specs=[ANY], out_specs=ANY, scratch_shapes=GATHER_SEMS,
    )(shard)


def _chip_scatter(parts):
    def body(p_ref, out_ref, *sems):
        start, wait = _scatter_plan(p_ref, out_ref, *sems)
        start()
        wait()

    return pl.pallas_call(
        body, name="chip_scatter", out_shape=jax.ShapeDtypeStruct((3,) + parts.shape[1:], parts.dtype),
        in_specs=[ANY], out_specs=ANY, scratch_shapes=SCATTER_SEMS,
    )(parts)


def _sibling_swap(a):
    def body(a_ref, out_ref, send_sem, recv_sem):
        x, y, c = _place()
        cp = pltpu.make_async_remote_copy(src_ref=a_ref, dst_ref=out_ref, send_sem=send_sem, recv_sem=recv_sem,
                                          device_id=(x, y, 1 - c), device_id_type=MESH)
        cp.start()
        cp.wait_recv()
        cp.wait_send()

    return pl.pallas_call(
        body, name="sibling_swap", out_shape=jax.ShapeDtypeStruct(a.shape, a.dtype),
        in_specs=[ANY], out_specs=ANY,
        scratch_shapes=[pltpu.SemaphoreType.DMA, pltpu.SemaphoreType.DMA],
    )(a)


WEIGHTS = ['hgrn_lb', 'ada_w', 'ada_b', 'norm_g', 'hg_in_w', 'hg_out_w', 'hg_onorm', 'sw_in_w', 'sw_out_w', 'sw_qnorm',
           'sw_knorm', 'sw_sinks', 'gd_in_w', 'gd_out_w', 'gd_conv_w', 'gd_a_log', 'gd_dt_bias', 'gd_onorm']
BIG = ['hg_in_w', 'hg_out_w', 'sw_in_w', 'sw_out_w', 'gd_in_w', 'gd_out_w']
SEG_FIRST = [('hg_in_w', 0), ('hg_out_w', 0)]
SEG_REST = [('hg_in_w', 1), ('hg_out_w', 1), ('sw_in_w', 0), ('sw_out_w', 0), ('gd_in_w', 0), ('gd_out_w', 0)]
PACK_ALIGN = 16
ROPE_THETA = 10000.0
ADA_S = 3 * D // 4
SMALL_ROW = {'hg_onorm': (0, 256), 'sw_qnorm': (256, 64), 'sw_knorm': (320, 64), 'sw_sinks': (384, 16),
             'gd_a_log': (400, 16), 'gd_dt_bias': (416, 16), 'gd_onorm': (432, 128)}


def _pack_rows(arrs):
    flat = jnp.concatenate([a.reshape(-1, D) for a in arrs], axis=0)
    return jnp.pad(flat, ((0, -flat.shape[0] % PACK_ALIGN), (0, 0)))


def _unpack_rows(packed, shapes):
    out, off = [], 0
    for s in shapes:
        rows = 1
        for d in s:
            rows *= d
        rows //= D
        out.append(packed[..., off:off + rows, :].reshape(packed.shape[:-2] + tuple(s)))
        off += rows
    return out


def _pack_small(vals):
    row = jnp.concatenate([vals[k].reshape(-1) for k in SMALL_ROW])
    row = jnp.pad(row, (0, D - row.shape[0]))[None]
    return jnp.concatenate([vals['hgrn_lb'], vals['norm_g'], vals['gd_conv_w'].reshape(16, D), row,
                            jnp.zeros((7, D), F32)], axis=0)


def _sw_cols(w, inverse=False):
    def split(a, heads):
        shp = (a.shape[0], 2, heads, 32) if inverse else (a.shape[0], heads, 2, 32)
        return a.reshape(shp).transpose(0, 2, 1, 3).reshape(a.shape[0], heads * 64)
    return jnp.concatenate([split(w[:, 0:1024], 16), split(w[:, 1024:1280], 4), w[:, 1280:]], axis=1)


def kernel(x, c, positions, hgrn_lb, ada_w, ada_b, norm_g, hg_in_w, hg_out_w, hg_onorm, sw_in_w, sw_out_w, sw_qnorm, sw_knorm, sw_sinks, gd_in_w, gd_out_w, gd_conv_w, gd_a_log, gd_dt_bias, gd_onorm, loss_target, m_hgrn_lb, m_ada_w, m_ada_b, m_norm_g, m_hg_in_w, m_hg_out_w, m_hg_onorm, m_sw_in_w, m_sw_out_w, m_sw_qnorm, m_sw_knorm, m_sw_sinks, m_gd_in_w, m_gd_out_w, m_gd_conv_w, m_gd_a_log, m_gd_dt_bias, m_gd_onorm, v_hgrn_lb, v_ada_w, v_ada_b, v_norm_g, v_hg_in_w, v_hg_out_w, v_hg_onorm, v_sw_in_w, v_sw_out_w, v_sw_qnorm, v_sw_knorm, v_sw_sinks, v_gd_in_w, v_gd_out_w, v_gd_conv_w, v_gd_a_log, v_gd_dt_bias, v_gd_onorm):
    w_in = dict(hgrn_lb=hgrn_lb, ada_w=ada_w, ada_b=ada_b, norm_g=norm_g, hg_in_w=hg_in_w, hg_out_w=hg_out_w,
                hg_onorm=hg_onorm, sw_in_w=sw_in_w, sw_out_w=sw_out_w, sw_qnorm=sw_qnorm, sw_knorm=sw_knorm,
                sw_sinks=sw_sinks, gd_in_w=gd_in_w, gd_out_w=gd_out_w, gd_conv_w=gd_conv_w, gd_a_log=gd_a_log,
                gd_dt_bias=gd_dt_bias, gd_onorm=gd_onorm)
    m_in = dict(zip(WEIGHTS, (m_hgrn_lb, m_ada_w, m_ada_b, m_norm_g, m_hg_in_w, m_hg_out_w, m_hg_onorm, m_sw_in_w,
                              m_sw_out_w, m_sw_qnorm, m_sw_knorm, m_sw_sinks, m_gd_in_w, m_gd_out_w, m_gd_conv_w,
                              m_gd_a_log, m_gd_dt_bias, m_gd_onorm)))
    v_in = dict(zip(WEIGHTS, (v_hgrn_lb, v_ada_w, v_ada_b, v_norm_g, v_hg_in_w, v_hg_out_w, v_hg_onorm, v_sw_in_w,
                              v_sw_out_w, v_sw_qnorm, v_sw_knorm, v_sw_sinks, v_gd_in_w, v_gd_out_w, v_gd_conv_w,
                              v_gd_a_log, v_gd_dt_bias, v_gd_onorm)))
    ax, ay, ac = _place()
    chip = 2 * ax + ay
    bidx = 4 * ax + 2 * ay + ac
    t = x.shape[1]
    x0, target = x[0], loss_target[0]

    c_all = _allgather8(jnp.pad(c, ((0, 7), (0, 0)))).reshape(8, 8, D)[:, 0, :]
    ada_b_cols = lax.dynamic_slice(ada_b, (0, chip * ADA_S), (4, ADA_S)).reshape(4, 1, ADA_S)
    mod_sh = _ada_fwd(c_all, ada_w, ada_b_cols)
    mod_g = _allgather8(mod_sh.reshape(32, ADA_S)).reshape(4, 2, 4, 8, ADA_S)[:, 0]
    mod = lax.dynamic_index_in_dim(mod_g, bidx, axis=2, keepdims=False).transpose(1, 0, 2).reshape(4, 3 * D)
    shift = [mod[l:l + 1, 0:D] for l in range(4)]
    scale = [mod[l:l + 1, D:2 * D] for l in range(4)]
    gate = [mod[l:l + 1, 2 * D:3 * D] for l in range(4)]

    h8 = jnp.concatenate([hgrn_lb, jnp.full((4, D), -1e30, F32)], axis=0)
    lb_all = _lb_fwd(h8)
    freq = ROPE_THETA ** (-jnp.arange(0, 64, 2, dtype=F32) / 64)
    cs = _rope_table(positions.reshape(t, 1), jnp.tile(freq, 4)[None])

    seg_shapes = lambda seg: [w_in[k].shape[1:] for k, _ in seg]
    pack_seg = lambda src, seg: _pack_rows([src[k][i] for k, i in seg])
    cols_full = lambda a: a.transpose(1, 0, 2).reshape(a.shape[1], 4 * a.shape[2])
    hg_in0_k, hg_out0_k = _unpack_rows(_chip_allgather(pack_seg(w_in, SEG_FIRST).astype(BF16)), seg_shapes(SEG_FIRST))
    win, wout = [cols_full(hg_in0_k)], [hg_out0_k.reshape(D, D)]
    rest_shard = pack_seg(w_in, SEG_REST).astype(BF16)
    tn_in = [1024, 1280, 896, 1024]

    gq = jnp.stack([jnp.tile(sw_qnorm[0, :32], 16), jnp.tile(sw_qnorm[0, 32:], 16)])
    gk = jnp.stack([jnp.tile(sw_knorm[0, :32], 4), jnp.tile(sw_knorm[0, 32:], 4)])
    pad128 = lambda a: jnp.pad(a, ((0, 0), (0, HD - a.shape[1])))
    sinks, alog, dtb = pad128(sw_sinks), pad128(gd_a_log), pad128(gd_dt_bias)
    cw8 = jnp.pad(_chip_allgather(gd_conv_w[0]).transpose(1, 0, 2).reshape(4, GD_QKV), ((0, 4), (0, 0)))
    lbs = {0: lb_all[0:1], 3: lb_all[3:4]}

    xs, us, hs, ps, stss = [x0], [], [], [], []
    for l in range(4):
        u, h = _ln_mm(xs[l], norm_g[l:l + 1], scale[l], shift[l], win[l], tn_in[l])
        if l == 0:
            p, sts, rest_k = _hg_fwd(u, lbs[l], hg_onorm[0:1], gather=rest_shard)
            hg_in1_k, hg_out1_k, sw_in_k, sw_out_k, gd_in_k, gd_out_k = _unpack_rows(rest_k, seg_shapes(SEG_REST))
            win += [_sw_cols(cols_full(sw_in_k)), jnp.pad(cols_full(gd_in_k), ((0, 0), (0, GD_N - 6176))),
                    cols_full(hg_in1_k)]
            wout += [sw_out_k.reshape(D, D), gd_out_k.reshape(GD_VW, D), hg_out1_k.reshape(D, D)]
        elif l % 3 == 0:
            p, sts = _hg_fwd(u, lbs[l], hg_onorm[l // 3:l // 3 + 1])
        elif l % 3 == 1:
            p, sts = _sw_fwd(u, cs, gq, gk, sinks), None
        else:
            p, *sts = _gd_fwd(u, cw8, alog, dtb, gd_onorm)
        xs.append(_mm_res(p, wout[l], xs[l], gate[l]))
        us.append(u), hs.append(h), ps.append(p), stss.append(sts)
    lpart, dx = _loss_grad(xs[4], target)
    loss = lax.psum(lpart[0, 0], ("x", "y", "c"))

    by_chip = lambda g, cols: g.reshape(g.shape[0], 4, cols).transpose(1, 0, 2)
    g_small = {}
    d_in, d_out, dmod, dnorm_g, dlb8, dgo_hg = [None] * 4, [None] * 4, [None] * 4, [None] * 4, jnp.zeros((8, D), F32), {}
    for l in (3, 2, 1, 0):
        dp = _mm_scaled(dx, gate[l], wout[l].T, 1024)
        d_out[l], dgate = _outgrad(_mm_tn_acc(ps[l], dx, D if ps[l].shape[1] == D else 512), wout[l], gate[l])
        if l == 0:
            rest_parts = {('hg_in_w', 1): by_chip(d_in[3], D), ('hg_out_w', 1): d_out[3].reshape(4, D // 4, D),
                          ('sw_in_w', 0): by_chip(_sw_cols(d_in[1], inverse=True), SW_N // 4),
                          ('sw_out_w', 0): d_out[1].reshape(4, D // 4, D),
                          ('gd_in_w', 0): by_chip(d_in[2][:, :6176], 1544),
                          ('gd_out_w', 0): d_out[2].reshape(4, GD_VW // 4, D)}
            rest_packed = jnp.stack([_pack_rows([rest_parts[s][j] for s in SEG_REST]) for j in range(4)])
            du, dlb, dgo_hg[0], rest_recv = _hg_bwd(us[l], stss[l], dp, lbs[l], hg_onorm[0:1],
                                                    scatter=rest_packed.astype(BF16))
            dlb8 = lax.dynamic_update_slice(dlb8, dlb, (l, 0))
        elif l % 3 == 0:
            du, dlb, dgo_hg[l // 3] = _hg_bwd(us[l], stss[l], dp, lbs[l], hg_onorm[l // 3:l // 3 + 1])
            dlb8 = lax.dynamic_update_slice(dlb8, dlb, (l, 0))
        elif l % 3 == 1:
            du, dgq, dgk, dsk = _sw_bwd(us[l], cs, dp, gq, gk, sinks)
            g_small['sw_qnorm'] = jnp.concatenate([dgq[0].reshape(16, 32).sum(0), dgq[1].reshape(16, 32).sum(0)])
            g_small['sw_knorm'] = jnp.concatenate([dgk[0].reshape(4, 32).sum(0), dgk[1].reshape(4, 32).sum(0)])
            g_small['sw_sinks'] = dsk[0, :16]
        else:
            du, dcw, dalog, ddtb, g_small['gd_onorm'] = _gd_bwd(us[l], *stss[l], dp, cw8, alog, dtb, gd_onorm)
            g_small['gd_conv_w'], g_small['gd_a_log'], g_small['gd_dt_bias'] = dcw[:4], dalog[0, :16], ddtb[0, :16]
        d_in[l] = _mm_tn_acc(hs[l], du, 896 if l == 2 else 512)
        dx, dvec = _inproj_bwd(du, win[l].T, xs[l], dx, norm_g[l:l + 1], scale[l], shift[l])
        dnorm_g[l] = dvec[0:1]
        dmod[l] = jnp.concatenate([dvec[2:3], dvec[1:2], dgate[0:1]], axis=1)
    grad_x = dx[None]

    g_small['hgrn_lb'] = _lb_bwd(h8, dlb8)[0:4]
    g_small['norm_g'] = jnp.concatenate(dnorm_g, axis=0)
    g_small['hg_onorm'] = jnp.concatenate([dgo_hg[0], dgo_hg[1]], axis=0)
    gs_all = _allgather8(_pack_small(g_small))
    gs = _sum_rows([gs_all[32 * d:32 * (d + 1)] for d in range(8)])

    def small_view(packed, k):
        if k == 'hgrn_lb':
            return packed[0:4]
        if k == 'norm_g':
            return packed[4:8]
        off, size = SMALL_ROW[k]
        return packed[24, off:off + size].reshape(w_in[k].shape)

    conv_sl = lambda full: lax.dynamic_slice(full.reshape(4, GD_QKV), (0, chip * D), (4, D))
    out = {}

    def put(k, res, shape):
        for name, r in zip(('grad_', 'delta_', 'new_m_', 'new_v_'), res):
            out[name + k] = r.reshape(shape)

    zero_conv = dict(gd_conv_w=jnp.zeros((4, GD_QKV), F32))
    small_names = ['hgrn_lb', 'norm_g'] + list(SMALL_ROW)
    res = _adamw(_pack_small({**{k: w_in[k] for k in small_names}, **zero_conv}), (gs,),
                 _pack_small({**{k: m_in[k] for k in small_names}, **zero_conv}),
                 _pack_small({**{k: v_in[k] for k in small_names}, **zero_conv}))
    for k in small_names:
        put(k, [small_view(r, k) for r in res], w_in[k].shape)
    put('gd_conv_w', _adamw(gd_conv_w[0], (conv_sl(gs[8:24]),), m_in['gd_conv_w'][0], v_in['gd_conv_w'][0]),
        gd_conv_w.shape)

    dm = _allgather8(jnp.pad(jnp.concatenate(dmod, axis=0), ((0, 4), (0, 0)))).reshape(8, 8, 3 * D)[:, :4]
    dm = dm.transpose(1, 0, 2)
    g_ada_w, g_ada_b = _ada_bwd(c_all, lax.dynamic_slice(dm, (0, 0, chip * ADA_S), (4, 8, ADA_S)), dm)
    put('ada_w', _adamw(ada_w.reshape(4 * D, ADA_S), (g_ada_w.reshape(4 * D, ADA_S),),
                        m_in['ada_w'].reshape(4 * D, ADA_S), v_in['ada_w'].reshape(4 * D, ADA_S)), ada_w.shape)
    put('ada_b', _adamw(ada_b, (g_ada_b.reshape(4, 3 * D),), m_in['ada_b'], v_in['ada_b']), ada_b.shape)

    first_parts = {('hg_in_w', 0): by_chip(d_in[0], D), ('hg_out_w', 0): d_out[0].reshape(4, D // 4, D)}
    first_packed = jnp.stack([_pack_rows([first_parts[s][j] for s in SEG_FIRST]) for j in range(4)])
    first_recv = _chip_scatter(first_packed.astype(BF16))
    own = lambda packed: lax.dynamic_index_in_dim(packed, chip, axis=0, keepdims=False)
    half = jnp.concatenate([_sum_rows([own(first_packed), first_recv[0], first_recv[1], first_recv[2]]),
                            _sum_rows([own(rest_packed), rest_recv[0], rest_recv[1], rest_recv[2]])], axis=0)
    other = _sibling_swap(half)
    pack_all = lambda src: jnp.concatenate([pack_seg(src, SEG_FIRST), pack_seg(src, SEG_REST)], axis=0)
    res = _adamw(pack_all(w_in), (half, other), pack_all(m_in), pack_all(v_in))
    n_first = first_packed.shape[1]
    for name, r in zip(('grad_', 'delta_', 'new_m_', 'new_v_'), res):
        pieces = dict(zip(SEG_FIRST, _unpack_rows(r[:n_first], seg_shapes(SEG_FIRST))))
        pieces.update(zip(SEG_REST, _unpack_rows(r[n_first:], seg_shapes(SEG_REST))))
        for k in BIG:
            out[name + k] = jnp.stack([pieces[(k, i)] for i in range(w_in[k].shape[0])])

    return (loss, grad_x, *[out[p + k] for p in ('grad_', 'delta_', 'new_m_', 'new_v_') for k in WEIGHTS])
```
